```python
import math
import jax
import jax.numpy as jnp
from jax import lax
import numpy as np

D_MODEL = 2048
BATCH = 8
SEQ = 2048
DEPTH = 2

RET_HEADS = 4
RET_QK_DIM = 256
RET_V_DIM = D_MODEL // RET_HEADS
RET_QK_WIDTH = RET_HEADS * RET_QK_DIM
CHUNK = 128
ROPE_BASE = 10000.0
SSM_GROUP = 16
SSM_GROUPS = D_MODEL // SSM_GROUP
SSM_STATE = 64
DT_MIN = 0.001
DT_MAX = 0.1
D_FF = ((8 * D_MODEL // 3 + 255) // 256) * 256
IN_WIDTH = 2 * RET_QK_WIDTH + 5 * D_MODEL
EPS = 1e-6

kernel_name = "hybrid_retention_s5_gated_encoder"


def rms_norm(x, g):
    xf = x.astype(jnp.float32)
    y = xf * lax.rsqrt(jnp.mean(xf * xf, axis=-1, keepdims=True) + EPS)
    return (y * g.astype(jnp.float32)).astype(x.dtype)


def rotary(x):
    L = x.shape[1]
    half = x.shape[-1] // 2
    inv = 1.0 / (ROPE_BASE ** (jnp.arange(half, dtype=jnp.float32) / half))
    ang = jnp.arange(L, dtype=jnp.float32)[:, None] * inv[None, :]
    cos = jnp.cos(ang)[None, :, None, :]
    sin = jnp.sin(ang)[None, :, None, :]
    xf = x.astype(jnp.float32)
    x1, x2 = xf[..., :half], xf[..., half:]
    return jnp.concatenate([x1 * cos - x2 * sin, x1 * sin + x2 * cos], axis=-1)


def retention(q, k, v, log_gamma):
    f32 = jnp.float32
    b, l, h, dk = q.shape
    dv = v.shape[-1]
    nc = l // CHUNK
    q = q.reshape(b, nc, CHUNK, h, dk)
    k = (k * dk ** -0.5).reshape(b, nc, CHUNK, h, dk)
    v = v.astype(f32).reshape(b, nc, CHUNK, h, dv)
    lg = log_gamma.astype(f32)
    lg_f, lg_b = lg[0], lg[1]
    t = jnp.arange(CHUNK, dtype=f32)
    diff = t[:, None] - t[None, :]
    dmat = jnp.exp(jnp.where(diff >= 0, lg_f[:, None, None] * diff, -lg_b[:, None, None] * diff))
    scores = jnp.einsum('bnthd,bnshd->bnhts', q, k) * dmat
    y = jnp.einsum('bnhts,bnshe->bnthe', scores, v)
    kf = k * jnp.exp(lg_f[None, :] * (CHUNK - 1.0 - t)[:, None])[:, :, None]
    kb = k * jnp.exp(lg_b[None, :] * t[:, None])[:, :, None]
    kv_f = jnp.einsum('bnshd,bnshe->nbhde', kf, v)
    kv_b = jnp.einsum('bnshd,bnshe->nbhde', kb, v)
    decay_f = jnp.exp(lg_f * CHUNK)[None, :, None, None]
    decay_b = jnp.exp(lg_b * CHUNK)[None, :, None, None]

    def step_f(s, kv):
        return decay_f * s + kv, s

    def step_b(s, kv):
        return decay_b * s + kv, s

    zero = jnp.zeros((b, h, dk, dv), f32)
    _, s_f = lax.scan(step_f, zero, kv_f)
    _, s_b = lax.scan(step_b, zero, kv_b, reverse=True)
    qf = q * jnp.exp(lg_f[None, :] * (t[:, None] + 1.0))[:, :, None]
    qb = q * jnp.exp(lg_b[None, :] * (CHUNK - t)[:, None])[:, :, None]
    y = y + jnp.einsum('bnthd,nbhde->bnthe', qf, s_f) + jnp.einsum('bnthd,nbhde->bnthe', qb, s_b)
    return y.reshape(b, l, h, dv)


def _linear_recurrence(e1, e2):
    a1, b1 = e1
    a2, b2 = e2
    return a1 * a2, a2 * b1 + b2


def s5_direction(u, a_re, a_im, log_dt, b_re, b_im, c_re, c_im, reverse):
    f32 = jnp.float32
    lam = lax.complex(a_re.astype(f32), a_im.astype(f32))
    dt = jnp.exp(log_dt.astype(f32))[:, None]
    lam_bar = jnp.exp(lam * dt)
    b_c = lax.complex(b_re.astype(f32), b_im.astype(f32))
    b_bar = ((lam_bar - 1.0) / lam)[:, :, None] * b_c
    bu = lax.complex(jnp.einsum('blgh,gph->blgp', u, jnp.real(b_bar)),
                     jnp.einsum('blgh,gph->blgp', u, jnp.imag(b_bar)))
    a = jnp.broadcast_to(lam_bar, bu.shape)
    _, xs = lax.associative_scan(_linear_recurrence, (a, bu), axis=1, reverse=reverse)
    return (jnp.einsum('blgp,ghp->blgh', jnp.real(xs), c_re.astype(f32))
            - jnp.einsum('blgp,ghp->blgh', jnp.imag(xs), c_im.astype(f32)))


def hybrid_mixer(h, w_in, log_gamma, a_re, a_im, log_dt, b_re, b_im, c_re, c_im,
                 d_skip, w_glu, b_glu, w_out):
    bsz, l, _ = h.shape
    dt_in = h.dtype
    proj = h @ w_in
    cuts = [RET_QK_WIDTH, 2 * RET_QK_WIDTH, 2 * RET_QK_WIDTH + D_MODEL,
            2 * RET_QK_WIDTH + 2 * D_MODEL, 2 * RET_QK_WIDTH + 3 * D_MODEL,
            2 * RET_QK_WIDTH + 4 * D_MODEL]
    q, k, v, g, u, gate_r, gate_s = jnp.split(proj, cuts, axis=-1)

    q = rotary(q.reshape(bsz, l, RET_HEADS, RET_QK_DIM))
    k = rotary(k.reshape(bsz, l, RET_HEADS, RET_QK_DIM))
    v = v.reshape(bsz, l, RET_HEADS, RET_V_DIM)
    y = retention(q, k, v, log_gamma)
    y = y * lax.rsqrt(jnp.mean(y * y, axis=-1, keepdims=True) + EPS)
    ret_out = jax.nn.silu(g.astype(jnp.float32)) * y.reshape(bsz, l, D_MODEL)

    uf = u.astype(jnp.float32)
    ug = uf.reshape(bsz, l, SSM_GROUPS, SSM_GROUP)
    ys = (s5_direction(ug, a_re[0], a_im[0], log_dt[0], b_re[0], b_im[0], c_re[0], c_im[0], False)
          + s5_direction(ug, a_re[1], a_im[1], log_dt[1], b_re[1], b_im[1], c_re[1], c_im[1], True))
    ys = ys.reshape(bsz, l, D_MODEL) + d_skip.astype(jnp.float32) * uf
    ys = jax.nn.gelu(ys).astype(dt_in)
    ssm_out = ys * jax.nn.sigmoid(ys @ w_glu + b_glu)

    merged = (jax.nn.sigmoid(gate_r) * ret_out.astype(dt_in)
              + jax.nn.sigmoid(gate_s) * ssm_out)
    return merged @ w_out


def swiglu(h, w_gate, w_up, w_down):
    return (jax.nn.silu(h @ w_gate) * (h @ w_up)) @ w_down


def _fwd_setup_inputs(seed: int = 0) -> dict:
    key = jax.random.key(seed)
    ks = jax.random.split(key, 24)
    f32 = jnp.float32
    G, P, Hg = SSM_GROUPS, SSM_STATE, SSM_GROUP
    nrm = lambda k, shape, scale: jax.random.normal(k, shape, f32) * scale
    x = nrm(ks[0], (BATCH, SEQ, D_MODEL), 1.0)
    ln_mix_g = 1.0 + nrm(ks[1], (DEPTH, D_MODEL), 0.02)
    w_in = nrm(ks[2], (DEPTH, D_MODEL, IN_WIDTH), D_MODEL ** -0.5)
    base_lg = jnp.log(1.0 - 2.0 ** (-5.0 - jnp.arange(RET_HEADS, dtype=f32)))
    ret_log_gamma = base_lg[None, None, :] * (1.0 + nrm(ks[3], (DEPTH, 2, RET_HEADS), 0.05))
    n = jnp.arange(P, dtype=f32)
    ssm_a_re = -0.5 + nrm(ks[4], (DEPTH, 2, G, P), 0.01)
    ssm_a_im = math.pi * n[None, None, None, :] + nrm(ks[5], (DEPTH, 2, G, P), 0.01)
    ssm_log_dt = jax.random.uniform(ks[6], (DEPTH, 2, G), f32,
                                    math.log(DT_MIN), math.log(DT_MAX))
    ssm_b_re = nrm(ks[7], (DEPTH, 2, G, P, Hg), (2.0 * Hg) ** -0.5)
    ssm_b_im = nrm(ks[8], (DEPTH, 2, G, P, Hg), (2.0 * Hg) ** -0.5)
    ssm_c_re = nrm(ks[9], (DEPTH, 2, G, Hg, P), (2.0 * P) ** -0.5)
    ssm_c_im = nrm(ks[10], (DEPTH, 2, G, Hg, P), (2.0 * P) ** -0.5)
    ssm_d = nrm(ks[11], (DEPTH, D_MODEL), 1.0)
    w_glu = nrm(ks[12], (DEPTH, D_MODEL, D_MODEL), D_MODEL ** -0.5)
    b_glu = nrm(ks[13], (DEPTH, D_MODEL), 0.01)
    w_out = nrm(ks[14], (DEPTH, D_MODEL, D_MODEL), D_MODEL ** -0.5)
    ln_ffn_g = 1.0 + nrm(ks[15], (DEPTH, D_MODEL), 0.02)
    w_ffn_gate = nrm(ks[16], (DEPTH, D_MODEL, D_FF), D_MODEL ** -0.5)
    w_ffn_up = nrm(ks[17], (DEPTH, D_MODEL, D_FF), D_MODEL ** -0.5)
    w_ffn_down = nrm(ks[18], (DEPTH, D_FF, D_MODEL), D_FF ** -0.5)
    ln_final_g = 1.0 + nrm(ks[19], (D_MODEL,), 0.02)
    return {"x": x, "ln_mix_g": ln_mix_g, "w_in": w_in, "ret_log_gamma": ret_log_gamma,
            "ssm_a_re": ssm_a_re, "ssm_a_im": ssm_a_im, "ssm_log_dt": ssm_log_dt,
            "ssm_b_re": ssm_b_re, "ssm_b_im": ssm_b_im, "ssm_c_re": ssm_c_re,
            "ssm_c_im": ssm_c_im, "ssm_d": ssm_d, "w_glu": w_glu, "b_glu": b_glu,
            "w_out": w_out, "ln_ffn_g": ln_ffn_g, "w_ffn_gate": w_ffn_gate,
            "w_ffn_up": w_ffn_up, "w_ffn_down": w_ffn_down, "ln_final_g": ln_final_g}


def _fwd_reference(x, ln_mix_g, w_in, ret_log_gamma, ssm_a_re, ssm_a_im, ssm_log_dt,
              ssm_b_re, ssm_b_im, ssm_c_re, ssm_c_im, ssm_d, w_glu, b_glu, w_out,
              ln_ffn_g, w_ffn_gate, w_ffn_up, w_ffn_down, ln_final_g):
    for i in range(DEPTH):
        h = rms_norm(x, ln_mix_g[i])
        x = x + hybrid_mixer(h, w_in[i], ret_log_gamma[i], ssm_a_re[i], ssm_a_im[i],
                             ssm_log_dt[i], ssm_b_re[i], ssm_b_im[i], ssm_c_re[i],
                             ssm_c_im[i], ssm_d[i], w_glu[i], b_glu[i], w_out[i])
        h = rms_norm(x, ln_ffn_g[i])
        x = x + swiglu(h, w_ffn_gate[i], w_ffn_up[i], w_ffn_down[i])
    return rms_norm(x, ln_final_g)


import jax as _jax
import jax.numpy as _jnp

TWIN_FORMAT = 'train_step'
FWD_PARAMS = ['x', 'ln_mix_g', 'w_in', 'ret_log_gamma', 'ssm_a_re', 'ssm_a_im', 'ssm_log_dt', 'ssm_b_re', 'ssm_b_im', 'ssm_c_re', 'ssm_c_im', 'ssm_d', 'w_glu', 'b_glu', 'w_out', 'ln_ffn_g', 'w_ffn_gate', 'w_ffn_up', 'w_ffn_down', 'ln_final_g']
TWIN_WEIGHTS = ['ln_mix_g', 'w_in', 'ret_log_gamma', 'ssm_a_re', 'ssm_a_im', 'ssm_log_dt', 'ssm_b_re', 'ssm_b_im', 'ssm_c_re', 'ssm_c_im', 'ssm_d', 'w_glu', 'b_glu', 'w_out', 'ln_ffn_g', 'w_ffn_gate', 'w_ffn_up', 'w_ffn_down', 'ln_final_g']
TWIN_DIFF_INPUT = 'x'
TWIN_INPUTS = ['x', 'ln_mix_g', 'w_in', 'ret_log_gamma', 'ssm_a_re', 'ssm_a_im', 'ssm_log_dt', 'ssm_b_re', 'ssm_b_im', 'ssm_c_re', 'ssm_c_im', 'ssm_d', 'w_glu', 'b_glu', 'w_out', 'ln_ffn_g', 'w_ffn_gate', 'w_ffn_up', 'w_ffn_down', 'ln_final_g', 'loss_target', 'm_ln_mix_g', 'm_w_in', 'm_ret_log_gamma', 'm_ssm_a_re', 'm_ssm_a_im', 'm_ssm_log_dt', 'm_ssm_b_re', 'm_ssm_b_im', 'm_ssm_c_re', 'm_ssm_c_im', 'm_ssm_d', 'm_w_glu', 'm_b_glu', 'm_w_out', 'm_ln_ffn_g', 'm_w_ffn_gate', 'm_w_ffn_up', 'm_w_ffn_down', 'm_ln_final_g', 'v_ln_mix_g', 'v_w_in', 'v_ret_log_gamma', 'v_ssm_a_re', 'v_ssm_a_im', 'v_ssm_log_dt', 'v_ssm_b_re', 'v_ssm_b_im', 'v_ssm_c_re', 'v_ssm_c_im', 'v_ssm_d', 'v_w_glu', 'v_b_glu', 'v_w_out', 'v_ln_ffn_g', 'v_w_ffn_gate', 'v_w_ffn_up', 'v_w_ffn_down', 'v_ln_final_g']
TWIN_OUTPUTS = ['loss', 'grad_x', 'grad_ln_mix_g', 'grad_w_in', 'grad_ret_log_gamma', 'grad_ssm_a_re', 'grad_ssm_a_im', 'grad_ssm_log_dt', 'grad_ssm_b_re', 'grad_ssm_b_im', 'grad_ssm_c_re', 'grad_ssm_c_im', 'grad_ssm_d', 'grad_w_glu', 'grad_b_glu', 'grad_w_out', 'grad_ln_ffn_g', 'grad_w_ffn_gate', 'grad_w_ffn_up', 'grad_w_ffn_down', 'grad_ln_final_g', 'delta_ln_mix_g', 'delta_w_in', 'delta_ret_log_gamma', 'delta_ssm_a_re', 'delta_ssm_a_im', 'delta_ssm_log_dt', 'delta_ssm_b_re', 'delta_ssm_b_im', 'delta_ssm_c_re', 'delta_ssm_c_im', 'delta_ssm_d', 'delta_w_glu', 'delta_b_glu', 'delta_w_out', 'delta_ln_ffn_g', 'delta_w_ffn_gate', 'delta_w_ffn_up', 'delta_w_ffn_down', 'delta_ln_final_g', 'new_m_ln_mix_g', 'new_m_w_in', 'new_m_ret_log_gamma', 'new_m_ssm_a_re', 'new_m_ssm_a_im', 'new_m_ssm_log_dt', 'new_m_ssm_b_re', 'new_m_ssm_b_im', 'new_m_ssm_c_re', 'new_m_ssm_c_im', 'new_m_ssm_d', 'new_m_w_glu', 'new_m_b_glu', 'new_m_w_out', 'new_m_ln_ffn_g', 'new_m_w_ffn_gate', 'new_m_w_ffn_up', 'new_m_w_ffn_down', 'new_m_ln_final_g', 'new_v_ln_mix_g', 'new_v_w_in', 'new_v_ret_log_gamma', 'new_v_ssm_a_re', 'new_v_ssm_a_im', 'new_v_ssm_log_dt', 'new_v_ssm_b_re', 'new_v_ssm_b_im', 'new_v_ssm_c_re', 'new_v_ssm_c_im', 'new_v_ssm_d', 'new_v_w_glu', 'new_v_b_glu', 'new_v_w_out', 'new_v_ln_ffn_g', 'new_v_w_ffn_gate', 'new_v_w_ffn_up', 'new_v_w_ffn_down', 'new_v_ln_final_g']
TWIN_LEAF_KINDS = {'loss': 'loss', 'grad_x': 'grad_x', 'grad_ln_mix_g': 'grad_w', 'grad_w_in': 'grad_w', 'grad_ret_log_gamma': 'grad_w', 'grad_ssm_a_re': 'grad_w', 'grad_ssm_a_im': 'grad_w', 'grad_ssm_log_dt': 'grad_w', 'grad_ssm_b_re': 'grad_w', 'grad_ssm_b_im': 'grad_w', 'grad_ssm_c_re': 'grad_w', 'grad_ssm_c_im': 'grad_w', 'grad_ssm_d': 'grad_w', 'grad_w_glu': 'grad_w', 'grad_b_glu': 'grad_w', 'grad_w_out': 'grad_w', 'grad_ln_ffn_g': 'grad_w', 'grad_w_ffn_gate': 'grad_w', 'grad_w_ffn_up': 'grad_w', 'grad_w_ffn_down': 'grad_w', 'grad_ln_final_g': 'grad_w', 'delta_ln_mix_g': 'delta_w', 'delta_w_in': 'delta_w', 'delta_ret_log_gamma': 'delta_w', 'delta_ssm_a_re': 'delta_w', 'delta_ssm_a_im': 'delta_w', 'delta_ssm_log_dt': 'delta_w', 'delta_ssm_b_re': 'delta_w', 'delta_ssm_b_im': 'delta_w', 'delta_ssm_c_re': 'delta_w', 'delta_ssm_c_im': 'delta_w', 'delta_ssm_d': 'delta_w', 'delta_w_glu': 'delta_w', 'delta_b_glu': 'delta_w', 'delta_w_out': 'delta_w', 'delta_ln_ffn_g': 'delta_w', 'delta_w_ffn_gate': 'delta_w', 'delta_w_ffn_up': 'delta_w', 'delta_w_ffn_down': 'delta_w', 'delta_ln_final_g': 'delta_w', 'new_m_ln_mix_g': 'new_m', 'new_m_w_in': 'new_m', 'new_m_ret_log_gamma': 'new_m', 'new_m_ssm_a_re': 'new_m', 'new_m_ssm_a_im': 'new_m', 'new_m_ssm_log_dt': 'new_m', 'new_m_ssm_b_re': 'new_m', 'new_m_ssm_b_im': 'new_m', 'new_m_ssm_c_re': 'new_m', 'new_m_ssm_c_im': 'new_m', 'new_m_ssm_d': 'new_m', 'new_m_w_glu': 'new_m', 'new_m_b_glu': 'new_m', 'new_m_w_out': 'new_m', 'new_m_ln_ffn_g': 'new_m', 'new_m_w_ffn_gate': 'new_m', 'new_m_w_ffn_up': 'new_m', 'new_m_w_ffn_down': 'new_m', 'new_m_ln_final_g': 'new_m', 'new_v_ln_mix_g': 'new_v', 'new_v_w_in': 'new_v', 'new_v_ret_log_gamma': 'new_v', 'new_v_ssm_a_re': 'new_v', 'new_v_ssm_a_im': 'new_v', 'new_v_ssm_log_dt': 'new_v', 'new_v_ssm_b_re': 'new_v', 'new_v_ssm_b_im': 'new_v', 'new_v_ssm_c_re': 'new_v', 'new_v_ssm_c_im': 'new_v', 'new_v_ssm_d': 'new_v', 'new_v_w_glu': 'new_v', 'new_v_b_glu': 'new_v', 'new_v_w_out': 'new_v', 'new_v_ln_ffn_g': 'new_v', 'new_v_w_ffn_gate': 'new_v', 'new_v_w_ffn_up': 'new_v', 'new_v_w_ffn_down': 'new_v', 'new_v_ln_final_g': 'new_v'}


def _forward(args):
    return _fwd_reference(*[args[k] for k in FWD_PARAMS])


def _output_shape():
    out = _jax.eval_shape(lambda: _forward(_fwd_setup_inputs(0)))
    return out.shape, out.dtype

N_MICROBATCH = 1
ADAM_LR = 0.001
ADAM_B1 = 0.9
ADAM_B2 = 0.999
ADAM_EPS = 1e-08
ADAM_WD = 0.01
ADAM_STEP = 10
PER_EXAMPLE_BATCH_AXIS = {'x': 0, 'loss_target': 0}
SHARED_INPUTS = []
_WEIGHT_DTYPES = {'ln_mix_g': _jnp.float32, 'w_in': _jnp.float32, 'ret_log_gamma': _jnp.float32, 'ssm_a_re': _jnp.float32, 'ssm_a_im': _jnp.float32, 'ssm_log_dt': _jnp.float32, 'ssm_b_re': _jnp.float32, 'ssm_b_im': _jnp.float32, 'ssm_c_re': _jnp.float32, 'ssm_c_im': _jnp.float32, 'ssm_d': _jnp.float32, 'w_glu': _jnp.float32, 'b_glu': _jnp.float32, 'w_out': _jnp.float32, 'ln_ffn_g': _jnp.float32, 'w_ffn_gate': _jnp.float32, 'w_ffn_up': _jnp.float32, 'w_ffn_down': _jnp.float32, 'ln_final_g': _jnp.float32}
MOMENT_SCALE = {'ln_mix_g': 4.949191e-02, 'w_in': 1.921176e-02, 'ret_log_gamma': 8.033801e+01, 'ssm_a_re': 6.537575e-04, 'ssm_a_im': 6.715218e-04, 'ssm_log_dt': 4.258251e-01, 'ssm_b_re': 4.423048e-04, 'ssm_b_im': 4.441673e-04, 'ssm_c_re': 8.780724e-04, 'ssm_c_im': 8.839912e-04, 'ssm_d': 1.379628e-02, 'w_glu': 3.703967e-03, 'b_glu': 5.544359e-03, 'w_out': 2.517010e-02, 'ln_ffn_g': 4.582964e-02, 'w_ffn_gate': 1.988261e-02, 'w_ffn_up': 1.925278e-02, 'w_ffn_down': 3.194252e-02, 'ln_final_g': 8.000123e+00}


def _to_microbatches(a, axis):
    t = _jnp.moveaxis(a, axis, 0)
    t = t.reshape((N_MICROBATCH, t.shape[0] // N_MICROBATCH) + t.shape[1:])
    return _jnp.moveaxis(t, 1, axis + 1)


def setup_inputs(seed: int = 0) -> dict:
    inp = _fwd_setup_inputs(seed)
    key = _jax.random.fold_in(_jax.random.key(seed), 7919)
    shape, _ = _output_shape()
    out = dict(inp)
    out["loss_target"] = _jax.random.normal(_jax.random.fold_in(key, 0), shape, _jnp.float32)
    for i, name in enumerate(TWIN_WEIGHTS):
        w = inp[name].astype(_jnp.float32)
        if MOMENT_SCALE is None:
            s = _jnp.sqrt(_jnp.mean(_jnp.square(w)) + 1e-30)
        else:
            s = MOMENT_SCALE[name]
        km, kv = _jax.random.split(_jax.random.fold_in(key, i + 1))
        out[name] = w
        out["m_" + name] = s * _jax.random.normal(km, w.shape, _jnp.float32)
        out["v_" + name] = (s * s) * _jax.random.uniform(kv, w.shape, _jnp.float32, 0.5, 1.5)
    if N_MICROBATCH > 1:
        for name, axis in PER_EXAMPLE_BATCH_AXIS.items():
            out[name] = _to_microbatches(out[name], axis)
    return {'x': out['x'], 'ln_mix_g': out['ln_mix_g'], 'w_in': out['w_in'], 'ret_log_gamma': out['ret_log_gamma'], 'ssm_a_re': out['ssm_a_re'], 'ssm_a_im': out['ssm_a_im'], 'ssm_log_dt': out['ssm_log_dt'], 'ssm_b_re': out['ssm_b_re'], 'ssm_b_im': out['ssm_b_im'], 'ssm_c_re': out['ssm_c_re'], 'ssm_c_im': out['ssm_c_im'], 'ssm_d': out['ssm_d'], 'w_glu': out['w_glu'], 'b_glu': out['b_glu'], 'w_out': out['w_out'], 'ln_ffn_g': out['ln_ffn_g'], 'w_ffn_gate': out['w_ffn_gate'], 'w_ffn_up': out['w_ffn_up'], 'w_ffn_down': out['w_ffn_down'], 'ln_final_g': out['ln_final_g'], 'loss_target': out['loss_target'], 'm_ln_mix_g': out['m_ln_mix_g'], 'm_w_in': out['m_w_in'], 'm_ret_log_gamma': out['m_ret_log_gamma'], 'm_ssm_a_re': out['m_ssm_a_re'], 'm_ssm_a_im': out['m_ssm_a_im'], 'm_ssm_log_dt': out['m_ssm_log_dt'], 'm_ssm_b_re': out['m_ssm_b_re'], 'm_ssm_b_im': out['m_ssm_b_im'], 'm_ssm_c_re': out['m_ssm_c_re'], 'm_ssm_c_im': out['m_ssm_c_im'], 'm_ssm_d': out['m_ssm_d'], 'm_w_glu': out['m_w_glu'], 'm_b_glu': out['m_b_glu'], 'm_w_out': out['m_w_out'], 'm_ln_ffn_g': out['m_ln_ffn_g'], 'm_w_ffn_gate': out['m_w_ffn_gate'], 'm_w_ffn_up': out['m_w_ffn_up'], 'm_w_ffn_down': out['m_w_ffn_down'], 'm_ln_final_g': out['m_ln_final_g'], 'v_ln_mix_g': out['v_ln_mix_g'], 'v_w_in': out['v_w_in'], 'v_ret_log_gamma': out['v_ret_log_gamma'], 'v_ssm_a_re': out['v_ssm_a_re'], 'v_ssm_a_im': out['v_ssm_a_im'], 'v_ssm_log_dt': out['v_ssm_log_dt'], 'v_ssm_b_re': out['v_ssm_b_re'], 'v_ssm_b_im': out['v_ssm_b_im'], 'v_ssm_c_re': out['v_ssm_c_re'], 'v_ssm_c_im': out['v_ssm_c_im'], 'v_ssm_d': out['v_ssm_d'], 'v_w_glu': out['v_w_glu'], 'v_b_glu': out['v_b_glu'], 'v_w_out': out['v_w_out'], 'v_ln_ffn_g': out['v_ln_ffn_g'], 'v_w_ffn_gate': out['v_w_ffn_gate'], 'v_w_ffn_up': out['v_w_ffn_up'], 'v_w_ffn_down': out['v_w_ffn_down'], 'v_ln_final_g': out['v_ln_final_g']}


def _loss(weights, diff, rest, loss_target):
    with _jax.named_scope("forward"):
        args = {**rest, TWIN_DIFF_INPUT: diff, **{k: w.astype(_WEIGHT_DTYPES[k]) for k, w in weights.items()}}
        y = _forward(args)
    with _jax.named_scope("loss_head"):
        err = _jnp.square(y.astype(_jnp.float32) - loss_target)
        return 0.5 * _jnp.sum(_jnp.mean(err, axis=-1)) if err.ndim else 0.5 * err


def _adamw(w, g, m, v):
    m = ADAM_B1 * m + (1.0 - ADAM_B1) * g
    v = ADAM_B2 * v + (1.0 - ADAM_B2) * _jnp.square(g)
    m_hat = m / (1.0 - ADAM_B1 ** ADAM_STEP)
    v_hat = v / (1.0 - ADAM_B2 ** ADAM_STEP)
    delta = -ADAM_LR * (m_hat / (_jnp.sqrt(v_hat) + ADAM_EPS) + ADAM_WD * w)
    return delta, m, v


def reference(x, ln_mix_g, w_in, ret_log_gamma, ssm_a_re, ssm_a_im, ssm_log_dt, ssm_b_re, ssm_b_im, ssm_c_re, ssm_c_im, ssm_d, w_glu, b_glu, w_out, ln_ffn_g, w_ffn_gate, w_ffn_up, w_ffn_down, ln_final_g, loss_target, m_ln_mix_g, m_w_in, m_ret_log_gamma, m_ssm_a_re, m_ssm_a_im, m_ssm_log_dt, m_ssm_b_re, m_ssm_b_im, m_ssm_c_re, m_ssm_c_im, m_ssm_d, m_w_glu, m_b_glu, m_w_out, m_ln_ffn_g, m_w_ffn_gate, m_w_ffn_up, m_w_ffn_down, m_ln_final_g, v_ln_mix_g, v_w_in, v_ret_log_gamma, v_ssm_a_re, v_ssm_a_im, v_ssm_log_dt, v_ssm_b_re, v_ssm_b_im, v_ssm_c_re, v_ssm_c_im, v_ssm_d, v_w_glu, v_b_glu, v_w_out, v_ln_ffn_g, v_w_ffn_gate, v_w_ffn_up, v_w_ffn_down, v_ln_final_g):
    given = dict(x=x, ln_mix_g=ln_mix_g, w_in=w_in, ret_log_gamma=ret_log_gamma, ssm_a_re=ssm_a_re, ssm_a_im=ssm_a_im, ssm_log_dt=ssm_log_dt, ssm_b_re=ssm_b_re, ssm_b_im=ssm_b_im, ssm_c_re=ssm_c_re, ssm_c_im=ssm_c_im, ssm_d=ssm_d, w_glu=w_glu, b_glu=b_glu, w_out=w_out, ln_ffn_g=ln_ffn_g, w_ffn_gate=w_ffn_gate, w_ffn_up=w_ffn_up, w_ffn_down=w_ffn_down, ln_final_g=ln_final_g, loss_target=loss_target, m_ln_mix_g=m_ln_mix_g, m_w_in=m_w_in, m_ret_log_gamma=m_ret_log_gamma, m_ssm_a_re=m_ssm_a_re, m_ssm_a_im=m_ssm_a_im, m_ssm_log_dt=m_ssm_log_dt, m_ssm_b_re=m_ssm_b_re, m_ssm_b_im=m_ssm_b_im, m_ssm_c_re=m_ssm_c_re, m_ssm_c_im=m_ssm_c_im, m_ssm_d=m_ssm_d, m_w_glu=m_w_glu, m_b_glu=m_b_glu, m_w_out=m_w_out, m_ln_ffn_g=m_ln_ffn_g, m_w_ffn_gate=m_w_ffn_gate, m_w_ffn_up=m_w_ffn_up, m_w_ffn_down=m_w_ffn_down, m_ln_final_g=m_ln_final_g, v_ln_mix_g=v_ln_mix_g, v_w_in=v_w_in, v_ret_log_gamma=v_ret_log_gamma, v_ssm_a_re=v_ssm_a_re, v_ssm_a_im=v_ssm_a_im, v_ssm_log_dt=v_ssm_log_dt, v_ssm_b_re=v_ssm_b_re, v_ssm_b_im=v_ssm_b_im, v_ssm_c_re=v_ssm_c_re, v_ssm_c_im=v_ssm_c_im, v_ssm_d=v_ssm_d, v_w_glu=v_w_glu, v_b_glu=v_b_glu, v_w_out=v_w_out, v_ln_ffn_g=v_ln_ffn_g, v_w_ffn_gate=v_w_ffn_gate, v_w_ffn_up=v_w_ffn_up, v_w_ffn_down=v_w_ffn_down, v_ln_final_g=v_ln_final_g)
    weights = {n: given[n] for n in TWIN_WEIGHTS}
    shared = {n: given[n] for n in SHARED_INPUTS}
    per_example = {n: given[n] for n in ['x']}
    grad_fn = _jax.value_and_grad(_loss, argnums=(0, 1))

    def one_microbatch(ex, loss_target):
        ex = dict(ex)
        diff = ex.pop(TWIN_DIFF_INPUT)
        return grad_fn(weights, diff, {**shared, **ex}, loss_target)

    if N_MICROBATCH == 1:
        loss, (grad_w, grad_x) = one_microbatch(per_example, given["loss_target"])
    else:
        def body(carry, xs):
            loss_sum, grad_sum = carry
            l_k, (gw_k, gx_k) = one_microbatch(xs[0], xs[1])
            with _jax.named_scope("update"):
                return (loss_sum + l_k, _jax.tree.map(_jnp.add, grad_sum, gw_k)), gx_k

        init = (_jnp.zeros((), _jnp.float32), _jax.tree.map(_jnp.zeros_like, weights))
        (loss, grad_w), grad_x = _jax.lax.scan(body, init, (per_example, given["loss_target"]))
    with _jax.named_scope("update"):
        delta_w, new_m, new_v = {}, {}, {}
        for n in TWIN_WEIGHTS:
            delta_w[n], new_m[n], new_v[n] = _adamw(weights[n], grad_w[n], given["m_" + n], given["v_" + n])
    return (loss, grad_x, *[grad_w[n] for n in TWIN_WEIGHTS], *[delta_w[n] for n in TWIN_WEIGHTS],
            *[new_m[n] for n in TWIN_WEIGHTS], *[new_v[n] for n in TWIN_WEIGHTS])
```

```python
import functools
import math

import jax
import jax.numpy as jnp
from jax import lax
from jax.experimental import pallas as pl
from jax.experimental.pallas import tpu as pltpu

F32 = jnp.float32
BF16 = jnp.bfloat16

D_MODEL = 2048
DEPTH = 2
HEADS = 4
QK_DIM = 256
V_DIM = 512
QK_WIDTH = HEADS * QK_DIM
ROPE_BASE = 10000.0
GROUP = 16
N_GROUPS = D_MODEL // GROUP
N_STATE = 64
D_FF = 5632
IN_WIDTH = 2 * QK_WIDTH + 5 * D_MODEL
EPS = 1e-6
N_CHIPS = 4

ADAM_LR = 0.001
ADAM_B1 = 0.9
ADAM_B2 = 0.999
ADAM_EPS = 1e-08
ADAM_WD = 0.01
ADAM_STEP = 10

LANES = 128
SUBLANES = 8
VMEM_LIMIT = 56 * 1024 * 1024
SEGMENTS = SUBLANES
GROUPS_PER_TILE = LANES // GROUP
STATE_COLS = GROUPS_PER_TILE * N_STATE
N_TILES = D_MODEL // LANES

MESH = pl.DeviceIdType.MESH
HBM_SPEC = pl.BlockSpec(memory_space=pltpu.HBM)


def _params(sem=None, **kw):
    return pltpu.CompilerParams(dimension_semantics=sem, vmem_limit_bytes=VMEM_LIMIT, **kw)


def _tile(n, cap=1024):
    for t in (1024, 512, 256, 128):
        if t <= cap and n % t == 0:
            return t
    raise ValueError(n)


def _rows_call(fn, rows, pars, row_outs, par_outs, *, tm, name):
    m = rows[0][0].shape[0]
    nr, npar, nro, npo = len(rows), len(pars), len(row_outs), len(par_outs)

    def body(*refs):
        rin = refs[:nr]
        pin = refs[nr:nr + npar]
        rout = refs[nr + npar:nr + npar + nro]
        pout = refs[nr + npar + nro:]
        res = fn(*[r[...] for r in rin], *[p[...] for p in pin])
        if not isinstance(res, (tuple, list)):
            res = (res,)
        for r, v in zip(rout, res[:nro]):
            r[...] = v.astype(r.dtype)
        if npo:
            @pl.when(pl.program_id(0) == 0)
            def _():
                for p in pout:
                    p[...] = jnp.zeros(p.shape, p.dtype)
            for p, v in zip(pout, res[nro:]):
                p[...] += v

    in_specs = [pl.BlockSpec((tm, w), functools.partial(lambda cb, i: (i, cb), cb)) for (_, w, cb) in rows]
    in_specs += [pl.BlockSpec(p.shape, lambda i: (0, 0)) for p in pars]
    out_specs = [pl.BlockSpec((tm, w), lambda i: (i, 0)) for (w, _) in row_outs]
    out_specs += [pl.BlockSpec(s, lambda i: (0, 0)) for s in par_outs]
    out_shape = [jax.ShapeDtypeStruct((m, w), dt) for (w, dt) in row_outs]
    out_shape += [jax.ShapeDtypeStruct(s, F32) for s in par_outs]
    res = pl.pallas_call(
        body, name=name, grid=(m // tm,), in_specs=in_specs, out_specs=out_specs, out_shape=out_shape,
        compiler_params=_params(("arbitrary",) if npo else ("parallel",)),
    )(*[a for (a, _, _) in rows], *pars)
    return res


def _f32(*vals):
    return [v.astype(F32) for v in vals]


def _f_rms(x, g):
    r = lax.rsqrt(jnp.mean(x * x, axis=-1, keepdims=True) + EPS)
    return x * r * g


def _rms_fwd(x, g, name):
    return _rows_call(lambda xv, gv: _f_rms(xv, gv), [(x, D_MODEL, 0)], [g], [(D_MODEL, BF16)], [],
                      tm=256, name=name)[0]


def _rms_bwd(x, dh, dres, g, name):
    def fn(xv, dhv, drv, gv):
        _, vjp = jax.vjp(_f_rms, xv, gv)
        dx, dg = vjp(dhv)
        dx = dx + drv
        return dx, dx, dg
    return _rows_call(fn, [(x, D_MODEL, 0), (dh, D_MODEL, 0), (dres, D_MODEL, 0)], [g],
                      [(D_MODEL, F32), (D_MODEL, BF16)], [(1, D_MODEL)], tm=256, name=name)


def _rot_heads(xv, cos, sin, scale):
    half = QK_DIM // 2
    outs = []
    for h in range(HEADS):
        x1 = xv[:, h * QK_DIM:h * QK_DIM + half]
        x2 = xv[:, h * QK_DIM + half:(h + 1) * QK_DIM]
        outs += [(x1 * cos - x2 * sin) * scale, (x1 * sin + x2 * cos) * scale]
    return jnp.concatenate(outs, axis=1)


def _rot_fwd(proj, cos, sin, name):
    def fn(q, k, cv, sv):
        return _rot_heads(q, cv, sv, 1.0), _rot_heads(k, cv, sv, QK_DIM ** -0.5)
    return _rows_call(fn, [(proj, QK_WIDTH, 0), (proj, QK_WIDTH, 1), (cos, LANES, 0), (sin, LANES, 0)], [],
                      [(QK_WIDTH, BF16), (QK_WIDTH, BF16)], [], tm=256, name=name)


def _rot_bwd(dqr, dkr, dv, cos, sin, name):
    def fn(dq, dk, dvv, cv, sv):
        return jnp.concatenate([_rot_heads(dq, cv, -sv, 1.0), _rot_heads(dk, cv, -sv, QK_DIM ** -0.5), dvv], axis=1)
    return _rows_call(fn, [(dqr, QK_WIDTH, 0), (dkr, QK_WIDTH, 0), (dv, D_MODEL, 0), (cos, LANES, 0), (sin, LANES, 0)],
                      [], [(2 * QK_WIDTH + D_MODEL, BF16)], [], tm=256, name=name)[0]


def _f_post1(y0, y1, y2, y3, g, gr, s5, u, dsk):
    yn = [yh * lax.rsqrt(jnp.mean(yh * yh, axis=-1, keepdims=True) + EPS) for yh in (y0, y1, y2, y3)]
    ret = jax.nn.sigmoid(gr) * (jax.nn.silu(g) * jnp.concatenate(yn, axis=1))
    ysg = jax.nn.gelu(s5 + dsk * u)
    return ret, ysg


def _post1_rows(y, proj, s5y):
    rows = [(y, V_DIM, h) for h in range(HEADS)]
    rows += [(proj, D_MODEL, 2), (proj, D_MODEL, 4), (s5y, D_MODEL, 0), (proj, D_MODEL, 3)]
    return rows


def _post1_fwd(y, proj, s5y, dsk, name):
    def fn(*vals):
        ret, ysg = _f_post1(*vals)
        return ret, ysg, ysg
    return _rows_call(fn, _post1_rows(y, proj, s5y), [dsk],
                      [(D_MODEL, F32), (D_MODEL, F32), (D_MODEL, BF16)], [], tm=128, name=name)


def _post1_bwd(y, proj, s5y, dret, dys, dsk, name):
    def fn(*vals):
        prim = vals[:8] + (vals[10],)
        _, vjp = jax.vjp(_f_post1, *prim)
        gy0, gy1, gy2, gy3, gg, ggr, gs5, gu, gd = vjp((vals[8], vals[9]))
        return jnp.concatenate([gy0, gy1, gy2, gy3], axis=1), gg, ggr, gs5, gu, gd
    rows = _post1_rows(y, proj, s5y) + [(dret, D_MODEL, 0), (dys, D_MODEL, 0)]
    return _rows_call(fn, rows, [dsk],
                      [(D_MODEL, BF16), (D_MODEL, BF16), (D_MODEL, BF16), (D_MODEL, F32), (D_MODEL, F32)],
                      [(1, D_MODEL)], tm=128, name=name)


def _f_merge(z, ysg, gs, ret, b):
    return ret + jax.nn.sigmoid(gs) * (ysg * jax.nn.sigmoid(z + b))


def _merge_fwd(z, ysg, proj, ret, b, name):
    return _rows_call(_f_merge, [(z, D_MODEL, 0), (ysg, D_MODEL, 0), (proj, D_MODEL, 5), (ret, D_MODEL, 0)], [b],
                      [(D_MODEL, BF16)], [], tm=128, name=name)[0]


def _merge_bwd(z, ysg, proj, ret, dm, b, name):
    def fn(zv, yv, gv, rv, dmv, bv):
        _, vjp = jax.vjp(_f_merge, zv, yv, gv, rv, bv)
        gz, gy, gg, _, gb = vjp(dmv)
        return gz, gy, gg, gb
    rows = [(z, D_MODEL, 0), (ysg, D_MODEL, 0), (proj, D_MODEL, 5), (ret, D_MODEL, 0), (dm, D_MODEL, 0)]
    return _rows_call(fn, rows, [b], [(D_MODEL, BF16), (D_MODEL, F32), (D_MODEL, BF16)], [(1, D_MODEL)],
                      tm=128, name=name)


def _f_glu(a, b):
    return jax.nn.silu(a) * b


def _glu_fwd(ab, name):
    return _rows_call(_f_glu, [(ab, D_FF, 0), (ab, D_FF, 1)], [], [(D_FF, BF16)], [], tm=128, name=name)[0]


def _glu_bwd(ab, df, name):
    def fn(a, b, d):
        _, vjp = jax.vjp(_f_glu, a, b)
        ga, gb = vjp(d)
        return jnp.concatenate([ga, gb], axis=1)
    return _rows_call(fn, [(ab, D_FF, 0), (ab, D_FF, 1), (df, D_FF, 0)], [], [(2 * D_FF, BF16)], [],
                      tm=128, name=name)[0]


def _loss_stage(x, tgt, g, name):
    def fn(xv, tv, gv):
        def lf(xx, gg):
            err = _f_rms(xx, gg) - tv
            row = jnp.mean(err * err, axis=-1, keepdims=True)
            return 0.5 * jnp.sum(row, axis=0, keepdims=True)
        l, vjp = jax.vjp(lf, xv, gv)
        dx, dg = vjp(jnp.ones((1, 1), F32))
        return dx, dx, jnp.broadcast_to(l, (1, LANES)), dg
    return _rows_call(fn, [(x, D_MODEL, 0), (tgt, D_MODEL, 0)], [g], [(D_MODEL, F32), (D_MODEL, BF16)],
                      [(1, LANES), (1, D_MODEL)], tm=256, name=name)


def _cast_bf16(a, name):
    m, n = a.shape
    return _rows_call(lambda v: v, [(a, n, 0)], [], [(n, BF16)], [], tm=_tile(m, 256), name=name)[0]


def _adamw(w, g, m, v, name):
    rows, cols = w.shape

    def fn(wv, gv, mv, vv):
        mn = ADAM_B1 * mv + (1.0 - ADAM_B1) * gv
        vn = ADAM_B2 * vv + (1.0 - ADAM_B2) * (gv * gv)
        m_hat = mn / (1.0 - ADAM_B1 ** ADAM_STEP)
        v_hat = vn / (1.0 - ADAM_B2 ** ADAM_STEP)
        delta = -ADAM_LR * (m_hat / (jnp.sqrt(v_hat) + ADAM_EPS) + ADAM_WD * wv)
        return delta, mn, vn
    tm = _tile(rows, 128 if cols > D_FF // N_CHIPS else (256 if cols > LANES else 512))
    return _rows_call(fn, [(w, cols, 0), (g, cols, 0), (m, cols, 0), (v, cols, 0)], [],
                      [(cols, F32)] * 3, [], tm=tm, name=name)


def _matmul(a, b, mode, out_dtypes, *, name, add=None, stack=None, layer=None):
    if mode == "nn":
        (m, k), (_, n) = a.shape, b.shape
    elif mode == "nt":
        (m, k), (n, _) = a.shape, b.shape
    else:
        (k, m), (_, n) = a.shape, b.shape
    tm, tn, tk = _tile(m), _tile(n), _tile(k)
    nk = k // tk
    if mode == "nn":
        a_spec = pl.BlockSpec((tm, tk), lambda i, j, kk: (i, kk))
        b_spec = pl.BlockSpec((tk, tn), lambda i, j, kk: (kk, j))
        dims = (((1,), (0,)), ((), ()))
    elif mode == "nt":
        a_spec = pl.BlockSpec((tm, tk), lambda i, j, kk: (i, kk))
        b_spec = pl.BlockSpec((tn, tk), lambda i, j, kk: (j, kk))
        dims = (((1,), (1,)), ((), ()))
    else:
        a_spec = pl.BlockSpec((tk, tm), lambda i, j, kk: (kk, i))
        b_spec = pl.BlockSpec((tk, tn), lambda i, j, kk: (kk, j))
        dims = (((0,), (0,)), ((), ()))
    n_out = len(out_dtypes)
    has_add = add is not None
    n_alias = n_out if (stack is not None and stack[0] is not None) else 0

    def body(*refs):
        a_ref, b_ref = refs[0], refs[1]
        add_ref = refs[2] if has_add else None
        outs = refs[2 + has_add + n_alias:2 + has_add + n_alias + n_out]
        acc = refs[-1]
        kk = pl.program_id(2)

        @pl.when(kk == 0)
        def _():
            acc[...] = jnp.zeros(acc.shape, F32)

        acc[...] += lax.dot_general(a_ref[...], b_ref[...], dims, preferred_element_type=F32)

        @pl.when(kk == nk - 1)
        def _():
            r = acc[...]
            if has_add:
                r = r + add_ref[...]
            for o in outs:
                o[...] = r.astype(o.dtype)

    in_specs = [a_spec, b_spec]
    args = [a, b]
    if has_add:
        in_specs.append(pl.BlockSpec((tm, tn), lambda i, j, kk: (i, j)))
        args.append(add)
    aliases = {}
    if stack is None:
        out_specs = [pl.BlockSpec((tm, tn), lambda i, j, kk: (i, j))] * n_out
        out_shape = [jax.ShapeDtypeStruct((m, n), dt) for dt in out_dtypes]
    else:
        out_specs = [pl.BlockSpec((None, tm, tn), functools.partial(lambda ly, i, j, kk: (ly, i, j), layer))] * n_out
        out_shape = [jax.ShapeDtypeStruct((DEPTH, m, n), dt) for dt in out_dtypes]
        if n_alias:
            for t, buf in enumerate(stack):
                aliases[len(args)] = t
                in_specs.append(pl.BlockSpec(memory_space=pl.ANY))
                args.append(buf)
    return pl.pallas_call(
        body, name=name, grid=(m // tm, n // tn, nk), in_specs=in_specs, out_specs=out_specs, out_shape=out_shape,
        scratch_shapes=[pltpu.VMEM((tm, tn), F32)], input_output_aliases=aliases,
        compiler_params=_params(("parallel", "parallel", "arbitrary")),
    )(*args)


RET_TQ = 512


def _decay(lg_ref, i, tq, seq):
    n_idx = i * tq + lax.broadcasted_iota(jnp.int32, (tq, seq), 0)
    m_idx = lax.broadcasted_iota(jnp.int32, (tq, seq), 1)
    diff = (n_idx - m_idx).astype(F32)
    lgf = lg_ref[0, 0:1, 0:1]
    lgb = lg_ref[0, 1:2, 0:1]
    causal = diff >= 0
    return jnp.exp(jnp.where(causal, lgf * diff, -lgb * diff)), diff, causal


_NT = (((1,), (1,)), ((), ()))
_TN = (((0,), (0,)), ((), ()))


def _ret_fwd(qr, kr, proj, lg, name):
    seq = qr.shape[0]
    tq = RET_TQ
    v_blk0 = (2 * QK_WIDTH) // V_DIM

    def body(q_ref, k_ref, v_ref, lg_ref, y_ref):
        i = pl.program_id(1)
        s = lax.dot_general(q_ref[...], k_ref[...], _NT, preferred_element_type=F32)
        dm, _, _ = _decay(lg_ref, i, tq, seq)
        p = (s * dm).astype(BF16)
        y_ref[...] = jnp.dot(p, v_ref[...].astype(BF16), preferred_element_type=F32)

    return pl.pallas_call(
        body, name=name, grid=(HEADS, seq // tq),
        in_specs=[pl.BlockSpec((tq, QK_DIM), lambda h, i: (i, h)),
                  pl.BlockSpec((seq, QK_DIM), lambda h, i: (0, h)),
                  pl.BlockSpec((seq, V_DIM), lambda h, i: (0, v_blk0 + h)),
                  pl.BlockSpec((1, 2, LANES), lambda h, i: (h, 0, 0))],
        out_specs=pl.BlockSpec((tq, V_DIM), lambda h, i: (i, h)),
        out_shape=jax.ShapeDtypeStruct((seq, HEADS * V_DIM), F32),
        compiler_params=_params(("parallel", "parallel")),
    )(qr, kr, proj, lg)


def _ret_bwd(qr, kr, proj, dy, lg, name):
    seq = qr.shape[0]
    tq = RET_TQ
    v_blk0 = (2 * QK_WIDTH) // V_DIM

    def body(q_ref, k_ref, v_ref, dy_ref, lg_ref, dq_ref, dk_ref, dv_ref, dlg_ref):
        i = pl.program_id(1)

        @pl.when(i == 0)
        def _():
            dk_ref[...] = jnp.zeros(dk_ref.shape, F32)
            dv_ref[...] = jnp.zeros(dv_ref.shape, F32)
            dlg_ref[...] = jnp.zeros(dlg_ref.shape, F32)

        q = q_ref[...]
        k = k_ref[...]
        vb = v_ref[...].astype(BF16)
        dyb = dy_ref[...]
        s = lax.dot_general(q, k, _NT, preferred_element_type=F32)
        dm, diff, causal = _decay(lg_ref, i, tq, seq)
        p = s * dm
        dp = lax.dot_general(dyb, vb, _NT, preferred_element_type=F32)
        dv_ref[...] += lax.dot_general(p.astype(BF16), dyb, _TN, preferred_element_type=F32)
        ds = (dp * dm).astype(BF16)
        dq_ref[...] = jnp.dot(ds, k, preferred_element_type=F32)
        dk_ref[...] += lax.dot_general(ds, q, _TN, preferred_element_type=F32)
        gd = dp * p * diff
        dlf = jnp.sum(jnp.sum(jnp.where(causal, gd, 0.0), axis=1, keepdims=True), axis=0, keepdims=True)
        dlb = jnp.sum(jnp.sum(jnp.where(causal, 0.0, -gd), axis=1, keepdims=True), axis=0, keepdims=True)
        row = lax.broadcasted_iota(jnp.int32, (2, LANES), 0)
        dlg_ref[0] += jnp.where(row == 0, dlf, dlb)

    return pl.pallas_call(
        body, name=name, grid=(HEADS, seq // tq),
        in_specs=[pl.BlockSpec((tq, QK_DIM), lambda h, i: (i, h)),
                  pl.BlockSpec((seq, QK_DIM), lambda h, i: (0, h)),
                  pl.BlockSpec((seq, V_DIM), lambda h, i: (0, v_blk0 + h)),
                  pl.BlockSpec((tq, V_DIM), lambda h, i: (i, h)),
                  pl.BlockSpec((1, 2, LANES), lambda h, i: (h, 0, 0))],
        out_specs=[pl.BlockSpec((tq, QK_DIM), lambda h, i: (i, h)),
                   pl.BlockSpec((seq, QK_DIM), lambda h, i: (0, h)),
                   pl.BlockSpec((seq, V_DIM), lambda h, i: (0, h)),
                   pl.BlockSpec((1, 2, LANES), lambda h, i: (h, 0, 0))],
        out_shape=[jax.ShapeDtypeStruct((seq, QK_WIDTH), F32), jax.ShapeDtypeStruct((seq, QK_WIDTH), F32),
                   jax.ShapeDtypeStruct((seq, HEADS * V_DIM), F32), jax.ShapeDtypeStruct((HEADS, 2, LANES), F32)],
        compiler_params=_params(("parallel", "arbitrary")),
    )(qr, kr, proj, dy, lg)


def _shift_rows(v, reverse):
    row = lax.broadcasted_iota(jnp.int32, v.shape, 0)
    if reverse:
        return jnp.where(row == SEGMENTS - 1, 0.0, pltpu.roll(v, SEGMENTS - 1, 0))
    return jnp.where(row == 0, 0.0, pltpu.roll(v, 1, 0))


def _scan(xr_ref, xi_ref, lam, reverse, conj):
    steps = xr_ref.shape[0] // SEGMENTS
    cols = xr_ref.shape[1]
    sign = -1.0 if conj else 1.0
    lr = jnp.broadcast_to(lam[0], (SEGMENTS, cols))
    li = jnp.broadcast_to(lam[1], (SEGMENTS, cols)) * sign
    lrt = jnp.broadcast_to(lam[2], (SEGMENTS, cols))
    lit = jnp.broadcast_to(lam[3], (SEGMENTS, cols)) * sign
    zero = jnp.zeros((SEGMENTS, cols), F32)

    def rows_of(t):
        tt = steps - 1 - t if reverse else t
        return pl.ds(pl.multiple_of(tt * SEGMENTS, SEGMENTS), SEGMENTS)

    def step(t, carry):
        sr, si = carry
        rows = rows_of(t)
        nr = lr * sr - li * si + xr_ref[rows, :]
        ni = lr * si + li * sr + xi_ref[rows, :]
        xr_ref[rows, :] = nr
        xi_ref[rows, :] = ni
        return nr, ni

    er, ei = lax.fori_loop(0, steps, step, (zero, zero))
    cr, ci = zero, zero
    for _ in range(SEGMENTS - 1):
        tr = er + lrt * cr - lit * ci
        ti = ei + lrt * ci + lit * cr
        cr, ci = _shift_rows(tr, reverse), _shift_rows(ti, reverse)

    def fix(t, carry):
        pr, pi = carry
        rows = rows_of(t)
        xr_ref[rows, :] += pr * cr - pi * ci
        xi_ref[rows, :] += pr * ci + pi * cr
        return pr * lr - pi * li, pr * li + pi * lr

    lax.fori_loop(0, steps, fix, (lr, li))


def _permute_in(dst_ref, src_ref):
    steps = src_ref.shape[0] // SEGMENTS
    for s in range(SEGMENTS):
        dst_ref[pl.ds(s, steps, stride=SEGMENTS), :] = src_ref[s * steps:(s + 1) * steps, :].astype(dst_ref.dtype)


def _unpermute(src_ref, s):
    steps = src_ref.shape[0] // SEGMENTS
    return src_ref[pl.ds(s, steps, stride=SEGMENTS), :]


def _s5_fwd(proj, bblk, cblk, lam, name):
    seq = proj.shape[0]
    u_blk0 = (2 * QK_WIDTH + 2 * D_MODEL) // LANES
    sc = STATE_COLS

    def body(u_ref, b_ref, c_ref, lam_ref, y_ref, up_ref, yp_ref, xr_ref, xi_ref):
        _permute_in(up_ref, u_ref)
        ub = up_ref[...].astype(BF16)
        for d in range(2):
            xr_ref[...] = jnp.dot(ub, b_ref[d, :, 0:sc], preferred_element_type=F32)
            xi_ref[...] = jnp.dot(ub, b_ref[d, :, sc:2 * sc], preferred_element_type=F32)
            lm = [lam_ref[d, r:r + 1, :] for r in range(4)]
            _scan(xr_ref, xi_ref, lm, reverse=(d == 1), conj=False)
            yd = (jnp.dot(xr_ref[...].astype(BF16), c_ref[d, 0:sc, :], preferred_element_type=F32)
                  + jnp.dot(xi_ref[...].astype(BF16), c_ref[d, sc:2 * sc, :], preferred_element_type=F32))
            if d == 0:
                yp_ref[...] = yd
            else:
                yp_ref[...] += yd
        steps = seq // SEGMENTS
        for s in range(SEGMENTS):
            y_ref[s * steps:(s + 1) * steps, :] = _unpermute(yp_ref, s)

    return pl.pallas_call(
        body, name=name, grid=(N_TILES,),
        in_specs=[pl.BlockSpec((seq, LANES), lambda j: (0, u_blk0 + j)),
                  pl.BlockSpec((2, None, LANES, 2 * sc), lambda j: (0, j, 0, 0)),
                  pl.BlockSpec((2, None, 2 * sc, LANES), lambda j: (0, j, 0, 0)),
                  pl.BlockSpec((2, None, 4, sc), lambda j: (0, j, 0, 0))],
        out_specs=pl.BlockSpec((seq, LANES), lambda j: (0, j)),
        out_shape=jax.ShapeDtypeStruct((seq, D_MODEL), F32),
        scratch_shapes=[pltpu.VMEM((seq, LANES), F32), pltpu.VMEM((seq, LANES), F32),
                        pltpu.VMEM((seq, sc), F32), pltpu.VMEM((seq, sc), F32)],
        compiler_params=_params(("parallel",)),
    )(proj, bblk, cblk, lam)


def _s5_bwd(proj, dy, du_part, bblk, cblk, lam, name):
    seq = proj.shape[0]
    u_blk0 = (2 * QK_WIDTH + 2 * D_MODEL) // LANES
    sc = STATE_COLS
    steps = seq // SEGMENTS

    def body(u_ref, dy_ref, dup_ref, b_ref, c_ref, lam_ref, du_ref, db_ref, dc_ref, dlam_ref,
             up_ref, dyp_ref, dua_ref, xr_ref, xi_ref, gr_ref, gi_ref):
        _permute_in(up_ref, u_ref)
        _permute_in(dyp_ref, dy_ref)
        ub = up_ref[...].astype(BF16)
        dyb = dyp_ref[...].astype(BF16)
        for d in range(2):
            reverse = d == 1
            xr_ref[...] = jnp.dot(ub, b_ref[d, :, 0:sc], preferred_element_type=F32)
            xi_ref[...] = jnp.dot(ub, b_ref[d, :, sc:2 * sc], preferred_element_type=F32)
            lm = [lam_ref[d, r:r + 1, :] for r in range(4)]
            _scan(xr_ref, xi_ref, lm, reverse=reverse, conj=False)
            xrb = xr_ref[...].astype(BF16)
            xib = xi_ref[...].astype(BF16)
            dc_ref[d, 0:sc, :] = lax.dot_general(xrb, dyb, _TN, preferred_element_type=F32)
            dc_ref[d, sc:2 * sc, :] = lax.dot_general(xib, dyb, _TN, preferred_element_type=F32)
            gr_ref[...] = lax.dot_general(dyb, c_ref[d, 0:sc, :], _NT, preferred_element_type=F32)
            gi_ref[...] = lax.dot_general(dyb, c_ref[d, sc:2 * sc, :], _NT, preferred_element_type=F32)
            _scan(gr_ref, gi_ref, lm, reverse=not reverse, conj=True)

            def prev_rows(t):
                tt = t + 1 if reverse else t - 1
                return pl.ds(pl.multiple_of(tt * SEGMENTS, SEGMENTS), SEGMENTS)

            def acc_step(t, carry):
                ar, ai = carry
                rows = pl.ds(pl.multiple_of(t * SEGMENTS, SEGMENTS), SEGMENTS)
                pr = xr_ref[prev_rows(t), :]
                pi = xi_ref[prev_rows(t), :]
                zr = gr_ref[rows, :]
                zi = gi_ref[rows, :]
                return ar + zr * pr + zi * pi, ai + zi * pr - zr * pi

            zero = jnp.zeros((SEGMENTS, sc), F32)
            if reverse:
                ar, ai = lax.fori_loop(0, steps - 1, acc_step, (zero, zero))
                edge = pl.ds((steps - 1) * SEGMENTS, SEGMENTS)
                pr = _shift_rows(xr_ref[0:SEGMENTS, :], True)
                pi = _shift_rows(xi_ref[0:SEGMENTS, :], True)
            else:
                ar, ai = lax.fori_loop(1, steps, acc_step, (zero, zero))
                edge = pl.ds(0, SEGMENTS)
                last = pl.ds((steps - 1) * SEGMENTS, SEGMENTS)
                pr = _shift_rows(xr_ref[last, :], False)
                pi = _shift_rows(xi_ref[last, :], False)
            zr = gr_ref[edge, :]
            zi = gi_ref[edge, :]
            ar = ar + zr * pr + zi * pi
            ai = ai + zi * pr - zr * pi
            dlam_ref[d, 0:1, :] = jnp.sum(ar, axis=0, keepdims=True)
            dlam_ref[d, 1:2, :] = jnp.sum(ai, axis=0, keepdims=True)

            grb = gr_ref[...].astype(BF16)
            gib = gi_ref[...].astype(BF16)
            db_ref[d, :, 0:sc] = lax.dot_general(ub, grb, _TN, preferred_element_type=F32)
            db_ref[d, :, sc:2 * sc] = lax.dot_general(ub, gib, _TN, preferred_element_type=F32)
            dud = (lax.dot_general(grb, b_ref[d, :, 0:sc], _NT, preferred_element_type=F32)
                   + lax.dot_general(gib, b_ref[d, :, sc:2 * sc], _NT, preferred_element_type=F32))
            if d == 0:
                dua_ref[...] = dud
            else:
                dua_ref[...] += dud
        for s in range(SEGMENTS):
            rows = slice(s * steps, (s + 1) * steps)
            du_ref[rows, :] = (_unpermute(dua_ref, s) + dup_ref[rows, :]).astype(du_ref.dtype)

    return pl.pallas_call(
        body, name=name, grid=(N_TILES,),
        in_specs=[pl.BlockSpec((seq, LANES), lambda j: (0, u_blk0 + j)),
                  pl.BlockSpec((seq, LANES), lambda j: (0, j)),
                  pl.BlockSpec((seq, LANES), lambda j: (0, j)),
                  pl.BlockSpec((2, None, LANES, 2 * sc), lambda j: (0, j, 0, 0)),
                  pl.BlockSpec((2, None, 2 * sc, LANES), lambda j: (0, j, 0, 0)),
                  pl.BlockSpec((2, None, 4, sc), lambda j: (0, j, 0, 0))],
        out_specs=[pl.BlockSpec((seq, LANES), lambda j: (0, j)),
                   pl.BlockSpec((2, None, LANES, 2 * sc), lambda j: (0, j, 0, 0)),
                   pl.BlockSpec((2, None, 2 * sc, LANES), lambda j: (0, j, 0, 0)),
                   pl.BlockSpec((2, None, 2, sc), lambda j: (0, j, 0, 0))],
        out_shape=[jax.ShapeDtypeStruct((seq, D_MODEL), BF16),
                   jax.ShapeDtypeStruct((2, N_TILES, LANES, 2 * sc), F32),
                   jax.ShapeDtypeStruct((2, N_TILES, 2 * sc, LANES), F32),
                   jax.ShapeDtypeStruct((2, N_TILES, 2, sc), F32)],
        scratch_shapes=[pltpu.VMEM((seq, LANES), F32), pltpu.VMEM((seq, LANES), F32), pltpu.VMEM((seq, LANES), F32),
                        pltpu.VMEM((seq, sc), F32), pltpu.VMEM((seq, sc), F32),
                        pltpu.VMEM((seq, sc), F32), pltpu.VMEM((seq, sc), F32)],
        compiler_params=_params(("parallel",)),
    )(proj, dy, du_part, bblk, cblk, lam)


def _s5_discretize(a_re, a_im, log_dt, b_re, b_im, seg_len):
    dt = jnp.exp(log_dt)[..., None]
    e = jnp.exp(a_re * dt)
    lr, li = e * jnp.cos(a_im * dt), e * jnp.sin(a_im * dt)
    et = jnp.exp(a_re * dt * seg_len)
    lrt, lit = et * jnp.cos(a_im * dt * seg_len), et * jnp.sin(a_im * dt * seg_len)
    den = a_re * a_re + a_im * a_im
    qr = ((lr - 1.0) * a_re + li * a_im) / den
    qi = (li * a_re - (lr - 1.0) * a_im) / den
    br = qr[..., None] * b_re - qi[..., None] * b_im
    bi = qr[..., None] * b_im + qi[..., None] * b_re
    return lr, li, lrt, lit, br, bi


def _s5_pack(lr, li, lrt, lit, br, bi, c_re, c_im):
    eye = jnp.eye(GROUPS_PER_TILE, dtype=F32)

    def bd_b(b):
        b5 = b.reshape(2, N_TILES, GROUPS_PER_TILE, N_STATE, GROUP)
        return jnp.einsum("dtgph,gk->dtghkp", b5, eye).reshape(2, N_TILES, LANES, STATE_COLS)

    def bd_c(c):
        c5 = c.reshape(2, N_TILES, GROUPS_PER_TILE, GROUP, N_STATE)
        return jnp.einsum("dtghp,gk->dtkpgh", c5, eye).reshape(2, N_TILES, STATE_COLS, LANES)

    bblk = jnp.concatenate([bd_b(br), bd_b(bi)], axis=3)
    cblk = jnp.concatenate([bd_c(c_re), -bd_c(c_im)], axis=2)
    lam = jnp.stack([v.reshape(2, N_TILES, STATE_COLS) for v in (lr, li, lrt, lit)], axis=2)
    return bblk, cblk, lam


def _s5_unpack(dbblk, dcblk, dlam):
    eye = jnp.eye(GROUPS_PER_TILE, dtype=F32)

    def diag_b(d):
        d6 = d.reshape(2, N_TILES, GROUPS_PER_TILE, GROUP, GROUPS_PER_TILE, N_STATE)
        return jnp.einsum("dtghkp,gk->dtgph", d6, eye).reshape(2, N_GROUPS, N_STATE, GROUP)

    def diag_c(d):
        d6 = d.reshape(2, N_TILES, GROUPS_PER_TILE, N_STATE, GROUPS_PER_TILE, GROUP)
        return jnp.einsum("dtkpgh,gk->dtghp", d6, eye).reshape(2, N_GROUPS, GROUP, N_STATE)

    dbr, dbi = diag_b(dbblk[..., :STATE_COLS]), diag_b(dbblk[..., STATE_COLS:])
    dcr, dci = diag_c(dcblk[:, :, :STATE_COLS, :]), -diag_c(dcblk[:, :, STATE_COLS:, :])
    dlr = dlam[:, :, 0, :].reshape(2, N_GROUPS, N_STATE)
    dli = dlam[:, :, 1, :].reshape(2, N_GROUPS, N_STATE)
    return dlr, dli, dbr, dbi, dcr, dci


def _pos():
    return lax.axis_index("x"), lax.axis_index("y"), lax.axis_index("c")


def _remote(src, dst, ssem, rsem, dev):
    return pltpu.make_async_remote_copy(src_ref=src, dst_ref=dst, send_sem=ssem, recv_sem=rsem,
                                        device_id=dev, device_id_type=MESH)


_PIECES = (
    ("w_in", "in", D_MODEL, IN_WIDTH // N_CHIPS, 0, IN_WIDTH // N_CHIPS, 0),
    ("w_glu", "glu", D_MODEL // N_CHIPS, D_MODEL, D_MODEL // N_CHIPS, 0, 0),
    ("w_out", "out", D_MODEL // N_CHIPS, D_MODEL, D_MODEL // N_CHIPS, 0, 0),
    ("w_ffn_gate", "gu", D_MODEL, D_FF // N_CHIPS, 0, D_FF // N_CHIPS, 0),
    ("w_ffn_up", "gu", D_MODEL, D_FF // N_CHIPS, 0, D_FF // N_CHIPS, D_FF),
    ("w_ffn_down", "down", D_FF // N_CHIPS, D_MODEL, D_FF // N_CHIPS, 0, 0),
)
_BUFFERS = (("in", D_MODEL, IN_WIDTH), ("glu", D_MODEL, D_MODEL), ("out", D_MODEL, D_MODEL),
            ("gu", D_MODEL, 2 * D_FF), ("down", D_FF, D_MODEL))
_BUF_INDEX = {name: t for t, (name, _, _) in enumerate(_BUFFERS)}
N_PIECES = len(_PIECES)
N_BUFFERS = len(_BUFFERS)


def _piece_view(ref, piece, j):
    _, _, rs, cs, rstep, cstep, coff = piece
    return ref.at[pl.ds(j * rstep, rs), pl.ds(coff + j * cstep, cs)]


def _gather_weights(shards):
    def body(*refs):
        sh = refs[:N_PIECES]
        full = refs[N_PIECES:N_PIECES + N_BUFFERS]
        ssem, rsem, fssem, frsem, lsem = refs[N_PIECES + N_BUFFERS:]
        x, y, c = _pos()
        my_chip = 2 * x + y

        for mine in range(N_CHIPS):
            @pl.when(my_chip == mine)
            def _(mine=mine):
                others = [j for j in range(N_CHIPS) if j != mine]

                def view(p, j, layer):
                    return _piece_view(full[_BUF_INDEX[_PIECES[p][1]]].at[layer], _PIECES[p], j)

                local = []
                for p in range(N_PIECES):
                    for layer in range(DEPTH):
                        cp = pltpu.make_async_copy(sh[p].at[layer], view(p, mine, layer), lsem.at[p * DEPTH + layer])
                        cp.start()
                        local.append(cp)
                sends = []
                for p in range(N_PIECES):
                    for j in others:
                        cp = _remote(sh[p].at[c], view(p, mine, c), ssem.at[p * N_CHIPS + j],
                                     rsem.at[p * N_CHIPS + mine], (j // 2, j % 2, c))
                        cp.start()
                        sends.append(cp)
                for p in range(N_PIECES):
                    for j in others:
                        _remote(sh[p].at[c], view(p, j, c), ssem.at[p * N_CHIPS + j],
                                rsem.at[p * N_CHIPS + j], (j // 2, j % 2, c)).wait_recv()
                        cp = _remote(view(p, j, c), view(p, j, c), fssem.at[p * N_CHIPS + j],
                                     frsem.at[p * N_CHIPS + j], (x, y, 1 - c))
                        cp.start()
                        sends.append(cp)
                for p in range(N_PIECES):
                    for j in others:
                        _remote(view(p, j, 1 - c), view(p, j, 1 - c), fssem.at[p * N_CHIPS + j],
                                frsem.at[p * N_CHIPS + j], (x, y, 1 - c)).wait_recv()
                for cp in sends:
                    cp.wait_send()
                for cp in local:
                    cp.wait()

    nsem = N_PIECES * N_CHIPS
    return pl.pallas_call(
        body, name="gather_weights",
        in_specs=[HBM_SPEC] * N_PIECES, out_specs=[HBM_SPEC] * N_BUFFERS,
        out_shape=[jax.ShapeDtypeStruct((DEPTH, r, cc), BF16) for (_, r, cc) in _BUFFERS],
        scratch_shapes=[pltpu.SemaphoreType.DMA((nsem,)), pltpu.SemaphoreType.DMA((nsem,)),
                        pltpu.SemaphoreType.DMA((nsem,)), pltpu.SemaphoreType.DMA((nsem,)),
                        pltpu.SemaphoreType.DMA((N_PIECES * DEPTH,))],
        compiler_params=_params(has_side_effects=True),
    )(*shards)


def _swap_other_layer(dwb):
    def body(*refs):
        src = refs[:N_BUFFERS]
        dst = refs[N_BUFFERS:2 * N_BUFFERS]
        ssem, rsem = refs[2 * N_BUFFERS:]
        x, y, c = _pos()
        cps = [_remote(src[t].at[1 - c], dst[t], ssem.at[t], rsem.at[t], (x, y, 1 - c)) for t in range(N_BUFFERS)]
        for cp in cps:
            cp.start()
        for cp in cps:
            cp.wait()

    return pl.pallas_call(
        body, name="swap_other_layer",
        in_specs=[HBM_SPEC] * N_BUFFERS, out_specs=[HBM_SPEC] * N_BUFFERS,
        out_shape=[jax.ShapeDtypeStruct((r, cc), BF16) for (_, r, cc) in _BUFFERS],
        scratch_shapes=[pltpu.SemaphoreType.DMA((N_BUFFERS,)), pltpu.SemaphoreType.DMA((N_BUFFERS,))],
        compiler_params=_params(has_side_effects=True),
    )(*dwb)


def _chip_partial(dw, got, c_idx, name):
    _, r, cc = dw.shape
    tm, tn = _tile(r, 512), _tile(cc)

    def body(c_ref, dw_ref, got_ref, out_ref):
        out_ref[...] = (dw_ref[...] + got_ref[...].astype(F32)).astype(BF16)

    return pl.pallas_call(
        body, name=name,
        grid_spec=pltpu.PrefetchScalarGridSpec(
            num_scalar_prefetch=1, grid=(r // tm, cc // tn),
            in_specs=[pl.BlockSpec((None, tm, tn), lambda i, j, cr: (cr[0], i, j)),
                      pl.BlockSpec((tm, tn), lambda i, j, cr: (i, j))],
            out_specs=pl.BlockSpec((tm, tn), lambda i, j, cr: (i, j))),
        out_shape=jax.ShapeDtypeStruct((r, cc), BF16),
        compiler_params=_params(("parallel", "parallel")),
    )(c_idx, dw, got)


def _scatter_partials(partials):
    def body(*refs):
        src = refs[:N_BUFFERS]
        dst = refs[N_BUFFERS:N_BUFFERS + N_PIECES]
        ssem, rsem = refs[N_BUFFERS + N_PIECES:]
        x, y, c = _pos()
        my_chip = 2 * x + y
        for mine in range(N_CHIPS):
            @pl.when(my_chip == mine)
            def _(mine=mine):
                others = [j for j in range(N_CHIPS) if j != mine]
                sends = []
                for p in range(N_PIECES):
                    buf = src[_BUF_INDEX[_PIECES[p][1]]]
                    for j in others:
                        cp = _remote(_piece_view(buf, _PIECES[p], j), dst[p].at[mine], ssem.at[p * N_CHIPS + j],
                                     rsem.at[p * N_CHIPS + mine], (j // 2, j % 2, c))
                        cp.start()
                        sends.append(cp)
                for p in range(N_PIECES):
                    for j in others:
                        _remote(dst[p].at[j], dst[p].at[j], ssem.at[p * N_CHIPS + j],
                                rsem.at[p * N_CHIPS + j], (j // 2, j % 2, c)).wait_recv()
                for cp in sends:
                    cp.wait_send()

    nsem = N_PIECES * N_CHIPS
    return pl.pallas_call(
        body, name="scatter_partials",
        in_specs=[HBM_SPEC] * N_BUFFERS, out_specs=[HBM_SPEC] * N_PIECES,
        out_shape=[jax.ShapeDtypeStruct((N_CHIPS, p[2], p[3]), BF16) for p in _PIECES],
        scratch_shapes=[pltpu.SemaphoreType.DMA((nsem,)), pltpu.SemaphoreType.DMA((nsem,))],
        compiler_params=_params(has_side_effects=True),
    )(*partials)


def _reduce_shard(piece, dw, got, recv, idx, name):
    _, _, rs, cs, rstep, cstep, coff = piece
    tm = _tile(rs, 256)
    rb = rstep // tm
    cb0, cbs = coff // cs, (cstep // cs)

    def own_map(i, s):
        return (i + s[1] * rb, cb0 + s[1] * cbs)

    def body(s_ref, dw_ref, got_ref, r1, r2, r3, out_ref):
        acc = dw_ref[...] + got_ref[...].astype(F32)
        for r in (r1, r2, r3):
            acc = acc + r[...].astype(F32)
        out_ref[...] = acc

    def recv_map(k):
        return lambda i, s: ((s[1] + k) % N_CHIPS, i, 0)

    return pl.pallas_call(
        body, name=name,
        grid_spec=pltpu.PrefetchScalarGridSpec(
            num_scalar_prefetch=1, grid=(rs // tm,),
            in_specs=[pl.BlockSpec((None, tm, cs), lambda i, s: (s[0],) + own_map(i, s)),
                      pl.BlockSpec((tm, cs), own_map),
                      pl.BlockSpec((None, tm, cs), recv_map(1)),
                      pl.BlockSpec((None, tm, cs), recv_map(2)),
                      pl.BlockSpec((None, tm, cs), recv_map(3))],
            out_specs=pl.BlockSpec((tm, cs), lambda i, s: (i, 0))),
        out_shape=jax.ShapeDtypeStruct((rs, cs), F32),
        compiler_params=_params(("parallel",)),
    )(idx, dw, got, recv, recv, recv)


def _share_with_sibling(reduced):
    def body(*refs):
        src = refs[:N_PIECES]
        dst = refs[N_PIECES:2 * N_PIECES]
        ssem, rsem, lsem = refs[2 * N_PIECES:]
        x, y, c = _pos()
        loc = [pltpu.make_async_copy(src[p], dst[p].at[c], lsem.at[p]) for p in range(N_PIECES)]
        rem = [_remote(src[p], dst[p].at[c], ssem.at[p], rsem.at[p], (x, y, 1 - c)) for p in range(N_PIECES)]
        for cp in loc + rem:
            cp.start()
        for p in range(N_PIECES):
            rem[p].wait_send()
            _remote(src[p], dst[p].at[1 - c], ssem.at[p], rsem.at[p], (x, y, 1 - c)).wait_recv()
            loc[p].wait()

    return pl.pallas_call(
        body, name="share_with_sibling",
        in_specs=[HBM_SPEC] * N_PIECES, out_specs=[HBM_SPEC] * N_PIECES,
        out_shape=[jax.ShapeDtypeStruct((DEPTH, p[2], p[3]), F32) for p in _PIECES],
        scratch_shapes=[pltpu.SemaphoreType.DMA((N_PIECES,)), pltpu.SemaphoreType.DMA((N_PIECES,)),
                        pltpu.SemaphoreType.DMA((N_PIECES,))],
        compiler_params=_params(has_side_effects=True),
    )(*reduced)


N_DEV = 8


def _all_gather_rows(v):
    rows = v.shape[0]

    def body(v_ref, out_ref, ssem, rsem, lsem):
        x, y, c = _pos()
        me, sibling = (x, y, c), (x, y, 1 - c)
        chips = [(1 - x, y), (x, 1 - y), (1 - x, 1 - y)]

        def slot(px, py, pc):
            return out_ref.at[4 * px + 2 * py + pc]

        def copy(k, block, to, src=None):
            return _remote(slot(*block) if src is None else src, slot(*block), ssem.at[k], rsem.at[k], to)

        mine = pltpu.make_async_copy(v_ref, slot(*me), lsem)
        mine.start()
        first = [copy(0, me, sibling, src=v_ref)]
        first += [copy(1 + j, me, (*chip, c), src=v_ref) for j, chip in enumerate(chips)]
        for cp in first:
            cp.start()
        passed = [copy(4 + j, (*chip, c), sibling) for j, chip in enumerate(chips)]
        for j, chip in enumerate(chips):
            copy(1 + j, (*chip, c), me).wait_recv()
            passed[j].start()
        copy(0, sibling, me).wait_recv()
        for j, chip in enumerate(chips):
            copy(4 + j, (*chip, 1 - c), me).wait_recv()
        for cp in first + passed:
            cp.wait_send()
        mine.wait()

    return pl.pallas_call(
        body, name="all_gather_small",
        in_specs=[HBM_SPEC], out_specs=HBM_SPEC,
        out_shape=jax.ShapeDtypeStruct((N_DEV, rows, LANES), F32),
        scratch_shapes=[pltpu.SemaphoreType.DMA((7,)), pltpu.SemaphoreType.DMA((7,)), pltpu.SemaphoreType.DMA],
        compiler_params=_params(has_side_effects=True),
    )(v)


def _sum_slots(g, name):
    n, rows, _ = g.shape
    tm = _tile(rows, 512)

    def body(g_ref, out_ref):
        acc = g_ref[0]
        for k in range(1, n):
            acc = acc + g_ref[k]
        out_ref[...] = acc

    return pl.pallas_call(
        body, name=name, grid=(rows // tm,),
        in_specs=[pl.BlockSpec((n, tm, LANES), lambda i: (0, i, 0))],
        out_specs=pl.BlockSpec((tm, LANES), lambda i: (i, 0)),
        out_shape=jax.ShapeDtypeStruct((rows, LANES), F32),
        compiler_params=_params(("parallel",)),
    )(g)


_SMALL = ("ln_mix_g", "ret_log_gamma", "ssm_a_re", "ssm_a_im", "ssm_log_dt", "ssm_b_re", "ssm_b_im",
          "ssm_c_re", "ssm_c_im", "ssm_d", "b_glu", "ln_ffn_g", "ln_final_g")
_FLAT_ALIGN = LANES * LANES


def _flatten_small(d):
    parts = []
    for n in _SMALL:
        f = d[n].reshape(-1)
        parts.append(jnp.pad(f, (0, (-f.shape[0]) % _FLAT_ALIGN)))
    return jnp.concatenate(parts).reshape(-1, LANES)


def _unflatten_small(flat, like):
    flat = flat.reshape(-1)
    out, off = {}, 0
    for n in _SMALL:
        size = math.prod(like[n].shape)
        out[n] = flat[off:off + size].reshape(like[n].shape)
        off += size + (-size) % _FLAT_ALIGN
    return out


_BIG = ("w_in", "w_glu", "w_out", "w_ffn_gate", "w_ffn_up", "w_ffn_down")
_WEIGHTS = ("ln_mix_g", "w_in", "ret_log_gamma", "ssm_a_re", "ssm_a_im", "ssm_log_dt", "ssm_b_re", "ssm_b_im",
            "ssm_c_re", "ssm_c_im", "ssm_d", "w_glu", "b_glu", "w_out", "ln_ffn_g", "w_ffn_gate", "w_ffn_up",
            "w_ffn_down", "ln_final_g")


def _rope_tables(seq):
    half = QK_DIM // 2
    inv = 1.0 / (ROPE_BASE ** (jnp.arange(half, dtype=F32) / half))
    ang = jnp.arange(seq, dtype=F32)[:, None] * inv[None, :]
    return jnp.cos(ang), jnp.sin(ang)


def _step(w, m, v, x, target):
    seq = x.shape[0]
    seg_len = float(seq // SEGMENTS)
    c_idx = lax.axis_index("c").astype(jnp.int32)
    chip_idx = (2 * lax.axis_index("x") + lax.axis_index("y")).astype(jnp.int32)
    c_arr = jnp.stack([c_idx])
    idx_arr = jnp.stack([c_idx, chip_idx])

    shards = []
    for n in _BIG:
        d, r, cc = w[n].shape
        shards.append(_cast_bf16(w[n].reshape(d * r, cc), "cast_" + n).reshape(d, r, cc))
    wf = dict(zip([b[0] for b in _BUFFERS], _gather_weights(shards)))
    cos, sin = _rope_tables(seq)

    saved = []
    xc = x
    for i in range(DEPTH):
        t = "_l%d" % i
        s = {"x_in": xc}
        s["h"] = _rms_fwd(xc, w["ln_mix_g"][i:i + 1], "rms_mix" + t)
        s["proj"] = _matmul(s["h"], wf["in"][i], "nn", [F32], name="mm_in" + t)[0]
        s["qr"], s["kr"] = _rot_fwd(s["proj"], cos, sin, "rot" + t)
        s["lg"] = jnp.broadcast_to(w["ret_log_gamma"][i].T[:, :, None], (HEADS, 2, LANES))
        s["y"] = _ret_fwd(s["qr"], s["kr"], s["proj"], s["lg"], "ret" + t)
        s5_raw = (w["ssm_a_re"][i], w["ssm_a_im"][i], w["ssm_log_dt"][i], w["ssm_b_re"][i], w["ssm_b_im"][i])
        disc, s["disc_vjp"] = jax.vjp(functools.partial(_s5_discretize, seg_len=seg_len), *s5_raw)
        bblk, cblk, lam = _s5_pack(*disc, w["ssm_c_re"][i], w["ssm_c_im"][i])
        s["s5"] = (bblk.astype(BF16), cblk.astype(BF16), lam)
        s["s5y"] = _s5_fwd(s["proj"], *s["s5"], "s5" + t)
        s["ret"], s["ysg"], s["ysgb"] = _post1_fwd(s["y"], s["proj"], s["s5y"], w["ssm_d"][i:i + 1], "post" + t)
        s["z"] = _matmul(s["ysgb"], wf["glu"][i], "nn", [F32], name="mm_glu" + t)[0]
        s["merged"] = _merge_fwd(s["z"], s["ysg"], s["proj"], s["ret"], w["b_glu"][i:i + 1], "merge" + t)
        s["x1"] = _matmul(s["merged"], wf["out"][i], "nn", [F32], add=xc, name="mm_out" + t)[0]
        s["h2"] = _rms_fwd(s["x1"], w["ln_ffn_g"][i:i + 1], "rms_ffn" + t)
        s["ab"] = _matmul(s["h2"], wf["gu"][i], "nn", [F32], name="mm_gu" + t)[0]
        s["f"] = _glu_fwd(s["ab"], "glu" + t)
        xc = _matmul(s["f"], wf["down"][i], "nn", [F32], add=s["x1"], name="mm_down" + t)[0]
        saved.append(s)

    dx, dxb, loss_row, dg_final = _loss_stage(xc, target, w["ln_final_g"][None, :], "loss")
    loss = lax.psum(loss_row[0, 0], ("x", "y", "c"))

    g_small = {"ln_final_g": dg_final[0]}
    per_layer = {n: [None] * DEPTH for n in _SMALL if n != "ln_final_g"}
    dws = {b[0]: None for b in _BUFFERS}

    def dw_mm(a, b, buf, i, name):
        dws[buf] = _matmul(a, b, "tn", [F32, BF16], stack=dws[buf] or (None, None), layer=i, name=name)

    for i in reversed(range(DEPTH)):
        t = "_l%d" % i
        s = saved[i]
        dw_mm(s["f"], dxb, "down", i, "dw_down" + t)
        df = _matmul(dxb, wf["down"][i], "nt", [F32], name="dx_down" + t)[0]
        dab = _glu_bwd(s["ab"], df, "glu_bwd" + t)
        dw_mm(s["h2"], dab, "gu", i, "dw_gu" + t)
        dh2 = _matmul(dab, wf["gu"][i], "nt", [F32], name="dx_gu" + t)[0]
        dx1, dx1b, dg = _rms_bwd(s["x1"], dh2, dx, w["ln_ffn_g"][i:i + 1], "rms_ffn_bwd" + t)
        per_layer["ln_ffn_g"][i] = dg[0]

        dw_mm(s["merged"], dx1b, "out", i, "dw_out" + t)
        dmerged = _matmul(dx1b, wf["out"][i], "nt", [F32], name="dx_out" + t)[0]
        dz, dys_part, dgs, db = _merge_bwd(s["z"], s["ysg"], s["proj"], s["ret"], dmerged, w["b_glu"][i:i + 1],
                                           "merge_bwd" + t)
        per_layer["b_glu"][i] = db[0]
        dw_mm(s["ysgb"], dz, "glu", i, "dw_glu" + t)
        dys = _matmul(dz, wf["glu"][i], "nt", [F32], add=dys_part, name="dx_glu" + t)[0]
        dy, dgg, dgr, ds5, du_part, dd = _post1_bwd(s["y"], s["proj"], s["s5y"], dmerged, dys,
                                                    w["ssm_d"][i:i + 1], "post_bwd" + t)
        per_layer["ssm_d"][i] = dd[0]
        du, dbblk, dcblk, dlam = _s5_bwd(s["proj"], ds5, du_part, *s["s5"], "s5_bwd" + t)
        dlr, dli, dbr, dbi, dcr, dci = _s5_unpack(dbblk, dcblk, dlam)
        zeros = jnp.zeros_like(dlr)
        da_re, da_im, dlog_dt, db_re, db_im = s["disc_vjp"]((dlr, dli, zeros, zeros, dbr, dbi))
        for n, val in (("ssm_a_re", da_re), ("ssm_a_im", da_im), ("ssm_log_dt", dlog_dt), ("ssm_b_re", db_re),
                       ("ssm_b_im", db_im), ("ssm_c_re", dcr), ("ssm_c_im", dci)):
            per_layer[n][i] = val
        dqr, dkr, dv, dlg = _ret_bwd(s["qr"], s["kr"], s["proj"], dy, s["lg"], "ret_bwd" + t)
        per_layer["ret_log_gamma"][i] = dlg[:, :, 0].T
        dqkv = _rot_bwd(dqr, dkr, dv, cos, sin, "rot_bwd" + t)
        dproj = jnp.concatenate([dqkv, dgg, du, dgr, dgs], axis=1)
        dw_mm(s["h"], dproj, "in", i, "dw_in" + t)
        dh = _matmul(dproj, wf["in"][i], "nt", [F32], name="dx_in" + t)[0]
        dx, dxb, dg = _rms_bwd(s["x_in"], dh, dx1, w["ln_mix_g"][i:i + 1], "rms_mix_bwd" + t)
        per_layer["ln_mix_g"][i] = dg[0]

    got = _swap_other_layer([dws[b[0]][1] for b in _BUFFERS])
    partials = [_chip_partial(dws[b[0]][0], got[k], c_arr, "chip_partial_" + b[0]) for k, b in enumerate(_BUFFERS)]
    recv = _scatter_partials(partials)
    reduced = []
    for p, piece in enumerate(_PIECES):
        k = _BUF_INDEX[piece[1]]
        reduced.append(_reduce_shard(piece, dws[piece[1]][0], got[k], recv[p], idx_arr, "reduce_" + piece[0]))
    g_big = dict(zip([p[0] for p in _PIECES], _share_with_sibling(reduced)))

    for n in per_layer:
        g_small[n] = jnp.stack(per_layer[n])
    g_flat = _sum_slots(_all_gather_rows(_flatten_small(g_small)), "sum_small")

    grads, delta, new_m, new_v = {}, {}, {}, {}
    for n in _BIG:
        d, r, cc = w[n].shape
        two_d = lambda a: a.reshape(d * r, cc)
        dl, mn, vn = _adamw(two_d(w[n]), two_d(g_big[n]), two_d(m[n]), two_d(v[n]), "adamw_" + n)
        grads[n], delta[n], new_m[n], new_v[n] = g_big[n], dl.reshape(d, r, cc), mn.reshape(d, r, cc), vn.reshape(d, r, cc)
    dl, mn, vn = _adamw(_flatten_small(w), g_flat, _flatten_small(m), _flatten_small(v), "adamw_small")
    for dst, flat in ((grads, g_flat), (delta, dl), (new_m, mn), (new_v, vn)):
        dst.update(_unflatten_small(flat, w))
    return loss, dx, grads, delta, new_m, new_v


def kernel(x, ln_mix_g, w_in, ret_log_gamma, ssm_a_re, ssm_a_im, ssm_log_dt, ssm_b_re, ssm_b_im, ssm_c_re, ssm_c_im, ssm_d, w_glu, b_glu, w_out, ln_ffn_g, w_ffn_gate, w_ffn_up, w_ffn_down, ln_final_g, loss_target, m_ln_mix_g, m_w_in, m_ret_log_gamma, m_ssm_a_re, m_ssm_a_im, m_ssm_log_dt, m_ssm_b_re, m_ssm_b_im, m_ssm_c_re, m_ssm_c_im, m_ssm_d, m_w_glu, m_b_glu, m_w_out, m_ln_ffn_g, m_w_ffn_gate, m_w_ffn_up, m_w_ffn_down, m_ln_final_g, v_ln_mix_g, v_w_in, v_ret_log_gamma, v_ssm_a_re, v_ssm_a_im, v_ssm_log_dt, v_ssm_b_re, v_ssm_b_im, v_ssm_c_re, v_ssm_c_im, v_ssm_d, v_w_glu, v_b_glu, v_w_out, v_ln_ffn_g, v_w_ffn_gate, v_w_ffn_up, v_w_ffn_down, v_ln_final_g):
    given = dict(locals())
    w = {n: given[n] for n in _WEIGHTS}
    m = {n: given["m_" + n] for n in _WEIGHTS}
    v = {n: given["v_" + n] for n in _WEIGHTS}
    loss, dx, grads, delta, new_m, new_v = _step(w, m, v, x[0], loss_target[0])
    return (loss, dx[None], *[grads[n] for n in _WEIGHTS], *[delta[n] for n in _WEIGHTS],
            *[new_m[n] for n in _WEIGHTS], *[new_v[n] for n in _WEIGHTS])
```

```python
import functools
import math

import jax
import jax.numpy as jnp
from jax import lax
from jax.experimental import pallas as pl
from jax.experimental.pallas import tpu as pltpu

F32 = jnp.float32
BF16 = jnp.bfloat16

D_MODEL = 2048
DEPTH = 2
HEADS = 4
QK_DIM = 256
V_DIM = 512
QK_WIDTH = HEADS * QK_DIM
ROPE_BASE = 10000.0
GROUP = 16
N_GROUPS = D_MODEL // GROUP
N_STATE = 64
D_FF = 5632
IN_WIDTH = 2 * QK_WIDTH + 5 * D_MODEL
EPS = 1e-6
N_CHIPS = 4

ADAM_LR = 0.001
ADAM_B1 = 0.9
ADAM_B2 = 0.999
ADAM_EPS = 1e-08
ADAM_WD = 0.01
ADAM_STEP = 10

LANES = 128
SUBLANES = 8
VMEM_LIMIT = 56 * 1024 * 1024
SEGMENTS = SUBLANES
GROUPS_PER_TILE = LANES // GROUP
STATE_COLS = GROUPS_PER_TILE * N_STATE
N_TILES = D_MODEL // LANES

MESH = pl.DeviceIdType.MESH
HBM_SPEC = pl.BlockSpec(memory_space=pltpu.HBM)


def _params(sem=None, **kw):
    return pltpu.CompilerParams(dimension_semantics=sem, vmem_limit_bytes=VMEM_LIMIT, **kw)


def _tile(n, cap=1024):
    for t in (1024, 512, 256, 128):
        if t <= cap and n % t == 0:
            return t
    raise ValueError(n)


def _rows_call(fn, rows, pars, row_outs, par_outs, *, tm, name):
    m = rows[0][0].shape[0]
    nr, npar, nro, npo = len(rows), len(pars), len(row_outs), len(par_outs)

    def body(*refs):
        rin = refs[:nr]
        pin = refs[nr:nr + npar]
        rout = refs[nr + npar:nr + npar + nro]
        pout = refs[nr + npar + nro:]
        res = fn(*[r[...] for r in rin], *[p[...] for p in pin])
        if not isinstance(res, (tuple, list)):
            res = (res,)
        for r, v in zip(rout, res[:nro]):
            r[...] = v.astype(r.dtype)
        if npo:
            @pl.when(pl.program_id(0) == 0)
            def _():
                for p in pout:
                    p[...] = jnp.zeros(p.shape, p.dtype)
            for p, v in zip(pout, res[nro:]):
                p[...] += v

    in_specs = [pl.BlockSpec((tm, w), functools.partial(lambda cb, i: (i, cb), cb)) for (_, w, cb) in rows]
    in_specs += [pl.BlockSpec(p.shape, lambda i: (0, 0)) for p in pars]
    out_specs = [pl.BlockSpec((tm, w), lambda i: (i, 0)) for (w, _) in row_outs]
    out_specs += [pl.BlockSpec(s, lambda i: (0, 0)) for s in par_outs]
    out_shape = [jax.ShapeDtypeStruct((m, w), dt) for (w, dt) in row_outs]
    out_shape += [jax.ShapeDtypeStruct(s, F32) for s in par_outs]
    res = pl.pallas_call(
        body, name=name, grid=(m // tm,), in_specs=in_specs, out_specs=out_specs, out_shape=out_shape,
        compiler_params=_params(("arbitrary",) if npo else ("parallel",)),
    )(*[a for (a, _, _) in rows], *pars)
    return res


def _f32(*vals):
    return [v.astype(F32) for v in vals]


def _f_rms(x, g):
    r = lax.rsqrt(jnp.mean(x * x, axis=-1, keepdims=True) + EPS)
    return x * r * g


def _rms_fwd(x, g, name):
    return _rows_call(lambda xv, gv: _f_rms(xv, gv), [(x, D_MODEL, 0)], [g], [(D_MODEL, BF16)], [],
                      tm=256, name=name)[0]


def _rms_bwd(x, dh, dres, g, name):
    def fn(xv, dhv, drv, gv):
        _, vjp = jax.vjp(_f_rms, xv, gv)
        dx, dg = vjp(dhv)
        dx = dx + drv
        return dx, dx, dg
    return _rows_call(fn, [(x, D_MODEL, 0), (dh, D_MODEL, 0), (dres, D_MODEL, 0)], [g],
                      [(D_MODEL, F32), (D_MODEL, BF16)], [(1, D_MODEL)], tm=256, name=name)


def _rot_heads(xv, cos, sin, scale):
    half = QK_DIM // 2
    outs = []
    for h in range(HEADS):
        x1 = xv[:, h * QK_DIM:h * QK_DIM + half]
        x2 = xv[:, h * QK_DIM + half:(h + 1) * QK_DIM]
        outs += [(x1 * cos - x2 * sin) * scale, (x1 * sin + x2 * cos) * scale]
    return jnp.concatenate(outs, axis=1)


def _rot_fwd(proj, cos, sin, name):
    def fn(q, k, cv, sv):
        return _rot_heads(q, cv, sv, 1.0), _rot_heads(k, cv, sv, QK_DIM ** -0.5)
    return _rows_call(fn, [(proj, QK_WIDTH, 0), (proj, QK_WIDTH, 1), (cos, LANES, 0), (sin, LANES, 0)], [],
                      [(QK_WIDTH, BF16), (QK_WIDTH, BF16)], [], tm=256, name=name)


def _rot_bwd(dqr, dkr, dv, cos, sin, name):
    def fn(dq, dk, dvv, cv, sv):
        return jnp.concatenate([_rot_heads(dq, cv, -sv, 1.0), _rot_heads(dk, cv, -sv, QK_DIM ** -0.5), dvv], axis=1)
    return _rows_call(fn, [(dqr, QK_WIDTH, 0), (dkr, QK_WIDTH, 0), (dv, D_MODEL, 0), (cos, LANES, 0), (sin, LANES, 0)],
                      [], [(2 * QK_WIDTH + D_MODEL, BF16)], [], tm=256, name=name)[0]


def _f_post1(y0, y1, y2, y3, g, gr, s5, u, dsk):
    yn = [yh * lax.rsqrt(jnp.mean(yh * yh, axis=-1, keepdims=True) + EPS) for yh in (y0, y1, y2, y3)]
    ret = jax.nn.sigmoid(gr) * (jax.nn.silu(g) * jnp.concatenate(yn, axis=1))
    ysg = jax.nn.gelu(s5 + dsk * u)
    return ret, ysg


def _post1_rows(y, proj, s5y):
    rows = [(y, V_DIM, h) for h in range(HEADS)]
    rows += [(proj, D_MODEL, 2), (proj, D_MODEL, 4), (s5y, D_MODEL, 0), (proj, D_MODEL, 3)]
    return rows


def _post1_fwd(y, proj, s5y, dsk, name):
    def fn(*vals):
        ret, ysg = _f_post1(*vals)
        return ret, ysg, ysg
    return _rows_call(fn, _post1_rows(y, proj, s5y), [dsk],
                      [(D_MODEL, F32), (D_MODEL, F32), (D_MODEL, BF16)], [], tm=128, name=name)


def _post1_bwd(y, proj, s5y, dret, dys, dsk, name):
    def fn(*vals):
        prim = vals[:8] + (vals[10],)
        _, vjp = jax.vjp(_f_post1, *prim)
        gy0, gy1, gy2, gy3, gg, ggr, gs5, gu, gd = vjp((vals[8], vals[9]))
        return jnp.concatenate([gy0, gy1, gy2, gy3], axis=1), gg, ggr, gs5, gu, gd
    rows = _post1_rows(y, proj, s5y) + [(dret, D_MODEL, 0), (dys, D_MODEL, 0)]
    return _rows_call(fn, rows, [dsk],
                      [(D_MODEL, BF16), (D_MODEL, BF16), (D_MODEL, BF16), (D_MODEL, F32), (D_MODEL, F32)],
                      [(1, D_MODEL)], tm=128, name=name)


def _f_merge(z, ysg, gs, ret, b):
    return ret + jax.nn.sigmoid(gs) * (ysg * jax.nn.sigmoid(z + b))


def _merge_fwd(z, ysg, proj, ret, b, name):
    return _rows_call(_f_merge, [(z, D_MODEL, 0), (ysg, D_MODEL, 0), (proj, D_MODEL, 5), (ret, D_MODEL, 0)], [b],
                      [(D_MODEL, BF16)], [], tm=128, name=name)[0]


def _merge_bwd(z, ysg, proj, ret, dm, b, name):
    def fn(zv, yv, gv, rv, dmv, bv):
        _, vjp = jax.vjp(_f_merge, zv, yv, gv, rv, bv)
        gz, gy, gg, _, gb = vjp(dmv)
        return gz, gy, gg, gb
    rows = [(z, D_MODEL, 0), (ysg, D_MODEL, 0), (proj, D_MODEL, 5), (ret, D_MODEL, 0), (dm, D_MODEL, 0)]
    return _rows_call(fn, rows, [b], [(D_MODEL, BF16), (D_MODEL, F32), (D_MODEL, BF16)], [(1, D_MODEL)],
                      tm=128, name=name)


def _f_glu(a, b):
    return jax.nn.silu(a) * b


def _glu_fwd(ab, name):
    return _rows_call(_f_glu, [(ab, D_FF, 0), (ab, D_FF, 1)], [], [(D_FF, BF16)], [], tm=128, name=name)[0]


def _glu_bwd(ab, df, name):
    def fn(a, b, d):
        _, vjp = jax.vjp(_f_glu, a, b)
        ga, gb = vjp(d)
        return jnp.concatenate([ga, gb], axis=1)
    return _rows_call(fn, [(ab, D_FF, 0), (ab, D_FF, 1), (df, D_FF, 0)], [], [(2 * D_FF, BF16)], [],
                      tm=128, name=name)[0]


def _loss_stage(x, tgt, g, name):
    def fn(xv, tv, gv):
        def lf(xx, gg):
            err = _f_rms(xx, gg) - tv
            row = jnp.mean(err * err, axis=-1, keepdims=True)
            return 0.5 * jnp.sum(row, axis=0, keepdims=True)
        l, vjp = jax.vjp(lf, xv, gv)
        dx, dg = vjp(jnp.ones((1, 1), F32))
        return dx, dx, jnp.broadcast_to(l, (1, LANES)), dg
    return _rows_call(fn, [(x, D_MODEL, 0), (tgt, D_MODEL, 0)], [g], [(D_MODEL, F32), (D_MODEL, BF16)],
                      [(1, LANES), (1, D_MODEL)], tm=256, name=name)


def _adamw(w, g, m, v, name):
    rows, cols = w.shape

    def fn(wv, gv, mv, vv):
        mn = ADAM_B1 * mv + (1.0 - ADAM_B1) * gv
        vn = ADAM_B2 * vv + (1.0 - ADAM_B2) * (gv * gv)
        m_hat = mn / (1.0 - ADAM_B1 ** ADAM_STEP)
        v_hat = vn / (1.0 - ADAM_B2 ** ADAM_STEP)
        delta = -ADAM_LR * (m_hat / (jnp.sqrt(v_hat) + ADAM_EPS) + ADAM_WD * wv)
        return delta, mn, vn
    tm = _tile(rows, 128 if cols > D_FF // N_CHIPS else (256 if cols > LANES else 512))
    return _rows_call(fn, [(w, cols, 0), (g, cols, 0), (m, cols, 0), (v, cols, 0)], [],
                      [(cols, F32)] * 3, [], tm=tm, name=name)


def _matmul(a, b, mode, out_dtypes, *, name, add=None, stack=None, layer=None):
    if mode == "nn":
        (m, k), (_, n) = a.shape, b.shape
    elif mode == "nt":
        (m, k), (n, _) = a.shape, b.shape
    else:
        (k, m), (_, n) = a.shape, b.shape
    tm, tn, tk = _tile(m), _tile(n), _tile(k)
    nk = k // tk
    if mode == "nn":
        a_spec = pl.BlockSpec((tm, tk), lambda i, j, kk: (i, kk))
        b_spec = pl.BlockSpec((tk, tn), lambda i, j, kk: (kk, j))
        dims = (((1,), (0,)), ((), ()))
    elif mode == "nt":
        a_spec = pl.BlockSpec((tm, tk), lambda i, j, kk: (i, kk))
        b_spec = pl.BlockSpec((tn, tk), lambda i, j, kk: (j, kk))
        dims = (((1,), (1,)), ((), ()))
    else:
        a_spec = pl.BlockSpec((tk, tm), lambda i, j, kk: (kk, i))
        b_spec = pl.BlockSpec((tk, tn), lambda i, j, kk: (kk, j))
        dims = (((0,), (0,)), ((), ()))
    n_out = len(out_dtypes)
    has_add = add is not None
    n_alias = n_out if (stack is not None and stack[0] is not None) else 0

    def body(*refs):
        a_ref, b_ref = refs[0], refs[1]
        add_ref = refs[2] if has_add else None
        outs = refs[2 + has_add + n_alias:2 + has_add + n_alias + n_out]
        acc = refs[-1]
        kk = pl.program_id(2)

        @pl.when(kk == 0)
        def _():
            acc[...] = jnp.zeros(acc.shape, F32)

        acc[...] += lax.dot_general(a_ref[...], b_ref[...], dims, preferred_element_type=F32)

        @pl.when(kk == nk - 1)
        def _():
            r = acc[...]
            if has_add:
                r = r + add_ref[...]
            for o in outs:
                o[...] = r.astype(o.dtype)

    in_specs = [a_spec, b_spec]
    args = [a, b]
    if has_add:
        in_specs.append(pl.BlockSpec((tm, tn), lambda i, j, kk: (i, j)))
        args.append(add)
    aliases = {}
    if stack is None:
        out_specs = [pl.BlockSpec((tm, tn), lambda i, j, kk: (i, j))] * n_out
        out_shape = [jax.ShapeDtypeStruct((m, n), dt) for dt in out_dtypes]
    else:
        out_specs = [pl.BlockSpec((None, tm, tn), functools.partial(lambda ly, i, j, kk: (ly, i, j), layer))] * n_out
        out_shape = [jax.ShapeDtypeStruct((DEPTH, m, n), dt) for dt in out_dtypes]
        if n_alias:
            for t, buf in enumerate(stack):
                aliases[len(args)] = t
                in_specs.append(pl.BlockSpec(memory_space=pl.ANY))
                args.append(buf)
    return pl.pallas_call(
        body, name=name, grid=(m // tm, n // tn, nk), in_specs=in_specs, out_specs=out_specs, out_shape=out_shape,
        scratch_shapes=[pltpu.VMEM((tm, tn), F32)], input_output_aliases=aliases,
        compiler_params=_params(("parallel", "parallel", "arbitrary")),
    )(*args)


RET_TQ = 512


def _decay(lg_ref, i, tq, seq):
    n_idx = i * tq + lax.broadcasted_iota(jnp.int32, (tq, seq), 0)
    m_idx = lax.broadcasted_iota(jnp.int32, (tq, seq), 1)
    diff = (n_idx - m_idx).astype(F32)
    lgf = lg_ref[0, 0:1, 0:1]
    lgb = lg_ref[0, 1:2, 0:1]
    causal = diff >= 0
    return jnp.exp(jnp.where(causal, lgf * diff, -lgb * diff)), diff, causal


_NT = (((1,), (1,)), ((), ()))
_TN = (((0,), (0,)), ((), ()))


def _ret_fwd(qr, kr, proj, lg, name):
    seq = qr.shape[0]
    tq = RET_TQ
    v_blk0 = (2 * QK_WIDTH) // V_DIM

    def body(q_ref, k_ref, v_ref, lg_ref, y_ref):
        i = pl.program_id(1)
        s = lax.dot_general(q_ref[...], k_ref[...], _NT, preferred_element_type=F32)
        dm, _, _ = _decay(lg_ref, i, tq, seq)
        p = (s * dm).astype(BF16)
        y_ref[...] = jnp.dot(p, v_ref[...].astype(BF16), preferred_element_type=F32)

    return pl.pallas_call(
        body, name=name, grid=(HEADS, seq // tq),
        in_specs=[pl.BlockSpec((tq, QK_DIM), lambda h, i: (i, h)),
                  pl.BlockSpec((seq, QK_DIM), lambda h, i: (0, h)),
                  pl.BlockSpec((seq, V_DIM), lambda h, i: (0, v_blk0 + h)),
                  pl.BlockSpec((1, 2, LANES), lambda h, i: (h, 0, 0))],
        out_specs=pl.BlockSpec((tq, V_DIM), lambda h, i: (i, h)),
        out_shape=jax.ShapeDtypeStruct((seq, HEADS * V_DIM), F32),
        compiler_params=_params(("parallel", "parallel")),
    )(qr, kr, proj, lg)


def _ret_bwd(qr, kr, proj, dy, lg, name):
    seq = qr.shape[0]
    tq = RET_TQ
    v_blk0 = (2 * QK_WIDTH) // V_DIM

    def body(q_ref, k_ref, v_ref, dy_ref, lg_ref, dq_ref, dk_ref, dv_ref, dlg_ref):
        i = pl.program_id(1)

        @pl.when(i == 0)
        def _():
            dk_ref[...] = jnp.zeros(dk_ref.shape, F32)
            dv_ref[...] = jnp.zeros(dv_ref.shape, F32)
            dlg_ref[...] = jnp.zeros(dlg_ref.shape, F32)

        q = q_ref[...]
        k = k_ref[...]
        vb = v_ref[...].astype(BF16)
        dyb = dy_ref[...]
        s = lax.dot_general(q, k, _NT, preferred_element_type=F32)
        dm, diff, causal = _decay(lg_ref, i, tq, seq)
        p = s * dm
        dp = lax.dot_general(dyb, vb, _NT, preferred_element_type=F32)
        dv_ref[...] += lax.dot_general(p.astype(BF16), dyb, _TN, preferred_element_type=F32)
        ds = (dp * dm).astype(BF16)
        dq_ref[...] = jnp.dot(ds, k, preferred_element_type=F32)
        dk_ref[...] += lax.dot_general(ds, q, _TN, preferred_element_type=F32)
        gd = dp * p * diff
        dlf = jnp.sum(jnp.sum(jnp.where(causal, gd, 0.0), axis=1, keepdims=True), axis=0, keepdims=True)
        dlb = jnp.sum(jnp.sum(jnp.where(causal, 0.0, -gd), axis=1, keepdims=True), axis=0, keepdims=True)
        row = lax.broadcasted_iota(jnp.int32, (2, LANES), 0)
        dlg_ref[0] += jnp.where(row == 0, dlf, dlb)

    return pl.pallas_call(
        body, name=name, grid=(HEADS, seq // tq),
        in_specs=[pl.BlockSpec((tq, QK_DIM), lambda h, i: (i, h)),
                  pl.BlockSpec((seq, QK_DIM), lambda h, i: (0, h)),
                  pl.BlockSpec((seq, V_DIM), lambda h, i: (0, v_blk0 + h)),
                  pl.BlockSpec((tq, V_DIM), lambda h, i: (i, h)),
                  pl.BlockSpec((1, 2, LANES), lambda h, i: (h, 0, 0))],
        out_specs=[pl.BlockSpec((tq, QK_DIM), lambda h, i: (i, h)),
                   pl.BlockSpec((seq, QK_DIM), lambda h, i: (0, h)),
                   pl.BlockSpec((seq, V_DIM), lambda h, i: (0, h)),
                   pl.BlockSpec((1, 2, LANES), lambda h, i: (h, 0, 0))],
        out_shape=[jax.ShapeDtypeStruct((seq, QK_WIDTH), F32), jax.ShapeDtypeStruct((seq, QK_WIDTH), F32),
                   jax.ShapeDtypeStruct((seq, HEADS * V_DIM), F32), jax.ShapeDtypeStruct((HEADS, 2, LANES), F32)],
        compiler_params=_params(("parallel", "arbitrary")),
    )(qr, kr, proj, dy, lg)


def _shift_rows(v, reverse):
    row = lax.broadcasted_iota(jnp.int32, v.shape, 0)
    if reverse:
        return jnp.where(row == SEGMENTS - 1, 0.0, pltpu.roll(v, SEGMENTS - 1, 0))
    return jnp.where(row == 0, 0.0, pltpu.roll(v, 1, 0))


def _scan(xr_ref, xi_ref, lam, reverse, conj):
    steps = xr_ref.shape[0] // SEGMENTS
    cols = xr_ref.shape[1]
    sign = -1.0 if conj else 1.0
    lr = jnp.broadcast_to(lam[0], (SEGMENTS, cols))
    li = jnp.broadcast_to(lam[1], (SEGMENTS, cols)) * sign
    lrt = jnp.broadcast_to(lam[2], (SEGMENTS, cols))
    lit = jnp.broadcast_to(lam[3], (SEGMENTS, cols)) * sign
    zero = jnp.zeros((SEGMENTS, cols), F32)

    def rows_of(t):
        tt = steps - 1 - t if reverse else t
        return pl.ds(pl.multiple_of(tt * SEGMENTS, SEGMENTS), SEGMENTS)

    def step(t, carry):
        sr, si = carry
        rows = rows_of(t)
        nr = lr * sr - li * si + xr_ref[rows, :]
        ni = lr * si + li * sr + xi_ref[rows, :]
        xr_ref[rows, :] = nr
        xi_ref[rows, :] = ni
        return nr, ni

    er, ei = lax.fori_loop(0, steps, step, (zero, zero))
    cr, ci = zero, zero
    for _ in range(SEGMENTS - 1):
        tr = er + lrt * cr - lit * ci
        ti = ei + lrt * ci + lit * cr
        cr, ci = _shift_rows(tr, reverse), _shift_rows(ti, reverse)

    def fix(t, carry):
        pr, pi = carry
        rows = rows_of(t)
        xr_ref[rows, :] += pr * cr - pi * ci
        xi_ref[rows, :] += pr * ci + pi * cr
        return pr * lr - pi * li, pr * li + pi * lr

    lax.fori_loop(0, steps, fix, (lr, li))


def _permute_in(dst_ref, src_ref):
    steps = src_ref.shape[0] // SEGMENTS
    for s in range(SEGMENTS):
        dst_ref[pl.ds(s, steps, stride=SEGMENTS), :] = src_ref[s * steps:(s + 1) * steps, :].astype(dst_ref.dtype)


def _unpermute(src_ref, s):
    steps = src_ref.shape[0] // SEGMENTS
    return src_ref[pl.ds(s, steps, stride=SEGMENTS), :]


def _s5_fwd(proj, bblk, cblk, lam, name):
    seq = proj.shape[0]
    u_blk0 = (2 * QK_WIDTH + 2 * D_MODEL) // LANES
    sc = STATE_COLS

    def body(u_ref, b_ref, c_ref, lam_ref, y_ref, up_ref, yp_ref, xr_ref, xi_ref):
        _permute_in(up_ref, u_ref)
        ub = up_ref[...].astype(BF16)
        for d in range(2):
            xr_ref[...] = jnp.dot(ub, b_ref[d, :, 0:sc], preferred_element_type=F32)
            xi_ref[...] = jnp.dot(ub, b_ref[d, :, sc:2 * sc], preferred_element_type=F32)
            lm = [lam_ref[d, r:r + 1, :] for r in range(4)]
            _scan(xr_ref, xi_ref, lm, reverse=(d == 1), conj=False)
            yd = (jnp.dot(xr_ref[...].astype(BF16), c_ref[d, 0:sc, :], preferred_element_type=F32)
                  + jnp.dot(xi_ref[...].astype(BF16), c_ref[d, sc:2 * sc, :], preferred_element_type=F32))
            if d == 0:
                yp_ref[...] = yd
            else:
                yp_ref[...] += yd
        steps = seq // SEGMENTS
        for s in range(SEGMENTS):
            y_ref[s * steps:(s + 1) * steps, :] = _unpermute(yp_ref, s)

    return pl.pallas_call(
        body, name=name, grid=(N_TILES,),
        in_specs=[pl.BlockSpec((seq, LANES), lambda j: (0, u_blk0 + j)),
                  pl.BlockSpec((2, None, LANES, 2 * sc), lambda j: (0, j, 0, 0)),
                  pl.BlockSpec((2, None, 2 * sc, LANES), lambda j: (0, j, 0, 0)),
                  pl.BlockSpec((2, None, 4, sc), lambda j: (0, j, 0, 0))],
        out_specs=pl.BlockSpec((seq, LANES), lambda j: (0, j)),
        out_shape=jax.ShapeDtypeStruct((seq, D_MODEL), F32),
        scratch_shapes=[pltpu.VMEM((seq, LANES), F32), pltpu.VMEM((seq, LANES), F32),
                        pltpu.VMEM((seq, sc), F32), pltpu.VMEM((seq, sc), F32)],
        compiler_params=_params(("parallel",)),
    )(proj, bblk, cblk, lam)


def _s5_bwd(proj, dy, du_part, bblk, cblk, lam, name):
    seq = proj.shape[0]
    u_blk0 = (2 * QK_WIDTH + 2 * D_MODEL) // LANES
    sc = STATE_COLS
    steps = seq // SEGMENTS

    def body(u_ref, dy_ref, dup_ref, b_ref, c_ref, lam_ref, du_ref, db_ref, dc_ref, dlam_ref,
             up_ref, dyp_ref, dua_ref, xr_ref, xi_ref, gr_ref, gi_ref):
        _permute_in(up_ref, u_ref)
        _permute_in(dyp_ref, dy_ref)
        ub = up_ref[...].astype(BF16)
        dyb = dyp_ref[...].astype(BF16)
        for d in range(2):
            reverse = d == 1
            xr_ref[...] = jnp.dot(ub, b_ref[d, :, 0:sc], preferred_element_type=F32)
            xi_ref[...] = jnp.dot(ub, b_ref[d, :, sc:2 * sc], preferred_element_type=F32)
            lm = [lam_ref[d, r:r + 1, :] for r in range(4)]
            _scan(xr_ref, xi_ref, lm, reverse=reverse, conj=False)
            xrb = xr_ref[...].astype(BF16)
            xib = xi_ref[...].astype(BF16)
            dc_ref[d, 0:sc, :] = lax.dot_general(xrb, dyb, _TN, preferred_element_type=F32)
            dc_ref[d, sc:2 * sc, :] = lax.dot_general(xib, dyb, _TN, preferred_element_type=F32)
            gr_ref[...] = lax.dot_general(dyb, c_ref[d, 0:sc, :], _NT, preferred_element_type=F32)
            gi_ref[...] = lax.dot_general(dyb, c_ref[d, sc:2 * sc, :], _NT, preferred_element_type=F32)
            _scan(gr_ref, gi_ref, lm, reverse=not reverse, conj=True)

            def prev_rows(t):
                tt = t + 1 if reverse else t - 1
                return pl.ds(pl.multiple_of(tt * SEGMENTS, SEGMENTS), SEGMENTS)

            def acc_step(t, carry):
                ar, ai = carry
                rows = pl.ds(pl.multiple_of(t * SEGMENTS, SEGMENTS), SEGMENTS)
                pr = xr_ref[prev_rows(t), :]
                pi = xi_ref[prev_rows(t), :]
                zr = gr_ref[rows, :]
                zi = gi_ref[rows, :]
                return ar + zr * pr + zi * pi, ai + zi * pr - zr * pi

            zero = jnp.zeros((SEGMENTS, sc), F32)
            if reverse:
                ar, ai = lax.fori_loop(0, steps - 1, acc_step, (zero, zero))
                edge = pl.ds((steps - 1) * SEGMENTS, SEGMENTS)
                pr = _shift_rows(xr_ref[0:SEGMENTS, :], True)
                pi = _shift_rows(xi_ref[0:SEGMENTS, :], True)
            else:
                ar, ai = lax.fori_loop(1, steps, acc_step, (zero, zero))
                edge = pl.ds(0, SEGMENTS)
                last = pl.ds((steps - 1) * SEGMENTS, SEGMENTS)
                pr = _shift_rows(xr_ref[last, :], False)
                pi = _shift_rows(xi_ref[last, :], False)
            zr = gr_ref[edge, :]
            zi = gi_ref[edge, :]
            ar = ar + zr * pr + zi * pi
            ai = ai + zi * pr - zr * pi
            dlam_ref[d, 0:1, :] = jnp.sum(ar, axis=0, keepdims=True)
            dlam_ref[d, 1:2, :] = jnp.sum(ai, axis=0, keepdims=True)

            grb = gr_ref[...].astype(BF16)
            gib = gi_ref[...].astype(BF16)
            db_ref[d, :, 0:sc] = lax.dot_general(ub, grb, _TN, preferred_element_type=F32)
            db_ref[d, :, sc:2 * sc] = lax.dot_general(ub, gib, _TN, preferred_element_type=F32)
            dud = (lax.dot_general(grb, b_ref[d, :, 0:sc], _NT, preferred_element_type=F32)
                   + lax.dot_general(gib, b_ref[d, :, sc:2 * sc], _NT, preferred_element_type=F32))
            if d == 0:
                dua_ref[...] = dud
            else:
                dua_ref[...] += dud
        for s in range(SEGMENTS):
            rows = slice(s * steps, (s + 1) * steps)
            du_ref[rows, :] = (_unpermute(dua_ref, s) + dup_ref[rows, :]).astype(du_ref.dtype)

    return pl.pallas_call(
        body, name=name, grid=(N_TILES,),
        in_specs=[pl.BlockSpec((seq, LANES), lambda j: (0, u_blk0 + j)),
                  pl.BlockSpec((seq, LANES), lambda j: (0, j)),
                  pl.BlockSpec((seq, LANES), lambda j: (0, j)),
                  pl.BlockSpec((2, None, LANES, 2 * sc), lambda j: (0, j, 0, 0)),
                  pl.BlockSpec((2, None, 2 * sc, LANES), lambda j: (0, j, 0, 0)),
                  pl.BlockSpec((2, None, 4, sc), lambda j: (0, j, 0, 0))],
        out_specs=[pl.BlockSpec((seq, LANES), lambda j: (0, j)),
                   pl.BlockSpec((2, None, LANES, 2 * sc), lambda j: (0, j, 0, 0)),
                   pl.BlockSpec((2, None, 2 * sc, LANES), lambda j: (0, j, 0, 0)),
                   pl.BlockSpec((2, None, 2, sc), lambda j: (0, j, 0, 0))],
        out_shape=[jax.ShapeDtypeStruct((seq, D_MODEL), BF16),
                   jax.ShapeDtypeStruct((2, N_TILES, LANES, 2 * sc), F32),
                   jax.ShapeDtypeStruct((2, N_TILES, 2 * sc, LANES), F32),
                   jax.ShapeDtypeStruct((2, N_TILES, 2, sc), F32)],
        scratch_shapes=[pltpu.VMEM((seq, LANES), F32), pltpu.VMEM((seq, LANES), F32), pltpu.VMEM((seq, LANES), F32),
                        pltpu.VMEM((seq, sc), F32), pltpu.VMEM((seq, sc), F32),
                        pltpu.VMEM((seq, sc), F32), pltpu.VMEM((seq, sc), F32)],
        compiler_params=_params(("parallel",)),
    )(proj, dy, du_part, bblk, cblk, lam)


def _s5_discretize(a_re, a_im, log_dt, b_re, b_im, seg_len):
    dt = jnp.exp(log_dt)[..., None]
    e = jnp.exp(a_re * dt)
    lr, li = e * jnp.cos(a_im * dt), e * jnp.sin(a_im * dt)
    et = jnp.exp(a_re * dt * seg_len)
    lrt, lit = et * jnp.cos(a_im * dt * seg_len), et * jnp.sin(a_im * dt * seg_len)
    den = a_re * a_re + a_im * a_im
    qr = ((lr - 1.0) * a_re + li * a_im) / den
    qi = (li * a_re - (lr - 1.0) * a_im) / den
    br = qr[..., None] * b_re - qi[..., None] * b_im
    bi = qr[..., None] * b_im + qi[..., None] * b_re
    return lr, li, lrt, lit, br, bi


def _s5_pack(lr, li, lrt, lit, br, bi, c_re, c_im):
    eye = jnp.eye(GROUPS_PER_TILE, dtype=F32)

    def bd_b(b):
        b5 = b.reshape(2, N_TILES, GROUPS_PER_TILE, N_STATE, GROUP)
        return jnp.einsum("dtgph,gk->dtghkp", b5, eye).reshape(2, N_TILES, LANES, STATE_COLS)

    def bd_c(c):
        c5 = c.reshape(2, N_TILES, GROUPS_PER_TILE, GROUP, N_STATE)
        return jnp.einsum("dtghp,gk->dtkpgh", c5, eye).reshape(2, N_TILES, STATE_COLS, LANES)

    bblk = jnp.concatenate([bd_b(br), bd_b(bi)], axis=3)
    cblk = jnp.concatenate([bd_c(c_re), -bd_c(c_im)], axis=2)
    lam = jnp.stack([v.reshape(2, N_TILES, STATE_COLS) for v in (lr, li, lrt, lit)], axis=2)
    return bblk, cblk, lam


def _s5_unpack(dbblk, dcblk, dlam):
    eye = jnp.eye(GROUPS_PER_TILE, dtype=F32)

    def diag_b(d):
        d6 = d.reshape(2, N_TILES, GROUPS_PER_TILE, GROUP, GROUPS_PER_TILE, N_STATE)
        return jnp.einsum("dtghkp,gk->dtgph", d6, eye).reshape(2, N_GROUPS, N_STATE, GROUP)

    def diag_c(d):
        d6 = d.reshape(2, N_TILES, GROUPS_PER_TILE, N_STATE, GROUPS_PER_TILE, GROUP)
        return jnp.einsum("dtkpgh,gk->dtghp", d6, eye).reshape(2, N_GROUPS, GROUP, N_STATE)

    dbr, dbi = diag_b(dbblk[..., :STATE_COLS]), diag_b(dbblk[..., STATE_COLS:])
    dcr, dci = diag_c(dcblk[:, :, :STATE_COLS, :]), -diag_c(dcblk[:, :, STATE_COLS:, :])
    dlr = dlam[:, :, 0, :].reshape(2, N_GROUPS, N_STATE)
    dli = dlam[:, :, 1, :].reshape(2, N_GROUPS, N_STATE)
    return dlr, dli, dbr, dbi, dcr, dci


def _pos():
    return lax.axis_index("x"), lax.axis_index("y"), lax.axis_index("c")


def _remote(src, dst, ssem, rsem, dev):
    return pltpu.make_async_remote_copy(src_ref=src, dst_ref=dst, send_sem=ssem, recv_sem=rsem,
                                        device_id=dev, device_id_type=MESH)


_PIECES = (
    ("w_in", "in", D_MODEL, IN_WIDTH // N_CHIPS, 0, IN_WIDTH // N_CHIPS, 0),
    ("w_glu", "glu", D_MODEL // N_CHIPS, D_MODEL, D_MODEL // N_CHIPS, 0, 0),
    ("w_out", "out", D_MODEL // N_CHIPS, D_MODEL, D_MODEL // N_CHIPS, 0, 0),
    ("w_ffn_gate", "gu", D_MODEL, D_FF // N_CHIPS, 0, D_FF // N_CHIPS, 0),
    ("w_ffn_up", "gu", D_MODEL, D_FF // N_CHIPS, 0, D_FF // N_CHIPS, D_FF),
    ("w_ffn_down", "down", D_FF // N_CHIPS, D_MODEL, D_FF // N_CHIPS, 0, 0),
)
_BUFFERS = (("in", D_MODEL, IN_WIDTH), ("glu", D_MODEL, D_MODEL), ("out", D_MODEL, D_MODEL),
            ("gu", D_MODEL, 2 * D_FF), ("down", D_FF, D_MODEL))
_BUF_INDEX = {name: t for t, (name, _, _) in enumerate(_BUFFERS)}
N_PIECES = len(_PIECES)
N_BUFFERS = len(_BUFFERS)


def _piece_view(ref, piece, j):
    _, _, rs, cs, rstep, cstep, coff = piece
    return ref.at[pl.ds(j * rstep, rs), pl.ds(coff + j * cstep, cs)]


def _own_block(piece, tm):
    _, _, _, cs, rstep, cstep, coff = piece
    return lambda i, chip: (i + chip * (rstep // tm), coff // cs + chip * (cstep // cs))


def _cast_place(piece, w3, prev, chip_arr, name):
    d, r, cc = w3.shape
    _, rf, cf = _BUFFERS[_BUF_INDEX[piece[1]]]
    tm = _tile(r, 256)
    own = _own_block(piece, tm)

    def body(s_ref, w_ref, *rest):
        rest[-1][...] = w_ref[...].astype(BF16)

    in_specs = [pl.BlockSpec((None, tm, cc), lambda l, i, s: (l, i, 0))]
    args = [w3]
    aliases = {}
    if prev is not None:
        in_specs.append(pl.BlockSpec(memory_space=pl.ANY))
        args.append(prev)
        aliases = {2: 0}
    return pl.pallas_call(
        body, name=name,
        grid_spec=pltpu.PrefetchScalarGridSpec(
            num_scalar_prefetch=1, grid=(d, r // tm), in_specs=in_specs,
            out_specs=pl.BlockSpec((None, tm, cc), lambda l, i, s: (l,) + own(i, s[0]))),
        out_shape=jax.ShapeDtypeStruct((d, rf, cf), BF16), input_output_aliases=aliases,
        compiler_params=_params(("parallel", "parallel")),
    )(chip_arr, *args)


def _gather_weights(placed):
    def body(*refs):
        full = refs[N_BUFFERS:2 * N_BUFFERS]
        ssem, rsem, fssem, frsem = refs[2 * N_BUFFERS:]
        x, y, c = _pos()
        my_chip = 2 * x + y

        for mine in range(N_CHIPS):
            @pl.when(my_chip == mine)
            def _(mine=mine):
                others = [j for j in range(N_CHIPS) if j != mine]

                def view(p, j, layer):
                    return _piece_view(full[_BUF_INDEX[_PIECES[p][1]]].at[layer], _PIECES[p], j)

                sends = []
                for p in range(N_PIECES):
                    for j in others:
                        cp = _remote(view(p, mine, c), view(p, mine, c), ssem.at[p * N_CHIPS + j],
                                     rsem.at[p * N_CHIPS + mine], (j // 2, j % 2, c))
                        cp.start()
                        sends.append(cp)
                for p in range(N_PIECES):
                    for j in others:
                        _remote(view(p, j, c), view(p, j, c), ssem.at[p * N_CHIPS + j],
                                rsem.at[p * N_CHIPS + j], (j // 2, j % 2, c)).wait_recv()
                        cp = _remote(view(p, j, c), view(p, j, c), fssem.at[p * N_CHIPS + j],
                                     frsem.at[p * N_CHIPS + j], (x, y, 1 - c))
                        cp.start()
                        sends.append(cp)
                for p in range(N_PIECES):
                    for j in others:
                        _remote(view(p, j, 1 - c), view(p, j, 1 - c), fssem.at[p * N_CHIPS + j],
                                frsem.at[p * N_CHIPS + j], (x, y, 1 - c)).wait_recv()
                for cp in sends:
                    cp.wait_send()

    nsem = N_PIECES * N_CHIPS
    return pl.pallas_call(
        body, name="gather_weights",
        in_specs=[HBM_SPEC] * N_BUFFERS, out_specs=[HBM_SPEC] * N_BUFFERS,
        out_shape=[jax.ShapeDtypeStruct((DEPTH, r, cc), BF16) for (_, r, cc) in _BUFFERS],
        input_output_aliases={t: t for t in range(N_BUFFERS)},
        scratch_shapes=[pltpu.SemaphoreType.DMA((nsem,)), pltpu.SemaphoreType.DMA((nsem,)),
                        pltpu.SemaphoreType.DMA((nsem,)), pltpu.SemaphoreType.DMA((nsem,))],
        compiler_params=_params(has_side_effects=True),
    )(*placed)


def _swap_other_layer(dwb):
    def body(*refs):
        src = refs[:N_BUFFERS]
        dst = refs[N_BUFFERS:2 * N_BUFFERS]
        ssem, rsem = refs[2 * N_BUFFERS:]
        x, y, c = _pos()
        cps = [_remote(src[t].at[1 - c], dst[t], ssem.at[t], rsem.at[t], (x, y, 1 - c)) for t in range(N_BUFFERS)]
        for cp in cps:
            cp.start()
        for cp in cps:
            cp.wait()

    return pl.pallas_call(
        body, name="swap_other_layer",
        in_specs=[HBM_SPEC] * N_BUFFERS, out_specs=[HBM_SPEC] * N_BUFFERS,
        out_shape=[jax.ShapeDtypeStruct((r, cc), BF16) for (_, r, cc) in _BUFFERS],
        scratch_shapes=[pltpu.SemaphoreType.DMA((N_BUFFERS,)), pltpu.SemaphoreType.DMA((N_BUFFERS,))],
        compiler_params=_params(has_side_effects=True),
    )(*dwb)


def _chip_partial(dw, got, c_idx, name):
    _, r, cc = dw.shape
    tm, tn = _tile(r, 512), _tile(cc)

    def body(c_ref, dw_ref, got_ref, out_ref):
        out_ref[...] = (dw_ref[...] + got_ref[...].astype(F32)).astype(BF16)

    return pl.pallas_call(
        body, name=name,
        grid_spec=pltpu.PrefetchScalarGridSpec(
            num_scalar_prefetch=1, grid=(r // tm, cc // tn),
            in_specs=[pl.BlockSpec((None, tm, tn), lambda i, j, cr: (cr[0], i, j)),
                      pl.BlockSpec((tm, tn), lambda i, j, cr: (i, j))],
            out_specs=pl.BlockSpec((tm, tn), lambda i, j, cr: (i, j))),
        out_shape=jax.ShapeDtypeStruct((r, cc), BF16),
        compiler_params=_params(("parallel", "parallel")),
    )(c_idx, dw, got)


def _scatter_partials(partials):
    def body(*refs):
        src = refs[:N_BUFFERS]
        dst = refs[N_BUFFERS:N_BUFFERS + N_PIECES]
        ssem, rsem = refs[N_BUFFERS + N_PIECES:]
        x, y, c = _pos()
        my_chip = 2 * x + y
        for mine in range(N_CHIPS):
            @pl.when(my_chip == mine)
            def _(mine=mine):
                others = [j for j in range(N_CHIPS) if j != mine]
                sends = []
                for p in range(N_PIECES):
                    buf = src[_BUF_INDEX[_PIECES[p][1]]]
                    for j in others:
                        cp = _remote(_piece_view(buf, _PIECES[p], j), dst[p].at[mine], ssem.at[p * N_CHIPS + j],
                                     rsem.at[p * N_CHIPS + mine], (j // 2, j % 2, c))
                        cp.start()
                        sends.append(cp)
                for p in range(N_PIECES):
                    for j in others:
                        _remote(dst[p].at[j], dst[p].at[j], ssem.at[p * N_CHIPS + j],
                                rsem.at[p * N_CHIPS + j], (j // 2, j % 2, c)).wait_recv()
                for cp in sends:
                    cp.wait_send()

    nsem = N_PIECES * N_CHIPS
    return pl.pallas_call(
        body, name="scatter_partials",
        in_specs=[HBM_SPEC] * N_BUFFERS, out_specs=[HBM_SPEC] * N_PIECES,
        out_shape=[jax.ShapeDtypeStruct((N_CHIPS, p[2], p[3]), BF16) for p in _PIECES],
        scratch_shapes=[pltpu.SemaphoreType.DMA((nsem,)), pltpu.SemaphoreType.DMA((nsem,))],
        compiler_params=_params(has_side_effects=True),
    )(*partials)


def _reduce_shard(piece, dw, got, recv, idx, name):
    _, _, rs, cs, _, _, _ = piece
    tm = _tile(rs, 256)
    own = _own_block(piece, tm)

    def own_map(i, s):
        return own(i, s[1])

    def body(s_ref, dw_ref, got_ref, r1, r2, r3, out_ref):
        acc = dw_ref[...] + got_ref[...].astype(F32)
        for r in (r1, r2, r3):
            acc = acc + r[...].astype(F32)
        out_ref[...] = acc

    def recv_map(k):
        return lambda i, s: ((s[1] + k) % N_CHIPS, i, 0)

    return pl.pallas_call(
        body, name=name,
        grid_spec=pltpu.PrefetchScalarGridSpec(
            num_scalar_prefetch=1, grid=(rs // tm,),
            in_specs=[pl.BlockSpec((None, tm, cs), lambda i, s: (s[0],) + own_map(i, s)),
                      pl.BlockSpec((tm, cs), own_map),
                      pl.BlockSpec((None, tm, cs), recv_map(1)),
                      pl.BlockSpec((None, tm, cs), recv_map(2)),
                      pl.BlockSpec((None, tm, cs), recv_map(3))],
            out_specs=pl.BlockSpec((None, tm, cs), lambda i, s: (s[0], i, 0))),
        out_shape=jax.ShapeDtypeStruct((DEPTH, rs, cs), F32),
        compiler_params=_params(("parallel",)),
    )(idx, dw, got, recv, recv, recv)


def _share_with_sibling(reduced):
    def body(*refs):
        buf = refs[N_PIECES:2 * N_PIECES]
        ssem, rsem = refs[2 * N_PIECES:]
        x, y, c = _pos()
        rem = [_remote(buf[p].at[c], buf[p].at[c], ssem.at[p], rsem.at[p], (x, y, 1 - c)) for p in range(N_PIECES)]
        for cp in rem:
            cp.start()
        for p in range(N_PIECES):
            rem[p].wait_send()
            _remote(buf[p].at[1 - c], buf[p].at[1 - c], ssem.at[p], rsem.at[p], (x, y, 1 - c)).wait_recv()

    return pl.pallas_call(
        body, name="share_with_sibling",
        in_specs=[HBM_SPEC] * N_PIECES, out_specs=[HBM_SPEC] * N_PIECES,
        out_shape=[jax.ShapeDtypeStruct((DEPTH, p[2], p[3]), F32) for p in _PIECES],
        input_output_aliases={t: t for t in range(N_PIECES)},
        scratch_shapes=[pltpu.SemaphoreType.DMA((N_PIECES,)), pltpu.SemaphoreType.DMA((N_PIECES,))],
        compiler_params=_params(has_side_effects=True),
    )(*reduced)


N_DEV = 8


def _all_gather_rows(v):
    rows = v.shape[0]

    def body(v_ref, out_ref, ssem, rsem, lsem):
        x, y, c = _pos()
        me, sibling = (x, y, c), (x, y, 1 - c)
        chips = [(1 - x, y), (x, 1 - y), (1 - x, 1 - y)]

        def slot(px, py, pc):
            return out_ref.at[4 * px + 2 * py + pc]

        def copy(k, block, to, src=None):
            return _remote(slot(*block) if src is None else src, slot(*block), ssem.at[k], rsem.at[k], to)

        mine = pltpu.make_async_copy(v_ref, slot(*me), lsem)
        mine.start()
        first = [copy(0, me, sibling, src=v_ref)]
        first += [copy(1 + j, me, (*chip, c), src=v_ref) for j, chip in enumerate(chips)]
        for cp in first:
            cp.start()
        passed = [copy(4 + j, (*chip, c), sibling) for j, chip in enumerate(chips)]
        for j, chip in enumerate(chips):
            copy(1 + j, (*chip, c), me).wait_recv()
            passed[j].start()
        copy(0, sibling, me).wait_recv()
        for j, chip in enumerate(chips):
            copy(4 + j, (*chip, 1 - c), me).wait_recv()
        for cp in first + passed:
            cp.wait_send()
        mine.wait()

    return pl.pallas_call(
        body, name="all_gather_small",
        in_specs=[HBM_SPEC], out_specs=HBM_SPEC,
        out_shape=jax.ShapeDtypeStruct((N_DEV, rows, LANES), F32),
        scratch_shapes=[pltpu.SemaphoreType.DMA((7,)), pltpu.SemaphoreType.DMA((7,)), pltpu.SemaphoreType.DMA],
        compiler_params=_params(has_side_effects=True),
    )(v)


def _sum_slots(g, name):
    n, rows, _ = g.shape
    tm = _tile(rows, 512)

    def body(g_ref, out_ref):
        acc = g_ref[0]
        for k in range(1, n):
            acc = acc + g_ref[k]
        out_ref[...] = acc

    return pl.pallas_call(
        body, name=name, grid=(rows // tm,),
        in_specs=[pl.BlockSpec((n, tm, LANES), lambda i: (0, i, 0))],
        out_specs=pl.BlockSpec((tm, LANES), lambda i: (i, 0)),
        out_shape=jax.ShapeDtypeStruct((rows, LANES), F32),
        compiler_params=_params(("parallel",)),
    )(g)


_SMALL = ("ln_mix_g", "ret_log_gamma", "ssm_a_re", "ssm_a_im", "ssm_log_dt", "ssm_b_re", "ssm_b_im",
          "ssm_c_re", "ssm_c_im", "ssm_d", "b_glu", "ln_ffn_g", "ln_final_g")
_FLAT_ALIGN = LANES * LANES


def _flatten_small(d):
    parts = []
    for n in _SMALL:
        f = d[n].reshape(-1)
        parts.append(jnp.pad(f, (0, (-f.shape[0]) % _FLAT_ALIGN)))
    return jnp.concatenate(parts).reshape(-1, LANES)


def _unflatten_small(flat, like):
    out, row = {}, 0
    for n in _SMALL:
        size = math.prod(like[n].shape)
        rows = (size + (-size) % _FLAT_ALIGN) // LANES
        part = lax.optimization_barrier(flat[row:row + rows])
        out[n] = part.reshape(-1)[:size].reshape(like[n].shape)
        row += rows
    return out


_BIG = ("w_in", "w_glu", "w_out", "w_ffn_gate", "w_ffn_up", "w_ffn_down")
_WEIGHTS = ("ln_mix_g", "w_in", "ret_log_gamma", "ssm_a_re", "ssm_a_im", "ssm_log_dt", "ssm_b_re", "ssm_b_im",
            "ssm_c_re", "ssm_c_im", "ssm_d", "w_glu", "b_glu", "w_out", "ln_ffn_g", "w_ffn_gate", "w_ffn_up",
            "w_ffn_down", "ln_final_g")


def _rope_tables(seq):
    half = QK_DIM // 2
    inv = 1.0 / (ROPE_BASE ** (jnp.arange(half, dtype=F32) / half))
    ang = jnp.arange(seq, dtype=F32)[:, None] * inv[None, :]
    return jnp.cos(ang), jnp.sin(ang)


def _step(w, m, v, x, target):
    seq = x.shape[0]
    seg_len = float(seq // SEGMENTS)
    c_idx = lax.axis_index("c").astype(jnp.int32)
    chip_idx = (2 * lax.axis_index("x") + lax.axis_index("y")).astype(jnp.int32)
    c_arr = jnp.stack([c_idx])
    idx_arr = jnp.stack([c_idx, chip_idx])

    chip_arr = jnp.stack([chip_idx])
    placed = {}
    for piece in _PIECES:
        placed[piece[1]] = _cast_place(piece, w[piece[0]], placed.get(piece[1]), chip_arr, "cast_" + piece[0])
    wf = dict(zip([b[0] for b in _BUFFERS], _gather_weights([placed[b[0]] for b in _BUFFERS])))
    cos, sin = _rope_tables(seq)

    saved = []
    xc = x
    for i in range(DEPTH):
        t = "_l%d" % i
        s = {"x_in": xc}
        s["h"] = _rms_fwd(xc, w["ln_mix_g"][i:i + 1], "rms_mix" + t)
        s["proj"] = _matmul(s["h"], wf["in"][i], "nn", [F32], name="mm_in" + t)[0]
        s["qr"], s["kr"] = _rot_fwd(s["proj"], cos, sin, "rot" + t)
        s["lg"] = jnp.broadcast_to(w["ret_log_gamma"][i].T[:, :, None], (HEADS, 2, LANES))
        s["y"] = _ret_fwd(s["qr"], s["kr"], s["proj"], s["lg"], "ret" + t)
        s5_raw = (w["ssm_a_re"][i], w["ssm_a_im"][i], w["ssm_log_dt"][i], w["ssm_b_re"][i], w["ssm_b_im"][i])
        disc, s["disc_vjp"] = jax.vjp(functools.partial(_s5_discretize, seg_len=seg_len), *s5_raw)
        bblk, cblk, lam = _s5_pack(*disc, w["ssm_c_re"][i], w["ssm_c_im"][i])
        s["s5"] = (bblk.astype(BF16), cblk.astype(BF16), lam)
        s["s5y"] = _s5_fwd(s["proj"], *s["s5"], "s5" + t)
        s["ret"], s["ysg"], s["ysgb"] = _post1_fwd(s["y"], s["proj"], s["s5y"], w["ssm_d"][i:i + 1], "post" + t)
        s["z"] = _matmul(s["ysgb"], wf["glu"][i], "nn", [F32], name="mm_glu" + t)[0]
        s["merged"] = _merge_fwd(s["z"], s["ysg"], s["proj"], s["ret"], w["b_glu"][i:i + 1], "merge" + t)
        s["x1"] = _matmul(s["merged"], wf["out"][i], "nn", [F32], add=xc, name="mm_out" + t)[0]
        s["h2"] = _rms_fwd(s["x1"], w["ln_ffn_g"][i:i + 1], "rms_ffn" + t)
        s["ab"] = _matmul(s["h2"], wf["gu"][i], "nn", [F32], name="mm_gu" + t)[0]
        s["f"] = _glu_fwd(s["ab"], "glu" + t)
        xc = _matmul(s["f"], wf["down"][i], "nn", [F32], add=s["x1"], name="mm_down" + t)[0]
        saved.append(s)

    dx, dxb, loss_row, dg_final = _loss_stage(xc, target, w["ln_final_g"][None, :], "loss")
    loss = lax.psum(loss_row[0, 0], ("x", "y", "c"))

    g_small = {"ln_final_g": dg_final[0]}
    per_layer = {n: [None] * DEPTH for n in _SMALL if n != "ln_final_g"}
    dws = {b[0]: None for b in _BUFFERS}

    def dw_mm(a, b, buf, i, name):
        dws[buf] = _matmul(a, b, "tn", [F32, BF16], stack=dws[buf] or (None, None), layer=i, name=name)

    for i in reversed(range(DEPTH)):
        t = "_l%d" % i
        s = saved[i]
        dw_mm(s["f"], dxb, "down", i, "dw_down" + t)
        df = _matmul(dxb, wf["down"][i], "nt", [F32], name="dx_down" + t)[0]
        dab = _glu_bwd(s["ab"], df, "glu_bwd" + t)
        dw_mm(s["h2"], dab, "gu", i, "dw_gu" + t)
        dh2 = _matmul(dab, wf["gu"][i], "nt", [F32], name="dx_gu" + t)[0]
        dx1, dx1b, dg = _rms_bwd(s["x1"], dh2, dx, w["ln_ffn_g"][i:i + 1], "rms_ffn_bwd" + t)
        per_layer["ln_ffn_g"][i] = dg[0]

        dw_mm(s["merged"], dx1b, "out", i, "dw_out" + t)
        dmerged = _matmul(dx1b, wf["out"][i], "nt", [F32], name="dx_out" + t)[0]
        dz, dys_part, dgs, db = _merge_bwd(s["z"], s["ysg"], s["proj"], s["ret"], dmerged, w["b_glu"][i:i + 1],
                                           "merge_bwd" + t)
        per_layer["b_glu"][i] = db[0]
        dw_mm(s["ysgb"], dz, "glu", i, "dw_glu" + t)
        dys = _matmul(dz, wf["glu"][i], "nt", [F32], add=dys_part, name="dx_glu" + t)[0]
        dy, dgg, dgr, ds5, du_part, dd = _post1_bwd(s["y"], s["proj"], s["s5y"], dmerged, dys,
                                                    w["ssm_d"][i:i + 1], "post_bwd" + t)
        per_layer["ssm_d"][i] = dd[0]
        du, dbblk, dcblk, dlam = _s5_bwd(s["proj"], ds5, du_part, *s["s5"], "s5_bwd" + t)
        dlr, dli, dbr, dbi, dcr, dci = _s5_unpack(dbblk, dcblk, dlam)
        zeros = jnp.zeros_like(dlr)
        da_re, da_im, dlog_dt, db_re, db_im = s["disc_vjp"]((dlr, dli, zeros, zeros, dbr, dbi))
        for n, val in (("ssm_a_re", da_re), ("ssm_a_im", da_im), ("ssm_log_dt", dlog_dt), ("ssm_b_re", db_re),
                       ("ssm_b_im", db_im), ("ssm_c_re", dcr), ("ssm_c_im", dci)):
            per_layer[n][i] = val
        dqr, dkr, dv, dlg = _ret_bwd(s["qr"], s["kr"], s["proj"], dy, s["lg"], "ret_bwd" + t)
        per_layer["ret_log_gamma"][i] = dlg[:, :, 0].T
        dqkv = _rot_bwd(dqr, dkr, dv, cos, sin, "rot_bwd" + t)
        dproj = jnp.concatenate([dqkv, dgg, du, dgr, dgs], axis=1)
        dw_mm(s["h"], dproj, "in", i, "dw_in" + t)
        dh = _matmul(dproj, wf["in"][i], "nt", [F32], name="dx_in" + t)[0]
        dx, dxb, dg = _rms_bwd(s["x_in"], dh, dx1, w["ln_mix_g"][i:i + 1], "rms_mix_bwd" + t)
        per_layer["ln_mix_g"][i] = dg[0]

    got = _swap_other_layer([dws[b[0]][1] for b in _BUFFERS])
    partials = [_chip_partial(dws[b[0]][0], got[k], c_arr, "chip_partial_" + b[0]) for k, b in enumerate(_BUFFERS)]
    recv = _scatter_partials(partials)
    reduced = []
    for p, piece in enumerate(_PIECES):
        k = _BUF_INDEX[piece[1]]
        reduced.append(_reduce_shard(piece, dws[piece[1]][0], got[k], recv[p], idx_arr, "reduce_" + piece[0]))
    g_big = dict(zip([p[0] for p in _PIECES], _share_with_sibling(reduced)))

    for n in per_layer:
        g_small[n] = jnp.stack(per_layer[n])
    g_flat = _sum_slots(_all_gather_rows(_flatten_small(g_small)), "sum_small")

    grads, delta, new_m, new_v = {}, {}, {}, {}
    for n in _BIG:
        d, r, cc = w[n].shape
        two_d = lambda a: a.reshape(d * r, cc)
        dl, mn, vn = _adamw(two_d(w[n]), two_d(g_big[n]), two_d(m[n]), two_d(v[n]), "adamw_" + n)
        grads[n], delta[n], new_m[n], new_v[n] = g_big[n], dl.reshape(d, r, cc), mn.reshape(d, r, cc), vn.reshape(d, r, cc)
    dl, mn, vn = _adamw(_flatten_small(w), g_flat, _flatten_small(m), _flatten_small(v), "adamw_small")
    for dst, flat in ((grads, g_flat), (delta, dl), (new_m, mn), (new_v, vn)):
        dst.update(_unflatten_small(flat, w))
    return loss, dx, grads, delta, new_m, new_v


def kernel(x, ln_mix_g, w_in, ret_log_gamma, ssm_a_re, ssm_a_im, ssm_log_dt, ssm_b_re, ssm_b_im, ssm_c_re, ssm_c_im, ssm_d, w_glu, b_glu, w_out, ln_ffn_g, w_ffn_gate, w_ffn_up, w_ffn_down, ln_final_g, loss_target, m_ln_mix_g, m_w_in, m_ret_log_gamma, m_ssm_a_re, m_ssm_a_im, m_ssm_log_dt, m_ssm_b_re, m_ssm_b_im, m_ssm_c_re, m_ssm_c_im, m_ssm_d, m_w_glu, m_b_glu, m_w_out, m_ln_ffn_g, m_w_ffn_gate, m_w_ffn_up, m_w_ffn_down, m_ln_final_g, v_ln_mix_g, v_w_in, v_ret_log_gamma, v_ssm_a_re, v_ssm_a_im, v_ssm_log_dt, v_ssm_b_re, v_ssm_b_im, v_ssm_c_re, v_ssm_c_im, v_ssm_d, v_w_glu, v_b_glu, v_w_out, v_ln_ffn_g, v_w_ffn_gate, v_w_ffn_up, v_w_ffn_down, v_ln_final_g):
    given = dict(locals())
    w = {n: given[n] for n in _WEIGHTS}
    m = {n: given["m_" + n] for n in _WEIGHTS}
    v = {n: given["v_" + n] for n in _WEIGHTS}
    loss, dx, grads, delta, new_m, new_v = _step(w, m, v, x[0], loss_target[0])
    return (loss, dx[None], *[grads[n] for n in _WEIGHTS], *[delta[n] for n in _WEIGHTS],
            *[new_m[n] for n in _WEIGHTS], *[new_v[n] for n in _WEIGHTS])
```

```python
import functools
import math

import jax
import jax.numpy as jnp
from jax import lax
from jax.experimental import pallas as pl
from jax.experimental.pallas import tpu as pltpu

F32 = jnp.float32
BF16 = jnp.bfloat16

D_MODEL = 2048
DEPTH = 2
HEADS = 4
QK_DIM = 256
V_DIM = 512
QK_WIDTH = HEADS * QK_DIM
ROPE_BASE = 10000.0
GROUP = 16
N_GROUPS = D_MODEL // GROUP
N_STATE = 64
D_FF = 5632
IN_WIDTH = 2 * QK_WIDTH + 5 * D_MODEL
EPS = 1e-6
N_CHIPS = 4

ADAM_LR = 0.001
ADAM_B1 = 0.9
ADAM_B2 = 0.999
ADAM_EPS = 1e-08
ADAM_WD = 0.01
ADAM_STEP = 10

LANES = 128
SUBLANES = 8
VMEM_LIMIT = 56 * 1024 * 1024
SEGMENTS = SUBLANES
GROUPS_PER_TILE = LANES // GROUP
STATE_COLS = GROUPS_PER_TILE * N_STATE
N_TILES = D_MODEL // LANES
SCAN_UNROLL = 4

MESH = pl.DeviceIdType.MESH
HBM_SPEC = pl.BlockSpec(memory_space=pltpu.HBM)


def _params(sem=None, **kw):
    return pltpu.CompilerParams(dimension_semantics=sem, vmem_limit_bytes=VMEM_LIMIT, **kw)


def _tile(n, cap=1024):
    for t in (1024, 512, 256, 128):
        if t <= cap and n % t == 0:
            return t
    raise ValueError(n)


def _rows_call(fn, rows, pars, row_outs, par_outs, *, tm, name):
    m = rows[0][0].shape[0]
    nr, npar, nro, npo = len(rows), len(pars), len(row_outs), len(par_outs)

    def body(*refs):
        rin = refs[:nr]
        pin = refs[nr:nr + npar]
        rout = refs[nr + npar:nr + npar + nro]
        pout = refs[nr + npar + nro:]
        res = fn(*[r[...] for r in rin], *[p[...] for p in pin])
        if not isinstance(res, (tuple, list)):
            res = (res,)
        for r, v in zip(rout, res[:nro]):
            r[...] = v.astype(r.dtype)
        if npo:
            @pl.when(pl.program_id(0) == 0)
            def _():
                for p in pout:
                    p[...] = jnp.zeros(p.shape, p.dtype)
            for p, v in zip(pout, res[nro:]):
                p[...] += v

    in_specs = [pl.BlockSpec((tm, w), functools.partial(lambda cb, i: (i, cb), cb)) for (_, w, cb) in rows]
    in_specs += [pl.BlockSpec(p.shape, lambda i: (0, 0)) for p in pars]
    out_specs = [pl.BlockSpec((tm, w), lambda i: (i, 0)) for (w, _) in row_outs]
    out_specs += [pl.BlockSpec(s, lambda i: (0, 0)) for s in par_outs]
    out_shape = [jax.ShapeDtypeStruct((m, w), dt) for (w, dt) in row_outs]
    out_shape += [jax.ShapeDtypeStruct(s, F32) for s in par_outs]
    res = pl.pallas_call(
        body, name=name, grid=(m // tm,), in_specs=in_specs, out_specs=out_specs, out_shape=out_shape,
        compiler_params=_params(("arbitrary",) if npo else ("parallel",)),
    )(*[a for (a, _, _) in rows], *pars)
    return res


def _f32(*vals):
    return [v.astype(F32) for v in vals]


def _f_rms(x, g):
    r = lax.rsqrt(jnp.mean(x * x, axis=-1, keepdims=True) + EPS)
    return x * r * g


def _rms_fwd(x, g, name):
    return _rows_call(lambda xv, gv: _f_rms(xv, gv), [(x, D_MODEL, 0)], [g], [(D_MODEL, BF16)], [],
                      tm=256, name=name)[0]


def _rms_bwd(x, dh, dres, g, name):
    def fn(xv, dhv, drv, gv):
        _, vjp = jax.vjp(_f_rms, xv, gv)
        dx, dg = vjp(dhv)
        dx = dx + drv
        return dx, dx, dg
    return _rows_call(fn, [(x, D_MODEL, 0), (dh, D_MODEL, 0), (dres, D_MODEL, 0)], [g],
                      [(D_MODEL, F32), (D_MODEL, BF16)], [(1, D_MODEL)], tm=256, name=name)


def _rot_heads(xv, cos, sin, scale):
    half = QK_DIM // 2
    outs = []
    for h in range(HEADS):
        x1 = xv[:, h * QK_DIM:h * QK_DIM + half]
        x2 = xv[:, h * QK_DIM + half:(h + 1) * QK_DIM]
        outs += [(x1 * cos - x2 * sin) * scale, (x1 * sin + x2 * cos) * scale]
    return jnp.concatenate(outs, axis=1)


def _rot_fwd(proj, cos, sin, name):
    def fn(q, k, cv, sv):
        return _rot_heads(q, cv, sv, 1.0), _rot_heads(k, cv, sv, QK_DIM ** -0.5)
    return _rows_call(fn, [(proj, QK_WIDTH, 0), (proj, QK_WIDTH, 1), (cos, LANES, 0), (sin, LANES, 0)], [],
                      [(QK_WIDTH, BF16), (QK_WIDTH, BF16)], [], tm=256, name=name)


def _rot_bwd(dqr, dkr, dv, cos, sin, name):
    def fn(dq, dk, dvv, cv, sv):
        return jnp.concatenate([_rot_heads(dq, cv, -sv, 1.0), _rot_heads(dk, cv, -sv, QK_DIM ** -0.5), dvv], axis=1)
    return _rows_call(fn, [(dqr, QK_WIDTH, 0), (dkr, QK_WIDTH, 0), (dv, D_MODEL, 0), (cos, LANES, 0), (sin, LANES, 0)],
                      [], [(2 * QK_WIDTH + D_MODEL, BF16)], [], tm=256, name=name)[0]


def _f_post1(y0, y1, y2, y3, g, gr, s5, u, dsk):
    yn = [yh * lax.rsqrt(jnp.mean(yh * yh, axis=-1, keepdims=True) + EPS) for yh in (y0, y1, y2, y3)]
    ret = jax.nn.sigmoid(gr) * (jax.nn.silu(g) * jnp.concatenate(yn, axis=1))
    ysg = jax.nn.gelu(s5 + dsk * u)
    return ret, ysg


def _post1_rows(y, proj, s5y):
    rows = [(y, V_DIM, h) for h in range(HEADS)]
    rows += [(proj, D_MODEL, 2), (proj, D_MODEL, 4), (s5y, D_MODEL, 0), (proj, D_MODEL, 3)]
    return rows


def _post1_fwd(y, proj, s5y, dsk, name):
    def fn(*vals):
        ret, ysg = _f_post1(*vals)
        return ret, ysg, ysg
    return _rows_call(fn, _post1_rows(y, proj, s5y), [dsk],
                      [(D_MODEL, F32), (D_MODEL, F32), (D_MODEL, BF16)], [], tm=128, name=name)


def _post1_bwd(y, proj, s5y, dret, dys, dsk, name):
    def fn(*vals):
        prim = vals[:8] + (vals[10],)
        _, vjp = jax.vjp(_f_post1, *prim)
        gy0, gy1, gy2, gy3, gg, ggr, gs5, gu, gd = vjp((vals[8], vals[9]))
        return jnp.concatenate([gy0, gy1, gy2, gy3], axis=1), gg, ggr, gs5, gu, gd
    rows = _post1_rows(y, proj, s5y) + [(dret, D_MODEL, 0), (dys, D_MODEL, 0)]
    return _rows_call(fn, rows, [dsk],
                      [(D_MODEL, BF16), (D_MODEL, BF16), (D_MODEL, BF16), (D_MODEL, F32), (D_MODEL, F32)],
                      [(1, D_MODEL)], tm=128, name=name)


def _f_merge(z, ysg, gs, ret, b):
    return ret + jax.nn.sigmoid(gs) * (ysg * jax.nn.sigmoid(z + b))


def _merge_fwd(z, ysg, proj, ret, b, name):
    return _rows_call(_f_merge, [(z, D_MODEL, 0), (ysg, D_MODEL, 0), (proj, D_MODEL, 5), (ret, D_MODEL, 0)], [b],
                      [(D_MODEL, BF16)], [], tm=128, name=name)[0]


def _merge_bwd(z, ysg, proj, ret, dm, b, name):
    def fn(zv, yv, gv, rv, dmv, bv):
        _, vjp = jax.vjp(_f_merge, zv, yv, gv, rv, bv)
        gz, gy, gg, _, gb = vjp(dmv)
        return gz, gy, gg, gb
    rows = [(z, D_MODEL, 0), (ysg, D_MODEL, 0), (proj, D_MODEL, 5), (ret, D_MODEL, 0), (dm, D_MODEL, 0)]
    return _rows_call(fn, rows, [b], [(D_MODEL, BF16), (D_MODEL, F32), (D_MODEL, BF16)], [(1, D_MODEL)],
                      tm=128, name=name)


def _f_glu(a, b):
    return jax.nn.silu(a) * b


def _glu_fwd(ab, name):
    return _rows_call(_f_glu, [(ab, D_FF, 0), (ab, D_FF, 1)], [], [(D_FF, BF16)], [], tm=128, name=name)[0]


def _glu_bwd(ab, df, name):
    def fn(a, b, d):
        _, vjp = jax.vjp(_f_glu, a, b)
        ga, gb = vjp(d)
        return jnp.concatenate([ga, gb], axis=1)
    return _rows_call(fn, [(ab, D_FF, 0), (ab, D_FF, 1), (df, D_FF, 0)], [], [(2 * D_FF, BF16)], [],
                      tm=128, name=name)[0]


def _loss_stage(x, tgt, g, name):
    def fn(xv, tv, gv):
        def lf(xx, gg):
            err = _f_rms(xx, gg) - tv
            row = jnp.mean(err * err, axis=-1, keepdims=True)
            return 0.5 * jnp.sum(row, axis=0, keepdims=True)
        l, vjp = jax.vjp(lf, xv, gv)
        dx, dg = vjp(jnp.ones((1, 1), F32))
        return dx, dx, jnp.broadcast_to(l, (1, LANES)), dg
    return _rows_call(fn, [(x, D_MODEL, 0), (tgt, D_MODEL, 0)], [g], [(D_MODEL, F32), (D_MODEL, BF16)],
                      [(1, LANES), (1, D_MODEL)], tm=256, name=name)


def _adamw(w, g, m, v, name):
    rows, cols = w.shape

    def fn(wv, gv, mv, vv):
        mn = ADAM_B1 * mv + (1.0 - ADAM_B1) * gv
        vn = ADAM_B2 * vv + (1.0 - ADAM_B2) * (gv * gv)
        m_hat = mn / (1.0 - ADAM_B1 ** ADAM_STEP)
        v_hat = vn / (1.0 - ADAM_B2 ** ADAM_STEP)
        delta = -ADAM_LR * (m_hat / (jnp.sqrt(v_hat) + ADAM_EPS) + ADAM_WD * wv)
        return delta, mn, vn
    tm = _tile(rows, 128 if cols > D_FF // N_CHIPS else (256 if cols > LANES else 512))
    return _rows_call(fn, [(w, cols, 0), (g, cols, 0), (m, cols, 0), (v, cols, 0)], [],
                      [(cols, F32)] * 3, [], tm=tm, name=name)


def _matmul(a, b, mode, out_dtypes, *, name, add=None, stack=None, layer=None):
    if mode == "nn":
        (m, k), (_, n) = a.shape, b.shape
    elif mode == "nt":
        (m, k), (n, _) = a.shape, b.shape
    else:
        (k, m), (_, n) = a.shape, b.shape
    tm, tn, tk = _tile(m), _tile(n), _tile(k)
    nk = k // tk
    if mode == "nn":
        a_spec = pl.BlockSpec((tm, tk), lambda i, j, kk: (i, kk))
        b_spec = pl.BlockSpec((tk, tn), lambda i, j, kk: (kk, j))
        dims = (((1,), (0,)), ((), ()))
    elif mode == "nt":
        a_spec = pl.BlockSpec((tm, tk), lambda i, j, kk: (i, kk))
        b_spec = pl.BlockSpec((tn, tk), lambda i, j, kk: (j, kk))
        dims = (((1,), (1,)), ((), ()))
    else:
        a_spec = pl.BlockSpec((tk, tm), lambda i, j, kk: (kk, i))
        b_spec = pl.BlockSpec((tk, tn), lambda i, j, kk: (kk, j))
        dims = (((0,), (0,)), ((), ()))
    n_out = len(out_dtypes)
    has_add = add is not None
    n_alias = n_out if (stack is not None and stack[0] is not None) else 0

    def body(*refs):
        a_ref, b_ref = refs[0], refs[1]
        add_ref = refs[2] if has_add else None
        outs = refs[2 + has_add + n_alias:2 + has_add + n_alias + n_out]
        acc = refs[-1]
        kk = pl.program_id(2)

        @pl.when(kk == 0)
        def _():
            acc[...] = jnp.zeros(acc.shape, F32)

        acc[...] += lax.dot_general(a_ref[...], b_ref[...], dims, preferred_element_type=F32)

        @pl.when(kk == nk - 1)
        def _():
            r = acc[...]
            if has_add:
                r = r + add_ref[...]
            for o in outs:
                o[...] = r.astype(o.dtype)

    in_specs = [a_spec, b_spec]
    args = [a, b]
    if has_add:
        in_specs.append(pl.BlockSpec((tm, tn), lambda i, j, kk: (i, j)))
        args.append(add)
    aliases = {}
    if stack is None:
        out_specs = [pl.BlockSpec((tm, tn), lambda i, j, kk: (i, j))] * n_out
        out_shape = [jax.ShapeDtypeStruct((m, n), dt) for dt in out_dtypes]
    else:
        out_specs = [pl.BlockSpec((None, tm, tn), functools.partial(lambda ly, i, j, kk: (ly, i, j), layer))] * n_out
        out_shape = [jax.ShapeDtypeStruct((DEPTH, m, n), dt) for dt in out_dtypes]
        if n_alias:
            for t, buf in enumerate(stack):
                aliases[len(args)] = t
                in_specs.append(pl.BlockSpec(memory_space=pl.ANY))
                args.append(buf)
    return pl.pallas_call(
        body, name=name, grid=(m // tm, n // tn, nk), in_specs=in_specs, out_specs=out_specs, out_shape=out_shape,
        scratch_shapes=[pltpu.VMEM((tm, tn), F32)], input_output_aliases=aliases,
        compiler_params=_params(("parallel", "parallel", "arbitrary")),
    )(*args)


RET_TQ = 512


def _decay(lg_ref, i, tq, seq):
    n_idx = i * tq + lax.broadcasted_iota(jnp.int32, (tq, seq), 0)
    m_idx = lax.broadcasted_iota(jnp.int32, (tq, seq), 1)
    diff = (n_idx - m_idx).astype(F32)
    lgf = lg_ref[0, 0:1, 0:1]
    lgb = lg_ref[0, 1:2, 0:1]
    causal = diff >= 0
    return jnp.exp(jnp.where(causal, lgf * diff, -lgb * diff)), diff, causal


_NT = (((1,), (1,)), ((), ()))
_TN = (((0,), (0,)), ((), ()))


def _ret_fwd(qr, kr, proj, lg, name):
    seq = qr.shape[0]
    tq = RET_TQ
    v_blk0 = (2 * QK_WIDTH) // V_DIM

    def body(q_ref, k_ref, v_ref, lg_ref, y_ref):
        i = pl.program_id(1)
        s = lax.dot_general(q_ref[...], k_ref[...], _NT, preferred_element_type=F32)
        dm, _, _ = _decay(lg_ref, i, tq, seq)
        p = (s * dm).astype(BF16)
        y_ref[...] = jnp.dot(p, v_ref[...].astype(BF16), preferred_element_type=F32)

    return pl.pallas_call(
        body, name=name, grid=(HEADS, seq // tq),
        in_specs=[pl.BlockSpec((tq, QK_DIM), lambda h, i: (i, h)),
                  pl.BlockSpec((seq, QK_DIM), lambda h, i: (0, h)),
                  pl.BlockSpec((seq, V_DIM), lambda h, i: (0, v_blk0 + h)),
                  pl.BlockSpec((1, 2, LANES), lambda h, i: (h, 0, 0))],
        out_specs=pl.BlockSpec((tq, V_DIM), lambda h, i: (i, h)),
        out_shape=jax.ShapeDtypeStruct((seq, HEADS * V_DIM), F32),
        compiler_params=_params(("parallel", "parallel")),
    )(qr, kr, proj, lg)


def _ret_bwd(qr, kr, proj, dy, lg, name):
    seq = qr.shape[0]
    tq = RET_TQ
    v_blk0 = (2 * QK_WIDTH) // V_DIM

    def body(q_ref, k_ref, v_ref, dy_ref, lg_ref, dq_ref, dk_ref, dv_ref, dlg_ref):
        i = pl.program_id(1)

        @pl.when(i == 0)
        def _():
            dk_ref[...] = jnp.zeros(dk_ref.shape, F32)
            dv_ref[...] = jnp.zeros(dv_ref.shape, F32)
            dlg_ref[...] = jnp.zeros(dlg_ref.shape, F32)

        q = q_ref[...]
        k = k_ref[...]
        vb = v_ref[...].astype(BF16)
        dyb = dy_ref[...]
        s = lax.dot_general(q, k, _NT, preferred_element_type=F32)
        dm, diff, causal = _decay(lg_ref, i, tq, seq)
        p = s * dm
        dp = lax.dot_general(dyb, vb, _NT, preferred_element_type=F32)
        dv_ref[...] += lax.dot_general(p.astype(BF16), dyb, _TN, preferred_element_type=F32)
        ds = (dp * dm).astype(BF16)
        dq_ref[...] = jnp.dot(ds, k, preferred_element_type=F32)
        dk_ref[...] += lax.dot_general(ds, q, _TN, preferred_element_type=F32)
        gd = dp * p * diff
        dlf = jnp.sum(jnp.sum(jnp.where(causal, gd, 0.0), axis=1, keepdims=True), axis=0, keepdims=True)
        dlb = jnp.sum(jnp.sum(jnp.where(causal, 0.0, -gd), axis=1, keepdims=True), axis=0, keepdims=True)
        row = lax.broadcasted_iota(jnp.int32, (2, LANES), 0)
        dlg_ref[0] += jnp.where(row == 0, dlf, dlb)

    return pl.pallas_call(
        body, name=name, grid=(HEADS, seq // tq),
        in_specs=[pl.BlockSpec((tq, QK_DIM), lambda h, i: (i, h)),
                  pl.BlockSpec((seq, QK_DIM), lambda h, i: (0, h)),
                  pl.BlockSpec((seq, V_DIM), lambda h, i: (0, v_blk0 + h)),
                  pl.BlockSpec((tq, V_DIM), lambda h, i: (i, h)),
                  pl.BlockSpec((1, 2, LANES), lambda h, i: (h, 0, 0))],
        out_specs=[pl.BlockSpec((tq, QK_DIM), lambda h, i: (i, h)),
                   pl.BlockSpec((seq, QK_DIM), lambda h, i: (0, h)),
                   pl.BlockSpec((seq, V_DIM), lambda h, i: (0, h)),
                   pl.BlockSpec((1, 2, LANES), lambda h, i: (h, 0, 0))],
        out_shape=[jax.ShapeDtypeStruct((seq, QK_WIDTH), F32), jax.ShapeDtypeStruct((seq, QK_WIDTH), F32),
                   jax.ShapeDtypeStruct((seq, HEADS * V_DIM), F32), jax.ShapeDtypeStruct((HEADS, 2, LANES), F32)],
        compiler_params=_params(("parallel", "arbitrary")),
    )(qr, kr, proj, dy, lg)


def _shift_rows(v, reverse):
    row = lax.broadcasted_iota(jnp.int32, v.shape, 0)
    if reverse:
        return jnp.where(row == SEGMENTS - 1, 0.0, pltpu.roll(v, SEGMENTS - 1, 0))
    return jnp.where(row == 0, 0.0, pltpu.roll(v, 1, 0))


def _slab(t):
    if isinstance(t, int):
        return pl.ds(t * SEGMENTS, SEGMENTS)
    return pl.ds(pl.multiple_of(t * SEGMENTS, SEGMENTS), SEGMENTS)


def _unrolled_loop(body, lo, hi, init):
    main = (hi - lo) // SCAN_UNROLL

    def unrolled(g, carry):
        for k in range(SCAN_UNROLL):
            carry = body(lo + g * SCAN_UNROLL + k, carry)
        return carry

    carry = lax.fori_loop(0, main, unrolled, init)
    for t in range(lo + main * SCAN_UNROLL, hi):
        carry = body(t, carry)
    return carry


def _scan(xr_ref, xi_ref, lam, reverse, conj):
    steps = xr_ref.shape[0] // SEGMENTS
    cols = xr_ref.shape[1]
    lr = jnp.broadcast_to(lam[0], (SEGMENTS, cols))
    li = jnp.broadcast_to(lam[1], (SEGMENTS, cols))
    lrt = jnp.broadcast_to(lam[2], (SEGMENTS, cols))
    lit = jnp.broadcast_to(lam[3], (SEGMENTS, cols))
    if conj:
        li, lit = -li, -lit
    zero = jnp.zeros((SEGMENTS, cols), F32)

    def rows_of(t):
        return _slab(steps - 1 - t if reverse else t)

    def advance(t, carry):
        sr, si = carry
        rows = rows_of(t)
        return lr * sr - li * si + xr_ref[rows, :], lr * si + li * sr + xi_ref[rows, :]

    def step(t, carry):
        nr, ni = advance(t, carry)
        rows = rows_of(t)
        xr_ref[rows, :] = nr
        xi_ref[rows, :] = ni
        return nr, ni

    def run(body, init):
        return _unrolled_loop(body, 0, steps, init)

    er, ei = run(advance, (zero, zero))
    cr, ci = zero, zero
    for _ in range(SEGMENTS - 1):
        tr = er + lrt * cr - lit * ci
        ti = ei + lrt * ci + lit * cr
        cr, ci = _shift_rows(tr, reverse), _shift_rows(ti, reverse)
    run(step, (cr, ci))


def _permute_in(dst_ref, src_ref):
    steps = src_ref.shape[0] // SEGMENTS
    for s in range(SEGMENTS):
        dst_ref[pl.ds(s, steps, stride=SEGMENTS), :] = src_ref[s * steps:(s + 1) * steps, :].astype(dst_ref.dtype)


def _unpermute(src_ref, s):
    steps = src_ref.shape[0] // SEGMENTS
    return src_ref[pl.ds(s, steps, stride=SEGMENTS), :]


def _s5_fwd(proj, bblk, cblk, lam, name):
    seq = proj.shape[0]
    u_blk0 = (2 * QK_WIDTH + 2 * D_MODEL) // LANES
    sc = STATE_COLS

    def body(u_ref, b_ref, c_ref, lam_ref, y_ref, up_ref, yp_ref, xr_ref, xi_ref):
        _permute_in(up_ref, u_ref)
        ub = up_ref[...].astype(BF16)
        for d in range(2):
            xr_ref[...] = jnp.dot(ub, b_ref[d, :, 0:sc], preferred_element_type=F32)
            xi_ref[...] = jnp.dot(ub, b_ref[d, :, sc:2 * sc], preferred_element_type=F32)
            lm = [lam_ref[d, r:r + 1, :] for r in range(4)]
            _scan(xr_ref, xi_ref, lm, reverse=(d == 1), conj=False)
            yd = (jnp.dot(xr_ref[...].astype(BF16), c_ref[d, 0:sc, :], preferred_element_type=F32)
                  + jnp.dot(xi_ref[...].astype(BF16), c_ref[d, sc:2 * sc, :], preferred_element_type=F32))
            if d == 0:
                yp_ref[...] = yd
            else:
                yp_ref[...] += yd
        steps = seq // SEGMENTS
        for s in range(SEGMENTS):
            y_ref[s * steps:(s + 1) * steps, :] = _unpermute(yp_ref, s)

    return pl.pallas_call(
        body, name=name, grid=(N_TILES,),
        in_specs=[pl.BlockSpec((seq, LANES), lambda j: (0, u_blk0 + j)),
                  pl.BlockSpec((2, None, LANES, 2 * sc), lambda j: (0, j, 0, 0)),
                  pl.BlockSpec((2, None, 2 * sc, LANES), lambda j: (0, j, 0, 0)),
                  pl.BlockSpec((2, None, 4, sc), lambda j: (0, j, 0, 0))],
        out_specs=pl.BlockSpec((seq, LANES), lambda j: (0, j)),
        out_shape=jax.ShapeDtypeStruct((seq, D_MODEL), F32),
        scratch_shapes=[pltpu.VMEM((seq, LANES), F32), pltpu.VMEM((seq, LANES), F32),
                        pltpu.VMEM((seq, sc), F32), pltpu.VMEM((seq, sc), F32)],
        compiler_params=_params(("parallel",)),
    )(proj, bblk, cblk, lam)


def _s5_bwd(proj, dy, du_part, bblk, cblk, lam, name):
    seq = proj.shape[0]
    u_blk0 = (2 * QK_WIDTH + 2 * D_MODEL) // LANES
    sc = STATE_COLS
    steps = seq // SEGMENTS

    def body(u_ref, dy_ref, dup_ref, b_ref, c_ref, lam_ref, du_ref, db_ref, dc_ref, dlam_ref,
             up_ref, dyp_ref, dua_ref, xr_ref, xi_ref, gr_ref, gi_ref):
        _permute_in(up_ref, u_ref)
        _permute_in(dyp_ref, dy_ref)
        ub = up_ref[...].astype(BF16)
        dyb = dyp_ref[...].astype(BF16)
        ubt = up_ref[...].T.astype(BF16)
        dybt = dyp_ref[...].T.astype(BF16)
        for d in range(2):
            reverse = d == 1
            xr_ref[...] = jnp.dot(ub, b_ref[d, :, 0:sc], preferred_element_type=F32)
            xi_ref[...] = jnp.dot(ub, b_ref[d, :, sc:2 * sc], preferred_element_type=F32)
            lm = [lam_ref[d, r:r + 1, :] for r in range(4)]
            _scan(xr_ref, xi_ref, lm, reverse=reverse, conj=False)
            xrb = xr_ref[...].astype(BF16)
            xib = xi_ref[...].astype(BF16)
            dc_ref[d, :, 0:sc] = jnp.dot(dybt, xrb, preferred_element_type=F32)
            dc_ref[d, :, sc:2 * sc] = jnp.dot(dybt, xib, preferred_element_type=F32)
            gr_ref[...] = lax.dot_general(dyb, c_ref[d, 0:sc, :], _NT, preferred_element_type=F32)
            gi_ref[...] = lax.dot_general(dyb, c_ref[d, sc:2 * sc, :], _NT, preferred_element_type=F32)
            _scan(gr_ref, gi_ref, lm, reverse=not reverse, conj=True)

            def acc_step(t, carry):
                ar, ai = carry
                prev = _slab(t + 1 if reverse else t - 1)
                pr = xr_ref[prev, :]
                pi = xi_ref[prev, :]
                zr = gr_ref[_slab(t), :]
                zi = gi_ref[_slab(t), :]
                return ar + zr * pr + zi * pi, ai + zi * pr - zr * pi

            zero = jnp.zeros((SEGMENTS, sc), F32)
            if reverse:
                ar, ai = _unrolled_loop(acc_step, 0, steps - 1, (zero, zero))
                edge = _slab(steps - 1)
                pr = _shift_rows(xr_ref[_slab(0), :], True)
                pi = _shift_rows(xi_ref[_slab(0), :], True)
            else:
                ar, ai = _unrolled_loop(acc_step, 1, steps, (zero, zero))
                edge = _slab(0)
                pr = _shift_rows(xr_ref[_slab(steps - 1), :], False)
                pi = _shift_rows(xi_ref[_slab(steps - 1), :], False)
            zr = gr_ref[edge, :]
            zi = gi_ref[edge, :]
            ar = ar + zr * pr + zi * pi
            ai = ai + zi * pr - zr * pi
            dlam_ref[d, 0:1, :] = jnp.sum(ar, axis=0, keepdims=True)
            dlam_ref[d, 1:2, :] = jnp.sum(ai, axis=0, keepdims=True)

            grb = gr_ref[...].astype(BF16)
            gib = gi_ref[...].astype(BF16)
            db_ref[d, :, 0:sc] = jnp.dot(ubt, grb, preferred_element_type=F32)
            db_ref[d, :, sc:2 * sc] = jnp.dot(ubt, gib, preferred_element_type=F32)
            dud = (lax.dot_general(grb, b_ref[d, :, 0:sc], _NT, preferred_element_type=F32)
                   + lax.dot_general(gib, b_ref[d, :, sc:2 * sc], _NT, preferred_element_type=F32))
            if d == 0:
                dua_ref[...] = dud
            else:
                dua_ref[...] += dud
        for s in range(SEGMENTS):
            rows = slice(s * steps, (s + 1) * steps)
            du_ref[rows, :] = (_unpermute(dua_ref, s) + dup_ref[rows, :]).astype(du_ref.dtype)

    return pl.pallas_call(
        body, name=name, grid=(N_TILES,),
        in_specs=[pl.BlockSpec((seq, LANES), lambda j: (0, u_blk0 + j)),
                  pl.BlockSpec((seq, LANES), lambda j: (0, j)),
                  pl.BlockSpec((seq, LANES), lambda j: (0, j)),
                  pl.BlockSpec((2, None, LANES, 2 * sc), lambda j: (0, j, 0, 0)),
                  pl.BlockSpec((2, None, 2 * sc, LANES), lambda j: (0, j, 0, 0)),
                  pl.BlockSpec((2, None, 4, sc), lambda j: (0, j, 0, 0))],
        out_specs=[pl.BlockSpec((seq, LANES), lambda j: (0, j)),
                   pl.BlockSpec((2, None, LANES, 2 * sc), lambda j: (0, j, 0, 0)),
                   pl.BlockSpec((2, None, LANES, 2 * sc), lambda j: (0, j, 0, 0)),
                   pl.BlockSpec((2, None, 2, sc), lambda j: (0, j, 0, 0))],
        out_shape=[jax.ShapeDtypeStruct((seq, D_MODEL), BF16),
                   jax.ShapeDtypeStruct((2, N_TILES, LANES, 2 * sc), F32),
                   jax.ShapeDtypeStruct((2, N_TILES, LANES, 2 * sc), F32),
                   jax.ShapeDtypeStruct((2, N_TILES, 2, sc), F32)],
        scratch_shapes=[pltpu.VMEM((seq, LANES), F32), pltpu.VMEM((seq, LANES), F32), pltpu.VMEM((seq, LANES), F32),
                        pltpu.VMEM((seq, sc), F32), pltpu.VMEM((seq, sc), F32),
                        pltpu.VMEM((seq, sc), F32), pltpu.VMEM((seq, sc), F32)],
        compiler_params=_params(("parallel",)),
    )(proj, dy, du_part, bblk, cblk, lam)


def _s5_discretize(a_re, a_im, log_dt, b_re, b_im, seg_len):
    dt = jnp.exp(log_dt)[..., None]
    e = jnp.exp(a_re * dt)
    lr, li = e * jnp.cos(a_im * dt), e * jnp.sin(a_im * dt)
    et = jnp.exp(a_re * dt * seg_len)
    lrt, lit = et * jnp.cos(a_im * dt * seg_len), et * jnp.sin(a_im * dt * seg_len)
    den = a_re * a_re + a_im * a_im
    qr = ((lr - 1.0) * a_re + li * a_im) / den
    qi = (li * a_re - (lr - 1.0) * a_im) / den
    br = qr[..., None] * b_re - qi[..., None] * b_im
    bi = qr[..., None] * b_im + qi[..., None] * b_re
    return lr, li, lrt, lit, br, bi


def _s5_pack(lr, li, lrt, lit, br, bi, c_re, c_im):
    eye = jnp.eye(GROUPS_PER_TILE, dtype=F32)

    def bd_b(b):
        b5 = b.reshape(2, N_TILES, GROUPS_PER_TILE, N_STATE, GROUP)
        return jnp.einsum("dtgph,gk->dtghkp", b5, eye).reshape(2, N_TILES, LANES, STATE_COLS)

    def bd_c(c):
        c5 = c.reshape(2, N_TILES, GROUPS_PER_TILE, GROUP, N_STATE)
        return jnp.einsum("dtghp,gk->dtkpgh", c5, eye).reshape(2, N_TILES, STATE_COLS, LANES)

    bblk = jnp.concatenate([bd_b(br), bd_b(bi)], axis=3)
    cblk = jnp.concatenate([bd_c(c_re), -bd_c(c_im)], axis=2)
    lam = jnp.stack([v.reshape(2, N_TILES, STATE_COLS) for v in (lr, li, lrt, lit)], axis=2)
    return bblk, cblk, lam


def _s5_unpack(dbblk, dcblk, dlam):
    eye = jnp.eye(GROUPS_PER_TILE, dtype=F32)

    def diag_b(d):
        d6 = d.reshape(2, N_TILES, GROUPS_PER_TILE, GROUP, GROUPS_PER_TILE, N_STATE)
        return jnp.einsum("dtghkp,gk->dtgph", d6, eye).reshape(2, N_GROUPS, N_STATE, GROUP)

    def diag_c(d):
        d6 = d.reshape(2, N_TILES, GROUPS_PER_TILE, GROUP, GROUPS_PER_TILE, N_STATE)
        return jnp.einsum("dtghkp,gk->dtghp", d6, eye).reshape(2, N_GROUPS, GROUP, N_STATE)

    dbr, dbi = diag_b(dbblk[..., :STATE_COLS]), diag_b(dbblk[..., STATE_COLS:])
    dcr, dci = diag_c(dcblk[..., :STATE_COLS]), -diag_c(dcblk[..., STATE_COLS:])
    dlr = dlam[:, :, 0, :].reshape(2, N_GROUPS, N_STATE)
    dli = dlam[:, :, 1, :].reshape(2, N_GROUPS, N_STATE)
    return dlr, dli, dbr, dbi, dcr, dci


def _pos():
    return lax.axis_index("x"), lax.axis_index("y"), lax.axis_index("c")


def _remote(src, dst, ssem, rsem, dev):
    return pltpu.make_async_remote_copy(src_ref=src, dst_ref=dst, send_sem=ssem, recv_sem=rsem,
                                        device_id=dev, device_id_type=MESH)


_PIECES = (
    ("w_in", "in", D_MODEL, IN_WIDTH // N_CHIPS, 0, IN_WIDTH // N_CHIPS, 0),
    ("w_glu", "glu", D_MODEL // N_CHIPS, D_MODEL, D_MODEL // N_CHIPS, 0, 0),
    ("w_out", "out", D_MODEL // N_CHIPS, D_MODEL, D_MODEL // N_CHIPS, 0, 0),
    ("w_ffn_gate", "gu", D_MODEL, D_FF // N_CHIPS, 0, D_FF // N_CHIPS, 0),
    ("w_ffn_up", "gu", D_MODEL, D_FF // N_CHIPS, 0, D_FF // N_CHIPS, D_FF),
    ("w_ffn_down", "down", D_FF // N_CHIPS, D_MODEL, D_FF // N_CHIPS, 0, 0),
)
_BUFFERS = (("in", D_MODEL, IN_WIDTH), ("glu", D_MODEL, D_MODEL), ("out", D_MODEL, D_MODEL),
            ("gu", D_MODEL, 2 * D_FF), ("down", D_FF, D_MODEL))
_BUF_INDEX = {name: t for t, (name, _, _) in enumerate(_BUFFERS)}
N_PIECES = len(_PIECES)
N_BUFFERS = len(_BUFFERS)


def _piece_view(ref, piece, j):
    _, _, rs, cs, rstep, cstep, coff = piece
    return ref.at[pl.ds(j * rstep, rs), pl.ds(coff + j * cstep, cs)]


def _own_block(piece, tm):
    _, _, _, cs, rstep, cstep, coff = piece
    return lambda i, chip: (i + chip * (rstep // tm), coff // cs + chip * (cstep // cs))


def _cast_place(piece, w3, layer, prev, chip_arr, name):
    _, r, cc = w3.shape
    _, rf, cf = _BUFFERS[_BUF_INDEX[piece[1]]]
    tm = _tile(r, 256)
    own = _own_block(piece, tm)

    def body(s_ref, w_ref, *rest):
        rest[-1][...] = w_ref[...].astype(BF16)

    in_specs = [pl.BlockSpec((None, tm, cc), lambda i, s: (layer, i, 0))]
    args = [w3]
    aliases = {}
    if prev is not None:
        in_specs.append(pl.BlockSpec(memory_space=pl.ANY))
        args.append(prev)
        aliases = {2: 0}
    return pl.pallas_call(
        body, name=name,
        grid_spec=pltpu.PrefetchScalarGridSpec(
            num_scalar_prefetch=1, grid=(r // tm,), in_specs=in_specs,
            out_specs=pl.BlockSpec((tm, cc), lambda i, s: own(i, s[0]))),
        out_shape=jax.ShapeDtypeStruct((rf, cf), BF16), input_output_aliases=aliases,
        compiler_params=_params(("parallel",)),
    )(chip_arr, *args)


_GATHER_GROUPS = ((0, (0,)), (0, (1, 2, 3, 4, 5)), (1, (0,)), (1, (1, 2, 3, 4, 5)))
_SPLIT_EFFECT = pltpu.SideEffectType.DATAFLOW_SIDE_EFFECTING
SEM_SPEC = pl.BlockSpec(memory_space=pltpu.SEMAPHORE)
BF16_ROWS = 2 * SUBLANES


def _group_keys(g):
    layer, pieces = _GATHER_GROUPS[g]
    keys = []
    for p in pieces:
        if (_PIECES[p][1], layer) not in keys:
            keys.append((_PIECES[p][1], layer))
    return keys


def _half_view(ref, piece, j, c):
    _, _, rs, cs, rstep, cstep, coff = piece
    half = rs // 2
    return ref.at[pl.ds(pl.multiple_of(j * rstep + c * half, BF16_ROWS), half), pl.ds(coff + j * cstep, cs)]


def _for_my_chip(fn):
    x, y, _ = _pos()
    for mine in range(N_CHIPS):
        pl.when(2 * x + y == mine)(functools.partial(fn, mine, [j for j in range(N_CHIPS) if j != mine]))


def _gather_start(placed):
    keys = [k for g in range(len(_GATHER_GROUPS)) for k in _group_keys(g)]
    nb, ng = len(keys), len(_GATHER_GROUPS)

    def body(*refs):
        bufs = dict(zip(keys, refs[nb:2 * nb]))
        ssems = refs[2 * nb:2 * nb + ng]
        rsems = refs[2 * nb + ng:2 * nb + 2 * ng]
        token = refs[2 * nb + 2 * ng]
        _, _, c = _pos()

        def send(mine, others):
            for g, (layer, pieces) in enumerate(_GATHER_GROUPS):
                for k, p in enumerate(pieces):
                    view = _half_view(bufs[(_PIECES[p][1], layer)], _PIECES[p], mine, c)
                    for j in others:
                        _remote(view, view, ssems[g].at[k * N_CHIPS + j], rsems[g].at[k * N_CHIPS + mine],
                                (j // 2, j % 2, c)).start()

        _for_my_chip(send)
        token[...] = jnp.zeros(token.shape, token.dtype)

    sems = [pltpu.SemaphoreType.DMA((N_CHIPS * len(pieces),)) for _, pieces in _GATHER_GROUPS]
    shapes = [jax.ShapeDtypeStruct(a.shape, a.dtype) for a in placed]
    res = pl.pallas_call(
        body, name="gather_start",
        in_specs=[HBM_SPEC] * nb,
        out_specs=[HBM_SPEC] * nb + [SEM_SPEC] * (2 * ng) + [pl.BlockSpec(memory_space=pltpu.VMEM)],
        out_shape=shapes + sems + sems + [jax.ShapeDtypeStruct((SUBLANES, LANES), F32)],
        input_output_aliases={t: t for t in range(nb)},
        compiler_params=_params(has_side_effects=_SPLIT_EFFECT),
    )(*[pltpu.with_memory_space_constraint(a, pltpu.HBM) for a in placed])
    return dict(zip(keys, res[:nb])), res[nb:nb + ng], res[nb + ng:nb + 2 * ng], res[nb + 2 * ng]


def _gather_wait(g, bufs, ssem, rsem, after):
    layer, pieces = _GATHER_GROUPS[g]
    keys = _group_keys(g)
    nb = len(keys)

    def body(*refs):
        ssem_ref, rsem_ref = refs[nb], refs[nb + 1]
        land = dict(zip(keys, refs[nb + 3:]))
        _, _, c = _pos()

        def wait(mine, others):
            for k, p in enumerate(pieces):
                ref = land[(_PIECES[p][1], layer)]
                for j in others:
                    cp = _remote(_half_view(ref, _PIECES[p], mine, c), _half_view(ref, _PIECES[p], j, c),
                                 ssem_ref.at[k * N_CHIPS + j], rsem_ref.at[k * N_CHIPS + j], (j // 2, j % 2, c))
                    cp.wait_send()
                    cp.wait_recv()

        _for_my_chip(wait)

    return pl.pallas_call(
        body, name="gather_wait_g%d" % g,
        in_specs=[HBM_SPEC] * nb + [SEM_SPEC, SEM_SPEC, pl.BlockSpec(memory_space=pl.ANY)],
        out_specs=[HBM_SPEC] * nb,
        out_shape=[jax.ShapeDtypeStruct(a.shape, a.dtype) for a in bufs],
        input_output_aliases={t: t for t in range(nb)},
        compiler_params=_params(has_side_effects=_SPLIT_EFFECT),
    )(*bufs, ssem, rsem, after)


def _gather_forward(g, bufs):
    layer, pieces = _GATHER_GROUPS[g]
    keys = _group_keys(g)
    nb = len(keys)

    def body(*refs):
        land = dict(zip(keys, refs[nb:2 * nb]))
        ssem, rsem = refs[2 * nb:]
        x, y, c = _pos()

        def forward(mine, others):
            cps = []
            for k, p in enumerate(pieces):
                ref = land[(_PIECES[p][1], layer)]
                for j in others:
                    view = _half_view(ref, _PIECES[p], j, c)
                    cp = _remote(view, view, ssem.at[k * N_CHIPS + j], rsem.at[k * N_CHIPS + j], (x, y, 1 - c))
                    cp.start()
                    cps.append(cp)
            for k, p in enumerate(pieces):
                ref = land[(_PIECES[p][1], layer)]
                for j in others:
                    view = _half_view(ref, _PIECES[p], j, 1 - c)
                    _remote(view, view, ssem.at[k * N_CHIPS + j], rsem.at[k * N_CHIPS + j], (x, y, 1 - c)).wait_recv()
            for cp in cps:
                cp.wait_send()

        _for_my_chip(forward)

    nsem = N_CHIPS * len(pieces)
    return pl.pallas_call(
        body, name="gather_forward_g%d" % g,
        in_specs=[HBM_SPEC] * nb, out_specs=[HBM_SPEC] * nb,
        out_shape=[jax.ShapeDtypeStruct(a.shape, a.dtype) for a in bufs],
        input_output_aliases={t: t for t in range(nb)},
        scratch_shapes=[pltpu.SemaphoreType.DMA((nsem,)), pltpu.SemaphoreType.DMA((nsem,))],
        compiler_params=_params(has_side_effects=True),
    )(*bufs)


def _swap_other_layer(dwb):
    def body(*refs):
        src = refs[:N_BUFFERS]
        dst = refs[N_BUFFERS:2 * N_BUFFERS]
        ssem, rsem = refs[2 * N_BUFFERS:]
        x, y, c = _pos()
        cps = [_remote(src[t].at[1 - c], dst[t], ssem.at[t], rsem.at[t], (x, y, 1 - c)) for t in range(N_BUFFERS)]
        for cp in cps:
            cp.start()
        for cp in cps:
            cp.wait()

    return pl.pallas_call(
        body, name="swap_other_layer",
        in_specs=[HBM_SPEC] * N_BUFFERS, out_specs=[HBM_SPEC] * N_BUFFERS,
        out_shape=[jax.ShapeDtypeStruct((r, cc), BF16) for (_, r, cc) in _BUFFERS],
        scratch_shapes=[pltpu.SemaphoreType.DMA((N_BUFFERS,)), pltpu.SemaphoreType.DMA((N_BUFFERS,))],
        compiler_params=_params(has_side_effects=True),
    )(*dwb)


def _chip_partial(dw, got, c_idx, name):
    _, r, cc = dw.shape
    tm, tn = _tile(r, 512), _tile(cc)

    def body(c_ref, dw_ref, got_ref, out_ref):
        out_ref[...] = (dw_ref[...] + got_ref[...].astype(F32)).astype(BF16)

    return pl.pallas_call(
        body, name=name,
        grid_spec=pltpu.PrefetchScalarGridSpec(
            num_scalar_prefetch=1, grid=(r // tm, cc // tn),
            in_specs=[pl.BlockSpec((None, tm, tn), lambda i, j, cr: (cr[0], i, j)),
                      pl.BlockSpec((tm, tn), lambda i, j, cr: (i, j))],
            out_specs=pl.BlockSpec((tm, tn), lambda i, j, cr: (i, j))),
        out_shape=jax.ShapeDtypeStruct((r, cc), BF16),
        compiler_params=_params(("parallel", "parallel")),
    )(c_idx, dw, got)


def _scatter_partials(partials):
    def body(*refs):
        src = refs[:N_BUFFERS]
        dst = refs[N_BUFFERS:N_BUFFERS + N_PIECES]
        ssem, rsem = refs[N_BUFFERS + N_PIECES:]
        x, y, c = _pos()
        my_chip = 2 * x + y
        for mine in range(N_CHIPS):
            @pl.when(my_chip == mine)
            def _(mine=mine):
                others = [j for j in range(N_CHIPS) if j != mine]
                sends = []
                for p in range(N_PIECES):
                    buf = src[_BUF_INDEX[_PIECES[p][1]]]
                    for j in others:
                        cp = _remote(_piece_view(buf, _PIECES[p], j), dst[p].at[mine], ssem.at[p * N_CHIPS + j],
                                     rsem.at[p * N_CHIPS + mine], (j // 2, j % 2, c))
                        cp.start()
                        sends.append(cp)
                for p in range(N_PIECES):
                    for j in others:
                        _remote(dst[p].at[j], dst[p].at[j], ssem.at[p * N_CHIPS + j],
                                rsem.at[p * N_CHIPS + j], (j // 2, j % 2, c)).wait_recv()
                for cp in sends:
                    cp.wait_send()

    nsem = N_PIECES * N_CHIPS
    return pl.pallas_call(
        body, name="scatter_partials",
        in_specs=[HBM_SPEC] * N_BUFFERS, out_specs=[HBM_SPEC] * N_PIECES,
        out_shape=[jax.ShapeDtypeStruct((N_CHIPS, p[2], p[3]), BF16) for p in _PIECES],
        scratch_shapes=[pltpu.SemaphoreType.DMA((nsem,)), pltpu.SemaphoreType.DMA((nsem,))],
        compiler_params=_params(has_side_effects=True),
    )(*partials)


def _reduce_shard(piece, dw, got, recv, idx, name):
    _, _, rs, cs, _, _, _ = piece
    tm = _tile(rs, 256)
    own = _own_block(piece, tm)

    def own_map(i, s):
        return own(i, s[1])

    def body(s_ref, dw_ref, got_ref, r1, r2, r3, out_ref):
        acc = dw_ref[...] + got_ref[...].astype(F32)
        for r in (r1, r2, r3):
            acc = acc + r[...].astype(F32)
        out_ref[...] = acc

    def recv_map(k):
        return lambda i, s: ((s[1] + k) % N_CHIPS, i, 0)

    return pl.pallas_call(
        body, name=name,
        grid_spec=pltpu.PrefetchScalarGridSpec(
            num_scalar_prefetch=1, grid=(rs // tm,),
            in_specs=[pl.BlockSpec((None, tm, cs), lambda i, s: (s[0],) + own_map(i, s)),
                      pl.BlockSpec((tm, cs), own_map),
                      pl.BlockSpec((None, tm, cs), recv_map(1)),
                      pl.BlockSpec((None, tm, cs), recv_map(2)),
                      pl.BlockSpec((None, tm, cs), recv_map(3))],
            out_specs=pl.BlockSpec((None, tm, cs), lambda i, s: (s[0], i, 0))),
        out_shape=jax.ShapeDtypeStruct((DEPTH, rs, cs), F32),
        compiler_params=_params(("parallel",)),
    )(idx, dw, got, recv, recv, recv)


def _share_with_sibling(reduced):
    def body(*refs):
        buf = refs[N_PIECES:2 * N_PIECES]
        ssem, rsem = refs[2 * N_PIECES:]
        x, y, c = _pos()
        rem = [_remote(buf[p].at[c], buf[p].at[c], ssem.at[p], rsem.at[p], (x, y, 1 - c)) for p in range(N_PIECES)]
        for cp in rem:
            cp.start()
        for p in range(N_PIECES):
            rem[p].wait_send()
            _remote(buf[p].at[1 - c], buf[p].at[1 - c], ssem.at[p], rsem.at[p], (x, y, 1 - c)).wait_recv()

    return pl.pallas_call(
        body, name="share_with_sibling",
        in_specs=[HBM_SPEC] * N_PIECES, out_specs=[HBM_SPEC] * N_PIECES,
        out_shape=[jax.ShapeDtypeStruct((DEPTH, p[2], p[3]), F32) for p in _PIECES],
        input_output_aliases={t: t for t in range(N_PIECES)},
        scratch_shapes=[pltpu.SemaphoreType.DMA((N_PIECES,)), pltpu.SemaphoreType.DMA((N_PIECES,))],
        compiler_params=_params(has_side_effects=True),
    )(*reduced)


N_DEV = 8


def _all_gather_rows(v):
    rows = v.shape[0]

    def body(v_ref, out_ref, ssem, rsem, lsem):
        x, y, c = _pos()
        me, sibling = (x, y, c), (x, y, 1 - c)
        chips = [(1 - x, y), (x, 1 - y), (1 - x, 1 - y)]

        def slot(px, py, pc):
            return out_ref.at[4 * px + 2 * py + pc]

        def copy(k, block, to, src=None):
            return _remote(slot(*block) if src is None else src, slot(*block), ssem.at[k], rsem.at[k], to)

        mine = pltpu.make_async_copy(v_ref, slot(*me), lsem)
        mine.start()
        first = [copy(0, me, sibling, src=v_ref)]
        first += [copy(1 + j, me, (*chip, c), src=v_ref) for j, chip in enumerate(chips)]
        for cp in first:
            cp.start()
        passed = [copy(4 + j, (*chip, c), sibling) for j, chip in enumerate(chips)]
        for j, chip in enumerate(chips):
            copy(1 + j, (*chip, c), me).wait_recv()
            passed[j].start()
        copy(0, sibling, me).wait_recv()
        for j, chip in enumerate(chips):
            copy(4 + j, (*chip, 1 - c), me).wait_recv()
        for cp in first + passed:
            cp.wait_send()
        mine.wait()

    return pl.pallas_call(
        body, name="all_gather_small",
        in_specs=[HBM_SPEC], out_specs=HBM_SPEC,
        out_shape=jax.ShapeDtypeStruct((N_DEV, rows, LANES), F32),
        scratch_shapes=[pltpu.SemaphoreType.DMA((7,)), pltpu.SemaphoreType.DMA((7,)), pltpu.SemaphoreType.DMA],
        compiler_params=_params(has_side_effects=True),
    )(v)


def _sum_slots(g, name):
    n, rows, _ = g.shape
    tm = _tile(rows, 512)

    def body(g_ref, out_ref):
        acc = g_ref[0]
        for k in range(1, n):
            acc = acc + g_ref[k]
        out_ref[...] = acc

    return pl.pallas_call(
        body, name=name, grid=(rows // tm,),
        in_specs=[pl.BlockSpec((n, tm, LANES), lambda i: (0, i, 0))],
        out_specs=pl.BlockSpec((tm, LANES), lambda i: (i, 0)),
        out_shape=jax.ShapeDtypeStruct((rows, LANES), F32),
        compiler_params=_params(("parallel",)),
    )(g)


_SMALL = ("ln_mix_g", "ret_log_gamma", "ssm_a_re", "ssm_a_im", "ssm_log_dt", "ssm_b_re", "ssm_b_im",
          "ssm_c_re", "ssm_c_im", "ssm_d", "b_glu", "ln_ffn_g", "ln_final_g")
_FLAT_ALIGN = LANES * LANES


def _flatten_small(d):
    parts = []
    for n in _SMALL:
        f = d[n].reshape(-1)
        parts.append(jnp.pad(f, (0, (-f.shape[0]) % _FLAT_ALIGN)))
    return jnp.concatenate(parts).reshape(-1, LANES)


def _unflatten_small(flat, like):
    out, row = {}, 0
    for n in _SMALL:
        size = math.prod(like[n].shape)
        rows = (size + (-size) % _FLAT_ALIGN) // LANES
        part = lax.optimization_barrier(flat[row:row + rows])
        out[n] = part.reshape(-1)[:size].reshape(like[n].shape)
        row += rows
    return out


_BIG = ("w_in", "w_glu", "w_out", "w_ffn_gate", "w_ffn_up", "w_ffn_down")
_WEIGHTS = ("ln_mix_g", "w_in", "ret_log_gamma", "ssm_a_re", "ssm_a_im", "ssm_log_dt", "ssm_b_re", "ssm_b_im",
            "ssm_c_re", "ssm_c_im", "ssm_d", "w_glu", "b_glu", "w_out", "ln_ffn_g", "w_ffn_gate", "w_ffn_up",
            "w_ffn_down", "ln_final_g")


def _rope_tables(seq):
    half = QK_DIM // 2
    inv = 1.0 / (ROPE_BASE ** (jnp.arange(half, dtype=F32) / half))
    ang = jnp.arange(seq, dtype=F32)[:, None] * inv[None, :]
    return jnp.cos(ang), jnp.sin(ang)


def _step(w, m, v, x, target):
    seq = x.shape[0]
    seg_len = float(seq // SEGMENTS)
    c_idx = lax.axis_index("c").astype(jnp.int32)
    chip_idx = (2 * lax.axis_index("x") + lax.axis_index("y")).astype(jnp.int32)
    c_arr = jnp.stack([c_idx])
    idx_arr = jnp.stack([c_idx, chip_idx])

    chip_arr = jnp.stack([chip_idx])
    placed = {}
    for piece in _PIECES:
        for layer in range(DEPTH):
            key = (piece[1], layer)
            placed[key] = _cast_place(piece, w[piece[0]], layer, placed.get(key), chip_arr,
                                      "cast_%s_l%d" % (piece[0], layer))
    keys = [k for g in range(len(_GATHER_GROUPS)) for k in _group_keys(g)]
    flying, ssems, rsems, token = _gather_start([placed[k] for k in keys])
    wf = {b[0]: [None] * DEPTH for b in _BUFFERS}

    def arrive(g, after):
        ks = _group_keys(g)
        landed = _gather_wait(g, [flying[k] for k in ks], ssems[g], rsems[g], after)
        for k, a in zip(ks, _gather_forward(g, landed)):
            wf[k[0]][k[1]] = a

    cos, sin = _rope_tables(seq)

    saved = []
    xc = x + token[0, 0]
    for i in range(DEPTH):
        t = "_l%d" % i
        s = {"x_in": xc}
        s["h"] = _rms_fwd(xc, w["ln_mix_g"][i:i + 1], "rms_mix" + t)
        arrive(2 * i, s["h"])
        s["proj"] = _matmul(s["h"], wf["in"][i], "nn", [F32], name="mm_in" + t)[0]
        s["qr"], s["kr"] = _rot_fwd(s["proj"], cos, sin, "rot" + t)
        s["lg"] = jnp.broadcast_to(w["ret_log_gamma"][i].T[:, :, None], (HEADS, 2, LANES))
        s["y"] = _ret_fwd(s["qr"], s["kr"], s["proj"], s["lg"], "ret" + t)
        s5_raw = (w["ssm_a_re"][i], w["ssm_a_im"][i], w["ssm_log_dt"][i], w["ssm_b_re"][i], w["ssm_b_im"][i])
        disc, s["disc_vjp"] = jax.vjp(functools.partial(_s5_discretize, seg_len=seg_len), *s5_raw)
        bblk, cblk, lam = _s5_pack(*disc, w["ssm_c_re"][i], w["ssm_c_im"][i])
        s["s5"] = (bblk.astype(BF16), cblk.astype(BF16), lam)
        s["s5y"] = _s5_fwd(s["proj"], *s["s5"], "s5" + t)
        s["ret"], s["ysg"], s["ysgb"] = _post1_fwd(s["y"], s["proj"], s["s5y"], w["ssm_d"][i:i + 1], "post" + t)
        arrive(2 * i + 1, s["ysgb"])
        s["z"] = _matmul(s["ysgb"], wf["glu"][i], "nn", [F32], name="mm_glu" + t)[0]
        s["merged"] = _merge_fwd(s["z"], s["ysg"], s["proj"], s["ret"], w["b_glu"][i:i + 1], "merge" + t)
        s["x1"] = _matmul(s["merged"], wf["out"][i], "nn", [F32], add=xc, name="mm_out" + t)[0]
        s["h2"] = _rms_fwd(s["x1"], w["ln_ffn_g"][i:i + 1], "rms_ffn" + t)
        s["ab"] = _matmul(s["h2"], wf["gu"][i], "nn", [F32], name="mm_gu" + t)[0]
        s["f"] = _glu_fwd(s["ab"], "glu" + t)
        xc = _matmul(s["f"], wf["down"][i], "nn", [F32], add=s["x1"], name="mm_down" + t)[0]
        saved.append(s)

    dx, dxb, loss_row, dg_final = _loss_stage(xc, target, w["ln_final_g"][None, :], "loss")
    loss = lax.psum(loss_row[0, 0], ("x", "y", "c"))

    g_small = {"ln_final_g": dg_final[0]}
    per_layer = {n: [None] * DEPTH for n in _SMALL if n != "ln_final_g"}
    dws = {b[0]: None for b in _BUFFERS}

    def dw_mm(a, b, buf, i, name):
        dws[buf] = _matmul(a, b, "tn", [F32, BF16], stack=dws[buf] or (None, None), layer=i, name=name)

    for i in reversed(range(DEPTH)):
        t = "_l%d" % i
        s = saved[i]
        dw_mm(s["f"], dxb, "down", i, "dw_down" + t)
        df = _matmul(dxb, wf["down"][i], "nt", [F32], name="dx_down" + t)[0]
        dab = _glu_bwd(s["ab"], df, "glu_bwd" + t)
        dw_mm(s["h2"], dab, "gu", i, "dw_gu" + t)
        dh2 = _matmul(dab, wf["gu"][i], "nt", [F32], name="dx_gu" + t)[0]
        dx1, dx1b, dg = _rms_bwd(s["x1"], dh2, dx, w["ln_ffn_g"][i:i + 1], "rms_ffn_bwd" + t)
        per_layer["ln_ffn_g"][i] = dg[0]

        dw_mm(s["merged"], dx1b, "out", i, "dw_out" + t)
        dmerged = _matmul(dx1b, wf["out"][i], "nt", [F32], name="dx_out" + t)[0]
        dz, dys_part, dgs, db = _merge_bwd(s["z"], s["ysg"], s["proj"], s["ret"], dmerged, w["b_glu"][i:i + 1],
                                           "merge_bwd" + t)
        per_layer["b_glu"][i] = db[0]
        dw_mm(s["ysgb"], dz, "glu", i, "dw_glu" + t)
        dys = _matmul(dz, wf["glu"][i], "nt", [F32], add=dys_part, name="dx_glu" + t)[0]
        dy, dgg, dgr, ds5, du_part, dd = _post1_bwd(s["y"], s["proj"], s["s5y"], dmerged, dys,
                                                    w["ssm_d"][i:i + 1], "post_bwd" + t)
        per_layer["ssm_d"][i] = dd[0]
        du, dbblk, dcblk, dlam = _s5_bwd(s["proj"], ds5, du_part, *s["s5"], "s5_bwd" + t)
        dlr, dli, dbr, dbi, dcr, dci = _s5_unpack(dbblk, dcblk, dlam)
        zeros = jnp.zeros_like(dlr)
        da_re, da_im, dlog_dt, db_re, db_im = s["disc_vjp"]((dlr, dli, zeros, zeros, dbr, dbi))
        for n, val in (("ssm_a_re", da_re), ("ssm_a_im", da_im), ("ssm_log_dt", dlog_dt), ("ssm_b_re", db_re),
                       ("ssm_b_im", db_im), ("ssm_c_re", dcr), ("ssm_c_im", dci)):
            per_layer[n][i] = val
        dqr, dkr, dv, dlg = _ret_bwd(s["qr"], s["kr"], s["proj"], dy, s["lg"], "ret_bwd" + t)
        per_layer["ret_log_gamma"][i] = dlg[:, :, 0].T
        dqkv = _rot_bwd(dqr, dkr, dv, cos, sin, "rot_bwd" + t)
        dproj = jnp.concatenate([dqkv, dgg, du, dgr, dgs], axis=1)
        dw_mm(s["h"], dproj, "in", i, "dw_in" + t)
        dh = _matmul(dproj, wf["in"][i], "nt", [F32], name="dx_in" + t)[0]
        dx, dxb, dg = _rms_bwd(s["x_in"], dh, dx1, w["ln_mix_g"][i:i + 1], "rms_mix_bwd" + t)
        per_layer["ln_mix_g"][i] = dg[0]

    got = _swap_other_layer([dws[b[0]][1] for b in _BUFFERS])
    partials = [_chip_partial(dws[b[0]][0], got[k], c_arr, "chip_partial_" + b[0]) for k, b in enumerate(_BUFFERS)]
    recv = _scatter_partials(partials)
    reduced = []
    for p, piece in enumerate(_PIECES):
        k = _BUF_INDEX[piece[1]]
        reduced.append(_reduce_shard(piece, dws[piece[1]][0], got[k], recv[p], idx_arr, "reduce_" + piece[0]))
    g_big = dict(zip([p[0] for p in _PIECES], _share_with_sibling(reduced)))

    for n in per_layer:
        g_small[n] = jnp.stack(per_layer[n])
    g_flat = _sum_slots(_all_gather_rows(_flatten_small(g_small)), "sum_small")

    grads, delta, new_m, new_v = {}, {}, {}, {}
    for n in _BIG:
        d, r, cc = w[n].shape
        two_d = lambda a: a.reshape(d * r, cc)
        dl, mn, vn = _adamw(two_d(w[n]), two_d(g_big[n]), two_d(m[n]), two_d(v[n]), "adamw_" + n)
        grads[n], delta[n], new_m[n], new_v[n] = g_big[n], dl.reshape(d, r, cc), mn.reshape(d, r, cc), vn.reshape(d, r, cc)
    dl, mn, vn = _adamw(_flatten_small(w), g_flat, _flatten_small(m), _flatten_small(v), "adamw_small")
    for dst, flat in ((grads, g_flat), (delta, dl), (new_m, mn), (new_v, vn)):
        dst.update(_unflatten_small(flat, w))
    return loss, dx, grads, delta, new_m, new_v


def kernel(x, ln_mix_g, w_in, ret_log_gamma, ssm_a_re, ssm_a_im, ssm_log_dt, ssm_b_re, ssm_b_im, ssm_c_re, ssm_c_im, ssm_d, w_glu, b_glu, w_out, ln_ffn_g, w_ffn_gate, w_ffn_up, w_ffn_down, ln_final_g, loss_target, m_ln_mix_g, m_w_in, m_ret_log_gamma, m_ssm_a_re, m_ssm_a_im, m_ssm_log_dt, m_ssm_b_re, m_ssm_b_im, m_ssm_c_re, m_ssm_c_im, m_ssm_d, m_w_glu, m_b_glu, m_w_out, m_ln_ffn_g, m_w_ffn_gate, m_w_ffn_up, m_w_ffn_down, m_ln_final_g, v_ln_mix_g, v_w_in, v_ret_log_gamma, v_ssm_a_re, v_ssm_a_im, v_ssm_log_dt, v_ssm_b_re, v_ssm_b_im, v_ssm_c_re, v_ssm_c_im, v_ssm_d, v_w_glu, v_b_glu, v_w_out, v_ln_ffn_g, v_w_ffn_gate, v_w_ffn_up, v_w_ffn_down, v_ln_final_g):
    given = dict(locals())
    w = {n: given[n] for n in _WEIGHTS}
    m = {n: given["m_" + n] for n in _WEIGHTS}
    v = {n: given["v_" + n] for n in _WEIGHTS}
    loss, dx, grads, delta, new_m, new_v = _step(w, m, v, x[0], loss_target[0])
    return (loss, dx[None], *[grads[n] for n in _WEIGHTS], *[delta[n] for n in _WEIGHTS],
            *[new_m[n] for n in _WEIGHTS], *[new_v[n] for n in _WEIGHTS])
```

```python
import functools
import math

import jax
import jax.numpy as jnp
from jax import lax
from jax.experimental import pallas as pl
from jax.experimental.pallas import tpu as pltpu

F32 = jnp.float32
BF16 = jnp.bfloat16

D_MODEL = 2048
DEPTH = 2
HEADS = 4
QK_DIM = 256
V_DIM = 512
QK_WIDTH = HEADS * QK_DIM
ROPE_BASE = 10000.0
GROUP = 16
N_GROUPS = D_MODEL // GROUP
N_STATE = 64
D_FF = 5632
IN_WIDTH = 2 * QK_WIDTH + 5 * D_MODEL
EPS = 1e-6
N_CHIPS = 4

ADAM_LR = 0.001
ADAM_B1 = 0.9
ADAM_B2 = 0.999
ADAM_EPS = 1e-08
ADAM_WD = 0.01
ADAM_STEP = 10

LANES = 128
SUBLANES = 8
VMEM_LIMIT = 56 * 1024 * 1024
SEGMENTS = SUBLANES
GROUPS_PER_TILE = LANES // GROUP
STATE_COLS = GROUPS_PER_TILE * N_STATE
N_TILES = D_MODEL // LANES
SCAN_UNROLL = 4

MESH = pl.DeviceIdType.MESH
HBM_SPEC = pl.BlockSpec(memory_space=pltpu.HBM)


def _params(sem=None, **kw):
    return pltpu.CompilerParams(dimension_semantics=sem, vmem_limit_bytes=VMEM_LIMIT, **kw)


def _tile(n, cap=1024):
    for t in (1024, 512, 256, 128, 64):
        if t <= cap and n % t == 0:
            return t
    raise ValueError(n)


def _rows_call(fn, rows, pars, row_outs, par_outs, *, tm, name):
    m = rows[0][0].shape[0]
    nr, npar, nro, npo = len(rows), len(pars), len(row_outs), len(par_outs)

    def body(*refs):
        rin = refs[:nr]
        pin = refs[nr:nr + npar]
        rout = refs[nr + npar:nr + npar + nro]
        pout = refs[nr + npar + nro:]
        res = fn(*[r[...] for r in rin], *[p[...] for p in pin])
        if not isinstance(res, (tuple, list)):
            res = (res,)
        for r, v in zip(rout, res[:nro]):
            r[...] = v.astype(r.dtype)
        if npo:
            @pl.when(pl.program_id(0) == 0)
            def _():
                for p in pout:
                    p[...] = jnp.zeros(p.shape, p.dtype)
            for p, v in zip(pout, res[nro:]):
                p[...] += v

    in_specs = [pl.BlockSpec((tm, w), functools.partial(lambda cb, i: (i, cb), cb)) for (_, w, cb) in rows]
    in_specs += [pl.BlockSpec(p.shape, lambda i: (0, 0)) for p in pars]
    out_specs = [pl.BlockSpec((tm, w), lambda i: (i, 0)) for (w, _) in row_outs]
    out_specs += [pl.BlockSpec(s, lambda i: (0, 0)) for s in par_outs]
    out_shape = [jax.ShapeDtypeStruct((m, w), dt) for (w, dt) in row_outs]
    out_shape += [jax.ShapeDtypeStruct(s, F32) for s in par_outs]
    res = pl.pallas_call(
        body, name=name, grid=(m // tm,), in_specs=in_specs, out_specs=out_specs, out_shape=out_shape,
        compiler_params=_params(("arbitrary",) if npo else ("parallel",)),
    )(*[a for (a, _, _) in rows], *pars)
    return res


def _f32(*vals):
    return [v.astype(F32) for v in vals]


def _f_rms(x, g):
    r = lax.rsqrt(jnp.mean(x * x, axis=-1, keepdims=True) + EPS)
    return x * r * g


def _rms_fwd(x, g, name):
    return _rows_call(lambda xv, gv: _f_rms(xv, gv), [(x, D_MODEL, 0)], [g], [(D_MODEL, BF16)], [],
                      tm=256, name=name)[0]


def _rms_bwd(x, dh, dres, g, name):
    def fn(xv, dhv, drv, gv):
        _, vjp = jax.vjp(_f_rms, xv, gv)
        dx, dg = vjp(dhv)
        dx = dx + drv
        return dx, dx, dg
    return _rows_call(fn, [(x, D_MODEL, 0), (dh, D_MODEL, 0), (dres, D_MODEL, 0)], [g],
                      [(D_MODEL, F32), (D_MODEL, BF16)], [(1, D_MODEL)], tm=256, name=name)


def _rot_heads(xv, cos, sin, scale):
    half = QK_DIM // 2
    outs = []
    for h in range(HEADS):
        x1 = xv[:, h * QK_DIM:h * QK_DIM + half]
        x2 = xv[:, h * QK_DIM + half:(h + 1) * QK_DIM]
        outs += [(x1 * cos - x2 * sin) * scale, (x1 * sin + x2 * cos) * scale]
    return jnp.concatenate(outs, axis=1)


def _rot_fwd(proj, cos, sin, name):
    def fn(q, k, cv, sv):
        return _rot_heads(q, cv, sv, 1.0), _rot_heads(k, cv, sv, QK_DIM ** -0.5)
    return _rows_call(fn, [(proj, QK_WIDTH, 0), (proj, QK_WIDTH, 1), (cos, LANES, 0), (sin, LANES, 0)], [],
                      [(QK_WIDTH, BF16), (QK_WIDTH, BF16)], [], tm=256, name=name)


def _rot_bwd(dqr, dkr, dv, cos, sin, name):
    def fn(dq, dk, dvv, cv, sv):
        return jnp.concatenate([_rot_heads(dq, cv, -sv, 1.0), _rot_heads(dk, cv, -sv, QK_DIM ** -0.5), dvv], axis=1)
    return _rows_call(fn, [(dqr, QK_WIDTH, 0), (dkr, QK_WIDTH, 0), (dv, D_MODEL, 0), (cos, LANES, 0), (sin, LANES, 0)],
                      [], [(2 * QK_WIDTH + D_MODEL, BF16)], [], tm=256, name=name)[0]


def _f_post1(y0, y1, y2, y3, g, gr, s5, u, dsk):
    yn = [yh * lax.rsqrt(jnp.mean(yh * yh, axis=-1, keepdims=True) + EPS) for yh in (y0, y1, y2, y3)]
    ret = jax.nn.sigmoid(gr) * (jax.nn.silu(g) * jnp.concatenate(yn, axis=1))
    ysg = jax.nn.gelu(s5 + dsk * u)
    return ret, ysg


def _post1_rows(y, proj, s5y):
    rows = [(y, V_DIM, h) for h in range(HEADS)]
    rows += [(proj, D_MODEL, 2), (proj, D_MODEL, 4), (s5y, D_MODEL, 0), (proj, D_MODEL, 3)]
    return rows


def _post1_fwd(y, proj, s5y, dsk, name):
    def fn(*vals):
        ret, ysg = _f_post1(*vals)
        return ret, ysg, ysg
    return _rows_call(fn, _post1_rows(y, proj, s5y), [dsk],
                      [(D_MODEL, F32), (D_MODEL, F32), (D_MODEL, BF16)], [], tm=128, name=name)


def _post1_bwd(y, proj, s5y, dret, dys, dsk, name):
    def fn(*vals):
        prim = vals[:8] + (vals[10],)
        _, vjp = jax.vjp(_f_post1, *prim)
        gy0, gy1, gy2, gy3, gg, ggr, gs5, gu, gd = vjp((vals[8], vals[9]))
        return jnp.concatenate([gy0, gy1, gy2, gy3], axis=1), gg, ggr, gs5, gu, gd
    rows = _post1_rows(y, proj, s5y) + [(dret, D_MODEL, 0), (dys, D_MODEL, 0)]
    return _rows_call(fn, rows, [dsk],
                      [(D_MODEL, BF16), (D_MODEL, BF16), (D_MODEL, BF16), (D_MODEL, F32), (D_MODEL, F32)],
                      [(1, D_MODEL)], tm=128, name=name)


def _f_merge(z, ysg, gs, ret, b):
    return ret + jax.nn.sigmoid(gs) * (ysg * jax.nn.sigmoid(z + b))


def _merge_fwd(z, ysg, proj, ret, b, name):
    return _rows_call(_f_merge, [(z, D_MODEL, 0), (ysg, D_MODEL, 0), (proj, D_MODEL, 5), (ret, D_MODEL, 0)], [b],
                      [(D_MODEL, BF16)], [], tm=128, name=name)[0]


def _merge_bwd(z, ysg, proj, ret, dm, b, name):
    def fn(zv, yv, gv, rv, dmv, bv):
        _, vjp = jax.vjp(_f_merge, zv, yv, gv, rv, bv)
        gz, gy, gg, _, gb = vjp(dmv)
        return gz, gy, gg, gb
    rows = [(z, D_MODEL, 0), (ysg, D_MODEL, 0), (proj, D_MODEL, 5), (ret, D_MODEL, 0), (dm, D_MODEL, 0)]
    return _rows_call(fn, rows, [b], [(D_MODEL, BF16), (D_MODEL, F32), (D_MODEL, BF16)], [(1, D_MODEL)],
                      tm=128, name=name)


def _f_glu(a, b):
    return jax.nn.silu(a) * b


def _glu_fwd(ab, name):
    return _rows_call(_f_glu, [(ab, D_FF, 0), (ab, D_FF, 1)], [], [(D_FF, BF16)], [], tm=128, name=name)[0]


def _glu_bwd(ab, df, name):
    def fn(a, b, d):
        _, vjp = jax.vjp(_f_glu, a, b)
        ga, gb = vjp(d)
        return jnp.concatenate([ga, gb], axis=1)
    return _rows_call(fn, [(ab, D_FF, 0), (ab, D_FF, 1), (df, D_FF, 0)], [], [(2 * D_FF, BF16)], [],
                      tm=128, name=name)[0]


def _loss_stage(x, tgt, g, name):
    def fn(xv, tv, gv):
        def lf(xx, gg):
            err = _f_rms(xx, gg) - tv
            row = jnp.mean(err * err, axis=-1, keepdims=True)
            return 0.5 * jnp.sum(row, axis=0, keepdims=True)
        l, vjp = jax.vjp(lf, xv, gv)
        dx, dg = vjp(jnp.ones((1, 1), F32))
        return dx, dx, jnp.broadcast_to(l, (1, LANES)), dg
    return _rows_call(fn, [(x, D_MODEL, 0), (tgt, D_MODEL, 0)], [g], [(D_MODEL, F32), (D_MODEL, BF16)],
                      [(1, LANES), (1, D_MODEL)], tm=256, name=name)


def _adamw(w, g, m, v, name):
    rows, cols = w.shape

    def fn(wv, gv, mv, vv):
        mn = ADAM_B1 * mv + (1.0 - ADAM_B1) * gv
        vn = ADAM_B2 * vv + (1.0 - ADAM_B2) * (gv * gv)
        m_hat = mn / (1.0 - ADAM_B1 ** ADAM_STEP)
        v_hat = vn / (1.0 - ADAM_B2 ** ADAM_STEP)
        delta = -ADAM_LR * (m_hat / (jnp.sqrt(v_hat) + ADAM_EPS) + ADAM_WD * wv)
        return delta, mn, vn
    tm = _tile(rows, 128 if cols > D_FF // N_CHIPS else (256 if cols > LANES else 512))
    return _rows_call(fn, [(w, cols, 0), (g, cols, 0), (m, cols, 0), (v, cols, 0)], [],
                      [(cols, F32)] * 3, [], tm=tm, name=name)


def _matmul(a, b, mode, out_dtypes, *, name, add=None):
    if mode == "nn":
        (m, k), (_, n) = a.shape, b.shape
    elif mode == "nt":
        (m, k), (n, _) = a.shape, b.shape
    else:
        (k, m), (_, n) = a.shape, b.shape
    tm, tn, tk = _tile(m), _tile(n), _tile(k)
    nk = k // tk
    if mode == "nn":
        a_spec = pl.BlockSpec((tm, tk), lambda i, j, kk: (i, kk))
        b_spec = pl.BlockSpec((tk, tn), lambda i, j, kk: (kk, j))
        dims = (((1,), (0,)), ((), ()))
    elif mode == "nt":
        a_spec = pl.BlockSpec((tm, tk), lambda i, j, kk: (i, kk))
        b_spec = pl.BlockSpec((tn, tk), lambda i, j, kk: (j, kk))
        dims = (((1,), (1,)), ((), ()))
    else:
        a_spec = pl.BlockSpec((tk, tm), lambda i, j, kk: (kk, i))
        b_spec = pl.BlockSpec((tk, tn), lambda i, j, kk: (kk, j))
        dims = (((0,), (0,)), ((), ()))
    n_out = len(out_dtypes)
    has_add = add is not None

    def body(*refs):
        a_ref, b_ref = refs[0], refs[1]
        add_ref = refs[2] if has_add else None
        outs = refs[2 + has_add:2 + has_add + n_out]
        acc = refs[-1]
        kk = pl.program_id(2)

        @pl.when(kk == 0)
        def _():
            acc[...] = jnp.zeros(acc.shape, F32)

        acc[...] += lax.dot_general(a_ref[...], b_ref[...], dims, preferred_element_type=F32)

        @pl.when(kk == nk - 1)
        def _():
            r = acc[...]
            if has_add:
                r = r + add_ref[...]
            for o in outs:
                o[...] = r.astype(o.dtype)

    in_specs = [a_spec, b_spec]
    args = [a, b]
    if has_add:
        in_specs.append(pl.BlockSpec((tm, tn), lambda i, j, kk: (i, j)))
        args.append(add)
    return pl.pallas_call(
        body, name=name, grid=(m // tm, n // tn, nk), in_specs=in_specs,
        out_specs=[pl.BlockSpec((tm, tn), lambda i, j, kk: (i, j))] * n_out,
        out_shape=[jax.ShapeDtypeStruct((m, n), dt) for dt in out_dtypes],
        scratch_shapes=[pltpu.VMEM((tm, tn), F32)],
        compiler_params=_params(("parallel", "parallel", "arbitrary")),
    )(*args)


RET_TQ = 512


def _decay(lg_ref, i, tq, seq):
    n_idx = i * tq + lax.broadcasted_iota(jnp.int32, (tq, seq), 0)
    m_idx = lax.broadcasted_iota(jnp.int32, (tq, seq), 1)
    diff = (n_idx - m_idx).astype(F32)
    lgf = lg_ref[0, 0:1, 0:1]
    lgb = lg_ref[0, 1:2, 0:1]
    causal = diff >= 0
    return jnp.exp(jnp.where(causal, lgf * diff, -lgb * diff)), diff, causal


_NT = (((1,), (1,)), ((), ()))
_TN = (((0,), (0,)), ((), ()))


def _ret_fwd(qr, kr, proj, lg, name):
    seq = qr.shape[0]
    tq = RET_TQ
    v_blk0 = (2 * QK_WIDTH) // V_DIM

    def body(q_ref, k_ref, v_ref, lg_ref, y_ref):
        i = pl.program_id(1)
        s = lax.dot_general(q_ref[...], k_ref[...], _NT, preferred_element_type=F32)
        dm, _, _ = _decay(lg_ref, i, tq, seq)
        p = (s * dm).astype(BF16)
        y_ref[...] = jnp.dot(p, v_ref[...].astype(BF16), preferred_element_type=F32)

    return pl.pallas_call(
        body, name=name, grid=(HEADS, seq // tq),
        in_specs=[pl.BlockSpec((tq, QK_DIM), lambda h, i: (i, h)),
                  pl.BlockSpec((seq, QK_DIM), lambda h, i: (0, h)),
                  pl.BlockSpec((seq, V_DIM), lambda h, i: (0, v_blk0 + h)),
                  pl.BlockSpec((1, 2, LANES), lambda h, i: (h, 0, 0))],
        out_specs=pl.BlockSpec((tq, V_DIM), lambda h, i: (i, h)),
        out_shape=jax.ShapeDtypeStruct((seq, HEADS * V_DIM), F32),
        compiler_params=_params(("parallel", "parallel")),
    )(qr, kr, proj, lg)


def _ret_bwd(qr, kr, proj, dy, lg, name):
    seq = qr.shape[0]
    tq = RET_TQ
    v_blk0 = (2 * QK_WIDTH) // V_DIM

    def body(q_ref, k_ref, v_ref, dy_ref, lg_ref, dq_ref, dk_ref, dv_ref, dlg_ref):
        i = pl.program_id(1)

        @pl.when(i == 0)
        def _():
            dk_ref[...] = jnp.zeros(dk_ref.shape, F32)
            dv_ref[...] = jnp.zeros(dv_ref.shape, F32)
            dlg_ref[...] = jnp.zeros(dlg_ref.shape, F32)

        q = q_ref[...]
        k = k_ref[...]
        vb = v_ref[...].astype(BF16)
        dyb = dy_ref[...]
        s = lax.dot_general(q, k, _NT, preferred_element_type=F32)
        dm, diff, causal = _decay(lg_ref, i, tq, seq)
        p = s * dm
        dp = lax.dot_general(dyb, vb, _NT, preferred_element_type=F32)
        dv_ref[...] += lax.dot_general(p.astype(BF16), dyb, _TN, preferred_element_type=F32)
        ds = (dp * dm).astype(BF16)
        dq_ref[...] = jnp.dot(ds, k, preferred_element_type=F32)
        dk_ref[...] += lax.dot_general(ds, q, _TN, preferred_element_type=F32)
        gd = dp * p * diff
        dlf = jnp.sum(jnp.sum(jnp.where(causal, gd, 0.0), axis=1, keepdims=True), axis=0, keepdims=True)
        dlb = jnp.sum(jnp.sum(jnp.where(causal, 0.0, -gd), axis=1, keepdims=True), axis=0, keepdims=True)
        row = lax.broadcasted_iota(jnp.int32, (2, LANES), 0)
        dlg_ref[0] += jnp.where(row == 0, dlf, dlb)

    return pl.pallas_call(
        body, name=name, grid=(HEADS, seq // tq),
        in_specs=[pl.BlockSpec((tq, QK_DIM), lambda h, i: (i, h)),
                  pl.BlockSpec((seq, QK_DIM), lambda h, i: (0, h)),
                  pl.BlockSpec((seq, V_DIM), lambda h, i: (0, v_blk0 + h)),
                  pl.BlockSpec((tq, V_DIM), lambda h, i: (i, h)),
                  pl.BlockSpec((1, 2, LANES), lambda h, i: (h, 0, 0))],
        out_specs=[pl.BlockSpec((tq, QK_DIM), lambda h, i: (i, h)),
                   pl.BlockSpec((seq, QK_DIM), lambda h, i: (0, h)),
                   pl.BlockSpec((seq, V_DIM), lambda h, i: (0, h)),
                   pl.BlockSpec((1, 2, LANES), lambda h, i: (h, 0, 0))],
        out_shape=[jax.ShapeDtypeStruct((seq, QK_WIDTH), F32), jax.ShapeDtypeStruct((seq, QK_WIDTH), F32),
                   jax.ShapeDtypeStruct((seq, HEADS * V_DIM), F32), jax.ShapeDtypeStruct((HEADS, 2, LANES), F32)],
        compiler_params=_params(("parallel", "arbitrary")),
    )(qr, kr, proj, dy, lg)


def _shift_rows(v, reverse):
    row = lax.broadcasted_iota(jnp.int32, v.shape, 0)
    if reverse:
        return jnp.where(row == SEGMENTS - 1, 0.0, pltpu.roll(v, SEGMENTS - 1, 0))
    return jnp.where(row == 0, 0.0, pltpu.roll(v, 1, 0))


def _slab(t):
    if isinstance(t, int):
        return pl.ds(t * SEGMENTS, SEGMENTS)
    return pl.ds(pl.multiple_of(t * SEGMENTS, SEGMENTS), SEGMENTS)


def _unrolled_loop(body, lo, hi, init):
    main = (hi - lo) // SCAN_UNROLL

    def unrolled(g, carry):
        for k in range(SCAN_UNROLL):
            carry = body(lo + g * SCAN_UNROLL + k, carry)
        return carry

    carry = lax.fori_loop(0, main, unrolled, init)
    for t in range(lo + main * SCAN_UNROLL, hi):
        carry = body(t, carry)
    return carry


def _scan(xr_ref, xi_ref, lam, reverse, conj):
    steps = xr_ref.shape[0] // SEGMENTS
    cols = xr_ref.shape[1]
    lr = jnp.broadcast_to(lam[0], (SEGMENTS, cols))
    li = jnp.broadcast_to(lam[1], (SEGMENTS, cols))
    lrt = jnp.broadcast_to(lam[2], (SEGMENTS, cols))
    lit = jnp.broadcast_to(lam[3], (SEGMENTS, cols))
    if conj:
        li, lit = -li, -lit
    zero = jnp.zeros((SEGMENTS, cols), F32)

    def rows_of(t):
        return _slab(steps - 1 - t if reverse else t)

    def advance(t, carry):
        sr, si = carry
        rows = rows_of(t)
        return lr * sr - li * si + xr_ref[rows, :], lr * si + li * sr + xi_ref[rows, :]

    def step(t, carry):
        nr, ni = advance(t, carry)
        rows = rows_of(t)
        xr_ref[rows, :] = nr
        xi_ref[rows, :] = ni
        return nr, ni

    def run(body, init):
        return _unrolled_loop(body, 0, steps, init)

    er, ei = run(advance, (zero, zero))
    cr, ci = zero, zero
    for _ in range(SEGMENTS - 1):
        tr = er + lrt * cr - lit * ci
        ti = ei + lrt * ci + lit * cr
        cr, ci = _shift_rows(tr, reverse), _shift_rows(ti, reverse)
    run(step, (cr, ci))


def _permute_in(dst_ref, src_ref):
    steps = src_ref.shape[0] // SEGMENTS
    for s in range(SEGMENTS):
        dst_ref[pl.ds(s, steps, stride=SEGMENTS), :] = src_ref[s * steps:(s + 1) * steps, :].astype(dst_ref.dtype)


def _unpermute(src_ref, s):
    steps = src_ref.shape[0] // SEGMENTS
    return src_ref[pl.ds(s, steps, stride=SEGMENTS), :]


def _s5_fwd(proj, bblk, cblk, lam, name):
    seq = proj.shape[0]
    u_blk0 = (2 * QK_WIDTH + 2 * D_MODEL) // LANES
    sc = STATE_COLS

    def body(u_ref, b_ref, c_ref, lam_ref, y_ref, up_ref, yp_ref, xr_ref, xi_ref):
        _permute_in(up_ref, u_ref)
        ub = up_ref[...].astype(BF16)
        for d in range(2):
            xr_ref[...] = jnp.dot(ub, b_ref[d, :, 0:sc], preferred_element_type=F32)
            xi_ref[...] = jnp.dot(ub, b_ref[d, :, sc:2 * sc], preferred_element_type=F32)
            lm = [lam_ref[d, r:r + 1, :] for r in range(4)]
            _scan(xr_ref, xi_ref, lm, reverse=(d == 1), conj=False)
            yd = (jnp.dot(xr_ref[...].astype(BF16), c_ref[d, 0:sc, :], preferred_element_type=F32)
                  + jnp.dot(xi_ref[...].astype(BF16), c_ref[d, sc:2 * sc, :], preferred_element_type=F32))
            if d == 0:
                yp_ref[...] = yd
            else:
                yp_ref[...] += yd
        steps = seq // SEGMENTS
        for s in range(SEGMENTS):
            y_ref[s * steps:(s + 1) * steps, :] = _unpermute(yp_ref, s)

    return pl.pallas_call(
        body, name=name, grid=(N_TILES,),
        in_specs=[pl.BlockSpec((seq, LANES), lambda j: (0, u_blk0 + j)),
                  pl.BlockSpec((2, None, LANES, 2 * sc), lambda j: (0, j, 0, 0)),
                  pl.BlockSpec((2, None, 2 * sc, LANES), lambda j: (0, j, 0, 0)),
                  pl.BlockSpec((2, None, 4, sc), lambda j: (0, j, 0, 0))],
        out_specs=pl.BlockSpec((seq, LANES), lambda j: (0, j)),
        out_shape=jax.ShapeDtypeStruct((seq, D_MODEL), F32),
        scratch_shapes=[pltpu.VMEM((seq, LANES), F32), pltpu.VMEM((seq, LANES), F32),
                        pltpu.VMEM((seq, sc), F32), pltpu.VMEM((seq, sc), F32)],
        compiler_params=_params(("parallel",)),
    )(proj, bblk, cblk, lam)


def _s5_bwd(proj, dy, du_part, bblk, cblk, lam, name):
    seq = proj.shape[0]
    u_blk0 = (2 * QK_WIDTH + 2 * D_MODEL) // LANES
    sc = STATE_COLS
    steps = seq // SEGMENTS

    def body(u_ref, dy_ref, dup_ref, b_ref, c_ref, lam_ref, du_ref, db_ref, dc_ref, dlam_ref,
             up_ref, dyp_ref, dua_ref, xr_ref, xi_ref, gr_ref, gi_ref):
        _permute_in(up_ref, u_ref)
        _permute_in(dyp_ref, dy_ref)
        ub = up_ref[...].astype(BF16)
        dyb = dyp_ref[...].astype(BF16)
        ubt = up_ref[...].T.astype(BF16)
        dybt = dyp_ref[...].T.astype(BF16)
        for d in range(2):
            reverse = d == 1
            xr_ref[...] = jnp.dot(ub, b_ref[d, :, 0:sc], preferred_element_type=F32)
            xi_ref[...] = jnp.dot(ub, b_ref[d, :, sc:2 * sc], preferred_element_type=F32)
            lm = [lam_ref[d, r:r + 1, :] for r in range(4)]
            _scan(xr_ref, xi_ref, lm, reverse=reverse, conj=False)
            xrb = xr_ref[...].astype(BF16)
            xib = xi_ref[...].astype(BF16)
            dc_ref[d, :, 0:sc] = jnp.dot(dybt, xrb, preferred_element_type=F32)
            dc_ref[d, :, sc:2 * sc] = jnp.dot(dybt, xib, preferred_element_type=F32)
            gr_ref[...] = lax.dot_general(dyb, c_ref[d, 0:sc, :], _NT, preferred_element_type=F32)
            gi_ref[...] = lax.dot_general(dyb, c_ref[d, sc:2 * sc, :], _NT, preferred_element_type=F32)
            _scan(gr_ref, gi_ref, lm, reverse=not reverse, conj=True)

            def acc_step(t, carry):
                ar, ai = carry
                prev = _slab(t + 1 if reverse else t - 1)
                pr = xr_ref[prev, :]
                pi = xi_ref[prev, :]
                zr = gr_ref[_slab(t), :]
                zi = gi_ref[_slab(t), :]
                return ar + zr * pr + zi * pi, ai + zi * pr - zr * pi

            zero = jnp.zeros((SEGMENTS, sc), F32)
            if reverse:
                ar, ai = _unrolled_loop(acc_step, 0, steps - 1, (zero, zero))
                edge = _slab(steps - 1)
                pr = _shift_rows(xr_ref[_slab(0), :], True)
                pi = _shift_rows(xi_ref[_slab(0), :], True)
            else:
                ar, ai = _unrolled_loop(acc_step, 1, steps, (zero, zero))
                edge = _slab(0)
                pr = _shift_rows(xr_ref[_slab(steps - 1), :], False)
                pi = _shift_rows(xi_ref[_slab(steps - 1), :], False)
            zr = gr_ref[edge, :]
            zi = gi_ref[edge, :]
            ar = ar + zr * pr + zi * pi
            ai = ai + zi * pr - zr * pi
            dlam_ref[d, 0:1, :] = jnp.sum(ar, axis=0, keepdims=True)
            dlam_ref[d, 1:2, :] = jnp.sum(ai, axis=0, keepdims=True)

            grb = gr_ref[...].astype(BF16)
            gib = gi_ref[...].astype(BF16)
            db_ref[d, :, 0:sc] = jnp.dot(ubt, grb, preferred_element_type=F32)
            db_ref[d, :, sc:2 * sc] = jnp.dot(ubt, gib, preferred_element_type=F32)
            dud = (lax.dot_general(grb, b_ref[d, :, 0:sc], _NT, preferred_element_type=F32)
                   + lax.dot_general(gib, b_ref[d, :, sc:2 * sc], _NT, preferred_element_type=F32))
            if d == 0:
                dua_ref[...] = dud
            else:
                dua_ref[...] += dud
        for s in range(SEGMENTS):
            rows = slice(s * steps, (s + 1) * steps)
            du_ref[rows, :] = (_unpermute(dua_ref, s) + dup_ref[rows, :]).astype(du_ref.dtype)

    return pl.pallas_call(
        body, name=name, grid=(N_TILES,),
        in_specs=[pl.BlockSpec((seq, LANES), lambda j: (0, u_blk0 + j)),
                  pl.BlockSpec((seq, LANES), lambda j: (0, j)),
                  pl.BlockSpec((seq, LANES), lambda j: (0, j)),
                  pl.BlockSpec((2, None, LANES, 2 * sc), lambda j: (0, j, 0, 0)),
                  pl.BlockSpec((2, None, 2 * sc, LANES), lambda j: (0, j, 0, 0)),
                  pl.BlockSpec((2, None, 4, sc), lambda j: (0, j, 0, 0))],
        out_specs=[pl.BlockSpec((seq, LANES), lambda j: (0, j)),
                   pl.BlockSpec((2, None, LANES, 2 * sc), lambda j: (0, j, 0, 0)),
                   pl.BlockSpec((2, None, LANES, 2 * sc), lambda j: (0, j, 0, 0)),
                   pl.BlockSpec((2, None, 2, sc), lambda j: (0, j, 0, 0))],
        out_shape=[jax.ShapeDtypeStruct((seq, D_MODEL), BF16),
                   jax.ShapeDtypeStruct((2, N_TILES, LANES, 2 * sc), F32),
                   jax.ShapeDtypeStruct((2, N_TILES, LANES, 2 * sc), F32),
                   jax.ShapeDtypeStruct((2, N_TILES, 2, sc), F32)],
        scratch_shapes=[pltpu.VMEM((seq, LANES), F32), pltpu.VMEM((seq, LANES), F32), pltpu.VMEM((seq, LANES), F32),
                        pltpu.VMEM((seq, sc), F32), pltpu.VMEM((seq, sc), F32),
                        pltpu.VMEM((seq, sc), F32), pltpu.VMEM((seq, sc), F32)],
        compiler_params=_params(("parallel",)),
    )(proj, dy, du_part, bblk, cblk, lam)


def _s5_discretize(a_re, a_im, log_dt, b_re, b_im, seg_len):
    dt = jnp.exp(log_dt)[..., None]
    e = jnp.exp(a_re * dt)
    lr, li = e * jnp.cos(a_im * dt), e * jnp.sin(a_im * dt)
    et = jnp.exp(a_re * dt * seg_len)
    lrt, lit = et * jnp.cos(a_im * dt * seg_len), et * jnp.sin(a_im * dt * seg_len)
    den = a_re * a_re + a_im * a_im
    qr = ((lr - 1.0) * a_re + li * a_im) / den
    qi = (li * a_re - (lr - 1.0) * a_im) / den
    br = qr[..., None] * b_re - qi[..., None] * b_im
    bi = qr[..., None] * b_im + qi[..., None] * b_re
    return lr, li, lrt, lit, br, bi


def _s5_pack(lr, li, lrt, lit, br, bi, c_re, c_im):
    eye = jnp.eye(GROUPS_PER_TILE, dtype=F32)

    def bd_b(b):
        b5 = b.reshape(2, N_TILES, GROUPS_PER_TILE, N_STATE, GROUP)
        return jnp.einsum("dtgph,gk->dtghkp", b5, eye).reshape(2, N_TILES, LANES, STATE_COLS)

    def bd_c(c):
        c5 = c.reshape(2, N_TILES, GROUPS_PER_TILE, GROUP, N_STATE)
        return jnp.einsum("dtghp,gk->dtkpgh", c5, eye).reshape(2, N_TILES, STATE_COLS, LANES)

    bblk = jnp.concatenate([bd_b(br), bd_b(bi)], axis=3)
    cblk = jnp.concatenate([bd_c(c_re), -bd_c(c_im)], axis=2)
    lam = jnp.stack([v.reshape(2, N_TILES, STATE_COLS) for v in (lr, li, lrt, lit)], axis=2)
    return bblk, cblk, lam


def _s5_unpack(dbblk, dcblk, dlam):
    eye = jnp.eye(GROUPS_PER_TILE, dtype=F32)

    def diag_b(d):
        d6 = d.reshape(2, N_TILES, GROUPS_PER_TILE, GROUP, GROUPS_PER_TILE, N_STATE)
        return jnp.einsum("dtghkp,gk->dtgph", d6, eye).reshape(2, N_GROUPS, N_STATE, GROUP)

    def diag_c(d):
        d6 = d.reshape(2, N_TILES, GROUPS_PER_TILE, GROUP, GROUPS_PER_TILE, N_STATE)
        return jnp.einsum("dtghkp,gk->dtghp", d6, eye).reshape(2, N_GROUPS, GROUP, N_STATE)

    dbr, dbi = diag_b(dbblk[..., :STATE_COLS]), diag_b(dbblk[..., STATE_COLS:])
    dcr, dci = diag_c(dcblk[..., :STATE_COLS]), -diag_c(dcblk[..., STATE_COLS:])
    dlr = dlam[:, :, 0, :].reshape(2, N_GROUPS, N_STATE)
    dli = dlam[:, :, 1, :].reshape(2, N_GROUPS, N_STATE)
    return dlr, dli, dbr, dbi, dcr, dci


def _pos():
    return lax.axis_index("x"), lax.axis_index("y"), lax.axis_index("c")


def _remote(src, dst, ssem, rsem, dev):
    return pltpu.make_async_remote_copy(src_ref=src, dst_ref=dst, send_sem=ssem, recv_sem=rsem,
                                        device_id=dev, device_id_type=MESH)


_PIECES = (
    ("w_in", "in", D_MODEL, IN_WIDTH // N_CHIPS, 0, IN_WIDTH // N_CHIPS, 0),
    ("w_glu", "glu", D_MODEL // N_CHIPS, D_MODEL, D_MODEL // N_CHIPS, 0, 0),
    ("w_out", "out", D_MODEL // N_CHIPS, D_MODEL, D_MODEL // N_CHIPS, 0, 0),
    ("w_ffn_gate", "gu", D_MODEL, D_FF // N_CHIPS, 0, D_FF // N_CHIPS, 0),
    ("w_ffn_up", "gu", D_MODEL, D_FF // N_CHIPS, 0, D_FF // N_CHIPS, D_FF),
    ("w_ffn_down", "down", D_FF // N_CHIPS, D_MODEL, D_FF // N_CHIPS, 0, 0),
)
_BUFFERS = (("in", D_MODEL, IN_WIDTH), ("glu", D_MODEL, D_MODEL), ("out", D_MODEL, D_MODEL),
            ("gu", D_MODEL, 2 * D_FF), ("down", D_FF, D_MODEL))
_BUF_INDEX = {name: t for t, (name, _, _) in enumerate(_BUFFERS)}
N_PIECES = len(_PIECES)
N_BUFFERS = len(_BUFFERS)


def _own_block(piece, tm):
    _, _, _, cs, rstep, cstep, coff = piece
    return lambda i, chip: (i + chip * (rstep // tm), coff // cs + chip * (cstep // cs))


def _cast_place(piece, w3, layer, prev, chip_arr, name):
    _, r, cc = w3.shape
    _, rf, cf = _BUFFERS[_BUF_INDEX[piece[1]]]
    tm = _tile(r, 256)
    own = _own_block(piece, tm)

    def body(s_ref, w_ref, *rest):
        rest[-1][...] = w_ref[...].astype(BF16)

    in_specs = [pl.BlockSpec((None, tm, cc), lambda i, s: (layer, i, 0))]
    args = [w3]
    aliases = {}
    if prev is not None:
        in_specs.append(pl.BlockSpec(memory_space=pl.ANY))
        args.append(prev)
        aliases = {2: 0}
    return pl.pallas_call(
        body, name=name,
        grid_spec=pltpu.PrefetchScalarGridSpec(
            num_scalar_prefetch=1, grid=(r // tm,), in_specs=in_specs,
            out_specs=pl.BlockSpec((tm, cc), lambda i, s: own(i, s[0]))),
        out_shape=jax.ShapeDtypeStruct((rf, cf), BF16), input_output_aliases=aliases,
        compiler_params=_params(("parallel",)),
    )(chip_arr, *args)


_GATHER_GROUPS = ((0, (0,)), (0, (1, 2, 3, 4, 5)), (1, (0,)), (1, (1, 2, 3, 4, 5)))
_SPLIT_EFFECT = pltpu.SideEffectType.DATAFLOW_SIDE_EFFECTING
SEM_SPEC = pl.BlockSpec(memory_space=pltpu.SEMAPHORE)
BF16_ROWS = 2 * SUBLANES


def _group_keys(g):
    layer, pieces = _GATHER_GROUPS[g]
    keys = []
    for p in pieces:
        if (_PIECES[p][1], layer) not in keys:
            keys.append((_PIECES[p][1], layer))
    return keys


def _half_view(ref, piece, j, c):
    _, _, rs, cs, rstep, cstep, coff = piece
    half = rs // 2
    return ref.at[pl.ds(pl.multiple_of(j * rstep + c * half, BF16_ROWS), half), pl.ds(coff + j * cstep, cs)]


def _for_my_chip(fn):
    x, y, _ = _pos()
    for mine in range(N_CHIPS):
        pl.when(2 * x + y == mine)(functools.partial(fn, mine, [j for j in range(N_CHIPS) if j != mine]))


def _gather_start(placed):
    keys = [k for g in range(len(_GATHER_GROUPS)) for k in _group_keys(g)]
    nb, ng = len(keys), len(_GATHER_GROUPS)

    def body(*refs):
        bufs = dict(zip(keys, refs[nb:2 * nb]))
        ssems = refs[2 * nb:2 * nb + ng]
        rsems = refs[2 * nb + ng:2 * nb + 2 * ng]
        token = refs[2 * nb + 2 * ng]
        _, _, c = _pos()

        def send(mine, others):
            for g, (layer, pieces) in enumerate(_GATHER_GROUPS):
                for k, p in enumerate(pieces):
                    view = _half_view(bufs[(_PIECES[p][1], layer)], _PIECES[p], mine, c)
                    for j in others:
                        _remote(view, view, ssems[g].at[k * N_CHIPS + j], rsems[g].at[k * N_CHIPS + mine],
                                (j // 2, j % 2, c)).start()

        _for_my_chip(send)
        token[...] = jnp.zeros(token.shape, token.dtype)

    sems = [pltpu.SemaphoreType.DMA((N_CHIPS * len(pieces),)) for _, pieces in _GATHER_GROUPS]
    shapes = [jax.ShapeDtypeStruct(a.shape, a.dtype) for a in placed]
    res = pl.pallas_call(
        body, name="gather_start",
        in_specs=[HBM_SPEC] * nb,
        out_specs=[HBM_SPEC] * nb + [SEM_SPEC] * (2 * ng) + [pl.BlockSpec(memory_space=pltpu.VMEM)],
        out_shape=shapes + sems + sems + [jax.ShapeDtypeStruct((SUBLANES, LANES), F32)],
        input_output_aliases={t: t for t in range(nb)},
        compiler_params=_params(has_side_effects=_SPLIT_EFFECT),
    )(*[pltpu.with_memory_space_constraint(a, pltpu.HBM) for a in placed])
    return dict(zip(keys, res[:nb])), res[nb:nb + ng], res[nb + ng:nb + 2 * ng], res[nb + 2 * ng]


def _gather_wait(g, bufs, ssem, rsem, after):
    layer, pieces = _GATHER_GROUPS[g]
    keys = _group_keys(g)
    nb = len(keys)

    def body(*refs):
        ssem_ref, rsem_ref = refs[nb], refs[nb + 1]
        land = dict(zip(keys, refs[nb + 3:]))
        _, _, c = _pos()

        def wait(mine, others):
            for k, p in enumerate(pieces):
                ref = land[(_PIECES[p][1], layer)]
                for j in others:
                    cp = _remote(_half_view(ref, _PIECES[p], mine, c), _half_view(ref, _PIECES[p], j, c),
                                 ssem_ref.at[k * N_CHIPS + j], rsem_ref.at[k * N_CHIPS + j], (j // 2, j % 2, c))
                    cp.wait_send()
                    cp.wait_recv()

        _for_my_chip(wait)

    return pl.pallas_call(
        body, name="gather_wait_g%d" % g,
        in_specs=[HBM_SPEC] * nb + [SEM_SPEC, SEM_SPEC, pl.BlockSpec(memory_space=pl.ANY)],
        out_specs=[HBM_SPEC] * nb,
        out_shape=[jax.ShapeDtypeStruct(a.shape, a.dtype) for a in bufs],
        input_output_aliases={t: t for t in range(nb)},
        compiler_params=_params(has_side_effects=_SPLIT_EFFECT),
    )(*bufs, ssem, rsem, after)


def _gather_forward(g, bufs):
    layer, pieces = _GATHER_GROUPS[g]
    keys = _group_keys(g)
    nb = len(keys)

    def body(*refs):
        land = dict(zip(keys, refs[nb:2 * nb]))
        ssem, rsem = refs[2 * nb:]
        x, y, c = _pos()

        def forward(mine, others):
            cps = []
            for k, p in enumerate(pieces):
                ref = land[(_PIECES[p][1], layer)]
                for j in others:
                    view = _half_view(ref, _PIECES[p], j, c)
                    cp = _remote(view, view, ssem.at[k * N_CHIPS + j], rsem.at[k * N_CHIPS + j], (x, y, 1 - c))
                    cp.start()
                    cps.append(cp)
            for k, p in enumerate(pieces):
                ref = land[(_PIECES[p][1], layer)]
                for j in others:
                    view = _half_view(ref, _PIECES[p], j, 1 - c)
                    _remote(view, view, ssem.at[k * N_CHIPS + j], rsem.at[k * N_CHIPS + j], (x, y, 1 - c)).wait_recv()
            for cp in cps:
                cp.wait_send()

        _for_my_chip(forward)

    nsem = N_CHIPS * len(pieces)
    return pl.pallas_call(
        body, name="gather_forward_g%d" % g,
        in_specs=[HBM_SPEC] * nb, out_specs=[HBM_SPEC] * nb,
        out_shape=[jax.ShapeDtypeStruct(a.shape, a.dtype) for a in bufs],
        input_output_aliases={t: t for t in range(nb)},
        scratch_shapes=[pltpu.SemaphoreType.DMA((nsem,)), pltpu.SemaphoreType.DMA((nsem,))],
        compiler_params=_params(has_side_effects=True),
    )(*bufs)


_REDUCE_GROUPS = (
    ((5, 1), (3, 1), (4, 1), (2, 1), (1, 1), (0, 1)),
    ((5, 0), (3, 0), (4, 0)),
    ((2, 0), (1, 0)),
    ((0, 0),),
)


def _reduce_keys(group):
    keys = []
    for p, layer in group:
        if (_PIECES[p][1], layer) not in keys:
            keys.append((_PIECES[p][1], layer))
    return keys


def _half_block(piece, tm):
    _, _, rs, cs, rstep, cstep, coff = piece
    return lambda i, j, c: (j * (rstep // tm) + c * (rs // 2 // tm) + i, coff // cs + j * (cstep // cs))


def _swap_halves(g, dwb):
    group = _REDUCE_GROUPS[g]
    keys = _reduce_keys(group)
    nk = len(keys)

    def body(*refs):
        src = dict(zip(keys, refs[:nk]))
        dst = dict(zip(keys, refs[nk:2 * nk]))
        ssem, rsem = refs[2 * nk:]
        x, y, c = _pos()
        cps = []
        for k, (p, layer) in enumerate(group):
            key = (_PIECES[p][1], layer)
            for j in range(N_CHIPS):
                cp = _remote(_half_view(src[key], _PIECES[p], j, 1 - c), _half_view(dst[key], _PIECES[p], j, 1 - c),
                             ssem.at[k * N_CHIPS + j], rsem.at[k * N_CHIPS + j], (x, y, 1 - c))
                cp.start()
                cps.append(cp)
        for k, (p, layer) in enumerate(group):
            key = (_PIECES[p][1], layer)
            for j in range(N_CHIPS):
                view = _half_view(dst[key], _PIECES[p], j, c)
                _remote(view, view, ssem.at[k * N_CHIPS + j], rsem.at[k * N_CHIPS + j], (x, y, 1 - c)).wait_recv()
        for cp in cps:
            cp.wait_send()

    nsem = N_CHIPS * len(group)
    res = pl.pallas_call(
        body, name="swap_halves_g%d" % g,
        in_specs=[HBM_SPEC] * nk, out_specs=[HBM_SPEC] * nk,
        out_shape=[jax.ShapeDtypeStruct(dwb[k].shape, BF16) for k in keys],
        scratch_shapes=[pltpu.SemaphoreType.DMA((nsem,)), pltpu.SemaphoreType.DMA((nsem,))],
        compiler_params=_params(has_side_effects=True),
    )(*[dwb[k] for k in keys])
    return dict(zip(keys, res))


def _chip_partial(piece, dw, got, prev, c_arr, name):
    _, _, rs, cs, _, _, _ = piece
    half = rs // 2
    tm = _tile(half, 256)
    blk = _half_block(piece, tm)

    def body(s_ref, dw_ref, got_ref, *rest):
        rest[-1][...] = (dw_ref[...] + got_ref[...].astype(F32)).astype(BF16)

    spec = pl.BlockSpec((tm, cs), lambda j, i, s: blk(i, j, s[0]))
    in_specs = [spec, spec]
    args = [dw, got]
    aliases = {}
    if prev is not None:
        in_specs.append(pl.BlockSpec(memory_space=pl.ANY))
        args.append(prev)
        aliases = {3: 0}
    return pl.pallas_call(
        body, name=name,
        grid_spec=pltpu.PrefetchScalarGridSpec(
            num_scalar_prefetch=1, grid=(N_CHIPS, half // tm), in_specs=in_specs, out_specs=spec),
        out_shape=jax.ShapeDtypeStruct(dw.shape, BF16), input_output_aliases=aliases,
        compiler_params=_params(("parallel", "parallel")),
    )(c_arr, *args)


def _scatter_start(g, partials):
    group = _REDUCE_GROUPS[g]
    keys = _reduce_keys(group)
    nk, n = len(keys), len(group)

    def body(*refs):
        pt = dict(zip(keys, refs[nk:2 * nk]))
        land = refs[2 * nk:2 * nk + n]
        ssem, rsem, token = refs[2 * nk + n:]
        _, _, c = _pos()

        def send(mine, others):
            for k, (p, layer) in enumerate(group):
                for j in others:
                    _remote(_half_view(pt[(_PIECES[p][1], layer)], _PIECES[p], j, c), land[k].at[mine],
                            ssem.at[k * N_CHIPS + j], rsem.at[k * N_CHIPS + mine], (j // 2, j % 2, c)).start()

        _for_my_chip(send)
        token[...] = jnp.zeros(token.shape, token.dtype)

    sem = pltpu.SemaphoreType.DMA((N_CHIPS * n,))
    res = pl.pallas_call(
        body, name="scatter_start_g%d" % g,
        in_specs=[HBM_SPEC] * nk,
        out_specs=[HBM_SPEC] * (nk + n) + [SEM_SPEC, SEM_SPEC, pl.BlockSpec(memory_space=pltpu.VMEM)],
        out_shape=([jax.ShapeDtypeStruct(partials[k].shape, BF16) for k in keys]
                   + [jax.ShapeDtypeStruct((N_CHIPS, _PIECES[p][2] // 2, _PIECES[p][3]), BF16) for p, _ in group]
                   + [sem, sem, jax.ShapeDtypeStruct((SUBLANES, LANES), F32)]),
        input_output_aliases={t: t for t in range(nk)},
        compiler_params=_params(has_side_effects=_SPLIT_EFFECT),
    )(*[pltpu.with_memory_space_constraint(partials[k], pltpu.HBM) for k in keys])
    return list(res[:nk]), list(res[nk:nk + n]), res[nk + n], res[nk + n + 1], res[nk + n + 2]


def _scatter_wait(g, partials, land, ssem, rsem, after):
    group = _REDUCE_GROUPS[g]
    keys = _reduce_keys(group)
    nk, n = len(keys), len(group)

    def body(*refs):
        ssem_ref, rsem_ref = refs[nk + n], refs[nk + n + 1]
        pt = dict(zip(keys, refs[nk + n + 3:2 * nk + n + 3]))
        land_ref = refs[2 * nk + n + 3:]
        _, _, c = _pos()

        def wait(mine, others):
            for k, (p, layer) in enumerate(group):
                for j in others:
                    cp = _remote(_half_view(pt[(_PIECES[p][1], layer)], _PIECES[p], j, c), land_ref[k].at[j],
                                 ssem_ref.at[k * N_CHIPS + j], rsem_ref.at[k * N_CHIPS + j], (j // 2, j % 2, c))
                    cp.wait_send()
                    cp.wait_recv()

        _for_my_chip(wait)

    res = pl.pallas_call(
        body, name="scatter_wait_g%d" % g,
        in_specs=[HBM_SPEC] * (nk + n) + [SEM_SPEC, SEM_SPEC, pl.BlockSpec(memory_space=pl.ANY)],
        out_specs=[HBM_SPEC] * (nk + n),
        out_shape=[jax.ShapeDtypeStruct(a.shape, a.dtype) for a in list(partials) + list(land)],
        input_output_aliases={t: t for t in range(nk + n)},
        compiler_params=_params(has_side_effects=_SPLIT_EFFECT),
    )(*partials, *land, ssem, rsem, after)
    return list(res[nk:])


def _reduce_half(piece, layer, dw, got, land, prev, idx, name):
    _, _, rs, cs, _, _, _ = piece
    half = rs // 2
    tm = _tile(half, 256)
    blk = _half_block(piece, tm)

    def body(s_ref, dw_ref, got_ref, r1, r2, r3, *rest):
        acc = dw_ref[...] + got_ref[...].astype(F32)
        for r in (r1, r2, r3):
            acc = acc + r[...].astype(F32)
        rest[-1][...] = acc

    def land_map(k):
        return lambda i, s: ((s[1] + k) % N_CHIPS, i, 0)

    own = pl.BlockSpec((tm, cs), lambda i, s: blk(i, s[1], s[0]))
    in_specs = [own, own] + [pl.BlockSpec((None, tm, cs), land_map(k)) for k in (1, 2, 3)]
    args = [dw, got, land, land, land]
    aliases = {}
    if prev is not None:
        in_specs.append(pl.BlockSpec(memory_space=pl.ANY))
        args.append(prev)
        aliases = {6: 0}
    return pl.pallas_call(
        body, name=name,
        grid_spec=pltpu.PrefetchScalarGridSpec(
            num_scalar_prefetch=1, grid=(half // tm,), in_specs=in_specs,
            out_specs=pl.BlockSpec((None, tm, cs), lambda i, s: (layer, s[0] * (half // tm) + i, 0))),
        out_shape=jax.ShapeDtypeStruct((DEPTH, rs, cs), F32), input_output_aliases=aliases,
        compiler_params=_params(("parallel",)),
    )(idx, *args)


def _share_halves(reduced):
    def body(*refs):
        buf = refs[N_PIECES:2 * N_PIECES]
        ssem, rsem = refs[2 * N_PIECES:]
        x, y, c = _pos()

        def half(p, layer, cc):
            rows = _PIECES[p][2] // 2
            return buf[p].at[layer, pl.ds(pl.multiple_of(cc * rows, SUBLANES), rows), :]

        pairs = [(p, layer) for p in range(N_PIECES) for layer in range(DEPTH)]
        rem = [_remote(half(p, layer, c), half(p, layer, c), ssem.at[k], rsem.at[k], (x, y, 1 - c))
               for k, (p, layer) in enumerate(pairs)]
        for cp in rem:
            cp.start()
        for k, (p, layer) in enumerate(pairs):
            rem[k].wait_send()
            _remote(half(p, layer, 1 - c), half(p, layer, 1 - c), ssem.at[k], rsem.at[k], (x, y, 1 - c)).wait_recv()

    nsem = N_PIECES * DEPTH
    return pl.pallas_call(
        body, name="share_halves",
        in_specs=[HBM_SPEC] * N_PIECES, out_specs=[HBM_SPEC] * N_PIECES,
        out_shape=[jax.ShapeDtypeStruct((DEPTH, p[2], p[3]), F32) for p in _PIECES],
        input_output_aliases={t: t for t in range(N_PIECES)},
        scratch_shapes=[pltpu.SemaphoreType.DMA((nsem,)), pltpu.SemaphoreType.DMA((nsem,))],
        compiler_params=_params(has_side_effects=True),
    )(*reduced)


N_DEV = 8


def _all_gather_rows(v):
    rows = v.shape[0]

    def body(v_ref, out_ref, ssem, rsem, lsem):
        x, y, c = _pos()
        me, sibling = (x, y, c), (x, y, 1 - c)
        chips = [(1 - x, y), (x, 1 - y), (1 - x, 1 - y)]

        def slot(px, py, pc):
            return out_ref.at[4 * px + 2 * py + pc]

        def copy(k, block, to, src=None):
            return _remote(slot(*block) if src is None else src, slot(*block), ssem.at[k], rsem.at[k], to)

        mine = pltpu.make_async_copy(v_ref, slot(*me), lsem)
        mine.start()
        first = [copy(0, me, sibling, src=v_ref)]
        first += [copy(1 + j, me, (*chip, c), src=v_ref) for j, chip in enumerate(chips)]
        for cp in first:
            cp.start()
        passed = [copy(4 + j, (*chip, c), sibling) for j, chip in enumerate(chips)]
        for j, chip in enumerate(chips):
            copy(1 + j, (*chip, c), me).wait_recv()
            passed[j].start()
        copy(0, sibling, me).wait_recv()
        for j, chip in enumerate(chips):
            copy(4 + j, (*chip, 1 - c), me).wait_recv()
        for cp in first + passed:
            cp.wait_send()
        mine.wait()

    return pl.pallas_call(
        body, name="all_gather_small",
        in_specs=[HBM_SPEC], out_specs=HBM_SPEC,
        out_shape=jax.ShapeDtypeStruct((N_DEV, rows, LANES), F32),
        scratch_shapes=[pltpu.SemaphoreType.DMA((7,)), pltpu.SemaphoreType.DMA((7,)), pltpu.SemaphoreType.DMA],
        compiler_params=_params(has_side_effects=True),
    )(v)


def _sum_slots(g, name):
    n, rows, _ = g.shape
    tm = _tile(rows, 512)

    def body(g_ref, out_ref):
        acc = g_ref[0]
        for k in range(1, n):
            acc = acc + g_ref[k]
        out_ref[...] = acc

    return pl.pallas_call(
        body, name=name, grid=(rows // tm,),
        in_specs=[pl.BlockSpec((n, tm, LANES), lambda i: (0, i, 0))],
        out_specs=pl.BlockSpec((tm, LANES), lambda i: (i, 0)),
        out_shape=jax.ShapeDtypeStruct((rows, LANES), F32),
        compiler_params=_params(("parallel",)),
    )(g)


_SMALL = ("ln_mix_g", "ret_log_gamma", "ssm_a_re", "ssm_a_im", "ssm_log_dt", "ssm_b_re", "ssm_b_im",
          "ssm_c_re", "ssm_c_im", "ssm_d", "b_glu", "ln_ffn_g", "ln_final_g")
_FLAT_ALIGN = LANES * LANES


def _flatten_small(d):
    parts = []
    for n in _SMALL:
        f = d[n].reshape(-1)
        parts.append(jnp.pad(f, (0, (-f.shape[0]) % _FLAT_ALIGN)))
    return jnp.concatenate(parts).reshape(-1, LANES)


def _unflatten_small(flat, like):
    out, row = {}, 0
    for n in _SMALL:
        size = math.prod(like[n].shape)
        rows = (size + (-size) % _FLAT_ALIGN) // LANES
        part = lax.optimization_barrier(flat[row:row + rows])
        out[n] = part.reshape(-1)[:size].reshape(like[n].shape)
        row += rows
    return out


_BIG = ("w_in", "w_glu", "w_out", "w_ffn_gate", "w_ffn_up", "w_ffn_down")
_WEIGHTS = ("ln_mix_g", "w_in", "ret_log_gamma", "ssm_a_re", "ssm_a_im", "ssm_log_dt", "ssm_b_re", "ssm_b_im",
            "ssm_c_re", "ssm_c_im", "ssm_d", "w_glu", "b_glu", "w_out", "ln_ffn_g", "w_ffn_gate", "w_ffn_up",
            "w_ffn_down", "ln_final_g")


def _rope_tables(seq):
    half = QK_DIM // 2
    inv = 1.0 / (ROPE_BASE ** (jnp.arange(half, dtype=F32) / half))
    ang = jnp.arange(seq, dtype=F32)[:, None] * inv[None, :]
    return jnp.cos(ang), jnp.sin(ang)


def _step(w, m, v, x, target):
    seq = x.shape[0]
    seg_len = float(seq // SEGMENTS)
    c_idx = lax.axis_index("c").astype(jnp.int32)
    chip_idx = (2 * lax.axis_index("x") + lax.axis_index("y")).astype(jnp.int32)
    c_arr = jnp.stack([c_idx])
    idx_arr = jnp.stack([c_idx, chip_idx])

    chip_arr = jnp.stack([chip_idx])
    placed = {}
    for piece in _PIECES:
        for layer in range(DEPTH):
            key = (piece[1], layer)
            placed[key] = _cast_place(piece, w[piece[0]], layer, placed.get(key), chip_arr,
                                      "cast_%s_l%d" % (piece[0], layer))
    keys = [k for g in range(len(_GATHER_GROUPS)) for k in _group_keys(g)]
    flying, ssems, rsems, token = _gather_start([placed[k] for k in keys])
    wf = {b[0]: [None] * DEPTH for b in _BUFFERS}

    def arrive(g, after):
        ks = _group_keys(g)
        landed = _gather_wait(g, [flying[k] for k in ks], ssems[g], rsems[g], after)
        for k, a in zip(ks, _gather_forward(g, landed)):
            wf[k[0]][k[1]] = a

    cos, sin = _rope_tables(seq)

    saved = []
    xc = x + token[0, 0]
    for i in range(DEPTH):
        t = "_l%d" % i
        s = {"x_in": xc}
        s["h"] = _rms_fwd(xc, w["ln_mix_g"][i:i + 1], "rms_mix" + t)
        arrive(2 * i, s["h"])
        s["proj"] = _matmul(s["h"], wf["in"][i], "nn", [F32], name="mm_in" + t)[0]
        s["qr"], s["kr"] = _rot_fwd(s["proj"], cos, sin, "rot" + t)
        s["lg"] = jnp.broadcast_to(w["ret_log_gamma"][i].T[:, :, None], (HEADS, 2, LANES))
        s["y"] = _ret_fwd(s["qr"], s["kr"], s["proj"], s["lg"], "ret" + t)
        s5_raw = (w["ssm_a_re"][i], w["ssm_a_im"][i], w["ssm_log_dt"][i], w["ssm_b_re"][i], w["ssm_b_im"][i])
        disc, s["disc_vjp"] = jax.vjp(functools.partial(_s5_discretize, seg_len=seg_len), *s5_raw)
        bblk, cblk, lam = _s5_pack(*disc, w["ssm_c_re"][i], w["ssm_c_im"][i])
        s["s5"] = (bblk.astype(BF16), cblk.astype(BF16), lam)
        s["s5y"] = _s5_fwd(s["proj"], *s["s5"], "s5" + t)
        s["ret"], s["ysg"], s["ysgb"] = _post1_fwd(s["y"], s["proj"], s["s5y"], w["ssm_d"][i:i + 1], "post" + t)
        arrive(2 * i + 1, s["ysgb"])
        s["z"] = _matmul(s["ysgb"], wf["glu"][i], "nn", [F32], name="mm_glu" + t)[0]
        s["merged"] = _merge_fwd(s["z"], s["ysg"], s["proj"], s["ret"], w["b_glu"][i:i + 1], "merge" + t)
        s["x1"] = _matmul(s["merged"], wf["out"][i], "nn", [F32], add=xc, name="mm_out" + t)[0]
        s["h2"] = _rms_fwd(s["x1"], w["ln_ffn_g"][i:i + 1], "rms_ffn" + t)
        s["ab"] = _matmul(s["h2"], wf["gu"][i], "nn", [F32], name="mm_gu" + t)[0]
        s["f"] = _glu_fwd(s["ab"], "glu" + t)
        xc = _matmul(s["f"], wf["down"][i], "nn", [F32], add=s["x1"], name="mm_down" + t)[0]
        saved.append(s)

    dx, dxb, loss_row, dg_final = _loss_stage(xc, target, w["ln_final_g"][None, :], "loss")
    loss = lax.psum(loss_row[0, 0], ("x", "y", "c"))

    g_small = {"ln_final_g": dg_final[0]}
    per_layer = {n: [None] * DEPTH for n in _SMALL if n != "ln_final_g"}
    dws, got, flights = {}, {}, []

    def dw_mm(a, b, buf, i, name):
        dws[(buf, i)] = _matmul(a, b, "tn", [F32, BF16], name=name)

    def launch(g):
        group = _REDUCE_GROUPS[g]
        keys = _reduce_keys(group)
        got.update(_swap_halves(g, {k: dws[k][1] for k in keys}))
        partials = {}
        for p, layer in group:
            key = (_PIECES[p][1], layer)
            partials[key] = _chip_partial(_PIECES[p], dws[key][0], got[key], partials.get(key), c_arr,
                                          "chip_partial_%s_l%d" % (_PIECES[p][0], layer))
        pt, land, ssem, rsem, tok = _scatter_start(g, partials)
        flights.append((g, pt, land, ssem, rsem))
        return tok[0:1, 0:1]

    for i in reversed(range(DEPTH)):
        t = "_l%d" % i
        s = saved[i]
        g_ffn, g_mix, d_skip = w["ln_ffn_g"][i:i + 1], w["ln_mix_g"][i:i + 1], w["ssm_d"][i:i + 1]
        dw_mm(s["f"], dxb, "down", i, "dw_down" + t)
        df = _matmul(dxb, wf["down"][i], "nt", [F32], name="dx_down" + t)[0]
        dab = _glu_bwd(s["ab"], df, "glu_bwd" + t)
        dw_mm(s["h2"], dab, "gu", i, "dw_gu" + t)
        if i == 0:
            g_ffn = g_ffn + launch(1)
        dh2 = _matmul(dab, wf["gu"][i], "nt", [F32], name="dx_gu" + t)[0]
        dx1, dx1b, dg = _rms_bwd(s["x1"], dh2, dx, g_ffn, "rms_ffn_bwd" + t)
        per_layer["ln_ffn_g"][i] = dg[0]

        dw_mm(s["merged"], dx1b, "out", i, "dw_out" + t)
        dmerged = _matmul(dx1b, wf["out"][i], "nt", [F32], name="dx_out" + t)[0]
        dz, dys_part, dgs, db = _merge_bwd(s["z"], s["ysg"], s["proj"], s["ret"], dmerged, w["b_glu"][i:i + 1],
                                           "merge_bwd" + t)
        per_layer["b_glu"][i] = db[0]
        dw_mm(s["ysgb"], dz, "glu", i, "dw_glu" + t)
        if i == 0:
            d_skip = d_skip + launch(2)
        dys = _matmul(dz, wf["glu"][i], "nt", [F32], add=dys_part, name="dx_glu" + t)[0]
        dy, dgg, dgr, ds5, du_part, dd = _post1_bwd(s["y"], s["proj"], s["s5y"], dmerged, dys,
                                                    d_skip, "post_bwd" + t)
        per_layer["ssm_d"][i] = dd[0]
        du, dbblk, dcblk, dlam = _s5_bwd(s["proj"], ds5, du_part, *s["s5"], "s5_bwd" + t)
        dlr, dli, dbr, dbi, dcr, dci = _s5_unpack(dbblk, dcblk, dlam)
        zeros = jnp.zeros_like(dlr)
        da_re, da_im, dlog_dt, db_re, db_im = s["disc_vjp"]((dlr, dli, zeros, zeros, dbr, dbi))
        for n, val in (("ssm_a_re", da_re), ("ssm_a_im", da_im), ("ssm_log_dt", dlog_dt), ("ssm_b_re", db_re),
                       ("ssm_b_im", db_im), ("ssm_c_re", dcr), ("ssm_c_im", dci)):
            per_layer[n][i] = val
        dqr, dkr, dv, dlg = _ret_bwd(s["qr"], s["kr"], s["proj"], dy, s["lg"], "ret_bwd" + t)
        per_layer["ret_log_gamma"][i] = dlg[:, :, 0].T
        dqkv = _rot_bwd(dqr, dkr, dv, cos, sin, "rot_bwd" + t)
        dproj = jnp.concatenate([dqkv, dgg, du, dgr, dgs], axis=1)
        dw_mm(s["h"], dproj, "in", i, "dw_in" + t)
        if i == 0:
            g_mix = g_mix + launch(3)
        dh = _matmul(dproj, wf["in"][i], "nt", [F32], name="dx_in" + t)[0]
        dx, dxb, dg = _rms_bwd(s["x_in"], dh, dx1, g_mix, "rms_mix_bwd" + t)
        per_layer["ln_mix_g"][i] = dg[0]
        if i == DEPTH - 1:
            dxb = dxb + launch(0).astype(BF16)

    reduced = [None] * N_PIECES
    for g, pt, land, ssem, rsem in flights:
        landed = _scatter_wait(g, pt, land, ssem, rsem, dxb)
        for (p, layer), buf in zip(_REDUCE_GROUPS[g], landed):
            key = (_PIECES[p][1], layer)
            reduced[p] = _reduce_half(_PIECES[p], layer, dws[key][0], got[key], buf, reduced[p], idx_arr,
                                      "reduce_%s_l%d" % (_PIECES[p][0], layer))
    g_big = dict(zip([p[0] for p in _PIECES], _share_halves(reduced)))

    for n in per_layer:
        g_small[n] = jnp.stack(per_layer[n])
    g_flat = _sum_slots(_all_gather_rows(_flatten_small(g_small)), "sum_small")

    grads, delta, new_m, new_v = {}, {}, {}, {}
    for n in _BIG:
        d, r, cc = w[n].shape
        two_d = lambda a: a.reshape(d * r, cc)
        dl, mn, vn = _adamw(two_d(w[n]), two_d(g_big[n]), two_d(m[n]), two_d(v[n]), "adamw_" + n)
        grads[n], delta[n], new_m[n], new_v[n] = g_big[n], dl.reshape(d, r, cc), mn.reshape(d, r, cc), vn.reshape(d, r, cc)
    dl, mn, vn = _adamw(_flatten_small(w), g_flat, _flatten_small(m), _flatten_small(v), "adamw_small")
    for dst, flat in ((grads, g_flat), (delta, dl), (new_m, mn), (new_v, vn)):
        dst.update(_unflatten_small(flat, w))
    return loss, dx, grads, delta, new_m, new_v


def kernel(x, ln_mix_g, w_in, ret_log_gamma, ssm_a_re, ssm_a_im, ssm_log_dt, ssm_b_re, ssm_b_im, ssm_c_re, ssm_c_im, ssm_d, w_glu, b_glu, w_out, ln_ffn_g, w_ffn_gate, w_ffn_up, w_ffn_down, ln_final_g, loss_target, m_ln_mix_g, m_w_in, m_ret_log_gamma, m_ssm_a_re, m_ssm_a_im, m_ssm_log_dt, m_ssm_b_re, m_ssm_b_im, m_ssm_c_re, m_ssm_c_im, m_ssm_d, m_w_glu, m_b_glu, m_w_out, m_ln_ffn_g, m_w_ffn_gate, m_w_ffn_up, m_w_ffn_down, m_ln_final_g, v_ln_mix_g, v_w_in, v_ret_log_gamma, v_ssm_a_re, v_ssm_a_im, v_ssm_log_dt, v_ssm_b_re, v_ssm_b_im, v_ssm_c_re, v_ssm_c_im, v_ssm_d, v_w_glu, v_b_glu, v_w_out, v_ln_ffn_g, v_w_ffn_gate, v_w_ffn_up, v_w_ffn_down, v_ln_final_g):
    given = dict(locals())
    w = {n: given[n] for n in _WEIGHTS}
    m = {n: given["m_" + n] for n in _WEIGHTS}
    v = {n: given["v_" + n] for n in _WEIGHTS}
    loss, dx, grads, delta, new_m, new_v = _step(w, m, v, x[0], loss_target[0])
    return (loss, dx[None], *[grads[n] for n in _WEIGHTS], *[delta[n] for n in _WEIGHTS],
            *[new_m[n] for n in _WEIGHTS], *[new_v[n] for n in _WEIGHTS])
```

```python
import functools
import math

import jax
import jax.numpy as jnp
from jax import lax
from jax.experimental import pallas as pl
from jax.experimental.pallas import tpu as pltpu

F32 = jnp.float32
BF16 = jnp.bfloat16

D_MODEL = 2048
DEPTH = 2
HEADS = 4
QK_DIM = 256
V_DIM = 512
QK_WIDTH = HEADS * QK_DIM
ROPE_BASE = 10000.0
GROUP = 16
N_GROUPS = D_MODEL // GROUP
N_STATE = 64
D_FF = 5632
IN_WIDTH = 2 * QK_WIDTH + 5 * D_MODEL
EPS = 1e-6
N_CHIPS = 4

ADAM_LR = 0.001
ADAM_B1 = 0.9
ADAM_B2 = 0.999
ADAM_EPS = 1e-08
ADAM_WD = 0.01
ADAM_STEP = 10

LANES = 128
SUBLANES = 8
VMEM_LIMIT = 56 * 1024 * 1024
SEGMENTS = SUBLANES
GROUPS_PER_TILE = LANES // GROUP
STATE_COLS = GROUPS_PER_TILE * N_STATE
N_TILES = D_MODEL // LANES
SCAN_UNROLL = 4

MESH = pl.DeviceIdType.MESH
HBM_SPEC = pl.BlockSpec(memory_space=pltpu.HBM)


def _params(sem=None, **kw):
    return pltpu.CompilerParams(dimension_semantics=sem, vmem_limit_bytes=VMEM_LIMIT, **kw)


def _tile(n, cap=1024):
    for t in (1024, 512, 256, 128, 64):
        if t <= cap and n % t == 0:
            return t
    raise ValueError(n)


def _rows_call(fn, rows, pars, row_outs, par_outs, *, tm, name):
    m = rows[0][0].shape[0]
    nr, npar, nro, npo = len(rows), len(pars), len(row_outs), len(par_outs)

    def body(*refs):
        rin = refs[:nr]
        pin = refs[nr:nr + npar]
        rout = refs[nr + npar:nr + npar + nro]
        pout = refs[nr + npar + nro:]
        res = fn(*[r[...] for r in rin], *[p[...] for p in pin])
        if not isinstance(res, (tuple, list)):
            res = (res,)
        for r, v in zip(rout, res[:nro]):
            r[...] = v.astype(r.dtype)
        if npo:
            @pl.when(pl.program_id(0) == 0)
            def _():
                for p in pout:
                    p[...] = jnp.zeros(p.shape, p.dtype)
            for p, v in zip(pout, res[nro:]):
                p[...] += v

    in_specs = [pl.BlockSpec((tm, w), functools.partial(lambda cb, i: (i, cb), cb)) for (_, w, cb) in rows]
    in_specs += [pl.BlockSpec(p.shape, lambda i: (0, 0)) for p in pars]
    out_specs = [pl.BlockSpec((tm, w), lambda i: (i, 0)) for (w, _) in row_outs]
    out_specs += [pl.BlockSpec(s, lambda i: (0, 0)) for s in par_outs]
    out_shape = [jax.ShapeDtypeStruct((m, w), dt) for (w, dt) in row_outs]
    out_shape += [jax.ShapeDtypeStruct(s, F32) for s in par_outs]
    res = pl.pallas_call(
        body, name=name, grid=(m // tm,), in_specs=in_specs, out_specs=out_specs, out_shape=out_shape,
        compiler_params=_params(("arbitrary",) if npo else ("parallel",)),
    )(*[a for (a, _, _) in rows], *pars)
    return res


def _f32(*vals):
    return [v.astype(F32) for v in vals]


def _f_rms(x, g):
    r = lax.rsqrt(jnp.mean(x * x, axis=-1, keepdims=True) + EPS)
    return x * r * g


def _rms_fwd(x, g, name):
    return _rows_call(lambda xv, gv: _f_rms(xv, gv), [(x, D_MODEL, 0)], [g], [(D_MODEL, BF16)], [],
                      tm=256, name=name)[0]


def _rms_bwd(x, dh, dres, g, name):
    def fn(xv, dhv, drv, gv):
        _, vjp = jax.vjp(_f_rms, xv, gv)
        dx, dg = vjp(dhv)
        dx = dx + drv
        return dx, dx, dg
    return _rows_call(fn, [(x, D_MODEL, 0), (dh, D_MODEL, 0), (dres, D_MODEL, 0)], [g],
                      [(D_MODEL, F32), (D_MODEL, BF16)], [(1, D_MODEL)], tm=256, name=name)


def _rot_heads(xv, cos, sin, scale):
    half = QK_DIM // 2
    outs = []
    for h in range(HEADS):
        x1 = xv[:, h * QK_DIM:h * QK_DIM + half]
        x2 = xv[:, h * QK_DIM + half:(h + 1) * QK_DIM]
        outs += [(x1 * cos - x2 * sin) * scale, (x1 * sin + x2 * cos) * scale]
    return jnp.concatenate(outs, axis=1)


def _rot_fwd(proj, cos, sin, name):
    def fn(q, k, cv, sv):
        return _rot_heads(q, cv, sv, 1.0), _rot_heads(k, cv, sv, QK_DIM ** -0.5)
    return _rows_call(fn, [(proj, QK_WIDTH, 0), (proj, QK_WIDTH, 1), (cos, LANES, 0), (sin, LANES, 0)], [],
                      [(QK_WIDTH, BF16), (QK_WIDTH, BF16)], [], tm=256, name=name)


def _rot_bwd(dqr, dkr, dv, cos, sin, name):
    def fn(dq, dk, dvv, cv, sv):
        return jnp.concatenate([_rot_heads(dq, cv, -sv, 1.0), _rot_heads(dk, cv, -sv, QK_DIM ** -0.5), dvv], axis=1)
    return _rows_call(fn, [(dqr, QK_WIDTH, 0), (dkr, QK_WIDTH, 0), (dv, D_MODEL, 0), (cos, LANES, 0), (sin, LANES, 0)],
                      [], [(2 * QK_WIDTH + D_MODEL, BF16)], [], tm=256, name=name)[0]


def _f_post1(y0, y1, y2, y3, g, gr, s5, u, dsk):
    yn = [yh * lax.rsqrt(jnp.mean(yh * yh, axis=-1, keepdims=True) + EPS) for yh in (y0, y1, y2, y3)]
    ret = jax.nn.sigmoid(gr) * (jax.nn.silu(g) * jnp.concatenate(yn, axis=1))
    ysg = jax.nn.gelu(s5 + dsk * u)
    return ret, ysg


def _post1_rows(y, proj, s5y):
    rows = [(y, V_DIM, h) for h in range(HEADS)]
    rows += [(proj, D_MODEL, 2), (proj, D_MODEL, 4), (s5y, D_MODEL, 0), (proj, D_MODEL, 3)]
    return rows


def _post1_fwd(y, proj, s5y, dsk, name):
    def fn(*vals):
        ret, ysg = _f_post1(*vals)
        return ret, ysg, ysg
    return _rows_call(fn, _post1_rows(y, proj, s5y), [dsk],
                      [(D_MODEL, F32), (D_MODEL, F32), (D_MODEL, BF16)], [], tm=128, name=name)


def _post1_bwd(y, proj, s5y, dret, dys, dsk, name):
    def fn(*vals):
        prim = vals[:8] + (vals[10],)
        _, vjp = jax.vjp(_f_post1, *prim)
        gy0, gy1, gy2, gy3, gg, ggr, gs5, gu, gd = vjp((vals[8], vals[9]))
        return jnp.concatenate([gy0, gy1, gy2, gy3], axis=1), gg, ggr, gs5, gu, gd
    rows = _post1_rows(y, proj, s5y) + [(dret, D_MODEL, 0), (dys, D_MODEL, 0)]
    return _rows_call(fn, rows, [dsk],
                      [(D_MODEL, BF16), (D_MODEL, BF16), (D_MODEL, BF16), (D_MODEL, F32), (D_MODEL, F32)],
                      [(1, D_MODEL)], tm=128, name=name)


def _f_merge(z, ysg, gs, ret, b):
    return ret + jax.nn.sigmoid(gs) * (ysg * jax.nn.sigmoid(z + b))


def _merge_fwd(z, ysg, proj, ret, b, name):
    return _rows_call(_f_merge, [(z, D_MODEL, 0), (ysg, D_MODEL, 0), (proj, D_MODEL, 5), (ret, D_MODEL, 0)], [b],
                      [(D_MODEL, BF16)], [], tm=128, name=name)[0]


def _merge_bwd(z, ysg, proj, ret, dm, b, name):
    def fn(zv, yv, gv, rv, dmv, bv):
        _, vjp = jax.vjp(_f_merge, zv, yv, gv, rv, bv)
        gz, gy, gg, _, gb = vjp(dmv)
        return gz, gy, gg, gb
    rows = [(z, D_MODEL, 0), (ysg, D_MODEL, 0), (proj, D_MODEL, 5), (ret, D_MODEL, 0), (dm, D_MODEL, 0)]
    return _rows_call(fn, rows, [b], [(D_MODEL, BF16), (D_MODEL, F32), (D_MODEL, BF16)], [(1, D_MODEL)],
                      tm=128, name=name)


def _f_glu(a, b):
    return jax.nn.silu(a) * b


def _glu_fwd(ab, name):
    return _rows_call(_f_glu, [(ab, D_FF, 0), (ab, D_FF, 1)], [], [(D_FF, BF16)], [], tm=128, name=name)[0]


def _glu_bwd(ab, df, name):
    def fn(a, b, d):
        _, vjp = jax.vjp(_f_glu, a, b)
        ga, gb = vjp(d)
        return jnp.concatenate([ga, gb], axis=1)
    return _rows_call(fn, [(ab, D_FF, 0), (ab, D_FF, 1), (df, D_FF, 0)], [], [(2 * D_FF, BF16)], [],
                      tm=128, name=name)[0]


def _loss_stage(x, tgt, g, name):
    def fn(xv, tv, gv):
        def lf(xx, gg):
            err = _f_rms(xx, gg) - tv
            row = jnp.mean(err * err, axis=-1, keepdims=True)
            return 0.5 * jnp.sum(row, axis=0, keepdims=True)
        l, vjp = jax.vjp(lf, xv, gv)
        dx, dg = vjp(jnp.ones((1, 1), F32))
        return dx, dx, jnp.broadcast_to(l, (1, LANES)), dg
    return _rows_call(fn, [(x, D_MODEL, 0), (tgt, D_MODEL, 0)], [g], [(D_MODEL, F32), (D_MODEL, BF16)],
                      [(1, LANES), (1, D_MODEL)], tm=256, name=name)


def _adam_math(wv, gv, mv, vv):
    mn = ADAM_B1 * mv + (1.0 - ADAM_B1) * gv
    vn = ADAM_B2 * vv + (1.0 - ADAM_B2) * (gv * gv)
    m_hat = mn / (1.0 - ADAM_B1 ** ADAM_STEP)
    v_hat = vn / (1.0 - ADAM_B2 ** ADAM_STEP)
    delta = -ADAM_LR * (m_hat / (jnp.sqrt(v_hat) + ADAM_EPS) + ADAM_WD * wv)
    return delta, mn, vn


def _adamw(w, g, m, v, name):
    rows, cols = w.shape
    tm = _tile(rows, 128 if cols > D_FF // N_CHIPS else (256 if cols > LANES else 512))
    return _rows_call(_adam_math, [(w, cols, 0), (g, cols, 0), (m, cols, 0), (v, cols, 0)], [],
                      [(cols, F32)] * 3, [], tm=tm, name=name)


def _adamw_nd(w, g, m, v, name):
    shape = w.shape
    lead = math.prod(shape[:-2])
    blk = (lead // 8,) + shape[-2:]
    three_d = lambda a: a.reshape((lead,) + shape[-2:])

    def body(w_ref, g_ref, m_ref, v_ref, d_ref, mn_ref, vn_ref):
        d_ref[...], mn_ref[...], vn_ref[...] = _adam_math(w_ref[...], g_ref[...], m_ref[...], v_ref[...])

    spec = pl.BlockSpec(blk, lambda i: (i, 0, 0))
    res = pl.pallas_call(
        body, name=name, grid=(8,), in_specs=[spec] * 4, out_specs=[spec] * 3,
        out_shape=[jax.ShapeDtypeStruct((lead,) + shape[-2:], F32)] * 3,
        compiler_params=_params(("parallel",)),
    )(three_d(w), three_d(g), three_d(m), three_d(v))
    return [r.reshape(shape) for r in res]


def _matmul(a, b, mode, out_dtypes, *, name, add=None):
    if mode == "nn":
        (m, k), (_, n) = a.shape, b.shape
    elif mode == "nt":
        (m, k), (n, _) = a.shape, b.shape
    else:
        (k, m), (_, n) = a.shape, b.shape
    tm, tn, tk = _tile(m), _tile(n), _tile(k)
    nk = k // tk
    if mode == "nn":
        a_spec = pl.BlockSpec((tm, tk), lambda i, j, kk: (i, kk))
        b_spec = pl.BlockSpec((tk, tn), lambda i, j, kk: (kk, j))
        dims = (((1,), (0,)), ((), ()))
    elif mode == "nt":
        a_spec = pl.BlockSpec((tm, tk), lambda i, j, kk: (i, kk))
        b_spec = pl.BlockSpec((tn, tk), lambda i, j, kk: (j, kk))
        dims = (((1,), (1,)), ((), ()))
    else:
        a_spec = pl.BlockSpec((tk, tm), lambda i, j, kk: (kk, i))
        b_spec = pl.BlockSpec((tk, tn), lambda i, j, kk: (kk, j))
        dims = (((0,), (0,)), ((), ()))
    n_out = len(out_dtypes)
    has_add = add is not None

    def body(*refs):
        a_ref, b_ref = refs[0], refs[1]
        add_ref = refs[2] if has_add else None
        outs = refs[2 + has_add:2 + has_add + n_out]
        acc = refs[-1]
        kk = pl.program_id(2)

        @pl.when(kk == 0)
        def _():
            acc[...] = jnp.zeros(acc.shape, F32)

        acc[...] += lax.dot_general(a_ref[...], b_ref[...], dims, preferred_element_type=F32)

        @pl.when(kk == nk - 1)
        def _():
            r = acc[...]
            if has_add:
                r = r + add_ref[...]
            for o in outs:
                o[...] = r.astype(o.dtype)

    in_specs = [a_spec, b_spec]
    args = [a, b]
    if has_add:
        in_specs.append(pl.BlockSpec((tm, tn), lambda i, j, kk: (i, j)))
        args.append(add)
    return pl.pallas_call(
        body, name=name, grid=(m // tm, n // tn, nk), in_specs=in_specs,
        out_specs=[pl.BlockSpec((tm, tn), lambda i, j, kk: (i, j))] * n_out,
        out_shape=[jax.ShapeDtypeStruct((m, n), dt) for dt in out_dtypes],
        scratch_shapes=[pltpu.VMEM((tm, tn), F32)],
        compiler_params=_params(("parallel", "parallel", "arbitrary")),
    )(*args)


RET_TQ = 512


def _decay(lg_ref, i, tq, seq):
    n_idx = i * tq + lax.broadcasted_iota(jnp.int32, (tq, seq), 0)
    m_idx = lax.broadcasted_iota(jnp.int32, (tq, seq), 1)
    diff = (n_idx - m_idx).astype(F32)
    lgf = lg_ref[0, 0:1, 0:1]
    lgb = lg_ref[0, 1:2, 0:1]
    causal = diff >= 0
    return jnp.exp(jnp.where(causal, lgf * diff, -lgb * diff)), diff, causal


_NT = (((1,), (1,)), ((), ()))
_TN = (((0,), (0,)), ((), ()))


def _ret_fwd(qr, kr, proj, lg, name):
    seq = qr.shape[0]
    tq = RET_TQ
    v_blk0 = (2 * QK_WIDTH) // V_DIM

    def body(q_ref, k_ref, v_ref, lg_ref, y_ref):
        i = pl.program_id(1)
        s = lax.dot_general(q_ref[...], k_ref[...], _NT, preferred_element_type=F32)
        dm, _, _ = _decay(lg_ref, i, tq, seq)
        p = (s * dm).astype(BF16)
        y_ref[...] = jnp.dot(p, v_ref[...].astype(BF16), preferred_element_type=F32)

    return pl.pallas_call(
        body, name=name, grid=(HEADS, seq // tq),
        in_specs=[pl.BlockSpec((tq, QK_DIM), lambda h, i: (i, h)),
                  pl.BlockSpec((seq, QK_DIM), lambda h, i: (0, h)),
                  pl.BlockSpec((seq, V_DIM), lambda h, i: (0, v_blk0 + h)),
                  pl.BlockSpec((1, 2, LANES), lambda h, i: (h, 0, 0))],
        out_specs=pl.BlockSpec((tq, V_DIM), lambda h, i: (i, h)),
        out_shape=jax.ShapeDtypeStruct((seq, HEADS * V_DIM), F32),
        compiler_params=_params(("parallel", "parallel")),
    )(qr, kr, proj, lg)


def _ret_bwd(qr, kr, proj, dy, lg, name):
    seq = qr.shape[0]
    tq = RET_TQ
    v_blk0 = (2 * QK_WIDTH) // V_DIM

    def body(q_ref, k_ref, v_ref, dy_ref, lg_ref, dq_ref, dk_ref, dv_ref, dlg_ref):
        i = pl.program_id(1)

        @pl.when(i == 0)
        def _():
            dk_ref[...] = jnp.zeros(dk_ref.shape, F32)
            dv_ref[...] = jnp.zeros(dv_ref.shape, F32)
            dlg_ref[...] = jnp.zeros(dlg_ref.shape, F32)

        q = q_ref[...]
        k = k_ref[...]
        vb = v_ref[...].astype(BF16)
        dyb = dy_ref[...]
        s = lax.dot_general(q, k, _NT, preferred_element_type=F32)
        dm, diff, causal = _decay(lg_ref, i, tq, seq)
        p = s * dm
        dp = lax.dot_general(dyb, vb, _NT, preferred_element_type=F32)
        dv_ref[...] += lax.dot_general(p.astype(BF16), dyb, _TN, preferred_element_type=F32)
        ds = (dp * dm).astype(BF16)
        dq_ref[...] = jnp.dot(ds, k, preferred_element_type=F32)
        dk_ref[...] += lax.dot_general(ds, q, _TN, preferred_element_type=F32)
        gd = dp * p * diff
        dlf = jnp.sum(jnp.sum(jnp.where(causal, gd, 0.0), axis=1, keepdims=True), axis=0, keepdims=True)
        dlb = jnp.sum(jnp.sum(jnp.where(causal, 0.0, -gd), axis=1, keepdims=True), axis=0, keepdims=True)
        row = lax.broadcasted_iota(jnp.int32, (2, LANES), 0)
        dlg_ref[0] += jnp.where(row == 0, dlf, dlb)

    return pl.pallas_call(
        body, name=name, grid=(HEADS, seq // tq),
        in_specs=[pl.BlockSpec((tq, QK_DIM), lambda h, i: (i, h)),
                  pl.BlockSpec((seq, QK_DIM), lambda h, i: (0, h)),
                  pl.BlockSpec((seq, V_DIM), lambda h, i: (0, v_blk0 + h)),
                  pl.BlockSpec((tq, V_DIM), lambda h, i: (i, h)),
                  pl.BlockSpec((1, 2, LANES), lambda h, i: (h, 0, 0))],
        out_specs=[pl.BlockSpec((tq, QK_DIM), lambda h, i: (i, h)),
                   pl.BlockSpec((seq, QK_DIM), lambda h, i: (0, h)),
                   pl.BlockSpec((seq, V_DIM), lambda h, i: (0, h)),
                   pl.BlockSpec((1, 2, LANES), lambda h, i: (h, 0, 0))],
        out_shape=[jax.ShapeDtypeStruct((seq, QK_WIDTH), F32), jax.ShapeDtypeStruct((seq, QK_WIDTH), F32),
                   jax.ShapeDtypeStruct((seq, HEADS * V_DIM), F32), jax.ShapeDtypeStruct((HEADS, 2, LANES), F32)],
        compiler_params=_params(("parallel", "arbitrary")),
    )(qr, kr, proj, dy, lg)


def _shift_rows(v, reverse):
    row = lax.broadcasted_iota(jnp.int32, v.shape, 0)
    if reverse:
        return jnp.where(row == SEGMENTS - 1, 0.0, pltpu.roll(v, SEGMENTS - 1, 0))
    return jnp.where(row == 0, 0.0, pltpu.roll(v, 1, 0))


def _slab(t):
    if isinstance(t, int):
        return pl.ds(t * SEGMENTS, SEGMENTS)
    return pl.ds(pl.multiple_of(t * SEGMENTS, SEGMENTS), SEGMENTS)


def _unrolled_loop(body, lo, hi, init):
    main = (hi - lo) // SCAN_UNROLL

    def unrolled(g, carry):
        for k in range(SCAN_UNROLL):
            carry = body(lo + g * SCAN_UNROLL + k, carry)
        return carry

    carry = lax.fori_loop(0, main, unrolled, init)
    for t in range(lo + main * SCAN_UNROLL, hi):
        carry = body(t, carry)
    return carry


def _scan(xr_ref, xi_ref, lam, reverse, conj):
    steps = xr_ref.shape[0] // SEGMENTS
    cols = xr_ref.shape[1]
    lr = jnp.broadcast_to(lam[0], (SEGMENTS, cols))
    li = jnp.broadcast_to(lam[1], (SEGMENTS, cols))
    lrt = jnp.broadcast_to(lam[2], (SEGMENTS, cols))
    lit = jnp.broadcast_to(lam[3], (SEGMENTS, cols))
    if conj:
        li, lit = -li, -lit
    zero = jnp.zeros((SEGMENTS, cols), F32)

    def rows_of(t):
        return _slab(steps - 1 - t if reverse else t)

    def advance(t, carry):
        sr, si = carry
        rows = rows_of(t)
        return lr * sr - li * si + xr_ref[rows, :], lr * si + li * sr + xi_ref[rows, :]

    def step(t, carry):
        nr, ni = advance(t, carry)
        rows = rows_of(t)
        xr_ref[rows, :] = nr
        xi_ref[rows, :] = ni
        return nr, ni

    def run(body, init):
        return _unrolled_loop(body, 0, steps, init)

    er, ei = run(advance, (zero, zero))
    cr, ci = zero, zero
    for _ in range(SEGMENTS - 1):
        tr = er + lrt * cr - lit * ci
        ti = ei + lrt * ci + lit * cr
        cr, ci = _shift_rows(tr, reverse), _shift_rows(ti, reverse)
    run(step, (cr, ci))


def _permute_in(dst_ref, src_ref):
    steps = src_ref.shape[0] // SEGMENTS
    for s in range(SEGMENTS):
        dst_ref[pl.ds(s, steps, stride=SEGMENTS), :] = src_ref[s * steps:(s + 1) * steps, :].astype(dst_ref.dtype)


def _unpermute(src_ref, s):
    steps = src_ref.shape[0] // SEGMENTS
    return src_ref[pl.ds(s, steps, stride=SEGMENTS), :]


def _s5_fwd(proj, bblk, cblk, lam, name):
    seq = proj.shape[0]
    u_blk0 = (2 * QK_WIDTH + 2 * D_MODEL) // LANES
    sc = STATE_COLS

    def body(u_ref, b_ref, c_ref, lam_ref, y_ref, up_ref, yp_ref, xr_ref, xi_ref):
        _permute_in(up_ref, u_ref)
        ub = up_ref[...].astype(BF16)
        for d in range(2):
            xr_ref[...] = jnp.dot(ub, b_ref[d, :, 0:sc], preferred_element_type=F32)
            xi_ref[...] = jnp.dot(ub, b_ref[d, :, sc:2 * sc], preferred_element_type=F32)
            lm = [lam_ref[d, r:r + 1, :] for r in range(4)]
            _scan(xr_ref, xi_ref, lm, reverse=(d == 1), conj=False)
            yd = (jnp.dot(xr_ref[...].astype(BF16), c_ref[d, 0:sc, :], preferred_element_type=F32)
                  + jnp.dot(xi_ref[...].astype(BF16), c_ref[d, sc:2 * sc, :], preferred_element_type=F32))
            if d == 0:
                yp_ref[...] = yd
            else:
                yp_ref[...] += yd
        steps = seq // SEGMENTS
        for s in range(SEGMENTS):
            y_ref[s * steps:(s + 1) * steps, :] = _unpermute(yp_ref, s)

    return pl.pallas_call(
        body, name=name, grid=(N_TILES,),
        in_specs=[pl.BlockSpec((seq, LANES), lambda j: (0, u_blk0 + j)),
                  pl.BlockSpec((2, None, LANES, 2 * sc), lambda j: (0, j, 0, 0)),
                  pl.BlockSpec((2, None, 2 * sc, LANES), lambda j: (0, j, 0, 0)),
                  pl.BlockSpec((2, None, 4, sc), lambda j: (0, j, 0, 0))],
        out_specs=pl.BlockSpec((seq, LANES), lambda j: (0, j)),
        out_shape=jax.ShapeDtypeStruct((seq, D_MODEL), F32),
        scratch_shapes=[pltpu.VMEM((seq, LANES), F32), pltpu.VMEM((seq, LANES), F32),
                        pltpu.VMEM((seq, sc), F32), pltpu.VMEM((seq, sc), F32)],
        compiler_params=_params(("parallel",)),
    )(proj, bblk, cblk, lam)


def _s5_bwd(proj, dy, du_part, bblk, cblk, lam, name):
    seq = proj.shape[0]
    u_blk0 = (2 * QK_WIDTH + 2 * D_MODEL) // LANES
    sc = STATE_COLS
    steps = seq // SEGMENTS

    def body(u_ref, dy_ref, dup_ref, b_ref, c_ref, lam_ref, du_ref, db_ref, dc_ref, dlam_ref,
             up_ref, dyp_ref, dua_ref, xr_ref, xi_ref, gr_ref, gi_ref):
        _permute_in(up_ref, u_ref)
        _permute_in(dyp_ref, dy_ref)
        ub = up_ref[...].astype(BF16)
        dyb = dyp_ref[...].astype(BF16)
        ubt = up_ref[...].T.astype(BF16)
        dybt = dyp_ref[...].T.astype(BF16)
        for d in range(2):
            reverse = d == 1
            xr_ref[...] = jnp.dot(ub, b_ref[d, :, 0:sc], preferred_element_type=F32)
            xi_ref[...] = jnp.dot(ub, b_ref[d, :, sc:2 * sc], preferred_element_type=F32)
            lm = [lam_ref[d, r:r + 1, :] for r in range(4)]
            _scan(xr_ref, xi_ref, lm, reverse=reverse, conj=False)
            xrb = xr_ref[...].astype(BF16)
            xib = xi_ref[...].astype(BF16)
            dc_ref[d, :, 0:sc] = jnp.dot(dybt, xrb, preferred_element_type=F32)
            dc_ref[d, :, sc:2 * sc] = jnp.dot(dybt, xib, preferred_element_type=F32)
            gr_ref[...] = lax.dot_general(dyb, c_ref[d, 0:sc, :], _NT, preferred_element_type=F32)
            gi_ref[...] = lax.dot_general(dyb, c_ref[d, sc:2 * sc, :], _NT, preferred_element_type=F32)
            _scan(gr_ref, gi_ref, lm, reverse=not reverse, conj=True)

            def acc_step(t, carry):
                ar, ai = carry
                prev = _slab(t + 1 if reverse else t - 1)
                pr = xr_ref[prev, :]
                pi = xi_ref[prev, :]
                zr = gr_ref[_slab(t), :]
                zi = gi_ref[_slab(t), :]
                return ar + zr * pr + zi * pi, ai + zi * pr - zr * pi

            zero = jnp.zeros((SEGMENTS, sc), F32)
            if reverse:
                ar, ai = _unrolled_loop(acc_step, 0, steps - 1, (zero, zero))
                edge = _slab(steps - 1)
                pr = _shift_rows(xr_ref[_slab(0), :], True)
                pi = _shift_rows(xi_ref[_slab(0), :], True)
            else:
                ar, ai = _unrolled_loop(acc_step, 1, steps, (zero, zero))
                edge = _slab(0)
                pr = _shift_rows(xr_ref[_slab(steps - 1), :], False)
                pi = _shift_rows(xi_ref[_slab(steps - 1), :], False)
            zr = gr_ref[edge, :]
            zi = gi_ref[edge, :]
            ar = ar + zr * pr + zi * pi
            ai = ai + zi * pr - zr * pi
            dlam_ref[d, 0:1, :] = jnp.sum(ar, axis=0, keepdims=True)
            dlam_ref[d, 1:2, :] = jnp.sum(ai, axis=0, keepdims=True)

            grb = gr_ref[...].astype(BF16)
            gib = gi_ref[...].astype(BF16)
            db_ref[d, :, 0:sc] = jnp.dot(ubt, grb, preferred_element_type=F32)
            db_ref[d, :, sc:2 * sc] = jnp.dot(ubt, gib, preferred_element_type=F32)
            dud = (lax.dot_general(grb, b_ref[d, :, 0:sc], _NT, preferred_element_type=F32)
                   + lax.dot_general(gib, b_ref[d, :, sc:2 * sc], _NT, preferred_element_type=F32))
            if d == 0:
                dua_ref[...] = dud
            else:
                dua_ref[...] += dud
        for s in range(SEGMENTS):
            rows = slice(s * steps, (s + 1) * steps)
            du_ref[rows, :] = (_unpermute(dua_ref, s) + dup_ref[rows, :]).astype(du_ref.dtype)

    return pl.pallas_call(
        body, name=name, grid=(N_TILES,),
        in_specs=[pl.BlockSpec((seq, LANES), lambda j: (0, u_blk0 + j)),
                  pl.BlockSpec((seq, LANES), lambda j: (0, j)),
                  pl.BlockSpec((seq, LANES), lambda j: (0, j)),
                  pl.BlockSpec((2, None, LANES, 2 * sc), lambda j: (0, j, 0, 0)),
                  pl.BlockSpec((2, None, 2 * sc, LANES), lambda j: (0, j, 0, 0)),
                  pl.BlockSpec((2, None, 4, sc), lambda j: (0, j, 0, 0))],
        out_specs=[pl.BlockSpec((seq, LANES), lambda j: (0, j)),
                   pl.BlockSpec((2, None, LANES, 2 * sc), lambda j: (0, j, 0, 0)),
                   pl.BlockSpec((2, None, LANES, 2 * sc), lambda j: (0, j, 0, 0)),
                   pl.BlockSpec((2, None, 2, sc), lambda j: (0, j, 0, 0))],
        out_shape=[jax.ShapeDtypeStruct((seq, D_MODEL), BF16),
                   jax.ShapeDtypeStruct((2, N_TILES, LANES, 2 * sc), F32),
                   jax.ShapeDtypeStruct((2, N_TILES, LANES, 2 * sc), F32),
                   jax.ShapeDtypeStruct((2, N_TILES, 2, sc), F32)],
        scratch_shapes=[pltpu.VMEM((seq, LANES), F32), pltpu.VMEM((seq, LANES), F32), pltpu.VMEM((seq, LANES), F32),
                        pltpu.VMEM((seq, sc), F32), pltpu.VMEM((seq, sc), F32),
                        pltpu.VMEM((seq, sc), F32), pltpu.VMEM((seq, sc), F32)],
        compiler_params=_params(("parallel",)),
    )(proj, dy, du_part, bblk, cblk, lam)


def _s5_discretize(a_re, a_im, log_dt, b_re, b_im, seg_len):
    dt = jnp.exp(log_dt)[..., None]
    e = jnp.exp(a_re * dt)
    lr, li = e * jnp.cos(a_im * dt), e * jnp.sin(a_im * dt)
    et = jnp.exp(a_re * dt * seg_len)
    lrt, lit = et * jnp.cos(a_im * dt * seg_len), et * jnp.sin(a_im * dt * seg_len)
    den = a_re * a_re + a_im * a_im
    qr = ((lr - 1.0) * a_re + li * a_im) / den
    qi = (li * a_re - (lr - 1.0) * a_im) / den
    br = qr[..., None] * b_re - qi[..., None] * b_im
    bi = qr[..., None] * b_im + qi[..., None] * b_re
    return lr, li, lrt, lit, br, bi


def _s5_pack(lr, li, lrt, lit, br, bi, c_re, c_im):
    eye = jnp.eye(GROUPS_PER_TILE, dtype=F32)

    def bd_b(b):
        b5 = b.reshape(2, N_TILES, GROUPS_PER_TILE, N_STATE, GROUP)
        return jnp.einsum("dtgph,gk->dtghkp", b5, eye).reshape(2, N_TILES, LANES, STATE_COLS)

    def bd_c(c):
        c5 = c.reshape(2, N_TILES, GROUPS_PER_TILE, GROUP, N_STATE)
        return jnp.einsum("dtghp,gk->dtkpgh", c5, eye).reshape(2, N_TILES, STATE_COLS, LANES)

    bblk = jnp.concatenate([bd_b(br), bd_b(bi)], axis=3)
    cblk = jnp.concatenate([bd_c(c_re), -bd_c(c_im)], axis=2)
    lam = jnp.stack([v.reshape(2, N_TILES, STATE_COLS) for v in (lr, li, lrt, lit)], axis=2)
    return bblk, cblk, lam


def _s5_unpack(dbblk, dcblk, dlam):
    eye = jnp.eye(GROUPS_PER_TILE, dtype=F32)

    def diag_b(d):
        d6 = d.reshape(2, N_TILES, GROUPS_PER_TILE, GROUP, GROUPS_PER_TILE, N_STATE)
        return jnp.einsum("dtghkp,gk->dtgph", d6, eye).reshape(2, N_GROUPS, N_STATE, GROUP)

    def diag_c(d):
        d6 = d.reshape(2, N_TILES, GROUPS_PER_TILE, GROUP, GROUPS_PER_TILE, N_STATE)
        return jnp.einsum("dtghkp,gk->dtghp", d6, eye).reshape(2, N_GROUPS, GROUP, N_STATE)

    dbr, dbi = diag_b(dbblk[..., :STATE_COLS]), diag_b(dbblk[..., STATE_COLS:])
    dcr, dci = diag_c(dcblk[..., :STATE_COLS]), -diag_c(dcblk[..., STATE_COLS:])
    dlr = dlam[:, :, 0, :].reshape(2, N_GROUPS, N_STATE)
    dli = dlam[:, :, 1, :].reshape(2, N_GROUPS, N_STATE)
    return dlr, dli, dbr, dbi, dcr, dci


def _pos():
    return lax.axis_index("x"), lax.axis_index("y"), lax.axis_index("c")


def _remote(src, dst, ssem, rsem, dev):
    return pltpu.make_async_remote_copy(src_ref=src, dst_ref=dst, send_sem=ssem, recv_sem=rsem,
                                        device_id=dev, device_id_type=MESH)


_PIECES = (
    ("w_in", "in", D_MODEL, IN_WIDTH // N_CHIPS, 0, IN_WIDTH // N_CHIPS, 0),
    ("w_glu", "glu", D_MODEL // N_CHIPS, D_MODEL, D_MODEL // N_CHIPS, 0, 0),
    ("w_out", "out", D_MODEL // N_CHIPS, D_MODEL, D_MODEL // N_CHIPS, 0, 0),
    ("w_ffn_gate", "gu", D_MODEL, D_FF // N_CHIPS, 0, D_FF // N_CHIPS, 0),
    ("w_ffn_up", "gu", D_MODEL, D_FF // N_CHIPS, 0, D_FF // N_CHIPS, D_FF),
    ("w_ffn_down", "down", D_FF // N_CHIPS, D_MODEL, D_FF // N_CHIPS, 0, 0),
)
_BUFFERS = (("in", D_MODEL, IN_WIDTH), ("glu", D_MODEL, D_MODEL), ("out", D_MODEL, D_MODEL),
            ("gu", D_MODEL, 2 * D_FF), ("down", D_FF, D_MODEL))
_BUF_INDEX = {name: t for t, (name, _, _) in enumerate(_BUFFERS)}
N_PIECES = len(_PIECES)
N_BUFFERS = len(_BUFFERS)


def _own_block(piece, tm):
    _, _, _, cs, rstep, cstep, coff = piece
    return lambda i, chip: (i + chip * (rstep // tm), coff // cs + chip * (cstep // cs))


def _cast_place(piece, w3, layer, prev, chip_arr, name):
    _, r, cc = w3.shape
    _, rf, cf = _BUFFERS[_BUF_INDEX[piece[1]]]
    tm = _tile(r, 256)
    own = _own_block(piece, tm)

    def body(s_ref, w_ref, *rest):
        rest[-1][...] = w_ref[...].astype(BF16)

    in_specs = [pl.BlockSpec((None, tm, cc), lambda i, s: (layer, i, 0))]
    args = [w3]
    aliases = {}
    if prev is not None:
        in_specs.append(pl.BlockSpec(memory_space=pl.ANY))
        args.append(prev)
        aliases = {2: 0}
    return pl.pallas_call(
        body, name=name,
        grid_spec=pltpu.PrefetchScalarGridSpec(
            num_scalar_prefetch=1, grid=(r // tm,), in_specs=in_specs,
            out_specs=pl.BlockSpec((tm, cc), lambda i, s: own(i, s[0]))),
        out_shape=jax.ShapeDtypeStruct((rf, cf), BF16), input_output_aliases=aliases,
        compiler_params=_params(("parallel",)),
    )(chip_arr, *args)


_GATHER_GROUPS = ((0, (0,)), (0, (1, 2, 3, 4, 5)), (1, (0,)), (1, (1, 2, 3, 4, 5)))
_SPLIT_EFFECT = pltpu.SideEffectType.DATAFLOW_SIDE_EFFECTING
SEM_SPEC = pl.BlockSpec(memory_space=pltpu.SEMAPHORE)
BF16_ROWS = 2 * SUBLANES


def _group_keys(g):
    layer, pieces = _GATHER_GROUPS[g]
    keys = []
    for p in pieces:
        if (_PIECES[p][1], layer) not in keys:
            keys.append((_PIECES[p][1], layer))
    return keys


def _half_view(ref, piece, j, c):
    _, _, rs, cs, rstep, cstep, coff = piece
    half = rs // 2
    return ref.at[pl.ds(pl.multiple_of(j * rstep + c * half, BF16_ROWS), half), pl.ds(coff + j * cstep, cs)]


def _for_my_chip(fn):
    x, y, _ = _pos()
    for mine in range(N_CHIPS):
        pl.when(2 * x + y == mine)(functools.partial(fn, mine, [j for j in range(N_CHIPS) if j != mine]))


def _gather_start(placed):
    keys = [k for g in range(len(_GATHER_GROUPS)) for k in _group_keys(g)]
    nb, ng = len(keys), len(_GATHER_GROUPS)

    def body(*refs):
        bufs = dict(zip(keys, refs[nb:2 * nb]))
        ssems = refs[2 * nb:2 * nb + ng]
        rsems = refs[2 * nb + ng:2 * nb + 2 * ng]
        token = refs[2 * nb + 2 * ng]
        _, _, c = _pos()

        def send(mine, others):
            for g, (layer, pieces) in enumerate(_GATHER_GROUPS):
                for k, p in enumerate(pieces):
                    view = _half_view(bufs[(_PIECES[p][1], layer)], _PIECES[p], mine, c)
                    for j in others:
                        _remote(view, view, ssems[g].at[k * N_CHIPS + j], rsems[g].at[k * N_CHIPS + mine],
                                (j // 2, j % 2, c)).start()

        _for_my_chip(send)
        token[...] = jnp.zeros(token.shape, token.dtype)

    sems = [pltpu.SemaphoreType.DMA((N_CHIPS * len(pieces),)) for _, pieces in _GATHER_GROUPS]
    shapes = [jax.ShapeDtypeStruct(a.shape, a.dtype) for a in placed]
    res = pl.pallas_call(
        body, name="gather_start",
        in_specs=[HBM_SPEC] * nb,
        out_specs=[HBM_SPEC] * nb + [SEM_SPEC] * (2 * ng) + [pl.BlockSpec(memory_space=pltpu.VMEM)],
        out_shape=shapes + sems + sems + [jax.ShapeDtypeStruct((SUBLANES, LANES), F32)],
        input_output_aliases={t: t for t in range(nb)},
        compiler_params=_params(has_side_effects=_SPLIT_EFFECT),
    )(*[pltpu.with_memory_space_constraint(a, pltpu.HBM) for a in placed])
    return dict(zip(keys, res[:nb])), res[nb:nb + ng], res[nb + ng:nb + 2 * ng], res[nb + 2 * ng]


def _gather_wait(g, bufs, ssem, rsem, after):
    layer, pieces = _GATHER_GROUPS[g]
    keys = _group_keys(g)
    nb = len(keys)

    def body(*refs):
        ssem_ref, rsem_ref = refs[nb], refs[nb + 1]
        land = dict(zip(keys, refs[nb + 3:]))
        _, _, c = _pos()

        def wait(mine, others):
            for k, p in enumerate(pieces):
                ref = land[(_PIECES[p][1], layer)]
                for j in others:
                    cp = _remote(_half_view(ref, _PIECES[p], mine, c), _half_view(ref, _PIECES[p], j, c),
                                 ssem_ref.at[k * N_CHIPS + j], rsem_ref.at[k * N_CHIPS + j], (j // 2, j % 2, c))
                    cp.wait_send()
                    cp.wait_recv()

        _for_my_chip(wait)

    return pl.pallas_call(
        body, name="gather_wait_g%d" % g,
        in_specs=[HBM_SPEC] * nb + [SEM_SPEC, SEM_SPEC, pl.BlockSpec(memory_space=pl.ANY)],
        out_specs=[HBM_SPEC] * nb,
        out_shape=[jax.ShapeDtypeStruct(a.shape, a.dtype) for a in bufs],
        input_output_aliases={t: t for t in range(nb)},
        compiler_params=_params(has_side_effects=_SPLIT_EFFECT),
    )(*bufs, ssem, rsem, after)


def _gather_forward(g, bufs):
    layer, pieces = _GATHER_GROUPS[g]
    keys = _group_keys(g)
    nb = len(keys)

    def body(*refs):
        land = dict(zip(keys, refs[nb:2 * nb]))
        ssem, rsem = refs[2 * nb:]
        x, y, c = _pos()

        def forward(mine, others):
            cps = []
            for k, p in enumerate(pieces):
                ref = land[(_PIECES[p][1], layer)]
                for j in others:
                    view = _half_view(ref, _PIECES[p], j, c)
                    cp = _remote(view, view, ssem.at[k * N_CHIPS + j], rsem.at[k * N_CHIPS + j], (x, y, 1 - c))
                    cp.start()
                    cps.append(cp)
            for k, p in enumerate(pieces):
                ref = land[(_PIECES[p][1], layer)]
                for j in others:
                    view = _half_view(ref, _PIECES[p], j, 1 - c)
                    _remote(view, view, ssem.at[k * N_CHIPS + j], rsem.at[k * N_CHIPS + j], (x, y, 1 - c)).wait_recv()
            for cp in cps:
                cp.wait_send()

        _for_my_chip(forward)

    nsem = N_CHIPS * len(pieces)
    return pl.pallas_call(
        body, name="gather_forward_g%d" % g,
        in_specs=[HBM_SPEC] * nb, out_specs=[HBM_SPEC] * nb,
        out_shape=[jax.ShapeDtypeStruct(a.shape, a.dtype) for a in bufs],
        input_output_aliases={t: t for t in range(nb)},
        scratch_shapes=[pltpu.SemaphoreType.DMA((nsem,)), pltpu.SemaphoreType.DMA((nsem,))],
        compiler_params=_params(has_side_effects=True),
    )(*bufs)


_REDUCE_GROUPS = (
    ((5, 1), (3, 1), (4, 1), (2, 1), (1, 1), (0, 1)),
    ((5, 0), (3, 0), (4, 0)),
    ((2, 0), (1, 0)),
    ((0, 0),),
)


def _reduce_keys(group):
    keys = []
    for p, layer in group:
        if (_PIECES[p][1], layer) not in keys:
            keys.append((_PIECES[p][1], layer))
    return keys


def _half_block(piece, tm):
    _, _, rs, cs, rstep, cstep, coff = piece
    return lambda i, j, c: (j * (rstep // tm) + c * (rs // 2 // tm) + i, coff // cs + j * (cstep // cs))


def _swap_halves(g, dwb):
    group = _REDUCE_GROUPS[g]
    keys = _reduce_keys(group)
    nk = len(keys)

    def body(*refs):
        src = dict(zip(keys, refs[:nk]))
        dst = dict(zip(keys, refs[nk:2 * nk]))
        ssem, rsem = refs[2 * nk:]
        x, y, c = _pos()
        cps = []
        for k, (p, layer) in enumerate(group):
            key = (_PIECES[p][1], layer)
            for j in range(N_CHIPS):
                cp = _remote(_half_view(src[key], _PIECES[p], j, 1 - c), _half_view(dst[key], _PIECES[p], j, 1 - c),
                             ssem.at[k * N_CHIPS + j], rsem.at[k * N_CHIPS + j], (x, y, 1 - c))
                cp.start()
                cps.append(cp)
        for k, (p, layer) in enumerate(group):
            key = (_PIECES[p][1], layer)
            for j in range(N_CHIPS):
                view = _half_view(dst[key], _PIECES[p], j, c)
                _remote(view, view, ssem.at[k * N_CHIPS + j], rsem.at[k * N_CHIPS + j], (x, y, 1 - c)).wait_recv()
        for cp in cps:
            cp.wait_send()

    nsem = N_CHIPS * len(group)
    res = pl.pallas_call(
        body, name="swap_halves_g%d" % g,
        in_specs=[HBM_SPEC] * nk, out_specs=[HBM_SPEC] * nk,
        out_shape=[jax.ShapeDtypeStruct(dwb[k].shape, BF16) for k in keys],
        scratch_shapes=[pltpu.SemaphoreType.DMA((nsem,)), pltpu.SemaphoreType.DMA((nsem,))],
        compiler_params=_params(has_side_effects=True),
    )(*[dwb[k] for k in keys])
    return dict(zip(keys, res))


def _chip_partial(piece, dw, got, prev, c_arr, name):
    _, _, rs, cs, _, _, _ = piece
    half = rs // 2
    tm = _tile(half, 256)
    blk = _half_block(piece, tm)

    def body(s_ref, dw_ref, got_ref, *rest):
        rest[-1][...] = (dw_ref[...] + got_ref[...].astype(F32)).astype(BF16)

    spec = pl.BlockSpec((tm, cs), lambda j, i, s: blk(i, j, s[0]))
    in_specs = [spec, spec]
    args = [dw, got]
    aliases = {}
    if prev is not None:
        in_specs.append(pl.BlockSpec(memory_space=pl.ANY))
        args.append(prev)
        aliases = {3: 0}
    return pl.pallas_call(
        body, name=name,
        grid_spec=pltpu.PrefetchScalarGridSpec(
            num_scalar_prefetch=1, grid=(N_CHIPS, half // tm), in_specs=in_specs, out_specs=spec),
        out_shape=jax.ShapeDtypeStruct(dw.shape, BF16), input_output_aliases=aliases,
        compiler_params=_params(("parallel", "parallel")),
    )(c_arr, *args)


def _scatter_start(g, partials):
    group = _REDUCE_GROUPS[g]
    keys = _reduce_keys(group)
    nk, n = len(keys), len(group)

    def body(*refs):
        pt = dict(zip(keys, refs[nk:2 * nk]))
        land = refs[2 * nk:2 * nk + n]
        ssem, rsem, token = refs[2 * nk + n:]
        _, _, c = _pos()

        def send(mine, others):
            for k, (p, layer) in enumerate(group):
                for j in others:
                    _remote(_half_view(pt[(_PIECES[p][1], layer)], _PIECES[p], j, c), land[k].at[mine],
                            ssem.at[k * N_CHIPS + j], rsem.at[k * N_CHIPS + mine], (j // 2, j % 2, c)).start()

        _for_my_chip(send)
        token[...] = jnp.zeros(token.shape, token.dtype)

    sem = pltpu.SemaphoreType.DMA((N_CHIPS * n,))
    res = pl.pallas_call(
        body, name="scatter_start_g%d" % g,
        in_specs=[HBM_SPEC] * nk,
        out_specs=[HBM_SPEC] * (nk + n) + [SEM_SPEC, SEM_SPEC, pl.BlockSpec(memory_space=pltpu.VMEM)],
        out_shape=([jax.ShapeDtypeStruct(partials[k].shape, BF16) for k in keys]
                   + [jax.ShapeDtypeStruct((N_CHIPS, _PIECES[p][2] // 2, _PIECES[p][3]), BF16) for p, _ in group]
                   + [sem, sem, jax.ShapeDtypeStruct((SUBLANES, LANES), F32)]),
        input_output_aliases={t: t for t in range(nk)},
        compiler_params=_params(has_side_effects=_SPLIT_EFFECT),
    )(*[pltpu.with_memory_space_constraint(partials[k], pltpu.HBM) for k in keys])
    return list(res[:nk]), list(res[nk:nk + n]), res[nk + n], res[nk + n + 1], res[nk + n + 2]


def _scatter_wait(g, partials, land, ssem, rsem, after):
    group = _REDUCE_GROUPS[g]
    keys = _reduce_keys(group)
    nk, n = len(keys), len(group)

    def body(*refs):
        ssem_ref, rsem_ref = refs[nk + n], refs[nk + n + 1]
        pt = dict(zip(keys, refs[nk + n + 3:2 * nk + n + 3]))
        land_ref = refs[2 * nk + n + 3:]
        _, _, c = _pos()

        def wait(mine, others):
            for k, (p, layer) in enumerate(group):
                for j in others:
                    cp = _remote(_half_view(pt[(_PIECES[p][1], layer)], _PIECES[p], j, c), land_ref[k].at[j],
                                 ssem_ref.at[k * N_CHIPS + j], rsem_ref.at[k * N_CHIPS + j], (j // 2, j % 2, c))
                    cp.wait_send()
                    cp.wait_recv()

        _for_my_chip(wait)

    res = pl.pallas_call(
        body, name="scatter_wait_g%d" % g,
        in_specs=[HBM_SPEC] * (nk + n) + [SEM_SPEC, SEM_SPEC, pl.BlockSpec(memory_space=pl.ANY)],
        out_specs=[HBM_SPEC] * (nk + n),
        out_shape=[jax.ShapeDtypeStruct(a.shape, a.dtype) for a in list(partials) + list(land)],
        input_output_aliases={t: t for t in range(nk + n)},
        compiler_params=_params(has_side_effects=_SPLIT_EFFECT),
    )(*partials, *land, ssem, rsem, after)
    return list(res[nk:])


def _reduce_half(piece, layer, dw, got, land, prev, idx, name):
    _, _, rs, cs, _, _, _ = piece
    half = rs // 2
    tm = _tile(half, 256)
    blk = _half_block(piece, tm)

    def body(s_ref, dw_ref, got_ref, r1, r2, r3, *rest):
        acc = dw_ref[...] + got_ref[...].astype(F32)
        for r in (r1, r2, r3):
            acc = acc + r[...].astype(F32)
        rest[-1][...] = acc

    def land_map(k):
        return lambda i, s: ((s[1] + k) % N_CHIPS, i, 0)

    own = pl.BlockSpec((tm, cs), lambda i, s: blk(i, s[1], s[0]))
    in_specs = [own, own] + [pl.BlockSpec((None, tm, cs), land_map(k)) for k in (1, 2, 3)]
    args = [dw, got, land, land, land]
    aliases = {}
    if prev is not None:
        in_specs.append(pl.BlockSpec(memory_space=pl.ANY))
        args.append(prev)
        aliases = {6: 0}
    return pl.pallas_call(
        body, name=name,
        grid_spec=pltpu.PrefetchScalarGridSpec(
            num_scalar_prefetch=1, grid=(half // tm,), in_specs=in_specs,
            out_specs=pl.BlockSpec((None, tm, cs), lambda i, s: (layer, s[0] * (half // tm) + i, 0))),
        out_shape=jax.ShapeDtypeStruct((DEPTH, rs, cs), F32), input_output_aliases=aliases,
        compiler_params=_params(("parallel",)),
    )(idx, *args)


def _share_halves(reduced):
    def body(*refs):
        buf = refs[N_PIECES:2 * N_PIECES]
        ssem, rsem = refs[2 * N_PIECES:]
        x, y, c = _pos()

        def half(p, layer, cc):
            rows = _PIECES[p][2] // 2
            return buf[p].at[layer, pl.ds(pl.multiple_of(cc * rows, SUBLANES), rows), :]

        pairs = [(p, layer) for p in range(N_PIECES) for layer in range(DEPTH)]
        rem = [_remote(half(p, layer, c), half(p, layer, c), ssem.at[k], rsem.at[k], (x, y, 1 - c))
               for k, (p, layer) in enumerate(pairs)]
        for cp in rem:
            cp.start()
        for k, (p, layer) in enumerate(pairs):
            rem[k].wait_send()
            _remote(half(p, layer, 1 - c), half(p, layer, 1 - c), ssem.at[k], rsem.at[k], (x, y, 1 - c)).wait_recv()

    nsem = N_PIECES * DEPTH
    return pl.pallas_call(
        body, name="share_halves",
        in_specs=[HBM_SPEC] * N_PIECES, out_specs=[HBM_SPEC] * N_PIECES,
        out_shape=[jax.ShapeDtypeStruct((DEPTH, p[2], p[3]), F32) for p in _PIECES],
        input_output_aliases={t: t for t in range(N_PIECES)},
        scratch_shapes=[pltpu.SemaphoreType.DMA((nsem,)), pltpu.SemaphoreType.DMA((nsem,))],
        compiler_params=_params(has_side_effects=True),
    )(*reduced)


N_DEV = 8


def _place_slot(v, me_arr):
    rows = v.shape[0]
    tm = _tile(rows, 512)

    def body(s_ref, v_ref, out_ref):
        out_ref[...] = v_ref[...]

    return pl.pallas_call(
        body, name="place_small",
        grid_spec=pltpu.PrefetchScalarGridSpec(
            num_scalar_prefetch=1, grid=(rows // tm,),
            in_specs=[pl.BlockSpec((tm, LANES), lambda i, s: (i, 0))],
            out_specs=pl.BlockSpec((None, tm, LANES), lambda i, s: (s[0], i, 0))),
        out_shape=jax.ShapeDtypeStruct((N_DEV, rows, LANES), F32),
        compiler_params=_params(("parallel",)),
    )(me_arr, v)


def _peers():
    x, y, c = _pos()
    return (x, y, c), (x, y, 1 - c), [(1 - x, y, c), (x, 1 - y, c), (1 - x, 1 - y, c)]


def _slot_of(ref, dev):
    return ref.at[4 * dev[0] + 2 * dev[1] + dev[2]]


def _small_gather_start(g):
    def body(g_in, g_ref, ssem, rsem, token):
        me, sibling, others = _peers()
        for k, dev in enumerate([sibling] + others):
            _remote(_slot_of(g_ref, me), _slot_of(g_ref, me), ssem.at[k], rsem.at[k], dev).start()
        token[...] = jnp.zeros(token.shape, token.dtype)

    sem = pltpu.SemaphoreType.DMA((N_CHIPS,))
    return pl.pallas_call(
        body, name="small_gather_start",
        in_specs=[HBM_SPEC], out_specs=[HBM_SPEC, SEM_SPEC, SEM_SPEC, pl.BlockSpec(memory_space=pltpu.VMEM)],
        out_shape=[jax.ShapeDtypeStruct(g.shape, g.dtype), sem, sem, jax.ShapeDtypeStruct((SUBLANES, LANES), F32)],
        input_output_aliases={0: 0},
        compiler_params=_params(has_side_effects=_SPLIT_EFFECT),
    )(pltpu.with_memory_space_constraint(g, pltpu.HBM))


def _small_gather_wait(g, ssem, rsem, after):
    def body(g_in, ssem_ref, rsem_ref, after_ref, g_ref):
        me, sibling, others = _peers()
        for k, dev in enumerate([sibling] + others):
            cp = _remote(_slot_of(g_ref, me), _slot_of(g_ref, dev), ssem_ref.at[k], rsem_ref.at[k], dev)
            cp.wait_send()
            cp.wait_recv()

    return pl.pallas_call(
        body, name="small_gather_wait",
        in_specs=[HBM_SPEC, SEM_SPEC, SEM_SPEC, pl.BlockSpec(memory_space=pl.ANY)], out_specs=HBM_SPEC,
        out_shape=jax.ShapeDtypeStruct(g.shape, g.dtype), input_output_aliases={0: 0},
        compiler_params=_params(has_side_effects=_SPLIT_EFFECT),
    )(g, ssem, rsem, after)


def _small_gather_forward(g):
    def body(g_in, g_ref, ssem, rsem):
        me, sibling, others = _peers()
        cps = [_remote(_slot_of(g_ref, dev), _slot_of(g_ref, dev), ssem.at[k], rsem.at[k], sibling)
               for k, dev in enumerate(others)]
        for cp in cps:
            cp.start()
        for k, dev in enumerate(others):
            theirs = _slot_of(g_ref, (dev[0], dev[1], sibling[2]))
            _remote(theirs, theirs, ssem.at[k], rsem.at[k], sibling).wait_recv()
        for cp in cps:
            cp.wait_send()

    nsem = N_CHIPS - 1
    return pl.pallas_call(
        body, name="small_gather_forward",
        in_specs=[HBM_SPEC], out_specs=HBM_SPEC, out_shape=jax.ShapeDtypeStruct(g.shape, g.dtype),
        input_output_aliases={0: 0},
        scratch_shapes=[pltpu.SemaphoreType.DMA((nsem,)), pltpu.SemaphoreType.DMA((nsem,))],
        compiler_params=_params(has_side_effects=True),
    )(g)


def _sum_slots(g, name):
    n, rows, _ = g.shape
    tm = _tile(rows, 512)

    def body(g_ref, out_ref):
        acc = g_ref[0]
        for k in range(1, n):
            acc = acc + g_ref[k]
        out_ref[...] = acc

    return pl.pallas_call(
        body, name=name, grid=(rows // tm,),
        in_specs=[pl.BlockSpec((n, tm, LANES), lambda i: (0, i, 0))],
        out_specs=pl.BlockSpec((tm, LANES), lambda i: (i, 0)),
        out_shape=jax.ShapeDtypeStruct((rows, LANES), F32),
        compiler_params=_params(("parallel",)),
    )(g)


_TINY = ("ln_mix_g", "ret_log_gamma", "ssm_a_re", "ssm_a_im", "ssm_log_dt", "ssm_d", "b_glu", "ln_ffn_g", "ln_final_g")
_MID = ("ssm_b_re", "ssm_b_im", "ssm_c_re", "ssm_c_im")
_SMALL = _TINY + _MID
_FLAT_ALIGN = LANES * LANES


def _flat_rows(like, names):
    return sum((math.prod(like[n].shape) + (-math.prod(like[n].shape)) % _FLAT_ALIGN) // LANES for n in names)


def _flatten(d, names):
    parts = []
    for n in names:
        f = d[n].reshape(-1)
        parts.append(jnp.pad(f, (0, (-f.shape[0]) % _FLAT_ALIGN)))
    return jnp.concatenate(parts).reshape(-1, LANES)


def _unflatten(flat, like, names):
    out, row = {}, 0
    for n in names:
        size = math.prod(like[n].shape)
        rows = (size + (-size) % _FLAT_ALIGN) // LANES
        part = lax.optimization_barrier(flat[row:row + rows])
        out[n] = part.reshape(-1)[:size].reshape(like[n].shape)
        row += rows
    return out


_BIG = ("w_in", "w_glu", "w_out", "w_ffn_gate", "w_ffn_up", "w_ffn_down")
_WEIGHTS = ("ln_mix_g", "w_in", "ret_log_gamma", "ssm_a_re", "ssm_a_im", "ssm_log_dt", "ssm_b_re", "ssm_b_im",
            "ssm_c_re", "ssm_c_im", "ssm_d", "w_glu", "b_glu", "w_out", "ln_ffn_g", "w_ffn_gate", "w_ffn_up",
            "w_ffn_down", "ln_final_g")


def _rope_tables(seq):
    half = QK_DIM // 2
    inv = 1.0 / (ROPE_BASE ** (jnp.arange(half, dtype=F32) / half))
    ang = jnp.arange(seq, dtype=F32)[:, None] * inv[None, :]
    return jnp.cos(ang), jnp.sin(ang)


def _step(w, m, v, x, target):
    seq = x.shape[0]
    seg_len = float(seq // SEGMENTS)
    c_idx = lax.axis_index("c").astype(jnp.int32)
    chip_idx = (2 * lax.axis_index("x") + lax.axis_index("y")).astype(jnp.int32)
    c_arr = jnp.stack([c_idx])
    idx_arr = jnp.stack([c_idx, chip_idx])

    chip_arr = jnp.stack([chip_idx])
    placed = {}
    for piece in _PIECES:
        for layer in range(DEPTH):
            key = (piece[1], layer)
            placed[key] = _cast_place(piece, w[piece[0]], layer, placed.get(key), chip_arr,
                                      "cast_%s_l%d" % (piece[0], layer))
    keys = [k for g in range(len(_GATHER_GROUPS)) for k in _group_keys(g)]
    flying, ssems, rsems, token = _gather_start([placed[k] for k in keys])
    wf = {b[0]: [None] * DEPTH for b in _BUFFERS}

    def arrive(g, after):
        ks = _group_keys(g)
        landed = _gather_wait(g, [flying[k] for k in ks], ssems[g], rsems[g], after)
        for k, a in zip(ks, _gather_forward(g, landed)):
            wf[k[0]][k[1]] = a

    cos, sin = _rope_tables(seq)

    saved = []
    xc = x + token[0, 0]
    for i in range(DEPTH):
        t = "_l%d" % i
        s = {"x_in": xc}
        s["h"] = _rms_fwd(xc, w["ln_mix_g"][i:i + 1], "rms_mix" + t)
        arrive(2 * i, s["h"])
        s["proj"] = _matmul(s["h"], wf["in"][i], "nn", [F32], name="mm_in" + t)[0]
        s["qr"], s["kr"] = _rot_fwd(s["proj"], cos, sin, "rot" + t)
        s["lg"] = jnp.broadcast_to(w["ret_log_gamma"][i].T[:, :, None], (HEADS, 2, LANES))
        s["y"] = _ret_fwd(s["qr"], s["kr"], s["proj"], s["lg"], "ret" + t)
        s5_raw = (w["ssm_a_re"][i], w["ssm_a_im"][i], w["ssm_log_dt"][i], w["ssm_b_re"][i], w["ssm_b_im"][i])
        disc, s["disc_vjp"] = jax.vjp(functools.partial(_s5_discretize, seg_len=seg_len), *s5_raw)
        bblk, cblk, lam = _s5_pack(*disc, w["ssm_c_re"][i], w["ssm_c_im"][i])
        s["s5"] = (bblk.astype(BF16), cblk.astype(BF16), lam)
        s["s5y"] = _s5_fwd(s["proj"], *s["s5"], "s5" + t)
        s["ret"], s["ysg"], s["ysgb"] = _post1_fwd(s["y"], s["proj"], s["s5y"], w["ssm_d"][i:i + 1], "post" + t)
        arrive(2 * i + 1, s["ysgb"])
        s["z"] = _matmul(s["ysgb"], wf["glu"][i], "nn", [F32], name="mm_glu" + t)[0]
        s["merged"] = _merge_fwd(s["z"], s["ysg"], s["proj"], s["ret"], w["b_glu"][i:i + 1], "merge" + t)
        s["x1"] = _matmul(s["merged"], wf["out"][i], "nn", [F32], add=xc, name="mm_out" + t)[0]
        s["h2"] = _rms_fwd(s["x1"], w["ln_ffn_g"][i:i + 1], "rms_ffn" + t)
        s["ab"] = _matmul(s["h2"], wf["gu"][i], "nn", [F32], name="mm_gu" + t)[0]
        s["f"] = _glu_fwd(s["ab"], "glu" + t)
        xc = _matmul(s["f"], wf["down"][i], "nn", [F32], add=s["x1"], name="mm_down" + t)[0]
        saved.append(s)

    dx, dxb, loss_row, dg_final = _loss_stage(xc, target, w["ln_final_g"][None, :], "loss")
    loss = lax.psum(loss_row[0, 0], ("x", "y", "c"))

    g_small = {"ln_final_g": dg_final[0]}
    per_layer = {n: [None] * DEPTH for n in _SMALL if n != "ln_final_g"}
    dws, got, flights = {}, {}, []

    def dw_mm(a, b, buf, i, name):
        dws[(buf, i)] = _matmul(a, b, "tn", [F32, BF16], name=name)

    def launch(g):
        group = _REDUCE_GROUPS[g]
        keys = _reduce_keys(group)
        got.update(_swap_halves(g, {k: dws[k][1] for k in keys}))
        partials = {}
        for p, layer in group:
            key = (_PIECES[p][1], layer)
            partials[key] = _chip_partial(_PIECES[p], dws[key][0], got[key], partials.get(key), c_arr,
                                          "chip_partial_%s_l%d" % (_PIECES[p][0], layer))
        pt, land, ssem, rsem, tok = _scatter_start(g, partials)
        flights.append((g, pt, land, ssem, rsem))
        return tok[0:1, 0:1]

    for i in reversed(range(DEPTH)):
        t = "_l%d" % i
        s = saved[i]
        g_ffn, g_mix, d_skip = w["ln_ffn_g"][i:i + 1], w["ln_mix_g"][i:i + 1], w["ssm_d"][i:i + 1]
        dw_mm(s["f"], dxb, "down", i, "dw_down" + t)
        df = _matmul(dxb, wf["down"][i], "nt", [F32], name="dx_down" + t)[0]
        dab = _glu_bwd(s["ab"], df, "glu_bwd" + t)
        dw_mm(s["h2"], dab, "gu", i, "dw_gu" + t)
        if i == 0:
            g_ffn = g_ffn + launch(1)
        dh2 = _matmul(dab, wf["gu"][i], "nt", [F32], name="dx_gu" + t)[0]
        dx1, dx1b, dg = _rms_bwd(s["x1"], dh2, dx, g_ffn, "rms_ffn_bwd" + t)
        per_layer["ln_ffn_g"][i] = dg[0]

        dw_mm(s["merged"], dx1b, "out", i, "dw_out" + t)
        dmerged = _matmul(dx1b, wf["out"][i], "nt", [F32], name="dx_out" + t)[0]
        dz, dys_part, dgs, db = _merge_bwd(s["z"], s["ysg"], s["proj"], s["ret"], dmerged, w["b_glu"][i:i + 1],
                                           "merge_bwd" + t)
        per_layer["b_glu"][i] = db[0]
        dw_mm(s["ysgb"], dz, "glu", i, "dw_glu" + t)
        if i == 0:
            d_skip = d_skip + launch(2)
        dys = _matmul(dz, wf["glu"][i], "nt", [F32], add=dys_part, name="dx_glu" + t)[0]
        dy, dgg, dgr, ds5, du_part, dd = _post1_bwd(s["y"], s["proj"], s["s5y"], dmerged, dys,
                                                    d_skip, "post_bwd" + t)
        per_layer["ssm_d"][i] = dd[0]
        du, dbblk, dcblk, dlam = _s5_bwd(s["proj"], ds5, du_part, *s["s5"], "s5_bwd" + t)
        dlr, dli, dbr, dbi, dcr, dci = _s5_unpack(dbblk, dcblk, dlam)
        zeros = jnp.zeros_like(dlr)
        da_re, da_im, dlog_dt, db_re, db_im = s["disc_vjp"]((dlr, dli, zeros, zeros, dbr, dbi))
        for n, val in (("ssm_a_re", da_re), ("ssm_a_im", da_im), ("ssm_log_dt", dlog_dt), ("ssm_b_re", db_re),
                       ("ssm_b_im", db_im), ("ssm_c_re", dcr), ("ssm_c_im", dci)):
            per_layer[n][i] = val
        dqr, dkr, dv, dlg = _ret_bwd(s["qr"], s["kr"], s["proj"], dy, s["lg"], "ret_bwd" + t)
        per_layer["ret_log_gamma"][i] = dlg[:, :, 0].T
        dqkv = _rot_bwd(dqr, dkr, dv, cos, sin, "rot_bwd" + t)
        dproj = jnp.concatenate([dqkv, dgg, du, dgr, dgs], axis=1)
        dw_mm(s["h"], dproj, "in", i, "dw_in" + t)
        if i == 0:
            g_mix = g_mix + launch(3)
        dh = _matmul(dproj, wf["in"][i], "nt", [F32], name="dx_in" + t)[0]
        dx, dxb, dg = _rms_bwd(s["x_in"], dh, dx1, g_mix, "rms_mix_bwd" + t)
        per_layer["ln_mix_g"][i] = dg[0]
        if i == DEPTH - 1:
            dxb = dxb + launch(0).astype(BF16)

    for n in per_layer:
        g_small[n] = jnp.stack(per_layer[n])
    me_arr = jnp.stack([2 * chip_idx + c_idx])
    flying_small, small_ssem, small_rsem, small_token = _small_gather_start(
        _place_slot(_flatten(g_small, _SMALL), me_arr))

    reduced = [None] * N_PIECES
    for g, pt, land, ssem, rsem in flights:
        landed = _scatter_wait(g, pt, land, ssem, rsem, small_token)
        for (p, layer), buf in zip(_REDUCE_GROUPS[g], landed):
            key = (_PIECES[p][1], layer)
            reduced[p] = _reduce_half(_PIECES[p], layer, dws[key][0], got[key], buf, reduced[p], idx_arr,
                                      "reduce_%s_l%d" % (_PIECES[p][0], layer))
    g_big = dict(zip([p[0] for p in _PIECES], _share_halves(reduced)))

    grads, delta, new_m, new_v = {}, {}, {}, {}
    for n in _BIG:
        d, r, cc = w[n].shape
        two_d = lambda a: a.reshape(d * r, cc)
        dl, mn, vn = _adamw(two_d(w[n]), two_d(g_big[n]), two_d(m[n]), two_d(v[n]), "adamw_" + n)
        grads[n], delta[n], new_m[n], new_v[n] = g_big[n], dl.reshape(d, r, cc), mn.reshape(d, r, cc), vn.reshape(d, r, cc)

    gathered = _small_gather_forward(_small_gather_wait(flying_small, small_ssem, small_rsem, delta[_BIG[-1]]))
    g_flat = _sum_slots(gathered, "sum_small")
    grads.update(_unflatten(g_flat, w, _SMALL))
    tiny_rows = _flat_rows(w, _TINY)
    dl, mn, vn = _adamw(_flatten(w, _TINY), g_flat[:tiny_rows], _flatten(m, _TINY), _flatten(v, _TINY), "adamw_tiny")
    for dst, flat in ((delta, dl), (new_m, mn), (new_v, vn)):
        dst.update(_unflatten(flat, w, _TINY))
    for n in _MID:
        delta[n], new_m[n], new_v[n] = _adamw_nd(w[n], grads[n], m[n], v[n], "adamw_" + n)
    return loss, dx, grads, delta, new_m, new_v


def kernel(x, ln_mix_g, w_in, ret_log_gamma, ssm_a_re, ssm_a_im, ssm_log_dt, ssm_b_re, ssm_b_im, ssm_c_re, ssm_c_im, ssm_d, w_glu, b_glu, w_out, ln_ffn_g, w_ffn_gate, w_ffn_up, w_ffn_down, ln_final_g, loss_target, m_ln_mix_g, m_w_in, m_ret_log_gamma, m_ssm_a_re, m_ssm_a_im, m_ssm_log_dt, m_ssm_b_re, m_ssm_b_im, m_ssm_c_re, m_ssm_c_im, m_ssm_d, m_w_glu, m_b_glu, m_w_out, m_ln_ffn_g, m_w_ffn_gate, m_w_ffn_up, m_w_ffn_down, m_ln_final_g, v_ln_mix_g, v_w_in, v_ret_log_gamma, v_ssm_a_re, v_ssm_a_im, v_ssm_log_dt, v_ssm_b_re, v_ssm_b_im, v_ssm_c_re, v_ssm_c_im, v_ssm_d, v_w_glu, v_b_glu, v_w_out, v_ln_ffn_g, v_w_ffn_gate, v_w_ffn_up, v_w_ffn_down, v_ln_final_g):
    given = dict(locals())
    w = {n: given[n] for n in _WEIGHTS}
    m = {n: given["m_" + n] for n in _WEIGHTS}
    v = {n: given["v_" + n] for n in _WEIGHTS}
    loss, dx, grads, delta, new_m, new_v = _step(w, m, v, x[0], loss_target[0])
    return (loss, dx[None], *[grads[n] for n in _WEIGHTS], *[delta[n] for n in _WEIGHTS],
            *[new_m[n] for n in _WEIGHTS], *[new_v[n] for n in _WEIGHTS])
```

```python
import functools
import math

import jax
import jax.numpy as jnp
from jax import lax
from jax.experimental import pallas as pl
from jax.experimental.pallas import tpu as pltpu

F32 = jnp.float32
BF16 = jnp.bfloat16

D_MODEL = 2048
DEPTH = 2
HEADS = 4
QK_DIM = 256
V_DIM = 512
QK_WIDTH = HEADS * QK_DIM
ROPE_BASE = 10000.0
GROUP = 16
N_GROUPS = D_MODEL // GROUP
N_STATE = 64
D_FF = 5632
IN_WIDTH = 2 * QK_WIDTH + 5 * D_MODEL
EPS = 1e-6
N_CHIPS = 4

ADAM_LR = 0.001
ADAM_B1 = 0.9
ADAM_B2 = 0.999
ADAM_EPS = 1e-08
ADAM_WD = 0.01
ADAM_STEP = 10

LANES = 128
SUBLANES = 8
VMEM_LIMIT = 56 * 1024 * 1024
SEGMENTS = SUBLANES
GROUPS_PER_TILE = LANES // GROUP
STATE_COLS = GROUPS_PER_TILE * N_STATE
N_TILES = D_MODEL // LANES
SCAN_UNROLL = 4

MESH = pl.DeviceIdType.MESH
HBM_SPEC = pl.BlockSpec(memory_space=pltpu.HBM)


def _params(sem=None, **kw):
    return pltpu.CompilerParams(dimension_semantics=sem, vmem_limit_bytes=VMEM_LIMIT, **kw)


def _tile(n, cap=1024):
    for t in (2048, 1024, 512, 256, 128, 64):
        if t <= cap and n % t == 0:
            return t
    raise ValueError(n)


def _rows_call(fn, rows, pars, row_outs, par_outs, *, tm, name):
    m = rows[0][0].shape[0]
    nr, npar, nro, npo = len(rows), len(pars), len(row_outs), len(par_outs)

    def body(*refs):
        rin = refs[:nr]
        pin = refs[nr:nr + npar]
        rout = refs[nr + npar:nr + npar + nro]
        pout = refs[nr + npar + nro:]
        res = fn(*[r[...] for r in rin], *[p[...] for p in pin])
        if not isinstance(res, (tuple, list)):
            res = (res,)
        for r, v in zip(rout, res[:nro]):
            r[...] = v.astype(r.dtype)
        if npo:
            @pl.when(pl.program_id(0) == 0)
            def _():
                for p in pout:
                    p[...] = jnp.zeros(p.shape, p.dtype)
            for p, v in zip(pout, res[nro:]):
                p[...] += v

    in_specs = [pl.BlockSpec((tm, w), functools.partial(lambda cb, i: (i, cb), cb)) for (_, w, cb) in rows]
    in_specs += [pl.BlockSpec(p.shape, lambda i: (0, 0)) for p in pars]
    out_specs = [pl.BlockSpec((tm, w), lambda i: (i, 0)) for (w, _) in row_outs]
    out_specs += [pl.BlockSpec(s, lambda i: (0, 0)) for s in par_outs]
    out_shape = [jax.ShapeDtypeStruct((m, w), dt) for (w, dt) in row_outs]
    out_shape += [jax.ShapeDtypeStruct(s, F32) for s in par_outs]
    res = pl.pallas_call(
        body, name=name, grid=(m // tm,), in_specs=in_specs, out_specs=out_specs, out_shape=out_shape,
        compiler_params=_params(("arbitrary",) if npo else ("parallel",)),
    )(*[a for (a, _, _) in rows], *pars)
    return res


def _f32(*vals):
    return [v.astype(F32) for v in vals]


def _f_rms(x, g):
    r = lax.rsqrt(jnp.mean(x * x, axis=-1, keepdims=True) + EPS)
    return x * r * g


def _rms_fwd(x, g, name):
    return _rows_call(lambda xv, gv: _f_rms(xv, gv), [(x, D_MODEL, 0)], [g], [(D_MODEL, BF16)], [],
                      tm=256, name=name)[0]


def _rms_bwd(x, dh, dres, g, name):
    def fn(xv, dhv, drv, gv):
        _, vjp = jax.vjp(_f_rms, xv, gv)
        dx, dg = vjp(dhv)
        dx = dx + drv
        return dx, dx, dg
    return _rows_call(fn, [(x, D_MODEL, 0), (dh, D_MODEL, 0), (dres, D_MODEL, 0)], [g],
                      [(D_MODEL, F32), (D_MODEL, BF16)], [(1, D_MODEL)], tm=256, name=name)


def _rot_heads(xv, cos, sin, scale):
    half = QK_DIM // 2
    outs = []
    for h in range(HEADS):
        x1 = xv[:, h * QK_DIM:h * QK_DIM + half]
        x2 = xv[:, h * QK_DIM + half:(h + 1) * QK_DIM]
        outs += [(x1 * cos - x2 * sin) * scale, (x1 * sin + x2 * cos) * scale]
    return jnp.concatenate(outs, axis=1)


def _rot_fwd(proj, cos, sin, name):
    def fn(q, k, cv, sv):
        return _rot_heads(q, cv, sv, 1.0), _rot_heads(k, cv, sv, QK_DIM ** -0.5)
    return _rows_call(fn, [(proj, QK_WIDTH, 0), (proj, QK_WIDTH, 1), (cos, LANES, 0), (sin, LANES, 0)], [],
                      [(QK_WIDTH, BF16), (QK_WIDTH, BF16)], [], tm=256, name=name)


def _rot_bwd(dqr, dkr, dv, cos, sin, name):
    def fn(dq, dk, dvv, cv, sv):
        return jnp.concatenate([_rot_heads(dq, cv, -sv, 1.0), _rot_heads(dk, cv, -sv, QK_DIM ** -0.5), dvv], axis=1)
    return _rows_call(fn, [(dqr, QK_WIDTH, 0), (dkr, QK_WIDTH, 0), (dv, D_MODEL, 0), (cos, LANES, 0), (sin, LANES, 0)],
                      [], [(2 * QK_WIDTH + D_MODEL, BF16)], [], tm=256, name=name)[0]


def _f_post1(y0, y1, y2, y3, g, gr, s5, u, dsk):
    yn = [yh * lax.rsqrt(jnp.mean(yh * yh, axis=-1, keepdims=True) + EPS) for yh in (y0, y1, y2, y3)]
    ret = jax.nn.sigmoid(gr) * (jax.nn.silu(g) * jnp.concatenate(yn, axis=1))
    ysg = jax.nn.gelu(s5 + dsk * u)
    return ret, ysg


def _post1_rows(y, proj, s5y):
    rows = [(y, V_DIM, h) for h in range(HEADS)]
    rows += [(proj, D_MODEL, 2), (proj, D_MODEL, 4), (s5y, D_MODEL, 0), (proj, D_MODEL, 3)]
    return rows


def _post1_fwd(y, proj, s5y, dsk, name):
    def fn(*vals):
        ret, ysg = _f_post1(*vals)
        return ret, ysg, ysg
    return _rows_call(fn, _post1_rows(y, proj, s5y), [dsk],
                      [(D_MODEL, F32), (D_MODEL, F32), (D_MODEL, BF16)], [], tm=128, name=name)


def _post1_bwd(y, proj, s5y, dret, dys, dsk, name):
    def fn(*vals):
        prim = vals[:8] + (vals[10],)
        _, vjp = jax.vjp(_f_post1, *prim)
        gy0, gy1, gy2, gy3, gg, ggr, gs5, gu, gd = vjp((vals[8], vals[9]))
        return jnp.concatenate([gy0, gy1, gy2, gy3], axis=1), gg, ggr, gs5, gu, gd
    rows = _post1_rows(y, proj, s5y) + [(dret, D_MODEL, 0), (dys, D_MODEL, 0)]
    return _rows_call(fn, rows, [dsk],
                      [(D_MODEL, BF16), (D_MODEL, BF16), (D_MODEL, BF16), (D_MODEL, F32), (D_MODEL, F32)],
                      [(1, D_MODEL)], tm=128, name=name)


def _f_merge(z, ysg, gs, ret, b):
    return ret + jax.nn.sigmoid(gs) * (ysg * jax.nn.sigmoid(z + b))


def _merge_fwd(z, ysg, proj, ret, b, name):
    return _rows_call(_f_merge, [(z, D_MODEL, 0), (ysg, D_MODEL, 0), (proj, D_MODEL, 5), (ret, D_MODEL, 0)], [b],
                      [(D_MODEL, BF16)], [], tm=128, name=name)[0]


def _merge_bwd(z, ysg, proj, ret, dm, b, name):
    def fn(zv, yv, gv, rv, dmv, bv):
        _, vjp = jax.vjp(_f_merge, zv, yv, gv, rv, bv)
        gz, gy, gg, _, gb = vjp(dmv)
        return gz, gy, gg, gb
    rows = [(z, D_MODEL, 0), (ysg, D_MODEL, 0), (proj, D_MODEL, 5), (ret, D_MODEL, 0), (dm, D_MODEL, 0)]
    return _rows_call(fn, rows, [b], [(D_MODEL, BF16), (D_MODEL, F32), (D_MODEL, BF16)], [(1, D_MODEL)],
                      tm=128, name=name)


def _f_glu(a, b):
    return jax.nn.silu(a) * b


def _glu_fwd(ab, name):
    return _rows_call(_f_glu, [(ab, D_FF, 0), (ab, D_FF, 1)], [], [(D_FF, BF16)], [], tm=128, name=name)[0]


def _glu_bwd(ab, df, name):
    def fn(a, b, d):
        _, vjp = jax.vjp(_f_glu, a, b)
        ga, gb = vjp(d)
        return jnp.concatenate([ga, gb], axis=1)
    return _rows_call(fn, [(ab, D_FF, 0), (ab, D_FF, 1), (df, D_FF, 0)], [], [(2 * D_FF, BF16)], [],
                      tm=128, name=name)[0]


def _loss_stage(x, tgt, g, name):
    def fn(xv, tv, gv):
        def lf(xx, gg):
            err = _f_rms(xx, gg) - tv
            row = jnp.mean(err * err, axis=-1, keepdims=True)
            return 0.5 * jnp.sum(row, axis=0, keepdims=True)
        l, vjp = jax.vjp(lf, xv, gv)
        dx, dg = vjp(jnp.ones((1, 1), F32))
        return dx, dx, jnp.broadcast_to(l, (1, LANES)), dg
    return _rows_call(fn, [(x, D_MODEL, 0), (tgt, D_MODEL, 0)], [g], [(D_MODEL, F32), (D_MODEL, BF16)],
                      [(1, LANES), (1, D_MODEL)], tm=256, name=name)


def _adam_math(wv, gv, mv, vv):
    mn = ADAM_B1 * mv + (1.0 - ADAM_B1) * gv
    vn = ADAM_B2 * vv + (1.0 - ADAM_B2) * (gv * gv)
    m_hat = mn / (1.0 - ADAM_B1 ** ADAM_STEP)
    v_hat = vn / (1.0 - ADAM_B2 ** ADAM_STEP)
    delta = -ADAM_LR * (m_hat / (jnp.sqrt(v_hat) + ADAM_EPS) + ADAM_WD * wv)
    return delta, mn, vn


def _adamw(w, g, m, v, name):
    rows, cols = w.shape
    tm = _tile(rows, 128 if cols > D_FF // N_CHIPS else (256 if cols > LANES else 512))
    return _rows_call(_adam_math, [(w, cols, 0), (g, cols, 0), (m, cols, 0), (v, cols, 0)], [],
                      [(cols, F32)] * 3, [], tm=tm, name=name)


def _adamw_nd(w, g, m, v, name):
    shape = w.shape
    lead = math.prod(shape[:-2])
    blk = (lead // 8,) + shape[-2:]
    three_d = lambda a: a.reshape((lead,) + shape[-2:])

    def body(w_ref, g_ref, m_ref, v_ref, d_ref, mn_ref, vn_ref):
        d_ref[...], mn_ref[...], vn_ref[...] = _adam_math(w_ref[...], g_ref[...], m_ref[...], v_ref[...])

    spec = pl.BlockSpec(blk, lambda i: (i, 0, 0))
    res = pl.pallas_call(
        body, name=name, grid=(8,), in_specs=[spec] * 4, out_specs=[spec] * 3,
        out_shape=[jax.ShapeDtypeStruct((lead,) + shape[-2:], F32)] * 3,
        compiler_params=_params(("parallel",)),
    )(three_d(w), three_d(g), three_d(m), three_d(v))
    return [r.reshape(shape) for r in res]


MATMUL_VMEM_BUDGET = 44 * 1024 * 1024


def _matmul_tiles(m, n, k, out_bytes, has_add):
    if k > 2048:
        return _tile(m, 1024), _tile(n, 1024), _tile(k, 1024)
    tm, tn, tk = _tile(m, 2048), _tile(n, 1024), k

    def footprint():
        acc = 4 * tm * tn if k // tk > 1 else 0
        return 2 * 2 * (tm * tk + tk * tn) + 2 * (out_bytes + 4 * has_add) * tm * tn + acc

    while footprint() > MATMUL_VMEM_BUDGET:
        if tn > 512 and n % (tn // 2) == 0:
            tn //= 2
        elif tk > 512 and k % (tk // 2) == 0:
            tk //= 2
        else:
            tm //= 2
    return tm, tn, tk


def _matmul(a, b, mode, out_dtypes, *, name, add=None):
    if mode == "nn":
        (m, k), (_, n) = a.shape, b.shape
    elif mode == "nt":
        (m, k), (n, _) = a.shape, b.shape
    else:
        (k, m), (_, n) = a.shape, b.shape
    n_out = len(out_dtypes)
    has_add = add is not None
    tm, tn, tk = _matmul_tiles(m, n, k, sum(jnp.dtype(dt).itemsize for dt in out_dtypes), has_add)
    nk = k // tk
    if mode == "nn":
        a_spec = pl.BlockSpec((tm, tk), lambda i, j, kk: (i, kk))
        b_spec = pl.BlockSpec((tk, tn), lambda i, j, kk: (kk, j))
        dims = (((1,), (0,)), ((), ()))
    elif mode == "nt":
        a_spec = pl.BlockSpec((tm, tk), lambda i, j, kk: (i, kk))
        b_spec = pl.BlockSpec((tn, tk), lambda i, j, kk: (j, kk))
        dims = (((1,), (1,)), ((), ()))
    else:
        a_spec = pl.BlockSpec((tk, tm), lambda i, j, kk: (kk, i))
        b_spec = pl.BlockSpec((tk, tn), lambda i, j, kk: (kk, j))
        dims = (((0,), (0,)), ((), ()))

    def body(*refs):
        a_ref, b_ref = refs[0], refs[1]
        add_ref = refs[2] if has_add else None
        outs = refs[2 + has_add:2 + has_add + n_out]

        def finish(r):
            if has_add:
                r = r + add_ref[...]
            for o in outs:
                o[...] = r.astype(o.dtype)

        if nk == 1:
            finish(lax.dot_general(a_ref[...], b_ref[...], dims, preferred_element_type=F32))
            return
        acc = refs[-1]
        kk = pl.program_id(2)

        @pl.when(kk == 0)
        def _():
            acc[...] = jnp.zeros(acc.shape, F32)

        acc[...] += lax.dot_general(a_ref[...], b_ref[...], dims, preferred_element_type=F32)

        @pl.when(kk == nk - 1)
        def _():
            finish(acc[...])

    in_specs = [a_spec, b_spec]
    args = [a, b]
    if has_add:
        in_specs.append(pl.BlockSpec((tm, tn), lambda i, j, kk: (i, j)))
        args.append(add)
    return pl.pallas_call(
        body, name=name, grid=(m // tm, n // tn, nk), in_specs=in_specs,
        out_specs=[pl.BlockSpec((tm, tn), lambda i, j, kk: (i, j))] * n_out,
        out_shape=[jax.ShapeDtypeStruct((m, n), dt) for dt in out_dtypes],
        scratch_shapes=[pltpu.VMEM((tm, tn), F32)] if nk > 1 else [],
        compiler_params=_params(("parallel", "parallel", "arbitrary")),
    )(*args)


RET_TQ = 512


def _decay(lg_ref, i, tq, seq):
    n_idx = i * tq + lax.broadcasted_iota(jnp.int32, (tq, seq), 0)
    m_idx = lax.broadcasted_iota(jnp.int32, (tq, seq), 1)
    diff = (n_idx - m_idx).astype(F32)
    lgf = lg_ref[0, 0:1, 0:1]
    lgb = lg_ref[0, 1:2, 0:1]
    causal = diff >= 0
    return jnp.exp(jnp.where(causal, lgf * diff, -lgb * diff)), diff, causal


_NT = (((1,), (1,)), ((), ()))
_TN = (((0,), (0,)), ((), ()))


def _ret_fwd(qr, kr, proj, lg, name):
    seq = qr.shape[0]
    tq = RET_TQ
    v_blk0 = (2 * QK_WIDTH) // V_DIM

    def body(q_ref, k_ref, v_ref, lg_ref, y_ref):
        i = pl.program_id(1)
        s = lax.dot_general(q_ref[...], k_ref[...], _NT, preferred_element_type=F32)
        dm, _, _ = _decay(lg_ref, i, tq, seq)
        p = (s * dm).astype(BF16)
        y_ref[...] = jnp.dot(p, v_ref[...].astype(BF16), preferred_element_type=F32)

    return pl.pallas_call(
        body, name=name, grid=(HEADS, seq // tq),
        in_specs=[pl.BlockSpec((tq, QK_DIM), lambda h, i: (i, h)),
                  pl.BlockSpec((seq, QK_DIM), lambda h, i: (0, h)),
                  pl.BlockSpec((seq, V_DIM), lambda h, i: (0, v_blk0 + h)),
                  pl.BlockSpec((1, 2, LANES), lambda h, i: (h, 0, 0))],
        out_specs=pl.BlockSpec((tq, V_DIM), lambda h, i: (i, h)),
        out_shape=jax.ShapeDtypeStruct((seq, HEADS * V_DIM), F32),
        compiler_params=_params(("parallel", "parallel")),
    )(qr, kr, proj, lg)


def _ret_bwd(qr, kr, proj, dy, lg, name):
    seq = qr.shape[0]
    tq = RET_TQ
    v_blk0 = (2 * QK_WIDTH) // V_DIM

    def body(q_ref, k_ref, v_ref, dy_ref, lg_ref, dq_ref, dk_ref, dv_ref, dlg_ref):
        i = pl.program_id(1)

        @pl.when(i == 0)
        def _():
            dk_ref[...] = jnp.zeros(dk_ref.shape, F32)
            dv_ref[...] = jnp.zeros(dv_ref.shape, F32)
            dlg_ref[...] = jnp.zeros(dlg_ref.shape, F32)

        q = q_ref[...]
        k = k_ref[...]
        vb = v_ref[...].astype(BF16)
        dyb = dy_ref[...]
        s = lax.dot_general(q, k, _NT, preferred_element_type=F32)
        dm, diff, causal = _decay(lg_ref, i, tq, seq)
        p = s * dm
        dp = lax.dot_general(dyb, vb, _NT, preferred_element_type=F32)
        dv_ref[...] += lax.dot_general(p.astype(BF16), dyb, _TN, preferred_element_type=F32)
        ds = (dp * dm).astype(BF16)
        dq_ref[...] = jnp.dot(ds, k, preferred_element_type=F32)
        dk_ref[...] += lax.dot_general(ds, q, _TN, preferred_element_type=F32)
        gd = dp * p * diff
        dlf = jnp.sum(jnp.sum(jnp.where(causal, gd, 0.0), axis=1, keepdims=True), axis=0, keepdims=True)
        dlb = jnp.sum(jnp.sum(jnp.where(causal, 0.0, -gd), axis=1, keepdims=True), axis=0, keepdims=True)
        row = lax.broadcasted_iota(jnp.int32, (2, LANES), 0)
        dlg_ref[0] += jnp.where(row == 0, dlf, dlb)

    return pl.pallas_call(
        body, name=name, grid=(HEADS, seq // tq),
        in_specs=[pl.BlockSpec((tq, QK_DIM), lambda h, i: (i, h)),
                  pl.BlockSpec((seq, QK_DIM), lambda h, i: (0, h)),
                  pl.BlockSpec((seq, V_DIM), lambda h, i: (0, v_blk0 + h)),
                  pl.BlockSpec((tq, V_DIM), lambda h, i: (i, h)),
                  pl.BlockSpec((1, 2, LANES), lambda h, i: (h, 0, 0))],
        out_specs=[pl.BlockSpec((tq, QK_DIM), lambda h, i: (i, h)),
                   pl.BlockSpec((seq, QK_DIM), lambda h, i: (0, h)),
                   pl.BlockSpec((seq, V_DIM), lambda h, i: (0, h)),
                   pl.BlockSpec((1, 2, LANES), lambda h, i: (h, 0, 0))],
        out_shape=[jax.ShapeDtypeStruct((seq, QK_WIDTH), F32), jax.ShapeDtypeStruct((seq, QK_WIDTH), F32),
                   jax.ShapeDtypeStruct((seq, HEADS * V_DIM), F32), jax.ShapeDtypeStruct((HEADS, 2, LANES), F32)],
        compiler_params=_params(("parallel", "arbitrary")),
    )(qr, kr, proj, dy, lg)


def _shift_rows(v, reverse):
    row = lax.broadcasted_iota(jnp.int32, v.shape, 0)
    if reverse:
        return jnp.where(row == SEGMENTS - 1, 0.0, pltpu.roll(v, SEGMENTS - 1, 0))
    return jnp.where(row == 0, 0.0, pltpu.roll(v, 1, 0))


def _slab(t):
    if isinstance(t, int):
        return pl.ds(t * SEGMENTS, SEGMENTS)
    return pl.ds(pl.multiple_of(t * SEGMENTS, SEGMENTS), SEGMENTS)


def _unrolled_loop(body, lo, hi, init):
    main = (hi - lo) // SCAN_UNROLL

    def unrolled(g, carry):
        for k in range(SCAN_UNROLL):
            carry = body(lo + g * SCAN_UNROLL + k, carry)
        return carry

    carry = lax.fori_loop(0, main, unrolled, init)
    for t in range(lo + main * SCAN_UNROLL, hi):
        carry = body(t, carry)
    return carry


def _scan(xr_ref, xi_ref, lam, reverse, conj):
    steps = xr_ref.shape[0] // SEGMENTS
    cols = xr_ref.shape[1]
    lr = jnp.broadcast_to(lam[0], (SEGMENTS, cols))
    li = jnp.broadcast_to(lam[1], (SEGMENTS, cols))
    lrt = jnp.broadcast_to(lam[2], (SEGMENTS, cols))
    lit = jnp.broadcast_to(lam[3], (SEGMENTS, cols))
    if conj:
        li, lit = -li, -lit
    zero = jnp.zeros((SEGMENTS, cols), F32)

    def rows_of(t):
        return _slab(steps - 1 - t if reverse else t)

    def advance(t, carry):
        sr, si = carry
        rows = rows_of(t)
        return lr * sr - li * si + xr_ref[rows, :], lr * si + li * sr + xi_ref[rows, :]

    def step(t, carry):
        nr, ni = advance(t, carry)
        rows = rows_of(t)
        xr_ref[rows, :] = nr
        xi_ref[rows, :] = ni
        return nr, ni

    def run(body, init):
        return _unrolled_loop(body, 0, steps, init)

    er, ei = run(advance, (zero, zero))
    cr, ci = zero, zero
    for _ in range(SEGMENTS - 1):
        tr = er + lrt * cr - lit * ci
        ti = ei + lrt * ci + lit * cr
        cr, ci = _shift_rows(tr, reverse), _shift_rows(ti, reverse)
    run(step, (cr, ci))


def _permute_in(dst_ref, src_ref):
    steps = src_ref.shape[0] // SEGMENTS
    for s in range(SEGMENTS):
        dst_ref[pl.ds(s, steps, stride=SEGMENTS), :] = src_ref[s * steps:(s + 1) * steps, :].astype(dst_ref.dtype)


def _unpermute(src_ref, s):
    steps = src_ref.shape[0] // SEGMENTS
    return src_ref[pl.ds(s, steps, stride=SEGMENTS), :]


def _s5_fwd(proj, bblk, cblk, lam, name):
    seq = proj.shape[0]
    u_blk0 = (2 * QK_WIDTH + 2 * D_MODEL) // LANES
    sc = STATE_COLS

    def body(u_ref, b_ref, c_ref, lam_ref, y_ref, up_ref, yp_ref, xr_ref, xi_ref):
        _permute_in(up_ref, u_ref)
        ub = up_ref[...].astype(BF16)
        for d in range(2):
            xr_ref[...] = jnp.dot(ub, b_ref[d, :, 0:sc], preferred_element_type=F32)
            xi_ref[...] = jnp.dot(ub, b_ref[d, :, sc:2 * sc], preferred_element_type=F32)
            lm = [lam_ref[d, r:r + 1, :] for r in range(4)]
            _scan(xr_ref, xi_ref, lm, reverse=(d == 1), conj=False)
            yd = (jnp.dot(xr_ref[...].astype(BF16), c_ref[d, 0:sc, :], preferred_element_type=F32)
                  + jnp.dot(xi_ref[...].astype(BF16), c_ref[d, sc:2 * sc, :], preferred_element_type=F32))
            if d == 0:
                yp_ref[...] = yd
            else:
                yp_ref[...] += yd
        steps = seq // SEGMENTS
        for s in range(SEGMENTS):
            y_ref[s * steps:(s + 1) * steps, :] = _unpermute(yp_ref, s)

    return pl.pallas_call(
        body, name=name, grid=(N_TILES,),
        in_specs=[pl.BlockSpec((seq, LANES), lambda j: (0, u_blk0 + j)),
                  pl.BlockSpec((2, None, LANES, 2 * sc), lambda j: (0, j, 0, 0)),
                  pl.BlockSpec((2, None, 2 * sc, LANES), lambda j: (0, j, 0, 0)),
                  pl.BlockSpec((2, None, 4, sc), lambda j: (0, j, 0, 0))],
        out_specs=pl.BlockSpec((seq, LANES), lambda j: (0, j)),
        out_shape=jax.ShapeDtypeStruct((seq, D_MODEL), F32),
        scratch_shapes=[pltpu.VMEM((seq, LANES), F32), pltpu.VMEM((seq, LANES), F32),
                        pltpu.VMEM((seq, sc), F32), pltpu.VMEM((seq, sc), F32)],
        compiler_params=_params(("parallel",)),
    )(proj, bblk, cblk, lam)


def _s5_bwd(proj, dy, du_part, bblk, cblk, lam, name):
    seq = proj.shape[0]
    u_blk0 = (2 * QK_WIDTH + 2 * D_MODEL) // LANES
    sc = STATE_COLS
    steps = seq // SEGMENTS

    def body(u_ref, dy_ref, dup_ref, b_ref, c_ref, lam_ref, du_ref, db_ref, dc_ref, dlam_ref,
             up_ref, dyp_ref, dua_ref, xr_ref, xi_ref, gr_ref, gi_ref):
        _permute_in(up_ref, u_ref)
        _permute_in(dyp_ref, dy_ref)
        ub = up_ref[...].astype(BF16)
        dyb = dyp_ref[...].astype(BF16)
        ubt = up_ref[...].T.astype(BF16)
        dybt = dyp_ref[...].T.astype(BF16)
        for d in range(2):
            reverse = d == 1
            xr_ref[...] = jnp.dot(ub, b_ref[d, :, 0:sc], preferred_element_type=F32)
            xi_ref[...] = jnp.dot(ub, b_ref[d, :, sc:2 * sc], preferred_element_type=F32)
            lm = [lam_ref[d, r:r + 1, :] for r in range(4)]
            _scan(xr_ref, xi_ref, lm, reverse=reverse, conj=False)
            xrb = xr_ref[...].astype(BF16)
            xib = xi_ref[...].astype(BF16)
            dc_ref[d, :, 0:sc] = jnp.dot(dybt, xrb, preferred_element_type=F32)
            dc_ref[d, :, sc:2 * sc] = jnp.dot(dybt, xib, preferred_element_type=F32)
            gr_ref[...] = lax.dot_general(dyb, c_ref[d, 0:sc, :], _NT, preferred_element_type=F32)
            gi_ref[...] = lax.dot_general(dyb, c_ref[d, sc:2 * sc, :], _NT, preferred_element_type=F32)
            _scan(gr_ref, gi_ref, lm, reverse=not reverse, conj=True)

            def acc_step(t, carry):
                ar, ai = carry
                prev = _slab(t + 1 if reverse else t - 1)
                pr = xr_ref[prev, :]
                pi = xi_ref[prev, :]
                zr = gr_ref[_slab(t), :]
                zi = gi_ref[_slab(t), :]
                return ar + zr * pr + zi * pi, ai + zi * pr - zr * pi

            zero = jnp.zeros((SEGMENTS, sc), F32)
            if reverse:
                ar, ai = _unrolled_loop(acc_step, 0, steps - 1, (zero, zero))
                edge = _slab(steps - 1)
                pr = _shift_rows(xr_ref[_slab(0), :], True)
                pi = _shift_rows(xi_ref[_slab(0), :], True)
            else:
                ar, ai = _unrolled_loop(acc_step, 1, steps, (zero, zero))
                edge = _slab(0)
                pr = _shift_rows(xr_ref[_slab(steps - 1), :], False)
                pi = _shift_rows(xi_ref[_slab(steps - 1), :], False)
            zr = gr_ref[edge, :]
            zi = gi_ref[edge, :]
            ar = ar + zr * pr + zi * pi
            ai = ai + zi * pr - zr * pi
            dlam_ref[d, 0:1, :] = jnp.sum(ar, axis=0, keepdims=True)
            dlam_ref[d, 1:2, :] = jnp.sum(ai, axis=0, keepdims=True)

            grb = gr_ref[...].astype(BF16)
            gib = gi_ref[...].astype(BF16)
            db_ref[d, :, 0:sc] = jnp.dot(ubt, grb, preferred_element_type=F32)
            db_ref[d, :, sc:2 * sc] = jnp.dot(ubt, gib, preferred_element_type=F32)
            dud = (lax.dot_general(grb, b_ref[d, :, 0:sc], _NT, preferred_element_type=F32)
                   + lax.dot_general(gib, b_ref[d, :, sc:2 * sc], _NT, preferred_element_type=F32))
            if d == 0:
                dua_ref[...] = dud
            else:
                dua_ref[...] += dud
        for s in range(SEGMENTS):
            rows = slice(s * steps, (s + 1) * steps)
            du_ref[rows, :] = (_unpermute(dua_ref, s) + dup_ref[rows, :]).astype(du_ref.dtype)

    return pl.pallas_call(
        body, name=name, grid=(N_TILES,),
        in_specs=[pl.BlockSpec((seq, LANES), lambda j: (0, u_blk0 + j)),
                  pl.BlockSpec((seq, LANES), lambda j: (0, j)),
                  pl.BlockSpec((seq, LANES), lambda j: (0, j)),
                  pl.BlockSpec((2, None, LANES, 2 * sc), lambda j: (0, j, 0, 0)),
                  pl.BlockSpec((2, None, 2 * sc, LANES), lambda j: (0, j, 0, 0)),
                  pl.BlockSpec((2, None, 4, sc), lambda j: (0, j, 0, 0))],
        out_specs=[pl.BlockSpec((seq, LANES), lambda j: (0, j)),
                   pl.BlockSpec((2, None, LANES, 2 * sc), lambda j: (0, j, 0, 0)),
                   pl.BlockSpec((2, None, LANES, 2 * sc), lambda j: (0, j, 0, 0)),
                   pl.BlockSpec((2, None, 2, sc), lambda j: (0, j, 0, 0))],
        out_shape=[jax.ShapeDtypeStruct((seq, D_MODEL), BF16),
                   jax.ShapeDtypeStruct((2, N_TILES, LANES, 2 * sc), F32),
                   jax.ShapeDtypeStruct((2, N_TILES, LANES, 2 * sc), F32),
                   jax.ShapeDtypeStruct((2, N_TILES, 2, sc), F32)],
        scratch_shapes=[pltpu.VMEM((seq, LANES), F32), pltpu.VMEM((seq, LANES), F32), pltpu.VMEM((seq, LANES), F32),
                        pltpu.VMEM((seq, sc), F32), pltpu.VMEM((seq, sc), F32),
                        pltpu.VMEM((seq, sc), F32), pltpu.VMEM((seq, sc), F32)],
        compiler_params=_params(("parallel",)),
    )(proj, dy, du_part, bblk, cblk, lam)


def _s5_discretize(a_re, a_im, log_dt, b_re, b_im, seg_len):
    dt = jnp.exp(log_dt)[..., None]
    e = jnp.exp(a_re * dt)
    lr, li = e * jnp.cos(a_im * dt), e * jnp.sin(a_im * dt)
    et = jnp.exp(a_re * dt * seg_len)
    lrt, lit = et * jnp.cos(a_im * dt * seg_len), et * jnp.sin(a_im * dt * seg_len)
    den = a_re * a_re + a_im * a_im
    qr = ((lr - 1.0) * a_re + li * a_im) / den
    qi = (li * a_re - (lr - 1.0) * a_im) / den
    br = qr[..., None] * b_re - qi[..., None] * b_im
    bi = qr[..., None] * b_im + qi[..., None] * b_re
    return lr, li, lrt, lit, br, bi


def _s5_pack(lr, li, lrt, lit, br, bi, c_re, c_im):
    eye = jnp.eye(GROUPS_PER_TILE, dtype=F32)

    def bd_b(b):
        b5 = b.reshape(2, N_TILES, GROUPS_PER_TILE, N_STATE, GROUP)
        return jnp.einsum("dtgph,gk->dtghkp", b5, eye).reshape(2, N_TILES, LANES, STATE_COLS)

    def bd_c(c):
        c5 = c.reshape(2, N_TILES, GROUPS_PER_TILE, GROUP, N_STATE)
        return jnp.einsum("dtghp,gk->dtkpgh", c5, eye).reshape(2, N_TILES, STATE_COLS, LANES)

    bblk = jnp.concatenate([bd_b(br), bd_b(bi)], axis=3)
    cblk = jnp.concatenate([bd_c(c_re), -bd_c(c_im)], axis=2)
    lam = jnp.stack([v.reshape(2, N_TILES, STATE_COLS) for v in (lr, li, lrt, lit)], axis=2)
    return bblk, cblk, lam


def _s5_unpack(dbblk, dcblk, dlam):
    eye = jnp.eye(GROUPS_PER_TILE, dtype=F32)

    def diag_b(d):
        d6 = d.reshape(2, N_TILES, GROUPS_PER_TILE, GROUP, GROUPS_PER_TILE, N_STATE)
        return jnp.einsum("dtghkp,gk->dtgph", d6, eye).reshape(2, N_GROUPS, N_STATE, GROUP)

    def diag_c(d):
        d6 = d.reshape(2, N_TILES, GROUPS_PER_TILE, GROUP, GROUPS_PER_TILE, N_STATE)
        return jnp.einsum("dtghkp,gk->dtghp", d6, eye).reshape(2, N_GROUPS, GROUP, N_STATE)

    dbr, dbi = diag_b(dbblk[..., :STATE_COLS]), diag_b(dbblk[..., STATE_COLS:])
    dcr, dci = diag_c(dcblk[..., :STATE_COLS]), -diag_c(dcblk[..., STATE_COLS:])
    dlr = dlam[:, :, 0, :].reshape(2, N_GROUPS, N_STATE)
    dli = dlam[:, :, 1, :].reshape(2, N_GROUPS, N_STATE)
    return dlr, dli, dbr, dbi, dcr, dci


def _pos():
    return lax.axis_index("x"), lax.axis_index("y"), lax.axis_index("c")


def _remote(src, dst, ssem, rsem, dev):
    return pltpu.make_async_remote_copy(src_ref=src, dst_ref=dst, send_sem=ssem, recv_sem=rsem,
                                        device_id=dev, device_id_type=MESH)


_PIECES = (
    ("w_in", "in", D_MODEL, IN_WIDTH // N_CHIPS, 0, IN_WIDTH // N_CHIPS, 0),
    ("w_glu", "glu", D_MODEL // N_CHIPS, D_MODEL, D_MODEL // N_CHIPS, 0, 0),
    ("w_out", "out", D_MODEL // N_CHIPS, D_MODEL, D_MODEL // N_CHIPS, 0, 0),
    ("w_ffn_gate", "gu", D_MODEL, D_FF // N_CHIPS, 0, D_FF // N_CHIPS, 0),
    ("w_ffn_up", "gu", D_MODEL, D_FF // N_CHIPS, 0, D_FF // N_CHIPS, D_FF),
    ("w_ffn_down", "down", D_FF // N_CHIPS, D_MODEL, D_FF // N_CHIPS, 0, 0),
)
_BUFFERS = (("in", D_MODEL, IN_WIDTH), ("glu", D_MODEL, D_MODEL), ("out", D_MODEL, D_MODEL),
            ("gu", D_MODEL, 2 * D_FF), ("down", D_FF, D_MODEL))
_BUF_INDEX = {name: t for t, (name, _, _) in enumerate(_BUFFERS)}
N_PIECES = len(_PIECES)
N_BUFFERS = len(_BUFFERS)


def _own_block(piece, tm):
    _, _, _, cs, rstep, cstep, coff = piece
    return lambda i, chip: (i + chip * (rstep // tm), coff // cs + chip * (cstep // cs))


def _cast_place(piece, w3, layer, prev, chip_arr, name):
    _, r, cc = w3.shape
    _, rf, cf = _BUFFERS[_BUF_INDEX[piece[1]]]
    tm = _tile(r, 256)
    own = _own_block(piece, tm)

    def body(s_ref, w_ref, *rest):
        rest[-1][...] = w_ref[...].astype(BF16)

    in_specs = [pl.BlockSpec((None, tm, cc), lambda i, s: (layer, i, 0))]
    args = [w3]
    aliases = {}
    if prev is not None:
        in_specs.append(pl.BlockSpec(memory_space=pl.ANY))
        args.append(prev)
        aliases = {2: 0}
    return pl.pallas_call(
        body, name=name,
        grid_spec=pltpu.PrefetchScalarGridSpec(
            num_scalar_prefetch=1, grid=(r // tm,), in_specs=in_specs,
            out_specs=pl.BlockSpec((tm, cc), lambda i, s: own(i, s[0]))),
        out_shape=jax.ShapeDtypeStruct((rf, cf), BF16), input_output_aliases=aliases,
        compiler_params=_params(("parallel",)),
    )(chip_arr, *args)


_GATHER_GROUPS = ((0, (0,)), (0, (1, 2, 3, 4, 5)), (1, (0,)), (1, (1, 2, 3, 4, 5)))
_SPLIT_EFFECT = pltpu.SideEffectType.DATAFLOW_SIDE_EFFECTING
SEM_SPEC = pl.BlockSpec(memory_space=pltpu.SEMAPHORE)
BF16_ROWS = 2 * SUBLANES


def _group_keys(g):
    layer, pieces = _GATHER_GROUPS[g]
    keys = []
    for p in pieces:
        if (_PIECES[p][1], layer) not in keys:
            keys.append((_PIECES[p][1], layer))
    return keys


def _half_view(ref, piece, j, c):
    _, _, rs, cs, rstep, cstep, coff = piece
    half = rs // 2
    return ref.at[pl.ds(pl.multiple_of(j * rstep + c * half, BF16_ROWS), half), pl.ds(coff + j * cstep, cs)]


def _for_my_chip(fn):
    x, y, _ = _pos()
    for mine in range(N_CHIPS):
        pl.when(2 * x + y == mine)(functools.partial(fn, mine, [j for j in range(N_CHIPS) if j != mine]))


def _gather_start(placed):
    keys = [k for g in range(len(_GATHER_GROUPS)) for k in _group_keys(g)]
    nb, ng = len(keys), len(_GATHER_GROUPS)

    def body(*refs):
        bufs = dict(zip(keys, refs[nb:2 * nb]))
        ssems = refs[2 * nb:2 * nb + ng]
        rsems = refs[2 * nb + ng:2 * nb + 2 * ng]
        token = refs[2 * nb + 2 * ng]
        _, _, c = _pos()

        def send(mine, others):
            for g, (layer, pieces) in enumerate(_GATHER_GROUPS):
                for k, p in enumerate(pieces):
                    view = _half_view(bufs[(_PIECES[p][1], layer)], _PIECES[p], mine, c)
                    for j in others:
                        _remote(view, view, ssems[g].at[k * N_CHIPS + j], rsems[g].at[k * N_CHIPS + mine],
                                (j // 2, j % 2, c)).start()

        _for_my_chip(send)
        token[...] = jnp.zeros(token.shape, token.dtype)

    sems = [pltpu.SemaphoreType.DMA((N_CHIPS * len(pieces),)) for _, pieces in _GATHER_GROUPS]
    shapes = [jax.ShapeDtypeStruct(a.shape, a.dtype) for a in placed]
    res = pl.pallas_call(
        body, name="gather_start",
        in_specs=[HBM_SPEC] * nb,
        out_specs=[HBM_SPEC] * nb + [SEM_SPEC] * (2 * ng) + [pl.BlockSpec(memory_space=pltpu.VMEM)],
        out_shape=shapes + sems + sems + [jax.ShapeDtypeStruct((SUBLANES, LANES), F32)],
        input_output_aliases={t: t for t in range(nb)},
        compiler_params=_params(has_side_effects=_SPLIT_EFFECT),
    )(*[pltpu.with_memory_space_constraint(a, pltpu.HBM) for a in placed])
    return dict(zip(keys, res[:nb])), res[nb:nb + ng], res[nb + ng:nb + 2 * ng], res[nb + 2 * ng]


def _gather_wait(g, bufs, ssem, rsem, after):
    layer, pieces = _GATHER_GROUPS[g]
    keys = _group_keys(g)
    nb = len(keys)

    def body(*refs):
        ssem_ref, rsem_ref = refs[nb], refs[nb + 1]
        land = dict(zip(keys, refs[nb + 3:]))
        _, _, c = _pos()

        def wait(mine, others):
            for k, p in enumerate(pieces):
                ref = land[(_PIECES[p][1], layer)]
                for j in others:
                    cp = _remote(_half_view(ref, _PIECES[p], mine, c), _half_view(ref, _PIECES[p], j, c),
                                 ssem_ref.at[k * N_CHIPS + j], rsem_ref.at[k * N_CHIPS + j], (j // 2, j % 2, c))
                    cp.wait_send()
                    cp.wait_recv()

        _for_my_chip(wait)

    return pl.pallas_call(
        body, name="gather_wait_g%d" % g,
        in_specs=[HBM_SPEC] * nb + [SEM_SPEC, SEM_SPEC, pl.BlockSpec(memory_space=pl.ANY)],
        out_specs=[HBM_SPEC] * nb,
        out_shape=[jax.ShapeDtypeStruct(a.shape, a.dtype) for a in bufs],
        input_output_aliases={t: t for t in range(nb)},
        compiler_params=_params(has_side_effects=_SPLIT_EFFECT),
    )(*bufs, ssem, rsem, after)


def _gather_forward(g, bufs):
    layer, pieces = _GATHER_GROUPS[g]
    keys = _group_keys(g)
    nb = len(keys)

    def body(*refs):
        land = dict(zip(keys, refs[nb:2 * nb]))
        ssem, rsem = refs[2 * nb:]
        x, y, c = _pos()

        def forward(mine, others):
            cps = []
            for k, p in enumerate(pieces):
                ref = land[(_PIECES[p][1], layer)]
                for j in others:
                    view = _half_view(ref, _PIECES[p], j, c)
                    cp = _remote(view, view, ssem.at[k * N_CHIPS + j], rsem.at[k * N_CHIPS + j], (x, y, 1 - c))
                    cp.start()
                    cps.append(cp)
            for k, p in enumerate(pieces):
                ref = land[(_PIECES[p][1], layer)]
                for j in others:
                    view = _half_view(ref, _PIECES[p], j, 1 - c)
                    _remote(view, view, ssem.at[k * N_CHIPS + j], rsem.at[k * N_CHIPS + j], (x, y, 1 - c)).wait_recv()
            for cp in cps:
                cp.wait_send()

        _for_my_chip(forward)

    nsem = N_CHIPS * len(pieces)
    return pl.pallas_call(
        body, name="gather_forward_g%d" % g,
        in_specs=[HBM_SPEC] * nb, out_specs=[HBM_SPEC] * nb,
        out_shape=[jax.ShapeDtypeStruct(a.shape, a.dtype) for a in bufs],
        input_output_aliases={t: t for t in range(nb)},
        scratch_shapes=[pltpu.SemaphoreType.DMA((nsem,)), pltpu.SemaphoreType.DMA((nsem,))],
        compiler_params=_params(has_side_effects=True),
    )(*bufs)


_REDUCE_GROUPS = (
    ((5, 1), (3, 1), (4, 1), (2, 1), (1, 1), (0, 1)),
    ((5, 0), (3, 0), (4, 0)),
    ((2, 0), (1, 0)),
    ((0, 0),),
)


def _reduce_keys(group):
    keys = []
    for p, layer in group:
        if (_PIECES[p][1], layer) not in keys:
            keys.append((_PIECES[p][1], layer))
    return keys


def _half_block(piece, tm):
    _, _, rs, cs, rstep, cstep, coff = piece
    return lambda i, j, c: (j * (rstep // tm) + c * (rs // 2 // tm) + i, coff // cs + j * (cstep // cs))


def _swap_start(g, dwb):
    group = _REDUCE_GROUPS[g]
    keys = _reduce_keys(group)
    nk = len(keys)

    def body(*refs):
        src = dict(zip(keys, refs[nk:2 * nk]))
        dst = dict(zip(keys, refs[2 * nk:3 * nk]))
        ssem, rsem, token = refs[3 * nk:]
        x, y, c = _pos()
        for k, (p, layer) in enumerate(group):
            key = (_PIECES[p][1], layer)
            for j in range(N_CHIPS):
                _remote(_half_view(src[key], _PIECES[p], j, 1 - c), _half_view(dst[key], _PIECES[p], j, 1 - c),
                        ssem.at[k * N_CHIPS + j], rsem.at[k * N_CHIPS + j], (x, y, 1 - c)).start()
        token[...] = jnp.zeros(token.shape, token.dtype)

    sem = pltpu.SemaphoreType.DMA((N_CHIPS * len(group),))
    shapes = [jax.ShapeDtypeStruct(dwb[k].shape, BF16) for k in keys]
    res = pl.pallas_call(
        body, name="swap_start_g%d" % g,
        in_specs=[HBM_SPEC] * nk,
        out_specs=[HBM_SPEC] * (2 * nk) + [SEM_SPEC, SEM_SPEC, pl.BlockSpec(memory_space=pltpu.VMEM)],
        out_shape=shapes + shapes + [sem, sem, jax.ShapeDtypeStruct((SUBLANES, LANES), F32)],
        input_output_aliases={t: t for t in range(nk)},
        compiler_params=_params(has_side_effects=_SPLIT_EFFECT),
    )(*[pltpu.with_memory_space_constraint(dwb[k], pltpu.HBM) for k in keys])
    return list(res[:nk]), list(res[nk:2 * nk]), res[2 * nk], res[2 * nk + 1], res[2 * nk + 2]


def _swap_wait(g, own, land, ssem, rsem, after):
    group = _REDUCE_GROUPS[g]
    keys = _reduce_keys(group)
    nk = len(keys)

    def body(*refs):
        ssem_ref, rsem_ref = refs[2 * nk], refs[2 * nk + 1]
        src = dict(zip(keys, refs[2 * nk + 3:3 * nk + 3]))
        dst = dict(zip(keys, refs[3 * nk + 3:]))
        x, y, c = _pos()
        for k, (p, layer) in enumerate(group):
            key = (_PIECES[p][1], layer)
            for j in range(N_CHIPS):
                cp = _remote(_half_view(src[key], _PIECES[p], j, 1 - c), _half_view(dst[key], _PIECES[p], j, c),
                             ssem_ref.at[k * N_CHIPS + j], rsem_ref.at[k * N_CHIPS + j], (x, y, 1 - c))
                cp.wait_send()
                cp.wait_recv()

    res = pl.pallas_call(
        body, name="swap_wait_g%d" % g,
        in_specs=[HBM_SPEC] * (2 * nk) + [SEM_SPEC, SEM_SPEC, pl.BlockSpec(memory_space=pl.ANY)],
        out_specs=[HBM_SPEC] * (2 * nk),
        out_shape=[jax.ShapeDtypeStruct(a.shape, a.dtype) for a in list(own) + list(land)],
        input_output_aliases={t: t for t in range(2 * nk)},
        compiler_params=_params(has_side_effects=_SPLIT_EFFECT),
    )(*own, *land, ssem, rsem, after)
    return dict(zip(keys, res[nk:]))


def _chip_partial(piece, dw, got, prev, c_arr, name):
    _, _, rs, cs, _, _, _ = piece
    half = rs // 2
    tm = _tile(half, 256)
    blk = _half_block(piece, tm)

    def body(s_ref, dw_ref, got_ref, *rest):
        rest[-1][...] = (dw_ref[...] + got_ref[...].astype(F32)).astype(BF16)

    spec = pl.BlockSpec((tm, cs), lambda j, i, s: blk(i, j, s[0]))
    in_specs = [spec, spec]
    args = [dw, got]
    aliases = {}
    if prev is not None:
        in_specs.append(pl.BlockSpec(memory_space=pl.ANY))
        args.append(prev)
        aliases = {3: 0}
    return pl.pallas_call(
        body, name=name,
        grid_spec=pltpu.PrefetchScalarGridSpec(
            num_scalar_prefetch=1, grid=(N_CHIPS, half // tm), in_specs=in_specs, out_specs=spec),
        out_shape=jax.ShapeDtypeStruct(dw.shape, BF16), input_output_aliases=aliases,
        compiler_params=_params(("parallel", "parallel")),
    )(c_arr, *args)


def _scatter_start(g, partials):
    group = _REDUCE_GROUPS[g]
    keys = _reduce_keys(group)
    nk, n = len(keys), len(group)

    def body(*refs):
        pt = dict(zip(keys, refs[nk:2 * nk]))
        land = refs[2 * nk:2 * nk + n]
        ssem, rsem, token = refs[2 * nk + n:]
        _, _, c = _pos()

        def send(mine, others):
            for k, (p, layer) in enumerate(group):
                for j in others:
                    _remote(_half_view(pt[(_PIECES[p][1], layer)], _PIECES[p], j, c), land[k].at[mine],
                            ssem.at[k * N_CHIPS + j], rsem.at[k * N_CHIPS + mine], (j // 2, j % 2, c)).start()

        _for_my_chip(send)
        token[...] = jnp.zeros(token.shape, token.dtype)

    sem = pltpu.SemaphoreType.DMA((N_CHIPS * n,))
    res = pl.pallas_call(
        body, name="scatter_start_g%d" % g,
        in_specs=[HBM_SPEC] * nk,
        out_specs=[HBM_SPEC] * (nk + n) + [SEM_SPEC, SEM_SPEC, pl.BlockSpec(memory_space=pltpu.VMEM)],
        out_shape=([jax.ShapeDtypeStruct(partials[k].shape, BF16) for k in keys]
                   + [jax.ShapeDtypeStruct((N_CHIPS, _PIECES[p][2] // 2, _PIECES[p][3]), BF16) for p, _ in group]
                   + [sem, sem, jax.ShapeDtypeStruct((SUBLANES, LANES), F32)]),
        input_output_aliases={t: t for t in range(nk)},
        compiler_params=_params(has_side_effects=_SPLIT_EFFECT),
    )(*[pltpu.with_memory_space_constraint(partials[k], pltpu.HBM) for k in keys])
    return list(res[:nk]), list(res[nk:nk + n]), res[nk + n], res[nk + n + 1], res[nk + n + 2]


def _scatter_wait(g, partials, land, ssem, rsem, after):
    group = _REDUCE_GROUPS[g]
    keys = _reduce_keys(group)
    nk, n = len(keys), len(group)

    def body(*refs):
        ssem_ref, rsem_ref = refs[nk + n], refs[nk + n + 1]
        pt = dict(zip(keys, refs[nk + n + 3:2 * nk + n + 3]))
        land_ref = refs[2 * nk + n + 3:]
        _, _, c = _pos()

        def wait(mine, others):
            for k, (p, layer) in enumerate(group):
                for j in others:
                    cp = _remote(_half_view(pt[(_PIECES[p][1], layer)], _PIECES[p], j, c), land_ref[k].at[j],
                                 ssem_ref.at[k * N_CHIPS + j], rsem_ref.at[k * N_CHIPS + j], (j // 2, j % 2, c))
                    cp.wait_send()
                    cp.wait_recv()

        _for_my_chip(wait)

    res = pl.pallas_call(
        body, name="scatter_wait_g%d" % g,
        in_specs=[HBM_SPEC] * (nk + n) + [SEM_SPEC, SEM_SPEC, pl.BlockSpec(memory_space=pl.ANY)],
        out_specs=[HBM_SPEC] * (nk + n),
        out_shape=[jax.ShapeDtypeStruct(a.shape, a.dtype) for a in list(partials) + list(land)],
        input_output_aliases={t: t for t in range(nk + n)},
        compiler_params=_params(has_side_effects=_SPLIT_EFFECT),
    )(*partials, *land, ssem, rsem, after)
    return list(res[nk:])


def _reduce_half(piece, layer, dw, got, land, prev, idx, name):
    _, _, rs, cs, _, _, _ = piece
    half = rs // 2
    tm = _tile(half, 256)
    blk = _half_block(piece, tm)

    def body(s_ref, dw_ref, got_ref, r1, r2, r3, *rest):
        acc = dw_ref[...] + got_ref[...].astype(F32)
        for r in (r1, r2, r3):
            acc = acc + r[...].astype(F32)
        rest[-1][...] = acc

    def land_map(k):
        return lambda i, s: ((s[1] + k) % N_CHIPS, i, 0)

    own = pl.BlockSpec((tm, cs), lambda i, s: blk(i, s[1], s[0]))
    in_specs = [own, own] + [pl.BlockSpec((None, tm, cs), land_map(k)) for k in (1, 2, 3)]
    args = [dw, got, land, land, land]
    aliases = {}
    if prev is not None:
        in_specs.append(pl.BlockSpec(memory_space=pl.ANY))
        args.append(prev)
        aliases = {6: 0}
    return pl.pallas_call(
        body, name=name,
        grid_spec=pltpu.PrefetchScalarGridSpec(
            num_scalar_prefetch=1, grid=(half // tm,), in_specs=in_specs,
            out_specs=pl.BlockSpec((None, tm, cs), lambda i, s: (layer, s[0] * (half // tm) + i, 0))),
        out_shape=jax.ShapeDtypeStruct((DEPTH, rs, cs), F32), input_output_aliases=aliases,
        compiler_params=_params(("parallel",)),
    )(idx, *args)


def _share_halves(reduced):
    def body(*refs):
        buf = refs[N_PIECES:2 * N_PIECES]
        ssem, rsem = refs[2 * N_PIECES:]
        x, y, c = _pos()

        def half(p, layer, cc):
            rows = _PIECES[p][2] // 2
            return buf[p].at[layer, pl.ds(pl.multiple_of(cc * rows, SUBLANES), rows), :]

        pairs = [(p, layer) for p in range(N_PIECES) for layer in range(DEPTH)]
        rem = [_remote(half(p, layer, c), half(p, layer, c), ssem.at[k], rsem.at[k], (x, y, 1 - c))
               for k, (p, layer) in enumerate(pairs)]
        for cp in rem:
            cp.start()
        for k, (p, layer) in enumerate(pairs):
            rem[k].wait_send()
            _remote(half(p, layer, 1 - c), half(p, layer, 1 - c), ssem.at[k], rsem.at[k], (x, y, 1 - c)).wait_recv()

    nsem = N_PIECES * DEPTH
    return pl.pallas_call(
        body, name="share_halves",
        in_specs=[HBM_SPEC] * N_PIECES, out_specs=[HBM_SPEC] * N_PIECES,
        out_shape=[jax.ShapeDtypeStruct((DEPTH, p[2], p[3]), F32) for p in _PIECES],
        input_output_aliases={t: t for t in range(N_PIECES)},
        scratch_shapes=[pltpu.SemaphoreType.DMA((nsem,)), pltpu.SemaphoreType.DMA((nsem,))],
        compiler_params=_params(has_side_effects=True),
    )(*reduced)


N_DEV = 8


def _place_slot(v, me_arr):
    rows = v.shape[0]
    tm = _tile(rows, 512)

    def body(s_ref, v_ref, out_ref):
        out_ref[...] = v_ref[...]

    return pl.pallas_call(
        body, name="place_small",
        grid_spec=pltpu.PrefetchScalarGridSpec(
            num_scalar_prefetch=1, grid=(rows // tm,),
            in_specs=[pl.BlockSpec((tm, LANES), lambda i, s: (i, 0))],
            out_specs=pl.BlockSpec((None, tm, LANES), lambda i, s: (s[0], i, 0))),
        out_shape=jax.ShapeDtypeStruct((N_DEV, rows, LANES), F32),
        compiler_params=_params(("parallel",)),
    )(me_arr, v)


def _peers():
    x, y, c = _pos()
    return (x, y, c), (x, y, 1 - c), [(1 - x, y, c), (x, 1 - y, c), (1 - x, 1 - y, c)]


def _slot_of(ref, dev):
    return ref.at[4 * dev[0] + 2 * dev[1] + dev[2]]


def _small_gather_start(g):
    def body(g_in, g_ref, ssem, rsem, token):
        me, sibling, others = _peers()
        for k, dev in enumerate([sibling] + others):
            _remote(_slot_of(g_ref, me), _slot_of(g_ref, me), ssem.at[k], rsem.at[k], dev).start()
        token[...] = jnp.zeros(token.shape, token.dtype)

    sem = pltpu.SemaphoreType.DMA((N_CHIPS,))
    return pl.pallas_call(
        body, name="small_gather_start",
        in_specs=[HBM_SPEC], out_specs=[HBM_SPEC, SEM_SPEC, SEM_SPEC, pl.BlockSpec(memory_space=pltpu.VMEM)],
        out_shape=[jax.ShapeDtypeStruct(g.shape, g.dtype), sem, sem, jax.ShapeDtypeStruct((SUBLANES, LANES), F32)],
        input_output_aliases={0: 0},
        compiler_params=_params(has_side_effects=_SPLIT_EFFECT),
    )(pltpu.with_memory_space_constraint(g, pltpu.HBM))


def _small_gather_wait(g, ssem, rsem, after):
    def body(g_in, ssem_ref, rsem_ref, after_ref, g_ref):
        me, sibling, others = _peers()
        for k, dev in enumerate([sibling] + others):
            cp = _remote(_slot_of(g_ref, me), _slot_of(g_ref, dev), ssem_ref.at[k], rsem_ref.at[k], dev)
            cp.wait_send()
            cp.wait_recv()

    return pl.pallas_call(
        body, name="small_gather_wait",
        in_specs=[HBM_SPEC, SEM_SPEC, SEM_SPEC, pl.BlockSpec(memory_space=pl.ANY)], out_specs=HBM_SPEC,
        out_shape=jax.ShapeDtypeStruct(g.shape, g.dtype), input_output_aliases={0: 0},
        compiler_params=_params(has_side_effects=_SPLIT_EFFECT),
    )(g, ssem, rsem, after)


def _small_gather_forward(g):
    def body(g_in, g_ref, ssem, rsem):
        me, sibling, others = _peers()
        cps = [_remote(_slot_of(g_ref, dev), _slot_of(g_ref, dev), ssem.at[k], rsem.at[k], sibling)
               for k, dev in enumerate(others)]
        for cp in cps:
            cp.start()
        for k, dev in enumerate(others):
            theirs = _slot_of(g_ref, (dev[0], dev[1], sibling[2]))
            _remote(theirs, theirs, ssem.at[k], rsem.at[k], sibling).wait_recv()
        for cp in cps:
            cp.wait_send()

    nsem = N_CHIPS - 1
    return pl.pallas_call(
        body, name="small_gather_forward",
        in_specs=[HBM_SPEC], out_specs=HBM_SPEC, out_shape=jax.ShapeDtypeStruct(g.shape, g.dtype),
        input_output_aliases={0: 0},
        scratch_shapes=[pltpu.SemaphoreType.DMA((nsem,)), pltpu.SemaphoreType.DMA((nsem,))],
        compiler_params=_params(has_side_effects=True),
    )(g)


def _sum_slots(g, name):
    n, rows, _ = g.shape
    tm = _tile(rows, 512)

    def body(g_ref, out_ref):
        acc = g_ref[0]
        for k in range(1, n):
            acc = acc + g_ref[k]
        out_ref[...] = acc

    return pl.pallas_call(
        body, name=name, grid=(rows // tm,),
        in_specs=[pl.BlockSpec((n, tm, LANES), lambda i: (0, i, 0))],
        out_specs=pl.BlockSpec((tm, LANES), lambda i: (i, 0)),
        out_shape=jax.ShapeDtypeStruct((rows, LANES), F32),
        compiler_params=_params(("parallel",)),
    )(g)


_TINY = ("ln_mix_g", "ret_log_gamma", "ssm_a_re", "ssm_a_im", "ssm_log_dt", "ssm_d", "b_glu", "ln_ffn_g", "ln_final_g")
_MID = ("ssm_b_re", "ssm_b_im", "ssm_c_re", "ssm_c_im")
_SMALL = _TINY + _MID
_FLAT_ALIGN = LANES * LANES


def _flat_rows(like, names):
    return sum((math.prod(like[n].shape) + (-math.prod(like[n].shape)) % _FLAT_ALIGN) // LANES for n in names)


def _flatten(d, names):
    parts = []
    for n in names:
        f = d[n].reshape(-1)
        parts.append(jnp.pad(f, (0, (-f.shape[0]) % _FLAT_ALIGN)))
    return jnp.concatenate(parts).reshape(-1, LANES)


def _unflatten(flat, like, names):
    out, row = {}, 0
    for n in names:
        size = math.prod(like[n].shape)
        rows = (size + (-size) % _FLAT_ALIGN) // LANES
        part = lax.optimization_barrier(flat[row:row + rows])
        out[n] = part.reshape(-1)[:size].reshape(like[n].shape)
        row += rows
    return out


_BIG = ("w_in", "w_glu", "w_out", "w_ffn_gate", "w_ffn_up", "w_ffn_down")
_WEIGHTS = ("ln_mix_g", "w_in", "ret_log_gamma", "ssm_a_re", "ssm_a_im", "ssm_log_dt", "ssm_b_re", "ssm_b_im",
            "ssm_c_re", "ssm_c_im", "ssm_d", "w_glu", "b_glu", "w_out", "ln_ffn_g", "w_ffn_gate", "w_ffn_up",
            "w_ffn_down", "ln_final_g")


def _rope_tables(seq):
    half = QK_DIM // 2
    inv = 1.0 / (ROPE_BASE ** (jnp.arange(half, dtype=F32) / half))
    ang = jnp.arange(seq, dtype=F32)[:, None] * inv[None, :]
    return jnp.cos(ang), jnp.sin(ang)


def _step(w, m, v, x, target):
    seq = x.shape[0]
    seg_len = float(seq // SEGMENTS)
    c_idx = lax.axis_index("c").astype(jnp.int32)
    chip_idx = (2 * lax.axis_index("x") + lax.axis_index("y")).astype(jnp.int32)
    c_arr = jnp.stack([c_idx])
    idx_arr = jnp.stack([c_idx, chip_idx])

    chip_arr = jnp.stack([chip_idx])
    placed = {}
    for piece in _PIECES:
        for layer in range(DEPTH):
            key = (piece[1], layer)
            placed[key] = _cast_place(piece, w[piece[0]], layer, placed.get(key), chip_arr,
                                      "cast_%s_l%d" % (piece[0], layer))
    keys = [k for g in range(len(_GATHER_GROUPS)) for k in _group_keys(g)]
    flying, ssems, rsems, token = _gather_start([placed[k] for k in keys])
    wf = {b[0]: [None] * DEPTH for b in _BUFFERS}

    def arrive(g, after):
        ks = _group_keys(g)
        landed = _gather_wait(g, [flying[k] for k in ks], ssems[g], rsems[g], after)
        for k, a in zip(ks, _gather_forward(g, landed)):
            wf[k[0]][k[1]] = a

    cos, sin = _rope_tables(seq)

    saved = []
    xc = x + token[0, 0]
    for i in range(DEPTH):
        t = "_l%d" % i
        s = {"x_in": xc}
        s["h"] = _rms_fwd(xc, w["ln_mix_g"][i:i + 1], "rms_mix" + t)
        arrive(2 * i, s["h"])
        s["proj"] = _matmul(s["h"], wf["in"][i], "nn", [F32], name="mm_in" + t)[0]
        s["qr"], s["kr"] = _rot_fwd(s["proj"], cos, sin, "rot" + t)
        s["lg"] = jnp.broadcast_to(w["ret_log_gamma"][i].T[:, :, None], (HEADS, 2, LANES))
        s["y"] = _ret_fwd(s["qr"], s["kr"], s["proj"], s["lg"], "ret" + t)
        s5_raw = (w["ssm_a_re"][i], w["ssm_a_im"][i], w["ssm_log_dt"][i], w["ssm_b_re"][i], w["ssm_b_im"][i])
        disc, s["disc_vjp"] = jax.vjp(functools.partial(_s5_discretize, seg_len=seg_len), *s5_raw)
        bblk, cblk, lam = _s5_pack(*disc, w["ssm_c_re"][i], w["ssm_c_im"][i])
        s["s5"] = (bblk.astype(BF16), cblk.astype(BF16), lam)
        s["s5y"] = _s5_fwd(s["proj"], *s["s5"], "s5" + t)
        s["ret"], s["ysg"], s["ysgb"] = _post1_fwd(s["y"], s["proj"], s["s5y"], w["ssm_d"][i:i + 1], "post" + t)
        arrive(2 * i + 1, s["ysgb"])
        s["z"] = _matmul(s["ysgb"], wf["glu"][i], "nn", [F32], name="mm_glu" + t)[0]
        s["merged"] = _merge_fwd(s["z"], s["ysg"], s["proj"], s["ret"], w["b_glu"][i:i + 1], "merge" + t)
        s["x1"] = _matmul(s["merged"], wf["out"][i], "nn", [F32], add=xc, name="mm_out" + t)[0]
        s["h2"] = _rms_fwd(s["x1"], w["ln_ffn_g"][i:i + 1], "rms_ffn" + t)
        s["ab"] = _matmul(s["h2"], wf["gu"][i], "nn", [F32], name="mm_gu" + t)[0]
        s["f"] = _glu_fwd(s["ab"], "glu" + t)
        xc = _matmul(s["f"], wf["down"][i], "nn", [F32], add=s["x1"], name="mm_down" + t)[0]
        saved.append(s)

    dx, dxb, loss_row, dg_final = _loss_stage(xc, target, w["ln_final_g"][None, :], "loss")
    loss = lax.psum(loss_row[0, 0], ("x", "y", "c"))

    g_small = {"ln_final_g": dg_final[0]}
    per_layer = {n: [None] * DEPTH for n in _SMALL if n != "ln_final_g"}
    dws, got, swaps, flights = {}, {}, {}, []

    def dw_mm(a, b, buf, i, name):
        dws[(buf, i)] = _matmul(a, b, "tn", [F32, BF16], name=name)

    def depart(g):
        keys = _reduce_keys(_REDUCE_GROUPS[g])
        own, land, ssem, rsem, tok = _swap_start(g, {k: dws[k][1] for k in keys})
        swaps[g] = (own, land, ssem, rsem)
        return tok[0:1, 0:1]

    def proceed(g, after):
        group = _REDUCE_GROUPS[g]
        got.update(_swap_wait(g, *swaps[g], after))
        partials = {}
        for p, layer in group:
            key = (_PIECES[p][1], layer)
            partials[key] = _chip_partial(_PIECES[p], dws[key][0], got[key], partials.get(key), c_arr,
                                          "chip_partial_%s_l%d" % (_PIECES[p][0], layer))
        pt, land, ssem, rsem, tok = _scatter_start(g, partials)
        flights.append((g, pt, land, ssem, rsem))
        return tok[0:1, 0:1]

    for i in reversed(range(DEPTH)):
        t = "_l%d" % i
        s = saved[i]
        g_ffn, g_mix, d_skip = w["ln_ffn_g"][i:i + 1], w["ln_mix_g"][i:i + 1], w["ssm_d"][i:i + 1]
        dw_mm(s["f"], dxb, "down", i, "dw_down" + t)
        df = _matmul(dxb, wf["down"][i], "nt", [F32], name="dx_down" + t)[0]
        if i == 0:
            g_ffn = g_ffn + proceed(0, df)
        dab = _glu_bwd(s["ab"], df, "glu_bwd" + t)
        dw_mm(s["h2"], dab, "gu", i, "dw_gu" + t)
        if i == 0:
            g_ffn = g_ffn + depart(1)
        dh2 = _matmul(dab, wf["gu"][i], "nt", [F32], name="dx_gu" + t)[0]
        if i == 0:
            g_ffn = g_ffn + proceed(1, dh2)
        dx1, dx1b, dg = _rms_bwd(s["x1"], dh2, dx, g_ffn, "rms_ffn_bwd" + t)
        per_layer["ln_ffn_g"][i] = dg[0]

        dw_mm(s["merged"], dx1b, "out", i, "dw_out" + t)
        dmerged = _matmul(dx1b, wf["out"][i], "nt", [F32], name="dx_out" + t)[0]
        dz, dys_part, dgs, db = _merge_bwd(s["z"], s["ysg"], s["proj"], s["ret"], dmerged, w["b_glu"][i:i + 1],
                                           "merge_bwd" + t)
        per_layer["b_glu"][i] = db[0]
        dw_mm(s["ysgb"], dz, "glu", i, "dw_glu" + t)
        if i == 0:
            d_skip = d_skip + depart(2)
        dys = _matmul(dz, wf["glu"][i], "nt", [F32], add=dys_part, name="dx_glu" + t)[0]
        if i == 0:
            d_skip = d_skip + proceed(2, dys)
        dy, dgg, dgr, ds5, du_part, dd = _post1_bwd(s["y"], s["proj"], s["s5y"], dmerged, dys,
                                                    d_skip, "post_bwd" + t)
        per_layer["ssm_d"][i] = dd[0]
        du, dbblk, dcblk, dlam = _s5_bwd(s["proj"], ds5, du_part, *s["s5"], "s5_bwd" + t)
        dlr, dli, dbr, dbi, dcr, dci = _s5_unpack(dbblk, dcblk, dlam)
        zeros = jnp.zeros_like(dlr)
        da_re, da_im, dlog_dt, db_re, db_im = s["disc_vjp"]((dlr, dli, zeros, zeros, dbr, dbi))
        for n, val in (("ssm_a_re", da_re), ("ssm_a_im", da_im), ("ssm_log_dt", dlog_dt), ("ssm_b_re", db_re),
                       ("ssm_b_im", db_im), ("ssm_c_re", dcr), ("ssm_c_im", dci)):
            per_layer[n][i] = val
        dqr, dkr, dv, dlg = _ret_bwd(s["qr"], s["kr"], s["proj"], dy, s["lg"], "ret_bwd" + t)
        per_layer["ret_log_gamma"][i] = dlg[:, :, 0].T
        dqkv = _rot_bwd(dqr, dkr, dv, cos, sin, "rot_bwd" + t)
        dproj = jnp.concatenate([dqkv, dgg, du, dgr, dgs], axis=1)
        dw_mm(s["h"], dproj, "in", i, "dw_in" + t)
        if i == 0:
            g_mix = g_mix + depart(3)
        dh = _matmul(dproj, wf["in"][i], "nt", [F32], name="dx_in" + t)[0]
        if i == 0:
            g_mix = g_mix + proceed(3, dh)
        dx, dxb, dg = _rms_bwd(s["x_in"], dh, dx1, g_mix, "rms_mix_bwd" + t)
        per_layer["ln_mix_g"][i] = dg[0]
        if i == DEPTH - 1:
            dxb = dxb + depart(0).astype(BF16)

    for n in per_layer:
        g_small[n] = jnp.stack(per_layer[n])
    me_arr = jnp.stack([2 * chip_idx + c_idx])
    flying_small, small_ssem, small_rsem, small_token = _small_gather_start(
        _place_slot(_flatten(g_small, _SMALL), me_arr))

    reduced = [None] * N_PIECES
    for g, pt, land, ssem, rsem in flights:
        landed = _scatter_wait(g, pt, land, ssem, rsem, small_token)
        for (p, layer), buf in zip(_REDUCE_GROUPS[g], landed):
            key = (_PIECES[p][1], layer)
            reduced[p] = _reduce_half(_PIECES[p], layer, dws[key][0], got[key], buf, reduced[p], idx_arr,
                                      "reduce_%s_l%d" % (_PIECES[p][0], layer))
    g_big = dict(zip([p[0] for p in _PIECES], _share_halves(reduced)))

    grads, delta, new_m, new_v = {}, {}, {}, {}
    for n in _BIG:
        d, r, cc = w[n].shape
        two_d = lambda a: a.reshape(d * r, cc)
        dl, mn, vn = _adamw(two_d(w[n]), two_d(g_big[n]), two_d(m[n]), two_d(v[n]), "adamw_" + n)
        grads[n], delta[n], new_m[n], new_v[n] = g_big[n], dl.reshape(d, r, cc), mn.reshape(d, r, cc), vn.reshape(d, r, cc)

    gathered = _small_gather_forward(_small_gather_wait(flying_small, small_ssem, small_rsem, delta[_BIG[-1]]))
    g_flat = _sum_slots(gathered, "sum_small")
    grads.update(_unflatten(g_flat, w, _SMALL))
    tiny_rows = _flat_rows(w, _TINY)
    dl, mn, vn = _adamw(_flatten(w, _TINY), g_flat[:tiny_rows], _flatten(m, _TINY), _flatten(v, _TINY), "adamw_tiny")
    for dst, flat in ((delta, dl), (new_m, mn), (new_v, vn)):
        dst.update(_unflatten(flat, w, _TINY))
    for n in _MID:
        delta[n], new_m[n], new_v[n] = _adamw_nd(w[n], grads[n], m[n], v[n], "adamw_" + n)
    return loss, dx, grads, delta, new_m, new_v


def kernel(x, ln_mix_g, w_in, ret_log_gamma, ssm_a_re, ssm_a_im, ssm_log_dt, ssm_b_re, ssm_b_im, ssm_c_re, ssm_c_im, ssm_d, w_glu, b_glu, w_out, ln_ffn_g, w_ffn_gate, w_ffn_up, w_ffn_down, ln_final_g, loss_target, m_ln_mix_g, m_w_in, m_ret_log_gamma, m_ssm_a_re, m_ssm_a_im, m_ssm_log_dt, m_ssm_b_re, m_ssm_b_im, m_ssm_c_re, m_ssm_c_im, m_ssm_d, m_w_glu, m_b_glu, m_w_out, m_ln_ffn_g, m_w_ffn_gate, m_w_ffn_up, m_w_ffn_down, m_ln_final_g, v_ln_mix_g, v_w_in, v_ret_log_gamma, v_ssm_a_re, v_ssm_a_im, v_ssm_log_dt, v_ssm_b_re, v_ssm_b_im, v_ssm_c_re, v_ssm_c_im, v_ssm_d, v_w_glu, v_b_glu, v_w_out, v_ln_ffn_g, v_w_ffn_gate, v_w_ffn_up, v_w_ffn_down, v_ln_final_g):
    given = dict(locals())
    w = {n: given[n] for n in _WEIGHTS}
    m = {n: given["m_" + n] for n in _WEIGHTS}
    v = {n: given["v_" + n] for n in _WEIGHTS}
    loss, dx, grads, delta, new_m, new_v = _step(w, m, v, x[0], loss_target[0])
    return (loss, dx[None], *[grads[n] for n in _WEIGHTS], *[delta[n] for n in _WEIGHTS],
            *[new_m[n] for n in _WEIGHTS], *[new_v[n] for n in _WEIGHTS])
```

```python
import functools
import math

import jax
import jax.numpy as jnp
from jax import lax
from jax.experimental import pallas as pl
from jax.experimental.pallas import tpu as pltpu

F32 = jnp.float32
BF16 = jnp.bfloat16

D_MODEL = 2048
DEPTH = 2
HEADS = 4
QK_DIM = 256
V_DIM = 512
QK_WIDTH = HEADS * QK_DIM
ROPE_BASE = 10000.0
GROUP = 16
N_GROUPS = D_MODEL // GROUP
N_STATE = 64
D_FF = 5632
IN_WIDTH = 2 * QK_WIDTH + 5 * D_MODEL
EPS = 1e-6
N_CHIPS = 4

ADAM_LR = 0.001
ADAM_B1 = 0.9
ADAM_B2 = 0.999
ADAM_EPS = 1e-08
ADAM_WD = 0.01
ADAM_STEP = 10

LANES = 128
SUBLANES = 8
VMEM_LIMIT = 56 * 1024 * 1024
SEGMENTS = SUBLANES
GROUPS_PER_TILE = LANES // GROUP
STATE_COLS = GROUPS_PER_TILE * N_STATE
N_TILES = D_MODEL // LANES
SCAN_UNROLL = 4

MESH = pl.DeviceIdType.MESH
HBM_SPEC = pl.BlockSpec(memory_space=pltpu.HBM)


def _params(sem=None, **kw):
    return pltpu.CompilerParams(dimension_semantics=sem, vmem_limit_bytes=VMEM_LIMIT, **kw)


def _tile(n, cap=1024):
    for t in (2048, 1024, 512, 256, 128, 64):
        if t <= cap and n % t == 0:
            return t
    raise ValueError(n)


def _rows_call(fn, rows, pars, row_outs, par_outs, *, tm, name):
    m = rows[0][0].shape[0]
    nr, npar, nro, npo = len(rows), len(pars), len(row_outs), len(par_outs)

    def body(*refs):
        rin = refs[:nr]
        pin = refs[nr:nr + npar]
        rout = refs[nr + npar:nr + npar + nro]
        pout = refs[nr + npar + nro:]
        res = fn(*[r[...] for r in rin], *[p[...] for p in pin])
        if not isinstance(res, (tuple, list)):
            res = (res,)
        for r, v in zip(rout, res[:nro]):
            r[...] = v.astype(r.dtype)
        if npo:
            @pl.when(pl.program_id(0) == 0)
            def _():
                for p in pout:
                    p[...] = jnp.zeros(p.shape, p.dtype)
            for p, v in zip(pout, res[nro:]):
                p[...] += v

    in_specs = [pl.BlockSpec((tm, w), functools.partial(lambda cb, i: (i, cb), cb)) for (_, w, cb) in rows]
    in_specs += [pl.BlockSpec(p.shape, lambda i: (0, 0)) for p in pars]
    out_specs = [pl.BlockSpec((tm, w), lambda i: (i, 0)) for (w, _) in row_outs]
    out_specs += [pl.BlockSpec(s, lambda i: (0, 0)) for s in par_outs]
    out_shape = [jax.ShapeDtypeStruct((m, w), dt) for (w, dt) in row_outs]
    out_shape += [jax.ShapeDtypeStruct(s, F32) for s in par_outs]
    res = pl.pallas_call(
        body, name=name, grid=(m // tm,), in_specs=in_specs, out_specs=out_specs, out_shape=out_shape,
        compiler_params=_params(("arbitrary",) if npo else ("parallel",)),
    )(*[a for (a, _, _) in rows], *pars)
    return res


def _f32(*vals):
    return [v.astype(F32) for v in vals]


def _f_rms(x, g):
    r = lax.rsqrt(jnp.mean(x * x, axis=-1, keepdims=True) + EPS)
    return x * r * g


def _rms_fwd(x, g, name):
    return _rows_call(lambda xv, gv: _f_rms(xv, gv), [(x, D_MODEL, 0)], [g], [(D_MODEL, BF16)], [],
                      tm=256, name=name)[0]


def _rms_bwd(x, dh, dres, g, name):
    def fn(xv, dhv, drv, gv):
        _, vjp = jax.vjp(_f_rms, xv, gv)
        dx, dg = vjp(dhv)
        dx = dx + drv
        return dx, dx, dg
    return _rows_call(fn, [(x, D_MODEL, 0), (dh, D_MODEL, 0), (dres, D_MODEL, 0)], [g],
                      [(D_MODEL, F32), (D_MODEL, BF16)], [(1, D_MODEL)], tm=256, name=name)


def _rot_heads(xv, cos, sin, scale):
    half = QK_DIM // 2
    outs = []
    for h in range(HEADS):
        x1 = xv[:, h * QK_DIM:h * QK_DIM + half]
        x2 = xv[:, h * QK_DIM + half:(h + 1) * QK_DIM]
        outs += [(x1 * cos - x2 * sin) * scale, (x1 * sin + x2 * cos) * scale]
    return jnp.concatenate(outs, axis=1)


def _rot_fwd(proj, cos, sin, name):
    def fn(q, k, cv, sv):
        return _rot_heads(q, cv, sv, 1.0), _rot_heads(k, cv, sv, QK_DIM ** -0.5)
    return _rows_call(fn, [(proj, QK_WIDTH, 0), (proj, QK_WIDTH, 1), (cos, LANES, 0), (sin, LANES, 0)], [],
                      [(QK_WIDTH, BF16), (QK_WIDTH, BF16)], [], tm=256, name=name)


def _rot_bwd(dqr, dkr, dv, cos, sin, name):
    def fn(dq, dk, dvv, cv, sv):
        return jnp.concatenate([_rot_heads(dq, cv, -sv, 1.0), _rot_heads(dk, cv, -sv, QK_DIM ** -0.5), dvv], axis=1)
    return _rows_call(fn, [(dqr, QK_WIDTH, 0), (dkr, QK_WIDTH, 0), (dv, D_MODEL, 0), (cos, LANES, 0), (sin, LANES, 0)],
                      [], [(2 * QK_WIDTH + D_MODEL, BF16)], [], tm=256, name=name)[0]


def _f_post1(y0, y1, y2, y3, g, gr, s5, u, dsk):
    yn = [yh * lax.rsqrt(jnp.mean(yh * yh, axis=-1, keepdims=True) + EPS) for yh in (y0, y1, y2, y3)]
    ret = jax.nn.sigmoid(gr) * (jax.nn.silu(g) * jnp.concatenate(yn, axis=1))
    ysg = jax.nn.gelu(s5 + dsk * u)
    return ret, ysg


def _post1_rows(y, proj, s5y):
    rows = [(y, V_DIM, h) for h in range(HEADS)]
    rows += [(proj, D_MODEL, 2), (proj, D_MODEL, 4), (s5y, D_MODEL, 0), (proj, D_MODEL, 3)]
    return rows


def _post1_fwd(y, proj, s5y, dsk, name):
    def fn(*vals):
        ret, ysg = _f_post1(*vals)
        return ret, ysg, ysg
    return _rows_call(fn, _post1_rows(y, proj, s5y), [dsk],
                      [(D_MODEL, F32), (D_MODEL, F32), (D_MODEL, BF16)], [], tm=128, name=name)


def _post1_bwd(y, proj, s5y, dret, dys, dsk, name):
    def fn(*vals):
        prim = vals[:8] + (vals[10],)
        _, vjp = jax.vjp(_f_post1, *prim)
        gy0, gy1, gy2, gy3, gg, ggr, gs5, gu, gd = vjp((vals[8], vals[9]))
        return jnp.concatenate([gy0, gy1, gy2, gy3], axis=1), gg, ggr, gs5, gu, gd
    rows = _post1_rows(y, proj, s5y) + [(dret, D_MODEL, 0), (dys, D_MODEL, 0)]
    return _rows_call(fn, rows, [dsk],
                      [(D_MODEL, BF16), (D_MODEL, BF16), (D_MODEL, BF16), (D_MODEL, F32), (D_MODEL, F32)],
                      [(1, D_MODEL)], tm=128, name=name)


def _f_merge(z, ysg, gs, ret, b):
    return ret + jax.nn.sigmoid(gs) * (ysg * jax.nn.sigmoid(z + b))


def _merge_fwd(z, ysg, proj, ret, b, name):
    return _rows_call(_f_merge, [(z, D_MODEL, 0), (ysg, D_MODEL, 0), (proj, D_MODEL, 5), (ret, D_MODEL, 0)], [b],
                      [(D_MODEL, BF16)], [], tm=128, name=name)[0]


def _merge_bwd(z, ysg, proj, ret, dm, b, name):
    def fn(zv, yv, gv, rv, dmv, bv):
        _, vjp = jax.vjp(_f_merge, zv, yv, gv, rv, bv)
        gz, gy, gg, _, gb = vjp(dmv)
        return gz, gy, gg, gb
    rows = [(z, D_MODEL, 0), (ysg, D_MODEL, 0), (proj, D_MODEL, 5), (ret, D_MODEL, 0), (dm, D_MODEL, 0)]
    return _rows_call(fn, rows, [b], [(D_MODEL, BF16), (D_MODEL, F32), (D_MODEL, BF16)], [(1, D_MODEL)],
                      tm=128, name=name)


def _f_glu(a, b):
    return jax.nn.silu(a) * b


def _glu_fwd(ab, name):
    return _rows_call(_f_glu, [(ab, D_FF, 0), (ab, D_FF, 1)], [], [(D_FF, BF16)], [], tm=128, name=name)[0]


def _glu_bwd(ab, df, name):
    def fn(a, b, d):
        _, vjp = jax.vjp(_f_glu, a, b)
        ga, gb = vjp(d)
        return jnp.concatenate([ga, gb], axis=1)
    return _rows_call(fn, [(ab, D_FF, 0), (ab, D_FF, 1), (df, D_FF, 0)], [], [(2 * D_FF, BF16)], [],
                      tm=128, name=name)[0]


def _loss_stage(x, tgt, g, name):
    def fn(xv, tv, gv):
        def lf(xx, gg):
            err = _f_rms(xx, gg) - tv
            row = jnp.mean(err * err, axis=-1, keepdims=True)
            return 0.5 * jnp.sum(row, axis=0, keepdims=True)
        l, vjp = jax.vjp(lf, xv, gv)
        dx, dg = vjp(jnp.ones((1, 1), F32))
        return dx, dx, jnp.broadcast_to(l, (1, LANES)), dg
    return _rows_call(fn, [(x, D_MODEL, 0), (tgt, D_MODEL, 0)], [g], [(D_MODEL, F32), (D_MODEL, BF16)],
                      [(1, LANES), (1, D_MODEL)], tm=256, name=name)


def _adam_math(wv, gv, mv, vv):
    mn = ADAM_B1 * mv + (1.0 - ADAM_B1) * gv
    vn = ADAM_B2 * vv + (1.0 - ADAM_B2) * (gv * gv)
    m_hat = mn / (1.0 - ADAM_B1 ** ADAM_STEP)
    v_hat = vn / (1.0 - ADAM_B2 ** ADAM_STEP)
    delta = -ADAM_LR * (m_hat / (jnp.sqrt(v_hat) + ADAM_EPS) + ADAM_WD * wv)
    return delta, mn, vn


def _adamw(w, g, m, v, name):
    rows, cols = w.shape
    tm = _tile(rows, 128 if cols > D_FF // N_CHIPS else (256 if cols > LANES else 512))
    return _rows_call(_adam_math, [(w, cols, 0), (g, cols, 0), (m, cols, 0), (v, cols, 0)], [],
                      [(cols, F32)] * 3, [], tm=tm, name=name)


def _adamw_nd(w, g, m, v, name):
    shape = w.shape
    lead = math.prod(shape[:-2])
    blk = (lead // 8,) + shape[-2:]
    three_d = lambda a: a.reshape((lead,) + shape[-2:])

    def body(w_ref, g_ref, m_ref, v_ref, d_ref, mn_ref, vn_ref):
        d_ref[...], mn_ref[...], vn_ref[...] = _adam_math(w_ref[...], g_ref[...], m_ref[...], v_ref[...])

    spec = pl.BlockSpec(blk, lambda i: (i, 0, 0))
    res = pl.pallas_call(
        body, name=name, grid=(8,), in_specs=[spec] * 4, out_specs=[spec] * 3,
        out_shape=[jax.ShapeDtypeStruct((lead,) + shape[-2:], F32)] * 3,
        compiler_params=_params(("parallel",)),
    )(three_d(w), three_d(g), three_d(m), three_d(v))
    return [r.reshape(shape) for r in res]


MATMUL_VMEM_BUDGET = 44 * 1024 * 1024


def _matmul_tiles(m, n, k, out_bytes, has_add):
    if k > 2048:
        return _tile(m, 1024), _tile(n, 1024), _tile(k, 1024)
    tm, tn, tk = _tile(m, 2048), _tile(n, 1024), k

    def footprint():
        acc = 4 * tm * tn if k // tk > 1 else 0
        return 2 * 2 * (tm * tk + tk * tn) + 2 * (out_bytes + 4 * has_add) * tm * tn + acc

    while footprint() > MATMUL_VMEM_BUDGET:
        if tn > 512 and n % (tn // 2) == 0:
            tn //= 2
        elif tk > 512 and k % (tk // 2) == 0:
            tk //= 2
        else:
            tm //= 2
    return tm, tn, tk


def _matmul(a, b, mode, out_dtypes, *, name, add=None):
    if mode == "nn":
        (m, k), (_, n) = a.shape, b.shape
    elif mode == "nt":
        (m, k), (n, _) = a.shape, b.shape
    else:
        (k, m), (_, n) = a.shape, b.shape
    n_out = len(out_dtypes)
    has_add = add is not None
    tm, tn, tk = _matmul_tiles(m, n, k, sum(jnp.dtype(dt).itemsize for dt in out_dtypes), has_add)
    nk = k // tk
    if mode == "nn":
        a_spec = pl.BlockSpec((tm, tk), lambda i, j, kk: (i, kk))
        b_spec = pl.BlockSpec((tk, tn), lambda i, j, kk: (kk, j))
        dims = (((1,), (0,)), ((), ()))
    elif mode == "nt":
        a_spec = pl.BlockSpec((tm, tk), lambda i, j, kk: (i, kk))
        b_spec = pl.BlockSpec((tn, tk), lambda i, j, kk: (j, kk))
        dims = (((1,), (1,)), ((), ()))
    else:
        a_spec = pl.BlockSpec((tk, tm), lambda i, j, kk: (kk, i))
        b_spec = pl.BlockSpec((tk, tn), lambda i, j, kk: (kk, j))
        dims = (((0,), (0,)), ((), ()))

    def body(*refs):
        a_ref, b_ref = refs[0], refs[1]
        add_ref = refs[2] if has_add else None
        outs = refs[2 + has_add:2 + has_add + n_out]

        def finish(r):
            if has_add:
                r = r + add_ref[...]
            for o in outs:
                o[...] = r.astype(o.dtype)

        if nk == 1:
            finish(lax.dot_general(a_ref[...], b_ref[...], dims, preferred_element_type=F32))
            return
        acc = refs[-1]
        kk = pl.program_id(2)

        @pl.when(kk == 0)
        def _():
            acc[...] = jnp.zeros(acc.shape, F32)

        acc[...] += lax.dot_general(a_ref[...], b_ref[...], dims, preferred_element_type=F32)

        @pl.when(kk == nk - 1)
        def _():
            finish(acc[...])

    in_specs = [a_spec, b_spec]
    args = [a, b]
    if has_add:
        in_specs.append(pl.BlockSpec((tm, tn), lambda i, j, kk: (i, j)))
        args.append(add)
    return pl.pallas_call(
        body, name=name, grid=(m // tm, n // tn, nk), in_specs=in_specs,
        out_specs=[pl.BlockSpec((tm, tn), lambda i, j, kk: (i, j))] * n_out,
        out_shape=[jax.ShapeDtypeStruct((m, n), dt) for dt in out_dtypes],
        scratch_shapes=[pltpu.VMEM((tm, tn), F32)] if nk > 1 else [],
        compiler_params=_params(("parallel", "parallel", "arbitrary")),
    )(*args)


RET_TQ = 512


def _decay(lg_ref, i, tq, seq):
    n_idx = i * tq + lax.broadcasted_iota(jnp.int32, (tq, seq), 0)
    m_idx = lax.broadcasted_iota(jnp.int32, (tq, seq), 1)
    diff = (n_idx - m_idx).astype(F32)
    lgf = lg_ref[0, 0:1, 0:1]
    lgb = lg_ref[0, 1:2, 0:1]
    causal = diff >= 0
    return jnp.exp(jnp.where(causal, lgf * diff, -lgb * diff)), diff, causal


_NT = (((1,), (1,)), ((), ()))
_TN = (((0,), (0,)), ((), ()))


def _ret_fwd(qr, kr, proj, lg, name):
    seq = qr.shape[0]
    tq = RET_TQ
    v_blk0 = (2 * QK_WIDTH) // V_DIM

    def body(q_ref, k_ref, v_ref, lg_ref, y_ref):
        i = pl.program_id(1)
        s = lax.dot_general(q_ref[...], k_ref[...], _NT, preferred_element_type=F32)
        dm, _, _ = _decay(lg_ref, i, tq, seq)
        p = (s * dm).astype(BF16)
        y_ref[...] = jnp.dot(p, v_ref[...].astype(BF16), preferred_element_type=F32)

    return pl.pallas_call(
        body, name=name, grid=(HEADS, seq // tq),
        in_specs=[pl.BlockSpec((tq, QK_DIM), lambda h, i: (i, h)),
                  pl.BlockSpec((seq, QK_DIM), lambda h, i: (0, h)),
                  pl.BlockSpec((seq, V_DIM), lambda h, i: (0, v_blk0 + h)),
                  pl.BlockSpec((1, 2, LANES), lambda h, i: (h, 0, 0))],
        out_specs=pl.BlockSpec((tq, V_DIM), lambda h, i: (i, h)),
        out_shape=jax.ShapeDtypeStruct((seq, HEADS * V_DIM), F32),
        compiler_params=_params(("parallel", "parallel")),
    )(qr, kr, proj, lg)


def _ret_bwd(qr, kr, proj, dy, lg, name):
    seq = qr.shape[0]
    tq = RET_TQ
    v_blk0 = (2 * QK_WIDTH) // V_DIM

    def body(q_ref, k_ref, v_ref, dy_ref, lg_ref, dq_ref, dk_ref, dv_ref, dlg_ref):
        i = pl.program_id(1)

        @pl.when(i == 0)
        def _():
            dk_ref[...] = jnp.zeros(dk_ref.shape, F32)
            dv_ref[...] = jnp.zeros(dv_ref.shape, F32)
            dlg_ref[...] = jnp.zeros(dlg_ref.shape, F32)

        q = q_ref[...]
        k = k_ref[...]
        vb = v_ref[...].astype(BF16)
        dyb = dy_ref[...]
        s = lax.dot_general(q, k, _NT, preferred_element_type=F32)
        dm, diff, causal = _decay(lg_ref, i, tq, seq)
        p = s * dm
        dp = lax.dot_general(dyb, vb, _NT, preferred_element_type=F32)
        dv_ref[...] += lax.dot_general(p.astype(BF16), dyb, _TN, preferred_element_type=F32)
        ds = (dp * dm).astype(BF16)
        dq_ref[...] = jnp.dot(ds, k, preferred_element_type=F32)
        dk_ref[...] += lax.dot_general(ds, q, _TN, preferred_element_type=F32)
        gd = dp * p * diff
        dlf = jnp.sum(jnp.sum(jnp.where(causal, gd, 0.0), axis=1, keepdims=True), axis=0, keepdims=True)
        dlb = jnp.sum(jnp.sum(jnp.where(causal, 0.0, -gd), axis=1, keepdims=True), axis=0, keepdims=True)
        row = lax.broadcasted_iota(jnp.int32, (2, LANES), 0)
        dlg_ref[0] += jnp.where(row == 0, dlf, dlb)

    return pl.pallas_call(
        body, name=name, grid=(HEADS, seq // tq),
        in_specs=[pl.BlockSpec((tq, QK_DIM), lambda h, i: (i, h)),
                  pl.BlockSpec((seq, QK_DIM), lambda h, i: (0, h)),
                  pl.BlockSpec((seq, V_DIM), lambda h, i: (0, v_blk0 + h)),
                  pl.BlockSpec((tq, V_DIM), lambda h, i: (i, h)),
                  pl.BlockSpec((1, 2, LANES), lambda h, i: (h, 0, 0))],
        out_specs=[pl.BlockSpec((tq, QK_DIM), lambda h, i: (i, h)),
                   pl.BlockSpec((seq, QK_DIM), lambda h, i: (0, h)),
                   pl.BlockSpec((seq, V_DIM), lambda h, i: (0, h)),
                   pl.BlockSpec((1, 2, LANES), lambda h, i: (h, 0, 0))],
        out_shape=[jax.ShapeDtypeStruct((seq, QK_WIDTH), F32), jax.ShapeDtypeStruct((seq, QK_WIDTH), F32),
                   jax.ShapeDtypeStruct((seq, HEADS * V_DIM), F32), jax.ShapeDtypeStruct((HEADS, 2, LANES), F32)],
        compiler_params=_params(("parallel", "arbitrary")),
    )(qr, kr, proj, dy, lg)


def _shift_rows(v, reverse):
    row = lax.broadcasted_iota(jnp.int32, v.shape, 0)
    if reverse:
        return jnp.where(row == SEGMENTS - 1, 0.0, pltpu.roll(v, SEGMENTS - 1, 0))
    return jnp.where(row == 0, 0.0, pltpu.roll(v, 1, 0))


def _slab(t):
    if isinstance(t, int):
        return pl.ds(t * SEGMENTS, SEGMENTS)
    return pl.ds(pl.multiple_of(t * SEGMENTS, SEGMENTS), SEGMENTS)


def _unrolled_loop(body, lo, hi, init):
    main = (hi - lo) // SCAN_UNROLL

    def unrolled(g, carry):
        for k in range(SCAN_UNROLL):
            carry = body(lo + g * SCAN_UNROLL + k, carry)
        return carry

    carry = lax.fori_loop(0, main, unrolled, init)
    for t in range(lo + main * SCAN_UNROLL, hi):
        carry = body(t, carry)
    return carry


def _scan(xr_ref, xi_ref, lam, reverse, conj):
    steps = xr_ref.shape[0] // SEGMENTS
    cols = xr_ref.shape[1]
    lr = jnp.broadcast_to(lam[0], (SEGMENTS, cols))
    li = jnp.broadcast_to(lam[1], (SEGMENTS, cols))
    lrt = jnp.broadcast_to(lam[2], (SEGMENTS, cols))
    lit = jnp.broadcast_to(lam[3], (SEGMENTS, cols))
    if conj:
        li, lit = -li, -lit
    zero = jnp.zeros((SEGMENTS, cols), F32)

    def rows_of(t):
        return _slab(steps - 1 - t if reverse else t)

    def advance(t, carry):
        sr, si = carry
        rows = rows_of(t)
        return lr * sr - li * si + xr_ref[rows, :], lr * si + li * sr + xi_ref[rows, :]

    def step(t, carry):
        nr, ni = advance(t, carry)
        rows = rows_of(t)
        xr_ref[rows, :] = nr
        xi_ref[rows, :] = ni
        return nr, ni

    def run(body, init):
        return _unrolled_loop(body, 0, steps, init)

    er, ei = run(advance, (zero, zero))
    cr, ci = zero, zero
    for _ in range(SEGMENTS - 1):
        tr = er + lrt * cr - lit * ci
        ti = ei + lrt * ci + lit * cr
        cr, ci = _shift_rows(tr, reverse), _shift_rows(ti, reverse)
    run(step, (cr, ci))


def _permute_in(dst_ref, src_ref):
    steps = src_ref.shape[0] // SEGMENTS
    for s in range(SEGMENTS):
        dst_ref[pl.ds(s, steps, stride=SEGMENTS), :] = src_ref[s * steps:(s + 1) * steps, :].astype(dst_ref.dtype)


def _unpermute(src_ref, s):
    steps = src_ref.shape[0] // SEGMENTS
    return src_ref[pl.ds(s, steps, stride=SEGMENTS), :]


def _s5_fwd(proj, bblk, cblk, lam, name):
    seq = proj.shape[0]
    u_blk0 = (2 * QK_WIDTH + 2 * D_MODEL) // LANES
    sc = STATE_COLS

    def body(u_ref, b_ref, c_ref, lam_ref, y_ref, up_ref, yp_ref, xr_ref, xi_ref):
        _permute_in(up_ref, u_ref)
        ub = up_ref[...].astype(BF16)
        for d in range(2):
            xr_ref[...] = jnp.dot(ub, b_ref[d, :, 0:sc], preferred_element_type=F32)
            xi_ref[...] = jnp.dot(ub, b_ref[d, :, sc:2 * sc], preferred_element_type=F32)
            lm = [lam_ref[d, r:r + 1, :] for r in range(4)]
            _scan(xr_ref, xi_ref, lm, reverse=(d == 1), conj=False)
            yd = (jnp.dot(xr_ref[...].astype(BF16), c_ref[d, 0:sc, :], preferred_element_type=F32)
                  + jnp.dot(xi_ref[...].astype(BF16), c_ref[d, sc:2 * sc, :], preferred_element_type=F32))
            if d == 0:
                yp_ref[...] = yd
            else:
                yp_ref[...] += yd
        steps = seq // SEGMENTS
        for s in range(SEGMENTS):
            y_ref[s * steps:(s + 1) * steps, :] = _unpermute(yp_ref, s)

    return pl.pallas_call(
        body, name=name, grid=(N_TILES,),
        in_specs=[pl.BlockSpec((seq, LANES), lambda j: (0, u_blk0 + j)),
                  pl.BlockSpec((2, None, LANES, 2 * sc), lambda j: (0, j, 0, 0)),
                  pl.BlockSpec((2, None, 2 * sc, LANES), lambda j: (0, j, 0, 0)),
                  pl.BlockSpec((2, None, 4, sc), lambda j: (0, j, 0, 0))],
        out_specs=pl.BlockSpec((seq, LANES), lambda j: (0, j)),
        out_shape=jax.ShapeDtypeStruct((seq, D_MODEL), F32),
        scratch_shapes=[pltpu.VMEM((seq, LANES), F32), pltpu.VMEM((seq, LANES), F32),
                        pltpu.VMEM((seq, sc), F32), pltpu.VMEM((seq, sc), F32)],
        compiler_params=_params(("parallel",)),
    )(proj, bblk, cblk, lam)


def _s5_bwd(proj, dy, du_part, bblk, cblk, lam, name):
    seq = proj.shape[0]
    u_blk0 = (2 * QK_WIDTH + 2 * D_MODEL) // LANES
    sc = STATE_COLS
    steps = seq // SEGMENTS

    def body(u_ref, dy_ref, dup_ref, b_ref, c_ref, lam_ref, du_ref, db_ref, dc_ref, dlam_ref,
             up_ref, dyp_ref, dua_ref, xr_ref, xi_ref, gr_ref, gi_ref):
        _permute_in(up_ref, u_ref)
        _permute_in(dyp_ref, dy_ref)
        ub = up_ref[...].astype(BF16)
        dyb = dyp_ref[...].astype(BF16)
        ubt = up_ref[...].T.astype(BF16)
        dybt = dyp_ref[...].T.astype(BF16)
        for d in range(2):
            reverse = d == 1
            xr_ref[...] = jnp.dot(ub, b_ref[d, :, 0:sc], preferred_element_type=F32)
            xi_ref[...] = jnp.dot(ub, b_ref[d, :, sc:2 * sc], preferred_element_type=F32)
            lm = [lam_ref[d, r:r + 1, :] for r in range(4)]
            _scan(xr_ref, xi_ref, lm, reverse=reverse, conj=False)
            xrb = xr_ref[...].astype(BF16)
            xib = xi_ref[...].astype(BF16)
            dc_ref[d, :, 0:sc] = jnp.dot(dybt, xrb, preferred_element_type=F32)
            dc_ref[d, :, sc:2 * sc] = jnp.dot(dybt, xib, preferred_element_type=F32)
            gr_ref[...] = lax.dot_general(dyb, c_ref[d, 0:sc, :], _NT, preferred_element_type=F32)
            gi_ref[...] = lax.dot_general(dyb, c_ref[d, sc:2 * sc, :], _NT, preferred_element_type=F32)
            _scan(gr_ref, gi_ref, lm, reverse=not reverse, conj=True)

            def acc_step(t, carry):
                ar, ai = carry
                prev = _slab(t + 1 if reverse else t - 1)
                pr = xr_ref[prev, :]
                pi = xi_ref[prev, :]
                zr = gr_ref[_slab(t), :]
                zi = gi_ref[_slab(t), :]
                return ar + zr * pr + zi * pi, ai + zi * pr - zr * pi

            zero = jnp.zeros((SEGMENTS, sc), F32)
            if reverse:
                ar, ai = _unrolled_loop(acc_step, 0, steps - 1, (zero, zero))
                edge = _slab(steps - 1)
                pr = _shift_rows(xr_ref[_slab(0), :], True)
                pi = _shift_rows(xi_ref[_slab(0), :], True)
            else:
                ar, ai = _unrolled_loop(acc_step, 1, steps, (zero, zero))
                edge = _slab(0)
                pr = _shift_rows(xr_ref[_slab(steps - 1), :], False)
                pi = _shift_rows(xi_ref[_slab(steps - 1), :], False)
            zr = gr_ref[edge, :]
            zi = gi_ref[edge, :]
            ar = ar + zr * pr + zi * pi
            ai = ai + zi * pr - zr * pi
            dlam_ref[d, 0:1, :] = jnp.sum(ar, axis=0, keepdims=True)
            dlam_ref[d, 1:2, :] = jnp.sum(ai, axis=0, keepdims=True)

            grb = gr_ref[...].astype(BF16)
            gib = gi_ref[...].astype(BF16)
            db_ref[d, :, 0:sc] = jnp.dot(ubt, grb, preferred_element_type=F32)
            db_ref[d, :, sc:2 * sc] = jnp.dot(ubt, gib, preferred_element_type=F32)
            dud = (lax.dot_general(grb, b_ref[d, :, 0:sc], _NT, preferred_element_type=F32)
                   + lax.dot_general(gib, b_ref[d, :, sc:2 * sc], _NT, preferred_element_type=F32))
            if d == 0:
                dua_ref[...] = dud
            else:
                dua_ref[...] += dud
        for s in range(SEGMENTS):
            rows = slice(s * steps, (s + 1) * steps)
            du_ref[rows, :] = (_unpermute(dua_ref, s) + dup_ref[rows, :]).astype(du_ref.dtype)

    return pl.pallas_call(
        body, name=name, grid=(N_TILES,),
        in_specs=[pl.BlockSpec((seq, LANES), lambda j: (0, u_blk0 + j)),
                  pl.BlockSpec((seq, LANES), lambda j: (0, j)),
                  pl.BlockSpec((seq, LANES), lambda j: (0, j)),
                  pl.BlockSpec((2, None, LANES, 2 * sc), lambda j: (0, j, 0, 0)),
                  pl.BlockSpec((2, None, 2 * sc, LANES), lambda j: (0, j, 0, 0)),
                  pl.BlockSpec((2, None, 4, sc), lambda j: (0, j, 0, 0))],
        out_specs=[pl.BlockSpec((seq, LANES), lambda j: (0, j)),
                   pl.BlockSpec((2, None, LANES, 2 * sc), lambda j: (0, j, 0, 0)),
                   pl.BlockSpec((2, None, LANES, 2 * sc), lambda j: (0, j, 0, 0)),
                   pl.BlockSpec((2, None, 2, sc), lambda j: (0, j, 0, 0))],
        out_shape=[jax.ShapeDtypeStruct((seq, D_MODEL), BF16),
                   jax.ShapeDtypeStruct((2, N_TILES, LANES, 2 * sc), F32),
                   jax.ShapeDtypeStruct((2, N_TILES, LANES, 2 * sc), F32),
                   jax.ShapeDtypeStruct((2, N_TILES, 2, sc), F32)],
        scratch_shapes=[pltpu.VMEM((seq, LANES), F32), pltpu.VMEM((seq, LANES), F32), pltpu.VMEM((seq, LANES), F32),
                        pltpu.VMEM((seq, sc), F32), pltpu.VMEM((seq, sc), F32),
                        pltpu.VMEM((seq, sc), F32), pltpu.VMEM((seq, sc), F32)],
        compiler_params=_params(("parallel",)),
    )(proj, dy, du_part, bblk, cblk, lam)


def _s5_discretize(a_re, a_im, log_dt, b_re, b_im, seg_len):
    dt = jnp.exp(log_dt)[..., None]
    e = jnp.exp(a_re * dt)
    lr, li = e * jnp.cos(a_im * dt), e * jnp.sin(a_im * dt)
    et = jnp.exp(a_re * dt * seg_len)
    lrt, lit = et * jnp.cos(a_im * dt * seg_len), et * jnp.sin(a_im * dt * seg_len)
    den = a_re * a_re + a_im * a_im
    qr = ((lr - 1.0) * a_re + li * a_im) / den
    qi = (li * a_re - (lr - 1.0) * a_im) / den
    br = qr[..., None] * b_re - qi[..., None] * b_im
    bi = qr[..., None] * b_im + qi[..., None] * b_re
    return lr, li, lrt, lit, br, bi


def _s5_pack(lr, li, lrt, lit, br, bi, c_re, c_im):
    eye = jnp.eye(GROUPS_PER_TILE, dtype=F32)

    def bd_b(b):
        b5 = b.reshape(2, N_TILES, GROUPS_PER_TILE, N_STATE, GROUP)
        return jnp.einsum("dtgph,gk->dtghkp", b5, eye).reshape(2, N_TILES, LANES, STATE_COLS)

    def bd_c(c):
        c5 = c.reshape(2, N_TILES, GROUPS_PER_TILE, GROUP, N_STATE)
        return jnp.einsum("dtghp,gk->dtkpgh", c5, eye).reshape(2, N_TILES, STATE_COLS, LANES)

    bblk = jnp.concatenate([bd_b(br), bd_b(bi)], axis=3)
    cblk = jnp.concatenate([bd_c(c_re), -bd_c(c_im)], axis=2)
    lam = jnp.stack([v.reshape(2, N_TILES, STATE_COLS) for v in (lr, li, lrt, lit)], axis=2)
    return bblk, cblk, lam


def _s5_unpack(dbblk, dcblk, dlam):
    eye = jnp.eye(GROUPS_PER_TILE, dtype=F32)

    def diag_b(d):
        d6 = d.reshape(2, N_TILES, GROUPS_PER_TILE, GROUP, GROUPS_PER_TILE, N_STATE)
        return jnp.einsum("dtghkp,gk->dtgph", d6, eye).reshape(2, N_GROUPS, N_STATE, GROUP)

    def diag_c(d):
        d6 = d.reshape(2, N_TILES, GROUPS_PER_TILE, GROUP, GROUPS_PER_TILE, N_STATE)
        return jnp.einsum("dtghkp,gk->dtghp", d6, eye).reshape(2, N_GROUPS, GROUP, N_STATE)

    dbr, dbi = diag_b(dbblk[..., :STATE_COLS]), diag_b(dbblk[..., STATE_COLS:])
    dcr, dci = diag_c(dcblk[..., :STATE_COLS]), -diag_c(dcblk[..., STATE_COLS:])
    dlr = dlam[:, :, 0, :].reshape(2, N_GROUPS, N_STATE)
    dli = dlam[:, :, 1, :].reshape(2, N_GROUPS, N_STATE)
    return dlr, dli, dbr, dbi, dcr, dci


def _pos():
    return lax.axis_index("x"), lax.axis_index("y"), lax.axis_index("c")


def _remote(src, dst, ssem, rsem, dev):
    return pltpu.make_async_remote_copy(src_ref=src, dst_ref=dst, send_sem=ssem, recv_sem=rsem,
                                        device_id=dev, device_id_type=MESH)


_PIECES = (
    ("w_in", "in", D_MODEL, IN_WIDTH // N_CHIPS, 0, IN_WIDTH // N_CHIPS, 0),
    ("w_glu", "glu", D_MODEL // N_CHIPS, D_MODEL, D_MODEL // N_CHIPS, 0, 0),
    ("w_out", "out", D_MODEL // N_CHIPS, D_MODEL, D_MODEL // N_CHIPS, 0, 0),
    ("w_ffn_gate", "gu", D_MODEL, D_FF // N_CHIPS, 0, D_FF // N_CHIPS, 0),
    ("w_ffn_up", "gu", D_MODEL, D_FF // N_CHIPS, 0, D_FF // N_CHIPS, D_FF),
    ("w_ffn_down", "down", D_FF // N_CHIPS, D_MODEL, D_FF // N_CHIPS, 0, 0),
)
_BUFFERS = (("in", D_MODEL, IN_WIDTH), ("glu", D_MODEL, D_MODEL), ("out", D_MODEL, D_MODEL),
            ("gu", D_MODEL, 2 * D_FF), ("down", D_FF, D_MODEL))
_BUF_INDEX = {name: t for t, (name, _, _) in enumerate(_BUFFERS)}
N_PIECES = len(_PIECES)
N_BUFFERS = len(_BUFFERS)


def _own_block(piece, tm):
    _, _, _, cs, rstep, cstep, coff = piece
    return lambda i, chip: (i + chip * (rstep // tm), coff // cs + chip * (cstep // cs))


def _cast_place(piece, w3, layer, prev, chip_arr, name):
    _, r, cc = w3.shape
    _, rf, cf = _BUFFERS[_BUF_INDEX[piece[1]]]
    tm = _tile(r, 256)
    own = _own_block(piece, tm)

    def body(s_ref, w_ref, *rest):
        rest[-1][...] = w_ref[...].astype(BF16)

    in_specs = [pl.BlockSpec((None, tm, cc), lambda i, s: (layer, i, 0))]
    args = [w3]
    aliases = {}
    if prev is not None:
        in_specs.append(pl.BlockSpec(memory_space=pl.ANY))
        args.append(prev)
        aliases = {2: 0}
    return pl.pallas_call(
        body, name=name,
        grid_spec=pltpu.PrefetchScalarGridSpec(
            num_scalar_prefetch=1, grid=(r // tm,), in_specs=in_specs,
            out_specs=pl.BlockSpec((tm, cc), lambda i, s: own(i, s[0]))),
        out_shape=jax.ShapeDtypeStruct((rf, cf), BF16), input_output_aliases=aliases,
        compiler_params=_params(("parallel",)),
    )(chip_arr, *args)


_GATHER_GROUPS = ((0, (0,)), (0, (1, 2, 3, 4, 5)), (1, (0,)), (1, (1, 2, 3, 4, 5)))
_SPLIT_EFFECT = pltpu.SideEffectType.DATAFLOW_SIDE_EFFECTING
SEM_SPEC = pl.BlockSpec(memory_space=pltpu.SEMAPHORE)
BF16_ROWS = 2 * SUBLANES


def _group_keys(g):
    layer, pieces = _GATHER_GROUPS[g]
    keys = []
    for p in pieces:
        if (_PIECES[p][1], layer) not in keys:
            keys.append((_PIECES[p][1], layer))
    return keys


def _half_view(ref, piece, j, c):
    _, _, rs, cs, rstep, cstep, coff = piece
    half = rs // 2
    return ref.at[pl.ds(pl.multiple_of(j * rstep + c * half, BF16_ROWS), half), pl.ds(coff + j * cstep, cs)]


def _for_my_chip(fn):
    x, y, _ = _pos()
    for mine in range(N_CHIPS):
        pl.when(2 * x + y == mine)(functools.partial(fn, mine, [j for j in range(N_CHIPS) if j != mine]))


def _gather_start(placed):
    keys = [k for g in range(len(_GATHER_GROUPS)) for k in _group_keys(g)]
    nb, ng = len(keys), len(_GATHER_GROUPS)

    def body(*refs):
        bufs = dict(zip(keys, refs[nb:2 * nb]))
        ssems = refs[2 * nb:2 * nb + ng]
        rsems = refs[2 * nb + ng:2 * nb + 2 * ng]
        token = refs[2 * nb + 2 * ng]
        _, _, c = _pos()

        def send(mine, others):
            for g, (layer, pieces) in enumerate(_GATHER_GROUPS):
                for k, p in enumerate(pieces):
                    view = _half_view(bufs[(_PIECES[p][1], layer)], _PIECES[p], mine, c)
                    for j in others:
                        _remote(view, view, ssems[g].at[k * N_CHIPS + j], rsems[g].at[k * N_CHIPS + mine],
                                (j // 2, j % 2, c)).start()

        _for_my_chip(send)
        token[...] = jnp.zeros(token.shape, token.dtype)

    sems = [pltpu.SemaphoreType.DMA((N_CHIPS * len(pieces),)) for _, pieces in _GATHER_GROUPS]
    shapes = [jax.ShapeDtypeStruct(a.shape, a.dtype) for a in placed]
    res = pl.pallas_call(
        body, name="gather_start",
        in_specs=[HBM_SPEC] * nb,
        out_specs=[HBM_SPEC] * nb + [SEM_SPEC] * (2 * ng) + [pl.BlockSpec(memory_space=pltpu.VMEM)],
        out_shape=shapes + sems + sems + [jax.ShapeDtypeStruct((SUBLANES, LANES), F32)],
        input_output_aliases={t: t for t in range(nb)},
        compiler_params=_params(has_side_effects=_SPLIT_EFFECT),
    )(*[pltpu.with_memory_space_constraint(a, pltpu.HBM) for a in placed])
    return dict(zip(keys, res[:nb])), res[nb:nb + ng], res[nb + ng:nb + 2 * ng], res[nb + 2 * ng]


def _gather_wait(g, bufs, ssem, rsem, after):
    layer, pieces = _GATHER_GROUPS[g]
    keys = _group_keys(g)
    nb = len(keys)

    def body(*refs):
        ssem_ref, rsem_ref = refs[nb], refs[nb + 1]
        land = dict(zip(keys, refs[nb + 3:]))
        _, _, c = _pos()

        def wait(mine, others):
            for k, p in enumerate(pieces):
                ref = land[(_PIECES[p][1], layer)]
                for j in others:
                    cp = _remote(_half_view(ref, _PIECES[p], mine, c), _half_view(ref, _PIECES[p], j, c),
                                 ssem_ref.at[k * N_CHIPS + j], rsem_ref.at[k * N_CHIPS + j], (j // 2, j % 2, c))
                    cp.wait_send()
                    cp.wait_recv()

        _for_my_chip(wait)

    return pl.pallas_call(
        body, name="gather_wait_g%d" % g,
        in_specs=[HBM_SPEC] * nb + [SEM_SPEC, SEM_SPEC, pl.BlockSpec(memory_space=pl.ANY)],
        out_specs=[HBM_SPEC] * nb,
        out_shape=[jax.ShapeDtypeStruct(a.shape, a.dtype) for a in bufs],
        input_output_aliases={t: t for t in range(nb)},
        compiler_params=_params(has_side_effects=_SPLIT_EFFECT),
    )(*bufs, ssem, rsem, after)


def _gather_forward(g, bufs):
    layer, pieces = _GATHER_GROUPS[g]
    keys = _group_keys(g)
    nb = len(keys)

    def body(*refs):
        land = dict(zip(keys, refs[nb:2 * nb]))
        ssem, rsem = refs[2 * nb:]
        x, y, c = _pos()

        def forward(mine, others):
            cps = []
            for k, p in enumerate(pieces):
                ref = land[(_PIECES[p][1], layer)]
                for j in others:
                    view = _half_view(ref, _PIECES[p], j, c)
                    cp = _remote(view, view, ssem.at[k * N_CHIPS + j], rsem.at[k * N_CHIPS + j], (x, y, 1 - c))
                    cp.start()
                    cps.append(cp)
            for k, p in enumerate(pieces):
                ref = land[(_PIECES[p][1], layer)]
                for j in others:
                    view = _half_view(ref, _PIECES[p], j, 1 - c)
                    _remote(view, view, ssem.at[k * N_CHIPS + j], rsem.at[k * N_CHIPS + j], (x, y, 1 - c)).wait_recv()
            for cp in cps:
                cp.wait_send()

        _for_my_chip(forward)

    nsem = N_CHIPS * len(pieces)
    return pl.pallas_call(
        body, name="gather_forward_g%d" % g,
        in_specs=[HBM_SPEC] * nb, out_specs=[HBM_SPEC] * nb,
        out_shape=[jax.ShapeDtypeStruct(a.shape, a.dtype) for a in bufs],
        input_output_aliases={t: t for t in range(nb)},
        scratch_shapes=[pltpu.SemaphoreType.DMA((nsem,)), pltpu.SemaphoreType.DMA((nsem,))],
        compiler_params=_params(has_side_effects=True),
    )(*bufs)


def _forward_start(g, bufs):
    layer, pieces = _GATHER_GROUPS[g]
    keys = _group_keys(g)
    nb = len(keys)

    def body(*refs):
        land = dict(zip(keys, refs[nb:2 * nb]))
        ssem, rsem, token = refs[2 * nb:]
        x, y, c = _pos()

        def forward(mine, others):
            for k, p in enumerate(pieces):
                for j in others:
                    view = _half_view(land[(_PIECES[p][1], layer)], _PIECES[p], j, c)
                    _remote(view, view, ssem.at[k * N_CHIPS + j], rsem.at[k * N_CHIPS + j], (x, y, 1 - c)).start()

        _for_my_chip(forward)
        token[...] = jnp.zeros(token.shape, token.dtype)

    sem = pltpu.SemaphoreType.DMA((N_CHIPS * len(pieces),))
    res = pl.pallas_call(
        body, name="forward_start_g%d" % g,
        in_specs=[HBM_SPEC] * nb,
        out_specs=[HBM_SPEC] * nb + [SEM_SPEC, SEM_SPEC, pl.BlockSpec(memory_space=pltpu.VMEM)],
        out_shape=[jax.ShapeDtypeStruct(a.shape, a.dtype) for a in bufs]
        + [sem, sem, jax.ShapeDtypeStruct((SUBLANES, LANES), F32)],
        input_output_aliases={t: t for t in range(nb)},
        compiler_params=_params(has_side_effects=_SPLIT_EFFECT),
    )(*bufs)
    return list(res[:nb]), res[nb], res[nb + 1], res[nb + 2]


def _forward_wait(g, bufs, ssem, rsem, after):
    layer, pieces = _GATHER_GROUPS[g]
    keys = _group_keys(g)
    nb = len(keys)

    def body(*refs):
        ssem_ref, rsem_ref = refs[nb], refs[nb + 1]
        land = dict(zip(keys, refs[nb + 3:]))
        x, y, c = _pos()

        def wait(mine, others):
            for k, p in enumerate(pieces):
                ref = land[(_PIECES[p][1], layer)]
                for j in others:
                    cp = _remote(_half_view(ref, _PIECES[p], j, c), _half_view(ref, _PIECES[p], j, 1 - c),
                                 ssem_ref.at[k * N_CHIPS + j], rsem_ref.at[k * N_CHIPS + j], (x, y, 1 - c))
                    cp.wait_send()
                    cp.wait_recv()

        _for_my_chip(wait)

    return pl.pallas_call(
        body, name="forward_wait_g%d" % g,
        in_specs=[HBM_SPEC] * nb + [SEM_SPEC, SEM_SPEC, pl.BlockSpec(memory_space=pl.ANY)],
        out_specs=[HBM_SPEC] * nb,
        out_shape=[jax.ShapeDtypeStruct(a.shape, a.dtype) for a in bufs],
        input_output_aliases={t: t for t in range(nb)},
        compiler_params=_params(has_side_effects=_SPLIT_EFFECT),
    )(*bufs, ssem, rsem, after)


_REDUCE_GROUPS = (
    ((5, 1), (3, 1), (4, 1), (2, 1), (1, 1), (0, 1)),
    ((5, 0), (3, 0), (4, 0)),
    ((2, 0), (1, 0)),
    ((0, 0),),
)


def _reduce_keys(group):
    keys = []
    for p, layer in group:
        if (_PIECES[p][1], layer) not in keys:
            keys.append((_PIECES[p][1], layer))
    return keys


def _half_block(piece, tm):
    _, _, rs, cs, rstep, cstep, coff = piece
    return lambda i, j, c: (j * (rstep // tm) + c * (rs // 2 // tm) + i, coff // cs + j * (cstep // cs))


def _swap_start(g, dwb):
    group = _REDUCE_GROUPS[g]
    keys = _reduce_keys(group)
    nk = len(keys)

    def body(*refs):
        src = dict(zip(keys, refs[nk:2 * nk]))
        dst = dict(zip(keys, refs[2 * nk:3 * nk]))
        ssem, rsem, token = refs[3 * nk:]
        x, y, c = _pos()
        for k, (p, layer) in enumerate(group):
            key = (_PIECES[p][1], layer)
            for j in range(N_CHIPS):
                _remote(_half_view(src[key], _PIECES[p], j, 1 - c), _half_view(dst[key], _PIECES[p], j, 1 - c),
                        ssem.at[k * N_CHIPS + j], rsem.at[k * N_CHIPS + j], (x, y, 1 - c)).start()
        token[...] = jnp.zeros(token.shape, token.dtype)

    sem = pltpu.SemaphoreType.DMA((N_CHIPS * len(group),))
    shapes = [jax.ShapeDtypeStruct(dwb[k].shape, BF16) for k in keys]
    res = pl.pallas_call(
        body, name="swap_start_g%d" % g,
        in_specs=[HBM_SPEC] * nk,
        out_specs=[HBM_SPEC] * (2 * nk) + [SEM_SPEC, SEM_SPEC, pl.BlockSpec(memory_space=pltpu.VMEM)],
        out_shape=shapes + shapes + [sem, sem, jax.ShapeDtypeStruct((SUBLANES, LANES), F32)],
        input_output_aliases={t: t for t in range(nk)},
        compiler_params=_params(has_side_effects=_SPLIT_EFFECT),
    )(*[pltpu.with_memory_space_constraint(dwb[k], pltpu.HBM) for k in keys])
    return list(res[:nk]), list(res[nk:2 * nk]), res[2 * nk], res[2 * nk + 1], res[2 * nk + 2]


def _swap_wait(g, own, land, ssem, rsem, after):
    group = _REDUCE_GROUPS[g]
    keys = _reduce_keys(group)
    nk = len(keys)

    def body(*refs):
        ssem_ref, rsem_ref = refs[2 * nk], refs[2 * nk + 1]
        src = dict(zip(keys, refs[2 * nk + 3:3 * nk + 3]))
        dst = dict(zip(keys, refs[3 * nk + 3:]))
        x, y, c = _pos()
        for k, (p, layer) in enumerate(group):
            key = (_PIECES[p][1], layer)
            for j in range(N_CHIPS):
                cp = _remote(_half_view(src[key], _PIECES[p], j, 1 - c), _half_view(dst[key], _PIECES[p], j, c),
                             ssem_ref.at[k * N_CHIPS + j], rsem_ref.at[k * N_CHIPS + j], (x, y, 1 - c))
                cp.wait_send()
                cp.wait_recv()

    res = pl.pallas_call(
        body, name="swap_wait_g%d" % g,
        in_specs=[HBM_SPEC] * (2 * nk) + [SEM_SPEC, SEM_SPEC, pl.BlockSpec(memory_space=pl.ANY)],
        out_specs=[HBM_SPEC] * (2 * nk),
        out_shape=[jax.ShapeDtypeStruct(a.shape, a.dtype) for a in list(own) + list(land)],
        input_output_aliases={t: t for t in range(2 * nk)},
        compiler_params=_params(has_side_effects=_SPLIT_EFFECT),
    )(*own, *land, ssem, rsem, after)
    return dict(zip(keys, res[nk:]))


def _chip_partial(piece, dw, got, prev, c_arr, name):
    _, _, rs, cs, _, _, _ = piece
    half = rs // 2
    tm = _tile(half, 256)
    blk = _half_block(piece, tm)

    def body(s_ref, dw_ref, got_ref, *rest):
        rest[-1][...] = (dw_ref[...] + got_ref[...].astype(F32)).astype(BF16)

    spec = pl.BlockSpec((tm, cs), lambda j, i, s: blk(i, j, s[0]))
    in_specs = [spec, spec]
    args = [dw, got]
    aliases = {}
    if prev is not None:
        in_specs.append(pl.BlockSpec(memory_space=pl.ANY))
        args.append(prev)
        aliases = {3: 0}
    return pl.pallas_call(
        body, name=name,
        grid_spec=pltpu.PrefetchScalarGridSpec(
            num_scalar_prefetch=1, grid=(N_CHIPS, half // tm), in_specs=in_specs, out_specs=spec),
        out_shape=jax.ShapeDtypeStruct(dw.shape, BF16), input_output_aliases=aliases,
        compiler_params=_params(("parallel", "parallel")),
    )(c_arr, *args)


def _scatter_start(g, partials):
    group = _REDUCE_GROUPS[g]
    keys = _reduce_keys(group)
    nk, n = len(keys), len(group)

    def body(*refs):
        pt = dict(zip(keys, refs[nk:2 * nk]))
        land = refs[2 * nk:2 * nk + n]
        ssem, rsem, token = refs[2 * nk + n:]
        _, _, c = _pos()

        def send(mine, others):
            for k, (p, layer) in enumerate(group):
                for j in others:
                    _remote(_half_view(pt[(_PIECES[p][1], layer)], _PIECES[p], j, c), land[k].at[mine],
                            ssem.at[k * N_CHIPS + j], rsem.at[k * N_CHIPS + mine], (j // 2, j % 2, c)).start()

        _for_my_chip(send)
        token[...] = jnp.zeros(token.shape, token.dtype)

    sem = pltpu.SemaphoreType.DMA((N_CHIPS * n,))
    res = pl.pallas_call(
        body, name="scatter_start_g%d" % g,
        in_specs=[HBM_SPEC] * nk,
        out_specs=[HBM_SPEC] * (nk + n) + [SEM_SPEC, SEM_SPEC, pl.BlockSpec(memory_space=pltpu.VMEM)],
        out_shape=([jax.ShapeDtypeStruct(partials[k].shape, BF16) for k in keys]
                   + [jax.ShapeDtypeStruct((N_CHIPS, _PIECES[p][2] // 2, _PIECES[p][3]), BF16) for p, _ in group]
                   + [sem, sem, jax.ShapeDtypeStruct((SUBLANES, LANES), F32)]),
        input_output_aliases={t: t for t in range(nk)},
        compiler_params=_params(has_side_effects=_SPLIT_EFFECT),
    )(*[pltpu.with_memory_space_constraint(partials[k], pltpu.HBM) for k in keys])
    return list(res[:nk]), list(res[nk:nk + n]), res[nk + n], res[nk + n + 1], res[nk + n + 2]


def _scatter_wait(g, partials, land, ssem, rsem, after):
    group = _REDUCE_GROUPS[g]
    keys = _reduce_keys(group)
    nk, n = len(keys), len(group)

    def body(*refs):
        ssem_ref, rsem_ref = refs[nk + n], refs[nk + n + 1]
        pt = dict(zip(keys, refs[nk + n + 3:2 * nk + n + 3]))
        land_ref = refs[2 * nk + n + 3:]
        _, _, c = _pos()

        def wait(mine, others):
            for k, (p, layer) in enumerate(group):
                for j in others:
                    cp = _remote(_half_view(pt[(_PIECES[p][1], layer)], _PIECES[p], j, c), land_ref[k].at[j],
                                 ssem_ref.at[k * N_CHIPS + j], rsem_ref.at[k * N_CHIPS + j], (j // 2, j % 2, c))
                    cp.wait_send()
                    cp.wait_recv()

        _for_my_chip(wait)

    res = pl.pallas_call(
        body, name="scatter_wait_g%d" % g,
        in_specs=[HBM_SPEC] * (nk + n) + [SEM_SPEC, SEM_SPEC, pl.BlockSpec(memory_space=pl.ANY)],
        out_specs=[HBM_SPEC] * (nk + n),
        out_shape=[jax.ShapeDtypeStruct(a.shape, a.dtype) for a in list(partials) + list(land)],
        input_output_aliases={t: t for t in range(nk + n)},
        compiler_params=_params(has_side_effects=_SPLIT_EFFECT),
    )(*partials, *land, ssem, rsem, after)
    return list(res[nk:])


def _reduce_half(piece, layer, dw, got, land, prev, idx, name):
    _, _, rs, cs, _, _, _ = piece
    half = rs // 2
    tm = _tile(half, 256)
    blk = _half_block(piece, tm)

    def body(s_ref, dw_ref, got_ref, r1, r2, r3, *rest):
        acc = dw_ref[...] + got_ref[...].astype(F32)
        for r in (r1, r2, r3):
            acc = acc + r[...].astype(F32)
        rest[-1][...] = acc

    def land_map(k):
        return lambda i, s: ((s[1] + k) % N_CHIPS, i, 0)

    own = pl.BlockSpec((tm, cs), lambda i, s: blk(i, s[1], s[0]))
    in_specs = [own, own] + [pl.BlockSpec((None, tm, cs), land_map(k)) for k in (1, 2, 3)]
    args = [dw, got, land, land, land]
    aliases = {}
    if prev is not None:
        in_specs.append(pl.BlockSpec(memory_space=pl.ANY))
        args.append(prev)
        aliases = {6: 0}
    return pl.pallas_call(
        body, name=name,
        grid_spec=pltpu.PrefetchScalarGridSpec(
            num_scalar_prefetch=1, grid=(half // tm,), in_specs=in_specs,
            out_specs=pl.BlockSpec((None, tm, cs), lambda i, s: (layer, s[0] * (half // tm) + i, 0))),
        out_shape=jax.ShapeDtypeStruct((DEPTH, rs, cs), F32), input_output_aliases=aliases,
        compiler_params=_params(("parallel",)),
    )(idx, *args)


def _share_halves(reduced):
    def body(*refs):
        buf = refs[N_PIECES:2 * N_PIECES]
        ssem, rsem = refs[2 * N_PIECES:]
        x, y, c = _pos()

        def half(p, layer, cc):
            rows = _PIECES[p][2] // 2
            return buf[p].at[layer, pl.ds(pl.multiple_of(cc * rows, SUBLANES), rows), :]

        pairs = [(p, layer) for p in range(N_PIECES) for layer in range(DEPTH)]
        rem = [_remote(half(p, layer, c), half(p, layer, c), ssem.at[k], rsem.at[k], (x, y, 1 - c))
               for k, (p, layer) in enumerate(pairs)]
        for cp in rem:
            cp.start()
        for k, (p, layer) in enumerate(pairs):
            rem[k].wait_send()
            _remote(half(p, layer, 1 - c), half(p, layer, 1 - c), ssem.at[k], rsem.at[k], (x, y, 1 - c)).wait_recv()

    nsem = N_PIECES * DEPTH
    return pl.pallas_call(
        body, name="share_halves",
        in_specs=[HBM_SPEC] * N_PIECES, out_specs=[HBM_SPEC] * N_PIECES,
        out_shape=[jax.ShapeDtypeStruct((DEPTH, p[2], p[3]), F32) for p in _PIECES],
        input_output_aliases={t: t for t in range(N_PIECES)},
        scratch_shapes=[pltpu.SemaphoreType.DMA((nsem,)), pltpu.SemaphoreType.DMA((nsem,))],
        compiler_params=_params(has_side_effects=True),
    )(*reduced)


N_DEV = 8


def _place_slot(v, me_arr):
    rows = v.shape[0]
    tm = _tile(rows, 512)

    def body(s_ref, v_ref, out_ref):
        out_ref[...] = v_ref[...]

    return pl.pallas_call(
        body, name="place_small",
        grid_spec=pltpu.PrefetchScalarGridSpec(
            num_scalar_prefetch=1, grid=(rows // tm,),
            in_specs=[pl.BlockSpec((tm, LANES), lambda i, s: (i, 0))],
            out_specs=pl.BlockSpec((None, tm, LANES), lambda i, s: (s[0], i, 0))),
        out_shape=jax.ShapeDtypeStruct((N_DEV, rows, LANES), F32),
        compiler_params=_params(("parallel",)),
    )(me_arr, v)


def _peers():
    x, y, c = _pos()
    return (x, y, c), (x, y, 1 - c), [(1 - x, y, c), (x, 1 - y, c), (1 - x, 1 - y, c)]


def _slot_of(ref, dev):
    return ref.at[4 * dev[0] + 2 * dev[1] + dev[2]]


def _small_gather_start(g):
    def body(g_in, g_ref, ssem, rsem, token):
        me, sibling, others = _peers()
        for k, dev in enumerate([sibling] + others):
            _remote(_slot_of(g_ref, me), _slot_of(g_ref, me), ssem.at[k], rsem.at[k], dev).start()
        token[...] = jnp.zeros(token.shape, token.dtype)

    sem = pltpu.SemaphoreType.DMA((N_CHIPS,))
    return pl.pallas_call(
        body, name="small_gather_start",
        in_specs=[HBM_SPEC], out_specs=[HBM_SPEC, SEM_SPEC, SEM_SPEC, pl.BlockSpec(memory_space=pltpu.VMEM)],
        out_shape=[jax.ShapeDtypeStruct(g.shape, g.dtype), sem, sem, jax.ShapeDtypeStruct((SUBLANES, LANES), F32)],
        input_output_aliases={0: 0},
        compiler_params=_params(has_side_effects=_SPLIT_EFFECT),
    )(pltpu.with_memory_space_constraint(g, pltpu.HBM))


def _small_gather_wait(g, ssem, rsem, after):
    def body(g_in, ssem_ref, rsem_ref, after_ref, g_ref):
        me, sibling, others = _peers()
        for k, dev in enumerate([sibling] + others):
            cp = _remote(_slot_of(g_ref, me), _slot_of(g_ref, dev), ssem_ref.at[k], rsem_ref.at[k], dev)
            cp.wait_send()
            cp.wait_recv()

    return pl.pallas_call(
        body, name="small_gather_wait",
        in_specs=[HBM_SPEC, SEM_SPEC, SEM_SPEC, pl.BlockSpec(memory_space=pl.ANY)], out_specs=HBM_SPEC,
        out_shape=jax.ShapeDtypeStruct(g.shape, g.dtype), input_output_aliases={0: 0},
        compiler_params=_params(has_side_effects=_SPLIT_EFFECT),
    )(g, ssem, rsem, after)


def _small_gather_forward(g):
    def body(g_in, g_ref, ssem, rsem):
        me, sibling, others = _peers()
        cps = [_remote(_slot_of(g_ref, dev), _slot_of(g_ref, dev), ssem.at[k], rsem.at[k], sibling)
               for k, dev in enumerate(others)]
        for cp in cps:
            cp.start()
        for k, dev in enumerate(others):
            theirs = _slot_of(g_ref, (dev[0], dev[1], sibling[2]))
            _remote(theirs, theirs, ssem.at[k], rsem.at[k], sibling).wait_recv()
        for cp in cps:
            cp.wait_send()

    nsem = N_CHIPS - 1
    return pl.pallas_call(
        body, name="small_gather_forward",
        in_specs=[HBM_SPEC], out_specs=HBM_SPEC, out_shape=jax.ShapeDtypeStruct(g.shape, g.dtype),
        input_output_aliases={0: 0},
        scratch_shapes=[pltpu.SemaphoreType.DMA((nsem,)), pltpu.SemaphoreType.DMA((nsem,))],
        compiler_params=_params(has_side_effects=True),
    )(g)


def _sum_slots(g, name):
    n, rows, _ = g.shape
    tm = _tile(rows, 512)

    def body(g_ref, out_ref):
        acc = g_ref[0]
        for k in range(1, n):
            acc = acc + g_ref[k]
        out_ref[...] = acc

    return pl.pallas_call(
        body, name=name, grid=(rows // tm,),
        in_specs=[pl.BlockSpec((n, tm, LANES), lambda i: (0, i, 0))],
        out_specs=pl.BlockSpec((tm, LANES), lambda i: (i, 0)),
        out_shape=jax.ShapeDtypeStruct((rows, LANES), F32),
        compiler_params=_params(("parallel",)),
    )(g)


_TINY = ("ln_mix_g", "ret_log_gamma", "ssm_a_re", "ssm_a_im", "ssm_log_dt", "ssm_d", "b_glu", "ln_ffn_g", "ln_final_g")
_MID = ("ssm_b_re", "ssm_b_im", "ssm_c_re", "ssm_c_im")
_SMALL = _TINY + _MID
_FLAT_ALIGN = LANES * LANES
_FLAT_ROWS = 1024


def _flat_rows(like, names):
    rows = sum((math.prod(like[n].shape) + (-math.prod(like[n].shape)) % _FLAT_ALIGN) // LANES for n in names)
    return rows + (-rows) % _FLAT_ROWS


def _flatten(d, names):
    parts = []
    for n in names:
        f = d[n].reshape(-1)
        parts.append(jnp.pad(f, (0, (-f.shape[0]) % _FLAT_ALIGN)))
    total = sum(p.shape[0] for p in parts)
    parts.append(jnp.zeros(((-total) % (_FLAT_ROWS * LANES),), F32))
    return jnp.concatenate(parts).reshape(-1, LANES)


def _unflatten(flat, like, names):
    out, row = {}, 0
    for n in names:
        size = math.prod(like[n].shape)
        rows = (size + (-size) % _FLAT_ALIGN) // LANES
        part = lax.optimization_barrier(flat[row:row + rows])
        out[n] = part.reshape(-1)[:size].reshape(like[n].shape)
        row += rows
    return out


_BIG = ("w_in", "w_glu", "w_out", "w_ffn_gate", "w_ffn_up", "w_ffn_down")
_WEIGHTS = ("ln_mix_g", "w_in", "ret_log_gamma", "ssm_a_re", "ssm_a_im", "ssm_log_dt", "ssm_b_re", "ssm_b_im",
            "ssm_c_re", "ssm_c_im", "ssm_d", "w_glu", "b_glu", "w_out", "ln_ffn_g", "w_ffn_gate", "w_ffn_up",
            "w_ffn_down", "ln_final_g")


def _rope_tables(seq):
    half = QK_DIM // 2
    inv = 1.0 / (ROPE_BASE ** (jnp.arange(half, dtype=F32) / half))
    ang = jnp.arange(seq, dtype=F32)[:, None] * inv[None, :]
    return jnp.cos(ang), jnp.sin(ang)


def _step(w, m, v, x, target):
    seq = x.shape[0]
    seg_len = float(seq // SEGMENTS)
    c_idx = lax.axis_index("c").astype(jnp.int32)
    chip_idx = (2 * lax.axis_index("x") + lax.axis_index("y")).astype(jnp.int32)
    c_arr = jnp.stack([c_idx])
    idx_arr = jnp.stack([c_idx, chip_idx])

    chip_arr = jnp.stack([chip_idx])
    placed = {}
    for piece in _PIECES:
        for layer in range(DEPTH):
            key = (piece[1], layer)
            placed[key] = _cast_place(piece, w[piece[0]], layer, placed.get(key), chip_arr,
                                      "cast_%s_l%d" % (piece[0], layer))
    keys = [k for g in range(len(_GATHER_GROUPS)) for k in _group_keys(g)]
    flying, ssems, rsems, token = _gather_start([placed[k] for k in keys])
    wf = {b[0]: [None] * DEPTH for b in _BUFFERS}

    handing = {}

    def arrive(g, after):
        ks = _group_keys(g)
        landed = _gather_wait(g, [flying[k] for k in ks], ssems[g], rsems[g], after)
        for k, a in zip(ks, _gather_forward(g, landed)):
            wf[k[0]][k[1]] = a

    def hand_over(g, after):
        ks = _group_keys(g)
        landed = _gather_wait(g, [flying[k] for k in ks], ssems[g], rsems[g], after)
        bufs, fs, fr, tok = _forward_start(g, landed)
        handing[g] = (bufs, fs, fr)
        return tok[0:1, 0:1]

    def complete(g, after):
        for k, a in zip(_group_keys(g), _forward_wait(g, *handing[g], after)):
            wf[k[0]][k[1]] = a

    cos, sin = _rope_tables(seq)

    saved = []
    xc = x + token[0, 0]
    for i in range(DEPTH):
        t = "_l%d" % i
        s = {"x_in": xc}
        if i == 0:
            s["h"] = _rms_fwd(xc, w["ln_mix_g"][i:i + 1], "rms_mix" + t)
            arrive(0, s["h"])
        else:
            s["h"] = _rms_fwd(xc, w["ln_mix_g"][i:i + 1] + next_in, "rms_mix" + t)
        s["proj"] = _matmul(s["h"], wf["in"][i], "nn", [F32], name="mm_in" + t)[0]
        s["qr"], s["kr"] = _rot_fwd(s["proj"], cos, sin, "rot" + t)
        s["lg"] = jnp.broadcast_to(w["ret_log_gamma"][i].T[:, :, None], (HEADS, 2, LANES))
        s["y"] = _ret_fwd(s["qr"], s["kr"], s["proj"], s["lg"], "ret" + t)
        s5_raw = (w["ssm_a_re"][i], w["ssm_a_im"][i], w["ssm_log_dt"][i], w["ssm_b_re"][i], w["ssm_b_im"][i])
        disc, s["disc_vjp"] = jax.vjp(functools.partial(_s5_discretize, seg_len=seg_len), *s5_raw)
        bblk, cblk, lam = _s5_pack(*disc, w["ssm_c_re"][i], w["ssm_c_im"][i])
        s["s5"] = (bblk.astype(BF16), cblk.astype(BF16), lam)
        s["s5y"] = _s5_fwd(s["proj"], *s["s5"], "s5" + t)
        d_skip = w["ssm_d"][i:i + 1] + hand_over(2 * i + 1, s["s5y"])
        s["ret"], s["ysg"], s["ysgb"] = _post1_fwd(s["y"], s["proj"], s["s5y"], d_skip, "post" + t)
        complete(2 * i + 1, s["ysgb"])
        s["z"] = _matmul(s["ysgb"], wf["glu"][i], "nn", [F32], name="mm_glu" + t)[0]
        s["merged"] = _merge_fwd(s["z"], s["ysg"], s["proj"], s["ret"], w["b_glu"][i:i + 1], "merge" + t)
        s["x1"] = _matmul(s["merged"], wf["out"][i], "nn", [F32], add=xc, name="mm_out" + t)[0]
        s["h2"] = _rms_fwd(s["x1"], w["ln_ffn_g"][i:i + 1], "rms_ffn" + t)
        s["ab"] = _matmul(s["h2"], wf["gu"][i], "nn", [F32], name="mm_gu" + t)[0]
        if i + 1 < DEPTH:
            next_in = hand_over(2 * i + 2, s["ab"])
        s["f"] = _glu_fwd(s["ab"], "glu" + t)
        xc = _matmul(s["f"], wf["down"][i], "nn", [F32], add=s["x1"], name="mm_down" + t)[0]
        if i + 1 < DEPTH:
            complete(2 * i + 2, xc)
        saved.append(s)

    dx, dxb, loss_row, dg_final = _loss_stage(xc, target, w["ln_final_g"][None, :], "loss")
    loss = lax.psum(loss_row[0, 0], ("x", "y", "c"))

    g_small = {"ln_final_g": dg_final[0]}
    per_layer = {n: [None] * DEPTH for n in _SMALL if n != "ln_final_g"}
    dws, got, swaps, flights = {}, {}, {}, []

    def dw_mm(a, b, buf, i, name):
        dws[(buf, i)] = _matmul(a, b, "tn", [F32, BF16], name=name)

    def depart(g):
        keys = _reduce_keys(_REDUCE_GROUPS[g])
        own, land, ssem, rsem, tok = _swap_start(g, {k: dws[k][1] for k in keys})
        swaps[g] = (own, land, ssem, rsem)
        return tok[0:1, 0:1]

    def proceed(g, after):
        group = _REDUCE_GROUPS[g]
        got.update(_swap_wait(g, *swaps[g], after))
        partials = {}
        for p, layer in group:
            key = (_PIECES[p][1], layer)
            partials[key] = _chip_partial(_PIECES[p], dws[key][0], got[key], partials.get(key), c_arr,
                                          "chip_partial_%s_l%d" % (_PIECES[p][0], layer))
        pt, land, ssem, rsem, tok = _scatter_start(g, partials)
        flights.append((g, pt, land, ssem, rsem))
        return tok[0:1, 0:1]

    for i in reversed(range(DEPTH)):
        t = "_l%d" % i
        s = saved[i]
        g_ffn, g_mix, d_skip = w["ln_ffn_g"][i:i + 1], w["ln_mix_g"][i:i + 1], w["ssm_d"][i:i + 1]
        dw_mm(s["f"], dxb, "down", i, "dw_down" + t)
        df = _matmul(dxb, wf["down"][i], "nt", [F32], name="dx_down" + t)[0]
        if i == 0:
            g_ffn = g_ffn + proceed(0, df)
        dab = _glu_bwd(s["ab"], df, "glu_bwd" + t)
        dw_mm(s["h2"], dab, "gu", i, "dw_gu" + t)
        if i == 0:
            g_ffn = g_ffn + depart(1)
        dh2 = _matmul(dab, wf["gu"][i], "nt", [F32], name="dx_gu" + t)[0]
        if i == 0:
            g_ffn = g_ffn + proceed(1, dh2)
        dx1, dx1b, dg = _rms_bwd(s["x1"], dh2, dx, g_ffn, "rms_ffn_bwd" + t)
        per_layer["ln_ffn_g"][i] = dg[0]

        dw_mm(s["merged"], dx1b, "out", i, "dw_out" + t)
        dmerged = _matmul(dx1b, wf["out"][i], "nt", [F32], name="dx_out" + t)[0]
        dz, dys_part, dgs, db = _merge_bwd(s["z"], s["ysg"], s["proj"], s["ret"], dmerged, w["b_glu"][i:i + 1],
                                           "merge_bwd" + t)
        per_layer["b_glu"][i] = db[0]
        dw_mm(s["ysgb"], dz, "glu", i, "dw_glu" + t)
        if i == 0:
            d_skip = d_skip + depart(2)
        dys = _matmul(dz, wf["glu"][i], "nt", [F32], add=dys_part, name="dx_glu" + t)[0]
        if i == 0:
            d_skip = d_skip + proceed(2, dys)
        dy, dgg, dgr, ds5, du_part, dd = _post1_bwd(s["y"], s["proj"], s["s5y"], dmerged, dys,
                                                    d_skip, "post_bwd" + t)
        per_layer["ssm_d"][i] = dd[0]
        du, dbblk, dcblk, dlam = _s5_bwd(s["proj"], ds5, du_part, *s["s5"], "s5_bwd" + t)
        dlr, dli, dbr, dbi, dcr, dci = _s5_unpack(dbblk, dcblk, dlam)
        zeros = jnp.zeros_like(dlr)
        da_re, da_im, dlog_dt, db_re, db_im = s["disc_vjp"]((dlr, dli, zeros, zeros, dbr, dbi))
        for n, val in (("ssm_a_re", da_re), ("ssm_a_im", da_im), ("ssm_log_dt", dlog_dt), ("ssm_b_re", db_re),
                       ("ssm_b_im", db_im), ("ssm_c_re", dcr), ("ssm_c_im", dci)):
            per_layer[n][i] = val
        dqr, dkr, dv, dlg = _ret_bwd(s["qr"], s["kr"], s["proj"], dy, s["lg"], "ret_bwd" + t)
        per_layer["ret_log_gamma"][i] = dlg[:, :, 0].T
        dqkv = _rot_bwd(dqr, dkr, dv, cos, sin, "rot_bwd" + t)
        dproj = jnp.concatenate([dqkv, dgg, du, dgr, dgs], axis=1)
        dw_mm(s["h"], dproj, "in", i, "dw_in" + t)
        if i == 0:
            g_mix = g_mix + depart(3)
        dh = _matmul(dproj, wf["in"][i], "nt", [F32], name="dx_in" + t)[0]
        if i == 0:
            g_mix = g_mix + proceed(3, dh)
        dx, dxb, dg = _rms_bwd(s["x_in"], dh, dx1, g_mix, "rms_mix_bwd" + t)
        per_layer["ln_mix_g"][i] = dg[0]
        if i == DEPTH - 1:
            dxb = dxb + depart(0).astype(BF16)

    for n in per_layer:
        g_small[n] = jnp.stack(per_layer[n])
    me_arr = jnp.stack([2 * chip_idx + c_idx])
    flying_small, small_ssem, small_rsem, small_token = _small_gather_start(
        _place_slot(_flatten(g_small, _SMALL), me_arr))

    reduced = [None] * N_PIECES
    for g, pt, land, ssem, rsem in flights:
        landed = _scatter_wait(g, pt, land, ssem, rsem, small_token)
        for (p, layer), buf in zip(_REDUCE_GROUPS[g], landed):
            key = (_PIECES[p][1], layer)
            reduced[p] = _reduce_half(_PIECES[p], layer, dws[key][0], got[key], buf, reduced[p], idx_arr,
                                      "reduce_%s_l%d" % (_PIECES[p][0], layer))
    g_big = dict(zip([p[0] for p in _PIECES], _share_halves(reduced)))

    grads, delta, new_m, new_v = {}, {}, {}, {}
    for n in _BIG:
        d, r, cc = w[n].shape
        two_d = lambda a: a.reshape(d * r, cc)
        dl, mn, vn = _adamw(two_d(w[n]), two_d(g_big[n]), two_d(m[n]), two_d(v[n]), "adamw_" + n)
        grads[n], delta[n], new_m[n], new_v[n] = g_big[n], dl.reshape(d, r, cc), mn.reshape(d, r, cc), vn.reshape(d, r, cc)

    gathered = _small_gather_forward(_small_gather_wait(flying_small, small_ssem, small_rsem, delta[_BIG[-1]]))
    g_flat = _sum_slots(gathered, "sum_small")
    grads.update(_unflatten(g_flat, w, _SMALL))
    tiny_rows = _flat_rows(w, _TINY)
    dl, mn, vn = _adamw(_flatten(w, _TINY), g_flat[:tiny_rows], _flatten(m, _TINY), _flatten(v, _TINY), "adamw_tiny")
    for dst, flat in ((delta, dl), (new_m, mn), (new_v, vn)):
        dst.update(_unflatten(flat, w, _TINY))
    for n in _MID:
        delta[n], new_m[n], new_v[n] = _adamw_nd(w[n], grads[n], m[n], v[n], "adamw_" + n)
    return loss, dx, grads, delta, new_m, new_v


def kernel(x, ln_mix_g, w_in, ret_log_gamma, ssm_a_re, ssm_a_im, ssm_log_dt, ssm_b_re, ssm_b_im, ssm_c_re, ssm_c_im, ssm_d, w_glu, b_glu, w_out, ln_ffn_g, w_ffn_gate, w_ffn_up, w_ffn_down, ln_final_g, loss_target, m_ln_mix_g, m_w_in, m_ret_log_gamma, m_ssm_a_re, m_ssm_a_im, m_ssm_log_dt, m_ssm_b_re, m_ssm_b_im, m_ssm_c_re, m_ssm_c_im, m_ssm_d, m_w_glu, m_b_glu, m_w_out, m_ln_ffn_g, m_w_ffn_gate, m_w_ffn_up, m_w_ffn_down, m_ln_final_g, v_ln_mix_g, v_w_in, v_ret_log_gamma, v_ssm_a_re, v_ssm_a_im, v_ssm_log_dt, v_ssm_b_re, v_ssm_b_im, v_ssm_c_re, v_ssm_c_im, v_ssm_d, v_w_glu, v_b_glu, v_w_out, v_ln_ffn_g, v_w_ffn_gate, v_w_ffn_up, v_w_ffn_down, v_ln_final_g):
    given = dict(locals())
    w = {n: given[n] for n in _WEIGHTS}
    m = {n: given["m_" + n] for n in _WEIGHTS}
    v = {n: given["v_" + n] for n in _WEIGHTS}
    loss, dx, grads, delta, new_m, new_v = _step(w, m, v, x[0], loss_target[0])
    return (loss, dx[None], *[grads[n] for n in _WEIGHTS], *[delta[n] for n in _WEIGHTS],
            *[new_m[n] for n in _WEIGHTS], *[new_v[n] for n in _WEIGHTS])
```

```python
import functools
import math

import jax
import jax.numpy as jnp
from jax import lax
from jax.experimental import pallas as pl
from jax.experimental.pallas import tpu as pltpu

F32 = jnp.float32
BF16 = jnp.bfloat16

D_MODEL = 2048
DEPTH = 2
HEADS = 4
QK_DIM = 256
V_DIM = 512
QK_WIDTH = HEADS * QK_DIM
ROPE_BASE = 10000.0
GROUP = 16
N_GROUPS = D_MODEL // GROUP
N_STATE = 64
D_FF = 5632
IN_WIDTH = 2 * QK_WIDTH + 5 * D_MODEL
EPS = 1e-6
N_CHIPS = 4

ADAM_LR = 0.001
ADAM_B1 = 0.9
ADAM_B2 = 0.999
ADAM_EPS = 1e-08
ADAM_WD = 0.01
ADAM_STEP = 10

LANES = 128
SUBLANES = 8
VMEM_LIMIT = 56 * 1024 * 1024
SEGMENTS = SUBLANES
GROUPS_PER_TILE = LANES // GROUP
STATE_COLS = GROUPS_PER_TILE * N_STATE
N_TILES = D_MODEL // LANES
SCAN_UNROLL = 4

MESH = pl.DeviceIdType.MESH
HBM_SPEC = pl.BlockSpec(memory_space=pltpu.HBM)


def _params(sem=None, **kw):
    return pltpu.CompilerParams(dimension_semantics=sem, vmem_limit_bytes=VMEM_LIMIT, **kw)


def _tile(n, cap=1024):
    for t in (2048, 1024, 512, 256, 128, 64):
        if t <= cap and n % t == 0:
            return t
    raise ValueError(n)


def _rows_call(fn, rows, pars, row_outs, par_outs, *, tm, name):
    m = rows[0][0].shape[0]
    nr, npar, nro, npo = len(rows), len(pars), len(row_outs), len(par_outs)

    def body(*refs):
        rin = refs[:nr]
        pin = refs[nr:nr + npar]
        rout = refs[nr + npar:nr + npar + nro]
        pout = refs[nr + npar + nro:]
        res = fn(*[r[...] for r in rin], *[p[...] for p in pin])
        if not isinstance(res, (tuple, list)):
            res = (res,)
        for r, v in zip(rout, res[:nro]):
            r[...] = v.astype(r.dtype)
        if npo:
            @pl.when(pl.program_id(0) == 0)
            def _():
                for p in pout:
                    p[...] = jnp.zeros(p.shape, p.dtype)
            for p, v in zip(pout, res[nro:]):
                p[...] += v

    in_specs = [pl.BlockSpec((tm, w), functools.partial(lambda cb, i: (i, cb), cb)) for (_, w, cb) in rows]
    in_specs += [pl.BlockSpec(p.shape, lambda i: (0, 0)) for p in pars]
    out_specs = [pl.BlockSpec((tm, w), lambda i: (i, 0)) for (w, _) in row_outs]
    out_specs += [pl.BlockSpec(s, lambda i: (0, 0)) for s in par_outs]
    out_shape = [jax.ShapeDtypeStruct((m, w), dt) for (w, dt) in row_outs]
    out_shape += [jax.ShapeDtypeStruct(s, F32) for s in par_outs]
    res = pl.pallas_call(
        body, name=name, grid=(m // tm,), in_specs=in_specs, out_specs=out_specs, out_shape=out_shape,
        compiler_params=_params(("arbitrary",) if npo else ("parallel",)),
    )(*[a for (a, _, _) in rows], *pars)
    return res


def _f32(*vals):
    return [v.astype(F32) for v in vals]


def _f_rms(x, g):
    r = lax.rsqrt(jnp.mean(x * x, axis=-1, keepdims=True) + EPS)
    return x * r * g


def _rms_fwd(x, g, name):
    return _rows_call(lambda xv, gv: _f_rms(xv, gv), [(x, D_MODEL, 0)], [g], [(D_MODEL, BF16)], [],
                      tm=256, name=name)[0]


def _rms_bwd(x, dh, dres, g, name):
    def fn(xv, dhv, drv, gv):
        _, vjp = jax.vjp(_f_rms, xv, gv)
        dx, dg = vjp(dhv)
        dx = dx + drv
        return dx, dx, dg
    return _rows_call(fn, [(x, D_MODEL, 0), (dh, D_MODEL, 0), (dres, D_MODEL, 0)], [g],
                      [(D_MODEL, F32), (D_MODEL, BF16)], [(1, D_MODEL)], tm=256, name=name)


def _rot_heads(xv, cos, sin, scale):
    half = QK_DIM // 2
    outs = []
    for h in range(HEADS):
        x1 = xv[:, h * QK_DIM:h * QK_DIM + half]
        x2 = xv[:, h * QK_DIM + half:(h + 1) * QK_DIM]
        outs += [(x1 * cos - x2 * sin) * scale, (x1 * sin + x2 * cos) * scale]
    return jnp.concatenate(outs, axis=1)


def _rot_fwd(proj, cos, sin, name):
    def fn(q, k, cv, sv):
        return _rot_heads(q, cv, sv, 1.0), _rot_heads(k, cv, sv, QK_DIM ** -0.5)
    return _rows_call(fn, [(proj, QK_WIDTH, 0), (proj, QK_WIDTH, 1), (cos, LANES, 0), (sin, LANES, 0)], [],
                      [(QK_WIDTH, BF16), (QK_WIDTH, BF16)], [], tm=256, name=name)


def _rot_bwd(dqr, dkr, dv, cos, sin, name):
    def fn(dq, dk, dvv, cv, sv):
        return jnp.concatenate([_rot_heads(dq, cv, -sv, 1.0), _rot_heads(dk, cv, -sv, QK_DIM ** -0.5), dvv], axis=1)
    return _rows_call(fn, [(dqr, QK_WIDTH, 0), (dkr, QK_WIDTH, 0), (dv, D_MODEL, 0), (cos, LANES, 0), (sin, LANES, 0)],
                      [], [(2 * QK_WIDTH + D_MODEL, BF16)], [], tm=256, name=name)[0]


def _f_post1(y0, y1, y2, y3, g, gr, s5, u, dsk):
    yn = [yh * lax.rsqrt(jnp.mean(yh * yh, axis=-1, keepdims=True) + EPS) for yh in (y0, y1, y2, y3)]
    ret = jax.nn.sigmoid(gr) * (jax.nn.silu(g) * jnp.concatenate(yn, axis=1))
    ysg = jax.nn.gelu(s5 + dsk * u)
    return ret, ysg


def _post1_rows(y, proj, s5y):
    rows = [(y, V_DIM, h) for h in range(HEADS)]
    rows += [(proj, D_MODEL, 2), (proj, D_MODEL, 4), (s5y, D_MODEL, 0), (proj, D_MODEL, 3)]
    return rows


def _post1_fwd(y, proj, s5y, dsk, name):
    def fn(*vals):
        ret, ysg = _f_post1(*vals)
        return ret, ysg, ysg
    return _rows_call(fn, _post1_rows(y, proj, s5y), [dsk],
                      [(D_MODEL, F32), (D_MODEL, F32), (D_MODEL, BF16)], [], tm=128, name=name)


def _post1_bwd(y, proj, s5y, dret, dys, dsk, name):
    def fn(*vals):
        prim = vals[:8] + (vals[10],)
        _, vjp = jax.vjp(_f_post1, *prim)
        gy0, gy1, gy2, gy3, gg, ggr, gs5, gu, gd = vjp((vals[8], vals[9]))
        return jnp.concatenate([gy0, gy1, gy2, gy3], axis=1), gg, ggr, gs5, gu, gd
    rows = _post1_rows(y, proj, s5y) + [(dret, D_MODEL, 0), (dys, D_MODEL, 0)]
    return _rows_call(fn, rows, [dsk],
                      [(D_MODEL, BF16), (D_MODEL, BF16), (D_MODEL, BF16), (D_MODEL, F32), (D_MODEL, F32)],
                      [(1, D_MODEL)], tm=128, name=name)


def _f_merge(z, ysg, gs, ret, b):
    return ret + jax.nn.sigmoid(gs) * (ysg * jax.nn.sigmoid(z + b))


def _merge_fwd(z, ysg, proj, ret, b, name):
    return _rows_call(_f_merge, [(z, D_MODEL, 0), (ysg, D_MODEL, 0), (proj, D_MODEL, 5), (ret, D_MODEL, 0)], [b],
                      [(D_MODEL, BF16)], [], tm=128, name=name)[0]


def _merge_bwd(z, ysg, proj, ret, dm, b, name):
    def fn(zv, yv, gv, rv, dmv, bv):
        _, vjp = jax.vjp(_f_merge, zv, yv, gv, rv, bv)
        gz, gy, gg, _, gb = vjp(dmv)
        return gz, gy, gg, gb
    rows = [(z, D_MODEL, 0), (ysg, D_MODEL, 0), (proj, D_MODEL, 5), (ret, D_MODEL, 0), (dm, D_MODEL, 0)]
    return _rows_call(fn, rows, [b], [(D_MODEL, BF16), (D_MODEL, F32), (D_MODEL, BF16)], [(1, D_MODEL)],
                      tm=128, name=name)


def _f_glu(a, b):
    return jax.nn.silu(a) * b


def _glu_fwd(ab, name):
    return _rows_call(_f_glu, [(ab, D_FF, 0), (ab, D_FF, 1)], [], [(D_FF, BF16)], [], tm=128, name=name)[0]


def _glu_bwd(ab, df, name):
    def fn(a, b, d):
        _, vjp = jax.vjp(_f_glu, a, b)
        ga, gb = vjp(d)
        return jnp.concatenate([ga, gb], axis=1)
    return _rows_call(fn, [(ab, D_FF, 0), (ab, D_FF, 1), (df, D_FF, 0)], [], [(2 * D_FF, BF16)], [],
                      tm=128, name=name)[0]


def _loss_stage(x, tgt, g, name):
    def fn(xv, tv, gv):
        def lf(xx, gg):
            err = _f_rms(xx, gg) - tv
            row = jnp.mean(err * err, axis=-1, keepdims=True)
            return 0.5 * jnp.sum(row, axis=0, keepdims=True)
        l, vjp = jax.vjp(lf, xv, gv)
        dx, dg = vjp(jnp.ones((1, 1), F32))
        return dx, dx, jnp.broadcast_to(l, (1, LANES)), dg
    return _rows_call(fn, [(x, D_MODEL, 0), (tgt, D_MODEL, 0)], [g], [(D_MODEL, F32), (D_MODEL, BF16)],
                      [(1, LANES), (1, D_MODEL)], tm=256, name=name)


def _adam_math(wv, gv, mv, vv):
    mn = ADAM_B1 * mv + (1.0 - ADAM_B1) * gv
    vn = ADAM_B2 * vv + (1.0 - ADAM_B2) * (gv * gv)
    m_hat = mn / (1.0 - ADAM_B1 ** ADAM_STEP)
    v_hat = vn / (1.0 - ADAM_B2 ** ADAM_STEP)
    delta = -ADAM_LR * (m_hat / (jnp.sqrt(v_hat) + ADAM_EPS) + ADAM_WD * wv)
    return delta, mn, vn


def _adamw(w, g, m, v, name):
    rows, cols = w.shape
    tm = _tile(rows, 128 if cols > D_FF // N_CHIPS else (256 if cols > LANES else 512))
    return _rows_call(_adam_math, [(w, cols, 0), (g, cols, 0), (m, cols, 0), (v, cols, 0)], [],
                      [(cols, F32)] * 3, [], tm=tm, name=name)


def _adamw_nd(w, g, m, v, name):
    shape = w.shape
    lead = math.prod(shape[:-2])
    blk = (lead // 8,) + shape[-2:]
    three_d = lambda a: a.reshape((lead,) + shape[-2:])

    def body(w_ref, g_ref, m_ref, v_ref, d_ref, mn_ref, vn_ref):
        d_ref[...], mn_ref[...], vn_ref[...] = _adam_math(w_ref[...], g_ref[...], m_ref[...], v_ref[...])

    spec = pl.BlockSpec(blk, lambda i: (i, 0, 0))
    res = pl.pallas_call(
        body, name=name, grid=(8,), in_specs=[spec] * 4, out_specs=[spec] * 3,
        out_shape=[jax.ShapeDtypeStruct((lead,) + shape[-2:], F32)] * 3,
        compiler_params=_params(("parallel",)),
    )(three_d(w), three_d(g), three_d(m), three_d(v))
    return [r.reshape(shape) for r in res]


MATMUL_VMEM_BUDGET = 44 * 1024 * 1024


def _matmul_tiles(m, n, k, out_bytes, has_add):
    if k > 2048:
        return _tile(m, 1024), _tile(n, 1024), _tile(k, 1024)
    tm, tn, tk = _tile(m, 2048), _tile(n, 1024), k

    def footprint():
        acc = 4 * tm * tn if k // tk > 1 else 0
        return 2 * 2 * (tm * tk + tk * tn) + 2 * (out_bytes + 4 * has_add) * tm * tn + acc

    while footprint() > MATMUL_VMEM_BUDGET:
        if tn > 512 and n % (tn // 2) == 0:
            tn //= 2
        elif tk > 512 and k % (tk // 2) == 0:
            tk //= 2
        else:
            tm //= 2
    return tm, tn, tk


def _matmul(a, b, mode, out_dtypes, *, name, add=None):
    if mode == "nn":
        (m, k), (_, n) = a.shape, b.shape
    elif mode == "nt":
        (m, k), (n, _) = a.shape, b.shape
    else:
        (k, m), (_, n) = a.shape, b.shape
    n_out = len(out_dtypes)
    has_add = add is not None
    tm, tn, tk = _matmul_tiles(m, n, k, sum(jnp.dtype(dt).itemsize for dt in out_dtypes), has_add)
    nk = k // tk
    if mode == "nn":
        a_spec = pl.BlockSpec((tm, tk), lambda i, j, kk: (i, kk))
        b_spec = pl.BlockSpec((tk, tn), lambda i, j, kk: (kk, j))
        dims = (((1,), (0,)), ((), ()))
    elif mode == "nt":
        a_spec = pl.BlockSpec((tm, tk), lambda i, j, kk: (i, kk))
        b_spec = pl.BlockSpec((tn, tk), lambda i, j, kk: (j, kk))
        dims = (((1,), (1,)), ((), ()))
    else:
        a_spec = pl.BlockSpec((tk, tm), lambda i, j, kk: (kk, i))
        b_spec = pl.BlockSpec((tk, tn), lambda i, j, kk: (kk, j))
        dims = (((0,), (0,)), ((), ()))

    def body(*refs):
        a_ref, b_ref = refs[0], refs[1]
        add_ref = refs[2] if has_add else None
        outs = refs[2 + has_add:2 + has_add + n_out]

        def finish(r):
            if has_add:
                r = r + add_ref[...]
            for o in outs:
                o[...] = r.astype(o.dtype)

        if nk == 1:
            finish(lax.dot_general(a_ref[...], b_ref[...], dims, preferred_element_type=F32))
            return
        acc = refs[-1]
        kk = pl.program_id(2)

        @pl.when(kk == 0)
        def _():
            acc[...] = jnp.zeros(acc.shape, F32)

        acc[...] += lax.dot_general(a_ref[...], b_ref[...], dims, preferred_element_type=F32)

        @pl.when(kk == nk - 1)
        def _():
            finish(acc[...])

    in_specs = [a_spec, b_spec]
    args = [a, b]
    if has_add:
        in_specs.append(pl.BlockSpec((tm, tn), lambda i, j, kk: (i, j)))
        args.append(add)
    return pl.pallas_call(
        body, name=name, grid=(m // tm, n // tn, nk), in_specs=in_specs,
        out_specs=[pl.BlockSpec((tm, tn), lambda i, j, kk: (i, j))] * n_out,
        out_shape=[jax.ShapeDtypeStruct((m, n), dt) for dt in out_dtypes],
        scratch_shapes=[pltpu.VMEM((tm, tn), F32)] if nk > 1 else [],
        compiler_params=_params(("parallel", "parallel", "arbitrary")),
    )(*args)


RET_TQ = 512


def _decay(lg_ref, i, tq, seq):
    n_idx = i * tq + lax.broadcasted_iota(jnp.int32, (tq, seq), 0)
    m_idx = lax.broadcasted_iota(jnp.int32, (tq, seq), 1)
    diff = (n_idx - m_idx).astype(F32)
    lgf = lg_ref[0, 0:1, 0:1]
    lgb = lg_ref[0, 1:2, 0:1]
    causal = diff >= 0
    return jnp.exp(jnp.where(causal, lgf * diff, -lgb * diff)), diff, causal


_NT = (((1,), (1,)), ((), ()))
_TN = (((0,), (0,)), ((), ()))


def _ret_fwd(qr, kr, proj, lg, name):
    seq = qr.shape[0]
    tq = RET_TQ
    v_blk0 = (2 * QK_WIDTH) // V_DIM

    def body(q_ref, k_ref, v_ref, lg_ref, y_ref):
        i = pl.program_id(1)
        s = lax.dot_general(q_ref[...], k_ref[...], _NT, preferred_element_type=F32)
        dm, _, _ = _decay(lg_ref, i, tq, seq)
        p = (s * dm).astype(BF16)
        y_ref[...] = jnp.dot(p, v_ref[...].astype(BF16), preferred_element_type=F32)

    return pl.pallas_call(
        body, name=name, grid=(HEADS, seq // tq),
        in_specs=[pl.BlockSpec((tq, QK_DIM), lambda h, i: (i, h)),
                  pl.BlockSpec((seq, QK_DIM), lambda h, i: (0, h)),
                  pl.BlockSpec((seq, V_DIM), lambda h, i: (0, v_blk0 + h)),
                  pl.BlockSpec((1, 2, LANES), lambda h, i: (h, 0, 0))],
        out_specs=pl.BlockSpec((tq, V_DIM), lambda h, i: (i, h)),
        out_shape=jax.ShapeDtypeStruct((seq, HEADS * V_DIM), F32),
        compiler_params=_params(("parallel", "parallel")),
    )(qr, kr, proj, lg)


def _ret_bwd(qr, kr, proj, dy, lg, name):
    seq = qr.shape[0]
    tq = RET_TQ
    v_blk0 = (2 * QK_WIDTH) // V_DIM

    def body(q_ref, k_ref, v_ref, dy_ref, lg_ref, dq_ref, dk_ref, dv_ref, dlg_ref):
        i = pl.program_id(1)

        @pl.when(i == 0)
        def _():
            dk_ref[...] = jnp.zeros(dk_ref.shape, F32)
            dv_ref[...] = jnp.zeros(dv_ref.shape, F32)
            dlg_ref[...] = jnp.zeros(dlg_ref.shape, F32)

        q = q_ref[...]
        k = k_ref[...]
        vb = v_ref[...].astype(BF16)
        dyb = dy_ref[...]
        s = lax.dot_general(q, k, _NT, preferred_element_type=F32)
        dm, diff, causal = _decay(lg_ref, i, tq, seq)
        p = s * dm
        dp = lax.dot_general(dyb, vb, _NT, preferred_element_type=F32)
        dv_ref[...] += lax.dot_general(p.astype(BF16), dyb, _TN, preferred_element_type=F32)
        ds = (dp * dm).astype(BF16)
        dq_ref[...] = jnp.dot(ds, k, preferred_element_type=F32)
        dk_ref[...] += lax.dot_general(ds, q, _TN, preferred_element_type=F32)
        gd = dp * p * diff
        dlf = jnp.sum(jnp.sum(jnp.where(causal, gd, 0.0), axis=1, keepdims=True), axis=0, keepdims=True)
        dlb = jnp.sum(jnp.sum(jnp.where(causal, 0.0, -gd), axis=1, keepdims=True), axis=0, keepdims=True)
        row = lax.broadcasted_iota(jnp.int32, (2, LANES), 0)
        dlg_ref[0] += jnp.where(row == 0, dlf, dlb)

    return pl.pallas_call(
        body, name=name, grid=(HEADS, seq // tq),
        in_specs=[pl.BlockSpec((tq, QK_DIM), lambda h, i: (i, h)),
                  pl.BlockSpec((seq, QK_DIM), lambda h, i: (0, h)),
                  pl.BlockSpec((seq, V_DIM), lambda h, i: (0, v_blk0 + h)),
                  pl.BlockSpec((tq, V_DIM), lambda h, i: (i, h)),
                  pl.BlockSpec((1, 2, LANES), lambda h, i: (h, 0, 0))],
        out_specs=[pl.BlockSpec((tq, QK_DIM), lambda h, i: (i, h)),
                   pl.BlockSpec((seq, QK_DIM), lambda h, i: (0, h)),
                   pl.BlockSpec((seq, V_DIM), lambda h, i: (0, h)),
                   pl.BlockSpec((1, 2, LANES), lambda h, i: (h, 0, 0))],
        out_shape=[jax.ShapeDtypeStruct((seq, QK_WIDTH), F32), jax.ShapeDtypeStruct((seq, QK_WIDTH), F32),
                   jax.ShapeDtypeStruct((seq, HEADS * V_DIM), F32), jax.ShapeDtypeStruct((HEADS, 2, LANES), F32)],
        compiler_params=_params(("parallel", "arbitrary")),
    )(qr, kr, proj, dy, lg)


def _shift_rows(v, reverse):
    row = lax.broadcasted_iota(jnp.int32, v.shape, 0)
    if reverse:
        return jnp.where(row == SEGMENTS - 1, 0.0, pltpu.roll(v, SEGMENTS - 1, 0))
    return jnp.where(row == 0, 0.0, pltpu.roll(v, 1, 0))


def _slab(t):
    if isinstance(t, int):
        return pl.ds(t * SEGMENTS, SEGMENTS)
    return pl.ds(pl.multiple_of(t * SEGMENTS, SEGMENTS), SEGMENTS)


def _unrolled_loop(body, lo, hi, init):
    main = (hi - lo) // SCAN_UNROLL

    def unrolled(g, carry):
        for k in range(SCAN_UNROLL):
            carry = body(lo + g * SCAN_UNROLL + k, carry)
        return carry

    carry = lax.fori_loop(0, main, unrolled, init)
    for t in range(lo + main * SCAN_UNROLL, hi):
        carry = body(t, carry)
    return carry


def _scan(xr_ref, xi_ref, lam, reverse, conj):
    steps = xr_ref.shape[0] // SEGMENTS
    cols = xr_ref.shape[1]
    lr = jnp.broadcast_to(lam[0], (SEGMENTS, cols))
    li = jnp.broadcast_to(lam[1], (SEGMENTS, cols))
    lrt = jnp.broadcast_to(lam[2], (SEGMENTS, cols))
    lit = jnp.broadcast_to(lam[3], (SEGMENTS, cols))
    if conj:
        li, lit = -li, -lit
    zero = jnp.zeros((SEGMENTS, cols), F32)

    def rows_of(t):
        return _slab(steps - 1 - t if reverse else t)

    def advance(t, carry):
        sr, si = carry
        rows = rows_of(t)
        return lr * sr - li * si + xr_ref[rows, :], lr * si + li * sr + xi_ref[rows, :]

    def step(t, carry):
        nr, ni = advance(t, carry)
        rows = rows_of(t)
        xr_ref[rows, :] = nr
        xi_ref[rows, :] = ni
        return nr, ni

    def run(body, init):
        return _unrolled_loop(body, 0, steps, init)

    er, ei = run(advance, (zero, zero))
    cr, ci = zero, zero
    for _ in range(SEGMENTS - 1):
        tr = er + lrt * cr - lit * ci
        ti = ei + lrt * ci + lit * cr
        cr, ci = _shift_rows(tr, reverse), _shift_rows(ti, reverse)
    run(step, (cr, ci))


def _permute_in(dst_ref, src_ref):
    steps = src_ref.shape[0] // SEGMENTS
    for s in range(SEGMENTS):
        dst_ref[pl.ds(s, steps, stride=SEGMENTS), :] = src_ref[s * steps:(s + 1) * steps, :].astype(dst_ref.dtype)


def _unpermute(src_ref, s):
    steps = src_ref.shape[0] // SEGMENTS
    return src_ref[pl.ds(s, steps, stride=SEGMENTS), :]


def _s5_fwd(proj, bblk, cblk, lam, name):
    seq = proj.shape[0]
    u_blk0 = (2 * QK_WIDTH + 2 * D_MODEL) // LANES
    sc = STATE_COLS

    def body(u_ref, b_ref, c_ref, lam_ref, y_ref, up_ref, yp_ref, xr_ref, xi_ref):
        _permute_in(up_ref, u_ref)
        ub = up_ref[...].astype(BF16)
        for d in range(2):
            xr_ref[...] = jnp.dot(ub, b_ref[d, :, 0:sc], preferred_element_type=F32)
            xi_ref[...] = jnp.dot(ub, b_ref[d, :, sc:2 * sc], preferred_element_type=F32)
            lm = [lam_ref[d, r:r + 1, :] for r in range(4)]
            _scan(xr_ref, xi_ref, lm, reverse=(d == 1), conj=False)
            yd = (jnp.dot(xr_ref[...].astype(BF16), c_ref[d, 0:sc, :], preferred_element_type=F32)
                  + jnp.dot(xi_ref[...].astype(BF16), c_ref[d, sc:2 * sc, :], preferred_element_type=F32))
            if d == 0:
                yp_ref[...] = yd
            else:
                yp_ref[...] += yd
        steps = seq // SEGMENTS
        for s in range(SEGMENTS):
            y_ref[s * steps:(s + 1) * steps, :] = _unpermute(yp_ref, s)

    return pl.pallas_call(
        body, name=name, grid=(N_TILES,),
        in_specs=[pl.BlockSpec((seq, LANES), lambda j: (0, u_blk0 + j)),
                  pl.BlockSpec((2, None, LANES, 2 * sc), lambda j: (0, j, 0, 0)),
                  pl.BlockSpec((2, None, 2 * sc, LANES), lambda j: (0, j, 0, 0)),
                  pl.BlockSpec((2, None, 4, sc), lambda j: (0, j, 0, 0))],
        out_specs=pl.BlockSpec((seq, LANES), lambda j: (0, j)),
        out_shape=jax.ShapeDtypeStruct((seq, D_MODEL), F32),
        scratch_shapes=[pltpu.VMEM((seq, LANES), F32), pltpu.VMEM((seq, LANES), F32),
                        pltpu.VMEM((seq, sc), F32), pltpu.VMEM((seq, sc), F32)],
        compiler_params=_params(("parallel",)),
    )(proj, bblk, cblk, lam)


def _s5_bwd(proj, dy, du_part, bblk, cblk, lam, name):
    seq = proj.shape[0]
    u_blk0 = (2 * QK_WIDTH + 2 * D_MODEL) // LANES
    sc = STATE_COLS
    steps = seq // SEGMENTS

    def body(u_ref, dy_ref, dup_ref, b_ref, c_ref, lam_ref, du_ref, db_ref, dc_ref, dlam_ref,
             up_ref, dyp_ref, dua_ref, xr_ref, xi_ref, gr_ref, gi_ref):
        _permute_in(up_ref, u_ref)
        _permute_in(dyp_ref, dy_ref)
        ub = up_ref[...].astype(BF16)
        dyb = dyp_ref[...].astype(BF16)
        ubt = up_ref[...].T.astype(BF16)
        dybt = dyp_ref[...].T.astype(BF16)
        for d in range(2):
            reverse = d == 1
            xr_ref[...] = jnp.dot(ub, b_ref[d, :, 0:sc], preferred_element_type=F32)
            xi_ref[...] = jnp.dot(ub, b_ref[d, :, sc:2 * sc], preferred_element_type=F32)
            lm = [lam_ref[d, r:r + 1, :] for r in range(4)]
            _scan(xr_ref, xi_ref, lm, reverse=reverse, conj=False)
            xrb = xr_ref[...].astype(BF16)
            xib = xi_ref[...].astype(BF16)
            dc_ref[d, :, 0:sc] = jnp.dot(dybt, xrb, preferred_element_type=F32)
            dc_ref[d, :, sc:2 * sc] = jnp.dot(dybt, xib, preferred_element_type=F32)
            gr_ref[...] = lax.dot_general(dyb, c_ref[d, 0:sc, :], _NT, preferred_element_type=F32)
            gi_ref[...] = lax.dot_general(dyb, c_ref[d, sc:2 * sc, :], _NT, preferred_element_type=F32)
            _scan(gr_ref, gi_ref, lm, reverse=not reverse, conj=True)

            def acc_step(t, carry):
                ar, ai = carry
                prev = _slab(t + 1 if reverse else t - 1)
                pr = xr_ref[prev, :]
                pi = xi_ref[prev, :]
                zr = gr_ref[_slab(t), :]
                zi = gi_ref[_slab(t), :]
                return ar + zr * pr + zi * pi, ai + zi * pr - zr * pi

            zero = jnp.zeros((SEGMENTS, sc), F32)
            if reverse:
                ar, ai = _unrolled_loop(acc_step, 0, steps - 1, (zero, zero))
                edge = _slab(steps - 1)
                pr = _shift_rows(xr_ref[_slab(0), :], True)
                pi = _shift_rows(xi_ref[_slab(0), :], True)
            else:
                ar, ai = _unrolled_loop(acc_step, 1, steps, (zero, zero))
                edge = _slab(0)
                pr = _shift_rows(xr_ref[_slab(steps - 1), :], False)
                pi = _shift_rows(xi_ref[_slab(steps - 1), :], False)
            zr = gr_ref[edge, :]
            zi = gi_ref[edge, :]
            ar = ar + zr * pr + zi * pi
            ai = ai + zi * pr - zr * pi
            dlam_ref[d, 0:1, :] = jnp.sum(ar, axis=0, keepdims=True)
            dlam_ref[d, 1:2, :] = jnp.sum(ai, axis=0, keepdims=True)

            grb = gr_ref[...].astype(BF16)
            gib = gi_ref[...].astype(BF16)
            db_ref[d, :, 0:sc] = jnp.dot(ubt, grb, preferred_element_type=F32)
            db_ref[d, :, sc:2 * sc] = jnp.dot(ubt, gib, preferred_element_type=F32)
            dud = (lax.dot_general(grb, b_ref[d, :, 0:sc], _NT, preferred_element_type=F32)
                   + lax.dot_general(gib, b_ref[d, :, sc:2 * sc], _NT, preferred_element_type=F32))
            if d == 0:
                dua_ref[...] = dud
            else:
                dua_ref[...] += dud
        for s in range(SEGMENTS):
            rows = slice(s * steps, (s + 1) * steps)
            du_ref[rows, :] = (_unpermute(dua_ref, s) + dup_ref[rows, :]).astype(du_ref.dtype)

    return pl.pallas_call(
        body, name=name, grid=(N_TILES,),
        in_specs=[pl.BlockSpec((seq, LANES), lambda j: (0, u_blk0 + j)),
                  pl.BlockSpec((seq, LANES), lambda j: (0, j)),
                  pl.BlockSpec((seq, LANES), lambda j: (0, j)),
                  pl.BlockSpec((2, None, LANES, 2 * sc), lambda j: (0, j, 0, 0)),
                  pl.BlockSpec((2, None, 2 * sc, LANES), lambda j: (0, j, 0, 0)),
                  pl.BlockSpec((2, None, 4, sc), lambda j: (0, j, 0, 0))],
        out_specs=[pl.BlockSpec((seq, LANES), lambda j: (0, j)),
                   pl.BlockSpec((2, None, LANES, 2 * sc), lambda j: (0, j, 0, 0)),
                   pl.BlockSpec((2, None, LANES, 2 * sc), lambda j: (0, j, 0, 0)),
                   pl.BlockSpec((2, None, 2, sc), lambda j: (0, j, 0, 0))],
        out_shape=[jax.ShapeDtypeStruct((seq, D_MODEL), BF16),
                   jax.ShapeDtypeStruct((2, N_TILES, LANES, 2 * sc), F32),
                   jax.ShapeDtypeStruct((2, N_TILES, LANES, 2 * sc), F32),
                   jax.ShapeDtypeStruct((2, N_TILES, 2, sc), F32)],
        scratch_shapes=[pltpu.VMEM((seq, LANES), F32), pltpu.VMEM((seq, LANES), F32), pltpu.VMEM((seq, LANES), F32),
                        pltpu.VMEM((seq, sc), F32), pltpu.VMEM((seq, sc), F32),
                        pltpu.VMEM((seq, sc), F32), pltpu.VMEM((seq, sc), F32)],
        compiler_params=_params(("parallel",)),
    )(proj, dy, du_part, bblk, cblk, lam)


def _s5_discretize(a_re, a_im, log_dt, b_re, b_im, seg_len):
    dt = jnp.exp(log_dt)[..., None]
    e = jnp.exp(a_re * dt)
    lr, li = e * jnp.cos(a_im * dt), e * jnp.sin(a_im * dt)
    et = jnp.exp(a_re * dt * seg_len)
    lrt, lit = et * jnp.cos(a_im * dt * seg_len), et * jnp.sin(a_im * dt * seg_len)
    den = a_re * a_re + a_im * a_im
    qr = ((lr - 1.0) * a_re + li * a_im) / den
    qi = (li * a_re - (lr - 1.0) * a_im) / den
    br = qr[..., None] * b_re - qi[..., None] * b_im
    bi = qr[..., None] * b_im + qi[..., None] * b_re
    return lr, li, lrt, lit, br, bi


def _s5_pack(lr, li, lrt, lit, br, bi, c_re, c_im):
    eye = jnp.eye(GROUPS_PER_TILE, dtype=F32)

    def bd_b(b):
        b5 = b.reshape(2, N_TILES, GROUPS_PER_TILE, N_STATE, GROUP)
        return jnp.einsum("dtgph,gk->dtghkp", b5, eye).reshape(2, N_TILES, LANES, STATE_COLS)

    def bd_c(c):
        c5 = c.reshape(2, N_TILES, GROUPS_PER_TILE, GROUP, N_STATE)
        return jnp.einsum("dtghp,gk->dtkpgh", c5, eye).reshape(2, N_TILES, STATE_COLS, LANES)

    bblk = jnp.concatenate([bd_b(br), bd_b(bi)], axis=3)
    cblk = jnp.concatenate([bd_c(c_re), -bd_c(c_im)], axis=2)
    lam = jnp.stack([v.reshape(2, N_TILES, STATE_COLS) for v in (lr, li, lrt, lit)], axis=2)
    return bblk, cblk, lam


def _s5_unpack(dbblk, dcblk, dlam):
    eye = jnp.eye(GROUPS_PER_TILE, dtype=F32)

    def diag_b(d):
        d6 = d.reshape(2, N_TILES, GROUPS_PER_TILE, GROUP, GROUPS_PER_TILE, N_STATE)
        return jnp.einsum("dtghkp,gk->dtgph", d6, eye).reshape(2, N_GROUPS, N_STATE, GROUP)

    def diag_c(d):
        d6 = d.reshape(2, N_TILES, GROUPS_PER_TILE, GROUP, GROUPS_PER_TILE, N_STATE)
        return jnp.einsum("dtghkp,gk->dtghp", d6, eye).reshape(2, N_GROUPS, GROUP, N_STATE)

    dbr, dbi = diag_b(dbblk[..., :STATE_COLS]), diag_b(dbblk[..., STATE_COLS:])
    dcr, dci = diag_c(dcblk[..., :STATE_COLS]), -diag_c(dcblk[..., STATE_COLS:])
    dlr = dlam[:, :, 0, :].reshape(2, N_GROUPS, N_STATE)
    dli = dlam[:, :, 1, :].reshape(2, N_GROUPS, N_STATE)
    return dlr, dli, dbr, dbi, dcr, dci


def _pos():
    return lax.axis_index("x"), lax.axis_index("y"), lax.axis_index("c")


def _remote(src, dst, ssem, rsem, dev):
    return pltpu.make_async_remote_copy(src_ref=src, dst_ref=dst, send_sem=ssem, recv_sem=rsem,
                                        device_id=dev, device_id_type=MESH)


_PIECES = (
    ("w_in", "in", D_MODEL, IN_WIDTH // N_CHIPS, 0, IN_WIDTH // N_CHIPS, 0),
    ("w_glu", "glu", D_MODEL // N_CHIPS, D_MODEL, D_MODEL // N_CHIPS, 0, 0),
    ("w_out", "out", D_MODEL // N_CHIPS, D_MODEL, D_MODEL // N_CHIPS, 0, 0),
    ("w_ffn_gate", "gu", D_MODEL, D_FF // N_CHIPS, 0, D_FF // N_CHIPS, 0),
    ("w_ffn_up", "gu", D_MODEL, D_FF // N_CHIPS, 0, D_FF // N_CHIPS, D_FF),
    ("w_ffn_down", "down", D_FF // N_CHIPS, D_MODEL, D_FF // N_CHIPS, 0, 0),
)
_BUFFERS = (("in", D_MODEL, IN_WIDTH), ("glu", D_MODEL, D_MODEL), ("out", D_MODEL, D_MODEL),
            ("gu", D_MODEL, 2 * D_FF), ("down", D_FF, D_MODEL))
_BUF_INDEX = {name: t for t, (name, _, _) in enumerate(_BUFFERS)}
N_PIECES = len(_PIECES)
N_BUFFERS = len(_BUFFERS)


def _own_block(piece, tm):
    _, _, _, cs, rstep, cstep, coff = piece
    return lambda i, chip: (i + chip * (rstep // tm), coff // cs + chip * (cstep // cs))


def _cast_place(piece, w3, layer, prev, chip_arr, name):
    _, r, cc = w3.shape
    _, rf, cf = _BUFFERS[_BUF_INDEX[piece[1]]]
    tm = _tile(r, 256)
    own = _own_block(piece, tm)

    def body(s_ref, w_ref, *rest):
        rest[-1][...] = w_ref[...].astype(BF16)

    in_specs = [pl.BlockSpec((None, tm, cc), lambda i, s: (layer, i, 0))]
    args = [w3]
    aliases = {}
    if prev is not None:
        in_specs.append(pl.BlockSpec(memory_space=pl.ANY))
        args.append(prev)
        aliases = {2: 0}
    return pl.pallas_call(
        body, name=name,
        grid_spec=pltpu.PrefetchScalarGridSpec(
            num_scalar_prefetch=1, grid=(r // tm,), in_specs=in_specs,
            out_specs=pl.BlockSpec((tm, cc), lambda i, s: own(i, s[0]))),
        out_shape=jax.ShapeDtypeStruct((rf, cf), BF16), input_output_aliases=aliases,
        compiler_params=_params(("parallel",)),
    )(chip_arr, *args)


_GATHER_GROUPS = ((0, (0,)), (0, (1, 2, 3, 4, 5)), (1, (0,)), (1, (1, 2, 3, 4, 5)))
_SPLIT_EFFECT = pltpu.SideEffectType.DATAFLOW_SIDE_EFFECTING
SEM_SPEC = pl.BlockSpec(memory_space=pltpu.SEMAPHORE)
BF16_ROWS = 2 * SUBLANES


def _group_keys(g):
    layer, pieces = _GATHER_GROUPS[g]
    keys = []
    for p in pieces:
        if (_PIECES[p][1], layer) not in keys:
            keys.append((_PIECES[p][1], layer))
    return keys


def _half_view(ref, piece, j, c):
    _, _, rs, cs, rstep, cstep, coff = piece
    half = rs // 2
    return ref.at[pl.ds(pl.multiple_of(j * rstep + c * half, BF16_ROWS), half), pl.ds(coff + j * cstep, cs)]


def _for_my_chip(fn):
    x, y, _ = _pos()
    for mine in range(N_CHIPS):
        pl.when(2 * x + y == mine)(functools.partial(fn, mine, [j for j in range(N_CHIPS) if j != mine]))


def _gather_start(groups, placed):
    keys = [k for g in groups for k in _group_keys(g)]
    nb, ng = len(keys), len(groups)

    def body(*refs):
        bufs = dict(zip(keys, refs[nb:2 * nb]))
        ssems = refs[2 * nb:2 * nb + ng]
        rsems = refs[2 * nb + ng:2 * nb + 2 * ng]
        token = refs[2 * nb + 2 * ng]
        _, _, c = _pos()

        def send(mine, others):
            for t, g in enumerate(groups):
                layer, pieces = _GATHER_GROUPS[g]
                for k, p in enumerate(pieces):
                    view = _half_view(bufs[(_PIECES[p][1], layer)], _PIECES[p], mine, c)
                    for j in others:
                        _remote(view, view, ssems[t].at[k * N_CHIPS + j], rsems[t].at[k * N_CHIPS + mine],
                                (j // 2, j % 2, c)).start()

        _for_my_chip(send)
        token[...] = jnp.zeros(token.shape, token.dtype)

    sems = [pltpu.SemaphoreType.DMA((N_CHIPS * len(_GATHER_GROUPS[g][1]),)) for g in groups]
    shapes = [jax.ShapeDtypeStruct(placed[k].shape, placed[k].dtype) for k in keys]
    res = pl.pallas_call(
        body, name="gather_start_g%d" % groups[0],
        in_specs=[HBM_SPEC] * nb,
        out_specs=[HBM_SPEC] * nb + [SEM_SPEC] * (2 * ng) + [pl.BlockSpec(memory_space=pltpu.VMEM)],
        out_shape=shapes + sems + sems + [jax.ShapeDtypeStruct((SUBLANES, LANES), F32)],
        input_output_aliases={t: t for t in range(nb)},
        compiler_params=_params(has_side_effects=_SPLIT_EFFECT),
    )(*[pltpu.with_memory_space_constraint(placed[k], pltpu.HBM) for k in keys])
    return (dict(zip(keys, res[:nb])), dict(zip(groups, res[nb:nb + ng])),
            dict(zip(groups, res[nb + ng:nb + 2 * ng])), res[nb + 2 * ng])


def _gather_wait(g, bufs, ssem, rsem, after):
    layer, pieces = _GATHER_GROUPS[g]
    keys = _group_keys(g)
    nb = len(keys)

    def body(*refs):
        ssem_ref, rsem_ref = refs[nb], refs[nb + 1]
        land = dict(zip(keys, refs[nb + 3:]))
        _, _, c = _pos()

        def wait(mine, others):
            for k, p in enumerate(pieces):
                ref = land[(_PIECES[p][1], layer)]
                for j in others:
                    cp = _remote(_half_view(ref, _PIECES[p], mine, c), _half_view(ref, _PIECES[p], j, c),
                                 ssem_ref.at[k * N_CHIPS + j], rsem_ref.at[k * N_CHIPS + j], (j // 2, j % 2, c))
                    cp.wait_send()
                    cp.wait_recv()

        _for_my_chip(wait)

    return pl.pallas_call(
        body, name="gather_wait_g%d" % g,
        in_specs=[HBM_SPEC] * nb + [SEM_SPEC, SEM_SPEC, pl.BlockSpec(memory_space=pl.ANY)],
        out_specs=[HBM_SPEC] * nb,
        out_shape=[jax.ShapeDtypeStruct(a.shape, a.dtype) for a in bufs],
        input_output_aliases={t: t for t in range(nb)},
        compiler_params=_params(has_side_effects=_SPLIT_EFFECT),
    )(*bufs, ssem, rsem, after)


def _gather_forward(g, bufs):
    layer, pieces = _GATHER_GROUPS[g]
    keys = _group_keys(g)
    nb = len(keys)

    def body(*refs):
        land = dict(zip(keys, refs[nb:2 * nb]))
        ssem, rsem = refs[2 * nb:]
        x, y, c = _pos()

        def forward(mine, others):
            cps = []
            for k, p in enumerate(pieces):
                ref = land[(_PIECES[p][1], layer)]
                for j in others:
                    view = _half_view(ref, _PIECES[p], j, c)
                    cp = _remote(view, view, ssem.at[k * N_CHIPS + j], rsem.at[k * N_CHIPS + j], (x, y, 1 - c))
                    cp.start()
                    cps.append(cp)
            for k, p in enumerate(pieces):
                ref = land[(_PIECES[p][1], layer)]
                for j in others:
                    view = _half_view(ref, _PIECES[p], j, 1 - c)
                    _remote(view, view, ssem.at[k * N_CHIPS + j], rsem.at[k * N_CHIPS + j], (x, y, 1 - c)).wait_recv()
            for cp in cps:
                cp.wait_send()

        _for_my_chip(forward)

    nsem = N_CHIPS * len(pieces)
    return pl.pallas_call(
        body, name="gather_forward_g%d" % g,
        in_specs=[HBM_SPEC] * nb, out_specs=[HBM_SPEC] * nb,
        out_shape=[jax.ShapeDtypeStruct(a.shape, a.dtype) for a in bufs],
        input_output_aliases={t: t for t in range(nb)},
        scratch_shapes=[pltpu.SemaphoreType.DMA((nsem,)), pltpu.SemaphoreType.DMA((nsem,))],
        compiler_params=_params(has_side_effects=True),
    )(*bufs)


def _forward_start(g, bufs):
    layer, pieces = _GATHER_GROUPS[g]
    keys = _group_keys(g)
    nb = len(keys)

    def body(*refs):
        land = dict(zip(keys, refs[nb:2 * nb]))
        ssem, rsem, token = refs[2 * nb:]
        x, y, c = _pos()

        def forward(mine, others):
            for k, p in enumerate(pieces):
                for j in others:
                    view = _half_view(land[(_PIECES[p][1], layer)], _PIECES[p], j, c)
                    _remote(view, view, ssem.at[k * N_CHIPS + j], rsem.at[k * N_CHIPS + j], (x, y, 1 - c)).start()

        _for_my_chip(forward)
        token[...] = jnp.zeros(token.shape, token.dtype)

    sem = pltpu.SemaphoreType.DMA((N_CHIPS * len(pieces),))
    res = pl.pallas_call(
        body, name="forward_start_g%d" % g,
        in_specs=[HBM_SPEC] * nb,
        out_specs=[HBM_SPEC] * nb + [SEM_SPEC, SEM_SPEC, pl.BlockSpec(memory_space=pltpu.VMEM)],
        out_shape=[jax.ShapeDtypeStruct(a.shape, a.dtype) for a in bufs]
        + [sem, sem, jax.ShapeDtypeStruct((SUBLANES, LANES), F32)],
        input_output_aliases={t: t for t in range(nb)},
        compiler_params=_params(has_side_effects=_SPLIT_EFFECT),
    )(*bufs)
    return list(res[:nb]), res[nb], res[nb + 1], res[nb + 2]


def _forward_wait(g, bufs, ssem, rsem, after):
    layer, pieces = _GATHER_GROUPS[g]
    keys = _group_keys(g)
    nb = len(keys)

    def body(*refs):
        ssem_ref, rsem_ref = refs[nb], refs[nb + 1]
        land = dict(zip(keys, refs[nb + 3:]))
        x, y, c = _pos()

        def wait(mine, others):
            for k, p in enumerate(pieces):
                ref = land[(_PIECES[p][1], layer)]
                for j in others:
                    cp = _remote(_half_view(ref, _PIECES[p], j, c), _half_view(ref, _PIECES[p], j, 1 - c),
                                 ssem_ref.at[k * N_CHIPS + j], rsem_ref.at[k * N_CHIPS + j], (x, y, 1 - c))
                    cp.wait_send()
                    cp.wait_recv()

        _for_my_chip(wait)

    return pl.pallas_call(
        body, name="forward_wait_g%d" % g,
        in_specs=[HBM_SPEC] * nb + [SEM_SPEC, SEM_SPEC, pl.BlockSpec(memory_space=pl.ANY)],
        out_specs=[HBM_SPEC] * nb,
        out_shape=[jax.ShapeDtypeStruct(a.shape, a.dtype) for a in bufs],
        input_output_aliases={t: t for t in range(nb)},
        compiler_params=_params(has_side_effects=_SPLIT_EFFECT),
    )(*bufs, ssem, rsem, after)


_REDUCE_GROUPS = (
    ((5, 1), (3, 1), (4, 1), (2, 1), (1, 1), (0, 1)),
    ((5, 0), (3, 0), (4, 0)),
    ((2, 0), (1, 0)),
    ((0, 0),),
)


def _reduce_keys(group):
    keys = []
    for p, layer in group:
        if (_PIECES[p][1], layer) not in keys:
            keys.append((_PIECES[p][1], layer))
    return keys


def _half_block(piece, tm):
    _, _, rs, cs, rstep, cstep, coff = piece
    return lambda i, j, c: (j * (rstep // tm) + c * (rs // 2 // tm) + i, coff // cs + j * (cstep // cs))


def _swap_start(g, dwb):
    group = _REDUCE_GROUPS[g]
    keys = _reduce_keys(group)
    nk = len(keys)

    def body(*refs):
        src = dict(zip(keys, refs[nk:2 * nk]))
        dst = dict(zip(keys, refs[2 * nk:3 * nk]))
        ssem, rsem, token = refs[3 * nk:]
        x, y, c = _pos()
        for k, (p, layer) in enumerate(group):
            key = (_PIECES[p][1], layer)
            for j in range(N_CHIPS):
                _remote(_half_view(src[key], _PIECES[p], j, 1 - c), _half_view(dst[key], _PIECES[p], j, 1 - c),
                        ssem.at[k * N_CHIPS + j], rsem.at[k * N_CHIPS + j], (x, y, 1 - c)).start()
        token[...] = jnp.zeros(token.shape, token.dtype)

    sem = pltpu.SemaphoreType.DMA((N_CHIPS * len(group),))
    shapes = [jax.ShapeDtypeStruct(dwb[k].shape, BF16) for k in keys]
    res = pl.pallas_call(
        body, name="swap_start_g%d" % g,
        in_specs=[HBM_SPEC] * nk,
        out_specs=[HBM_SPEC] * (2 * nk) + [SEM_SPEC, SEM_SPEC, pl.BlockSpec(memory_space=pltpu.VMEM)],
        out_shape=shapes + shapes + [sem, sem, jax.ShapeDtypeStruct((SUBLANES, LANES), F32)],
        input_output_aliases={t: t for t in range(nk)},
        compiler_params=_params(has_side_effects=_SPLIT_EFFECT),
    )(*[pltpu.with_memory_space_constraint(dwb[k], pltpu.HBM) for k in keys])
    return list(res[:nk]), list(res[nk:2 * nk]), res[2 * nk], res[2 * nk + 1], res[2 * nk + 2]


def _swap_wait(g, own, land, ssem, rsem, after):
    group = _REDUCE_GROUPS[g]
    keys = _reduce_keys(group)
    nk = len(keys)

    def body(*refs):
        ssem_ref, rsem_ref = refs[2 * nk], refs[2 * nk + 1]
        src = dict(zip(keys, refs[2 * nk + 3:3 * nk + 3]))
        dst = dict(zip(keys, refs[3 * nk + 3:]))
        x, y, c = _pos()
        for k, (p, layer) in enumerate(group):
            key = (_PIECES[p][1], layer)
            for j in range(N_CHIPS):
                cp = _remote(_half_view(src[key], _PIECES[p], j, 1 - c), _half_view(dst[key], _PIECES[p], j, c),
                             ssem_ref.at[k * N_CHIPS + j], rsem_ref.at[k * N_CHIPS + j], (x, y, 1 - c))
                cp.wait_send()
                cp.wait_recv()

    res = pl.pallas_call(
        body, name="swap_wait_g%d" % g,
        in_specs=[HBM_SPEC] * (2 * nk) + [SEM_SPEC, SEM_SPEC, pl.BlockSpec(memory_space=pl.ANY)],
        out_specs=[HBM_SPEC] * (2 * nk),
        out_shape=[jax.ShapeDtypeStruct(a.shape, a.dtype) for a in list(own) + list(land)],
        input_output_aliases={t: t for t in range(2 * nk)},
        compiler_params=_params(has_side_effects=_SPLIT_EFFECT),
    )(*own, *land, ssem, rsem, after)
    return dict(zip(keys, res[nk:]))


def _chip_partial(piece, dw, got, prev, c_arr, name):
    _, _, rs, cs, _, _, _ = piece
    half = rs // 2
    tm = _tile(half, 256)
    blk = _half_block(piece, tm)

    def body(s_ref, dw_ref, got_ref, *rest):
        rest[-1][...] = (dw_ref[...] + got_ref[...].astype(F32)).astype(BF16)

    spec = pl.BlockSpec((tm, cs), lambda j, i, s: blk(i, j, s[0]))
    in_specs = [spec, spec]
    args = [dw, got]
    aliases = {}
    if prev is not None:
        in_specs.append(pl.BlockSpec(memory_space=pl.ANY))
        args.append(prev)
        aliases = {3: 0}
    return pl.pallas_call(
        body, name=name,
        grid_spec=pltpu.PrefetchScalarGridSpec(
            num_scalar_prefetch=1, grid=(N_CHIPS, half // tm), in_specs=in_specs, out_specs=spec),
        out_shape=jax.ShapeDtypeStruct(dw.shape, BF16), input_output_aliases=aliases,
        compiler_params=_params(("parallel", "parallel")),
    )(c_arr, *args)


def _scatter_start(g, partials):
    group = _REDUCE_GROUPS[g]
    keys = _reduce_keys(group)
    nk, n = len(keys), len(group)

    def body(*refs):
        pt = dict(zip(keys, refs[nk:2 * nk]))
        land = refs[2 * nk:2 * nk + n]
        ssem, rsem, token = refs[2 * nk + n:]
        _, _, c = _pos()

        def send(mine, others):
            for k, (p, layer) in enumerate(group):
                for j in others:
                    _remote(_half_view(pt[(_PIECES[p][1], layer)], _PIECES[p], j, c), land[k].at[mine],
                            ssem.at[k * N_CHIPS + j], rsem.at[k * N_CHIPS + mine], (j // 2, j % 2, c)).start()

        _for_my_chip(send)
        token[...] = jnp.zeros(token.shape, token.dtype)

    sem = pltpu.SemaphoreType.DMA((N_CHIPS * n,))
    res = pl.pallas_call(
        body, name="scatter_start_g%d" % g,
        in_specs=[HBM_SPEC] * nk,
        out_specs=[HBM_SPEC] * (nk + n) + [SEM_SPEC, SEM_SPEC, pl.BlockSpec(memory_space=pltpu.VMEM)],
        out_shape=([jax.ShapeDtypeStruct(partials[k].shape, BF16) for k in keys]
                   + [jax.ShapeDtypeStruct((N_CHIPS, _PIECES[p][2] // 2, _PIECES[p][3]), BF16) for p, _ in group]
                   + [sem, sem, jax.ShapeDtypeStruct((SUBLANES, LANES), F32)]),
        input_output_aliases={t: t for t in range(nk)},
        compiler_params=_params(has_side_effects=_SPLIT_EFFECT),
    )(*[pltpu.with_memory_space_constraint(partials[k], pltpu.HBM) for k in keys])
    return list(res[:nk]), list(res[nk:nk + n]), res[nk + n], res[nk + n + 1], res[nk + n + 2]


def _scatter_wait(g, partials, land, ssem, rsem, after):
    group = _REDUCE_GROUPS[g]
    keys = _reduce_keys(group)
    nk, n = len(keys), len(group)

    def body(*refs):
        ssem_ref, rsem_ref = refs[nk + n], refs[nk + n + 1]
        pt = dict(zip(keys, refs[nk + n + 3:2 * nk + n + 3]))
        land_ref = refs[2 * nk + n + 3:]
        _, _, c = _pos()

        def wait(mine, others):
            for k, (p, layer) in enumerate(group):
                for j in others:
                    cp = _remote(_half_view(pt[(_PIECES[p][1], layer)], _PIECES[p], j, c), land_ref[k].at[j],
                                 ssem_ref.at[k * N_CHIPS + j], rsem_ref.at[k * N_CHIPS + j], (j // 2, j % 2, c))
                    cp.wait_send()
                    cp.wait_recv()

        _for_my_chip(wait)

    res = pl.pallas_call(
        body, name="scatter_wait_g%d" % g,
        in_specs=[HBM_SPEC] * (nk + n) + [SEM_SPEC, SEM_SPEC, pl.BlockSpec(memory_space=pl.ANY)],
        out_specs=[HBM_SPEC] * (nk + n),
        out_shape=[jax.ShapeDtypeStruct(a.shape, a.dtype) for a in list(partials) + list(land)],
        input_output_aliases={t: t for t in range(nk + n)},
        compiler_params=_params(has_side_effects=_SPLIT_EFFECT),
    )(*partials, *land, ssem, rsem, after)
    return list(res[nk:])


def _reduce_half(piece, layer, dw, got, land, prev, idx, name):
    _, _, rs, cs, _, _, _ = piece
    half = rs // 2
    tm = _tile(half, 256)
    blk = _half_block(piece, tm)

    def body(s_ref, dw_ref, got_ref, r1, r2, r3, *rest):
        acc = dw_ref[...] + got_ref[...].astype(F32)
        for r in (r1, r2, r3):
            acc = acc + r[...].astype(F32)
        rest[-1][...] = acc

    def land_map(k):
        return lambda i, s: ((s[1] + k) % N_CHIPS, i, 0)

    own = pl.BlockSpec((tm, cs), lambda i, s: blk(i, s[1], s[0]))
    in_specs = [own, own] + [pl.BlockSpec((None, tm, cs), land_map(k)) for k in (1, 2, 3)]
    args = [dw, got, land, land, land]
    aliases = {}
    if prev is not None:
        in_specs.append(pl.BlockSpec(memory_space=pl.ANY))
        args.append(prev)
        aliases = {6: 0}
    return pl.pallas_call(
        body, name=name,
        grid_spec=pltpu.PrefetchScalarGridSpec(
            num_scalar_prefetch=1, grid=(half // tm,), in_specs=in_specs,
            out_specs=pl.BlockSpec((None, tm, cs), lambda i, s: (layer, s[0] * (half // tm) + i, 0))),
        out_shape=jax.ShapeDtypeStruct((DEPTH, rs, cs), F32), input_output_aliases=aliases,
        compiler_params=_params(("parallel",)),
    )(idx, *args)


def _share_halves(reduced):
    def body(*refs):
        buf = refs[N_PIECES:2 * N_PIECES]
        ssem, rsem = refs[2 * N_PIECES:]
        x, y, c = _pos()

        def half(p, layer, cc):
            rows = _PIECES[p][2] // 2
            return buf[p].at[layer, pl.ds(pl.multiple_of(cc * rows, SUBLANES), rows), :]

        pairs = [(p, layer) for p in range(N_PIECES) for layer in range(DEPTH)]
        rem = [_remote(half(p, layer, c), half(p, layer, c), ssem.at[k], rsem.at[k], (x, y, 1 - c))
               for k, (p, layer) in enumerate(pairs)]
        for cp in rem:
            cp.start()
        for k, (p, layer) in enumerate(pairs):
            rem[k].wait_send()
            _remote(half(p, layer, 1 - c), half(p, layer, 1 - c), ssem.at[k], rsem.at[k], (x, y, 1 - c)).wait_recv()

    nsem = N_PIECES * DEPTH
    return pl.pallas_call(
        body, name="share_halves",
        in_specs=[HBM_SPEC] * N_PIECES, out_specs=[HBM_SPEC] * N_PIECES,
        out_shape=[jax.ShapeDtypeStruct((DEPTH, p[2], p[3]), F32) for p in _PIECES],
        input_output_aliases={t: t for t in range(N_PIECES)},
        scratch_shapes=[pltpu.SemaphoreType.DMA((nsem,)), pltpu.SemaphoreType.DMA((nsem,))],
        compiler_params=_params(has_side_effects=True),
    )(*reduced)


N_DEV = 8


def _place_slot(v, me_arr, take_block):
    rows = v.shape[0] // N_DEV if take_block else v.shape[0]
    tm = _tile(rows, 512)
    steps = rows // tm

    def body(s_ref, v_ref, out_ref):
        out_ref[...] = v_ref[...]

    return pl.pallas_call(
        body, name="place_small_block" if take_block else "place_small_sum",
        grid_spec=pltpu.PrefetchScalarGridSpec(
            num_scalar_prefetch=1, grid=(steps,),
            in_specs=[pl.BlockSpec((tm, LANES), lambda i, s: (s[0] * steps * take_block + i, 0))],
            out_specs=pl.BlockSpec((None, tm, LANES), lambda i, s: (s[0], i, 0))),
        out_shape=jax.ShapeDtypeStruct((N_DEV, rows, LANES), F32),
        compiler_params=_params(("parallel",)),
    )(me_arr, v)


def _all_peers():
    x, y, c = _pos()
    flip = lambda v, f: 1 - v if f else v
    return (x, y, c), [(flip(x, a), flip(y, b), flip(c, d))
                       for a in (0, 1) for b in (0, 1) for d in (0, 1) if a or b or d]


def _slot_index(dev):
    return 4 * dev[0] + 2 * dev[1] + dev[2]


def _exchange_start(g, src, name):
    rows = g.shape[1]
    n_in = 1 if src is None else 2

    def body(*refs):
        g_ref = refs[n_in]
        src_ref = refs[n_in + 1] if src is not None else None
        ssem, rsem, token = refs[2 * n_in:]
        me, peers = _all_peers()
        for k, dev in enumerate(peers):
            if src is None:
                mine = g_ref.at[_slot_index(me)]
            else:
                mine = src_ref.at[pl.ds(pl.multiple_of(_slot_index(dev) * rows, SUBLANES), rows), :]
            _remote(mine, g_ref.at[_slot_index(me)], ssem.at[k], rsem.at[k], dev).start()
        token[...] = jnp.zeros(token.shape, token.dtype)

    sem = pltpu.SemaphoreType.DMA((N_DEV - 1,))
    args = [g] if src is None else [g, src]
    res = pl.pallas_call(
        body, name=name,
        in_specs=[HBM_SPEC] * n_in,
        out_specs=[HBM_SPEC] * n_in + [SEM_SPEC, SEM_SPEC, pl.BlockSpec(memory_space=pltpu.VMEM)],
        out_shape=[jax.ShapeDtypeStruct(a.shape, a.dtype) for a in args]
        + [sem, sem, jax.ShapeDtypeStruct((SUBLANES, LANES), F32)],
        input_output_aliases={t: t for t in range(n_in)},
        compiler_params=_params(has_side_effects=_SPLIT_EFFECT),
    )(*[pltpu.with_memory_space_constraint(a, pltpu.HBM) for a in args])
    return list(res[:n_in]), res[n_in], res[n_in + 1], res[n_in + 2]


def _exchange_wait(bufs, ssem, rsem, after, name):
    n_in = len(bufs)

    def body(*refs):
        ssem_ref, rsem_ref = refs[n_in], refs[n_in + 1]
        g_ref = refs[n_in + 3]
        me, peers = _all_peers()
        for k, dev in enumerate(peers):
            cp = _remote(g_ref.at[_slot_index(me)], g_ref.at[_slot_index(dev)], ssem_ref.at[k], rsem_ref.at[k], dev)
            cp.wait_send()
            cp.wait_recv()

    res = pl.pallas_call(
        body, name=name,
        in_specs=[HBM_SPEC] * n_in + [SEM_SPEC, SEM_SPEC, pl.BlockSpec(memory_space=pl.ANY)],
        out_specs=[HBM_SPEC] * n_in,
        out_shape=[jax.ShapeDtypeStruct(a.shape, a.dtype) for a in bufs],
        input_output_aliases={t: t for t in range(n_in)},
        compiler_params=_params(has_side_effects=_SPLIT_EFFECT),
    )(*bufs, ssem, rsem, after)
    return res[0]


def _sum_slots(g, name):
    n, rows, _ = g.shape
    tm = _tile(rows, 512)

    def body(g_ref, out_ref):
        acc = g_ref[0]
        for k in range(1, n):
            acc = acc + g_ref[k]
        out_ref[...] = acc

    return pl.pallas_call(
        body, name=name, grid=(rows // tm,),
        in_specs=[pl.BlockSpec((n, tm, LANES), lambda i: (0, i, 0))],
        out_specs=pl.BlockSpec((tm, LANES), lambda i: (i, 0)),
        out_shape=jax.ShapeDtypeStruct((rows, LANES), F32),
        compiler_params=_params(("parallel",)),
    )(g)


_TINY = ("ln_mix_g", "ret_log_gamma", "ssm_a_re", "ssm_a_im", "ssm_log_dt", "ssm_d", "b_glu", "ln_ffn_g", "ln_final_g")
_MID = ("ssm_b_re", "ssm_b_im", "ssm_c_re", "ssm_c_im")
_SMALL = _TINY + _MID
_FLAT_ALIGN = LANES * LANES
_FLAT_ROWS = 1024


def _flat_rows(like, names):
    rows = sum((math.prod(like[n].shape) + (-math.prod(like[n].shape)) % _FLAT_ALIGN) // LANES for n in names)
    return rows + (-rows) % _FLAT_ROWS


def _flatten(d, names):
    parts = []
    for n in names:
        f = d[n].reshape(-1)
        parts.append(jnp.pad(f, (0, (-f.shape[0]) % _FLAT_ALIGN)))
    total = sum(p.shape[0] for p in parts)
    parts.append(jnp.zeros(((-total) % (_FLAT_ROWS * LANES),), F32))
    return jnp.concatenate(parts).reshape(-1, LANES)


def _unflatten(flat, like, names):
    out, row = {}, 0
    for n in names:
        size = math.prod(like[n].shape)
        rows = (size + (-size) % _FLAT_ALIGN) // LANES
        part = lax.optimization_barrier(flat[row:row + rows])
        out[n] = part.reshape(-1)[:size].reshape(like[n].shape)
        row += rows
    return out


_BIG = ("w_in", "w_glu", "w_out", "w_ffn_gate", "w_ffn_up", "w_ffn_down")
_WEIGHTS = ("ln_mix_g", "w_in", "ret_log_gamma", "ssm_a_re", "ssm_a_im", "ssm_log_dt", "ssm_b_re", "ssm_b_im",
            "ssm_c_re", "ssm_c_im", "ssm_d", "w_glu", "b_glu", "w_out", "ln_ffn_g", "w_ffn_gate", "w_ffn_up",
            "w_ffn_down", "ln_final_g")


def _rope_tables(seq):
    half = QK_DIM // 2
    inv = 1.0 / (ROPE_BASE ** (jnp.arange(half, dtype=F32) / half))
    ang = jnp.arange(seq, dtype=F32)[:, None] * inv[None, :]
    return jnp.cos(ang), jnp.sin(ang)


def _step(w, m, v, x, target):
    seq = x.shape[0]
    seg_len = float(seq // SEGMENTS)
    c_idx = lax.axis_index("c").astype(jnp.int32)
    chip_idx = (2 * lax.axis_index("x") + lax.axis_index("y")).astype(jnp.int32)
    c_arr = jnp.stack([c_idx])
    idx_arr = jnp.stack([c_idx, chip_idx])

    chip_arr = jnp.stack([chip_idx])
    placed = {}

    def cast(pieces, layer):
        for p in pieces:
            key = (_PIECES[p][1], layer)
            placed[key] = _cast_place(_PIECES[p], w[_PIECES[p][0]], layer, placed.get(key), chip_arr,
                                      "cast_%s_l%d" % (_PIECES[p][0], layer))

    cast(*_GATHER_GROUPS[0][::-1])
    flying, ssems, rsems, token = _gather_start([0], placed)
    for layer, pieces in _GATHER_GROUPS[1:]:
        cast(pieces, layer)
    later = _gather_start([1, 2, 3], placed)
    flying.update(later[0])
    ssems.update(later[1])
    rsems.update(later[2])
    wf = {b[0]: [None] * DEPTH for b in _BUFFERS}

    handing = {}

    def arrive(g, after):
        ks = _group_keys(g)
        landed = _gather_wait(g, [flying[k] for k in ks], ssems[g], rsems[g], after)
        for k, a in zip(ks, _gather_forward(g, landed)):
            wf[k[0]][k[1]] = a

    def hand_over(g, after):
        ks = _group_keys(g)
        landed = _gather_wait(g, [flying[k] for k in ks], ssems[g], rsems[g], after)
        bufs, fs, fr, tok = _forward_start(g, landed)
        handing[g] = (bufs, fs, fr)
        return tok[0:1, 0:1]

    def complete(g, after):
        for k, a in zip(_group_keys(g), _forward_wait(g, *handing[g], after)):
            wf[k[0]][k[1]] = a

    cos, sin = _rope_tables(seq)

    saved = []
    xc = x + token[0, 0]
    for i in range(DEPTH):
        t = "_l%d" % i
        s = {"x_in": xc}
        if i == 0:
            s["h"] = _rms_fwd(xc, w["ln_mix_g"][i:i + 1] + later[3][0:1, 0:1], "rms_mix" + t)
            arrive(0, s["h"])
        else:
            s["h"] = _rms_fwd(xc, w["ln_mix_g"][i:i + 1] + next_in, "rms_mix" + t)
        s["proj"] = _matmul(s["h"], wf["in"][i], "nn", [F32], name="mm_in" + t)[0]
        s["qr"], s["kr"] = _rot_fwd(s["proj"], cos, sin, "rot" + t)
        s["lg"] = jnp.broadcast_to(w["ret_log_gamma"][i].T[:, :, None], (HEADS, 2, LANES))
        s["y"] = _ret_fwd(s["qr"], s["kr"], s["proj"], s["lg"], "ret" + t)
        s5_raw = (w["ssm_a_re"][i], w["ssm_a_im"][i], w["ssm_log_dt"][i], w["ssm_b_re"][i], w["ssm_b_im"][i])
        disc, s["disc_vjp"] = jax.vjp(functools.partial(_s5_discretize, seg_len=seg_len), *s5_raw)
        bblk, cblk, lam = _s5_pack(*disc, w["ssm_c_re"][i], w["ssm_c_im"][i])
        s["s5"] = (bblk.astype(BF16), cblk.astype(BF16), lam)
        s["s5y"] = _s5_fwd(s["proj"], *s["s5"], "s5" + t)
        d_skip = w["ssm_d"][i:i + 1] + hand_over(2 * i + 1, s["s5y"])
        s["ret"], s["ysg"], s["ysgb"] = _post1_fwd(s["y"], s["proj"], s["s5y"], d_skip, "post" + t)
        complete(2 * i + 1, s["ysgb"])
        s["z"] = _matmul(s["ysgb"], wf["glu"][i], "nn", [F32], name="mm_glu" + t)[0]
        s["merged"] = _merge_fwd(s["z"], s["ysg"], s["proj"], s["ret"], w["b_glu"][i:i + 1], "merge" + t)
        s["x1"] = _matmul(s["merged"], wf["out"][i], "nn", [F32], add=xc, name="mm_out" + t)[0]
        s["h2"] = _rms_fwd(s["x1"], w["ln_ffn_g"][i:i + 1], "rms_ffn" + t)
        s["ab"] = _matmul(s["h2"], wf["gu"][i], "nn", [F32], name="mm_gu" + t)[0]
        if i + 1 < DEPTH:
            next_in = hand_over(2 * i + 2, s["ab"])
        s["f"] = _glu_fwd(s["ab"], "glu" + t)
        xc = _matmul(s["f"], wf["down"][i], "nn", [F32], add=s["x1"], name="mm_down" + t)[0]
        if i + 1 < DEPTH:
            complete(2 * i + 2, xc)
        saved.append(s)

    dx, dxb, loss_row, dg_final = _loss_stage(xc, target, w["ln_final_g"][None, :], "loss")
    loss = lax.psum(loss_row[0, 0], ("x", "y", "c"))

    g_small = {"ln_final_g": dg_final[0]}
    per_layer = {n: [None] * DEPTH for n in _SMALL if n != "ln_final_g"}
    dws, got, swaps, flights = {}, {}, {}, []

    def dw_mm(a, b, buf, i, name):
        dws[(buf, i)] = _matmul(a, b, "tn", [F32, BF16], name=name)

    def depart(g):
        keys = _reduce_keys(_REDUCE_GROUPS[g])
        own, land, ssem, rsem, tok = _swap_start(g, {k: dws[k][1] for k in keys})
        swaps[g] = (own, land, ssem, rsem)
        return tok[0:1, 0:1]

    def proceed(g, after):
        group = _REDUCE_GROUPS[g]
        got.update(_swap_wait(g, *swaps[g], after))
        partials = {}
        for p, layer in group:
            key = (_PIECES[p][1], layer)
            partials[key] = _chip_partial(_PIECES[p], dws[key][0], got[key], partials.get(key), c_arr,
                                          "chip_partial_%s_l%d" % (_PIECES[p][0], layer))
        pt, land, ssem, rsem, tok = _scatter_start(g, partials)
        flights.append((g, pt, land, ssem, rsem))
        return tok[0:1, 0:1]

    for i in reversed(range(DEPTH)):
        t = "_l%d" % i
        s = saved[i]
        g_ffn, g_mix, d_skip = w["ln_ffn_g"][i:i + 1], w["ln_mix_g"][i:i + 1], w["ssm_d"][i:i + 1]
        dw_mm(s["f"], dxb, "down", i, "dw_down" + t)
        df = _matmul(dxb, wf["down"][i], "nt", [F32], name="dx_down" + t)[0]
        if i == 0:
            g_ffn = g_ffn + proceed(0, df)
        dab = _glu_bwd(s["ab"], df, "glu_bwd" + t)
        dw_mm(s["h2"], dab, "gu", i, "dw_gu" + t)
        if i == 0:
            g_ffn = g_ffn + depart(1)
        dh2 = _matmul(dab, wf["gu"][i], "nt", [F32], name="dx_gu" + t)[0]
        if i == 0:
            g_ffn = g_ffn + proceed(1, dh2)
        dx1, dx1b, dg = _rms_bwd(s["x1"], dh2, dx, g_ffn, "rms_ffn_bwd" + t)
        per_layer["ln_ffn_g"][i] = dg[0]

        dw_mm(s["merged"], dx1b, "out", i, "dw_out" + t)
        dmerged = _matmul(dx1b, wf["out"][i], "nt", [F32], name="dx_out" + t)[0]
        dz, dys_part, dgs, db = _merge_bwd(s["z"], s["ysg"], s["proj"], s["ret"], dmerged, w["b_glu"][i:i + 1],
                                           "merge_bwd" + t)
        per_layer["b_glu"][i] = db[0]
        dw_mm(s["ysgb"], dz, "glu", i, "dw_glu" + t)
        if i == 0:
            d_skip = d_skip + depart(2)
        dys = _matmul(dz, wf["glu"][i], "nt", [F32], add=dys_part, name="dx_glu" + t)[0]
        if i == 0:
            d_skip = d_skip + proceed(2, dys)
        dy, dgg, dgr, ds5, du_part, dd = _post1_bwd(s["y"], s["proj"], s["s5y"], dmerged, dys,
                                                    d_skip, "post_bwd" + t)
        per_layer["ssm_d"][i] = dd[0]
        du, dbblk, dcblk, dlam = _s5_bwd(s["proj"], ds5, du_part, *s["s5"], "s5_bwd" + t)
        dlr, dli, dbr, dbi, dcr, dci = _s5_unpack(dbblk, dcblk, dlam)
        zeros = jnp.zeros_like(dlr)
        da_re, da_im, dlog_dt, db_re, db_im = s["disc_vjp"]((dlr, dli, zeros, zeros, dbr, dbi))
        for n, val in (("ssm_a_re", da_re), ("ssm_a_im", da_im), ("ssm_log_dt", dlog_dt), ("ssm_b_re", db_re),
                       ("ssm_b_im", db_im), ("ssm_c_re", dcr), ("ssm_c_im", dci)):
            per_layer[n][i] = val
        dqr, dkr, dv, dlg = _ret_bwd(s["qr"], s["kr"], s["proj"], dy, s["lg"], "ret_bwd" + t)
        per_layer["ret_log_gamma"][i] = dlg[:, :, 0].T
        dqkv = _rot_bwd(dqr, dkr, dv, cos, sin, "rot_bwd" + t)
        dproj = jnp.concatenate([dqkv, dgg, du, dgr, dgs], axis=1)
        dw_mm(s["h"], dproj, "in", i, "dw_in" + t)
        if i == 0:
            g_mix = g_mix + depart(3)
        dh = _matmul(dproj, wf["in"][i], "nt", [F32], name="dx_in" + t)[0]
        if i == 0:
            g_mix = g_mix + proceed(3, dh)
        dx, dxb, dg = _rms_bwd(s["x_in"], dh, dx1, g_mix, "rms_mix_bwd" + t)
        per_layer["ln_mix_g"][i] = dg[0]
        if i == DEPTH - 1:
            dxb = dxb + depart(0).astype(BF16)

    for n in per_layer:
        g_small[n] = jnp.stack(per_layer[n])
    me_arr = jnp.stack([2 * chip_idx + c_idx])
    g_mine = _flatten(g_small, _SMALL)
    rs_bufs, rs_ssem, rs_rsem, small_token = _exchange_start(_place_slot(g_mine, me_arr, True), g_mine,
                                                             "small_scatter_start")

    reduced = [None] * N_PIECES
    for g, pt, land, ssem, rsem in flights:
        landed = _scatter_wait(g, pt, land, ssem, rsem, small_token)
        for (p, layer), buf in zip(_REDUCE_GROUPS[g], landed):
            key = (_PIECES[p][1], layer)
            reduced[p] = _reduce_half(_PIECES[p], layer, dws[key][0], got[key], buf, reduced[p], idx_arr,
                                      "reduce_%s_l%d" % (_PIECES[p][0], layer))
    g_big = dict(zip([p[0] for p in _PIECES], _share_halves(reduced)))

    landed = _exchange_wait(rs_bufs, rs_ssem, rs_rsem, g_big[_BIG[-1]], "small_scatter_wait")
    ag_bufs, ag_ssem, ag_rsem, _ = _exchange_start(_place_slot(_sum_slots(landed, "sum_small"), me_arr, False),
                                                   None, "small_gather_start")

    grads, delta, new_m, new_v = {}, {}, {}, {}
    for n in _BIG:
        d, r, cc = w[n].shape
        two_d = lambda a: a.reshape(d * r, cc)
        dl, mn, vn = _adamw(two_d(w[n]), two_d(g_big[n]), two_d(m[n]), two_d(v[n]), "adamw_" + n)
        grads[n], delta[n], new_m[n], new_v[n] = g_big[n], dl.reshape(d, r, cc), mn.reshape(d, r, cc), vn.reshape(d, r, cc)

    gathered = _exchange_wait(ag_bufs, ag_ssem, ag_rsem, delta[_BIG[-1]], "small_gather_wait")
    g_flat = gathered.reshape(-1, LANES)
    grads.update(_unflatten(g_flat, w, _SMALL))
    tiny_rows = _flat_rows(w, _TINY)
    dl, mn, vn = _adamw(_flatten(w, _TINY), g_flat[:tiny_rows], _flatten(m, _TINY), _flatten(v, _TINY), "adamw_tiny")
    for dst, flat in ((delta, dl), (new_m, mn), (new_v, vn)):
        dst.update(_unflatten(flat, w, _TINY))
    for n in _MID:
        delta[n], new_m[n], new_v[n] = _adamw_nd(w[n], grads[n], m[n], v[n], "adamw_" + n)
    return loss, dx, grads, delta, new_m, new_v


def kernel(x, ln_mix_g, w_in, ret_log_gamma, ssm_a_re, ssm_a_im, ssm_log_dt, ssm_b_re, ssm_b_im, ssm_c_re, ssm_c_im, ssm_d, w_glu, b_glu, w_out, ln_ffn_g, w_ffn_gate, w_ffn_up, w_ffn_down, ln_final_g, loss_target, m_ln_mix_g, m_w_in, m_ret_log_gamma, m_ssm_a_re, m_ssm_a_im, m_ssm_log_dt, m_ssm_b_re, m_ssm_b_im, m_ssm_c_re, m_ssm_c_im, m_ssm_d, m_w_glu, m_b_glu, m_w_out, m_ln_ffn_g, m_w_ffn_gate, m_w_ffn_up, m_w_ffn_down, m_ln_final_g, v_ln_mix_g, v_w_in, v_ret_log_gamma, v_ssm_a_re, v_ssm_a_im, v_ssm_log_dt, v_ssm_b_re, v_ssm_b_im, v_ssm_c_re, v_ssm_c_im, v_ssm_d, v_w_glu, v_b_glu, v_w_out, v_ln_ffn_g, v_w_ffn_gate, v_w_ffn_up, v_w_ffn_down, v_ln_final_g):
    given = dict(locals())
    w = {n: given[n] for n in _WEIGHTS}
    m = {n: given["m_" + n] for n in _WEIGHTS}
    v = {n: given["v_" + n] for n in _WEIGHTS}
    loss, dx, grads, delta, new_m, new_v = _step(w, m, v, x[0], loss_target[0])
    return (loss, dx[None], *[grads[n] for n in _WEIGHTS], *[delta[n] for n in _WEIGHTS],
            *[new_m[n] for n in _WEIGHTS], *[new_v[n] for n in _WEIGHTS])
```

```python
import functools
import math

import jax
import jax.numpy as jnp
from jax import lax
from jax.experimental import pallas as pl
from jax.experimental.pallas import tpu as pltpu

F32 = jnp.float32
BF16 = jnp.bfloat16

D_MODEL = 2048
DEPTH = 2
HEADS = 4
QK_DIM = 256
V_DIM = 512
QK_WIDTH = HEADS * QK_DIM
ROPE_BASE = 10000.0
GROUP = 16
N_GROUPS = D_MODEL // GROUP
N_STATE = 64
D_FF = 5632
IN_WIDTH = 2 * QK_WIDTH + 5 * D_MODEL
EPS = 1e-6
N_CHIPS = 4

ADAM_LR = 0.001
ADAM_B1 = 0.9
ADAM_B2 = 0.999
ADAM_EPS = 1e-08
ADAM_WD = 0.01
ADAM_STEP = 10

LANES = 128
SUBLANES = 8
VMEM_LIMIT = 56 * 1024 * 1024
SEGMENTS = SUBLANES
GROUPS_PER_TILE = LANES // GROUP
STATE_COLS = GROUPS_PER_TILE * N_STATE
N_TILES = D_MODEL // LANES
SCAN_UNROLL = 4

MESH = pl.DeviceIdType.MESH
HBM_SPEC = pl.BlockSpec(memory_space=pltpu.HBM)


def _params(sem=None, **kw):
    return pltpu.CompilerParams(dimension_semantics=sem, vmem_limit_bytes=VMEM_LIMIT, **kw)


def _tile(n, cap=1024):
    for t in (2048, 1024, 512, 256, 128, 64):
        if t <= cap and n % t == 0:
            return t
    raise ValueError(n)


def _rows_call(fn, rows, pars, row_outs, par_outs, *, tm, name):
    m = rows[0][0].shape[0]
    nr, npar, nro, npo = len(rows), len(pars), len(row_outs), len(par_outs)

    def body(*refs):
        rin = refs[:nr]
        pin = refs[nr:nr + npar]
        rout = refs[nr + npar:nr + npar + nro]
        pout = refs[nr + npar + nro:]
        res = fn(*[r[...] for r in rin], *[p[...] for p in pin])
        if not isinstance(res, (tuple, list)):
            res = (res,)
        for r, v in zip(rout, res[:nro]):
            r[...] = v.astype(r.dtype)
        if npo:
            @pl.when(pl.program_id(0) == 0)
            def _():
                for p in pout:
                    p[...] = jnp.zeros(p.shape, p.dtype)
            for p, v in zip(pout, res[nro:]):
                p[...] += v

    in_specs = [pl.BlockSpec((tm, w), functools.partial(lambda cb, i: (i, cb), cb)) for (_, w, cb) in rows]
    in_specs += [pl.BlockSpec(p.shape, lambda i: (0, 0)) for p in pars]
    out_specs = [pl.BlockSpec((tm, w), lambda i: (i, 0)) for (w, _) in row_outs]
    out_specs += [pl.BlockSpec(s, lambda i: (0, 0)) for s in par_outs]
    out_shape = [jax.ShapeDtypeStruct((m, w), dt) for (w, dt) in row_outs]
    out_shape += [jax.ShapeDtypeStruct(s, F32) for s in par_outs]
    res = pl.pallas_call(
        body, name=name, grid=(m // tm,), in_specs=in_specs, out_specs=out_specs, out_shape=out_shape,
        compiler_params=_params(("arbitrary",) if npo else ("parallel",)),
    )(*[a for (a, _, _) in rows], *pars)
    return res


def _f32(*vals):
    return [v.astype(F32) for v in vals]


def _f_rms(x, g):
    r = lax.rsqrt(jnp.mean(x * x, axis=-1, keepdims=True) + EPS)
    return x * r * g


def _rms_fwd(x, g, name):
    return _rows_call(lambda xv, gv: _f_rms(xv, gv), [(x, D_MODEL, 0)], [g], [(D_MODEL, BF16)], [],
                      tm=256, name=name)[0]


def _rms_bwd(x, dh, dres, g, name):
    def fn(xv, dhv, drv, gv):
        _, vjp = jax.vjp(_f_rms, xv, gv)
        dx, dg = vjp(dhv)
        dx = dx + drv
        return dx, dx, dg
    return _rows_call(fn, [(x, D_MODEL, 0), (dh, D_MODEL, 0), (dres, D_MODEL, 0)], [g],
                      [(D_MODEL, F32), (D_MODEL, BF16)], [(1, D_MODEL)], tm=256, name=name)


def _rot_heads(xv, cos, sin, scale):
    half = QK_DIM // 2
    outs = []
    for h in range(HEADS):
        x1 = xv[:, h * QK_DIM:h * QK_DIM + half]
        x2 = xv[:, h * QK_DIM + half:(h + 1) * QK_DIM]
        outs += [(x1 * cos - x2 * sin) * scale, (x1 * sin + x2 * cos) * scale]
    return jnp.concatenate(outs, axis=1)


def _rot_fwd(proj, cos, sin, name):
    def fn(q, k, cv, sv):
        return _rot_heads(q, cv, sv, 1.0), _rot_heads(k, cv, sv, QK_DIM ** -0.5)
    return _rows_call(fn, [(proj, QK_WIDTH, 0), (proj, QK_WIDTH, 1), (cos, LANES, 0), (sin, LANES, 0)], [],
                      [(QK_WIDTH, BF16), (QK_WIDTH, BF16)], [], tm=256, name=name)


def _rot_bwd(dqr, dkr, dv, cos, sin, name):
    def fn(dq, dk, dvv, cv, sv):
        return jnp.concatenate([_rot_heads(dq, cv, -sv, 1.0), _rot_heads(dk, cv, -sv, QK_DIM ** -0.5), dvv], axis=1)
    return _rows_call(fn, [(dqr, QK_WIDTH, 0), (dkr, QK_WIDTH, 0), (dv, D_MODEL, 0), (cos, LANES, 0), (sin, LANES, 0)],
                      [], [(2 * QK_WIDTH + D_MODEL, BF16)], [], tm=256, name=name)[0]


def _f_post1(y0, y1, y2, y3, g, gr, s5, u, dsk):
    yn = [yh * lax.rsqrt(jnp.mean(yh * yh, axis=-1, keepdims=True) + EPS) for yh in (y0, y1, y2, y3)]
    ret = jax.nn.sigmoid(gr) * (jax.nn.silu(g) * jnp.concatenate(yn, axis=1))
    ysg = jax.nn.gelu(s5 + dsk * u)
    return ret, ysg


def _post1_rows(y, proj, s5y):
    rows = [(y, V_DIM, h) for h in range(HEADS)]
    rows += [(proj, D_MODEL, 2), (proj, D_MODEL, 4), (s5y, D_MODEL, 0), (proj, D_MODEL, 3)]
    return rows


def _post1_fwd(y, proj, s5y, dsk, name):
    def fn(*vals):
        ret, ysg = _f_post1(*vals)
        return ret, ysg, ysg
    return _rows_call(fn, _post1_rows(y, proj, s5y), [dsk],
                      [(D_MODEL, F32), (D_MODEL, F32), (D_MODEL, BF16)], [], tm=128, name=name)


def _post1_bwd(y, proj, s5y, dret, dys, dsk, name):
    def fn(*vals):
        prim = vals[:8] + (vals[10],)
        _, vjp = jax.vjp(_f_post1, *prim)
        gy0, gy1, gy2, gy3, gg, ggr, gs5, gu, gd = vjp((vals[8], vals[9]))
        return jnp.concatenate([gy0, gy1, gy2, gy3], axis=1), gg, ggr, gs5, gu, gd
    rows = _post1_rows(y, proj, s5y) + [(dret, D_MODEL, 0), (dys, D_MODEL, 0)]
    return _rows_call(fn, rows, [dsk],
                      [(D_MODEL, BF16), (D_MODEL, BF16), (D_MODEL, BF16), (D_MODEL, F32), (D_MODEL, F32)],
                      [(1, D_MODEL)], tm=128, name=name)


def _f_merge(z, ysg, gs, ret, b):
    return ret + jax.nn.sigmoid(gs) * (ysg * jax.nn.sigmoid(z + b))


def _merge_fwd(z, ysg, proj, ret, b, name):
    return _rows_call(_f_merge, [(z, D_MODEL, 0), (ysg, D_MODEL, 0), (proj, D_MODEL, 5), (ret, D_MODEL, 0)], [b],
                      [(D_MODEL, BF16)], [], tm=128, name=name)[0]


def _merge_bwd(z, ysg, proj, ret, dm, b, name):
    def fn(zv, yv, gv, rv, dmv, bv):
        _, vjp = jax.vjp(_f_merge, zv, yv, gv, rv, bv)
        gz, gy, gg, _, gb = vjp(dmv)
        return gz, gy, gg, gb
    rows = [(z, D_MODEL, 0), (ysg, D_MODEL, 0), (proj, D_MODEL, 5), (ret, D_MODEL, 0), (dm, D_MODEL, 0)]
    return _rows_call(fn, rows, [b], [(D_MODEL, BF16), (D_MODEL, F32), (D_MODEL, BF16)], [(1, D_MODEL)],
                      tm=128, name=name)


def _f_glu(a, b):
    return jax.nn.silu(a) * b


def _glu_fwd(ab, name):
    return _rows_call(_f_glu, [(ab, D_FF, 0), (ab, D_FF, 1)], [], [(D_FF, BF16)], [], tm=128, name=name)[0]


def _glu_bwd(ab, df, name):
    def fn(a, b, d):
        _, vjp = jax.vjp(_f_glu, a, b)
        ga, gb = vjp(d)
        return jnp.concatenate([ga, gb], axis=1)
    return _rows_call(fn, [(ab, D_FF, 0), (ab, D_FF, 1), (df, D_FF, 0)], [], [(2 * D_FF, BF16)], [],
                      tm=128, name=name)[0]


def _loss_stage(x, tgt, g, name):
    def fn(xv, tv, gv):
        def lf(xx, gg):
            err = _f_rms(xx, gg) - tv
            row = jnp.mean(err * err, axis=-1, keepdims=True)
            return 0.5 * jnp.sum(row, axis=0, keepdims=True)
        l, vjp = jax.vjp(lf, xv, gv)
        dx, dg = vjp(jnp.ones((1, 1), F32))
        return dx, dx, jnp.broadcast_to(l, (1, LANES)), dg
    return _rows_call(fn, [(x, D_MODEL, 0), (tgt, D_MODEL, 0)], [g], [(D_MODEL, F32), (D_MODEL, BF16)],
                      [(1, LANES), (1, D_MODEL)], tm=256, name=name)


def _adam_math(wv, gv, mv, vv):
    mn = ADAM_B1 * mv + (1.0 - ADAM_B1) * gv
    vn = ADAM_B2 * vv + (1.0 - ADAM_B2) * (gv * gv)
    m_hat = mn / (1.0 - ADAM_B1 ** ADAM_STEP)
    v_hat = vn / (1.0 - ADAM_B2 ** ADAM_STEP)
    delta = -ADAM_LR * (m_hat / (jnp.sqrt(v_hat) + ADAM_EPS) + ADAM_WD * wv)
    return delta, mn, vn


def _adamw(w, g, m, v, name):
    rows, cols = w.shape
    tm = _tile(rows, 128 if cols > D_FF // N_CHIPS else (256 if cols > LANES else 512))
    return _rows_call(_adam_math, [(w, cols, 0), (g, cols, 0), (m, cols, 0), (v, cols, 0)], [],
                      [(cols, F32)] * 3, [], tm=tm, name=name)


def _adamw_nd(w, g, m, v, name):
    shape = w.shape
    lead = math.prod(shape[:-2])
    blk = (lead // 8,) + shape[-2:]
    three_d = lambda a: a.reshape((lead,) + shape[-2:])

    def body(w_ref, g_ref, m_ref, v_ref, d_ref, mn_ref, vn_ref):
        d_ref[...], mn_ref[...], vn_ref[...] = _adam_math(w_ref[...], g_ref[...], m_ref[...], v_ref[...])

    spec = pl.BlockSpec(blk, lambda i: (i, 0, 0))
    res = pl.pallas_call(
        body, name=name, grid=(8,), in_specs=[spec] * 4, out_specs=[spec] * 3,
        out_shape=[jax.ShapeDtypeStruct((lead,) + shape[-2:], F32)] * 3,
        compiler_params=_params(("parallel",)),
    )(three_d(w), three_d(g), three_d(m), three_d(v))
    return [r.reshape(shape) for r in res]


MATMUL_VMEM_BUDGET = 44 * 1024 * 1024


def _matmul_tiles(m, n, k, out_bytes, has_add):
    if k > 2048:
        return _tile(m, 1024), _tile(n, 1024), _tile(k, 1024)
    tm, tn, tk = _tile(m, 2048), _tile(n, 1024), k

    def footprint():
        acc = 4 * tm * tn if k // tk > 1 else 0
        return 2 * 2 * (tm * tk + tk * tn) + 2 * (out_bytes + 4 * has_add) * tm * tn + acc

    while footprint() > MATMUL_VMEM_BUDGET:
        if tn > 512 and n % (tn // 2) == 0:
            tn //= 2
        elif tk > 512 and k % (tk // 2) == 0:
            tk //= 2
        else:
            tm //= 2
    return tm, tn, tk


def _matmul(a, b, mode, out_dtypes, *, name, add=None):
    if mode == "nn":
        (m, k), (_, n) = a.shape, b.shape
    elif mode == "nt":
        (m, k), (n, _) = a.shape, b.shape
    else:
        (k, m), (_, n) = a.shape, b.shape
    n_out = len(out_dtypes)
    has_add = add is not None
    tm, tn, tk = _matmul_tiles(m, n, k, sum(jnp.dtype(dt).itemsize for dt in out_dtypes), has_add)
    nk = k // tk
    if mode == "nn":
        a_spec = pl.BlockSpec((tm, tk), lambda i, j, kk: (i, kk))
        b_spec = pl.BlockSpec((tk, tn), lambda i, j, kk: (kk, j))
        dims = (((1,), (0,)), ((), ()))
    elif mode == "nt":
        a_spec = pl.BlockSpec((tm, tk), lambda i, j, kk: (i, kk))
        b_spec = pl.BlockSpec((tn, tk), lambda i, j, kk: (j, kk))
        dims = (((1,), (1,)), ((), ()))
    else:
        a_spec = pl.BlockSpec((tk, tm), lambda i, j, kk: (kk, i))
        b_spec = pl.BlockSpec((tk, tn), lambda i, j, kk: (kk, j))
        dims = (((0,), (0,)), ((), ()))

    def body(*refs):
        a_ref, b_ref = refs[0], refs[1]
        add_ref = refs[2] if has_add else None
        outs = refs[2 + has_add:2 + has_add + n_out]

        def finish(r):
            if has_add:
                r = r + add_ref[...]
            for o in outs:
                o[...] = r.astype(o.dtype)

        if nk == 1:
            finish(lax.dot_general(a_ref[...], b_ref[...], dims, preferred_element_type=F32))
            return
        acc = refs[-1]
        kk = pl.program_id(2)

        @pl.when(kk == 0)
        def _():
            acc[...] = jnp.zeros(acc.shape, F32)

        acc[...] += lax.dot_general(a_ref[...], b_ref[...], dims, preferred_element_type=F32)

        @pl.when(kk == nk - 1)
        def _():
            finish(acc[...])

    in_specs = [a_spec, b_spec]
    args = [a, b]
    if has_add:
        in_specs.append(pl.BlockSpec((tm, tn), lambda i, j, kk: (i, j)))
        args.append(add)
    return pl.pallas_call(
        body, name=name, grid=(m // tm, n // tn, nk), in_specs=in_specs,
        out_specs=[pl.BlockSpec((tm, tn), lambda i, j, kk: (i, j))] * n_out,
        out_shape=[jax.ShapeDtypeStruct((m, n), dt) for dt in out_dtypes],
        scratch_shapes=[pltpu.VMEM((tm, tn), F32)] if nk > 1 else [],
        compiler_params=_params(("parallel", "parallel", "arbitrary")),
    )(*args)


RET_TQ = 512


def _decay(lg_ref, i, tq, seq):
    n_idx = i * tq + lax.broadcasted_iota(jnp.int32, (tq, seq), 0)
    m_idx = lax.broadcasted_iota(jnp.int32, (tq, seq), 1)
    diff = (n_idx - m_idx).astype(F32)
    lgf = lg_ref[0, 0:1, 0:1]
    lgb = lg_ref[0, 1:2, 0:1]
    causal = diff >= 0
    return jnp.exp(jnp.where(causal, lgf * diff, -lgb * diff)), diff, causal


_NT = (((1,), (1,)), ((), ()))
_TN = (((0,), (0,)), ((), ()))


def _ret_fwd(qr, kr, proj, lg, name):
    seq = qr.shape[0]
    tq = RET_TQ
    v_blk0 = (2 * QK_WIDTH) // V_DIM

    def body(q_ref, k_ref, v_ref, lg_ref, y_ref):
        i = pl.program_id(1)
        s = lax.dot_general(q_ref[...], k_ref[...], _NT, preferred_element_type=F32)
        dm, _, _ = _decay(lg_ref, i, tq, seq)
        p = (s * dm).astype(BF16)
        y_ref[...] = jnp.dot(p, v_ref[...].astype(BF16), preferred_element_type=F32)

    return pl.pallas_call(
        body, name=name, grid=(HEADS, seq // tq),
        in_specs=[pl.BlockSpec((tq, QK_DIM), lambda h, i: (i, h)),
                  pl.BlockSpec((seq, QK_DIM), lambda h, i: (0, h)),
                  pl.BlockSpec((seq, V_DIM), lambda h, i: (0, v_blk0 + h)),
                  pl.BlockSpec((1, 2, LANES), lambda h, i: (h, 0, 0))],
        out_specs=pl.BlockSpec((tq, V_DIM), lambda h, i: (i, h)),
        out_shape=jax.ShapeDtypeStruct((seq, HEADS * V_DIM), F32),
        compiler_params=_params(("parallel", "parallel")),
    )(qr, kr, proj, lg)


def _ret_bwd(qr, kr, proj, dy, lg, name):
    seq = qr.shape[0]
    tq = RET_TQ
    v_blk0 = (2 * QK_WIDTH) // V_DIM

    def body(q_ref, k_ref, v_ref, dy_ref, lg_ref, dq_ref, dk_ref, dv_ref, dlg_ref):
        i = pl.program_id(1)

        @pl.when(i == 0)
        def _():
            dk_ref[...] = jnp.zeros(dk_ref.shape, F32)
            dv_ref[...] = jnp.zeros(dv_ref.shape, F32)
            dlg_ref[...] = jnp.zeros(dlg_ref.shape, F32)

        q = q_ref[...]
        k = k_ref[...]
        vb = v_ref[...].astype(BF16)
        dyb = dy_ref[...]
        s = lax.dot_general(q, k, _NT, preferred_element_type=F32)
        dm, diff, causal = _decay(lg_ref, i, tq, seq)
        p = s * dm
        dp = lax.dot_general(dyb, vb, _NT, preferred_element_type=F32)
        dv_ref[...] += lax.dot_general(p.astype(BF16), dyb, _TN, preferred_element_type=F32)
        ds = (dp * dm).astype(BF16)
        dq_ref[...] = jnp.dot(ds, k, preferred_element_type=F32)
        dk_ref[...] += lax.dot_general(ds, q, _TN, preferred_element_type=F32)
        gd = dp * p * diff
        dlf = jnp.sum(jnp.sum(jnp.where(causal, gd, 0.0), axis=1, keepdims=True), axis=0, keepdims=True)
        dlb = jnp.sum(jnp.sum(jnp.where(causal, 0.0, -gd), axis=1, keepdims=True), axis=0, keepdims=True)
        row = lax.broadcasted_iota(jnp.int32, (2, LANES), 0)
        dlg_ref[0] += jnp.where(row == 0, dlf, dlb)

    return pl.pallas_call(
        body, name=name, grid=(HEADS, seq // tq),
        in_specs=[pl.BlockSpec((tq, QK_DIM), lambda h, i: (i, h)),
                  pl.BlockSpec((seq, QK_DIM), lambda h, i: (0, h)),
                  pl.BlockSpec((seq, V_DIM), lambda h, i: (0, v_blk0 + h)),
                  pl.BlockSpec((tq, V_DIM), lambda h, i: (i, h)),
                  pl.BlockSpec((1, 2, LANES), lambda h, i: (h, 0, 0))],
        out_specs=[pl.BlockSpec((tq, QK_DIM), lambda h, i: (i, h)),
                   pl.BlockSpec((seq, QK_DIM), lambda h, i: (0, h)),
                   pl.BlockSpec((seq, V_DIM), lambda h, i: (0, h)),
                   pl.BlockSpec((1, 2, LANES), lambda h, i: (h, 0, 0))],
        out_shape=[jax.ShapeDtypeStruct((seq, QK_WIDTH), F32), jax.ShapeDtypeStruct((seq, QK_WIDTH), F32),
                   jax.ShapeDtypeStruct((seq, HEADS * V_DIM), F32), jax.ShapeDtypeStruct((HEADS, 2, LANES), F32)],
        compiler_params=_params(("parallel", "arbitrary")),
    )(qr, kr, proj, dy, lg)


def _shift_rows(v, reverse):
    row = lax.broadcasted_iota(jnp.int32, v.shape, 0)
    if reverse:
        return jnp.where(row == SEGMENTS - 1, 0.0, pltpu.roll(v, SEGMENTS - 1, 0))
    return jnp.where(row == 0, 0.0, pltpu.roll(v, 1, 0))


def _slab(t):
    if isinstance(t, int):
        return pl.ds(t * SEGMENTS, SEGMENTS)
    return pl.ds(pl.multiple_of(t * SEGMENTS, SEGMENTS), SEGMENTS)


def _unrolled_loop(body, lo, hi, init):
    main = (hi - lo) // SCAN_UNROLL

    def unrolled(g, carry):
        for k in range(SCAN_UNROLL):
            carry = body(lo + g * SCAN_UNROLL + k, carry)
        return carry

    carry = lax.fori_loop(0, main, unrolled, init)
    for t in range(lo + main * SCAN_UNROLL, hi):
        carry = body(t, carry)
    return carry


def _scan(xr_ref, xi_ref, lam, reverse, conj):
    steps = xr_ref.shape[0] // SEGMENTS
    cols = xr_ref.shape[1]
    lr = jnp.broadcast_to(lam[0], (SEGMENTS, cols))
    li = jnp.broadcast_to(lam[1], (SEGMENTS, cols))
    lrt = jnp.broadcast_to(lam[2], (SEGMENTS, cols))
    lit = jnp.broadcast_to(lam[3], (SEGMENTS, cols))
    if conj:
        li, lit = -li, -lit
    zero = jnp.zeros((SEGMENTS, cols), F32)

    def rows_of(t):
        return _slab(steps - 1 - t if reverse else t)

    def advance(t, carry):
        sr, si = carry
        rows = rows_of(t)
        return lr * sr - li * si + xr_ref[rows, :], lr * si + li * sr + xi_ref[rows, :]

    def step(t, carry):
        nr, ni = advance(t, carry)
        rows = rows_of(t)
        xr_ref[rows, :] = nr
        xi_ref[rows, :] = ni
        return nr, ni

    def run(body, init):
        return _unrolled_loop(body, 0, steps, init)

    er, ei = run(advance, (zero, zero))
    cr, ci = zero, zero
    for _ in range(SEGMENTS - 1):
        tr = er + lrt * cr - lit * ci
        ti = ei + lrt * ci + lit * cr
        cr, ci = _shift_rows(tr, reverse), _shift_rows(ti, reverse)
    run(step, (cr, ci))


def _permute_in(dst_ref, src_ref):
    steps = src_ref.shape[0] // SEGMENTS
    for s in range(SEGMENTS):
        dst_ref[pl.ds(s, steps, stride=SEGMENTS), :] = src_ref[s * steps:(s + 1) * steps, :].astype(dst_ref.dtype)


def _unpermute(src_ref, s):
    steps = src_ref.shape[0] // SEGMENTS
    return src_ref[pl.ds(s, steps, stride=SEGMENTS), :]


def _s5_fwd(proj, bblk, cblk, lam, name):
    seq = proj.shape[0]
    u_blk0 = (2 * QK_WIDTH + 2 * D_MODEL) // LANES
    sc = STATE_COLS

    def body(u_ref, b_ref, c_ref, lam_ref, y_ref, up_ref, yp_ref, xr_ref, xi_ref):
        _permute_in(up_ref, u_ref)
        ub = up_ref[...].astype(BF16)
        for d in range(2):
            xr_ref[...] = jnp.dot(ub, b_ref[d, :, 0:sc], preferred_element_type=F32)
            xi_ref[...] = jnp.dot(ub, b_ref[d, :, sc:2 * sc], preferred_element_type=F32)
            lm = [lam_ref[d, r:r + 1, :] for r in range(4)]
            _scan(xr_ref, xi_ref, lm, reverse=(d == 1), conj=False)
            yd = (jnp.dot(xr_ref[...].astype(BF16), c_ref[d, 0:sc, :], preferred_element_type=F32)
                  + jnp.dot(xi_ref[...].astype(BF16), c_ref[d, sc:2 * sc, :], preferred_element_type=F32))
            if d == 0:
                yp_ref[...] = yd
            else:
                yp_ref[...] += yd
        steps = seq // SEGMENTS
        for s in range(SEGMENTS):
            y_ref[s * steps:(s + 1) * steps, :] = _unpermute(yp_ref, s)

    return pl.pallas_call(
        body, name=name, grid=(N_TILES,),
        in_specs=[pl.BlockSpec((seq, LANES), lambda j: (0, u_blk0 + j)),
                  pl.BlockSpec((2, None, LANES, 2 * sc), lambda j: (0, j, 0, 0)),
                  pl.BlockSpec((2, None, 2 * sc, LANES), lambda j: (0, j, 0, 0)),
                  pl.BlockSpec((2, None, 4, sc), lambda j: (0, j, 0, 0))],
        out_specs=pl.BlockSpec((seq, LANES), lambda j: (0, j)),
        out_shape=jax.ShapeDtypeStruct((seq, D_MODEL), F32),
        scratch_shapes=[pltpu.VMEM((seq, LANES), F32), pltpu.VMEM((seq, LANES), F32),
                        pltpu.VMEM((seq, sc), F32), pltpu.VMEM((seq, sc), F32)],
        compiler_params=_params(("parallel",)),
    )(proj, bblk, cblk, lam)


def _s5_bwd(proj, dy, du_part, bblk, cblk, lam, name):
    seq = proj.shape[0]
    u_blk0 = (2 * QK_WIDTH + 2 * D_MODEL) // LANES
    sc = STATE_COLS
    steps = seq // SEGMENTS

    def body(u_ref, dy_ref, dup_ref, b_ref, c_ref, lam_ref, du_ref, db_ref, dc_ref, dlam_ref,
             up_ref, dyp_ref, dua_ref, xr_ref, xi_ref, gr_ref, gi_ref):
        _permute_in(up_ref, u_ref)
        _permute_in(dyp_ref, dy_ref)
        ub = up_ref[...].astype(BF16)
        dyb = dyp_ref[...].astype(BF16)
        ubt = up_ref[...].T.astype(BF16)
        dybt = dyp_ref[...].T.astype(BF16)
        for d in range(2):
            reverse = d == 1
            xr_ref[...] = jnp.dot(ub, b_ref[d, :, 0:sc], preferred_element_type=F32)
            xi_ref[...] = jnp.dot(ub, b_ref[d, :, sc:2 * sc], preferred_element_type=F32)
            lm = [lam_ref[d, r:r + 1, :] for r in range(4)]
            _scan(xr_ref, xi_ref, lm, reverse=reverse, conj=False)
            xrb = xr_ref[...].astype(BF16)
            xib = xi_ref[...].astype(BF16)
            dc_ref[d, :, 0:sc] = jnp.dot(dybt, xrb, preferred_element_type=F32)
            dc_ref[d, :, sc:2 * sc] = jnp.dot(dybt, xib, preferred_element_type=F32)
            gr_ref[...] = lax.dot_general(dyb, c_ref[d, 0:sc, :], _NT, preferred_element_type=F32)
            gi_ref[...] = lax.dot_general(dyb, c_ref[d, sc:2 * sc, :], _NT, preferred_element_type=F32)
            _scan(gr_ref, gi_ref, lm, reverse=not reverse, conj=True)

            def acc_step(t, carry):
                ar, ai = carry
                prev = _slab(t + 1 if reverse else t - 1)
                pr = xr_ref[prev, :]
                pi = xi_ref[prev, :]
                zr = gr_ref[_slab(t), :]
                zi = gi_ref[_slab(t), :]
                return ar + zr * pr + zi * pi, ai + zi * pr - zr * pi

            zero = jnp.zeros((SEGMENTS, sc), F32)
            if reverse:
                ar, ai = _unrolled_loop(acc_step, 0, steps - 1, (zero, zero))
                edge = _slab(steps - 1)
                pr = _shift_rows(xr_ref[_slab(0), :], True)
                pi = _shift_rows(xi_ref[_slab(0), :], True)
            else:
                ar, ai = _unrolled_loop(acc_step, 1, steps, (zero, zero))
                edge = _slab(0)
                pr = _shift_rows(xr_ref[_slab(steps - 1), :], False)
                pi = _shift_rows(xi_ref[_slab(steps - 1), :], False)
            zr = gr_ref[edge, :]
            zi = gi_ref[edge, :]
            ar = ar + zr * pr + zi * pi
            ai = ai + zi * pr - zr * pi
            dlam_ref[d, 0:1, :] = jnp.sum(ar, axis=0, keepdims=True)
            dlam_ref[d, 1:2, :] = jnp.sum(ai, axis=0, keepdims=True)

            grb = gr_ref[...].astype(BF16)
            gib = gi_ref[...].astype(BF16)
            db_ref[d, :, 0:sc] = jnp.dot(ubt, grb, preferred_element_type=F32)
            db_ref[d, :, sc:2 * sc] = jnp.dot(ubt, gib, preferred_element_type=F32)
            dud = (lax.dot_general(grb, b_ref[d, :, 0:sc], _NT, preferred_element_type=F32)
                   + lax.dot_general(gib, b_ref[d, :, sc:2 * sc], _NT, preferred_element_type=F32))
            if d == 0:
                dua_ref[...] = dud
            else:
                dua_ref[...] += dud
        for s in range(SEGMENTS):
            rows = slice(s * steps, (s + 1) * steps)
            du_ref[rows, :] = (_unpermute(dua_ref, s) + dup_ref[rows, :]).astype(du_ref.dtype)

    return pl.pallas_call(
        body, name=name, grid=(N_TILES,),
        in_specs=[pl.BlockSpec((seq, LANES), lambda j: (0, u_blk0 + j)),
                  pl.BlockSpec((seq, LANES), lambda j: (0, j)),
                  pl.BlockSpec((seq, LANES), lambda j: (0, j)),
                  pl.BlockSpec((2, None, LANES, 2 * sc), lambda j: (0, j, 0, 0)),
                  pl.BlockSpec((2, None, 2 * sc, LANES), lambda j: (0, j, 0, 0)),
                  pl.BlockSpec((2, None, 4, sc), lambda j: (0, j, 0, 0))],
        out_specs=[pl.BlockSpec((seq, LANES), lambda j: (0, j)),
                   pl.BlockSpec((2, None, LANES, 2 * sc), lambda j: (0, j, 0, 0)),
                   pl.BlockSpec((2, None, LANES, 2 * sc), lambda j: (0, j, 0, 0)),
                   pl.BlockSpec((2, None, 2, sc), lambda j: (0, j, 0, 0))],
        out_shape=[jax.ShapeDtypeStruct((seq, D_MODEL), BF16),
                   jax.ShapeDtypeStruct((2, N_TILES, LANES, 2 * sc), F32),
                   jax.ShapeDtypeStruct((2, N_TILES, LANES, 2 * sc), F32),
                   jax.ShapeDtypeStruct((2, N_TILES, 2, sc), F32)],
        scratch_shapes=[pltpu.VMEM((seq, LANES), F32), pltpu.VMEM((seq, LANES), F32), pltpu.VMEM((seq, LANES), F32),
                        pltpu.VMEM((seq, sc), F32), pltpu.VMEM((seq, sc), F32),
                        pltpu.VMEM((seq, sc), F32), pltpu.VMEM((seq, sc), F32)],
        compiler_params=_params(("parallel",)),
    )(proj, dy, du_part, bblk, cblk, lam)


def _s5_discretize(a_re, a_im, log_dt, b_re, b_im, seg_len):
    dt = jnp.exp(log_dt)[..., None]
    e = jnp.exp(a_re * dt)
    lr, li = e * jnp.cos(a_im * dt), e * jnp.sin(a_im * dt)
    et = jnp.exp(a_re * dt * seg_len)
    lrt, lit = et * jnp.cos(a_im * dt * seg_len), et * jnp.sin(a_im * dt * seg_len)
    den = a_re * a_re + a_im * a_im
    qr = ((lr - 1.0) * a_re + li * a_im) / den
    qi = (li * a_re - (lr - 1.0) * a_im) / den
    br = qr[..., None] * b_re - qi[..., None] * b_im
    bi = qr[..., None] * b_im + qi[..., None] * b_re
    return lr, li, lrt, lit, br, bi


def _s5_pack(lr, li, lrt, lit, br, bi, c_re, c_im):
    eye = jnp.eye(GROUPS_PER_TILE, dtype=F32)

    def bd_b(b):
        b5 = b.reshape(2, N_TILES, GROUPS_PER_TILE, N_STATE, GROUP)
        return jnp.einsum("dtgph,gk->dtghkp", b5, eye).reshape(2, N_TILES, LANES, STATE_COLS)

    def bd_c(c):
        c5 = c.reshape(2, N_TILES, GROUPS_PER_TILE, GROUP, N_STATE)
        return jnp.einsum("dtghp,gk->dtkpgh", c5, eye).reshape(2, N_TILES, STATE_COLS, LANES)

    bblk = jnp.concatenate([bd_b(br), bd_b(bi)], axis=3)
    cblk = jnp.concatenate([bd_c(c_re), -bd_c(c_im)], axis=2)
    lam = jnp.stack([v.reshape(2, N_TILES, STATE_COLS) for v in (lr, li, lrt, lit)], axis=2)
    return bblk, cblk, lam


def _s5_unpack(dbblk, dcblk, dlam):
    eye = jnp.eye(GROUPS_PER_TILE, dtype=F32)

    def diag_b(d):
        d6 = d.reshape(2, N_TILES, GROUPS_PER_TILE, GROUP, GROUPS_PER_TILE, N_STATE)
        return jnp.einsum("dtghkp,gk->dtgph", d6, eye).reshape(2, N_GROUPS, N_STATE, GROUP)

    def diag_c(d):
        d6 = d.reshape(2, N_TILES, GROUPS_PER_TILE, GROUP, GROUPS_PER_TILE, N_STATE)
        return jnp.einsum("dtghkp,gk->dtghp", d6, eye).reshape(2, N_GROUPS, GROUP, N_STATE)

    dbr, dbi = diag_b(dbblk[..., :STATE_COLS]), diag_b(dbblk[..., STATE_COLS:])
    dcr, dci = diag_c(dcblk[..., :STATE_COLS]), -diag_c(dcblk[..., STATE_COLS:])
    dlr = dlam[:, :, 0, :].reshape(2, N_GROUPS, N_STATE)
    dli = dlam[:, :, 1, :].reshape(2, N_GROUPS, N_STATE)
    return dlr, dli, dbr, dbi, dcr, dci


def _pos():
    return lax.axis_index("x"), lax.axis_index("y"), lax.axis_index("c")


def _remote(src, dst, ssem, rsem, dev):
    return pltpu.make_async_remote_copy(src_ref=src, dst_ref=dst, send_sem=ssem, recv_sem=rsem,
                                        device_id=dev, device_id_type=MESH)


_PIECES = (
    ("w_in", "in", D_MODEL, IN_WIDTH // N_CHIPS, 0, IN_WIDTH // N_CHIPS, 0),
    ("w_glu", "glu", D_MODEL // N_CHIPS, D_MODEL, D_MODEL // N_CHIPS, 0, 0),
    ("w_out", "out", D_MODEL // N_CHIPS, D_MODEL, D_MODEL // N_CHIPS, 0, 0),
    ("w_ffn_gate", "gu", D_MODEL, D_FF // N_CHIPS, 0, D_FF // N_CHIPS, 0),
    ("w_ffn_up", "gu", D_MODEL, D_FF // N_CHIPS, 0, D_FF // N_CHIPS, D_FF),
    ("w_ffn_down", "down", D_FF // N_CHIPS, D_MODEL, D_FF // N_CHIPS, 0, 0),
)
_BUFFERS = (("in", D_MODEL, IN_WIDTH), ("glu", D_MODEL, D_MODEL), ("out", D_MODEL, D_MODEL),
            ("gu", D_MODEL, 2 * D_FF), ("down", D_FF, D_MODEL))
_BUF_INDEX = {name: t for t, (name, _, _) in enumerate(_BUFFERS)}
N_PIECES = len(_PIECES)
N_BUFFERS = len(_BUFFERS)


def _own_block(piece, tm):
    _, _, _, cs, rstep, cstep, coff = piece
    return lambda i, chip: (i + chip * (rstep // tm), coff // cs + chip * (cstep // cs))


def _cast_place(piece, w3, layer, prev, chip_arr, name):
    _, r, cc = w3.shape
    _, rf, cf = _BUFFERS[_BUF_INDEX[piece[1]]]
    tm = _tile(r, 256)
    own = _own_block(piece, tm)

    def body(s_ref, w_ref, *rest):
        rest[-1][...] = w_ref[...].astype(BF16)

    in_specs = [pl.BlockSpec((None, tm, cc), lambda i, s: (layer, i, 0))]
    args = [w3]
    aliases = {}
    if prev is not None:
        in_specs.append(pl.BlockSpec(memory_space=pl.ANY))
        args.append(prev)
        aliases = {2: 0}
    return pl.pallas_call(
        body, name=name,
        grid_spec=pltpu.PrefetchScalarGridSpec(
            num_scalar_prefetch=1, grid=(r // tm,), in_specs=in_specs,
            out_specs=pl.BlockSpec((tm, cc), lambda i, s: own(i, s[0]))),
        out_shape=jax.ShapeDtypeStruct((rf, cf), BF16), input_output_aliases=aliases,
        compiler_params=_params(("parallel",)),
    )(chip_arr, *args)


_GATHER_GROUPS = ((0, (0,)), (0, (1, 2, 3, 4, 5)), (1, (0,)), (1, (1, 2, 3, 4, 5)))
_SPLIT_EFFECT = pltpu.SideEffectType.DATAFLOW_SIDE_EFFECTING
SEM_SPEC = pl.BlockSpec(memory_space=pltpu.SEMAPHORE)
BF16_ROWS = 2 * SUBLANES


def _group_keys(g):
    layer, pieces = _GATHER_GROUPS[g]
    keys = []
    for p in pieces:
        if (_PIECES[p][1], layer) not in keys:
            keys.append((_PIECES[p][1], layer))
    return keys


def _half_view(ref, piece, j, c):
    _, _, rs, cs, rstep, cstep, coff = piece
    half = rs // 2
    return ref.at[pl.ds(pl.multiple_of(j * rstep + c * half, BF16_ROWS), half), pl.ds(coff + j * cstep, cs)]


def _for_my_chip(fn):
    x, y, _ = _pos()
    for mine in range(N_CHIPS):
        pl.when(2 * x + y == mine)(functools.partial(fn, mine, [j for j in range(N_CHIPS) if j != mine]))


def _gather_start(groups, placed):
    keys = [k for g in groups for k in _group_keys(g)]
    nb, ng = len(keys), len(groups)

    def body(*refs):
        bufs = dict(zip(keys, refs[nb:2 * nb]))
        ssems = refs[2 * nb:2 * nb + ng]
        rsems = refs[2 * nb + ng:2 * nb + 2 * ng]
        token = refs[2 * nb + 2 * ng]
        _, _, c = _pos()

        def send(mine, others):
            for t, g in enumerate(groups):
                layer, pieces = _GATHER_GROUPS[g]
                for k, p in enumerate(pieces):
                    view = _half_view(bufs[(_PIECES[p][1], layer)], _PIECES[p], mine, c)
                    for j in others:
                        _remote(view, view, ssems[t].at[k * N_CHIPS + j], rsems[t].at[k * N_CHIPS + mine],
                                (j // 2, j % 2, c)).start()

        _for_my_chip(send)
        token[...] = jnp.zeros(token.shape, token.dtype)

    sems = [pltpu.SemaphoreType.DMA((N_CHIPS * len(_GATHER_GROUPS[g][1]),)) for g in groups]
    shapes = [jax.ShapeDtypeStruct(placed[k].shape, placed[k].dtype) for k in keys]
    res = pl.pallas_call(
        body, name="gather_start_g%d" % groups[0],
        in_specs=[HBM_SPEC] * nb,
        out_specs=[HBM_SPEC] * nb + [SEM_SPEC] * (2 * ng) + [pl.BlockSpec(memory_space=pltpu.VMEM)],
        out_shape=shapes + sems + sems + [jax.ShapeDtypeStruct((SUBLANES, LANES), F32)],
        input_output_aliases={t: t for t in range(nb)},
        compiler_params=_params(has_side_effects=_SPLIT_EFFECT),
    )(*[pltpu.with_memory_space_constraint(placed[k], pltpu.HBM) for k in keys])
    return (dict(zip(keys, res[:nb])), dict(zip(groups, res[nb:nb + ng])),
            dict(zip(groups, res[nb + ng:nb + 2 * ng])), res[nb + 2 * ng])


def _gather_wait(g, bufs, ssem, rsem, after):
    layer, pieces = _GATHER_GROUPS[g]
    keys = _group_keys(g)
    nb = len(keys)

    def body(*refs):
        ssem_ref, rsem_ref = refs[nb], refs[nb + 1]
        land = dict(zip(keys, refs[nb + 3:]))
        _, _, c = _pos()

        def wait(mine, others):
            for k, p in enumerate(pieces):
                ref = land[(_PIECES[p][1], layer)]
                for j in others:
                    cp = _remote(_half_view(ref, _PIECES[p], mine, c), _half_view(ref, _PIECES[p], j, c),
                                 ssem_ref.at[k * N_CHIPS + j], rsem_ref.at[k * N_CHIPS + j], (j // 2, j % 2, c))
                    cp.wait_send()
                    cp.wait_recv()

        _for_my_chip(wait)

    return pl.pallas_call(
        body, name="gather_wait_g%d" % g,
        in_specs=[HBM_SPEC] * nb + [SEM_SPEC, SEM_SPEC, pl.BlockSpec(memory_space=pl.ANY)],
        out_specs=[HBM_SPEC] * nb,
        out_shape=[jax.ShapeDtypeStruct(a.shape, a.dtype) for a in bufs],
        input_output_aliases={t: t for t in range(nb)},
        compiler_params=_params(has_side_effects=_SPLIT_EFFECT),
    )(*bufs, ssem, rsem, after)


def _gather_forward(g, bufs):
    layer, pieces = _GATHER_GROUPS[g]
    keys = _group_keys(g)
    nb = len(keys)

    def body(*refs):
        land = dict(zip(keys, refs[nb:2 * nb]))
        ssem, rsem = refs[2 * nb:]
        x, y, c = _pos()

        def forward(mine, others):
            cps = []
            for k, p in enumerate(pieces):
                ref = land[(_PIECES[p][1], layer)]
                for j in others:
                    view = _half_view(ref, _PIECES[p], j, c)
                    cp = _remote(view, view, ssem.at[k * N_CHIPS + j], rsem.at[k * N_CHIPS + j], (x, y, 1 - c))
                    cp.start()
                    cps.append(cp)
            for k, p in enumerate(pieces):
                ref = land[(_PIECES[p][1], layer)]
                for j in others:
                    view = _half_view(ref, _PIECES[p], j, 1 - c)
                    _remote(view, view, ssem.at[k * N_CHIPS + j], rsem.at[k * N_CHIPS + j], (x, y, 1 - c)).wait_recv()
            for cp in cps:
                cp.wait_send()

        _for_my_chip(forward)

    nsem = N_CHIPS * len(pieces)
    return pl.pallas_call(
        body, name="gather_forward_g%d" % g,
        in_specs=[HBM_SPEC] * nb, out_specs=[HBM_SPEC] * nb,
        out_shape=[jax.ShapeDtypeStruct(a.shape, a.dtype) for a in bufs],
        input_output_aliases={t: t for t in range(nb)},
        scratch_shapes=[pltpu.SemaphoreType.DMA((nsem,)), pltpu.SemaphoreType.DMA((nsem,))],
        compiler_params=_params(has_side_effects=True),
    )(*bufs)


def _forward_start(g, bufs):
    layer, pieces = _GATHER_GROUPS[g]
    keys = _group_keys(g)
    nb = len(keys)

    def body(*refs):
        land = dict(zip(keys, refs[nb:2 * nb]))
        ssem, rsem, token = refs[2 * nb:]
        x, y, c = _pos()

        def forward(mine, others):
            for k, p in enumerate(pieces):
                for j in others:
                    view = _half_view(land[(_PIECES[p][1], layer)], _PIECES[p], j, c)
                    _remote(view, view, ssem.at[k * N_CHIPS + j], rsem.at[k * N_CHIPS + j], (x, y, 1 - c)).start()

        _for_my_chip(forward)
        token[...] = jnp.zeros(token.shape, token.dtype)

    sem = pltpu.SemaphoreType.DMA((N_CHIPS * len(pieces),))
    res = pl.pallas_call(
        body, name="forward_start_g%d" % g,
        in_specs=[HBM_SPEC] * nb,
        out_specs=[HBM_SPEC] * nb + [SEM_SPEC, SEM_SPEC, pl.BlockSpec(memory_space=pltpu.VMEM)],
        out_shape=[jax.ShapeDtypeStruct(a.shape, a.dtype) for a in bufs]
        + [sem, sem, jax.ShapeDtypeStruct((SUBLANES, LANES), F32)],
        input_output_aliases={t: t for t in range(nb)},
        compiler_params=_params(has_side_effects=_SPLIT_EFFECT),
    )(*bufs)
    return list(res[:nb]), res[nb], res[nb + 1], res[nb + 2]


def _forward_wait(g, bufs, ssem, rsem, after):
    layer, pieces = _GATHER_GROUPS[g]
    keys = _group_keys(g)
    nb = len(keys)

    def body(*refs):
        ssem_ref, rsem_ref = refs[nb], refs[nb + 1]
        land = dict(zip(keys, refs[nb + 3:]))
        x, y, c = _pos()

        def wait(mine, others):
            for k, p in enumerate(pieces):
                ref = land[(_PIECES[p][1], layer)]
                for j in others:
                    cp = _remote(_half_view(ref, _PIECES[p], j, c), _half_view(ref, _PIECES[p], j, 1 - c),
                                 ssem_ref.at[k * N_CHIPS + j], rsem_ref.at[k * N_CHIPS + j], (x, y, 1 - c))
                    cp.wait_send()
                    cp.wait_recv()

        _for_my_chip(wait)

    return pl.pallas_call(
        body, name="forward_wait_g%d" % g,
        in_specs=[HBM_SPEC] * nb + [SEM_SPEC, SEM_SPEC, pl.BlockSpec(memory_space=pl.ANY)],
        out_specs=[HBM_SPEC] * nb,
        out_shape=[jax.ShapeDtypeStruct(a.shape, a.dtype) for a in bufs],
        input_output_aliases={t: t for t in range(nb)},
        compiler_params=_params(has_side_effects=_SPLIT_EFFECT),
    )(*bufs, ssem, rsem, after)


_REDUCE_GROUPS = (
    ((5, 1), (3, 1), (4, 1), (2, 1), (1, 1), (0, 1)),
    ((5, 0), (3, 0), (4, 0)),
    ((2, 0), (1, 0)),
    ((0, 0),),
)


def _reduce_keys(group):
    keys = []
    for p, layer in group:
        if (_PIECES[p][1], layer) not in keys:
            keys.append((_PIECES[p][1], layer))
    return keys


def _half_block(piece, tm):
    _, _, rs, cs, rstep, cstep, coff = piece
    return lambda i, j, c: (j * (rstep // tm) + c * (rs // 2 // tm) + i, coff // cs + j * (cstep // cs))


def _swap_start(g, dwb):
    group = _REDUCE_GROUPS[g]
    keys = _reduce_keys(group)
    nk = len(keys)

    def body(*refs):
        src = dict(zip(keys, refs[nk:2 * nk]))
        dst = dict(zip(keys, refs[2 * nk:3 * nk]))
        ssem, rsem, token = refs[3 * nk:]
        x, y, c = _pos()
        for k, (p, layer) in enumerate(group):
            key = (_PIECES[p][1], layer)
            for j in range(N_CHIPS):
                _remote(_half_view(src[key], _PIECES[p], j, 1 - c), _half_view(dst[key], _PIECES[p], j, 1 - c),
                        ssem.at[k * N_CHIPS + j], rsem.at[k * N_CHIPS + j], (x, y, 1 - c)).start()
        token[...] = jnp.zeros(token.shape, token.dtype)

    sem = pltpu.SemaphoreType.DMA((N_CHIPS * len(group),))
    shapes = [jax.ShapeDtypeStruct(dwb[k].shape, BF16) for k in keys]
    res = pl.pallas_call(
        body, name="swap_start_g%d" % g,
        in_specs=[HBM_SPEC] * nk,
        out_specs=[HBM_SPEC] * (2 * nk) + [SEM_SPEC, SEM_SPEC, pl.BlockSpec(memory_space=pltpu.VMEM)],
        out_shape=shapes + shapes + [sem, sem, jax.ShapeDtypeStruct((SUBLANES, LANES), F32)],
        input_output_aliases={t: t for t in range(nk)},
        compiler_params=_params(has_side_effects=_SPLIT_EFFECT),
    )(*[pltpu.with_memory_space_constraint(dwb[k], pltpu.HBM) for k in keys])
    return list(res[:nk]), list(res[nk:2 * nk]), res[2 * nk], res[2 * nk + 1], res[2 * nk + 2]


def _swap_wait(g, own, land, ssem, rsem, after):
    group = _REDUCE_GROUPS[g]
    keys = _reduce_keys(group)
    nk = len(keys)

    def body(*refs):
        ssem_ref, rsem_ref = refs[2 * nk], refs[2 * nk + 1]
        src = dict(zip(keys, refs[2 * nk + 3:3 * nk + 3]))
        dst = dict(zip(keys, refs[3 * nk + 3:]))
        x, y, c = _pos()
        for k, (p, layer) in enumerate(group):
            key = (_PIECES[p][1], layer)
            for j in range(N_CHIPS):
                cp = _remote(_half_view(src[key], _PIECES[p], j, 1 - c), _half_view(dst[key], _PIECES[p], j, c),
                             ssem_ref.at[k * N_CHIPS + j], rsem_ref.at[k * N_CHIPS + j], (x, y, 1 - c))
                cp.wait_send()
                cp.wait_recv()

    res = pl.pallas_call(
        body, name="swap_wait_g%d" % g,
        in_specs=[HBM_SPEC] * (2 * nk) + [SEM_SPEC, SEM_SPEC, pl.BlockSpec(memory_space=pl.ANY)],
        out_specs=[HBM_SPEC] * (2 * nk),
        out_shape=[jax.ShapeDtypeStruct(a.shape, a.dtype) for a in list(own) + list(land)],
        input_output_aliases={t: t for t in range(2 * nk)},
        compiler_params=_params(has_side_effects=_SPLIT_EFFECT),
    )(*own, *land, ssem, rsem, after)
    return dict(zip(keys, res[nk:]))


def _chip_partial(piece, dw, got, prev, c_arr, name):
    _, _, rs, cs, _, _, _ = piece
    half = rs // 2
    tm = _tile(half, 256)
    blk = _half_block(piece, tm)

    def body(s_ref, dw_ref, got_ref, *rest):
        rest[-1][...] = (dw_ref[...] + got_ref[...].astype(F32)).astype(BF16)

    spec = pl.BlockSpec((tm, cs), lambda j, i, s: blk(i, j, s[0]))
    in_specs = [spec, spec]
    args = [dw, got]
    aliases = {}
    if prev is not None:
        in_specs.append(pl.BlockSpec(memory_space=pl.ANY))
        args.append(prev)
        aliases = {3: 0}
    return pl.pallas_call(
        body, name=name,
        grid_spec=pltpu.PrefetchScalarGridSpec(
            num_scalar_prefetch=1, grid=(N_CHIPS, half // tm), in_specs=in_specs, out_specs=spec),
        out_shape=jax.ShapeDtypeStruct(dw.shape, BF16), input_output_aliases=aliases,
        compiler_params=_params(("parallel", "parallel")),
    )(c_arr, *args)


def _scatter_start(g, partials):
    group = _REDUCE_GROUPS[g]
    keys = _reduce_keys(group)
    nk, n = len(keys), len(group)

    def body(*refs):
        pt = dict(zip(keys, refs[nk:2 * nk]))
        land = refs[2 * nk:2 * nk + n]
        ssem, rsem, token = refs[2 * nk + n:]
        _, _, c = _pos()

        def send(mine, others):
            for k, (p, layer) in enumerate(group):
                for j in others:
                    _remote(_half_view(pt[(_PIECES[p][1], layer)], _PIECES[p], j, c), land[k].at[mine],
                            ssem.at[k * N_CHIPS + j], rsem.at[k * N_CHIPS + mine], (j // 2, j % 2, c)).start()

        _for_my_chip(send)
        token[...] = jnp.zeros(token.shape, token.dtype)

    sem = pltpu.SemaphoreType.DMA((N_CHIPS * n,))
    res = pl.pallas_call(
        body, name="scatter_start_g%d" % g,
        in_specs=[HBM_SPEC] * nk,
        out_specs=[HBM_SPEC] * (nk + n) + [SEM_SPEC, SEM_SPEC, pl.BlockSpec(memory_space=pltpu.VMEM)],
        out_shape=([jax.ShapeDtypeStruct(partials[k].shape, BF16) for k in keys]
                   + [jax.ShapeDtypeStruct((N_CHIPS, _PIECES[p][2] // 2, _PIECES[p][3]), BF16) for p, _ in group]
                   + [sem, sem, jax.ShapeDtypeStruct((SUBLANES, LANES), F32)]),
        input_output_aliases={t: t for t in range(nk)},
        compiler_params=_params(has_side_effects=_SPLIT_EFFECT),
    )(*[pltpu.with_memory_space_constraint(partials[k], pltpu.HBM) for k in keys])
    return list(res[:nk]), list(res[nk:nk + n]), res[nk + n], res[nk + n + 1], res[nk + n + 2]


def _scatter_wait(g, partials, land, ssem, rsem, after):
    group = _REDUCE_GROUPS[g]
    keys = _reduce_keys(group)
    nk, n = len(keys), len(group)

    def body(*refs):
        ssem_ref, rsem_ref = refs[nk + n], refs[nk + n + 1]
        pt = dict(zip(keys, refs[nk + n + 3:2 * nk + n + 3]))
        land_ref = refs[2 * nk + n + 3:]
        _, _, c = _pos()

        def wait(mine, others):
            for k, (p, layer) in enumerate(group):
                for j in others:
                    cp = _remote(_half_view(pt[(_PIECES[p][1], layer)], _PIECES[p], j, c), land_ref[k].at[j],
                                 ssem_ref.at[k * N_CHIPS + j], rsem_ref.at[k * N_CHIPS + j], (j // 2, j % 2, c))
                    cp.wait_send()
                    cp.wait_recv()

        _for_my_chip(wait)

    res = pl.pallas_call(
        body, name="scatter_wait_g%d" % g,
        in_specs=[HBM_SPEC] * (nk + n) + [SEM_SPEC, SEM_SPEC, pl.BlockSpec(memory_space=pl.ANY)],
        out_specs=[HBM_SPEC] * (nk + n),
        out_shape=[jax.ShapeDtypeStruct(a.shape, a.dtype) for a in list(partials) + list(land)],
        input_output_aliases={t: t for t in range(nk + n)},
        compiler_params=_params(has_side_effects=_SPLIT_EFFECT),
    )(*partials, *land, ssem, rsem, after)
    return list(res[nk:])


def _reduce_half(piece, layer, dw, got, land, prev, idx, name):
    _, _, rs, cs, _, _, _ = piece
    half = rs // 2
    tm = _tile(half, 256)
    blk = _half_block(piece, tm)

    def body(s_ref, dw_ref, got_ref, r1, r2, r3, *rest):
        acc = dw_ref[...] + got_ref[...].astype(F32)
        for r in (r1, r2, r3):
            acc = acc + r[...].astype(F32)
        rest[-1][...] = acc

    def land_map(k):
        return lambda i, s: ((s[1] + k) % N_CHIPS, i, 0)

    own = pl.BlockSpec((tm, cs), lambda i, s: blk(i, s[1], s[0]))
    in_specs = [own, own] + [pl.BlockSpec((None, tm, cs), land_map(k)) for k in (1, 2, 3)]
    args = [dw, got, land, land, land]
    aliases = {}
    if prev is not None:
        in_specs.append(pl.BlockSpec(memory_space=pl.ANY))
        args.append(prev)
        aliases = {6: 0}
    return pl.pallas_call(
        body, name=name,
        grid_spec=pltpu.PrefetchScalarGridSpec(
            num_scalar_prefetch=1, grid=(half // tm,), in_specs=in_specs,
            out_specs=pl.BlockSpec((None, tm, cs), lambda i, s: (layer, s[0] * (half // tm) + i, 0))),
        out_shape=jax.ShapeDtypeStruct((DEPTH, rs, cs), F32), input_output_aliases=aliases,
        compiler_params=_params(("parallel",)),
    )(idx, *args)


def _share_halves(reduced):
    def body(*refs):
        buf = refs[N_PIECES:2 * N_PIECES]
        ssem, rsem = refs[2 * N_PIECES:]
        x, y, c = _pos()

        def half(p, layer, cc):
            rows = _PIECES[p][2] // 2
            return buf[p].at[layer, pl.ds(pl.multiple_of(cc * rows, SUBLANES), rows), :]

        pairs = [(p, layer) for p in range(N_PIECES) for layer in range(DEPTH)]
        rem = [_remote(half(p, layer, c), half(p, layer, c), ssem.at[k], rsem.at[k], (x, y, 1 - c))
               for k, (p, layer) in enumerate(pairs)]
        for cp in rem:
            cp.start()
        for k, (p, layer) in enumerate(pairs):
            rem[k].wait_send()
            _remote(half(p, layer, 1 - c), half(p, layer, 1 - c), ssem.at[k], rsem.at[k], (x, y, 1 - c)).wait_recv()

    nsem = N_PIECES * DEPTH
    return pl.pallas_call(
        body, name="share_halves",
        in_specs=[HBM_SPEC] * N_PIECES, out_specs=[HBM_SPEC] * N_PIECES,
        out_shape=[jax.ShapeDtypeStruct((DEPTH, p[2], p[3]), F32) for p in _PIECES],
        input_output_aliases={t: t for t in range(N_PIECES)},
        scratch_shapes=[pltpu.SemaphoreType.DMA((nsem,)), pltpu.SemaphoreType.DMA((nsem,))],
        compiler_params=_params(has_side_effects=True),
    )(*reduced)


N_DEV = 8


def _place_slot(v, me_arr, take_block):
    rows = v.shape[0] // N_DEV if take_block else v.shape[0]
    tm = _tile(rows, 512)
    steps = rows // tm

    def body(s_ref, v_ref, out_ref):
        out_ref[...] = v_ref[...]

    return pl.pallas_call(
        body, name="place_small_block" if take_block else "place_small_sum",
        grid_spec=pltpu.PrefetchScalarGridSpec(
            num_scalar_prefetch=1, grid=(steps,),
            in_specs=[pl.BlockSpec((tm, LANES), lambda i, s: (s[0] * steps * take_block + i, 0))],
            out_specs=pl.BlockSpec((None, tm, LANES), lambda i, s: (s[0], i, 0))),
        out_shape=jax.ShapeDtypeStruct((N_DEV, rows, LANES), F32),
        compiler_params=_params(("parallel",)),
    )(me_arr, v)


def _all_peers():
    x, y, c = _pos()
    flip = lambda v, f: 1 - v if f else v
    return (x, y, c), [(flip(x, a), flip(y, b), flip(c, d))
                       for a in (0, 1) for b in (0, 1) for d in (0, 1) if a or b or d]


def _slot_index(dev):
    return 4 * dev[0] + 2 * dev[1] + dev[2]


def _exchange_start(g, src, name):
    rows = g.shape[1]
    n_in = 1 if src is None else 2

    def body(*refs):
        g_ref = refs[n_in]
        src_ref = refs[n_in + 1] if src is not None else None
        ssem, rsem, token = refs[2 * n_in:]
        me, peers = _all_peers()
        for k, dev in enumerate(peers):
            if src is None:
                mine = g_ref.at[_slot_index(me)]
            else:
                mine = src_ref.at[pl.ds(pl.multiple_of(_slot_index(dev) * rows, SUBLANES), rows), :]
            _remote(mine, g_ref.at[_slot_index(me)], ssem.at[k], rsem.at[k], dev).start()
        token[...] = jnp.zeros(token.shape, token.dtype)

    sem = pltpu.SemaphoreType.DMA((N_DEV - 1,))
    args = [g] if src is None else [g, src]
    res = pl.pallas_call(
        body, name=name,
        in_specs=[HBM_SPEC] * n_in,
        out_specs=[HBM_SPEC] * n_in + [SEM_SPEC, SEM_SPEC, pl.BlockSpec(memory_space=pltpu.VMEM)],
        out_shape=[jax.ShapeDtypeStruct(a.shape, a.dtype) for a in args]
        + [sem, sem, jax.ShapeDtypeStruct((SUBLANES, LANES), F32)],
        input_output_aliases={t: t for t in range(n_in)},
        compiler_params=_params(has_side_effects=_SPLIT_EFFECT),
    )(*[pltpu.with_memory_space_constraint(a, pltpu.HBM) for a in args])
    return list(res[:n_in]), res[n_in], res[n_in + 1], res[n_in + 2]


def _exchange_wait(bufs, ssem, rsem, after, name):
    n_in = len(bufs)

    def body(*refs):
        ssem_ref, rsem_ref = refs[n_in], refs[n_in + 1]
        g_ref = refs[n_in + 3]
        me, peers = _all_peers()
        for k, dev in enumerate(peers):
            cp = _remote(g_ref.at[_slot_index(me)], g_ref.at[_slot_index(dev)], ssem_ref.at[k], rsem_ref.at[k], dev)
            cp.wait_send()
            cp.wait_recv()

    res = pl.pallas_call(
        body, name=name,
        in_specs=[HBM_SPEC] * n_in + [SEM_SPEC, SEM_SPEC, pl.BlockSpec(memory_space=pl.ANY)],
        out_specs=[HBM_SPEC] * n_in,
        out_shape=[jax.ShapeDtypeStruct(a.shape, a.dtype) for a in bufs],
        input_output_aliases={t: t for t in range(n_in)},
        compiler_params=_params(has_side_effects=_SPLIT_EFFECT),
    )(*bufs, ssem, rsem, after)
    return res[0]


def _sum_slots(g, name):
    n, rows, _ = g.shape
    tm = _tile(rows, 512)

    def body(g_ref, out_ref):
        acc = g_ref[0]
        for k in range(1, n):
            acc = acc + g_ref[k]
        out_ref[...] = acc

    return pl.pallas_call(
        body, name=name, grid=(rows // tm,),
        in_specs=[pl.BlockSpec((n, tm, LANES), lambda i: (0, i, 0))],
        out_specs=pl.BlockSpec((tm, LANES), lambda i: (i, 0)),
        out_shape=jax.ShapeDtypeStruct((rows, LANES), F32),
        compiler_params=_params(("parallel",)),
    )(g)


_TINY = ("ln_mix_g", "ret_log_gamma", "ssm_a_re", "ssm_a_im", "ssm_log_dt", "ssm_d", "b_glu", "ln_ffn_g", "ln_final_g")
_MID = ("ssm_b_re", "ssm_b_im", "ssm_c_re", "ssm_c_im")
_SMALL = _TINY + _MID
_FLAT_ALIGN = LANES * LANES
_FLAT_ROWS = 1024


def _flat_rows(like, names):
    rows = sum((math.prod(like[n].shape) + (-math.prod(like[n].shape)) % _FLAT_ALIGN) // LANES for n in names)
    return rows + (-rows) % _FLAT_ROWS


def _flatten(d, names):
    parts = []
    for n in names:
        f = d[n].reshape(-1)
        parts.append(jnp.pad(f, (0, (-f.shape[0]) % _FLAT_ALIGN)))
    total = sum(p.shape[0] for p in parts)
    parts.append(jnp.zeros(((-total) % (_FLAT_ROWS * LANES),), F32))
    return jnp.concatenate(parts).reshape(-1, LANES)


def _unflatten(flat, like, names):
    out, row = {}, 0
    for n in names:
        size = math.prod(like[n].shape)
        rows = (size + (-size) % _FLAT_ALIGN) // LANES
        part = lax.optimization_barrier(flat[row:row + rows])
        out[n] = part.reshape(-1)[:size].reshape(like[n].shape)
        row += rows
    return out


_BIG = ("w_in", "w_glu", "w_out", "w_ffn_gate", "w_ffn_up", "w_ffn_down")
_WEIGHTS = ("ln_mix_g", "w_in", "ret_log_gamma", "ssm_a_re", "ssm_a_im", "ssm_log_dt", "ssm_b_re", "ssm_b_im",
            "ssm_c_re", "ssm_c_im", "ssm_d", "w_glu", "b_glu", "w_out", "ln_ffn_g", "w_ffn_gate", "w_ffn_up",
            "w_ffn_down", "ln_final_g")


def _rope_tables(seq):
    half = QK_DIM // 2
    inv = 1.0 / (ROPE_BASE ** (jnp.arange(half, dtype=F32) / half))
    ang = jnp.arange(seq, dtype=F32)[:, None] * inv[None, :]
    return jnp.cos(ang), jnp.sin(ang)


def _step(w, m, v, x, target):
    seq = x.shape[0]
    seg_len = float(seq // SEGMENTS)
    c_idx = lax.axis_index("c").astype(jnp.int32)
    chip_idx = (2 * lax.axis_index("x") + lax.axis_index("y")).astype(jnp.int32)
    c_arr = jnp.stack([c_idx])
    idx_arr = jnp.stack([c_idx, chip_idx])

    chip_arr = jnp.stack([chip_idx])
    placed = {}

    def cast(pieces, layer):
        for p in pieces:
            key = (_PIECES[p][1], layer)
            placed[key] = _cast_place(_PIECES[p], w[_PIECES[p][0]], layer, placed.get(key), chip_arr,
                                      "cast_%s_l%d" % (_PIECES[p][0], layer))

    for layer, pieces in _GATHER_GROUPS:
        cast(pieces, layer)
    flying, ssems, rsems, token = _gather_start(list(range(len(_GATHER_GROUPS))), placed)
    wf = {b[0]: [None] * DEPTH for b in _BUFFERS}

    handing = {}

    def arrive(g, after):
        ks = _group_keys(g)
        landed = _gather_wait(g, [flying[k] for k in ks], ssems[g], rsems[g], after)
        for k, a in zip(ks, _gather_forward(g, landed)):
            wf[k[0]][k[1]] = a

    def hand_over(g, after):
        ks = _group_keys(g)
        landed = _gather_wait(g, [flying[k] for k in ks], ssems[g], rsems[g], after)
        bufs, fs, fr, tok = _forward_start(g, landed)
        handing[g] = (bufs, fs, fr)
        return tok[0:1, 0:1]

    def complete(g, after):
        for k, a in zip(_group_keys(g), _forward_wait(g, *handing[g], after)):
            wf[k[0]][k[1]] = a

    cos, sin = _rope_tables(seq)

    saved = []
    xc = x + token[0, 0]
    for i in range(DEPTH):
        t = "_l%d" % i
        s = {"x_in": xc}
        if i == 0:
            s["h"] = _rms_fwd(xc, w["ln_mix_g"][i:i + 1], "rms_mix" + t)
            arrive(0, s["h"])
        else:
            s["h"] = _rms_fwd(xc, w["ln_mix_g"][i:i + 1] + next_in, "rms_mix" + t)
        s["proj"] = _matmul(s["h"], wf["in"][i], "nn", [F32], name="mm_in" + t)[0]
        s["qr"], s["kr"] = _rot_fwd(s["proj"], cos, sin, "rot" + t)
        s["lg"] = jnp.broadcast_to(w["ret_log_gamma"][i].T[:, :, None], (HEADS, 2, LANES))
        s["y"] = _ret_fwd(s["qr"], s["kr"], s["proj"], s["lg"], "ret" + t)
        s5_raw = (w["ssm_a_re"][i], w["ssm_a_im"][i], w["ssm_log_dt"][i], w["ssm_b_re"][i], w["ssm_b_im"][i])
        disc, s["disc_vjp"] = jax.vjp(functools.partial(_s5_discretize, seg_len=seg_len), *s5_raw)
        bblk, cblk, lam = _s5_pack(*disc, w["ssm_c_re"][i], w["ssm_c_im"][i])
        s["s5"] = (bblk.astype(BF16), cblk.astype(BF16), lam)
        s["s5y"] = _s5_fwd(s["proj"], *s["s5"], "s5" + t)
        d_skip = w["ssm_d"][i:i + 1] + hand_over(2 * i + 1, s["s5y"])
        s["ret"], s["ysg"], s["ysgb"] = _post1_fwd(s["y"], s["proj"], s["s5y"], d_skip, "post" + t)
        complete(2 * i + 1, s["ysgb"])
        s["z"] = _matmul(s["ysgb"], wf["glu"][i], "nn", [F32], name="mm_glu" + t)[0]
        s["merged"] = _merge_fwd(s["z"], s["ysg"], s["proj"], s["ret"], w["b_glu"][i:i + 1], "merge" + t)
        s["x1"] = _matmul(s["merged"], wf["out"][i], "nn", [F32], add=xc, name="mm_out" + t)[0]
        s["h2"] = _rms_fwd(s["x1"], w["ln_ffn_g"][i:i + 1], "rms_ffn" + t)
        s["ab"] = _matmul(s["h2"], wf["gu"][i], "nn", [F32], name="mm_gu" + t)[0]
        if i + 1 < DEPTH:
            next_in = hand_over(2 * i + 2, s["ab"])
        s["f"] = _glu_fwd(s["ab"], "glu" + t)
        xc = _matmul(s["f"], wf["down"][i], "nn", [F32], add=s["x1"], name="mm_down" + t)[0]
        if i + 1 < DEPTH:
            complete(2 * i + 2, xc)
        saved.append(s)

    dx, dxb, loss_row, dg_final = _loss_stage(xc, target, w["ln_final_g"][None, :], "loss")
    loss = lax.psum(loss_row[0, 0], ("x", "y", "c"))

    g_small = {"ln_final_g": dg_final[0]}
    per_layer = {n: [None] * DEPTH for n in _SMALL if n != "ln_final_g"}
    dws, got, swaps, flights = {}, {}, {}, []

    def dw_mm(a, b, buf, i, name):
        dws[(buf, i)] = _matmul(a, b, "tn", [F32, BF16], name=name)

    def depart(g):
        keys = _reduce_keys(_REDUCE_GROUPS[g])
        own, land, ssem, rsem, tok = _swap_start(g, {k: dws[k][1] for k in keys})
        swaps[g] = (own, land, ssem, rsem)
        return tok[0:1, 0:1]

    def proceed(g, after):
        group = _REDUCE_GROUPS[g]
        got.update(_swap_wait(g, *swaps[g], after))
        partials = {}
        for p, layer in group:
            key = (_PIECES[p][1], layer)
            partials[key] = _chip_partial(_PIECES[p], dws[key][0], got[key], partials.get(key), c_arr,
                                          "chip_partial_%s_l%d" % (_PIECES[p][0], layer))
        pt, land, ssem, rsem, tok = _scatter_start(g, partials)
        flights.append((g, pt, land, ssem, rsem))
        return tok[0:1, 0:1]

    for i in reversed(range(DEPTH)):
        t = "_l%d" % i
        s = saved[i]
        g_ffn, g_mix, d_skip = w["ln_ffn_g"][i:i + 1], w["ln_mix_g"][i:i + 1], w["ssm_d"][i:i + 1]
        dw_mm(s["f"], dxb, "down", i, "dw_down" + t)
        df = _matmul(dxb, wf["down"][i], "nt", [F32], name="dx_down" + t)[0]
        if i == 0:
            g_ffn = g_ffn + proceed(0, df)
        dab = _glu_bwd(s["ab"], df, "glu_bwd" + t)
        dw_mm(s["h2"], dab, "gu", i, "dw_gu" + t)
        if i == 0:
            g_ffn = g_ffn + depart(1)
        dh2 = _matmul(dab, wf["gu"][i], "nt", [F32], name="dx_gu" + t)[0]
        if i == 0:
            g_ffn = g_ffn + proceed(1, dh2)
        dx1, dx1b, dg = _rms_bwd(s["x1"], dh2, dx, g_ffn, "rms_ffn_bwd" + t)
        per_layer["ln_ffn_g"][i] = dg[0]

        dw_mm(s["merged"], dx1b, "out", i, "dw_out" + t)
        dmerged = _matmul(dx1b, wf["out"][i], "nt", [F32], name="dx_out" + t)[0]
        dz, dys_part, dgs, db = _merge_bwd(s["z"], s["ysg"], s["proj"], s["ret"], dmerged, w["b_glu"][i:i + 1],
                                           "merge_bwd" + t)
        per_layer["b_glu"][i] = db[0]
        dw_mm(s["ysgb"], dz, "glu", i, "dw_glu" + t)
        if i == 0:
            d_skip = d_skip + depart(2)
        dys = _matmul(dz, wf["glu"][i], "nt", [F32], add=dys_part, name="dx_glu" + t)[0]
        if i == 0:
            d_skip = d_skip + proceed(2, dys)
        dy, dgg, dgr, ds5, du_part, dd = _post1_bwd(s["y"], s["proj"], s["s5y"], dmerged, dys,
                                                    d_skip, "post_bwd" + t)
        per_layer["ssm_d"][i] = dd[0]
        du, dbblk, dcblk, dlam = _s5_bwd(s["proj"], ds5, du_part, *s["s5"], "s5_bwd" + t)
        dlr, dli, dbr, dbi, dcr, dci = _s5_unpack(dbblk, dcblk, dlam)
        zeros = jnp.zeros_like(dlr)
        da_re, da_im, dlog_dt, db_re, db_im = s["disc_vjp"]((dlr, dli, zeros, zeros, dbr, dbi))
        for n, val in (("ssm_a_re", da_re), ("ssm_a_im", da_im), ("ssm_log_dt", dlog_dt), ("ssm_b_re", db_re),
                       ("ssm_b_im", db_im), ("ssm_c_re", dcr), ("ssm_c_im", dci)):
            per_layer[n][i] = val
        dqr, dkr, dv, dlg = _ret_bwd(s["qr"], s["kr"], s["proj"], dy, s["lg"], "ret_bwd" + t)
        per_layer["ret_log_gamma"][i] = dlg[:, :, 0].T
        dqkv = _rot_bwd(dqr, dkr, dv, cos, sin, "rot_bwd" + t)
        dproj = jnp.concatenate([dqkv, dgg, du, dgr, dgs], axis=1)
        dw_mm(s["h"], dproj, "in", i, "dw_in" + t)
        if i == 0:
            g_mix = g_mix + depart(3)
        dh = _matmul(dproj, wf["in"][i], "nt", [F32], name="dx_in" + t)[0]
        if i == 0:
            g_mix = g_mix + proceed(3, dh)
        dx, dxb, dg = _rms_bwd(s["x_in"], dh, dx1, g_mix, "rms_mix_bwd" + t)
        per_layer["ln_mix_g"][i] = dg[0]
        if i == DEPTH - 1:
            dxb = dxb + depart(0).astype(BF16)

    for n in per_layer:
        g_small[n] = jnp.stack(per_layer[n])
    me_arr = jnp.stack([2 * chip_idx + c_idx])
    g_mine = _flatten(g_small, _SMALL)
    rs_bufs, rs_ssem, rs_rsem, small_token = _exchange_start(_place_slot(g_mine, me_arr, True), g_mine,
                                                             "small_scatter_start")

    reduced = [None] * N_PIECES
    for g, pt, land, ssem, rsem in flights:
        landed = _scatter_wait(g, pt, land, ssem, rsem, small_token)
        for (p, layer), buf in zip(_REDUCE_GROUPS[g], landed):
            key = (_PIECES[p][1], layer)
            reduced[p] = _reduce_half(_PIECES[p], layer, dws[key][0], got[key], buf, reduced[p], idx_arr,
                                      "reduce_%s_l%d" % (_PIECES[p][0], layer))
    g_big = dict(zip([p[0] for p in _PIECES], _share_halves(reduced)))

    landed = _exchange_wait(rs_bufs, rs_ssem, rs_rsem, g_big[_BIG[-1]], "small_scatter_wait")
    ag_bufs, ag_ssem, ag_rsem, _ = _exchange_start(_place_slot(_sum_slots(landed, "sum_small"), me_arr, False),
                                                   None, "small_gather_start")

    grads, delta, new_m, new_v = {}, {}, {}, {}
    for n in _BIG:
        d, r, cc = w[n].shape
        two_d = lambda a: a.reshape(d * r, cc)
        dl, mn, vn = _adamw(two_d(w[n]), two_d(g_big[n]), two_d(m[n]), two_d(v[n]), "adamw_" + n)
        grads[n], delta[n], new_m[n], new_v[n] = g_big[n], dl.reshape(d, r, cc), mn.reshape(d, r, cc), vn.reshape(d, r, cc)

    gathered = _exchange_wait(ag_bufs, ag_ssem, ag_rsem, delta[_BIG[-1]], "small_gather_wait")
    g_flat = gathered.reshape(-1, LANES)
    grads.update(_unflatten(g_flat, w, _SMALL))
    tiny_rows = _flat_rows(w, _TINY)
    dl, mn, vn = _adamw(_flatten(w, _TINY), g_flat[:tiny_rows], _flatten(m, _TINY), _flatten(v, _TINY), "adamw_tiny")
    for dst, flat in ((delta, dl), (new_m, mn), (new_v, vn)):
        dst.update(_unflatten(flat, w, _TINY))
    for n in _MID:
        delta[n], new_m[n], new_v[n] = _adamw_nd(w[n], grads[n], m[n], v[n], "adamw_" + n)
    return loss, dx, grads, delta, new_m, new_v


def kernel(x, ln_mix_g, w_in, ret_log_gamma, ssm_a_re, ssm_a_im, ssm_log_dt, ssm_b_re, ssm_b_im, ssm_c_re, ssm_c_im, ssm_d, w_glu, b_glu, w_out, ln_ffn_g, w_ffn_gate, w_ffn_up, w_ffn_down, ln_final_g, loss_target, m_ln_mix_g, m_w_in, m_ret_log_gamma, m_ssm_a_re, m_ssm_a_im, m_ssm_log_dt, m_ssm_b_re, m_ssm_b_im, m_ssm_c_re, m_ssm_c_im, m_ssm_d, m_w_glu, m_b_glu, m_w_out, m_ln_ffn_g, m_w_ffn_gate, m_w_ffn_up, m_w_ffn_down, m_ln_final_g, v_ln_mix_g, v_w_in, v_ret_log_gamma, v_ssm_a_re, v_ssm_a_im, v_ssm_log_dt, v_ssm_b_re, v_ssm_b_im, v_ssm_c_re, v_ssm_c_im, v_ssm_d, v_w_glu, v_b_glu, v_w_out, v_ln_ffn_g, v_w_ffn_gate, v_w_ffn_up, v_w_ffn_down, v_ln_final_g):
    given = dict(locals())
    w = {n: given[n] for n in _WEIGHTS}
    m = {n: given["m_" + n] for n in _WEIGHTS}
    v = {n: given["v_" + n] for n in _WEIGHTS}
    loss, dx, grads, delta, new_m, new_v = _step(w, m, v, x[0], loss_target[0])
    return (loss, dx[None], *[grads[n] for n in _WEIGHTS], *[delta[n] for n in _WEIGHTS],
            *[new_m[n] for n in _WEIGHTS], *[new_v[n] for n in _WEIGHTS])
```

```python
import functools
import math

import jax
import jax.numpy as jnp
from jax import lax
from jax.experimental import pallas as pl
from jax.experimental.pallas import tpu as pltpu

F32 = jnp.float32
BF16 = jnp.bfloat16

D_MODEL = 2048
DEPTH = 2
HEADS = 4
QK_DIM = 256
V_DIM = 512
QK_WIDTH = HEADS * QK_DIM
ROPE_BASE = 10000.0
GROUP = 16
N_GROUPS = D_MODEL // GROUP
N_STATE = 64
D_FF = 5632
IN_WIDTH = 2 * QK_WIDTH + 5 * D_MODEL
EPS = 1e-6
N_CHIPS = 4

ADAM_LR = 0.001
ADAM_B1 = 0.9
ADAM_B2 = 0.999
ADAM_EPS = 1e-08
ADAM_WD = 0.01
ADAM_STEP = 10

LANES = 128
SUBLANES = 8
VMEM_LIMIT = 56 * 1024 * 1024
SEGMENTS = SUBLANES
GROUPS_PER_TILE = LANES // GROUP
STATE_COLS = GROUPS_PER_TILE * N_STATE
N_TILES = D_MODEL // LANES
SCAN_UNROLL = 4

MESH = pl.DeviceIdType.MESH
HBM_SPEC = pl.BlockSpec(memory_space=pltpu.HBM)


def _params(sem=None, **kw):
    return pltpu.CompilerParams(dimension_semantics=sem, vmem_limit_bytes=VMEM_LIMIT, **kw)


def _tile(n, cap=1024):
    for t in (2048, 1024, 512, 256, 128, 64):
        if t <= cap and n % t == 0:
            return t
    raise ValueError(n)


def _rows_call(fn, rows, pars, row_outs, par_outs, *, tm, name):
    m = rows[0][0].shape[0]
    nr, npar, nro, npo = len(rows), len(pars), len(row_outs), len(par_outs)

    def body(*refs):
        rin = refs[:nr]
        pin = refs[nr:nr + npar]
        rout = refs[nr + npar:nr + npar + nro]
        pout = refs[nr + npar + nro:]
        res = fn(*[r[...] for r in rin], *[p[...] for p in pin])
        if not isinstance(res, (tuple, list)):
            res = (res,)
        for r, v in zip(rout, res[:nro]):
            r[...] = v.astype(r.dtype)
        if npo:
            @pl.when(pl.program_id(0) == 0)
            def _():
                for p in pout:
                    p[...] = jnp.zeros(p.shape, p.dtype)
            for p, v in zip(pout, res[nro:]):
                p[...] += v

    in_specs = [pl.BlockSpec((tm, w), functools.partial(lambda cb, i: (i, cb), cb)) for (_, w, cb) in rows]
    in_specs += [pl.BlockSpec(p.shape, lambda i: (0, 0)) for p in pars]
    out_specs = [pl.BlockSpec((tm, w), lambda i: (i, 0)) for (w, _) in row_outs]
    out_specs += [pl.BlockSpec(s, lambda i: (0, 0)) for s in par_outs]
    out_shape = [jax.ShapeDtypeStruct((m, w), dt) for (w, dt) in row_outs]
    out_shape += [jax.ShapeDtypeStruct(s, F32) for s in par_outs]
    res = pl.pallas_call(
        body, name=name, grid=(m // tm,), in_specs=in_specs, out_specs=out_specs, out_shape=out_shape,
        compiler_params=_params(("arbitrary",) if npo else ("parallel",)),
    )(*[a for (a, _, _) in rows], *pars)
    return res


def _f32(*vals):
    return [v.astype(F32) for v in vals]


def _f_rms(x, g):
    r = lax.rsqrt(jnp.mean(x * x, axis=-1, keepdims=True) + EPS)
    return x * r * g


def _rms_fwd(x, g, name):
    return _rows_call(lambda xv, gv: _f_rms(xv, gv), [(x, D_MODEL, 0)], [g], [(D_MODEL, BF16)], [],
                      tm=256, name=name)[0]


def _rms_bwd(x, dh, dres, g, name):
    def fn(xv, dhv, drv, gv):
        _, vjp = jax.vjp(_f_rms, xv, gv)
        dx, dg = vjp(dhv)
        dx = dx + drv
        return dx, dx, dg
    return _rows_call(fn, [(x, D_MODEL, 0), (dh, D_MODEL, 0), (dres, D_MODEL, 0)], [g],
                      [(D_MODEL, F32), (D_MODEL, BF16)], [(1, D_MODEL)], tm=256, name=name)


def _rot_heads(xv, cos, sin, scale):
    half = QK_DIM // 2
    outs = []
    for h in range(HEADS):
        x1 = xv[:, h * QK_DIM:h * QK_DIM + half]
        x2 = xv[:, h * QK_DIM + half:(h + 1) * QK_DIM]
        outs += [(x1 * cos - x2 * sin) * scale, (x1 * sin + x2 * cos) * scale]
    return jnp.concatenate(outs, axis=1)


def _rot_fwd(proj, cos, sin, name):
    def fn(q, k, cv, sv):
        return _rot_heads(q, cv, sv, 1.0), _rot_heads(k, cv, sv, QK_DIM ** -0.5)
    return _rows_call(fn, [(proj, QK_WIDTH, 0), (proj, QK_WIDTH, 1), (cos, LANES, 0), (sin, LANES, 0)], [],
                      [(QK_WIDTH, BF16), (QK_WIDTH, BF16)], [], tm=256, name=name)


def _rot_bwd(dqr, dkr, dv, cos, sin, name):
    def fn(dq, dk, dvv, cv, sv):
        return jnp.concatenate([_rot_heads(dq, cv, -sv, 1.0), _rot_heads(dk, cv, -sv, QK_DIM ** -0.5), dvv], axis=1)
    return _rows_call(fn, [(dqr, QK_WIDTH, 0), (dkr, QK_WIDTH, 0), (dv, D_MODEL, 0), (cos, LANES, 0), (sin, LANES, 0)],
                      [], [(2 * QK_WIDTH + D_MODEL, BF16)], [], tm=256, name=name)[0]


def _f_post1(y0, y1, y2, y3, g, gr, s5, u, dsk):
    yn = [yh * lax.rsqrt(jnp.mean(yh * yh, axis=-1, keepdims=True) + EPS) for yh in (y0, y1, y2, y3)]
    ret = jax.nn.sigmoid(gr) * (jax.nn.silu(g) * jnp.concatenate(yn, axis=1))
    ysg = jax.nn.gelu(s5 + dsk * u)
    return ret, ysg


def _post1_rows(y, proj, s5y):
    rows = [(y, V_DIM, h) for h in range(HEADS)]
    rows += [(proj, D_MODEL, 2), (proj, D_MODEL, 4), (s5y, D_MODEL, 0), (proj, D_MODEL, 3)]
    return rows


def _post1_fwd(y, proj, s5y, dsk, name):
    def fn(*vals):
        ret, ysg = _f_post1(*vals)
        return ret, ysg, ysg
    return _rows_call(fn, _post1_rows(y, proj, s5y), [dsk],
                      [(D_MODEL, F32), (D_MODEL, F32), (D_MODEL, BF16)], [], tm=128, name=name)


def _post1_bwd(y, proj, s5y, dret, dys, dsk, name):
    def fn(*vals):
        prim = vals[:8] + (vals[10],)
        _, vjp = jax.vjp(_f_post1, *prim)
        gy0, gy1, gy2, gy3, gg, ggr, gs5, gu, gd = vjp((vals[8], vals[9]))
        return jnp.concatenate([gy0, gy1, gy2, gy3], axis=1), gg, ggr, gs5, gu, gd
    rows = _post1_rows(y, proj, s5y) + [(dret, D_MODEL, 0), (dys, D_MODEL, 0)]
    return _rows_call(fn, rows, [dsk],
                      [(D_MODEL, BF16), (D_MODEL, BF16), (D_MODEL, BF16), (D_MODEL, F32), (D_MODEL, F32)],
                      [(1, D_MODEL)], tm=128, name=name)


def _f_merge(z, ysg, gs, ret, b):
    return ret + jax.nn.sigmoid(gs) * (ysg * jax.nn.sigmoid(z + b))


def _merge_fwd(z, ysg, proj, ret, b, name):
    return _rows_call(_f_merge, [(z, D_MODEL, 0), (ysg, D_MODEL, 0), (proj, D_MODEL, 5), (ret, D_MODEL, 0)], [b],
                      [(D_MODEL, BF16)], [], tm=128, name=name)[0]


def _merge_bwd(z, ysg, proj, ret, dm, b, name):
    def fn(zv, yv, gv, rv, dmv, bv):
        _, vjp = jax.vjp(_f_merge, zv, yv, gv, rv, bv)
        gz, gy, gg, _, gb = vjp(dmv)
        return gz, gy, gg, gb
    rows = [(z, D_MODEL, 0), (ysg, D_MODEL, 0), (proj, D_MODEL, 5), (ret, D_MODEL, 0), (dm, D_MODEL, 0)]
    return _rows_call(fn, rows, [b], [(D_MODEL, BF16), (D_MODEL, F32), (D_MODEL, BF16)], [(1, D_MODEL)],
                      tm=128, name=name)


def _f_glu(a, b):
    return jax.nn.silu(a) * b


def _glu_fwd(ab, name):
    return _rows_call(_f_glu, [(ab, D_FF, 0), (ab, D_FF, 1)], [], [(D_FF, BF16)], [], tm=128, name=name)[0]


def _glu_bwd(ab, df, name):
    def fn(a, b, d):
        _, vjp = jax.vjp(_f_glu, a, b)
        ga, gb = vjp(d)
        return jnp.concatenate([ga, gb], axis=1)
    return _rows_call(fn, [(ab, D_FF, 0), (ab, D_FF, 1), (df, D_FF, 0)], [], [(2 * D_FF, BF16)], [],
                      tm=128, name=name)[0]


def _loss_stage(x, tgt, g, name):
    def fn(xv, tv, gv):
        def lf(xx, gg):
            err = _f_rms(xx, gg) - tv
            row = jnp.mean(err * err, axis=-1, keepdims=True)
            return 0.5 * jnp.sum(row, axis=0, keepdims=True)
        l, vjp = jax.vjp(lf, xv, gv)
        dx, dg = vjp(jnp.ones((1, 1), F32))
        return dx, dx, jnp.broadcast_to(l, (1, LANES)), dg
    return _rows_call(fn, [(x, D_MODEL, 0), (tgt, D_MODEL, 0)], [g], [(D_MODEL, F32), (D_MODEL, BF16)],
                      [(1, LANES), (1, D_MODEL)], tm=256, name=name)


def _adam_math(wv, gv, mv, vv):
    mn = ADAM_B1 * mv + (1.0 - ADAM_B1) * gv
    vn = ADAM_B2 * vv + (1.0 - ADAM_B2) * (gv * gv)
    m_hat = mn / (1.0 - ADAM_B1 ** ADAM_STEP)
    v_hat = vn / (1.0 - ADAM_B2 ** ADAM_STEP)
    delta = -ADAM_LR * (m_hat / (jnp.sqrt(v_hat) + ADAM_EPS) + ADAM_WD * wv)
    return delta, mn, vn


def _adamw(w, g, m, v, name, after=None):
    rows, cols = w.shape
    tm = _tile(rows, 128 if cols > D_FF // N_CHIPS else (256 if cols > LANES else 512))
    fn = _adam_math if after is None else (lambda wv, gv, mv, vv, _: _adam_math(wv, gv, mv, vv))
    return _rows_call(fn, [(w, cols, 0), (g, cols, 0), (m, cols, 0), (v, cols, 0)], [] if after is None else [after],
                      [(cols, F32)] * 3, [], tm=tm, name=name)


def _adamw_nd(w, g, m, v, name):
    shape = w.shape
    lead = math.prod(shape[:-2])
    blk = (lead // 8,) + shape[-2:]
    three_d = lambda a: a.reshape((lead,) + shape[-2:])

    def body(w_ref, g_ref, m_ref, v_ref, d_ref, mn_ref, vn_ref):
        d_ref[...], mn_ref[...], vn_ref[...] = _adam_math(w_ref[...], g_ref[...], m_ref[...], v_ref[...])

    spec = pl.BlockSpec(blk, lambda i: (i, 0, 0))
    res = pl.pallas_call(
        body, name=name, grid=(8,), in_specs=[spec] * 4, out_specs=[spec] * 3,
        out_shape=[jax.ShapeDtypeStruct((lead,) + shape[-2:], F32)] * 3,
        compiler_params=_params(("parallel",)),
    )(three_d(w), three_d(g), three_d(m), three_d(v))
    return [r.reshape(shape) for r in res]


MATMUL_VMEM_BUDGET = 44 * 1024 * 1024


def _matmul_tiles(m, n, k, out_bytes, has_add):
    if k > 2048:
        return _tile(m, 1024), _tile(n, 1024), _tile(k, 1024)
    tm, tn, tk = _tile(m, 2048), _tile(n, 1024), k

    def footprint():
        acc = 4 * tm * tn if k // tk > 1 else 0
        return 2 * 2 * (tm * tk + tk * tn) + 2 * (out_bytes + 4 * has_add) * tm * tn + acc

    while footprint() > MATMUL_VMEM_BUDGET:
        if tn > 512 and n % (tn // 2) == 0:
            tn //= 2
        elif tk > 512 and k % (tk // 2) == 0:
            tk //= 2
        else:
            tm //= 2
    return tm, tn, tk


def _matmul(a, b, mode, out_dtypes, *, name, add=None):
    if mode == "nn":
        (m, k), (_, n) = a.shape, b.shape
    elif mode == "nt":
        (m, k), (n, _) = a.shape, b.shape
    else:
        (k, m), (_, n) = a.shape, b.shape
    n_out = len(out_dtypes)
    has_add = add is not None
    tm, tn, tk = _matmul_tiles(m, n, k, sum(jnp.dtype(dt).itemsize for dt in out_dtypes), has_add)
    nk = k // tk
    if mode == "nn":
        a_spec = pl.BlockSpec((tm, tk), lambda i, j, kk: (i, kk))
        b_spec = pl.BlockSpec((tk, tn), lambda i, j, kk: (kk, j))
        dims = (((1,), (0,)), ((), ()))
    elif mode == "nt":
        a_spec = pl.BlockSpec((tm, tk), lambda i, j, kk: (i, kk))
        b_spec = pl.BlockSpec((tn, tk), lambda i, j, kk: (j, kk))
        dims = (((1,), (1,)), ((), ()))
    else:
        a_spec = pl.BlockSpec((tk, tm), lambda i, j, kk: (kk, i))
        b_spec = pl.BlockSpec((tk, tn), lambda i, j, kk: (kk, j))
        dims = (((0,), (0,)), ((), ()))

    def body(*refs):
        a_ref, b_ref = refs[0], refs[1]
        add_ref = refs[2] if has_add else None
        outs = refs[2 + has_add:2 + has_add + n_out]

        def finish(r):
            if has_add:
                r = r + add_ref[...]
            for o in outs:
                o[...] = r.astype(o.dtype)

        if nk == 1:
            finish(lax.dot_general(a_ref[...], b_ref[...], dims, preferred_element_type=F32))
            return
        acc = refs[-1]
        kk = pl.program_id(2)

        @pl.when(kk == 0)
        def _():
            acc[...] = jnp.zeros(acc.shape, F32)

        acc[...] += lax.dot_general(a_ref[...], b_ref[...], dims, preferred_element_type=F32)

        @pl.when(kk == nk - 1)
        def _():
            finish(acc[...])

    in_specs = [a_spec, b_spec]
    args = [a, b]
    if has_add:
        in_specs.append(pl.BlockSpec((tm, tn), lambda i, j, kk: (i, j)))
        args.append(add)
    return pl.pallas_call(
        body, name=name, grid=(m // tm, n // tn, nk), in_specs=in_specs,
        out_specs=[pl.BlockSpec((tm, tn), lambda i, j, kk: (i, j))] * n_out,
        out_shape=[jax.ShapeDtypeStruct((m, n), dt) for dt in out_dtypes],
        scratch_shapes=[pltpu.VMEM((tm, tn), F32)] if nk > 1 else [],
        compiler_params=_params(("parallel", "parallel", "arbitrary")),
    )(*args)


RET_TQ = 512


def _decay(lg_ref, i, tq, seq):
    n_idx = i * tq + lax.broadcasted_iota(jnp.int32, (tq, seq), 0)
    m_idx = lax.broadcasted_iota(jnp.int32, (tq, seq), 1)
    diff = (n_idx - m_idx).astype(F32)
    lgf = lg_ref[0, 0:1, 0:1]
    lgb = lg_ref[0, 1:2, 0:1]
    causal = diff >= 0
    return jnp.exp(jnp.where(causal, lgf * diff, -lgb * diff)), diff, causal


_NT = (((1,), (1,)), ((), ()))
_TN = (((0,), (0,)), ((), ()))


def _ret_fwd(qr, kr, proj, lg, name):
    seq = qr.shape[0]
    tq = RET_TQ
    v_blk0 = (2 * QK_WIDTH) // V_DIM

    def body(q_ref, k_ref, v_ref, lg_ref, y_ref):
        i = pl.program_id(1)
        s = lax.dot_general(q_ref[...], k_ref[...], _NT, preferred_element_type=F32)
        dm, _, _ = _decay(lg_ref, i, tq, seq)
        p = (s * dm).astype(BF16)
        y_ref[...] = jnp.dot(p, v_ref[...].astype(BF16), preferred_element_type=F32)

    return pl.pallas_call(
        body, name=name, grid=(HEADS, seq // tq),
        in_specs=[pl.BlockSpec((tq, QK_DIM), lambda h, i: (i, h)),
                  pl.BlockSpec((seq, QK_DIM), lambda h, i: (0, h)),
                  pl.BlockSpec((seq, V_DIM), lambda h, i: (0, v_blk0 + h)),
                  pl.BlockSpec((1, 2, LANES), lambda h, i: (h, 0, 0))],
        out_specs=pl.BlockSpec((tq, V_DIM), lambda h, i: (i, h)),
        out_shape=jax.ShapeDtypeStruct((seq, HEADS * V_DIM), F32),
        compiler_params=_params(("parallel", "parallel")),
    )(qr, kr, proj, lg)


def _ret_bwd(qr, kr, proj, dy, lg, name):
    seq = qr.shape[0]
    tq = RET_TQ
    v_blk0 = (2 * QK_WIDTH) // V_DIM

    def body(q_ref, k_ref, v_ref, dy_ref, lg_ref, dq_ref, dk_ref, dv_ref, dlg_ref):
        i = pl.program_id(1)

        @pl.when(i == 0)
        def _():
            dk_ref[...] = jnp.zeros(dk_ref.shape, F32)
            dv_ref[...] = jnp.zeros(dv_ref.shape, F32)
            dlg_ref[...] = jnp.zeros(dlg_ref.shape, F32)

        q = q_ref[...]
        k = k_ref[...]
        vb = v_ref[...].astype(BF16)
        dyb = dy_ref[...]
        s = lax.dot_general(q, k, _NT, preferred_element_type=F32)
        dm, diff, causal = _decay(lg_ref, i, tq, seq)
        p = s * dm
        dp = lax.dot_general(dyb, vb, _NT, preferred_element_type=F32)
        dv_ref[...] += lax.dot_general(p.astype(BF16), dyb, _TN, preferred_element_type=F32)
        ds = (dp * dm).astype(BF16)
        dq_ref[...] = jnp.dot(ds, k, preferred_element_type=F32)
        dk_ref[...] += lax.dot_general(ds, q, _TN, preferred_element_type=F32)
        gd = dp * p * diff
        dlf = jnp.sum(jnp.sum(jnp.where(causal, gd, 0.0), axis=1, keepdims=True), axis=0, keepdims=True)
        dlb = jnp.sum(jnp.sum(jnp.where(causal, 0.0, -gd), axis=1, keepdims=True), axis=0, keepdims=True)
        row = lax.broadcasted_iota(jnp.int32, (2, LANES), 0)
        dlg_ref[0] += jnp.where(row == 0, dlf, dlb)

    return pl.pallas_call(
        body, name=name, grid=(HEADS, seq // tq),
        in_specs=[pl.BlockSpec((tq, QK_DIM), lambda h, i: (i, h)),
                  pl.BlockSpec((seq, QK_DIM), lambda h, i: (0, h)),
                  pl.BlockSpec((seq, V_DIM), lambda h, i: (0, v_blk0 + h)),
                  pl.BlockSpec((tq, V_DIM), lambda h, i: (i, h)),
                  pl.BlockSpec((1, 2, LANES), lambda h, i: (h, 0, 0))],
        out_specs=[pl.BlockSpec((tq, QK_DIM), lambda h, i: (i, h)),
                   pl.BlockSpec((seq, QK_DIM), lambda h, i: (0, h)),
                   pl.BlockSpec((seq, V_DIM), lambda h, i: (0, h)),
                   pl.BlockSpec((1, 2, LANES), lambda h, i: (h, 0, 0))],
        out_shape=[jax.ShapeDtypeStruct((seq, QK_WIDTH), F32), jax.ShapeDtypeStruct((seq, QK_WIDTH), F32),
                   jax.ShapeDtypeStruct((seq, HEADS * V_DIM), F32), jax.ShapeDtypeStruct((HEADS, 2, LANES), F32)],
        compiler_params=_params(("parallel", "arbitrary")),
    )(qr, kr, proj, dy, lg)


def _shift_rows(v, reverse):
    row = lax.broadcasted_iota(jnp.int32, v.shape, 0)
    if reverse:
        return jnp.where(row == SEGMENTS - 1, 0.0, pltpu.roll(v, SEGMENTS - 1, 0))
    return jnp.where(row == 0, 0.0, pltpu.roll(v, 1, 0))


def _slab(t):
    if isinstance(t, int):
        return pl.ds(t * SEGMENTS, SEGMENTS)
    return pl.ds(pl.multiple_of(t * SEGMENTS, SEGMENTS), SEGMENTS)


def _unrolled_loop(body, lo, hi, init):
    main = (hi - lo) // SCAN_UNROLL

    def unrolled(g, carry):
        for k in range(SCAN_UNROLL):
            carry = body(lo + g * SCAN_UNROLL + k, carry)
        return carry

    carry = lax.fori_loop(0, main, unrolled, init)
    for t in range(lo + main * SCAN_UNROLL, hi):
        carry = body(t, carry)
    return carry


def _scan(xr_ref, xi_ref, lam, reverse, conj):
    steps = xr_ref.shape[0] // SEGMENTS
    cols = xr_ref.shape[1]
    lr = jnp.broadcast_to(lam[0], (SEGMENTS, cols))
    li = jnp.broadcast_to(lam[1], (SEGMENTS, cols))
    lrt = jnp.broadcast_to(lam[2], (SEGMENTS, cols))
    lit = jnp.broadcast_to(lam[3], (SEGMENTS, cols))
    if conj:
        li, lit = -li, -lit
    zero = jnp.zeros((SEGMENTS, cols), F32)

    def rows_of(t):
        return _slab(steps - 1 - t if reverse else t)

    def advance(t, carry):
        sr, si = carry
        rows = rows_of(t)
        return lr * sr - li * si + xr_ref[rows, :], lr * si + li * sr + xi_ref[rows, :]

    def step(t, carry):
        nr, ni = advance(t, carry)
        rows = rows_of(t)
        xr_ref[rows, :] = nr
        xi_ref[rows, :] = ni
        return nr, ni

    def run(body, init):
        return _unrolled_loop(body, 0, steps, init)

    er, ei = run(advance, (zero, zero))
    cr, ci = zero, zero
    for _ in range(SEGMENTS - 1):
        tr = er + lrt * cr - lit * ci
        ti = ei + lrt * ci + lit * cr
        cr, ci = _shift_rows(tr, reverse), _shift_rows(ti, reverse)
    run(step, (cr, ci))


def _permute_in(dst_ref, src_ref):
    steps = src_ref.shape[0] // SEGMENTS
    for s in range(SEGMENTS):
        dst_ref[pl.ds(s, steps, stride=SEGMENTS), :] = src_ref[s * steps:(s + 1) * steps, :].astype(dst_ref.dtype)


def _unpermute(src_ref, s):
    steps = src_ref.shape[0] // SEGMENTS
    return src_ref[pl.ds(s, steps, stride=SEGMENTS), :]


def _s5_fwd(proj, bblk, cblk, lam, name):
    seq = proj.shape[0]
    u_blk0 = (2 * QK_WIDTH + 2 * D_MODEL) // LANES
    sc = STATE_COLS

    def body(u_ref, b_ref, c_ref, lam_ref, y_ref, up_ref, yp_ref, xr_ref, xi_ref):
        _permute_in(up_ref, u_ref)
        ub = up_ref[...].astype(BF16)
        for d in range(2):
            xr_ref[...] = jnp.dot(ub, b_ref[d, :, 0:sc], preferred_element_type=F32)
            xi_ref[...] = jnp.dot(ub, b_ref[d, :, sc:2 * sc], preferred_element_type=F32)
            lm = [lam_ref[d, r:r + 1, :] for r in range(4)]
            _scan(xr_ref, xi_ref, lm, reverse=(d == 1), conj=False)
            yd = (jnp.dot(xr_ref[...].astype(BF16), c_ref[d, 0:sc, :], preferred_element_type=F32)
                  + jnp.dot(xi_ref[...].astype(BF16), c_ref[d, sc:2 * sc, :], preferred_element_type=F32))
            if d == 0:
                yp_ref[...] = yd
            else:
                yp_ref[...] += yd
        steps = seq // SEGMENTS
        for s in range(SEGMENTS):
            y_ref[s * steps:(s + 1) * steps, :] = _unpermute(yp_ref, s)

    return pl.pallas_call(
        body, name=name, grid=(N_TILES,),
        in_specs=[pl.BlockSpec((seq, LANES), lambda j: (0, u_blk0 + j)),
                  pl.BlockSpec((2, None, LANES, 2 * sc), lambda j: (0, j, 0, 0)),
                  pl.BlockSpec((2, None, 2 * sc, LANES), lambda j: (0, j, 0, 0)),
                  pl.BlockSpec((2, None, 4, sc), lambda j: (0, j, 0, 0))],
        out_specs=pl.BlockSpec((seq, LANES), lambda j: (0, j)),
        out_shape=jax.ShapeDtypeStruct((seq, D_MODEL), F32),
        scratch_shapes=[pltpu.VMEM((seq, LANES), F32), pltpu.VMEM((seq, LANES), F32),
                        pltpu.VMEM((seq, sc), F32), pltpu.VMEM((seq, sc), F32)],
        compiler_params=_params(("parallel",)),
    )(proj, bblk, cblk, lam)


def _s5_bwd(proj, dy, du_part, bblk, cblk, lam, name):
    seq = proj.shape[0]
    u_blk0 = (2 * QK_WIDTH + 2 * D_MODEL) // LANES
    sc = STATE_COLS
    steps = seq // SEGMENTS

    def body(u_ref, dy_ref, dup_ref, b_ref, c_ref, lam_ref, du_ref, db_ref, dc_ref, dlam_ref,
             up_ref, dyp_ref, dua_ref, xr_ref, xi_ref, gr_ref, gi_ref):
        _permute_in(up_ref, u_ref)
        _permute_in(dyp_ref, dy_ref)
        ub = up_ref[...].astype(BF16)
        dyb = dyp_ref[...].astype(BF16)
        ubt = up_ref[...].T.astype(BF16)
        dybt = dyp_ref[...].T.astype(BF16)
        for d in range(2):
            reverse = d == 1
            xr_ref[...] = jnp.dot(ub, b_ref[d, :, 0:sc], preferred_element_type=F32)
            xi_ref[...] = jnp.dot(ub, b_ref[d, :, sc:2 * sc], preferred_element_type=F32)
            lm = [lam_ref[d, r:r + 1, :] for r in range(4)]
            _scan(xr_ref, xi_ref, lm, reverse=reverse, conj=False)
            xrb = xr_ref[...].astype(BF16)
            xib = xi_ref[...].astype(BF16)
            dc_ref[d, :, 0:sc] = jnp.dot(dybt, xrb, preferred_element_type=F32)
            dc_ref[d, :, sc:2 * sc] = jnp.dot(dybt, xib, preferred_element_type=F32)
            gr_ref[...] = lax.dot_general(dyb, c_ref[d, 0:sc, :], _NT, preferred_element_type=F32)
            gi_ref[...] = lax.dot_general(dyb, c_ref[d, sc:2 * sc, :], _NT, preferred_element_type=F32)
            _scan(gr_ref, gi_ref, lm, reverse=not reverse, conj=True)

            def acc_step(t, carry):
                ar, ai = carry
                prev = _slab(t + 1 if reverse else t - 1)
                pr = xr_ref[prev, :]
                pi = xi_ref[prev, :]
                zr = gr_ref[_slab(t), :]
                zi = gi_ref[_slab(t), :]
                return ar + zr * pr + zi * pi, ai + zi * pr - zr * pi

            zero = jnp.zeros((SEGMENTS, sc), F32)
            if reverse:
                ar, ai = _unrolled_loop(acc_step, 0, steps - 1, (zero, zero))
                edge = _slab(steps - 1)
                pr = _shift_rows(xr_ref[_slab(0), :], True)
                pi = _shift_rows(xi_ref[_slab(0), :], True)
            else:
                ar, ai = _unrolled_loop(acc_step, 1, steps, (zero, zero))
                edge = _slab(0)
                pr = _shift_rows(xr_ref[_slab(steps - 1), :], False)
                pi = _shift_rows(xi_ref[_slab(steps - 1), :], False)
            zr = gr_ref[edge, :]
            zi = gi_ref[edge, :]
            ar = ar + zr * pr + zi * pi
            ai = ai + zi * pr - zr * pi
            dlam_ref[d, 0:1, :] = jnp.sum(ar, axis=0, keepdims=True)
            dlam_ref[d, 1:2, :] = jnp.sum(ai, axis=0, keepdims=True)

            grb = gr_ref[...].astype(BF16)
            gib = gi_ref[...].astype(BF16)
            db_ref[d, :, 0:sc] = jnp.dot(ubt, grb, preferred_element_type=F32)
            db_ref[d, :, sc:2 * sc] = jnp.dot(ubt, gib, preferred_element_type=F32)
            dud = (lax.dot_general(grb, b_ref[d, :, 0:sc], _NT, preferred_element_type=F32)
                   + lax.dot_general(gib, b_ref[d, :, sc:2 * sc], _NT, preferred_element_type=F32))
            if d == 0:
                dua_ref[...] = dud
            else:
                dua_ref[...] += dud
        for s in range(SEGMENTS):
            rows = slice(s * steps, (s + 1) * steps)
            du_ref[rows, :] = (_unpermute(dua_ref, s) + dup_ref[rows, :]).astype(du_ref.dtype)

    return pl.pallas_call(
        body, name=name, grid=(N_TILES,),
        in_specs=[pl.BlockSpec((seq, LANES), lambda j: (0, u_blk0 + j)),
                  pl.BlockSpec((seq, LANES), lambda j: (0, j)),
                  pl.BlockSpec((seq, LANES), lambda j: (0, j)),
                  pl.BlockSpec((2, None, LANES, 2 * sc), lambda j: (0, j, 0, 0)),
                  pl.BlockSpec((2, None, 2 * sc, LANES), lambda j: (0, j, 0, 0)),
                  pl.BlockSpec((2, None, 4, sc), lambda j: (0, j, 0, 0))],
        out_specs=[pl.BlockSpec((seq, LANES), lambda j: (0, j)),
                   pl.BlockSpec((2, None, LANES, 2 * sc), lambda j: (0, j, 0, 0)),
                   pl.BlockSpec((2, None, LANES, 2 * sc), lambda j: (0, j, 0, 0)),
                   pl.BlockSpec((2, None, 2, sc), lambda j: (0, j, 0, 0))],
        out_shape=[jax.ShapeDtypeStruct((seq, D_MODEL), BF16),
                   jax.ShapeDtypeStruct((2, N_TILES, LANES, 2 * sc), F32),
                   jax.ShapeDtypeStruct((2, N_TILES, LANES, 2 * sc), F32),
                   jax.ShapeDtypeStruct((2, N_TILES, 2, sc), F32)],
        scratch_shapes=[pltpu.VMEM((seq, LANES), F32), pltpu.VMEM((seq, LANES), F32), pltpu.VMEM((seq, LANES), F32),
                        pltpu.VMEM((seq, sc), F32), pltpu.VMEM((seq, sc), F32),
                        pltpu.VMEM((seq, sc), F32), pltpu.VMEM((seq, sc), F32)],
        compiler_params=_params(("parallel",)),
    )(proj, dy, du_part, bblk, cblk, lam)


def _s5_discretize(a_re, a_im, log_dt, b_re, b_im, seg_len):
    dt = jnp.exp(log_dt)[..., None]
    e = jnp.exp(a_re * dt)
    lr, li = e * jnp.cos(a_im * dt), e * jnp.sin(a_im * dt)
    et = jnp.exp(a_re * dt * seg_len)
    lrt, lit = et * jnp.cos(a_im * dt * seg_len), et * jnp.sin(a_im * dt * seg_len)
    den = a_re * a_re + a_im * a_im
    qr = ((lr - 1.0) * a_re + li * a_im) / den
    qi = (li * a_re - (lr - 1.0) * a_im) / den
    br = qr[..., None] * b_re - qi[..., None] * b_im
    bi = qr[..., None] * b_im + qi[..., None] * b_re
    return lr, li, lrt, lit, br, bi


def _s5_pack(lr, li, lrt, lit, br, bi, c_re, c_im):
    eye = jnp.eye(GROUPS_PER_TILE, dtype=F32)

    def bd_b(b):
        b5 = b.reshape(2, N_TILES, GROUPS_PER_TILE, N_STATE, GROUP)
        return jnp.einsum("dtgph,gk->dtghkp", b5, eye).reshape(2, N_TILES, LANES, STATE_COLS)

    def bd_c(c):
        c5 = c.reshape(2, N_TILES, GROUPS_PER_TILE, GROUP, N_STATE)
        return jnp.einsum("dtghp,gk->dtkpgh", c5, eye).reshape(2, N_TILES, STATE_COLS, LANES)

    bblk = jnp.concatenate([bd_b(br), bd_b(bi)], axis=3)
    cblk = jnp.concatenate([bd_c(c_re), -bd_c(c_im)], axis=2)
    lam = jnp.stack([v.reshape(2, N_TILES, STATE_COLS) for v in (lr, li, lrt, lit)], axis=2)
    return bblk, cblk, lam


def _s5_unpack(dbblk, dcblk, dlam):
    eye = jnp.eye(GROUPS_PER_TILE, dtype=F32)

    def diag_b(d):
        d6 = d.reshape(2, N_TILES, GROUPS_PER_TILE, GROUP, GROUPS_PER_TILE, N_STATE)
        return jnp.einsum("dtghkp,gk->dtgph", d6, eye).reshape(2, N_GROUPS, N_STATE, GROUP)

    def diag_c(d):
        d6 = d.reshape(2, N_TILES, GROUPS_PER_TILE, GROUP, GROUPS_PER_TILE, N_STATE)
        return jnp.einsum("dtghkp,gk->dtghp", d6, eye).reshape(2, N_GROUPS, GROUP, N_STATE)

    dbr, dbi = diag_b(dbblk[..., :STATE_COLS]), diag_b(dbblk[..., STATE_COLS:])
    dcr, dci = diag_c(dcblk[..., :STATE_COLS]), -diag_c(dcblk[..., STATE_COLS:])
    dlr = dlam[:, :, 0, :].reshape(2, N_GROUPS, N_STATE)
    dli = dlam[:, :, 1, :].reshape(2, N_GROUPS, N_STATE)
    return dlr, dli, dbr, dbi, dcr, dci


def _pos():
    return lax.axis_index("x"), lax.axis_index("y"), lax.axis_index("c")


def _remote(src, dst, ssem, rsem, dev):
    return pltpu.make_async_remote_copy(src_ref=src, dst_ref=dst, send_sem=ssem, recv_sem=rsem,
                                        device_id=dev, device_id_type=MESH)


_PIECES = (
    ("w_in", "in", D_MODEL, IN_WIDTH // N_CHIPS, 0, IN_WIDTH // N_CHIPS, 0),
    ("w_glu", "glu", D_MODEL // N_CHIPS, D_MODEL, D_MODEL // N_CHIPS, 0, 0),
    ("w_out", "out", D_MODEL // N_CHIPS, D_MODEL, D_MODEL // N_CHIPS, 0, 0),
    ("w_ffn_gate", "gu", D_MODEL, D_FF // N_CHIPS, 0, D_FF // N_CHIPS, 0),
    ("w_ffn_up", "gu", D_MODEL, D_FF // N_CHIPS, 0, D_FF // N_CHIPS, D_FF),
    ("w_ffn_down", "down", D_FF // N_CHIPS, D_MODEL, D_FF // N_CHIPS, 0, 0),
)
_BUFFERS = (("in", D_MODEL, IN_WIDTH), ("glu", D_MODEL, D_MODEL), ("out", D_MODEL, D_MODEL),
            ("gu", D_MODEL, 2 * D_FF), ("down", D_FF, D_MODEL))
_BUF_INDEX = {name: t for t, (name, _, _) in enumerate(_BUFFERS)}
N_PIECES = len(_PIECES)
N_BUFFERS = len(_BUFFERS)


def _own_block(piece, tm):
    _, _, _, cs, rstep, cstep, coff = piece
    return lambda i, chip: (i + chip * (rstep // tm), coff // cs + chip * (cstep // cs))


def _cast_place(piece, w3, layer, prev, chip_arr, name):
    _, r, cc = w3.shape
    _, rf, cf = _BUFFERS[_BUF_INDEX[piece[1]]]
    tm = _tile(r, 256)
    own = _own_block(piece, tm)

    def body(s_ref, w_ref, *rest):
        rest[-1][...] = w_ref[...].astype(BF16)

    in_specs = [pl.BlockSpec((None, tm, cc), lambda i, s: (layer, i, 0))]
    args = [w3]
    aliases = {}
    if prev is not None:
        in_specs.append(pl.BlockSpec(memory_space=pl.ANY))
        args.append(prev)
        aliases = {2: 0}
    return pl.pallas_call(
        body, name=name,
        grid_spec=pltpu.PrefetchScalarGridSpec(
            num_scalar_prefetch=1, grid=(r // tm,), in_specs=in_specs,
            out_specs=pl.BlockSpec((tm, cc), lambda i, s: own(i, s[0]))),
        out_shape=jax.ShapeDtypeStruct((rf, cf), BF16), input_output_aliases=aliases,
        compiler_params=_params(("parallel",)),
    )(chip_arr, *args)


_GATHER_GROUPS = ((0, (0,)), (0, (1, 2, 3, 4, 5)), (1, (0,)), (1, (1, 2, 3, 4, 5)))
_SPLIT_EFFECT = pltpu.SideEffectType.DATAFLOW_SIDE_EFFECTING
SEM_SPEC = pl.BlockSpec(memory_space=pltpu.SEMAPHORE)
BF16_ROWS = 2 * SUBLANES


def _group_keys(g):
    layer, pieces = _GATHER_GROUPS[g]
    keys = []
    for p in pieces:
        if (_PIECES[p][1], layer) not in keys:
            keys.append((_PIECES[p][1], layer))
    return keys


def _half_view(ref, piece, j, c):
    _, _, rs, cs, rstep, cstep, coff = piece
    half = rs // 2
    return ref.at[pl.ds(pl.multiple_of(j * rstep + c * half, BF16_ROWS), half), pl.ds(coff + j * cstep, cs)]


def _for_my_chip(fn):
    x, y, _ = _pos()
    for mine in range(N_CHIPS):
        pl.when(2 * x + y == mine)(functools.partial(fn, mine, [j for j in range(N_CHIPS) if j != mine]))


def _gather_start(groups, placed):
    keys = [k for g in groups for k in _group_keys(g)]
    nb, ng = len(keys), len(groups)

    def body(*refs):
        bufs = dict(zip(keys, refs[nb:2 * nb]))
        ssems = refs[2 * nb:2 * nb + ng]
        rsems = refs[2 * nb + ng:2 * nb + 2 * ng]
        token = refs[2 * nb + 2 * ng]
        _, _, c = _pos()

        def send(mine, others):
            for t, g in enumerate(groups):
                layer, pieces = _GATHER_GROUPS[g]
                for k, p in enumerate(pieces):
                    view = _half_view(bufs[(_PIECES[p][1], layer)], _PIECES[p], mine, c)
                    for j in others:
                        _remote(view, view, ssems[t].at[k * N_CHIPS + j], rsems[t].at[k * N_CHIPS + mine],
                                (j // 2, j % 2, c)).start()

        _for_my_chip(send)
        token[...] = jnp.zeros(token.shape, token.dtype)

    sems = [pltpu.SemaphoreType.DMA((N_CHIPS * len(_GATHER_GROUPS[g][1]),)) for g in groups]
    shapes = [jax.ShapeDtypeStruct(placed[k].shape, placed[k].dtype) for k in keys]
    res = pl.pallas_call(
        body, name="gather_start_g%d" % groups[0],
        in_specs=[HBM_SPEC] * nb,
        out_specs=[HBM_SPEC] * nb + [SEM_SPEC] * (2 * ng) + [pl.BlockSpec(memory_space=pltpu.VMEM)],
        out_shape=shapes + sems + sems + [jax.ShapeDtypeStruct((SUBLANES, LANES), F32)],
        input_output_aliases={t: t for t in range(nb)},
        compiler_params=_params(has_side_effects=_SPLIT_EFFECT),
    )(*[pltpu.with_memory_space_constraint(placed[k], pltpu.HBM) for k in keys])
    return (dict(zip(keys, res[:nb])), dict(zip(groups, res[nb:nb + ng])),
            dict(zip(groups, res[nb + ng:nb + 2 * ng])), res[nb + 2 * ng])


def _gather_wait(g, bufs, ssem, rsem, after):
    layer, pieces = _GATHER_GROUPS[g]
    keys = _group_keys(g)
    nb = len(keys)

    def body(*refs):
        ssem_ref, rsem_ref = refs[nb], refs[nb + 1]
        land = dict(zip(keys, refs[nb + 3:]))
        _, _, c = _pos()

        def wait(mine, others):
            for k, p in enumerate(pieces):
                ref = land[(_PIECES[p][1], layer)]
                for j in others:
                    cp = _remote(_half_view(ref, _PIECES[p], mine, c), _half_view(ref, _PIECES[p], j, c),
                                 ssem_ref.at[k * N_CHIPS + j], rsem_ref.at[k * N_CHIPS + j], (j // 2, j % 2, c))
                    cp.wait_send()
                    cp.wait_recv()

        _for_my_chip(wait)

    return pl.pallas_call(
        body, name="gather_wait_g%d" % g,
        in_specs=[HBM_SPEC] * nb + [SEM_SPEC, SEM_SPEC, pl.BlockSpec(memory_space=pl.ANY)],
        out_specs=[HBM_SPEC] * nb,
        out_shape=[jax.ShapeDtypeStruct(a.shape, a.dtype) for a in bufs],
        input_output_aliases={t: t for t in range(nb)},
        compiler_params=_params(has_side_effects=_SPLIT_EFFECT),
    )(*bufs, ssem, rsem, after)


def _gather_forward(g, bufs):
    layer, pieces = _GATHER_GROUPS[g]
    keys = _group_keys(g)
    nb = len(keys)

    def body(*refs):
        land = dict(zip(keys, refs[nb:2 * nb]))
        ssem, rsem = refs[2 * nb:]
        x, y, c = _pos()

        def forward(mine, others):
            cps = []
            for k, p in enumerate(pieces):
                ref = land[(_PIECES[p][1], layer)]
                for j in others:
                    view = _half_view(ref, _PIECES[p], j, c)
                    cp = _remote(view, view, ssem.at[k * N_CHIPS + j], rsem.at[k * N_CHIPS + j], (x, y, 1 - c))
                    cp.start()
                    cps.append(cp)
            for k, p in enumerate(pieces):
                ref = land[(_PIECES[p][1], layer)]
                for j in others:
                    view = _half_view(ref, _PIECES[p], j, 1 - c)
                    _remote(view, view, ssem.at[k * N_CHIPS + j], rsem.at[k * N_CHIPS + j], (x, y, 1 - c)).wait_recv()
            for cp in cps:
                cp.wait_send()

        _for_my_chip(forward)

    nsem = N_CHIPS * len(pieces)
    return pl.pallas_call(
        body, name="gather_forward_g%d" % g,
        in_specs=[HBM_SPEC] * nb, out_specs=[HBM_SPEC] * nb,
        out_shape=[jax.ShapeDtypeStruct(a.shape, a.dtype) for a in bufs],
        input_output_aliases={t: t for t in range(nb)},
        scratch_shapes=[pltpu.SemaphoreType.DMA((nsem,)), pltpu.SemaphoreType.DMA((nsem,))],
        compiler_params=_params(has_side_effects=True),
    )(*bufs)


def _forward_start(g, bufs):
    layer, pieces = _GATHER_GROUPS[g]
    keys = _group_keys(g)
    nb = len(keys)

    def body(*refs):
        land = dict(zip(keys, refs[nb:2 * nb]))
        ssem, rsem, token = refs[2 * nb:]
        x, y, c = _pos()

        def forward(mine, others):
            for k, p in enumerate(pieces):
                for j in others:
                    view = _half_view(land[(_PIECES[p][1], layer)], _PIECES[p], j, c)
                    _remote(view, view, ssem.at[k * N_CHIPS + j], rsem.at[k * N_CHIPS + j], (x, y, 1 - c)).start()

        _for_my_chip(forward)
        token[...] = jnp.zeros(token.shape, token.dtype)

    sem = pltpu.SemaphoreType.DMA((N_CHIPS * len(pieces),))
    res = pl.pallas_call(
        body, name="forward_start_g%d" % g,
        in_specs=[HBM_SPEC] * nb,
        out_specs=[HBM_SPEC] * nb + [SEM_SPEC, SEM_SPEC, pl.BlockSpec(memory_space=pltpu.VMEM)],
        out_shape=[jax.ShapeDtypeStruct(a.shape, a.dtype) for a in bufs]
        + [sem, sem, jax.ShapeDtypeStruct((SUBLANES, LANES), F32)],
        input_output_aliases={t: t for t in range(nb)},
        compiler_params=_params(has_side_effects=_SPLIT_EFFECT),
    )(*bufs)
    return list(res[:nb]), res[nb], res[nb + 1], res[nb + 2]


def _forward_wait(g, bufs, ssem, rsem, after):
    layer, pieces = _GATHER_GROUPS[g]
    keys = _group_keys(g)
    nb = len(keys)

    def body(*refs):
        ssem_ref, rsem_ref = refs[nb], refs[nb + 1]
        land = dict(zip(keys, refs[nb + 3:]))
        x, y, c = _pos()

        def wait(mine, others):
            for k, p in enumerate(pieces):
                ref = land[(_PIECES[p][1], layer)]
                for j in others:
                    cp = _remote(_half_view(ref, _PIECES[p], j, c), _half_view(ref, _PIECES[p], j, 1 - c),
                                 ssem_ref.at[k * N_CHIPS + j], rsem_ref.at[k * N_CHIPS + j], (x, y, 1 - c))
                    cp.wait_send()
                    cp.wait_recv()

        _for_my_chip(wait)

    return pl.pallas_call(
        body, name="forward_wait_g%d" % g,
        in_specs=[HBM_SPEC] * nb + [SEM_SPEC, SEM_SPEC, pl.BlockSpec(memory_space=pl.ANY)],
        out_specs=[HBM_SPEC] * nb,
        out_shape=[jax.ShapeDtypeStruct(a.shape, a.dtype) for a in bufs],
        input_output_aliases={t: t for t in range(nb)},
        compiler_params=_params(has_side_effects=_SPLIT_EFFECT),
    )(*bufs, ssem, rsem, after)


_REDUCE_GROUPS = (
    ((5, 1), (3, 1), (4, 1), (2, 1), (1, 1), (0, 1)),
    ((5, 0), (3, 0), (4, 0)),
    ((2, 0), (1, 0)),
    ((0, 0),),
)


def _reduce_keys(group):
    keys = []
    for p, layer in group:
        if (_PIECES[p][1], layer) not in keys:
            keys.append((_PIECES[p][1], layer))
    return keys


def _half_block(piece, tm):
    _, _, rs, cs, rstep, cstep, coff = piece
    return lambda i, j, c: (j * (rstep // tm) + c * (rs // 2 // tm) + i, coff // cs + j * (cstep // cs))


def _swap_start(g, dwb):
    group = _REDUCE_GROUPS[g]
    keys = _reduce_keys(group)
    nk = len(keys)

    def body(*refs):
        src = dict(zip(keys, refs[nk:2 * nk]))
        dst = dict(zip(keys, refs[2 * nk:3 * nk]))
        ssem, rsem, token = refs[3 * nk:]
        x, y, c = _pos()
        for k, (p, layer) in enumerate(group):
            key = (_PIECES[p][1], layer)
            for j in range(N_CHIPS):
                _remote(_half_view(src[key], _PIECES[p], j, 1 - c), _half_view(dst[key], _PIECES[p], j, 1 - c),
                        ssem.at[k * N_CHIPS + j], rsem.at[k * N_CHIPS + j], (x, y, 1 - c)).start()
        token[...] = jnp.zeros(token.shape, token.dtype)

    sem = pltpu.SemaphoreType.DMA((N_CHIPS * len(group),))
    shapes = [jax.ShapeDtypeStruct(dwb[k].shape, BF16) for k in keys]
    res = pl.pallas_call(
        body, name="swap_start_g%d" % g,
        in_specs=[HBM_SPEC] * nk,
        out_specs=[HBM_SPEC] * (2 * nk) + [SEM_SPEC, SEM_SPEC, pl.BlockSpec(memory_space=pltpu.VMEM)],
        out_shape=shapes + shapes + [sem, sem, jax.ShapeDtypeStruct((SUBLANES, LANES), F32)],
        input_output_aliases={t: t for t in range(nk)},
        compiler_params=_params(has_side_effects=_SPLIT_EFFECT),
    )(*[pltpu.with_memory_space_constraint(dwb[k], pltpu.HBM) for k in keys])
    return list(res[:nk]), list(res[nk:2 * nk]), res[2 * nk], res[2 * nk + 1], res[2 * nk + 2]


def _swap_wait(g, own, land, ssem, rsem, after):
    group = _REDUCE_GROUPS[g]
    keys = _reduce_keys(group)
    nk = len(keys)

    def body(*refs):
        ssem_ref, rsem_ref = refs[2 * nk], refs[2 * nk + 1]
        src = dict(zip(keys, refs[2 * nk + 3:3 * nk + 3]))
        dst = dict(zip(keys, refs[3 * nk + 3:]))
        x, y, c = _pos()
        for k, (p, layer) in enumerate(group):
            key = (_PIECES[p][1], layer)
            for j in range(N_CHIPS):
                cp = _remote(_half_view(src[key], _PIECES[p], j, 1 - c), _half_view(dst[key], _PIECES[p], j, c),
                             ssem_ref.at[k * N_CHIPS + j], rsem_ref.at[k * N_CHIPS + j], (x, y, 1 - c))
                cp.wait_send()
                cp.wait_recv()

    res = pl.pallas_call(
        body, name="swap_wait_g%d" % g,
        in_specs=[HBM_SPEC] * (2 * nk) + [SEM_SPEC, SEM_SPEC, pl.BlockSpec(memory_space=pl.ANY)],
        out_specs=[HBM_SPEC] * (2 * nk),
        out_shape=[jax.ShapeDtypeStruct(a.shape, a.dtype) for a in list(own) + list(land)],
        input_output_aliases={t: t for t in range(2 * nk)},
        compiler_params=_params(has_side_effects=_SPLIT_EFFECT),
    )(*own, *land, ssem, rsem, after)
    return dict(zip(keys, res[nk:]))


def _chip_partial(piece, dw, got, prev, c_arr, name):
    _, _, rs, cs, _, _, _ = piece
    half = rs // 2
    tm = _tile(half, 256)
    blk = _half_block(piece, tm)

    def body(s_ref, dw_ref, got_ref, *rest):
        rest[-1][...] = (dw_ref[...] + got_ref[...].astype(F32)).astype(BF16)

    spec = pl.BlockSpec((tm, cs), lambda j, i, s: blk(i, j, s[0]))
    in_specs = [spec, spec]
    args = [dw, got]
    aliases = {}
    if prev is not None:
        in_specs.append(pl.BlockSpec(memory_space=pl.ANY))
        args.append(prev)
        aliases = {3: 0}
    return pl.pallas_call(
        body, name=name,
        grid_spec=pltpu.PrefetchScalarGridSpec(
            num_scalar_prefetch=1, grid=(N_CHIPS, half // tm), in_specs=in_specs, out_specs=spec),
        out_shape=jax.ShapeDtypeStruct(dw.shape, BF16), input_output_aliases=aliases,
        compiler_params=_params(("parallel", "parallel")),
    )(c_arr, *args)


def _scatter_start(g, partials):
    group = _REDUCE_GROUPS[g]
    keys = _reduce_keys(group)
    nk, n = len(keys), len(group)

    def body(*refs):
        pt = dict(zip(keys, refs[nk:2 * nk]))
        land = refs[2 * nk:2 * nk + n]
        ssem, rsem, token = refs[2 * nk + n:]
        _, _, c = _pos()

        def send(mine, others):
            for k, (p, layer) in enumerate(group):
                for j in others:
                    _remote(_half_view(pt[(_PIECES[p][1], layer)], _PIECES[p], j, c), land[k].at[mine],
                            ssem.at[k * N_CHIPS + j], rsem.at[k * N_CHIPS + mine], (j // 2, j % 2, c)).start()

        _for_my_chip(send)
        token[...] = jnp.zeros(token.shape, token.dtype)

    sem = pltpu.SemaphoreType.DMA((N_CHIPS * n,))
    res = pl.pallas_call(
        body, name="scatter_start_g%d" % g,
        in_specs=[HBM_SPEC] * nk,
        out_specs=[HBM_SPEC] * (nk + n) + [SEM_SPEC, SEM_SPEC, pl.BlockSpec(memory_space=pltpu.VMEM)],
        out_shape=([jax.ShapeDtypeStruct(partials[k].shape, BF16) for k in keys]
                   + [jax.ShapeDtypeStruct((N_CHIPS, _PIECES[p][2] // 2, _PIECES[p][3]), BF16) for p, _ in group]
                   + [sem, sem, jax.ShapeDtypeStruct((SUBLANES, LANES), F32)]),
        input_output_aliases={t: t for t in range(nk)},
        compiler_params=_params(has_side_effects=_SPLIT_EFFECT),
    )(*[pltpu.with_memory_space_constraint(partials[k], pltpu.HBM) for k in keys])
    return list(res[:nk]), list(res[nk:nk + n]), res[nk + n], res[nk + n + 1], res[nk + n + 2]


def _scatter_wait(g, partials, land, ssem, rsem, after):
    group = _REDUCE_GROUPS[g]
    keys = _reduce_keys(group)
    nk, n = len(keys), len(group)

    def body(*refs):
        ssem_ref, rsem_ref = refs[nk + n], refs[nk + n + 1]
        pt = dict(zip(keys, refs[nk + n + 3:2 * nk + n + 3]))
        land_ref = refs[2 * nk + n + 3:]
        _, _, c = _pos()

        def wait(mine, others):
            for k, (p, layer) in enumerate(group):
                for j in others:
                    cp = _remote(_half_view(pt[(_PIECES[p][1], layer)], _PIECES[p], j, c), land_ref[k].at[j],
                                 ssem_ref.at[k * N_CHIPS + j], rsem_ref.at[k * N_CHIPS + j], (j // 2, j % 2, c))
                    cp.wait_send()
                    cp.wait_recv()

        _for_my_chip(wait)

    res = pl.pallas_call(
        body, name="scatter_wait_g%d" % g,
        in_specs=[HBM_SPEC] * (nk + n) + [SEM_SPEC, SEM_SPEC, pl.BlockSpec(memory_space=pl.ANY)],
        out_specs=[HBM_SPEC] * (nk + n),
        out_shape=[jax.ShapeDtypeStruct(a.shape, a.dtype) for a in list(partials) + list(land)],
        input_output_aliases={t: t for t in range(nk + n)},
        compiler_params=_params(has_side_effects=_SPLIT_EFFECT),
    )(*partials, *land, ssem, rsem, after)
    return list(res[nk:])


def _reduce_half(piece, layer, dw, got, land, prev, idx, name):
    _, _, rs, cs, _, _, _ = piece
    half = rs // 2
    tm = _tile(half, 256)
    blk = _half_block(piece, tm)

    def body(s_ref, dw_ref, got_ref, r1, r2, r3, *rest):
        acc = dw_ref[...] + got_ref[...].astype(F32)
        for r in (r1, r2, r3):
            acc = acc + r[...].astype(F32)
        rest[-1][...] = acc

    def land_map(k):
        return lambda i, s: ((s[1] + k) % N_CHIPS, i, 0)

    own = pl.BlockSpec((tm, cs), lambda i, s: blk(i, s[1], s[0]))
    in_specs = [own, own] + [pl.BlockSpec((None, tm, cs), land_map(k)) for k in (1, 2, 3)]
    args = [dw, got, land, land, land]
    aliases = {}
    if prev is not None:
        in_specs.append(pl.BlockSpec(memory_space=pl.ANY))
        args.append(prev)
        aliases = {6: 0}
    return pl.pallas_call(
        body, name=name,
        grid_spec=pltpu.PrefetchScalarGridSpec(
            num_scalar_prefetch=1, grid=(half // tm,), in_specs=in_specs,
            out_specs=pl.BlockSpec((None, tm, cs), lambda i, s: (layer, s[0] * (half // tm) + i, 0))),
        out_shape=jax.ShapeDtypeStruct((DEPTH, rs, cs), F32), input_output_aliases=aliases,
        compiler_params=_params(("parallel",)),
    )(idx, *args)


def _share_halves(reduced):
    def body(*refs):
        buf = refs[N_PIECES:2 * N_PIECES]
        ssem, rsem = refs[2 * N_PIECES:]
        x, y, c = _pos()

        def half(p, layer, cc):
            rows = _PIECES[p][2] // 2
            return buf[p].at[layer, pl.ds(pl.multiple_of(cc * rows, SUBLANES), rows), :]

        pairs = [(p, layer) for p in range(N_PIECES) for layer in range(DEPTH)]
        rem = [_remote(half(p, layer, c), half(p, layer, c), ssem.at[k], rsem.at[k], (x, y, 1 - c))
               for k, (p, layer) in enumerate(pairs)]
        for cp in rem:
            cp.start()
        for k, (p, layer) in enumerate(pairs):
            rem[k].wait_send()
            _remote(half(p, layer, 1 - c), half(p, layer, 1 - c), ssem.at[k], rsem.at[k], (x, y, 1 - c)).wait_recv()

    nsem = N_PIECES * DEPTH
    return pl.pallas_call(
        body, name="share_halves",
        in_specs=[HBM_SPEC] * N_PIECES, out_specs=[HBM_SPEC] * N_PIECES,
        out_shape=[jax.ShapeDtypeStruct((DEPTH, p[2], p[3]), F32) for p in _PIECES],
        input_output_aliases={t: t for t in range(N_PIECES)},
        scratch_shapes=[pltpu.SemaphoreType.DMA((nsem,)), pltpu.SemaphoreType.DMA((nsem,))],
        compiler_params=_params(has_side_effects=True),
    )(*reduced)


N_DEV = 8


def _place_slot(v, me_arr, take_block):
    rows = v.shape[0] // N_DEV if take_block else v.shape[0]
    tm = _tile(rows, 512)
    steps = rows // tm

    def body(s_ref, v_ref, out_ref):
        out_ref[...] = v_ref[...]

    return pl.pallas_call(
        body, name="place_small_block" if take_block else "place_small_sum",
        grid_spec=pltpu.PrefetchScalarGridSpec(
            num_scalar_prefetch=1, grid=(steps,),
            in_specs=[pl.BlockSpec((tm, LANES), lambda i, s: (s[0] * steps * take_block + i, 0))],
            out_specs=pl.BlockSpec((None, tm, LANES), lambda i, s: (s[0], i, 0))),
        out_shape=jax.ShapeDtypeStruct((N_DEV, rows, LANES), F32),
        compiler_params=_params(("parallel",)),
    )(me_arr, v)


def _all_peers():
    x, y, c = _pos()
    flip = lambda v, f: 1 - v if f else v
    return (x, y, c), [(flip(x, a), flip(y, b), flip(c, d))
                       for a in (0, 1) for b in (0, 1) for d in (0, 1) if a or b or d]


def _slot_index(dev):
    return 4 * dev[0] + 2 * dev[1] + dev[2]


def _exchange_start(g, src, name):
    rows = g.shape[1]
    n_in = 1 if src is None else 2

    def body(*refs):
        g_ref = refs[n_in]
        src_ref = refs[n_in + 1] if src is not None else None
        ssem, rsem, token = refs[2 * n_in:]
        me, peers = _all_peers()
        for k, dev in enumerate(peers):
            if src is None:
                mine = g_ref.at[_slot_index(me)]
            else:
                mine = src_ref.at[pl.ds(pl.multiple_of(_slot_index(dev) * rows, SUBLANES), rows), :]
            _remote(mine, g_ref.at[_slot_index(me)], ssem.at[k], rsem.at[k], dev).start()
        token[...] = jnp.zeros(token.shape, token.dtype)

    sem = pltpu.SemaphoreType.DMA((N_DEV - 1,))
    args = [g] if src is None else [g, src]
    res = pl.pallas_call(
        body, name=name,
        in_specs=[HBM_SPEC] * n_in,
        out_specs=[HBM_SPEC] * n_in + [SEM_SPEC, SEM_SPEC, pl.BlockSpec(memory_space=pltpu.VMEM)],
        out_shape=[jax.ShapeDtypeStruct(a.shape, a.dtype) for a in args]
        + [sem, sem, jax.ShapeDtypeStruct((SUBLANES, LANES), F32)],
        input_output_aliases={t: t for t in range(n_in)},
        compiler_params=_params(has_side_effects=_SPLIT_EFFECT),
    )(*[pltpu.with_memory_space_constraint(a, pltpu.HBM) for a in args])
    return list(res[:n_in]), res[n_in], res[n_in + 1], res[n_in + 2]


def _exchange_wait(bufs, ssem, rsem, after, name):
    n_in = len(bufs)

    def body(*refs):
        ssem_ref, rsem_ref = refs[n_in], refs[n_in + 1]
        g_ref = refs[n_in + 3]
        me, peers = _all_peers()
        for k, dev in enumerate(peers):
            cp = _remote(g_ref.at[_slot_index(me)], g_ref.at[_slot_index(dev)], ssem_ref.at[k], rsem_ref.at[k], dev)
            cp.wait_send()
            cp.wait_recv()

    res = pl.pallas_call(
        body, name=name,
        in_specs=[HBM_SPEC] * n_in + [SEM_SPEC, SEM_SPEC, pl.BlockSpec(memory_space=pl.ANY)],
        out_specs=[HBM_SPEC] * n_in,
        out_shape=[jax.ShapeDtypeStruct(a.shape, a.dtype) for a in bufs],
        input_output_aliases={t: t for t in range(n_in)},
        compiler_params=_params(has_side_effects=_SPLIT_EFFECT),
    )(*bufs, ssem, rsem, after)
    return res[0]


def _sum_slots(g, name):
    n, rows, _ = g.shape
    tm = _tile(rows, 512)

    def body(g_ref, out_ref):
        acc = g_ref[0]
        for k in range(1, n):
            acc = acc + g_ref[k]
        out_ref[...] = acc

    return pl.pallas_call(
        body, name=name, grid=(rows // tm,),
        in_specs=[pl.BlockSpec((n, tm, LANES), lambda i: (0, i, 0))],
        out_specs=pl.BlockSpec((tm, LANES), lambda i: (i, 0)),
        out_shape=jax.ShapeDtypeStruct((rows, LANES), F32),
        compiler_params=_params(("parallel",)),
    )(g)


_TINY = ("ln_mix_g", "ret_log_gamma", "ssm_a_re", "ssm_a_im", "ssm_log_dt", "ssm_d", "b_glu", "ln_ffn_g", "ln_final_g")
_MID = ("ssm_b_re", "ssm_b_im", "ssm_c_re", "ssm_c_im")
_SMALL = _TINY + _MID
_FLAT_ALIGN = LANES * LANES
_FLAT_ROWS = 1024


def _flat_rows(like, names):
    rows = sum((math.prod(like[n].shape) + (-math.prod(like[n].shape)) % _FLAT_ALIGN) // LANES for n in names)
    return rows + (-rows) % _FLAT_ROWS


def _flatten(d, names):
    parts = []
    for n in names:
        f = d[n].reshape(-1)
        parts.append(jnp.pad(f, (0, (-f.shape[0]) % _FLAT_ALIGN)))
    total = sum(p.shape[0] for p in parts)
    parts.append(jnp.zeros(((-total) % (_FLAT_ROWS * LANES),), F32))
    return jnp.concatenate(parts).reshape(-1, LANES)


def _unflatten(flat, like, names):
    out, row = {}, 0
    for n in names:
        size = math.prod(like[n].shape)
        rows = (size + (-size) % _FLAT_ALIGN) // LANES
        part = lax.optimization_barrier(flat[row:row + rows])
        out[n] = part.reshape(-1)[:size].reshape(like[n].shape)
        row += rows
    return out


_BIG = ("w_in", "w_glu", "w_out", "w_ffn_gate", "w_ffn_up", "w_ffn_down")
_WEIGHTS = ("ln_mix_g", "w_in", "ret_log_gamma", "ssm_a_re", "ssm_a_im", "ssm_log_dt", "ssm_b_re", "ssm_b_im",
            "ssm_c_re", "ssm_c_im", "ssm_d", "w_glu", "b_glu", "w_out", "ln_ffn_g", "w_ffn_gate", "w_ffn_up",
            "w_ffn_down", "ln_final_g")


def _rope_tables(seq):
    half = QK_DIM // 2
    inv = 1.0 / (ROPE_BASE ** (jnp.arange(half, dtype=F32) / half))
    ang = jnp.arange(seq, dtype=F32)[:, None] * inv[None, :]
    return jnp.cos(ang), jnp.sin(ang)


def _step(w, m, v, x, target):
    seq = x.shape[0]
    seg_len = float(seq // SEGMENTS)
    c_idx = lax.axis_index("c").astype(jnp.int32)
    chip_idx = (2 * lax.axis_index("x") + lax.axis_index("y")).astype(jnp.int32)
    c_arr = jnp.stack([c_idx])
    idx_arr = jnp.stack([c_idx, chip_idx])

    chip_arr = jnp.stack([chip_idx])
    placed = {}

    def cast(pieces, layer):
        for p in pieces:
            key = (_PIECES[p][1], layer)
            placed[key] = _cast_place(_PIECES[p], w[_PIECES[p][0]], layer, placed.get(key), chip_arr,
                                      "cast_%s_l%d" % (_PIECES[p][0], layer))

    for layer, pieces in _GATHER_GROUPS:
        cast(pieces, layer)
    flying, ssems, rsems, token = _gather_start(list(range(len(_GATHER_GROUPS))), placed)
    wf = {b[0]: [None] * DEPTH for b in _BUFFERS}

    handing = {}

    def arrive(g, after):
        ks = _group_keys(g)
        landed = _gather_wait(g, [flying[k] for k in ks], ssems[g], rsems[g], after)
        for k, a in zip(ks, _gather_forward(g, landed)):
            wf[k[0]][k[1]] = a

    def hand_over(g, after):
        ks = _group_keys(g)
        landed = _gather_wait(g, [flying[k] for k in ks], ssems[g], rsems[g], after)
        bufs, fs, fr, tok = _forward_start(g, landed)
        handing[g] = (bufs, fs, fr)
        return tok[0:1, 0:1]

    def complete(g, after):
        for k, a in zip(_group_keys(g), _forward_wait(g, *handing[g], after)):
            wf[k[0]][k[1]] = a

    cos, sin = _rope_tables(seq)

    started = token[0, 0]
    s5_ops, s5_vjps = [], []
    for i in range(DEPTH):
        s5_raw = (w["ssm_a_re"][i] + started, w["ssm_a_im"][i], w["ssm_log_dt"][i], w["ssm_b_re"][i], w["ssm_b_im"][i])
        disc, disc_vjp = jax.vjp(functools.partial(_s5_discretize, seg_len=seg_len), *s5_raw)
        bblk, cblk, lam = _s5_pack(*disc, w["ssm_c_re"][i] + started, w["ssm_c_im"][i] + started)
        s5_ops.append((bblk.astype(BF16), cblk.astype(BF16), lam))
        s5_vjps.append(disc_vjp)
    tiny_flat = [_flatten({**d, "ln_final_g": d["ln_final_g"] + started}, _TINY) for d in (w, m, v)]
    corner = lambda a: a[(0,) * (a.ndim - 2)][0:1, 0:1].astype(F32)
    prepared = sum(corner(a) for ops in s5_ops for a in ops) + sum(corner(a) for a in tiny_flat) + corner(cos) + corner(sin)

    saved = []
    xc = x + token[0, 0]
    for i in range(DEPTH):
        t = "_l%d" % i
        s = {"x_in": xc}
        if i == 0:
            s["h"] = _rms_fwd(xc, w["ln_mix_g"][i:i + 1], "rms_mix" + t)
            arrive(0, prepared + corner(s["h"]))
        else:
            s["h"] = _rms_fwd(xc, w["ln_mix_g"][i:i + 1] + next_in, "rms_mix" + t)
        s["proj"] = _matmul(s["h"], wf["in"][i], "nn", [F32], name="mm_in" + t)[0]
        s["qr"], s["kr"] = _rot_fwd(s["proj"], cos, sin, "rot" + t)
        s["lg"] = jnp.broadcast_to(w["ret_log_gamma"][i].T[:, :, None], (HEADS, 2, LANES))
        s["y"] = _ret_fwd(s["qr"], s["kr"], s["proj"], s["lg"], "ret" + t)
        s["s5"], s["disc_vjp"] = s5_ops[i], s5_vjps[i]
        s["s5y"] = _s5_fwd(s["proj"], *s["s5"], "s5" + t)
        d_skip = w["ssm_d"][i:i + 1] + hand_over(2 * i + 1, s["s5y"])
        s["ret"], s["ysg"], s["ysgb"] = _post1_fwd(s["y"], s["proj"], s["s5y"], d_skip, "post" + t)
        complete(2 * i + 1, s["ysgb"])
        s["z"] = _matmul(s["ysgb"], wf["glu"][i], "nn", [F32], name="mm_glu" + t)[0]
        s["merged"] = _merge_fwd(s["z"], s["ysg"], s["proj"], s["ret"], w["b_glu"][i:i + 1], "merge" + t)
        s["x1"] = _matmul(s["merged"], wf["out"][i], "nn", [F32], add=xc, name="mm_out" + t)[0]
        s["h2"] = _rms_fwd(s["x1"], w["ln_ffn_g"][i:i + 1], "rms_ffn" + t)
        s["ab"] = _matmul(s["h2"], wf["gu"][i], "nn", [F32], name="mm_gu" + t)[0]
        if i + 1 < DEPTH:
            next_in = hand_over(2 * i + 2, s["ab"])
        s["f"] = _glu_fwd(s["ab"], "glu" + t)
        xc = _matmul(s["f"], wf["down"][i], "nn", [F32], add=s["x1"], name="mm_down" + t)[0]
        if i + 1 < DEPTH:
            complete(2 * i + 2, xc)
        saved.append(s)

    dx, dxb, loss_row, dg_final = _loss_stage(xc, target, w["ln_final_g"][None, :], "loss")
    loss = lax.psum(loss_row[0, 0], ("x", "y", "c"))

    g_small = {"ln_final_g": dg_final[0]}
    per_layer = {n: [None] * DEPTH for n in _SMALL if n != "ln_final_g"}
    dws, got, swaps, flights = {}, {}, {}, []

    def dw_mm(a, b, buf, i, name):
        dws[(buf, i)] = _matmul(a, b, "tn", [F32, BF16], name=name)

    def depart(g):
        keys = _reduce_keys(_REDUCE_GROUPS[g])
        own, land, ssem, rsem, tok = _swap_start(g, {k: dws[k][1] for k in keys})
        swaps[g] = (own, land, ssem, rsem)
        return tok[0:1, 0:1]

    def proceed(g, after):
        group = _REDUCE_GROUPS[g]
        got.update(_swap_wait(g, *swaps[g], after))
        partials = {}
        for p, layer in group:
            key = (_PIECES[p][1], layer)
            partials[key] = _chip_partial(_PIECES[p], dws[key][0], got[key], partials.get(key), c_arr,
                                          "chip_partial_%s_l%d" % (_PIECES[p][0], layer))
        pt, land, ssem, rsem, tok = _scatter_start(g, partials)
        flights.append((g, pt, land, ssem, rsem))
        return tok[0:1, 0:1]

    for i in reversed(range(DEPTH)):
        t = "_l%d" % i
        s = saved[i]
        g_ffn, g_mix, d_skip = w["ln_ffn_g"][i:i + 1], w["ln_mix_g"][i:i + 1], w["ssm_d"][i:i + 1]
        dw_mm(s["f"], dxb, "down", i, "dw_down" + t)
        df = _matmul(dxb, wf["down"][i], "nt", [F32], name="dx_down" + t)[0]
        if i == 0:
            g_ffn = g_ffn + proceed(0, df)
        dab = _glu_bwd(s["ab"], df, "glu_bwd" + t)
        dw_mm(s["h2"], dab, "gu", i, "dw_gu" + t)
        if i == 0:
            g_ffn = g_ffn + depart(1)
        dh2 = _matmul(dab, wf["gu"][i], "nt", [F32], name="dx_gu" + t)[0]
        if i == 0:
            g_ffn = g_ffn + proceed(1, dh2)
        dx1, dx1b, dg = _rms_bwd(s["x1"], dh2, dx, g_ffn, "rms_ffn_bwd" + t)
        per_layer["ln_ffn_g"][i] = dg[0]

        dw_mm(s["merged"], dx1b, "out", i, "dw_out" + t)
        dmerged = _matmul(dx1b, wf["out"][i], "nt", [F32], name="dx_out" + t)[0]
        dz, dys_part, dgs, db = _merge_bwd(s["z"], s["ysg"], s["proj"], s["ret"], dmerged, w["b_glu"][i:i + 1],
                                           "merge_bwd" + t)
        per_layer["b_glu"][i] = db[0]
        dw_mm(s["ysgb"], dz, "glu", i, "dw_glu" + t)
        if i == 0:
            d_skip = d_skip + depart(2)
        dys = _matmul(dz, wf["glu"][i], "nt", [F32], add=dys_part, name="dx_glu" + t)[0]
        if i == 0:
            d_skip = d_skip + proceed(2, dys)
        dy, dgg, dgr, ds5, du_part, dd = _post1_bwd(s["y"], s["proj"], s["s5y"], dmerged, dys,
                                                    d_skip, "post_bwd" + t)
        per_layer["ssm_d"][i] = dd[0]
        du, dbblk, dcblk, dlam = _s5_bwd(s["proj"], ds5, du_part, *s["s5"], "s5_bwd" + t)
        dlr, dli, dbr, dbi, dcr, dci = _s5_unpack(dbblk, dcblk, dlam)
        zeros = jnp.zeros_like(dlr)
        da_re, da_im, dlog_dt, db_re, db_im = s["disc_vjp"]((dlr, dli, zeros, zeros, dbr, dbi))
        for n, val in (("ssm_a_re", da_re), ("ssm_a_im", da_im), ("ssm_log_dt", dlog_dt), ("ssm_b_re", db_re),
                       ("ssm_b_im", db_im), ("ssm_c_re", dcr), ("ssm_c_im", dci)):
            per_layer[n][i] = val
        dqr, dkr, dv, dlg = _ret_bwd(s["qr"], s["kr"], s["proj"], dy, s["lg"], "ret_bwd" + t)
        per_layer["ret_log_gamma"][i] = dlg[:, :, 0].T
        dqkv = _rot_bwd(dqr, dkr, dv, cos, sin, "rot_bwd" + t)
        dproj = jnp.concatenate([dqkv, dgg, du, dgr, dgs], axis=1)
        dw_mm(s["h"], dproj, "in", i, "dw_in" + t)
        if i == 0:
            g_mix = g_mix + depart(3)
        dh = _matmul(dproj, wf["in"][i], "nt", [F32], name="dx_in" + t)[0]
        if i == 0:
            g_mix = g_mix + proceed(3, dh)
        dx, dxb, dg = _rms_bwd(s["x_in"], dh, dx1, g_mix, "rms_mix_bwd" + t)
        per_layer["ln_mix_g"][i] = dg[0]
        if i == DEPTH - 1:
            dxb = dxb + depart(0).astype(BF16)

    for n in per_layer:
        g_small[n] = jnp.stack(per_layer[n])
    me_arr = jnp.stack([2 * chip_idx + c_idx])
    g_mine = _flatten(g_small, _SMALL)
    rs_bufs, rs_ssem, rs_rsem, small_token = _exchange_start(_place_slot(g_mine, me_arr, True), g_mine,
                                                             "small_scatter_start")

    reduced = [None] * N_PIECES
    for g, pt, land, ssem, rsem in flights:
        landed = _scatter_wait(g, pt, land, ssem, rsem, small_token)
        for (p, layer), buf in zip(_REDUCE_GROUPS[g], landed):
            key = (_PIECES[p][1], layer)
            reduced[p] = _reduce_half(_PIECES[p], layer, dws[key][0], got[key], buf, reduced[p], idx_arr,
                                      "reduce_%s_l%d" % (_PIECES[p][0], layer))
    g_big = dict(zip([p[0] for p in _PIECES], _share_halves(reduced)))

    landed = _exchange_wait(rs_bufs, rs_ssem, rs_rsem, g_big[_BIG[-1]], "small_scatter_wait")
    ag_bufs, ag_ssem, ag_rsem, ag_token = _exchange_start(
        _place_slot(_sum_slots(landed, "sum_small"), me_arr, False), None, "small_gather_start")

    grads, delta, new_m, new_v = {}, {}, {}, {}
    for n in _BIG:
        d, r, cc = w[n].shape
        two_d = lambda a: a.reshape(d * r, cc)
        dl, mn, vn = _adamw(two_d(w[n]), two_d(g_big[n]), two_d(m[n]), two_d(v[n]), "adamw_" + n, after=ag_token)
        grads[n], delta[n], new_m[n], new_v[n] = g_big[n], dl.reshape(d, r, cc), mn.reshape(d, r, cc), vn.reshape(d, r, cc)

    all_done = sum(corner(delta[n]) for n in _BIG)
    gathered = _exchange_wait(ag_bufs, ag_ssem, ag_rsem, all_done, "small_gather_wait")
    g_flat = gathered.reshape(-1, LANES)
    grads.update(_unflatten(g_flat, w, _SMALL))
    tiny_rows = _flat_rows(w, _TINY)
    dl, mn, vn = _adamw(tiny_flat[0], g_flat[:tiny_rows], tiny_flat[1], tiny_flat[2], "adamw_tiny")
    for dst, flat in ((delta, dl), (new_m, mn), (new_v, vn)):
        dst.update(_unflatten(flat, w, _TINY))
    for n in _MID:
        delta[n], new_m[n], new_v[n] = _adamw_nd(w[n], grads[n], m[n], v[n], "adamw_" + n)
    return loss, dx, grads, delta, new_m, new_v


def kernel(x, ln_mix_g, w_in, ret_log_gamma, ssm_a_re, ssm_a_im, ssm_log_dt, ssm_b_re, ssm_b_im, ssm_c_re, ssm_c_im, ssm_d, w_glu, b_glu, w_out, ln_ffn_g, w_ffn_gate, w_ffn_up, w_ffn_down, ln_final_g, loss_target, m_ln_mix_g, m_w_in, m_ret_log_gamma, m_ssm_a_re, m_ssm_a_im, m_ssm_log_dt, m_ssm_b_re, m_ssm_b_im, m_ssm_c_re, m_ssm_c_im, m_ssm_d, m_w_glu, m_b_glu, m_w_out, m_ln_ffn_g, m_w_ffn_gate, m_w_ffn_up, m_w_ffn_down, m_ln_final_g, v_ln_mix_g, v_w_in, v_ret_log_gamma, v_ssm_a_re, v_ssm_a_im, v_ssm_log_dt, v_ssm_b_re, v_ssm_b_im, v_ssm_c_re, v_ssm_c_im, v_ssm_d, v_w_glu, v_b_glu, v_w_out, v_ln_ffn_g, v_w_ffn_gate, v_w_ffn_up, v_w_ffn_down, v_ln_final_g):
    given = dict(locals())
    w = {n: given[n] for n in _WEIGHTS}
    m = {n: given["m_" + n] for n in _WEIGHTS}
    v = {n: given["v_" + n] for n in _WEIGHTS}
    loss, dx, grads, delta, new_m, new_v = _step(w, m, v, x[0], loss_target[0])
    return (loss, dx[None], *[grads[n] for n in _WEIGHTS], *[delta[n] for n in _WEIGHTS],
            *[new_m[n] for n in _WEIGHTS], *[new_v[n] for n in _WEIGHTS])
```

```python
import functools
import math

import jax
import jax.numpy as jnp
from jax import lax
from jax.experimental import pallas as pl
from jax.experimental.pallas import tpu as pltpu

F32 = jnp.float32
BF16 = jnp.bfloat16

D_MODEL = 2048
DEPTH = 2
HEADS = 4
QK_DIM = 256
V_DIM = 512
QK_WIDTH = HEADS * QK_DIM
ROPE_BASE = 10000.0
GROUP = 16
N_GROUPS = D_MODEL // GROUP
N_STATE = 64
D_FF = 5632
IN_WIDTH = 2 * QK_WIDTH + 5 * D_MODEL
EPS = 1e-6
N_CHIPS = 4

ADAM_LR = 0.001
ADAM_B1 = 0.9
ADAM_B2 = 0.999
ADAM_EPS = 1e-08
ADAM_WD = 0.01
ADAM_STEP = 10

LANES = 128
SUBLANES = 8
VMEM_LIMIT = 56 * 1024 * 1024
SEGMENTS = SUBLANES
GROUPS_PER_TILE = LANES // GROUP
STATE_COLS = GROUPS_PER_TILE * N_STATE
N_TILES = D_MODEL // LANES
SCAN_UNROLL = 4

MESH = pl.DeviceIdType.MESH
HBM_SPEC = pl.BlockSpec(memory_space=pltpu.HBM)


def _params(sem=None, **kw):
    return pltpu.CompilerParams(dimension_semantics=sem, vmem_limit_bytes=VMEM_LIMIT, **kw)


def _tile(n, cap=1024):
    for t in (2048, 1024, 512, 256, 128, 64):
        if t <= cap and n % t == 0:
            return t
    raise ValueError(n)


def _rows_call(fn, rows, pars, row_outs, par_outs, *, tm, name):
    m = rows[0][0].shape[0]
    nr, npar, nro, npo = len(rows), len(pars), len(row_outs), len(par_outs)

    def body(*refs):
        rin = refs[:nr]
        pin = refs[nr:nr + npar]
        rout = refs[nr + npar:nr + npar + nro]
        pout = refs[nr + npar + nro:]
        res = fn(*[r[...] for r in rin], *[p[...] for p in pin])
        if not isinstance(res, (tuple, list)):
            res = (res,)
        for r, v in zip(rout, res[:nro]):
            r[...] = v.astype(r.dtype)
        if npo:
            @pl.when(pl.program_id(0) == 0)
            def _():
                for p in pout:
                    p[...] = jnp.zeros(p.shape, p.dtype)
            for p, v in zip(pout, res[nro:]):
                p[...] += v

    in_specs = [pl.BlockSpec((tm, w), functools.partial(lambda cb, i: (i, cb), cb)) for (_, w, cb) in rows]
    in_specs += [pl.BlockSpec(p.shape, lambda i: (0, 0)) for p in pars]
    out_specs = [pl.BlockSpec((tm, w), lambda i: (i, 0)) for (w, _) in row_outs]
    out_specs += [pl.BlockSpec(s, lambda i: (0, 0)) for s in par_outs]
    out_shape = [jax.ShapeDtypeStruct((m, w), dt) for (w, dt) in row_outs]
    out_shape += [jax.ShapeDtypeStruct(s, F32) for s in par_outs]
    res = pl.pallas_call(
        body, name=name, grid=(m // tm,), in_specs=in_specs, out_specs=out_specs, out_shape=out_shape,
        compiler_params=_params(("arbitrary",) if npo else ("parallel",)),
    )(*[a for (a, _, _) in rows], *pars)
    return res


def _f32(*vals):
    return [v.astype(F32) for v in vals]


def _f_rms(x, g):
    r = lax.rsqrt(jnp.mean(x * x, axis=-1, keepdims=True) + EPS)
    return x * r * g


def _rms_fwd(x, g, name):
    return _rows_call(lambda xv, gv: _f_rms(xv, gv), [(x, D_MODEL, 0)], [g], [(D_MODEL, BF16)], [],
                      tm=256, name=name)[0]


def _rms_bwd(x, dh, dres, g, name):
    def fn(xv, dhv, drv, gv):
        _, vjp = jax.vjp(_f_rms, xv, gv)
        dx, dg = vjp(dhv)
        dx = dx + drv
        return dx, dx, dg
    return _rows_call(fn, [(x, D_MODEL, 0), (dh, D_MODEL, 0), (dres, D_MODEL, 0)], [g],
                      [(D_MODEL, F32), (D_MODEL, BF16)], [(1, D_MODEL)], tm=256, name=name)


def _rot_heads(xv, cos, sin, scale):
    half = QK_DIM // 2
    outs = []
    for h in range(HEADS):
        x1 = xv[:, h * QK_DIM:h * QK_DIM + half]
        x2 = xv[:, h * QK_DIM + half:(h + 1) * QK_DIM]
        outs += [(x1 * cos - x2 * sin) * scale, (x1 * sin + x2 * cos) * scale]
    return jnp.concatenate(outs, axis=1)


def _rot_fwd(proj, cos, sin, name):
    def fn(q, k, cv, sv):
        return _rot_heads(q, cv, sv, 1.0), _rot_heads(k, cv, sv, QK_DIM ** -0.5)
    return _rows_call(fn, [(proj, QK_WIDTH, 0), (proj, QK_WIDTH, 1), (cos, LANES, 0), (sin, LANES, 0)], [],
                      [(QK_WIDTH, BF16), (QK_WIDTH, BF16)], [], tm=256, name=name)


def _rot_bwd(dqr, dkr, dv, cos, sin, name):
    def fn(dq, dk, dvv, cv, sv):
        return jnp.concatenate([_rot_heads(dq, cv, -sv, 1.0), _rot_heads(dk, cv, -sv, QK_DIM ** -0.5), dvv], axis=1)
    return _rows_call(fn, [(dqr, QK_WIDTH, 0), (dkr, QK_WIDTH, 0), (dv, D_MODEL, 0), (cos, LANES, 0), (sin, LANES, 0)],
                      [], [(2 * QK_WIDTH + D_MODEL, BF16)], [], tm=256, name=name)[0]


def _f_post1(y0, y1, y2, y3, g, gr, s5, u, dsk):
    yn = [yh * lax.rsqrt(jnp.mean(yh * yh, axis=-1, keepdims=True) + EPS) for yh in (y0, y1, y2, y3)]
    ret = jax.nn.sigmoid(gr) * (jax.nn.silu(g) * jnp.concatenate(yn, axis=1))
    ysg = jax.nn.gelu(s5 + dsk * u)
    return ret, ysg


def _post1_rows(y, proj, s5y):
    rows = [(y, V_DIM, h) for h in range(HEADS)]
    rows += [(proj, D_MODEL, 2), (proj, D_MODEL, 4), (s5y, D_MODEL, 0), (proj, D_MODEL, 3)]
    return rows


def _post1_fwd(y, proj, s5y, dsk, name):
    def fn(*vals):
        ret, ysg = _f_post1(*vals)
        return ret, ysg, ysg
    return _rows_call(fn, _post1_rows(y, proj, s5y), [dsk],
                      [(D_MODEL, F32), (D_MODEL, F32), (D_MODEL, BF16)], [], tm=128, name=name)


def _post1_bwd(y, proj, s5y, dret, dys, dsk, name):
    def fn(*vals):
        prim = vals[:8] + (vals[10],)
        _, vjp = jax.vjp(_f_post1, *prim)
        gy0, gy1, gy2, gy3, gg, ggr, gs5, gu, gd = vjp((vals[8], vals[9]))
        return jnp.concatenate([gy0, gy1, gy2, gy3], axis=1), gg, ggr, gs5, gu, gd
    rows = _post1_rows(y, proj, s5y) + [(dret, D_MODEL, 0), (dys, D_MODEL, 0)]
    return _rows_call(fn, rows, [dsk],
                      [(D_MODEL, BF16), (D_MODEL, BF16), (D_MODEL, BF16), (D_MODEL, F32), (D_MODEL, F32)],
                      [(1, D_MODEL)], tm=128, name=name)


def _f_merge(z, ysg, gs, ret, b):
    return ret + jax.nn.sigmoid(gs) * (ysg * jax.nn.sigmoid(z + b))


def _merge_fwd(z, ysg, proj, ret, b, name):
    return _rows_call(_f_merge, [(z, D_MODEL, 0), (ysg, D_MODEL, 0), (proj, D_MODEL, 5), (ret, D_MODEL, 0)], [b],
                      [(D_MODEL, BF16)], [], tm=128, name=name)[0]


def _merge_bwd(z, ysg, proj, ret, dm, b, name):
    def fn(zv, yv, gv, rv, dmv, bv):
        _, vjp = jax.vjp(_f_merge, zv, yv, gv, rv, bv)
        gz, gy, gg, _, gb = vjp(dmv)
        return gz, gy, gg, gb
    rows = [(z, D_MODEL, 0), (ysg, D_MODEL, 0), (proj, D_MODEL, 5), (ret, D_MODEL, 0), (dm, D_MODEL, 0)]
    return _rows_call(fn, rows, [b], [(D_MODEL, BF16), (D_MODEL, F32), (D_MODEL, BF16)], [(1, D_MODEL)],
                      tm=128, name=name)


def _f_glu(a, b):
    return jax.nn.silu(a) * b


def _glu_fwd(ab, name):
    return _rows_call(_f_glu, [(ab, D_FF, 0), (ab, D_FF, 1)], [], [(D_FF, BF16)], [], tm=128, name=name)[0]


def _glu_bwd(ab, df, name):
    def fn(a, b, d):
        _, vjp = jax.vjp(_f_glu, a, b)
        ga, gb = vjp(d)
        return jnp.concatenate([ga, gb], axis=1)
    return _rows_call(fn, [(ab, D_FF, 0), (ab, D_FF, 1), (df, D_FF, 0)], [], [(2 * D_FF, BF16)], [],
                      tm=128, name=name)[0]


def _loss_stage(x, tgt, g, name):
    def fn(xv, tv, gv):
        def lf(xx, gg):
            err = _f_rms(xx, gg) - tv
            row = jnp.mean(err * err, axis=-1, keepdims=True)
            return 0.5 * jnp.sum(row, axis=0, keepdims=True)
        l, vjp = jax.vjp(lf, xv, gv)
        dx, dg = vjp(jnp.ones((1, 1), F32))
        return dx, dx, jnp.broadcast_to(l, (1, LANES)), dg
    return _rows_call(fn, [(x, D_MODEL, 0), (tgt, D_MODEL, 0)], [g], [(D_MODEL, F32), (D_MODEL, BF16)],
                      [(1, LANES), (1, D_MODEL)], tm=256, name=name)


def _adam_math(wv, gv, mv, vv):
    mn = ADAM_B1 * mv + (1.0 - ADAM_B1) * gv
    vn = ADAM_B2 * vv + (1.0 - ADAM_B2) * (gv * gv)
    m_hat = mn / (1.0 - ADAM_B1 ** ADAM_STEP)
    v_hat = vn / (1.0 - ADAM_B2 ** ADAM_STEP)
    delta = -ADAM_LR * (m_hat / (jnp.sqrt(v_hat) + ADAM_EPS) + ADAM_WD * wv)
    return delta, mn, vn


def _adamw(w, g, m, v, name, after=None):
    rows, cols = w.shape
    tm = _tile(rows, 128 if cols > D_FF // N_CHIPS else (256 if cols > LANES else 512))
    fn = _adam_math if after is None else (lambda wv, gv, mv, vv, _: _adam_math(wv, gv, mv, vv))
    return _rows_call(fn, [(w, cols, 0), (g, cols, 0), (m, cols, 0), (v, cols, 0)], [] if after is None else [after],
                      [(cols, F32)] * 3, [], tm=tm, name=name)


def _adamw_nd(w, g, m, v, name):
    shape = w.shape
    lead = math.prod(shape[:-2])
    blk = (lead // 8,) + shape[-2:]
    three_d = lambda a: a.reshape((lead,) + shape[-2:])

    def body(w_ref, g_ref, m_ref, v_ref, d_ref, mn_ref, vn_ref):
        d_ref[...], mn_ref[...], vn_ref[...] = _adam_math(w_ref[...], g_ref[...], m_ref[...], v_ref[...])

    spec = pl.BlockSpec(blk, lambda i: (i, 0, 0))
    res = pl.pallas_call(
        body, name=name, grid=(8,), in_specs=[spec] * 4, out_specs=[spec] * 3,
        out_shape=[jax.ShapeDtypeStruct((lead,) + shape[-2:], F32)] * 3,
        compiler_params=_params(("parallel",)),
    )(three_d(w), three_d(g), three_d(m), three_d(v))
    return [r.reshape(shape) for r in res]


MATMUL_VMEM_BUDGET = 44 * 1024 * 1024


def _matmul_tiles(m, n, k, out_bytes, has_add):
    if k > 2048:
        return _tile(m, 1024), _tile(n, 1024), _tile(k, 1024)
    tm, tn, tk = _tile(m, 2048), _tile(n, 1024), k

    def footprint():
        acc = 4 * tm * tn if k // tk > 1 else 0
        return 2 * 2 * (tm * tk + tk * tn) + 2 * (out_bytes + 4 * has_add) * tm * tn + acc

    while footprint() > MATMUL_VMEM_BUDGET:
        if tn > 512 and n % (tn // 2) == 0:
            tn //= 2
        elif tk > 512 and k % (tk // 2) == 0:
            tk //= 2
        else:
            tm //= 2
    return tm, tn, tk


def _matmul(a, b, mode, out_dtypes, *, name, add=None):
    if mode == "nn":
        (m, k), (_, n) = a.shape, b.shape
    elif mode == "nt":
        (m, k), (n, _) = a.shape, b.shape
    else:
        (k, m), (_, n) = a.shape, b.shape
    n_out = len(out_dtypes)
    has_add = add is not None
    tm, tn, tk = _matmul_tiles(m, n, k, sum(jnp.dtype(dt).itemsize for dt in out_dtypes), has_add)
    nk = k // tk
    if mode == "nn":
        a_spec = pl.BlockSpec((tm, tk), lambda i, j, kk: (i, kk))
        b_spec = pl.BlockSpec((tk, tn), lambda i, j, kk: (kk, j))
        dims = (((1,), (0,)), ((), ()))
    elif mode == "nt":
        a_spec = pl.BlockSpec((tm, tk), lambda i, j, kk: (i, kk))
        b_spec = pl.BlockSpec((tn, tk), lambda i, j, kk: (j, kk))
        dims = (((1,), (1,)), ((), ()))
    else:
        a_spec = pl.BlockSpec((tk, tm), lambda i, j, kk: (kk, i))
        b_spec = pl.BlockSpec((tk, tn), lambda i, j, kk: (kk, j))
        dims = (((0,), (0,)), ((), ()))

    def body(*refs):
        a_ref, b_ref = refs[0], refs[1]
        add_ref = refs[2] if has_add else None
        outs = refs[2 + has_add:2 + has_add + n_out]

        def finish(r):
            if has_add:
                r = r + add_ref[...]
            for o in outs:
                o[...] = r.astype(o.dtype)

        if nk == 1:
            finish(lax.dot_general(a_ref[...], b_ref[...], dims, preferred_element_type=F32))
            return
        acc = refs[-1]
        kk = pl.program_id(2)

        @pl.when(kk == 0)
        def _():
            acc[...] = jnp.zeros(acc.shape, F32)

        acc[...] += lax.dot_general(a_ref[...], b_ref[...], dims, preferred_element_type=F32)

        @pl.when(kk == nk - 1)
        def _():
            finish(acc[...])

    in_specs = [a_spec, b_spec]
    args = [a, b]
    if has_add:
        in_specs.append(pl.BlockSpec((tm, tn), lambda i, j, kk: (i, j)))
        args.append(add)
    return pl.pallas_call(
        body, name=name, grid=(m // tm, n // tn, nk), in_specs=in_specs,
        out_specs=[pl.BlockSpec((tm, tn), lambda i, j, kk: (i, j))] * n_out,
        out_shape=[jax.ShapeDtypeStruct((m, n), dt) for dt in out_dtypes],
        scratch_shapes=[pltpu.VMEM((tm, tn), F32)] if nk > 1 else [],
        compiler_params=_params(("parallel", "parallel", "arbitrary")),
    )(*args)


RET_TQ = 512


def _decay(lg_ref, i, tq, seq):
    n_idx = i * tq + lax.broadcasted_iota(jnp.int32, (tq, seq), 0)
    m_idx = lax.broadcasted_iota(jnp.int32, (tq, seq), 1)
    diff = (n_idx - m_idx).astype(F32)
    lgf = lg_ref[0, 0:1, 0:1]
    lgb = lg_ref[0, 1:2, 0:1]
    causal = diff >= 0
    return jnp.exp(jnp.where(causal, lgf * diff, -lgb * diff)), diff, causal


_NT = (((1,), (1,)), ((), ()))
_TN = (((0,), (0,)), ((), ()))


def _ret_fwd(qr, kr, proj, lg, name):
    seq = qr.shape[0]
    tq = RET_TQ
    v_blk0 = (2 * QK_WIDTH) // V_DIM

    def body(q_ref, k_ref, v_ref, lg_ref, y_ref):
        i = pl.program_id(1)
        s = lax.dot_general(q_ref[...], k_ref[...], _NT, preferred_element_type=F32)
        dm, _, _ = _decay(lg_ref, i, tq, seq)
        p = (s * dm).astype(BF16)
        y_ref[...] = jnp.dot(p, v_ref[...].astype(BF16), preferred_element_type=F32)

    return pl.pallas_call(
        body, name=name, grid=(HEADS, seq // tq),
        in_specs=[pl.BlockSpec((tq, QK_DIM), lambda h, i: (i, h)),
                  pl.BlockSpec((seq, QK_DIM), lambda h, i: (0, h)),
                  pl.BlockSpec((seq, V_DIM), lambda h, i: (0, v_blk0 + h)),
                  pl.BlockSpec((1, 2, LANES), lambda h, i: (h, 0, 0))],
        out_specs=pl.BlockSpec((tq, V_DIM), lambda h, i: (i, h)),
        out_shape=jax.ShapeDtypeStruct((seq, HEADS * V_DIM), F32),
        compiler_params=_params(("parallel", "parallel")),
    )(qr, kr, proj, lg)


def _ret_bwd(qr, kr, proj, dy, lg, name):
    seq = qr.shape[0]
    tq = RET_TQ
    v_blk0 = (2 * QK_WIDTH) // V_DIM

    def body(q_ref, k_ref, v_ref, dy_ref, lg_ref, dq_ref, dk_ref, dv_ref, dlg_ref):
        i = pl.program_id(1)

        @pl.when(i == 0)
        def _():
            dk_ref[...] = jnp.zeros(dk_ref.shape, F32)
            dv_ref[...] = jnp.zeros(dv_ref.shape, F32)
            dlg_ref[...] = jnp.zeros(dlg_ref.shape, F32)

        q = q_ref[...]
        k = k_ref[...]
        vb = v_ref[...].astype(BF16)
        dyb = dy_ref[...]
        s = lax.dot_general(q, k, _NT, preferred_element_type=F32)
        dm, diff, causal = _decay(lg_ref, i, tq, seq)
        p = s * dm
        dp = lax.dot_general(dyb, vb, _NT, preferred_element_type=F32)
        dv_ref[...] += lax.dot_general(p.astype(BF16), dyb, _TN, preferred_element_type=F32)
        ds = (dp * dm).astype(BF16)
        dq_ref[...] = jnp.dot(ds, k, preferred_element_type=F32)
        dk_ref[...] += lax.dot_general(ds, q, _TN, preferred_element_type=F32)
        gd = dp * p * diff
        dlf = jnp.sum(jnp.sum(jnp.where(causal, gd, 0.0), axis=1, keepdims=True), axis=0, keepdims=True)
        dlb = jnp.sum(jnp.sum(jnp.where(causal, 0.0, -gd), axis=1, keepdims=True), axis=0, keepdims=True)
        row = lax.broadcasted_iota(jnp.int32, (2, LANES), 0)
        dlg_ref[0] += jnp.where(row == 0, dlf, dlb)

    return pl.pallas_call(
        body, name=name, grid=(HEADS, seq // tq),
        in_specs=[pl.BlockSpec((tq, QK_DIM), lambda h, i: (i, h)),
                  pl.BlockSpec((seq, QK_DIM), lambda h, i: (0, h)),
                  pl.BlockSpec((seq, V_DIM), lambda h, i: (0, v_blk0 + h)),
                  pl.BlockSpec((tq, V_DIM), lambda h, i: (i, h)),
                  pl.BlockSpec((1, 2, LANES), lambda h, i: (h, 0, 0))],
        out_specs=[pl.BlockSpec((tq, QK_DIM), lambda h, i: (i, h)),
                   pl.BlockSpec((seq, QK_DIM), lambda h, i: (0, h)),
                   pl.BlockSpec((seq, V_DIM), lambda h, i: (0, h)),
                   pl.BlockSpec((1, 2, LANES), lambda h, i: (h, 0, 0))],
        out_shape=[jax.ShapeDtypeStruct((seq, QK_WIDTH), F32), jax.ShapeDtypeStruct((seq, QK_WIDTH), F32),
                   jax.ShapeDtypeStruct((seq, HEADS * V_DIM), F32), jax.ShapeDtypeStruct((HEADS, 2, LANES), F32)],
        compiler_params=_params(("parallel", "arbitrary")),
    )(qr, kr, proj, dy, lg)


def _shift_rows(v, reverse):
    row = lax.broadcasted_iota(jnp.int32, v.shape, 0)
    if reverse:
        return jnp.where(row == SEGMENTS - 1, 0.0, pltpu.roll(v, SEGMENTS - 1, 0))
    return jnp.where(row == 0, 0.0, pltpu.roll(v, 1, 0))


def _slab(t):
    if isinstance(t, int):
        return pl.ds(t * SEGMENTS, SEGMENTS)
    return pl.ds(pl.multiple_of(t * SEGMENTS, SEGMENTS), SEGMENTS)


def _unrolled_loop(body, lo, hi, init):
    main = (hi - lo) // SCAN_UNROLL

    def unrolled(g, carry):
        for k in range(SCAN_UNROLL):
            carry = body(lo + g * SCAN_UNROLL + k, carry)
        return carry

    carry = lax.fori_loop(0, main, unrolled, init)
    for t in range(lo + main * SCAN_UNROLL, hi):
        carry = body(t, carry)
    return carry


def _scan(xr_ref, xi_ref, lam, reverse, conj):
    steps = xr_ref.shape[0] // SEGMENTS
    cols = xr_ref.shape[1]
    lr = jnp.broadcast_to(lam[0], (SEGMENTS, cols))
    li = jnp.broadcast_to(lam[1], (SEGMENTS, cols))
    lrt = jnp.broadcast_to(lam[2], (SEGMENTS, cols))
    lit = jnp.broadcast_to(lam[3], (SEGMENTS, cols))
    if conj:
        li, lit = -li, -lit
    zero = jnp.zeros((SEGMENTS, cols), F32)

    def rows_of(t):
        return _slab(steps - 1 - t if reverse else t)

    def advance(t, carry):
        sr, si = carry
        rows = rows_of(t)
        return lr * sr - li * si + xr_ref[rows, :], lr * si + li * sr + xi_ref[rows, :]

    def step(t, carry):
        nr, ni = advance(t, carry)
        rows = rows_of(t)
        xr_ref[rows, :] = nr
        xi_ref[rows, :] = ni
        return nr, ni

    def run(body, init):
        return _unrolled_loop(body, 0, steps, init)

    er, ei = run(advance, (zero, zero))
    cr, ci = zero, zero
    for _ in range(SEGMENTS - 1):
        tr = er + lrt * cr - lit * ci
        ti = ei + lrt * ci + lit * cr
        cr, ci = _shift_rows(tr, reverse), _shift_rows(ti, reverse)
    run(step, (cr, ci))


def _permute_in(dst_ref, src_ref):
    steps = src_ref.shape[0] // SEGMENTS
    for s in range(SEGMENTS):
        dst_ref[pl.ds(s, steps, stride=SEGMENTS), :] = src_ref[s * steps:(s + 1) * steps, :].astype(dst_ref.dtype)


def _unpermute(src_ref, s):
    steps = src_ref.shape[0] // SEGMENTS
    return src_ref[pl.ds(s, steps, stride=SEGMENTS), :]


def _s5_fwd(proj, bblk, cblk, lam, name):
    seq = proj.shape[0]
    u_blk0 = (2 * QK_WIDTH + 2 * D_MODEL) // LANES
    sc = STATE_COLS

    def body(u_ref, b_ref, c_ref, lam_ref, y_ref, up_ref, yp_ref, xr_ref, xi_ref):
        _permute_in(up_ref, u_ref)
        ub = up_ref[...].astype(BF16)
        for d in range(2):
            xr_ref[...] = jnp.dot(ub, b_ref[d, :, 0:sc], preferred_element_type=F32)
            xi_ref[...] = jnp.dot(ub, b_ref[d, :, sc:2 * sc], preferred_element_type=F32)
            lm = [lam_ref[d, r:r + 1, :] for r in range(4)]
            _scan(xr_ref, xi_ref, lm, reverse=(d == 1), conj=False)
            yd = (jnp.dot(xr_ref[...].astype(BF16), c_ref[d, 0:sc, :], preferred_element_type=F32)
                  + jnp.dot(xi_ref[...].astype(BF16), c_ref[d, sc:2 * sc, :], preferred_element_type=F32))
            if d == 0:
                yp_ref[...] = yd
            else:
                yp_ref[...] += yd
        steps = seq // SEGMENTS
        for s in range(SEGMENTS):
            y_ref[s * steps:(s + 1) * steps, :] = _unpermute(yp_ref, s)

    return pl.pallas_call(
        body, name=name, grid=(N_TILES,),
        in_specs=[pl.BlockSpec((seq, LANES), lambda j: (0, u_blk0 + j)),
                  pl.BlockSpec((2, None, LANES, 2 * sc), lambda j: (0, j, 0, 0)),
                  pl.BlockSpec((2, None, 2 * sc, LANES), lambda j: (0, j, 0, 0)),
                  pl.BlockSpec((2, None, 4, sc), lambda j: (0, j, 0, 0))],
        out_specs=pl.BlockSpec((seq, LANES), lambda j: (0, j)),
        out_shape=jax.ShapeDtypeStruct((seq, D_MODEL), F32),
        scratch_shapes=[pltpu.VMEM((seq, LANES), F32), pltpu.VMEM((seq, LANES), F32),
                        pltpu.VMEM((seq, sc), F32), pltpu.VMEM((seq, sc), F32)],
        compiler_params=_params(("parallel",)),
    )(proj, bblk, cblk, lam)


def _s5_bwd(proj, dy, du_part, bblk, cblk, lam, name):
    seq = proj.shape[0]
    u_blk0 = (2 * QK_WIDTH + 2 * D_MODEL) // LANES
    sc = STATE_COLS
    steps = seq // SEGMENTS

    def body(u_ref, dy_ref, dup_ref, b_ref, c_ref, lam_ref, du_ref, db_ref, dc_ref, dlam_ref,
             up_ref, dyp_ref, dua_ref, xr_ref, xi_ref, gr_ref, gi_ref):
        _permute_in(up_ref, u_ref)
        _permute_in(dyp_ref, dy_ref)
        ub = up_ref[...].astype(BF16)
        dyb = dyp_ref[...].astype(BF16)
        ubt = up_ref[...].T.astype(BF16)
        dybt = dyp_ref[...].T.astype(BF16)
        for d in range(2):
            reverse = d == 1
            xr_ref[...] = jnp.dot(ub, b_ref[d, :, 0:sc], preferred_element_type=F32)
            xi_ref[...] = jnp.dot(ub, b_ref[d, :, sc:2 * sc], preferred_element_type=F32)
            lm = [lam_ref[d, r:r + 1, :] for r in range(4)]
            _scan(xr_ref, xi_ref, lm, reverse=reverse, conj=False)
            xrb = xr_ref[...].astype(BF16)
            xib = xi_ref[...].astype(BF16)
            dc_ref[d, :, 0:sc] = jnp.dot(dybt, xrb, preferred_element_type=F32)
            dc_ref[d, :, sc:2 * sc] = jnp.dot(dybt, xib, preferred_element_type=F32)
            gr_ref[...] = lax.dot_general(dyb, c_ref[d, 0:sc, :], _NT, preferred_element_type=F32)
            gi_ref[...] = lax.dot_general(dyb, c_ref[d, sc:2 * sc, :], _NT, preferred_element_type=F32)
            _scan(gr_ref, gi_ref, lm, reverse=not reverse, conj=True)

            def acc_step(t, carry):
                ar, ai = carry
                prev = _slab(t + 1 if reverse else t - 1)
                pr = xr_ref[prev, :]
                pi = xi_ref[prev, :]
                zr = gr_ref[_slab(t), :]
                zi = gi_ref[_slab(t), :]
                return ar + zr * pr + zi * pi, ai + zi * pr - zr * pi

            zero = jnp.zeros((SEGMENTS, sc), F32)
            if reverse:
                ar, ai = _unrolled_loop(acc_step, 0, steps - 1, (zero, zero))
                edge = _slab(steps - 1)
                pr = _shift_rows(xr_ref[_slab(0), :], True)
                pi = _shift_rows(xi_ref[_slab(0), :], True)
            else:
                ar, ai = _unrolled_loop(acc_step, 1, steps, (zero, zero))
                edge = _slab(0)
                pr = _shift_rows(xr_ref[_slab(steps - 1), :], False)
                pi = _shift_rows(xi_ref[_slab(steps - 1), :], False)
            zr = gr_ref[edge, :]
            zi = gi_ref[edge, :]
            ar = ar + zr * pr + zi * pi
            ai = ai + zi * pr - zr * pi
            dlam_ref[d, 0:1, :] = jnp.sum(ar, axis=0, keepdims=True)
            dlam_ref[d, 1:2, :] = jnp.sum(ai, axis=0, keepdims=True)

            grb = gr_ref[...].astype(BF16)
            gib = gi_ref[...].astype(BF16)
            db_ref[d, :, 0:sc] = jnp.dot(ubt, grb, preferred_element_type=F32)
            db_ref[d, :, sc:2 * sc] = jnp.dot(ubt, gib, preferred_element_type=F32)
            dud = (lax.dot_general(grb, b_ref[d, :, 0:sc], _NT, preferred_element_type=F32)
                   + lax.dot_general(gib, b_ref[d, :, sc:2 * sc], _NT, preferred_element_type=F32))
            if d == 0:
                dua_ref[...] = dud
            else:
                dua_ref[...] += dud
        for s in range(SEGMENTS):
            rows = slice(s * steps, (s + 1) * steps)
            du_ref[rows, :] = (_unpermute(dua_ref, s) + dup_ref[rows, :]).astype(du_ref.dtype)

    return pl.pallas_call(
        body, name=name, grid=(N_TILES,),
        in_specs=[pl.BlockSpec((seq, LANES), lambda j: (0, u_blk0 + j)),
                  pl.BlockSpec((seq, LANES), lambda j: (0, j)),
                  pl.BlockSpec((seq, LANES), lambda j: (0, j)),
                  pl.BlockSpec((2, None, LANES, 2 * sc), lambda j: (0, j, 0, 0)),
                  pl.BlockSpec((2, None, 2 * sc, LANES), lambda j: (0, j, 0, 0)),
                  pl.BlockSpec((2, None, 4, sc), lambda j: (0, j, 0, 0))],
        out_specs=[pl.BlockSpec((seq, LANES), lambda j: (0, j)),
                   pl.BlockSpec((2, None, LANES, 2 * sc), lambda j: (0, j, 0, 0)),
                   pl.BlockSpec((2, None, LANES, 2 * sc), lambda j: (0, j, 0, 0)),
                   pl.BlockSpec((2, None, 2, sc), lambda j: (0, j, 0, 0))],
        out_shape=[jax.ShapeDtypeStruct((seq, D_MODEL), BF16),
                   jax.ShapeDtypeStruct((2, N_TILES, LANES, 2 * sc), F32),
                   jax.ShapeDtypeStruct((2, N_TILES, LANES, 2 * sc), F32),
                   jax.ShapeDtypeStruct((2, N_TILES, 2, sc), F32)],
        scratch_shapes=[pltpu.VMEM((seq, LANES), F32), pltpu.VMEM((seq, LANES), F32), pltpu.VMEM((seq, LANES), F32),
                        pltpu.VMEM((seq, sc), F32), pltpu.VMEM((seq, sc), F32),
                        pltpu.VMEM((seq, sc), F32), pltpu.VMEM((seq, sc), F32)],
        compiler_params=_params(("parallel",)),
    )(proj, dy, du_part, bblk, cblk, lam)


def _s5_discretize(a_re, a_im, log_dt, b_re, b_im, seg_len):
    dt = jnp.exp(log_dt)[..., None]
    e = jnp.exp(a_re * dt)
    lr, li = e * jnp.cos(a_im * dt), e * jnp.sin(a_im * dt)
    et = jnp.exp(a_re * dt * seg_len)
    lrt, lit = et * jnp.cos(a_im * dt * seg_len), et * jnp.sin(a_im * dt * seg_len)
    den = a_re * a_re + a_im * a_im
    qr = ((lr - 1.0) * a_re + li * a_im) / den
    qi = (li * a_re - (lr - 1.0) * a_im) / den
    br = qr[..., None] * b_re - qi[..., None] * b_im
    bi = qr[..., None] * b_im + qi[..., None] * b_re
    return lr, li, lrt, lit, br, bi


def _s5_pack(lr, li, lrt, lit, br, bi, c_re, c_im):
    eye = jnp.eye(GROUPS_PER_TILE, dtype=F32)

    def bd_b(b):
        b5 = b.reshape(2, N_TILES, GROUPS_PER_TILE, N_STATE, GROUP)
        return jnp.einsum("dtgph,gk->dtghkp", b5, eye).reshape(2, N_TILES, LANES, STATE_COLS)

    def bd_c(c):
        c5 = c.reshape(2, N_TILES, GROUPS_PER_TILE, GROUP, N_STATE)
        return jnp.einsum("dtghp,gk->dtkpgh", c5, eye).reshape(2, N_TILES, STATE_COLS, LANES)

    bblk = jnp.concatenate([bd_b(br), bd_b(bi)], axis=3)
    cblk = jnp.concatenate([bd_c(c_re), -bd_c(c_im)], axis=2)
    lam = jnp.stack([v.reshape(2, N_TILES, STATE_COLS) for v in (lr, li, lrt, lit)], axis=2)
    return bblk, cblk, lam


def _s5_unpack(dbblk, dcblk, dlam):
    eye = jnp.eye(GROUPS_PER_TILE, dtype=F32)

    def diag_b(d):
        d6 = d.reshape(2, N_TILES, GROUPS_PER_TILE, GROUP, GROUPS_PER_TILE, N_STATE)
        return jnp.einsum("dtghkp,gk->dtgph", d6, eye).reshape(2, N_GROUPS, N_STATE, GROUP)

    def diag_c(d):
        d6 = d.reshape(2, N_TILES, GROUPS_PER_TILE, GROUP, GROUPS_PER_TILE, N_STATE)
        return jnp.einsum("dtghkp,gk->dtghp", d6, eye).reshape(2, N_GROUPS, GROUP, N_STATE)

    dbr, dbi = diag_b(dbblk[..., :STATE_COLS]), diag_b(dbblk[..., STATE_COLS:])
    dcr, dci = diag_c(dcblk[..., :STATE_COLS]), -diag_c(dcblk[..., STATE_COLS:])
    dlr = dlam[:, :, 0, :].reshape(2, N_GROUPS, N_STATE)
    dli = dlam[:, :, 1, :].reshape(2, N_GROUPS, N_STATE)
    return dlr, dli, dbr, dbi, dcr, dci


def _pos():
    return lax.axis_index("x"), lax.axis_index("y"), lax.axis_index("c")


def _remote(src, dst, ssem, rsem, dev):
    return pltpu.make_async_remote_copy(src_ref=src, dst_ref=dst, send_sem=ssem, recv_sem=rsem,
                                        device_id=dev, device_id_type=MESH)


_PIECES = (
    ("w_in", "in", D_MODEL, IN_WIDTH // N_CHIPS, 0, IN_WIDTH // N_CHIPS, 0),
    ("w_glu", "glu", D_MODEL // N_CHIPS, D_MODEL, D_MODEL // N_CHIPS, 0, 0),
    ("w_out", "out", D_MODEL // N_CHIPS, D_MODEL, D_MODEL // N_CHIPS, 0, 0),
    ("w_ffn_gate", "gu", D_MODEL, D_FF // N_CHIPS, 0, D_FF // N_CHIPS, 0),
    ("w_ffn_up", "gu", D_MODEL, D_FF // N_CHIPS, 0, D_FF // N_CHIPS, D_FF),
    ("w_ffn_down", "down", D_FF // N_CHIPS, D_MODEL, D_FF // N_CHIPS, 0, 0),
)
_BUFFERS = (("in", D_MODEL, IN_WIDTH), ("glu", D_MODEL, D_MODEL), ("out", D_MODEL, D_MODEL),
            ("gu", D_MODEL, 2 * D_FF), ("down", D_FF, D_MODEL))
_BUF_INDEX = {name: t for t, (name, _, _) in enumerate(_BUFFERS)}
N_PIECES = len(_PIECES)
N_BUFFERS = len(_BUFFERS)


def _own_block(piece, tm):
    _, _, _, cs, rstep, cstep, coff = piece
    return lambda i, chip: (i + chip * (rstep // tm), coff // cs + chip * (cstep // cs))


def _cast_place(piece, w3, layer, prev, chip_arr, name):
    _, r, cc = w3.shape
    _, rf, cf = _BUFFERS[_BUF_INDEX[piece[1]]]
    tm = _tile(r, 256)
    own = _own_block(piece, tm)

    def body(s_ref, w_ref, *rest):
        rest[-1][...] = w_ref[...].astype(BF16)

    in_specs = [pl.BlockSpec((None, tm, cc), lambda i, s: (layer, i, 0))]
    args = [w3]
    aliases = {}
    if prev is not None:
        in_specs.append(pl.BlockSpec(memory_space=pl.ANY))
        args.append(prev)
        aliases = {2: 0}
    return pl.pallas_call(
        body, name=name,
        grid_spec=pltpu.PrefetchScalarGridSpec(
            num_scalar_prefetch=1, grid=(r // tm,), in_specs=in_specs,
            out_specs=pl.BlockSpec((tm, cc), lambda i, s: own(i, s[0]))),
        out_shape=jax.ShapeDtypeStruct((rf, cf), BF16), input_output_aliases=aliases,
        compiler_params=_params(("parallel",)),
    )(chip_arr, *args)


_GATHER_GROUPS = ((0, (0,)), (0, (1, 2)), (0, (3, 4, 5)), (1, (0,)), (1, (1, 2)), (1, (3, 4, 5)))
GROUPS_PER_LAYER = len(_GATHER_GROUPS) // DEPTH
_SPLIT_EFFECT = pltpu.SideEffectType.DATAFLOW_SIDE_EFFECTING
SEM_SPEC = pl.BlockSpec(memory_space=pltpu.SEMAPHORE)
BF16_ROWS = 2 * SUBLANES


def _group_keys(g):
    layer, pieces = _GATHER_GROUPS[g]
    keys = []
    for p in pieces:
        if (_PIECES[p][1], layer) not in keys:
            keys.append((_PIECES[p][1], layer))
    return keys


def _half_view(ref, piece, j, c):
    _, _, rs, cs, rstep, cstep, coff = piece
    half = rs // 2
    return ref.at[pl.ds(pl.multiple_of(j * rstep + c * half, BF16_ROWS), half), pl.ds(coff + j * cstep, cs)]


def _for_my_chip(fn):
    x, y, _ = _pos()
    for mine in range(N_CHIPS):
        pl.when(2 * x + y == mine)(functools.partial(fn, mine, [j for j in range(N_CHIPS) if j != mine]))


def _gather_start(groups, placed):
    keys = [k for g in groups for k in _group_keys(g)]
    nb, ng = len(keys), len(groups)

    def body(*refs):
        bufs = dict(zip(keys, refs[nb:2 * nb]))
        ssems = refs[2 * nb:2 * nb + ng]
        rsems = refs[2 * nb + ng:2 * nb + 2 * ng]
        token = refs[2 * nb + 2 * ng]
        _, _, c = _pos()

        def send(mine, others):
            for t, g in enumerate(groups):
                layer, pieces = _GATHER_GROUPS[g]
                for k, p in enumerate(pieces):
                    view = _half_view(bufs[(_PIECES[p][1], layer)], _PIECES[p], mine, c)
                    for j in others:
                        _remote(view, view, ssems[t].at[k * N_CHIPS + j], rsems[t].at[k * N_CHIPS + mine],
                                (j // 2, j % 2, c)).start()

        _for_my_chip(send)
        token[...] = jnp.zeros(token.shape, token.dtype)

    sems = [pltpu.SemaphoreType.DMA((N_CHIPS * len(_GATHER_GROUPS[g][1]),)) for g in groups]
    shapes = [jax.ShapeDtypeStruct(placed[k].shape, placed[k].dtype) for k in keys]
    res = pl.pallas_call(
        body, name="gather_start_g%d" % groups[0],
        in_specs=[HBM_SPEC] * nb,
        out_specs=[HBM_SPEC] * nb + [SEM_SPEC] * (2 * ng) + [pl.BlockSpec(memory_space=pltpu.VMEM)],
        out_shape=shapes + sems + sems + [jax.ShapeDtypeStruct((SUBLANES, LANES), F32)],
        input_output_aliases={t: t for t in range(nb)},
        compiler_params=_params(has_side_effects=_SPLIT_EFFECT),
    )(*[pltpu.with_memory_space_constraint(placed[k], pltpu.HBM) for k in keys])
    return (dict(zip(keys, res[:nb])), dict(zip(groups, res[nb:nb + ng])),
            dict(zip(groups, res[nb + ng:nb + 2 * ng])), res[nb + 2 * ng])


def _gather_wait(g, bufs, ssem, rsem, after):
    layer, pieces = _GATHER_GROUPS[g]
    keys = _group_keys(g)
    nb = len(keys)

    def body(*refs):
        ssem_ref, rsem_ref = refs[nb], refs[nb + 1]
        land = dict(zip(keys, refs[nb + 3:]))
        _, _, c = _pos()

        def wait(mine, others):
            for k, p in enumerate(pieces):
                ref = land[(_PIECES[p][1], layer)]
                for j in others:
                    cp = _remote(_half_view(ref, _PIECES[p], mine, c), _half_view(ref, _PIECES[p], j, c),
                                 ssem_ref.at[k * N_CHIPS + j], rsem_ref.at[k * N_CHIPS + j], (j // 2, j % 2, c))
                    cp.wait_send()
                    cp.wait_recv()

        _for_my_chip(wait)

    return pl.pallas_call(
        body, name="gather_wait_g%d" % g,
        in_specs=[HBM_SPEC] * nb + [SEM_SPEC, SEM_SPEC, pl.BlockSpec(memory_space=pl.ANY)],
        out_specs=[HBM_SPEC] * nb,
        out_shape=[jax.ShapeDtypeStruct(a.shape, a.dtype) for a in bufs],
        input_output_aliases={t: t for t in range(nb)},
        compiler_params=_params(has_side_effects=_SPLIT_EFFECT),
    )(*bufs, ssem, rsem, after)


def _gather_forward(g, bufs):
    layer, pieces = _GATHER_GROUPS[g]
    keys = _group_keys(g)
    nb = len(keys)

    def body(*refs):
        land = dict(zip(keys, refs[nb:2 * nb]))
        ssem, rsem = refs[2 * nb:]
        x, y, c = _pos()

        def forward(mine, others):
            cps = []
            for k, p in enumerate(pieces):
                ref = land[(_PIECES[p][1], layer)]
                for j in others:
                    view = _half_view(ref, _PIECES[p], j, c)
                    cp = _remote(view, view, ssem.at[k * N_CHIPS + j], rsem.at[k * N_CHIPS + j], (x, y, 1 - c))
                    cp.start()
                    cps.append(cp)
            for k, p in enumerate(pieces):
                ref = land[(_PIECES[p][1], layer)]
                for j in others:
                    view = _half_view(ref, _PIECES[p], j, 1 - c)
                    _remote(view, view, ssem.at[k * N_CHIPS + j], rsem.at[k * N_CHIPS + j], (x, y, 1 - c)).wait_recv()
            for cp in cps:
                cp.wait_send()

        _for_my_chip(forward)

    nsem = N_CHIPS * len(pieces)
    return pl.pallas_call(
        body, name="gather_forward_g%d" % g,
        in_specs=[HBM_SPEC] * nb, out_specs=[HBM_SPEC] * nb,
        out_shape=[jax.ShapeDtypeStruct(a.shape, a.dtype) for a in bufs],
        input_output_aliases={t: t for t in range(nb)},
        scratch_shapes=[pltpu.SemaphoreType.DMA((nsem,)), pltpu.SemaphoreType.DMA((nsem,))],
        compiler_params=_params(has_side_effects=True),
    )(*bufs)


def _forward_start(g, bufs):
    layer, pieces = _GATHER_GROUPS[g]
    keys = _group_keys(g)
    nb = len(keys)

    def body(*refs):
        land = dict(zip(keys, refs[nb:2 * nb]))
        ssem, rsem, token = refs[2 * nb:]
        x, y, c = _pos()

        def forward(mine, others):
            for k, p in enumerate(pieces):
                for j in others:
                    view = _half_view(land[(_PIECES[p][1], layer)], _PIECES[p], j, c)
                    _remote(view, view, ssem.at[k * N_CHIPS + j], rsem.at[k * N_CHIPS + j], (x, y, 1 - c)).start()

        _for_my_chip(forward)
        token[...] = jnp.zeros(token.shape, token.dtype)

    sem = pltpu.SemaphoreType.DMA((N_CHIPS * len(pieces),))
    res = pl.pallas_call(
        body, name="forward_start_g%d" % g,
        in_specs=[HBM_SPEC] * nb,
        out_specs=[HBM_SPEC] * nb + [SEM_SPEC, SEM_SPEC, pl.BlockSpec(memory_space=pltpu.VMEM)],
        out_shape=[jax.ShapeDtypeStruct(a.shape, a.dtype) for a in bufs]
        + [sem, sem, jax.ShapeDtypeStruct((SUBLANES, LANES), F32)],
        input_output_aliases={t: t for t in range(nb)},
        compiler_params=_params(has_side_effects=_SPLIT_EFFECT),
    )(*bufs)
    return list(res[:nb]), res[nb], res[nb + 1], res[nb + 2]


def _forward_wait(g, bufs, ssem, rsem, after):
    layer, pieces = _GATHER_GROUPS[g]
    keys = _group_keys(g)
    nb = len(keys)

    def body(*refs):
        ssem_ref, rsem_ref = refs[nb], refs[nb + 1]
        land = dict(zip(keys, refs[nb + 3:]))
        x, y, c = _pos()

        def wait(mine, others):
            for k, p in enumerate(pieces):
                ref = land[(_PIECES[p][1], layer)]
                for j in others:
                    cp = _remote(_half_view(ref, _PIECES[p], j, c), _half_view(ref, _PIECES[p], j, 1 - c),
                                 ssem_ref.at[k * N_CHIPS + j], rsem_ref.at[k * N_CHIPS + j], (x, y, 1 - c))
                    cp.wait_send()
                    cp.wait_recv()

        _for_my_chip(wait)

    return pl.pallas_call(
        body, name="forward_wait_g%d" % g,
        in_specs=[HBM_SPEC] * nb + [SEM_SPEC, SEM_SPEC, pl.BlockSpec(memory_space=pl.ANY)],
        out_specs=[HBM_SPEC] * nb,
        out_shape=[jax.ShapeDtypeStruct(a.shape, a.dtype) for a in bufs],
        input_output_aliases={t: t for t in range(nb)},
        compiler_params=_params(has_side_effects=_SPLIT_EFFECT),
    )(*bufs, ssem, rsem, after)


_REDUCE_GROUPS = (
    ((5, 1), (3, 1), (4, 1), (2, 1), (1, 1), (0, 1)),
    ((5, 0), (3, 0), (4, 0)),
    ((2, 0), (1, 0)),
    ((0, 0),),
)


def _reduce_keys(group):
    keys = []
    for p, layer in group:
        if (_PIECES[p][1], layer) not in keys:
            keys.append((_PIECES[p][1], layer))
    return keys


def _half_block(piece, tm):
    _, _, rs, cs, rstep, cstep, coff = piece
    return lambda i, j, c: (j * (rstep // tm) + c * (rs // 2 // tm) + i, coff // cs + j * (cstep // cs))


def _swap_start(g, dwb):
    group = _REDUCE_GROUPS[g]
    keys = _reduce_keys(group)
    nk = len(keys)

    def body(*refs):
        src = dict(zip(keys, refs[nk:2 * nk]))
        dst = dict(zip(keys, refs[2 * nk:3 * nk]))
        ssem, rsem, token = refs[3 * nk:]
        x, y, c = _pos()
        for k, (p, layer) in enumerate(group):
            key = (_PIECES[p][1], layer)
            for j in range(N_CHIPS):
                _remote(_half_view(src[key], _PIECES[p], j, 1 - c), _half_view(dst[key], _PIECES[p], j, 1 - c),
                        ssem.at[k * N_CHIPS + j], rsem.at[k * N_CHIPS + j], (x, y, 1 - c)).start()
        token[...] = jnp.zeros(token.shape, token.dtype)

    sem = pltpu.SemaphoreType.DMA((N_CHIPS * len(group),))
    shapes = [jax.ShapeDtypeStruct(dwb[k].shape, BF16) for k in keys]
    res = pl.pallas_call(
        body, name="swap_start_g%d" % g,
        in_specs=[HBM_SPEC] * nk,
        out_specs=[HBM_SPEC] * (2 * nk) + [SEM_SPEC, SEM_SPEC, pl.BlockSpec(memory_space=pltpu.VMEM)],
        out_shape=shapes + shapes + [sem, sem, jax.ShapeDtypeStruct((SUBLANES, LANES), F32)],
        input_output_aliases={t: t for t in range(nk)},
        compiler_params=_params(has_side_effects=_SPLIT_EFFECT),
    )(*[pltpu.with_memory_space_constraint(dwb[k], pltpu.HBM) for k in keys])
    return list(res[:nk]), list(res[nk:2 * nk]), res[2 * nk], res[2 * nk + 1], res[2 * nk + 2]


def _swap_wait(g, own, land, ssem, rsem, after):
    group = _REDUCE_GROUPS[g]
    keys = _reduce_keys(group)
    nk = len(keys)

    def body(*refs):
        ssem_ref, rsem_ref = refs[2 * nk], refs[2 * nk + 1]
        src = dict(zip(keys, refs[2 * nk + 3:3 * nk + 3]))
        dst = dict(zip(keys, refs[3 * nk + 3:]))
        x, y, c = _pos()
        for k, (p, layer) in enumerate(group):
            key = (_PIECES[p][1], layer)
            for j in range(N_CHIPS):
                cp = _remote(_half_view(src[key], _PIECES[p], j, 1 - c), _half_view(dst[key], _PIECES[p], j, c),
                             ssem_ref.at[k * N_CHIPS + j], rsem_ref.at[k * N_CHIPS + j], (x, y, 1 - c))
                cp.wait_send()
                cp.wait_recv()

    res = pl.pallas_call(
        body, name="swap_wait_g%d" % g,
        in_specs=[HBM_SPEC] * (2 * nk) + [SEM_SPEC, SEM_SPEC, pl.BlockSpec(memory_space=pl.ANY)],
        out_specs=[HBM_SPEC] * (2 * nk),
        out_shape=[jax.ShapeDtypeStruct(a.shape, a.dtype) for a in list(own) + list(land)],
        input_output_aliases={t: t for t in range(2 * nk)},
        compiler_params=_params(has_side_effects=_SPLIT_EFFECT),
    )(*own, *land, ssem, rsem, after)
    return dict(zip(keys, res[nk:]))


def _chip_partial(piece, dw, got, prev, c_arr, name):
    _, _, rs, cs, _, _, _ = piece
    half = rs // 2
    tm = _tile(half, 256)
    blk = _half_block(piece, tm)

    def body(s_ref, dw_ref, got_ref, *rest):
        rest[-1][...] = (dw_ref[...] + got_ref[...].astype(F32)).astype(BF16)

    spec = pl.BlockSpec((tm, cs), lambda j, i, s: blk(i, j, s[0]))
    in_specs = [spec, spec]
    args = [dw, got]
    aliases = {}
    if prev is not None:
        in_specs.append(pl.BlockSpec(memory_space=pl.ANY))
        args.append(prev)
        aliases = {3: 0}
    return pl.pallas_call(
        body, name=name,
        grid_spec=pltpu.PrefetchScalarGridSpec(
            num_scalar_prefetch=1, grid=(N_CHIPS, half // tm), in_specs=in_specs, out_specs=spec),
        out_shape=jax.ShapeDtypeStruct(dw.shape, BF16), input_output_aliases=aliases,
        compiler_params=_params(("parallel", "parallel")),
    )(c_arr, *args)


def _scatter_start(g, partials):
    group = _REDUCE_GROUPS[g]
    keys = _reduce_keys(group)
    nk, n = len(keys), len(group)

    def body(*refs):
        pt = dict(zip(keys, refs[nk:2 * nk]))
        land = refs[2 * nk:2 * nk + n]
        ssem, rsem, token = refs[2 * nk + n:]
        _, _, c = _pos()

        def send(mine, others):
            for k, (p, layer) in enumerate(group):
                for j in others:
                    _remote(_half_view(pt[(_PIECES[p][1], layer)], _PIECES[p], j, c), land[k].at[mine],
                            ssem.at[k * N_CHIPS + j], rsem.at[k * N_CHIPS + mine], (j // 2, j % 2, c)).start()

        _for_my_chip(send)
        token[...] = jnp.zeros(token.shape, token.dtype)

    sem = pltpu.SemaphoreType.DMA((N_CHIPS * n,))
    res = pl.pallas_call(
        body, name="scatter_start_g%d" % g,
        in_specs=[HBM_SPEC] * nk,
        out_specs=[HBM_SPEC] * (nk + n) + [SEM_SPEC, SEM_SPEC, pl.BlockSpec(memory_space=pltpu.VMEM)],
        out_shape=([jax.ShapeDtypeStruct(partials[k].shape, BF16) for k in keys]
                   + [jax.ShapeDtypeStruct((N_CHIPS, _PIECES[p][2] // 2, _PIECES[p][3]), BF16) for p, _ in group]
                   + [sem, sem, jax.ShapeDtypeStruct((SUBLANES, LANES), F32)]),
        input_output_aliases={t: t for t in range(nk)},
        compiler_params=_params(has_side_effects=_SPLIT_EFFECT),
    )(*[pltpu.with_memory_space_constraint(partials[k], pltpu.HBM) for k in keys])
    return list(res[:nk]), list(res[nk:nk + n]), res[nk + n], res[nk + n + 1], res[nk + n + 2]


def _scatter_wait(g, partials, land, ssem, rsem, after):
    group = _REDUCE_GROUPS[g]
    keys = _reduce_keys(group)
    nk, n = len(keys), len(group)

    def body(*refs):
        ssem_ref, rsem_ref = refs[nk + n], refs[nk + n + 1]
        pt = dict(zip(keys, refs[nk + n + 3:2 * nk + n + 3]))
        land_ref = refs[2 * nk + n + 3:]
        _, _, c = _pos()

        def wait(mine, others):
            for k, (p, layer) in enumerate(group):
                for j in others:
                    cp = _remote(_half_view(pt[(_PIECES[p][1], layer)], _PIECES[p], j, c), land_ref[k].at[j],
                                 ssem_ref.at[k * N_CHIPS + j], rsem_ref.at[k * N_CHIPS + j], (j // 2, j % 2, c))
                    cp.wait_send()
                    cp.wait_recv()

        _for_my_chip(wait)

    res = pl.pallas_call(
        body, name="scatter_wait_g%d" % g,
        in_specs=[HBM_SPEC] * (nk + n) + [SEM_SPEC, SEM_SPEC, pl.BlockSpec(memory_space=pl.ANY)],
        out_specs=[HBM_SPEC] * (nk + n),
        out_shape=[jax.ShapeDtypeStruct(a.shape, a.dtype) for a in list(partials) + list(land)],
        input_output_aliases={t: t for t in range(nk + n)},
        compiler_params=_params(has_side_effects=_SPLIT_EFFECT),
    )(*partials, *land, ssem, rsem, after)
    return list(res[nk:])


def _reduce_half(piece, layer, dw, got, land, prev, idx, name):
    _, _, rs, cs, _, _, _ = piece
    half = rs // 2
    tm = _tile(half, 256)
    blk = _half_block(piece, tm)

    def body(s_ref, dw_ref, got_ref, r1, r2, r3, *rest):
        acc = dw_ref[...] + got_ref[...].astype(F32)
        for r in (r1, r2, r3):
            acc = acc + r[...].astype(F32)
        rest[-1][...] = acc

    def land_map(k):
        return lambda i, s: ((s[1] + k) % N_CHIPS, i, 0)

    own = pl.BlockSpec((tm, cs), lambda i, s: blk(i, s[1], s[0]))
    in_specs = [own, own] + [pl.BlockSpec((None, tm, cs), land_map(k)) for k in (1, 2, 3)]
    args = [dw, got, land, land, land]
    aliases = {}
    if prev is not None:
        in_specs.append(pl.BlockSpec(memory_space=pl.ANY))
        args.append(prev)
        aliases = {6: 0}
    return pl.pallas_call(
        body, name=name,
        grid_spec=pltpu.PrefetchScalarGridSpec(
            num_scalar_prefetch=1, grid=(half // tm,), in_specs=in_specs,
            out_specs=pl.BlockSpec((None, tm, cs), lambda i, s: (layer, s[0] * (half // tm) + i, 0))),
        out_shape=jax.ShapeDtypeStruct((DEPTH, rs, cs), F32), input_output_aliases=aliases,
        compiler_params=_params(("parallel",)),
    )(idx, *args)


def _share_halves(reduced):
    def body(*refs):
        buf = refs[N_PIECES:2 * N_PIECES]
        ssem, rsem = refs[2 * N_PIECES:]
        x, y, c = _pos()

        def half(p, layer, cc):
            rows = _PIECES[p][2] // 2
            return buf[p].at[layer, pl.ds(pl.multiple_of(cc * rows, SUBLANES), rows), :]

        pairs = [(p, layer) for p in range(N_PIECES) for layer in range(DEPTH)]
        rem = [_remote(half(p, layer, c), half(p, layer, c), ssem.at[k], rsem.at[k], (x, y, 1 - c))
               for k, (p, layer) in enumerate(pairs)]
        for cp in rem:
            cp.start()
        for k, (p, layer) in enumerate(pairs):
            rem[k].wait_send()
            _remote(half(p, layer, 1 - c), half(p, layer, 1 - c), ssem.at[k], rsem.at[k], (x, y, 1 - c)).wait_recv()

    nsem = N_PIECES * DEPTH
    return pl.pallas_call(
        body, name="share_halves",
        in_specs=[HBM_SPEC] * N_PIECES, out_specs=[HBM_SPEC] * N_PIECES,
        out_shape=[jax.ShapeDtypeStruct((DEPTH, p[2], p[3]), F32) for p in _PIECES],
        input_output_aliases={t: t for t in range(N_PIECES)},
        scratch_shapes=[pltpu.SemaphoreType.DMA((nsem,)), pltpu.SemaphoreType.DMA((nsem,))],
        compiler_params=_params(has_side_effects=True),
    )(*reduced)


N_DEV = 8


def _place_slot(v, me_arr, take_block):
    rows = v.shape[0] // N_DEV if take_block else v.shape[0]
    tm = _tile(rows, 512)
    steps = rows // tm

    def body(s_ref, v_ref, out_ref):
        out_ref[...] = v_ref[...]

    return pl.pallas_call(
        body, name="place_small_block" if take_block else "place_small_sum",
        grid_spec=pltpu.PrefetchScalarGridSpec(
            num_scalar_prefetch=1, grid=(steps,),
            in_specs=[pl.BlockSpec((tm, LANES), lambda i, s: (s[0] * steps * take_block + i, 0))],
            out_specs=pl.BlockSpec((None, tm, LANES), lambda i, s: (s[0], i, 0))),
        out_shape=jax.ShapeDtypeStruct((N_DEV, rows, LANES), F32),
        compiler_params=_params(("parallel",)),
    )(me_arr, v)


def _all_peers():
    x, y, c = _pos()
    flip = lambda v, f: 1 - v if f else v
    return (x, y, c), [(flip(x, a), flip(y, b), flip(c, d))
                       for a in (0, 1) for b in (0, 1) for d in (0, 1) if a or b or d]


def _slot_index(dev):
    return 4 * dev[0] + 2 * dev[1] + dev[2]


def _exchange_start(g, src, name):
    rows = g.shape[1]
    n_in = 1 if src is None else 2

    def body(*refs):
        g_ref = refs[n_in]
        src_ref = refs[n_in + 1] if src is not None else None
        ssem, rsem, token = refs[2 * n_in:]
        me, peers = _all_peers()
        for k, dev in enumerate(peers):
            if src is None:
                mine = g_ref.at[_slot_index(me)]
            else:
                mine = src_ref.at[pl.ds(pl.multiple_of(_slot_index(dev) * rows, SUBLANES), rows), :]
            _remote(mine, g_ref.at[_slot_index(me)], ssem.at[k], rsem.at[k], dev).start()
        token[...] = jnp.zeros(token.shape, token.dtype)

    sem = pltpu.SemaphoreType.DMA((N_DEV - 1,))
    args = [g] if src is None else [g, src]
    res = pl.pallas_call(
        body, name=name,
        in_specs=[HBM_SPEC] * n_in,
        out_specs=[HBM_SPEC] * n_in + [SEM_SPEC, SEM_SPEC, pl.BlockSpec(memory_space=pltpu.VMEM)],
        out_shape=[jax.ShapeDtypeStruct(a.shape, a.dtype) for a in args]
        + [sem, sem, jax.ShapeDtypeStruct((SUBLANES, LANES), F32)],
        input_output_aliases={t: t for t in range(n_in)},
        compiler_params=_params(has_side_effects=_SPLIT_EFFECT),
    )(*[pltpu.with_memory_space_constraint(a, pltpu.HBM) for a in args])
    return list(res[:n_in]), res[n_in], res[n_in + 1], res[n_in + 2]


def _exchange_wait(bufs, ssem, rsem, after, name):
    n_in = len(bufs)

    def body(*refs):
        ssem_ref, rsem_ref = refs[n_in], refs[n_in + 1]
        g_ref = refs[n_in + 3]
        me, peers = _all_peers()
        for k, dev in enumerate(peers):
            cp = _remote(g_ref.at[_slot_index(me)], g_ref.at[_slot_index(dev)], ssem_ref.at[k], rsem_ref.at[k], dev)
            cp.wait_send()
            cp.wait_recv()

    res = pl.pallas_call(
        body, name=name,
        in_specs=[HBM_SPEC] * n_in + [SEM_SPEC, SEM_SPEC, pl.BlockSpec(memory_space=pl.ANY)],
        out_specs=[HBM_SPEC] * n_in,
        out_shape=[jax.ShapeDtypeStruct(a.shape, a.dtype) for a in bufs],
        input_output_aliases={t: t for t in range(n_in)},
        compiler_params=_params(has_side_effects=_SPLIT_EFFECT),
    )(*bufs, ssem, rsem, after)
    return res[0]


def _sum_slots(g, name):
    n, rows, _ = g.shape
    tm = _tile(rows, 512)

    def body(g_ref, out_ref):
        acc = g_ref[0]
        for k in range(1, n):
            acc = acc + g_ref[k]
        out_ref[...] = acc

    return pl.pallas_call(
        body, name=name, grid=(rows // tm,),
        in_specs=[pl.BlockSpec((n, tm, LANES), lambda i: (0, i, 0))],
        out_specs=pl.BlockSpec((tm, LANES), lambda i: (i, 0)),
        out_shape=jax.ShapeDtypeStruct((rows, LANES), F32),
        compiler_params=_params(("parallel",)),
    )(g)


_TINY = ("ln_mix_g", "ret_log_gamma", "ssm_a_re", "ssm_a_im", "ssm_log_dt", "ssm_d", "b_glu", "ln_ffn_g", "ln_final_g")
_MID = ("ssm_b_re", "ssm_b_im", "ssm_c_re", "ssm_c_im")
_SMALL = _TINY + _MID
_FLAT_ALIGN = LANES * LANES
_FLAT_ROWS = 1024


def _flat_rows(like, names):
    rows = sum((math.prod(like[n].shape) + (-math.prod(like[n].shape)) % _FLAT_ALIGN) // LANES for n in names)
    return rows + (-rows) % _FLAT_ROWS


def _flatten(d, names):
    parts = []
    for n in names:
        f = d[n].reshape(-1)
        parts.append(jnp.pad(f, (0, (-f.shape[0]) % _FLAT_ALIGN)))
    total = sum(p.shape[0] for p in parts)
    parts.append(jnp.zeros(((-total) % (_FLAT_ROWS * LANES),), F32))
    return jnp.concatenate(parts).reshape(-1, LANES)


def _unflatten(flat, like, names):
    out, row = {}, 0
    for n in names:
        size = math.prod(like[n].shape)
        rows = (size + (-size) % _FLAT_ALIGN) // LANES
        part = lax.optimization_barrier(flat[row:row + rows])
        out[n] = part.reshape(-1)[:size].reshape(like[n].shape)
        row += rows
    return out


_BIG = ("w_in", "w_glu", "w_out", "w_ffn_gate", "w_ffn_up", "w_ffn_down")
_WEIGHTS = ("ln_mix_g", "w_in", "ret_log_gamma", "ssm_a_re", "ssm_a_im", "ssm_log_dt", "ssm_b_re", "ssm_b_im",
            "ssm_c_re", "ssm_c_im", "ssm_d", "w_glu", "b_glu", "w_out", "ln_ffn_g", "w_ffn_gate", "w_ffn_up",
            "w_ffn_down", "ln_final_g")


def _rope_tables(seq):
    half = QK_DIM // 2
    inv = 1.0 / (ROPE_BASE ** (jnp.arange(half, dtype=F32) / half))
    ang = jnp.arange(seq, dtype=F32)[:, None] * inv[None, :]
    return jnp.cos(ang), jnp.sin(ang)


def _step(w, m, v, x, target):
    seq = x.shape[0]
    seg_len = float(seq // SEGMENTS)
    c_idx = lax.axis_index("c").astype(jnp.int32)
    chip_idx = (2 * lax.axis_index("x") + lax.axis_index("y")).astype(jnp.int32)
    c_arr = jnp.stack([c_idx])
    idx_arr = jnp.stack([c_idx, chip_idx])

    chip_arr = jnp.stack([chip_idx])
    placed = {}

    def cast(pieces, layer):
        for p in pieces:
            key = (_PIECES[p][1], layer)
            placed[key] = _cast_place(_PIECES[p], w[_PIECES[p][0]], layer, placed.get(key), chip_arr,
                                      "cast_%s_l%d" % (_PIECES[p][0], layer))

    for layer, pieces in _GATHER_GROUPS:
        cast(pieces, layer)
    flying, ssems, rsems, token = _gather_start(list(range(len(_GATHER_GROUPS))), placed)
    wf = {b[0]: [None] * DEPTH for b in _BUFFERS}

    handing = {}

    def arrive(g, after):
        ks = _group_keys(g)
        landed = _gather_wait(g, [flying[k] for k in ks], ssems[g], rsems[g], after)
        for k, a in zip(ks, _gather_forward(g, landed)):
            wf[k[0]][k[1]] = a

    def hand_over(g, after):
        ks = _group_keys(g)
        landed = _gather_wait(g, [flying[k] for k in ks], ssems[g], rsems[g], after)
        bufs, fs, fr, tok = _forward_start(g, landed)
        handing[g] = (bufs, fs, fr)
        return tok[0:1, 0:1]

    def complete(g, after):
        for k, a in zip(_group_keys(g), _forward_wait(g, *handing[g], after)):
            wf[k[0]][k[1]] = a

    cos, sin = _rope_tables(seq)

    started = token[0, 0]
    s5_ops, s5_vjps = [], []
    for i in range(DEPTH):
        s5_raw = (w["ssm_a_re"][i] + started, w["ssm_a_im"][i], w["ssm_log_dt"][i], w["ssm_b_re"][i], w["ssm_b_im"][i])
        disc, disc_vjp = jax.vjp(functools.partial(_s5_discretize, seg_len=seg_len), *s5_raw)
        bblk, cblk, lam = _s5_pack(*disc, w["ssm_c_re"][i] + started, w["ssm_c_im"][i] + started)
        s5_ops.append((bblk.astype(BF16), cblk.astype(BF16), lam))
        s5_vjps.append(disc_vjp)
    tiny_flat = [_flatten({**d, "ln_final_g": d["ln_final_g"] + started}, _TINY) for d in (w, m, v)]
    corner = lambda a: a[(0,) * (a.ndim - 2)][0:1, 0:1].astype(F32)
    prepared = sum(corner(a) for ops in s5_ops for a in ops) + sum(corner(a) for a in tiny_flat) + corner(cos) + corner(sin)

    saved = []
    xc = x + token[0, 0]
    for i in range(DEPTH):
        t = "_l%d" % i
        s = {"x_in": xc}
        if i == 0:
            s["h"] = _rms_fwd(xc, w["ln_mix_g"][i:i + 1], "rms_mix" + t)
            arrive(0, prepared + corner(s["h"]))
        else:
            s["h"] = _rms_fwd(xc, w["ln_mix_g"][i:i + 1] + next_in, "rms_mix" + t)
        s["proj"] = _matmul(s["h"], wf["in"][i], "nn", [F32], name="mm_in" + t)[0]
        s["qr"], s["kr"] = _rot_fwd(s["proj"], cos, sin, "rot" + t)
        s["lg"] = jnp.broadcast_to(w["ret_log_gamma"][i].T[:, :, None], (HEADS, 2, LANES))
        s["y"] = _ret_fwd(s["qr"], s["kr"], s["proj"], s["lg"], "ret" + t)
        s["s5"], s["disc_vjp"] = s5_ops[i], s5_vjps[i]
        s["s5y"] = _s5_fwd(s["proj"], *s["s5"], "s5" + t)
        first = GROUPS_PER_LAYER * i
        d_skip = w["ssm_d"][i:i + 1] + hand_over(first + 1, s["s5y"])
        s["ret"], s["ysg"], s["ysgb"] = _post1_fwd(s["y"], s["proj"], s["s5y"], d_skip, "post" + t)
        complete(first + 1, s["ysgb"])
        s["z"] = _matmul(s["ysgb"], wf["glu"][i], "nn", [F32], name="mm_glu" + t)[0]
        b_glu = w["b_glu"][i:i + 1] + hand_over(first + 2, s["z"])
        s["merged"] = _merge_fwd(s["z"], s["ysg"], s["proj"], s["ret"], b_glu, "merge" + t)
        s["x1"] = _matmul(s["merged"], wf["out"][i], "nn", [F32], add=xc, name="mm_out" + t)[0]
        s["h2"] = _rms_fwd(s["x1"], w["ln_ffn_g"][i:i + 1], "rms_ffn" + t)
        complete(first + 2, s["h2"])
        s["ab"] = _matmul(s["h2"], wf["gu"][i], "nn", [F32], name="mm_gu" + t)[0]
        if i + 1 < DEPTH:
            next_in = hand_over(first + GROUPS_PER_LAYER, s["ab"])
        s["f"] = _glu_fwd(s["ab"], "glu" + t)
        xc = _matmul(s["f"], wf["down"][i], "nn", [F32], add=s["x1"], name="mm_down" + t)[0]
        if i + 1 < DEPTH:
            complete(first + GROUPS_PER_LAYER, xc)
        saved.append(s)

    dx, dxb, loss_row, dg_final = _loss_stage(xc, target, w["ln_final_g"][None, :], "loss")
    loss = lax.psum(loss_row[0, 0], ("x", "y", "c"))

    g_small = {"ln_final_g": dg_final[0]}
    per_layer = {n: [None] * DEPTH for n in _SMALL if n != "ln_final_g"}
    dws, got, swaps, flights = {}, {}, {}, []

    def dw_mm(a, b, buf, i, name):
        dws[(buf, i)] = _matmul(a, b, "tn", [F32, BF16], name=name)

    def depart(g):
        keys = _reduce_keys(_REDUCE_GROUPS[g])
        own, land, ssem, rsem, tok = _swap_start(g, {k: dws[k][1] for k in keys})
        swaps[g] = (own, land, ssem, rsem)
        return tok[0:1, 0:1]

    def proceed(g, after):
        group = _REDUCE_GROUPS[g]
        got.update(_swap_wait(g, *swaps[g], after))
        partials = {}
        for p, layer in group:
            key = (_PIECES[p][1], layer)
            partials[key] = _chip_partial(_PIECES[p], dws[key][0], got[key], partials.get(key), c_arr,
                                          "chip_partial_%s_l%d" % (_PIECES[p][0], layer))
        pt, land, ssem, rsem, tok = _scatter_start(g, partials)
        flights.append((g, pt, land, ssem, rsem))
        return tok[0:1, 0:1]

    for i in reversed(range(DEPTH)):
        t = "_l%d" % i
        s = saved[i]
        g_ffn, g_mix, d_skip = w["ln_ffn_g"][i:i + 1], w["ln_mix_g"][i:i + 1], w["ssm_d"][i:i + 1]
        dw_mm(s["f"], dxb, "down", i, "dw_down" + t)
        df = _matmul(dxb, wf["down"][i], "nt", [F32], name="dx_down" + t)[0]
        if i == 0:
            g_ffn = g_ffn + proceed(0, df)
        dab = _glu_bwd(s["ab"], df, "glu_bwd" + t)
        dw_mm(s["h2"], dab, "gu", i, "dw_gu" + t)
        if i == 0:
            g_ffn = g_ffn + depart(1)
        dh2 = _matmul(dab, wf["gu"][i], "nt", [F32], name="dx_gu" + t)[0]
        if i == 0:
            g_ffn = g_ffn + proceed(1, dh2)
        dx1, dx1b, dg = _rms_bwd(s["x1"], dh2, dx, g_ffn, "rms_ffn_bwd" + t)
        per_layer["ln_ffn_g"][i] = dg[0]

        dw_mm(s["merged"], dx1b, "out", i, "dw_out" + t)
        dmerged = _matmul(dx1b, wf["out"][i], "nt", [F32], name="dx_out" + t)[0]
        dz, dys_part, dgs, db = _merge_bwd(s["z"], s["ysg"], s["proj"], s["ret"], dmerged, w["b_glu"][i:i + 1],
                                           "merge_bwd" + t)
        per_layer["b_glu"][i] = db[0]
        dw_mm(s["ysgb"], dz, "glu", i, "dw_glu" + t)
        if i == 0:
            d_skip = d_skip + depart(2)
        dys = _matmul(dz, wf["glu"][i], "nt", [F32], add=dys_part, name="dx_glu" + t)[0]
        if i == 0:
            d_skip = d_skip + proceed(2, dys)
        dy, dgg, dgr, ds5, du_part, dd = _post1_bwd(s["y"], s["proj"], s["s5y"], dmerged, dys,
                                                    d_skip, "post_bwd" + t)
        per_layer["ssm_d"][i] = dd[0]
        du, dbblk, dcblk, dlam = _s5_bwd(s["proj"], ds5, du_part, *s["s5"], "s5_bwd" + t)
        dlr, dli, dbr, dbi, dcr, dci = _s5_unpack(dbblk, dcblk, dlam)
        zeros = jnp.zeros_like(dlr)
        da_re, da_im, dlog_dt, db_re, db_im = s["disc_vjp"]((dlr, dli, zeros, zeros, dbr, dbi))
        for n, val in (("ssm_a_re", da_re), ("ssm_a_im", da_im), ("ssm_log_dt", dlog_dt), ("ssm_b_re", db_re),
                       ("ssm_b_im", db_im), ("ssm_c_re", dcr), ("ssm_c_im", dci)):
            per_layer[n][i] = val
        dqr, dkr, dv, dlg = _ret_bwd(s["qr"], s["kr"], s["proj"], dy, s["lg"], "ret_bwd" + t)
        per_layer["ret_log_gamma"][i] = dlg[:, :, 0].T
        dqkv = _rot_bwd(dqr, dkr, dv, cos, sin, "rot_bwd" + t)
        dproj = jnp.concatenate([dqkv, dgg, du, dgr, dgs], axis=1)
        dw_mm(s["h"], dproj, "in", i, "dw_in" + t)
        if i == 0:
            g_mix = g_mix + depart(3)
        dh = _matmul(dproj, wf["in"][i], "nt", [F32], name="dx_in" + t)[0]
        if i == 0:
            g_mix = g_mix + proceed(3, dh)
        dx, dxb, dg = _rms_bwd(s["x_in"], dh, dx1, g_mix, "rms_mix_bwd" + t)
        per_layer["ln_mix_g"][i] = dg[0]
        if i == DEPTH - 1:
            dxb = dxb + depart(0).astype(BF16)

    for n in per_layer:
        g_small[n] = jnp.stack(per_layer[n])
    me_arr = jnp.stack([2 * chip_idx + c_idx])
    g_mine = _flatten(g_small, _SMALL)
    rs_bufs, rs_ssem, rs_rsem, small_token = _exchange_start(_place_slot(g_mine, me_arr, True), g_mine,
                                                             "small_scatter_start")

    reduced = [None] * N_PIECES
    for g, pt, land, ssem, rsem in flights:
        landed = _scatter_wait(g, pt, land, ssem, rsem, small_token)
        for (p, layer), buf in zip(_REDUCE_GROUPS[g], landed):
            key = (_PIECES[p][1], layer)
            reduced[p] = _reduce_half(_PIECES[p], layer, dws[key][0], got[key], buf, reduced[p], idx_arr,
                                      "reduce_%s_l%d" % (_PIECES[p][0], layer))
    g_big = dict(zip([p[0] for p in _PIECES], _share_halves(reduced)))

    landed = _exchange_wait(rs_bufs, rs_ssem, rs_rsem, g_big[_BIG[-1]], "small_scatter_wait")
    ag_bufs, ag_ssem, ag_rsem, ag_token = _exchange_start(
        _place_slot(_sum_slots(landed, "sum_small"), me_arr, False), None, "small_gather_start")

    grads, delta, new_m, new_v = {}, {}, {}, {}
    for n in _BIG:
        d, r, cc = w[n].shape
        two_d = lambda a: a.reshape(d * r, cc)
        dl, mn, vn = _adamw(two_d(w[n]), two_d(g_big[n]), two_d(m[n]), two_d(v[n]), "adamw_" + n, after=ag_token)
        grads[n], delta[n], new_m[n], new_v[n] = g_big[n], dl.reshape(d, r, cc), mn.reshape(d, r, cc), vn.reshape(d, r, cc)

    all_done = sum(corner(delta[n]) for n in _BIG)
    gathered = _exchange_wait(ag_bufs, ag_ssem, ag_rsem, all_done, "small_gather_wait")
    g_flat = gathered.reshape(-1, LANES)
    grads.update(_unflatten(g_flat, w, _SMALL))
    tiny_rows = _flat_rows(w, _TINY)
    dl, mn, vn = _adamw(tiny_flat[0], g_flat[:tiny_rows], tiny_flat[1], tiny_flat[2], "adamw_tiny")
    for dst, flat in ((delta, dl), (new_m, mn), (new_v, vn)):
        dst.update(_unflatten(flat, w, _TINY))
    for n in _MID:
        delta[n], new_m[n], new_v[n] = _adamw_nd(w[n], grads[n], m[n], v[n], "adamw_" + n)
    return loss, dx, grads, delta, new_m, new_v


def kernel(x, ln_mix_g, w_in, ret_log_gamma, ssm_a_re, ssm_a_im, ssm_log_dt, ssm_b_re, ssm_b_im, ssm_c_re, ssm_c_im, ssm_d, w_glu, b_glu, w_out, ln_ffn_g, w_ffn_gate, w_ffn_up, w_ffn_down, ln_final_g, loss_target, m_ln_mix_g, m_w_in, m_ret_log_gamma, m_ssm_a_re, m_ssm_a_im, m_ssm_log_dt, m_ssm_b_re, m_ssm_b_im, m_ssm_c_re, m_ssm_c_im, m_ssm_d, m_w_glu, m_b_glu, m_w_out, m_ln_ffn_g, m_w_ffn_gate, m_w_ffn_up, m_w_ffn_down, m_ln_final_g, v_ln_mix_g, v_w_in, v_ret_log_gamma, v_ssm_a_re, v_ssm_a_im, v_ssm_log_dt, v_ssm_b_re, v_ssm_b_im, v_ssm_c_re, v_ssm_c_im, v_ssm_d, v_w_glu, v_b_glu, v_w_out, v_ln_ffn_g, v_w_ffn_gate, v_w_ffn_up, v_w_ffn_down, v_ln_final_g):
    given = dict(locals())
    w = {n: given[n] for n in _WEIGHTS}
    m = {n: given["m_" + n] for n in _WEIGHTS}
    v = {n: given["v_" + n] for n in _WEIGHTS}
    loss, dx, grads, delta, new_m, new_v = _step(w, m, v, x[0], loss_target[0])
    return (loss, dx[None], *[grads[n] for n in _WEIGHTS], *[delta[n] for n in _WEIGHTS],
            *[new_m[n] for n in _WEIGHTS], *[new_v[n] for n in _WEIGHTS])
```

```python
import functools
import math

import jax
import jax.numpy as jnp
from jax import lax
from jax.experimental import pallas as pl
from jax.experimental.pallas import tpu as pltpu

F32 = jnp.float32
BF16 = jnp.bfloat16

D_MODEL = 2048
DEPTH = 2
HEADS = 4
QK_DIM = 256
V_DIM = 512
QK_WIDTH = HEADS * QK_DIM
ROPE_BASE = 10000.0
GROUP = 16
N_GROUPS = D_MODEL // GROUP
N_STATE = 64
D_FF = 5632
IN_WIDTH = 2 * QK_WIDTH + 5 * D_MODEL
EPS = 1e-6
N_CHIPS = 4

ADAM_LR = 0.001
ADAM_B1 = 0.9
ADAM_B2 = 0.999
ADAM_EPS = 1e-08
ADAM_WD = 0.01
ADAM_STEP = 10

LANES = 128
SUBLANES = 8
VMEM_LIMIT = 56 * 1024 * 1024
SEGMENTS = SUBLANES
GROUPS_PER_TILE = LANES // GROUP
STATE_COLS = GROUPS_PER_TILE * N_STATE
N_TILES = D_MODEL // LANES
SCAN_UNROLL = 4

MESH = pl.DeviceIdType.MESH
HBM_SPEC = pl.BlockSpec(memory_space=pltpu.HBM)


def _params(sem=None, **kw):
    return pltpu.CompilerParams(dimension_semantics=sem, vmem_limit_bytes=VMEM_LIMIT, **kw)


def _tile(n, cap=1024):
    for t in (2048, 1024, 512, 256, 128, 64):
        if t <= cap and n % t == 0:
            return t
    raise ValueError(n)


def _rows_call(fn, rows, pars, row_outs, par_outs, *, tm, name):
    m = rows[0][0].shape[0]
    nr, npar, nro, npo = len(rows), len(pars), len(row_outs), len(par_outs)

    def body(*refs):
        rin = refs[:nr]
        pin = refs[nr:nr + npar]
        rout = refs[nr + npar:nr + npar + nro]
        pout = refs[nr + npar + nro:]
        res = fn(*[r[...] for r in rin], *[p[...] for p in pin])
        if not isinstance(res, (tuple, list)):
            res = (res,)
        for r, v in zip(rout, res[:nro]):
            r[...] = v.astype(r.dtype)
        if npo:
            @pl.when(pl.program_id(0) == 0)
            def _():
                for p in pout:
                    p[...] = jnp.zeros(p.shape, p.dtype)
            for p, v in zip(pout, res[nro:]):
                p[...] += v

    in_specs = [pl.BlockSpec((tm, w), functools.partial(lambda cb, i: (i, cb), cb)) for (_, w, cb) in rows]
    in_specs += [pl.BlockSpec(p.shape, lambda i: (0, 0)) for p in pars]
    out_specs = [pl.BlockSpec((tm, w), lambda i: (i, 0)) for (w, _) in row_outs]
    out_specs += [pl.BlockSpec(s, lambda i: (0, 0)) for s in par_outs]
    out_shape = [jax.ShapeDtypeStruct((m, w), dt) for (w, dt) in row_outs]
    out_shape += [jax.ShapeDtypeStruct(s, F32) for s in par_outs]
    res = pl.pallas_call(
        body, name=name, grid=(m // tm,), in_specs=in_specs, out_specs=out_specs, out_shape=out_shape,
        compiler_params=_params(("arbitrary",) if npo else ("parallel",)),
    )(*[a for (a, _, _) in rows], *pars)
    return res


def _f32(*vals):
    return [v.astype(F32) for v in vals]


def _f_rms(x, g):
    r = lax.rsqrt(jnp.mean(x * x, axis=-1, keepdims=True) + EPS)
    return x * r * g


def _rms_fwd(x, g, name):
    return _rows_call(lambda xv, gv: _f_rms(xv, gv), [(x, D_MODEL, 0)], [g], [(D_MODEL, BF16)], [],
                      tm=256, name=name)[0]


def _rms_bwd(x, dh, dres, g, name):
    def fn(xv, dhv, drv, gv):
        _, vjp = jax.vjp(_f_rms, xv, gv)
        dx, dg = vjp(dhv)
        dx = dx + drv
        return dx, dx, dg
    return _rows_call(fn, [(x, D_MODEL, 0), (dh, D_MODEL, 0), (dres, D_MODEL, 0)], [g],
                      [(D_MODEL, F32), (D_MODEL, BF16)], [(1, D_MODEL)], tm=256, name=name)


def _rot_heads(xv, cos, sin, scale):
    half = QK_DIM // 2
    outs = []
    for h in range(HEADS):
        x1 = xv[:, h * QK_DIM:h * QK_DIM + half]
        x2 = xv[:, h * QK_DIM + half:(h + 1) * QK_DIM]
        outs += [(x1 * cos - x2 * sin) * scale, (x1 * sin + x2 * cos) * scale]
    return jnp.concatenate(outs, axis=1)


def _rot_fwd(proj, cos, sin, name):
    def fn(q, k, cv, sv):
        return _rot_heads(q, cv, sv, 1.0), _rot_heads(k, cv, sv, QK_DIM ** -0.5)
    return _rows_call(fn, [(proj, QK_WIDTH, 0), (proj, QK_WIDTH, 1), (cos, LANES, 0), (sin, LANES, 0)], [],
                      [(QK_WIDTH, BF16), (QK_WIDTH, BF16)], [], tm=256, name=name)


def _rot_bwd(dqr, dkr, dv, cos, sin, name):
    def fn(dq, dk, dvv, cv, sv):
        return jnp.concatenate([_rot_heads(dq, cv, -sv, 1.0), _rot_heads(dk, cv, -sv, QK_DIM ** -0.5), dvv], axis=1)
    return _rows_call(fn, [(dqr, QK_WIDTH, 0), (dkr, QK_WIDTH, 0), (dv, D_MODEL, 0), (cos, LANES, 0), (sin, LANES, 0)],
                      [], [(2 * QK_WIDTH + D_MODEL, BF16)], [], tm=256, name=name)[0]


def _f_post1(y0, y1, y2, y3, g, gr, s5, u, dsk):
    yn = [yh * lax.rsqrt(jnp.mean(yh * yh, axis=-1, keepdims=True) + EPS) for yh in (y0, y1, y2, y3)]
    ret = jax.nn.sigmoid(gr) * (jax.nn.silu(g) * jnp.concatenate(yn, axis=1))
    ysg = jax.nn.gelu(s5 + dsk * u)
    return ret, ysg


def _post1_rows(y, proj, s5y):
    rows = [(y, V_DIM, h) for h in range(HEADS)]
    rows += [(proj, D_MODEL, 2), (proj, D_MODEL, 4), (s5y, D_MODEL, 0), (proj, D_MODEL, 3)]
    return rows


def _post1_fwd(y, proj, s5y, dsk, name):
    def fn(*vals):
        ret, ysg = _f_post1(*vals)
        return ret, ysg, ysg
    return _rows_call(fn, _post1_rows(y, proj, s5y), [dsk],
                      [(D_MODEL, F32), (D_MODEL, F32), (D_MODEL, BF16)], [], tm=128, name=name)


def _post1_bwd(y, proj, s5y, dret, dys, dsk, name):
    def fn(*vals):
        prim = vals[:8] + (vals[10],)
        _, vjp = jax.vjp(_f_post1, *prim)
        gy0, gy1, gy2, gy3, gg, ggr, gs5, gu, gd = vjp((vals[8], vals[9]))
        return jnp.concatenate([gy0, gy1, gy2, gy3], axis=1), gg, ggr, gs5, gu, gd
    rows = _post1_rows(y, proj, s5y) + [(dret, D_MODEL, 0), (dys, D_MODEL, 0)]
    return _rows_call(fn, rows, [dsk],
                      [(D_MODEL, BF16), (D_MODEL, BF16), (D_MODEL, BF16), (D_MODEL, F32), (D_MODEL, F32)],
                      [(1, D_MODEL)], tm=128, name=name)


def _f_merge(z, ysg, gs, ret, b):
    return ret + jax.nn.sigmoid(gs) * (ysg * jax.nn.sigmoid(z + b))


def _merge_fwd(z, ysg, proj, ret, b, name):
    return _rows_call(_f_merge, [(z, D_MODEL, 0), (ysg, D_MODEL, 0), (proj, D_MODEL, 5), (ret, D_MODEL, 0)], [b],
                      [(D_MODEL, BF16)], [], tm=128, name=name)[0]


def _merge_bwd(z, ysg, proj, ret, dm, b, name):
    def fn(zv, yv, gv, rv, dmv, bv):
        _, vjp = jax.vjp(_f_merge, zv, yv, gv, rv, bv)
        gz, gy, gg, _, gb = vjp(dmv)
        return gz, gy, gg, gb
    rows = [(z, D_MODEL, 0), (ysg, D_MODEL, 0), (proj, D_MODEL, 5), (ret, D_MODEL, 0), (dm, D_MODEL, 0)]
    return _rows_call(fn, rows, [b], [(D_MODEL, BF16), (D_MODEL, F32), (D_MODEL, BF16)], [(1, D_MODEL)],
                      tm=128, name=name)


def _f_glu(a, b):
    return jax.nn.silu(a) * b


def _glu_fwd(ab, name):
    return _rows_call(_f_glu, [(ab, D_FF, 0), (ab, D_FF, 1)], [], [(D_FF, BF16)], [], tm=128, name=name)[0]


def _glu_bwd(ab, df, name):
    def fn(a, b, d):
        _, vjp = jax.vjp(_f_glu, a, b)
        ga, gb = vjp(d)
        return jnp.concatenate([ga, gb], axis=1)
    return _rows_call(fn, [(ab, D_FF, 0), (ab, D_FF, 1), (df, D_FF, 0)], [], [(2 * D_FF, BF16)], [],
                      tm=128, name=name)[0]


def _loss_stage(x, tgt, g, name):
    def fn(xv, tv, gv):
        def lf(xx, gg):
            err = _f_rms(xx, gg) - tv
            row = jnp.mean(err * err, axis=-1, keepdims=True)
            return 0.5 * jnp.sum(row, axis=0, keepdims=True)
        l, vjp = jax.vjp(lf, xv, gv)
        dx, dg = vjp(jnp.ones((1, 1), F32))
        return dx, dx, jnp.broadcast_to(l, (1, LANES)), dg
    return _rows_call(fn, [(x, D_MODEL, 0), (tgt, D_MODEL, 0)], [g], [(D_MODEL, F32), (D_MODEL, BF16)],
                      [(1, LANES), (1, D_MODEL)], tm=256, name=name)


def _adam_math(wv, gv, mv, vv):
    mn = ADAM_B1 * mv + (1.0 - ADAM_B1) * gv
    vn = ADAM_B2 * vv + (1.0 - ADAM_B2) * (gv * gv)
    m_hat = mn / (1.0 - ADAM_B1 ** ADAM_STEP)
    v_hat = vn / (1.0 - ADAM_B2 ** ADAM_STEP)
    delta = -ADAM_LR * (m_hat / (jnp.sqrt(v_hat) + ADAM_EPS) + ADAM_WD * wv)
    return delta, mn, vn


def _adamw(w, g, m, v, name, after=None):
    rows, cols = w.shape
    tm = _tile(rows, 128 if cols > D_FF // N_CHIPS else (256 if cols > LANES else 512))
    fn = _adam_math if after is None else (lambda wv, gv, mv, vv, _: _adam_math(wv, gv, mv, vv))
    return _rows_call(fn, [(w, cols, 0), (g, cols, 0), (m, cols, 0), (v, cols, 0)], [] if after is None else [after],
                      [(cols, F32)] * 3, [], tm=tm, name=name)


def _adamw_nd(w, g, m, v, name):
    shape = w.shape
    lead = math.prod(shape[:-2])
    blk = (lead // 8,) + shape[-2:]
    three_d = lambda a: a.reshape((lead,) + shape[-2:])

    def body(w_ref, g_ref, m_ref, v_ref, d_ref, mn_ref, vn_ref):
        d_ref[...], mn_ref[...], vn_ref[...] = _adam_math(w_ref[...], g_ref[...], m_ref[...], v_ref[...])

    spec = pl.BlockSpec(blk, lambda i: (i, 0, 0))
    res = pl.pallas_call(
        body, name=name, grid=(8,), in_specs=[spec] * 4, out_specs=[spec] * 3,
        out_shape=[jax.ShapeDtypeStruct((lead,) + shape[-2:], F32)] * 3,
        compiler_params=_params(("parallel",)),
    )(three_d(w), three_d(g), three_d(m), three_d(v))
    return [r.reshape(shape) for r in res]


MATMUL_VMEM_BUDGET = 44 * 1024 * 1024


def _matmul_tiles(m, n, k, out_bytes, has_add):
    if k > 2048:
        return _tile(m, 1024), _tile(n, 1024), _tile(k, 1024)
    tm, tn, tk = _tile(m, 2048), _tile(n, 1024), k

    def footprint():
        acc = 4 * tm * tn if k // tk > 1 else 0
        return 2 * 2 * (tm * tk + tk * tn) + 2 * (out_bytes + 4 * has_add) * tm * tn + acc

    while footprint() > MATMUL_VMEM_BUDGET:
        if tn > 512 and n % (tn // 2) == 0:
            tn //= 2
        elif tk > 512 and k % (tk // 2) == 0:
            tk //= 2
        else:
            tm //= 2
    return tm, tn, tk


def _matmul(a, b, mode, out_dtypes, *, name, add=None):
    if mode == "nn":
        (m, k), (_, n) = a.shape, b.shape
    elif mode == "nt":
        (m, k), (n, _) = a.shape, b.shape
    else:
        (k, m), (_, n) = a.shape, b.shape
    n_out = len(out_dtypes)
    has_add = add is not None
    tm, tn, tk = _matmul_tiles(m, n, k, sum(jnp.dtype(dt).itemsize for dt in out_dtypes), has_add)
    nk = k // tk
    if mode == "nn":
        a_spec = pl.BlockSpec((tm, tk), lambda i, j, kk: (i, kk))
        b_spec = pl.BlockSpec((tk, tn), lambda i, j, kk: (kk, j))
        dims = (((1,), (0,)), ((), ()))
    elif mode == "nt":
        a_spec = pl.BlockSpec((tm, tk), lambda i, j, kk: (i, kk))
        b_spec = pl.BlockSpec((tn, tk), lambda i, j, kk: (j, kk))
        dims = (((1,), (1,)), ((), ()))
    else:
        a_spec = pl.BlockSpec((tk, tm), lambda i, j, kk: (kk, i))
        b_spec = pl.BlockSpec((tk, tn), lambda i, j, kk: (kk, j))
        dims = (((0,), (0,)), ((), ()))

    def body(*refs):
        a_ref, b_ref = refs[0], refs[1]
        add_ref = refs[2] if has_add else None
        outs = refs[2 + has_add:2 + has_add + n_out]

        def finish(r):
            if has_add:
                r = r + add_ref[...]
            for o in outs:
                o[...] = r.astype(o.dtype)

        if nk == 1:
            finish(lax.dot_general(a_ref[...], b_ref[...], dims, preferred_element_type=F32))
            return
        acc = refs[-1]
        kk = pl.program_id(2)

        @pl.when(kk == 0)
        def _():
            acc[...] = jnp.zeros(acc.shape, F32)

        acc[...] += lax.dot_general(a_ref[...], b_ref[...], dims, preferred_element_type=F32)

        @pl.when(kk == nk - 1)
        def _():
            finish(acc[...])

    in_specs = [a_spec, b_spec]
    args = [a, b]
    if has_add:
        in_specs.append(pl.BlockSpec((tm, tn), lambda i, j, kk: (i, j)))
        args.append(add)
    return pl.pallas_call(
        body, name=name, grid=(m // tm, n // tn, nk), in_specs=in_specs,
        out_specs=[pl.BlockSpec((tm, tn), lambda i, j, kk: (i, j))] * n_out,
        out_shape=[jax.ShapeDtypeStruct((m, n), dt) for dt in out_dtypes],
        scratch_shapes=[pltpu.VMEM((tm, tn), F32)] if nk > 1 else [],
        compiler_params=_params(("parallel", "parallel", "arbitrary")),
    )(*args)


RET_TQ = 512


def _decay(lg_ref, i, tq, seq):
    n_idx = i * tq + lax.broadcasted_iota(jnp.int32, (tq, seq), 0)
    m_idx = lax.broadcasted_iota(jnp.int32, (tq, seq), 1)
    diff = (n_idx - m_idx).astype(F32)
    lgf = lg_ref[0, 0:1, 0:1]
    lgb = lg_ref[0, 1:2, 0:1]
    causal = diff >= 0
    return jnp.exp(jnp.where(causal, lgf * diff, -lgb * diff)), diff, causal


_NT = (((1,), (1,)), ((), ()))
_TN = (((0,), (0,)), ((), ()))


def _ret_fwd(qr, kr, proj, lg, name):
    seq = qr.shape[0]
    tq = RET_TQ
    v_blk0 = (2 * QK_WIDTH) // V_DIM

    def body(q_ref, k_ref, v_ref, lg_ref, y_ref):
        i = pl.program_id(1)
        s = lax.dot_general(q_ref[...], k_ref[...], _NT, preferred_element_type=F32)
        dm, _, _ = _decay(lg_ref, i, tq, seq)
        p = (s * dm).astype(BF16)
        y_ref[...] = jnp.dot(p, v_ref[...].astype(BF16), preferred_element_type=F32)

    return pl.pallas_call(
        body, name=name, grid=(HEADS, seq // tq),
        in_specs=[pl.BlockSpec((tq, QK_DIM), lambda h, i: (i, h)),
                  pl.BlockSpec((seq, QK_DIM), lambda h, i: (0, h)),
                  pl.BlockSpec((seq, V_DIM), lambda h, i: (0, v_blk0 + h)),
                  pl.BlockSpec((1, 2, LANES), lambda h, i: (h, 0, 0))],
        out_specs=pl.BlockSpec((tq, V_DIM), lambda h, i: (i, h)),
        out_shape=jax.ShapeDtypeStruct((seq, HEADS * V_DIM), F32),
        compiler_params=_params(("parallel", "parallel")),
    )(qr, kr, proj, lg)


def _ret_bwd(qr, kr, proj, dy, lg, name):
    seq = qr.shape[0]
    tq = RET_TQ
    v_blk0 = (2 * QK_WIDTH) // V_DIM

    def body(q_ref, k_ref, v_ref, dy_ref, lg_ref, dq_ref, dk_ref, dv_ref, dlg_ref):
        i = pl.program_id(1)

        @pl.when(i == 0)
        def _():
            dk_ref[...] = jnp.zeros(dk_ref.shape, F32)
            dv_ref[...] = jnp.zeros(dv_ref.shape, F32)
            dlg_ref[...] = jnp.zeros(dlg_ref.shape, F32)

        q = q_ref[...]
        k = k_ref[...]
        vb = v_ref[...].astype(BF16)
        dyb = dy_ref[...]
        s = lax.dot_general(q, k, _NT, preferred_element_type=F32)
        dm, diff, causal = _decay(lg_ref, i, tq, seq)
        p = s * dm
        dp = lax.dot_general(dyb, vb, _NT, preferred_element_type=F32)
        dv_ref[...] += lax.dot_general(p.astype(BF16), dyb, _TN, preferred_element_type=F32)
        ds = (dp * dm).astype(BF16)
        dq_ref[...] = jnp.dot(ds, k, preferred_element_type=F32)
        dk_ref[...] += lax.dot_general(ds, q, _TN, preferred_element_type=F32)
        gd = dp * p * diff
        dlf = jnp.sum(jnp.sum(jnp.where(causal, gd, 0.0), axis=1, keepdims=True), axis=0, keepdims=True)
        dlb = jnp.sum(jnp.sum(jnp.where(causal, 0.0, -gd), axis=1, keepdims=True), axis=0, keepdims=True)
        row = lax.broadcasted_iota(jnp.int32, (2, LANES), 0)
        dlg_ref[0] += jnp.where(row == 0, dlf, dlb)

    return pl.pallas_call(
        body, name=name, grid=(HEADS, seq // tq),
        in_specs=[pl.BlockSpec((tq, QK_DIM), lambda h, i: (i, h)),
                  pl.BlockSpec((seq, QK_DIM), lambda h, i: (0, h)),
                  pl.BlockSpec((seq, V_DIM), lambda h, i: (0, v_blk0 + h)),
                  pl.BlockSpec((tq, V_DIM), lambda h, i: (i, h)),
                  pl.BlockSpec((1, 2, LANES), lambda h, i: (h, 0, 0))],
        out_specs=[pl.BlockSpec((tq, QK_DIM), lambda h, i: (i, h)),
                   pl.BlockSpec((seq, QK_DIM), lambda h, i: (0, h)),
                   pl.BlockSpec((seq, V_DIM), lambda h, i: (0, h)),
                   pl.BlockSpec((1, 2, LANES), lambda h, i: (h, 0, 0))],
        out_shape=[jax.ShapeDtypeStruct((seq, QK_WIDTH), F32), jax.ShapeDtypeStruct((seq, QK_WIDTH), F32),
                   jax.ShapeDtypeStruct((seq, HEADS * V_DIM), F32), jax.ShapeDtypeStruct((HEADS, 2, LANES), F32)],
        compiler_params=_params(("parallel", "arbitrary")),
    )(qr, kr, proj, dy, lg)


def _shift_rows(v, reverse):
    row = lax.broadcasted_iota(jnp.int32, v.shape, 0)
    if reverse:
        return jnp.where(row == SEGMENTS - 1, 0.0, pltpu.roll(v, SEGMENTS - 1, 0))
    return jnp.where(row == 0, 0.0, pltpu.roll(v, 1, 0))


def _slab(t):
    if isinstance(t, int):
        return pl.ds(t * SEGMENTS, SEGMENTS)
    return pl.ds(pl.multiple_of(t * SEGMENTS, SEGMENTS), SEGMENTS)


def _unrolled_loop(body, lo, hi, init):
    main = (hi - lo) // SCAN_UNROLL

    def unrolled(g, carry):
        for k in range(SCAN_UNROLL):
            carry = body(lo + g * SCAN_UNROLL + k, carry)
        return carry

    carry = lax.fori_loop(0, main, unrolled, init)
    for t in range(lo + main * SCAN_UNROLL, hi):
        carry = body(t, carry)
    return carry


def _scan(xr_ref, xi_ref, lam, reverse, conj):
    steps = xr_ref.shape[0] // SEGMENTS
    cols = xr_ref.shape[1]
    lr = jnp.broadcast_to(lam[0], (SEGMENTS, cols))
    li = jnp.broadcast_to(lam[1], (SEGMENTS, cols))
    lrt = jnp.broadcast_to(lam[2], (SEGMENTS, cols))
    lit = jnp.broadcast_to(lam[3], (SEGMENTS, cols))
    if conj:
        li, lit = -li, -lit
    zero = jnp.zeros((SEGMENTS, cols), F32)

    def rows_of(t):
        return _slab(steps - 1 - t if reverse else t)

    def advance(t, carry):
        sr, si = carry
        rows = rows_of(t)
        return lr * sr - li * si + xr_ref[rows, :], lr * si + li * sr + xi_ref[rows, :]

    def step(t, carry):
        nr, ni = advance(t, carry)
        rows = rows_of(t)
        xr_ref[rows, :] = nr
        xi_ref[rows, :] = ni
        return nr, ni

    def run(body, init):
        return _unrolled_loop(body, 0, steps, init)

    er, ei = run(advance, (zero, zero))
    cr, ci = zero, zero
    for _ in range(SEGMENTS - 1):
        tr = er + lrt * cr - lit * ci
        ti = ei + lrt * ci + lit * cr
        cr, ci = _shift_rows(tr, reverse), _shift_rows(ti, reverse)
    run(step, (cr, ci))


def _permute_in(dst_ref, src_ref):
    steps = src_ref.shape[0] // SEGMENTS
    for s in range(SEGMENTS):
        dst_ref[pl.ds(s, steps, stride=SEGMENTS), :] = src_ref[s * steps:(s + 1) * steps, :].astype(dst_ref.dtype)


def _unpermute(src_ref, s):
    steps = src_ref.shape[0] // SEGMENTS
    return src_ref[pl.ds(s, steps, stride=SEGMENTS), :]


def _s5_fwd(proj, bblk, cblk, lam, name):
    seq = proj.shape[0]
    u_blk0 = (2 * QK_WIDTH + 2 * D_MODEL) // LANES
    sc = STATE_COLS

    def body(u_ref, b_ref, c_ref, lam_ref, y_ref, up_ref, yp_ref, xr_ref, xi_ref):
        _permute_in(up_ref, u_ref)
        ub = up_ref[...].astype(BF16)
        for d in range(2):
            xr_ref[...] = jnp.dot(ub, b_ref[d, :, 0:sc], preferred_element_type=F32)
            xi_ref[...] = jnp.dot(ub, b_ref[d, :, sc:2 * sc], preferred_element_type=F32)
            lm = [lam_ref[d, r:r + 1, :] for r in range(4)]
            _scan(xr_ref, xi_ref, lm, reverse=(d == 1), conj=False)
            yd = (jnp.dot(xr_ref[...].astype(BF16), c_ref[d, 0:sc, :], preferred_element_type=F32)
                  + jnp.dot(xi_ref[...].astype(BF16), c_ref[d, sc:2 * sc, :], preferred_element_type=F32))
            if d == 0:
                yp_ref[...] = yd
            else:
                yp_ref[...] += yd
        steps = seq // SEGMENTS
        for s in range(SEGMENTS):
            y_ref[s * steps:(s + 1) * steps, :] = _unpermute(yp_ref, s)

    return pl.pallas_call(
        body, name=name, grid=(N_TILES,),
        in_specs=[pl.BlockSpec((seq, LANES), lambda j: (0, u_blk0 + j)),
                  pl.BlockSpec((2, None, LANES, 2 * sc), lambda j: (0, j, 0, 0)),
                  pl.BlockSpec((2, None, 2 * sc, LANES), lambda j: (0, j, 0, 0)),
                  pl.BlockSpec((2, None, 4, sc), lambda j: (0, j, 0, 0))],
        out_specs=pl.BlockSpec((seq, LANES), lambda j: (0, j)),
        out_shape=jax.ShapeDtypeStruct((seq, D_MODEL), F32),
        scratch_shapes=[pltpu.VMEM((seq, LANES), F32), pltpu.VMEM((seq, LANES), F32),
                        pltpu.VMEM((seq, sc), F32), pltpu.VMEM((seq, sc), F32)],
        compiler_params=_params(("parallel",)),
    )(proj, bblk, cblk, lam)


def _s5_bwd(proj, dy, du_part, bblk, cblk, lam, name):
    seq = proj.shape[0]
    u_blk0 = (2 * QK_WIDTH + 2 * D_MODEL) // LANES
    sc = STATE_COLS
    steps = seq // SEGMENTS

    def body(u_ref, dy_ref, dup_ref, b_ref, c_ref, lam_ref, du_ref, db_ref, dc_ref, dlam_ref,
             up_ref, dyp_ref, dua_ref, xr_ref, xi_ref, gr_ref, gi_ref):
        _permute_in(up_ref, u_ref)
        _permute_in(dyp_ref, dy_ref)
        ub = up_ref[...].astype(BF16)
        dyb = dyp_ref[...].astype(BF16)
        ubt = up_ref[...].T.astype(BF16)
        dybt = dyp_ref[...].T.astype(BF16)
        for d in range(2):
            reverse = d == 1
            xr_ref[...] = jnp.dot(ub, b_ref[d, :, 0:sc], preferred_element_type=F32)
            xi_ref[...] = jnp.dot(ub, b_ref[d, :, sc:2 * sc], preferred_element_type=F32)
            lm = [lam_ref[d, r:r + 1, :] for r in range(4)]
            _scan(xr_ref, xi_ref, lm, reverse=reverse, conj=False)
            xrb = xr_ref[...].astype(BF16)
            xib = xi_ref[...].astype(BF16)
            dc_ref[d, :, 0:sc] = jnp.dot(dybt, xrb, preferred_element_type=F32)
            dc_ref[d, :, sc:2 * sc] = jnp.dot(dybt, xib, preferred_element_type=F32)
            gr_ref[...] = lax.dot_general(dyb, c_ref[d, 0:sc, :], _NT, preferred_element_type=F32)
            gi_ref[...] = lax.dot_general(dyb, c_ref[d, sc:2 * sc, :], _NT, preferred_element_type=F32)
            _scan(gr_ref, gi_ref, lm, reverse=not reverse, conj=True)

            def acc_step(t, carry):
                ar, ai = carry
                prev = _slab(t + 1 if reverse else t - 1)
                pr = xr_ref[prev, :]
                pi = xi_ref[prev, :]
                zr = gr_ref[_slab(t), :]
                zi = gi_ref[_slab(t), :]
                return ar + zr * pr + zi * pi, ai + zi * pr - zr * pi

            zero = jnp.zeros((SEGMENTS, sc), F32)
            if reverse:
                ar, ai = _unrolled_loop(acc_step, 0, steps - 1, (zero, zero))
                edge = _slab(steps - 1)
                pr = _shift_rows(xr_ref[_slab(0), :], True)
                pi = _shift_rows(xi_ref[_slab(0), :], True)
            else:
                ar, ai = _unrolled_loop(acc_step, 1, steps, (zero, zero))
                edge = _slab(0)
                pr = _shift_rows(xr_ref[_slab(steps - 1), :], False)
                pi = _shift_rows(xi_ref[_slab(steps - 1), :], False)
            zr = gr_ref[edge, :]
            zi = gi_ref[edge, :]
            ar = ar + zr * pr + zi * pi
            ai = ai + zi * pr - zr * pi
            dlam_ref[d, 0:1, :] = jnp.sum(ar, axis=0, keepdims=True)
            dlam_ref[d, 1:2, :] = jnp.sum(ai, axis=0, keepdims=True)

            grb = gr_ref[...].astype(BF16)
            gib = gi_ref[...].astype(BF16)
            db_ref[d, :, 0:sc] = jnp.dot(ubt, grb, preferred_element_type=F32)
            db_ref[d, :, sc:2 * sc] = jnp.dot(ubt, gib, preferred_element_type=F32)
            dud = (lax.dot_general(grb, b_ref[d, :, 0:sc], _NT, preferred_element_type=F32)
                   + lax.dot_general(gib, b_ref[d, :, sc:2 * sc], _NT, preferred_element_type=F32))
            if d == 0:
                dua_ref[...] = dud
            else:
                dua_ref[...] += dud
        for s in range(SEGMENTS):
            rows = slice(s * steps, (s + 1) * steps)
            du_ref[rows, :] = (_unpermute(dua_ref, s) + dup_ref[rows, :]).astype(du_ref.dtype)

    return pl.pallas_call(
        body, name=name, grid=(N_TILES,),
        in_specs=[pl.BlockSpec((seq, LANES), lambda j: (0, u_blk0 + j)),
                  pl.BlockSpec((seq, LANES), lambda j: (0, j)),
                  pl.BlockSpec((seq, LANES), lambda j: (0, j)),
                  pl.BlockSpec((2, None, LANES, 2 * sc), lambda j: (0, j, 0, 0)),
                  pl.BlockSpec((2, None, 2 * sc, LANES), lambda j: (0, j, 0, 0)),
                  pl.BlockSpec((2, None, 4, sc), lambda j: (0, j, 0, 0))],
        out_specs=[pl.BlockSpec((seq, LANES), lambda j: (0, j)),
                   pl.BlockSpec((2, None, LANES, 2 * sc), lambda j: (0, j, 0, 0)),
                   pl.BlockSpec((2, None, LANES, 2 * sc), lambda j: (0, j, 0, 0)),
                   pl.BlockSpec((2, None, 2, sc), lambda j: (0, j, 0, 0))],
        out_shape=[jax.ShapeDtypeStruct((seq, D_MODEL), BF16),
                   jax.ShapeDtypeStruct((2, N_TILES, LANES, 2 * sc), F32),
                   jax.ShapeDtypeStruct((2, N_TILES, LANES, 2 * sc), F32),
                   jax.ShapeDtypeStruct((2, N_TILES, 2, sc), F32)],
        scratch_shapes=[pltpu.VMEM((seq, LANES), F32), pltpu.VMEM((seq, LANES), F32), pltpu.VMEM((seq, LANES), F32),
                        pltpu.VMEM((seq, sc), F32), pltpu.VMEM((seq, sc), F32),
                        pltpu.VMEM((seq, sc), F32), pltpu.VMEM((seq, sc), F32)],
        compiler_params=_params(("parallel",)),
    )(proj, dy, du_part, bblk, cblk, lam)


def _s5_discretize(a_re, a_im, log_dt, b_re, b_im, seg_len):
    dt = jnp.exp(log_dt)[..., None]
    e = jnp.exp(a_re * dt)
    lr, li = e * jnp.cos(a_im * dt), e * jnp.sin(a_im * dt)
    et = jnp.exp(a_re * dt * seg_len)
    lrt, lit = et * jnp.cos(a_im * dt * seg_len), et * jnp.sin(a_im * dt * seg_len)
    den = a_re * a_re + a_im * a_im
    qr = ((lr - 1.0) * a_re + li * a_im) / den
    qi = (li * a_re - (lr - 1.0) * a_im) / den
    br = qr[..., None] * b_re - qi[..., None] * b_im
    bi = qr[..., None] * b_im + qi[..., None] * b_re
    return lr, li, lrt, lit, br, bi


def _s5_pack(lr, li, lrt, lit, br, bi, c_re, c_im):
    eye = jnp.eye(GROUPS_PER_TILE, dtype=F32)

    def bd_b(b):
        b5 = b.reshape(2, N_TILES, GROUPS_PER_TILE, N_STATE, GROUP)
        return jnp.einsum("dtgph,gk->dtghkp", b5, eye).reshape(2, N_TILES, LANES, STATE_COLS)

    def bd_c(c):
        c5 = c.reshape(2, N_TILES, GROUPS_PER_TILE, GROUP, N_STATE)
        return jnp.einsum("dtghp,gk->dtkpgh", c5, eye).reshape(2, N_TILES, STATE_COLS, LANES)

    bblk = jnp.concatenate([bd_b(br), bd_b(bi)], axis=3)
    cblk = jnp.concatenate([bd_c(c_re), -bd_c(c_im)], axis=2)
    lam = jnp.stack([v.reshape(2, N_TILES, STATE_COLS) for v in (lr, li, lrt, lit)], axis=2)
    return bblk, cblk, lam


def _s5_unpack(dbblk, dcblk, dlam):
    eye = jnp.eye(GROUPS_PER_TILE, dtype=F32)

    def diag_b(d):
        d6 = d.reshape(2, N_TILES, GROUPS_PER_TILE, GROUP, GROUPS_PER_TILE, N_STATE)
        return jnp.einsum("dtghkp,gk->dtgph", d6, eye).reshape(2, N_GROUPS, N_STATE, GROUP)

    def diag_c(d):
        d6 = d.reshape(2, N_TILES, GROUPS_PER_TILE, GROUP, GROUPS_PER_TILE, N_STATE)
        return jnp.einsum("dtghkp,gk->dtghp", d6, eye).reshape(2, N_GROUPS, GROUP, N_STATE)

    dbr, dbi = diag_b(dbblk[..., :STATE_COLS]), diag_b(dbblk[..., STATE_COLS:])
    dcr, dci = diag_c(dcblk[..., :STATE_COLS]), -diag_c(dcblk[..., STATE_COLS:])
    dlr = dlam[:, :, 0, :].reshape(2, N_GROUPS, N_STATE)
    dli = dlam[:, :, 1, :].reshape(2, N_GROUPS, N_STATE)
    return dlr, dli, dbr, dbi, dcr, dci


def _pos():
    return lax.axis_index("x"), lax.axis_index("y"), lax.axis_index("c")


def _remote(src, dst, ssem, rsem, dev):
    return pltpu.make_async_remote_copy(src_ref=src, dst_ref=dst, send_sem=ssem, recv_sem=rsem,
                                        device_id=dev, device_id_type=MESH)


_PIECES = (
    ("w_in", "in", D_MODEL, IN_WIDTH // N_CHIPS, 0, IN_WIDTH // N_CHIPS, 0),
    ("w_glu", "glu", D_MODEL // N_CHIPS, D_MODEL, D_MODEL // N_CHIPS, 0, 0),
    ("w_out", "out", D_MODEL // N_CHIPS, D_MODEL, D_MODEL // N_CHIPS, 0, 0),
    ("w_ffn_gate", "gu", D_MODEL, D_FF // N_CHIPS, 0, D_FF // N_CHIPS, 0),
    ("w_ffn_up", "gu", D_MODEL, D_FF // N_CHIPS, 0, D_FF // N_CHIPS, D_FF),
    ("w_ffn_down", "down", D_FF // N_CHIPS, D_MODEL, D_FF // N_CHIPS, 0, 0),
)
_BUFFERS = (("in", D_MODEL, IN_WIDTH), ("glu", D_MODEL, D_MODEL), ("out", D_MODEL, D_MODEL),
            ("gu", D_MODEL, 2 * D_FF), ("down", D_FF, D_MODEL))
_BUF_INDEX = {name: t for t, (name, _, _) in enumerate(_BUFFERS)}
N_PIECES = len(_PIECES)
N_BUFFERS = len(_BUFFERS)


def _own_block(piece, tm):
    _, _, _, cs, rstep, cstep, coff = piece
    return lambda i, chip: (i + chip * (rstep // tm), coff // cs + chip * (cstep // cs))


def _cast_place(piece, w3, layer, prev, chip_arr, name):
    _, r, cc = w3.shape
    _, rf, cf = _BUFFERS[_BUF_INDEX[piece[1]]]
    tm = _tile(r, 256)
    own = _own_block(piece, tm)

    def body(s_ref, w_ref, *rest):
        rest[-1][...] = w_ref[...].astype(BF16)

    in_specs = [pl.BlockSpec((None, tm, cc), lambda i, s: (layer, i, 0))]
    args = [w3]
    aliases = {}
    if prev is not None:
        in_specs.append(pl.BlockSpec(memory_space=pl.ANY))
        args.append(prev)
        aliases = {2: 0}
    return pl.pallas_call(
        body, name=name,
        grid_spec=pltpu.PrefetchScalarGridSpec(
            num_scalar_prefetch=1, grid=(r // tm,), in_specs=in_specs,
            out_specs=pl.BlockSpec((tm, cc), lambda i, s: own(i, s[0]))),
        out_shape=jax.ShapeDtypeStruct((rf, cf), BF16), input_output_aliases=aliases,
        compiler_params=_params(("parallel",)),
    )(chip_arr, *args)


_GATHER_GROUPS = ((0, (0,)), (0, (1, 2)), (0, (3, 4, 5)), (1, (0,)), (1, (1, 2)), (1, (3, 4, 5)))
GROUPS_PER_LAYER = len(_GATHER_GROUPS) // DEPTH
_SPLIT_EFFECT = pltpu.SideEffectType.DATAFLOW_SIDE_EFFECTING
SEM_SPEC = pl.BlockSpec(memory_space=pltpu.SEMAPHORE)
BF16_ROWS = 2 * SUBLANES


def _group_keys(g):
    layer, pieces = _GATHER_GROUPS[g]
    keys = []
    for p in pieces:
        if (_PIECES[p][1], layer) not in keys:
            keys.append((_PIECES[p][1], layer))
    return keys


def _half_view(ref, piece, j, c):
    _, _, rs, cs, rstep, cstep, coff = piece
    half = rs // 2
    return ref.at[pl.ds(pl.multiple_of(j * rstep + c * half, BF16_ROWS), half), pl.ds(coff + j * cstep, cs)]


def _for_my_chip(fn):
    x, y, _ = _pos()
    for mine in range(N_CHIPS):
        pl.when(2 * x + y == mine)(functools.partial(fn, mine, [j for j in range(N_CHIPS) if j != mine]))


def _gather_start(groups, placed):
    keys = [k for g in groups for k in _group_keys(g)]
    nb, ng = len(keys), len(groups)

    def body(*refs):
        bufs = dict(zip(keys, refs[nb:2 * nb]))
        ssems = refs[2 * nb:2 * nb + ng]
        rsems = refs[2 * nb + ng:2 * nb + 2 * ng]
        token = refs[2 * nb + 2 * ng]
        _, _, c = _pos()

        def send(mine, others):
            for t, g in enumerate(groups):
                layer, pieces = _GATHER_GROUPS[g]
                for k, p in enumerate(pieces):
                    view = _half_view(bufs[(_PIECES[p][1], layer)], _PIECES[p], mine, c)
                    for j in others:
                        _remote(view, view, ssems[t].at[k * N_CHIPS + j], rsems[t].at[k * N_CHIPS + mine],
                                (j // 2, j % 2, c)).start()

        _for_my_chip(send)
        token[...] = jnp.zeros(token.shape, token.dtype)

    sems = [pltpu.SemaphoreType.DMA((N_CHIPS * len(_GATHER_GROUPS[g][1]),)) for g in groups]
    shapes = [jax.ShapeDtypeStruct(placed[k].shape, placed[k].dtype) for k in keys]
    res = pl.pallas_call(
        body, name="gather_start_g%d" % groups[0],
        in_specs=[HBM_SPEC] * nb,
        out_specs=[HBM_SPEC] * nb + [SEM_SPEC] * (2 * ng) + [pl.BlockSpec(memory_space=pltpu.VMEM)],
        out_shape=shapes + sems + sems + [jax.ShapeDtypeStruct((SUBLANES, LANES), F32)],
        input_output_aliases={t: t for t in range(nb)},
        compiler_params=_params(has_side_effects=_SPLIT_EFFECT),
    )(*[pltpu.with_memory_space_constraint(placed[k], pltpu.HBM) for k in keys])
    return (dict(zip(keys, res[:nb])), dict(zip(groups, res[nb:nb + ng])),
            dict(zip(groups, res[nb + ng:nb + 2 * ng])), res[nb + 2 * ng])


def _gather_wait(g, bufs, ssem, rsem, after):
    layer, pieces = _GATHER_GROUPS[g]
    keys = _group_keys(g)
    nb = len(keys)

    def body(*refs):
        ssem_ref, rsem_ref = refs[nb], refs[nb + 1]
        land = dict(zip(keys, refs[nb + 3:]))
        _, _, c = _pos()

        def wait(mine, others):
            for k, p in enumerate(pieces):
                ref = land[(_PIECES[p][1], layer)]
                for j in others:
                    cp = _remote(_half_view(ref, _PIECES[p], mine, c), _half_view(ref, _PIECES[p], j, c),
                                 ssem_ref.at[k * N_CHIPS + j], rsem_ref.at[k * N_CHIPS + j], (j // 2, j % 2, c))
                    cp.wait_send()
                    cp.wait_recv()

        _for_my_chip(wait)

    return pl.pallas_call(
        body, name="gather_wait_g%d" % g,
        in_specs=[HBM_SPEC] * nb + [SEM_SPEC, SEM_SPEC, pl.BlockSpec(memory_space=pl.ANY)],
        out_specs=[HBM_SPEC] * nb,
        out_shape=[jax.ShapeDtypeStruct(a.shape, a.dtype) for a in bufs],
        input_output_aliases={t: t for t in range(nb)},
        compiler_params=_params(has_side_effects=_SPLIT_EFFECT),
    )(*bufs, ssem, rsem, after)


def _gather_forward(g, bufs):
    layer, pieces = _GATHER_GROUPS[g]
    keys = _group_keys(g)
    nb = len(keys)

    def body(*refs):
        land = dict(zip(keys, refs[nb:2 * nb]))
        ssem, rsem = refs[2 * nb:]
        x, y, c = _pos()

        def forward(mine, others):
            cps = []
            for k, p in enumerate(pieces):
                ref = land[(_PIECES[p][1], layer)]
                for j in others:
                    view = _half_view(ref, _PIECES[p], j, c)
                    cp = _remote(view, view, ssem.at[k * N_CHIPS + j], rsem.at[k * N_CHIPS + j], (x, y, 1 - c))
                    cp.start()
                    cps.append(cp)
            for k, p in enumerate(pieces):
                ref = land[(_PIECES[p][1], layer)]
                for j in others:
                    view = _half_view(ref, _PIECES[p], j, 1 - c)
                    _remote(view, view, ssem.at[k * N_CHIPS + j], rsem.at[k * N_CHIPS + j], (x, y, 1 - c)).wait_recv()
            for cp in cps:
                cp.wait_send()

        _for_my_chip(forward)

    nsem = N_CHIPS * len(pieces)
    return pl.pallas_call(
        body, name="gather_forward_g%d" % g,
        in_specs=[HBM_SPEC] * nb, out_specs=[HBM_SPEC] * nb,
        out_shape=[jax.ShapeDtypeStruct(a.shape, a.dtype) for a in bufs],
        input_output_aliases={t: t for t in range(nb)},
        scratch_shapes=[pltpu.SemaphoreType.DMA((nsem,)), pltpu.SemaphoreType.DMA((nsem,))],
        compiler_params=_params(has_side_effects=True),
    )(*bufs)


def _forward_start(g, bufs):
    layer, pieces = _GATHER_GROUPS[g]
    keys = _group_keys(g)
    nb = len(keys)

    def body(*refs):
        land = dict(zip(keys, refs[nb:2 * nb]))
        ssem, rsem, token = refs[2 * nb:]
        x, y, c = _pos()

        def forward(mine, others):
            for k, p in enumerate(pieces):
                for j in others:
                    view = _half_view(land[(_PIECES[p][1], layer)], _PIECES[p], j, c)
                    _remote(view, view, ssem.at[k * N_CHIPS + j], rsem.at[k * N_CHIPS + j], (x, y, 1 - c)).start()

        _for_my_chip(forward)
        token[...] = jnp.zeros(token.shape, token.dtype)

    sem = pltpu.SemaphoreType.DMA((N_CHIPS * len(pieces),))
    res = pl.pallas_call(
        body, name="forward_start_g%d" % g,
        in_specs=[HBM_SPEC] * nb,
        out_specs=[HBM_SPEC] * nb + [SEM_SPEC, SEM_SPEC, pl.BlockSpec(memory_space=pltpu.VMEM)],
        out_shape=[jax.ShapeDtypeStruct(a.shape, a.dtype) for a in bufs]
        + [sem, sem, jax.ShapeDtypeStruct((SUBLANES, LANES), F32)],
        input_output_aliases={t: t for t in range(nb)},
        compiler_params=_params(has_side_effects=_SPLIT_EFFECT),
    )(*bufs)
    return list(res[:nb]), res[nb], res[nb + 1], res[nb + 2]


def _forward_wait(g, bufs, ssem, rsem, after):
    layer, pieces = _GATHER_GROUPS[g]
    keys = _group_keys(g)
    nb = len(keys)

    def body(*refs):
        ssem_ref, rsem_ref = refs[nb], refs[nb + 1]
        land = dict(zip(keys, refs[nb + 3:]))
        x, y, c = _pos()

        def wait(mine, others):
            for k, p in enumerate(pieces):
                ref = land[(_PIECES[p][1], layer)]
                for j in others:
                    cp = _remote(_half_view(ref, _PIECES[p], j, c), _half_view(ref, _PIECES[p], j, 1 - c),
                                 ssem_ref.at[k * N_CHIPS + j], rsem_ref.at[k * N_CHIPS + j], (x, y, 1 - c))
                    cp.wait_send()
                    cp.wait_recv()

        _for_my_chip(wait)

    return pl.pallas_call(
        body, name="forward_wait_g%d" % g,
        in_specs=[HBM_SPEC] * nb + [SEM_SPEC, SEM_SPEC, pl.BlockSpec(memory_space=pl.ANY)],
        out_specs=[HBM_SPEC] * nb,
        out_shape=[jax.ShapeDtypeStruct(a.shape, a.dtype) for a in bufs],
        input_output_aliases={t: t for t in range(nb)},
        compiler_params=_params(has_side_effects=_SPLIT_EFFECT),
    )(*bufs, ssem, rsem, after)


_REDUCE_GROUPS = (
    ((5, 1), (3, 1), (4, 1), (2, 1), (1, 1), (0, 1)),
    ((5, 0), (3, 0), (4, 0)),
    ((2, 0), (1, 0)),
    ((0, 0),),
)


def _reduce_keys(group):
    keys = []
    for p, layer in group:
        if (_PIECES[p][1], layer) not in keys:
            keys.append((_PIECES[p][1], layer))
    return keys


def _half_block(piece, tm):
    _, _, rs, cs, rstep, cstep, coff = piece
    return lambda i, j, c: (j * (rstep // tm) + c * (rs // 2 // tm) + i, coff // cs + j * (cstep // cs))


def _swap_start(g, dwb):
    group = _REDUCE_GROUPS[g]
    keys = _reduce_keys(group)
    nk = len(keys)

    def body(*refs):
        src = dict(zip(keys, refs[nk:2 * nk]))
        dst = dict(zip(keys, refs[2 * nk:3 * nk]))
        ssem, rsem, token = refs[3 * nk:]
        x, y, c = _pos()
        for k, (p, layer) in enumerate(group):
            key = (_PIECES[p][1], layer)
            for j in range(N_CHIPS):
                _remote(_half_view(src[key], _PIECES[p], j, 1 - c), _half_view(dst[key], _PIECES[p], j, 1 - c),
                        ssem.at[k * N_CHIPS + j], rsem.at[k * N_CHIPS + j], (x, y, 1 - c)).start()
        token[...] = jnp.zeros(token.shape, token.dtype)

    sem = pltpu.SemaphoreType.DMA((N_CHIPS * len(group),))
    shapes = [jax.ShapeDtypeStruct(dwb[k].shape, BF16) for k in keys]
    res = pl.pallas_call(
        body, name="swap_start_g%d" % g,
        in_specs=[HBM_SPEC] * nk,
        out_specs=[HBM_SPEC] * (2 * nk) + [SEM_SPEC, SEM_SPEC, pl.BlockSpec(memory_space=pltpu.VMEM)],
        out_shape=shapes + shapes + [sem, sem, jax.ShapeDtypeStruct((SUBLANES, LANES), F32)],
        input_output_aliases={t: t for t in range(nk)},
        compiler_params=_params(has_side_effects=_SPLIT_EFFECT),
    )(*[pltpu.with_memory_space_constraint(dwb[k], pltpu.HBM) for k in keys])
    return list(res[:nk]), list(res[nk:2 * nk]), res[2 * nk], res[2 * nk + 1], res[2 * nk + 2]


def _swap_wait(g, own, land, ssem, rsem, after):
    group = _REDUCE_GROUPS[g]
    keys = _reduce_keys(group)
    nk = len(keys)

    def body(*refs):
        ssem_ref, rsem_ref = refs[2 * nk], refs[2 * nk + 1]
        src = dict(zip(keys, refs[2 * nk + 3:3 * nk + 3]))
        dst = dict(zip(keys, refs[3 * nk + 3:]))
        x, y, c = _pos()
        for k, (p, layer) in enumerate(group):
            key = (_PIECES[p][1], layer)
            for j in range(N_CHIPS):
                cp = _remote(_half_view(src[key], _PIECES[p], j, 1 - c), _half_view(dst[key], _PIECES[p], j, c),
                             ssem_ref.at[k * N_CHIPS + j], rsem_ref.at[k * N_CHIPS + j], (x, y, 1 - c))
                cp.wait_send()
                cp.wait_recv()

    res = pl.pallas_call(
        body, name="swap_wait_g%d" % g,
        in_specs=[HBM_SPEC] * (2 * nk) + [SEM_SPEC, SEM_SPEC, pl.BlockSpec(memory_space=pl.ANY)],
        out_specs=[HBM_SPEC] * (2 * nk),
        out_shape=[jax.ShapeDtypeStruct(a.shape, a.dtype) for a in list(own) + list(land)],
        input_output_aliases={t: t for t in range(2 * nk)},
        compiler_params=_params(has_side_effects=_SPLIT_EFFECT),
    )(*own, *land, ssem, rsem, after)
    return dict(zip(keys, res[nk:]))


def _chip_partial(piece, dw, got, prev, c_arr, name):
    _, _, rs, cs, _, _, _ = piece
    half = rs // 2
    tm = _tile(half, 256)
    blk = _half_block(piece, tm)

    def body(s_ref, dw_ref, got_ref, *rest):
        rest[-1][...] = (dw_ref[...] + got_ref[...].astype(F32)).astype(BF16)

    spec = pl.BlockSpec((tm, cs), lambda j, i, s: blk(i, j, s[0]))
    in_specs = [spec, spec]
    args = [dw, got]
    aliases = {}
    if prev is not None:
        in_specs.append(pl.BlockSpec(memory_space=pl.ANY))
        args.append(prev)
        aliases = {3: 0}
    return pl.pallas_call(
        body, name=name,
        grid_spec=pltpu.PrefetchScalarGridSpec(
            num_scalar_prefetch=1, grid=(N_CHIPS, half // tm), in_specs=in_specs, out_specs=spec),
        out_shape=jax.ShapeDtypeStruct(dw.shape, BF16), input_output_aliases=aliases,
        compiler_params=_params(("parallel", "parallel")),
    )(c_arr, *args)


def _scatter_start(g, partials):
    group = _REDUCE_GROUPS[g]
    keys = _reduce_keys(group)
    nk, n = len(keys), len(group)

    def body(*refs):
        pt = dict(zip(keys, refs[nk:2 * nk]))
        land = refs[2 * nk:2 * nk + n]
        ssem, rsem, token = refs[2 * nk + n:]
        _, _, c = _pos()

        def send(mine, others):
            for k, (p, layer) in enumerate(group):
                for j in others:
                    _remote(_half_view(pt[(_PIECES[p][1], layer)], _PIECES[p], j, c), land[k].at[mine],
                            ssem.at[k * N_CHIPS + j], rsem.at[k * N_CHIPS + mine], (j // 2, j % 2, c)).start()

        _for_my_chip(send)
        token[...] = jnp.zeros(token.shape, token.dtype)

    sem = pltpu.SemaphoreType.DMA((N_CHIPS * n,))
    res = pl.pallas_call(
        body, name="scatter_start_g%d" % g,
        in_specs=[HBM_SPEC] * nk,
        out_specs=[HBM_SPEC] * (nk + n) + [SEM_SPEC, SEM_SPEC, pl.BlockSpec(memory_space=pltpu.VMEM)],
        out_shape=([jax.ShapeDtypeStruct(partials[k].shape, BF16) for k in keys]
                   + [jax.ShapeDtypeStruct((N_CHIPS, _PIECES[p][2] // 2, _PIECES[p][3]), BF16) for p, _ in group]
                   + [sem, sem, jax.ShapeDtypeStruct((SUBLANES, LANES), F32)]),
        input_output_aliases={t: t for t in range(nk)},
        compiler_params=_params(has_side_effects=_SPLIT_EFFECT),
    )(*[pltpu.with_memory_space_constraint(partials[k], pltpu.HBM) for k in keys])
    return list(res[:nk]), list(res[nk:nk + n]), res[nk + n], res[nk + n + 1], res[nk + n + 2]


def _scatter_wait(g, partials, land, ssem, rsem, after):
    group = _REDUCE_GROUPS[g]
    keys = _reduce_keys(group)
    nk, n = len(keys), len(group)

    def body(*refs):
        ssem_ref, rsem_ref = refs[nk + n], refs[nk + n + 1]
        pt = dict(zip(keys, refs[nk + n + 3:2 * nk + n + 3]))
        land_ref = refs[2 * nk + n + 3:]
        _, _, c = _pos()

        def wait(mine, others):
            for k, (p, layer) in enumerate(group):
                for j in others:
                    cp = _remote(_half_view(pt[(_PIECES[p][1], layer)], _PIECES[p], j, c), land_ref[k].at[j],
                                 ssem_ref.at[k * N_CHIPS + j], rsem_ref.at[k * N_CHIPS + j], (j // 2, j % 2, c))
                    cp.wait_send()
                    cp.wait_recv()

        _for_my_chip(wait)

    res = pl.pallas_call(
        body, name="scatter_wait_g%d" % g,
        in_specs=[HBM_SPEC] * (nk + n) + [SEM_SPEC, SEM_SPEC, pl.BlockSpec(memory_space=pl.ANY)],
        out_specs=[HBM_SPEC] * (nk + n),
        out_shape=[jax.ShapeDtypeStruct(a.shape, a.dtype) for a in list(partials) + list(land)],
        input_output_aliases={t: t for t in range(nk + n)},
        compiler_params=_params(has_side_effects=_SPLIT_EFFECT),
    )(*partials, *land, ssem, rsem, after)
    return list(res[nk:])


def _reduce_half(piece, layer, dw, got, land, prev, idx, name):
    _, _, rs, cs, _, _, _ = piece
    half = rs // 2
    tm = _tile(half, 256)
    blk = _half_block(piece, tm)

    def body(s_ref, dw_ref, got_ref, r1, r2, r3, *rest):
        acc = dw_ref[...] + got_ref[...].astype(F32)
        for r in (r1, r2, r3):
            acc = acc + r[...].astype(F32)
        rest[-1][...] = acc

    def land_map(k):
        return lambda i, s: ((s[1] + k) % N_CHIPS, i, 0)

    own = pl.BlockSpec((tm, cs), lambda i, s: blk(i, s[1], s[0]))
    in_specs = [own, own] + [pl.BlockSpec((None, tm, cs), land_map(k)) for k in (1, 2, 3)]
    args = [dw, got, land, land, land]
    aliases = {}
    if prev is not None:
        in_specs.append(pl.BlockSpec(memory_space=pl.ANY))
        args.append(prev)
        aliases = {6: 0}
    return pl.pallas_call(
        body, name=name,
        grid_spec=pltpu.PrefetchScalarGridSpec(
            num_scalar_prefetch=1, grid=(half // tm,), in_specs=in_specs,
            out_specs=pl.BlockSpec((None, tm, cs), lambda i, s: (layer, s[0] * (half // tm) + i, 0))),
        out_shape=jax.ShapeDtypeStruct((DEPTH, rs, cs), F32), input_output_aliases=aliases,
        compiler_params=_params(("parallel",)),
    )(idx, *args)


def _share_halves(reduced):
    def body(*refs):
        buf = refs[N_PIECES:2 * N_PIECES]
        ssem, rsem = refs[2 * N_PIECES:]
        x, y, c = _pos()

        def half(p, layer, cc):
            rows = _PIECES[p][2] // 2
            return buf[p].at[layer, pl.ds(pl.multiple_of(cc * rows, SUBLANES), rows), :]

        pairs = [(p, layer) for p in range(N_PIECES) for layer in range(DEPTH)]
        rem = [_remote(half(p, layer, c), half(p, layer, c), ssem.at[k], rsem.at[k], (x, y, 1 - c))
               for k, (p, layer) in enumerate(pairs)]
        for cp in rem:
            cp.start()
        for k, (p, layer) in enumerate(pairs):
            rem[k].wait_send()
            _remote(half(p, layer, 1 - c), half(p, layer, 1 - c), ssem.at[k], rsem.at[k], (x, y, 1 - c)).wait_recv()

    nsem = N_PIECES * DEPTH
    return pl.pallas_call(
        body, name="share_halves",
        in_specs=[HBM_SPEC] * N_PIECES, out_specs=[HBM_SPEC] * N_PIECES,
        out_shape=[jax.ShapeDtypeStruct((DEPTH, p[2], p[3]), F32) for p in _PIECES],
        input_output_aliases={t: t for t in range(N_PIECES)},
        scratch_shapes=[pltpu.SemaphoreType.DMA((nsem,)), pltpu.SemaphoreType.DMA((nsem,))],
        compiler_params=_params(has_side_effects=True),
    )(*reduced)


N_DEV = 8


def _place_slot(v, me_arr, take_block):
    rows = v.shape[0] // N_DEV if take_block else v.shape[0]
    tm = _tile(rows, 512)
    steps = rows // tm

    def body(s_ref, v_ref, out_ref):
        out_ref[...] = v_ref[...]

    return pl.pallas_call(
        body, name="place_small_block" if take_block else "place_small_sum",
        grid_spec=pltpu.PrefetchScalarGridSpec(
            num_scalar_prefetch=1, grid=(steps,),
            in_specs=[pl.BlockSpec((tm, LANES), lambda i, s: (s[0] * steps * take_block + i, 0))],
            out_specs=pl.BlockSpec((None, tm, LANES), lambda i, s: (s[0], i, 0))),
        out_shape=jax.ShapeDtypeStruct((N_DEV, rows, LANES), F32),
        compiler_params=_params(("parallel",)),
    )(me_arr, v)


def _all_peers():
    x, y, c = _pos()
    flip = lambda v, f: 1 - v if f else v
    return (x, y, c), [(flip(x, a), flip(y, b), flip(c, d))
                       for a in (0, 1) for b in (0, 1) for d in (0, 1) if a or b or d]


def _slot_index(dev):
    return 4 * dev[0] + 2 * dev[1] + dev[2]


def _exchange_start(g, src, name):
    rows = g.shape[1]
    n_in = 1 if src is None else 2

    def body(*refs):
        g_ref = refs[n_in]
        src_ref = refs[n_in + 1] if src is not None else None
        ssem, rsem, token = refs[2 * n_in:]
        me, peers = _all_peers()
        for k, dev in enumerate(peers):
            if src is None:
                mine = g_ref.at[_slot_index(me)]
            else:
                mine = src_ref.at[pl.ds(pl.multiple_of(_slot_index(dev) * rows, SUBLANES), rows), :]
            _remote(mine, g_ref.at[_slot_index(me)], ssem.at[k], rsem.at[k], dev).start()
        token[...] = jnp.zeros(token.shape, token.dtype)

    sem = pltpu.SemaphoreType.DMA((N_DEV - 1,))
    args = [g] if src is None else [g, src]
    res = pl.pallas_call(
        body, name=name,
        in_specs=[HBM_SPEC] * n_in,
        out_specs=[HBM_SPEC] * n_in + [SEM_SPEC, SEM_SPEC, pl.BlockSpec(memory_space=pltpu.VMEM)],
        out_shape=[jax.ShapeDtypeStruct(a.shape, a.dtype) for a in args]
        + [sem, sem, jax.ShapeDtypeStruct((SUBLANES, LANES), F32)],
        input_output_aliases={t: t for t in range(n_in)},
        compiler_params=_params(has_side_effects=_SPLIT_EFFECT),
    )(*[pltpu.with_memory_space_constraint(a, pltpu.HBM) for a in args])
    return list(res[:n_in]), res[n_in], res[n_in + 1], res[n_in + 2]


def _exchange_wait(bufs, ssem, rsem, after, name):
    n_in = len(bufs)

    def body(*refs):
        ssem_ref, rsem_ref = refs[n_in], refs[n_in + 1]
        g_ref = refs[n_in + 3]
        me, peers = _all_peers()
        for k, dev in enumerate(peers):
            cp = _remote(g_ref.at[_slot_index(me)], g_ref.at[_slot_index(dev)], ssem_ref.at[k], rsem_ref.at[k], dev)
            cp.wait_send()
            cp.wait_recv()

    res = pl.pallas_call(
        body, name=name,
        in_specs=[HBM_SPEC] * n_in + [SEM_SPEC, SEM_SPEC, pl.BlockSpec(memory_space=pl.ANY)],
        out_specs=[HBM_SPEC] * n_in,
        out_shape=[jax.ShapeDtypeStruct(a.shape, a.dtype) for a in bufs],
        input_output_aliases={t: t for t in range(n_in)},
        compiler_params=_params(has_side_effects=_SPLIT_EFFECT),
    )(*bufs, ssem, rsem, after)
    return res[0]


def _sum_slots(g, name):
    n, rows, _ = g.shape
    tm = _tile(rows, 512)

    def body(g_ref, out_ref):
        acc = g_ref[0]
        for k in range(1, n):
            acc = acc + g_ref[k]
        out_ref[...] = acc

    return pl.pallas_call(
        body, name=name, grid=(rows // tm,),
        in_specs=[pl.BlockSpec((n, tm, LANES), lambda i: (0, i, 0))],
        out_specs=pl.BlockSpec((tm, LANES), lambda i: (i, 0)),
        out_shape=jax.ShapeDtypeStruct((rows, LANES), F32),
        compiler_params=_params(("parallel",)),
    )(g)


_TINY = ("ln_mix_g", "ret_log_gamma", "ssm_a_re", "ssm_a_im", "ssm_log_dt", "ssm_d", "b_glu", "ln_ffn_g", "ln_final_g")
_MID = ("ssm_b_re", "ssm_b_im", "ssm_c_re", "ssm_c_im")
_SMALL = _TINY + _MID
_FLAT_ALIGN = LANES * LANES
_FLAT_ROWS = 1024


def _flat_rows(like, names):
    rows = sum((math.prod(like[n].shape) + (-math.prod(like[n].shape)) % _FLAT_ALIGN) // LANES for n in names)
    return rows + (-rows) % _FLAT_ROWS


def _flatten(d, names):
    parts = []
    for n in names:
        f = d[n].reshape(-1)
        parts.append(jnp.pad(f, (0, (-f.shape[0]) % _FLAT_ALIGN)))
    total = sum(p.shape[0] for p in parts)
    parts.append(jnp.zeros(((-total) % (_FLAT_ROWS * LANES),), F32))
    return jnp.concatenate(parts).reshape(-1, LANES)


def _unflatten(flat, like, names):
    out, row = {}, 0
    for n in names:
        size = math.prod(like[n].shape)
        rows = (size + (-size) % _FLAT_ALIGN) // LANES
        part = lax.optimization_barrier(flat[row:row + rows])
        out[n] = part.reshape(-1)[:size].reshape(like[n].shape)
        row += rows
    return out


_BIG = ("w_in", "w_glu", "w_out", "w_ffn_gate", "w_ffn_up", "w_ffn_down")
_WEIGHTS = ("ln_mix_g", "w_in", "ret_log_gamma", "ssm_a_re", "ssm_a_im", "ssm_log_dt", "ssm_b_re", "ssm_b_im",
            "ssm_c_re", "ssm_c_im", "ssm_d", "w_glu", "b_glu", "w_out", "ln_ffn_g", "w_ffn_gate", "w_ffn_up",
            "w_ffn_down", "ln_final_g")


def _rope_tables(seq):
    half = QK_DIM // 2
    inv = 1.0 / (ROPE_BASE ** (jnp.arange(half, dtype=F32) / half))
    ang = jnp.arange(seq, dtype=F32)[:, None] * inv[None, :]
    return jnp.cos(ang), jnp.sin(ang)


def _step(w, m, v, x, target):
    seq = x.shape[0]
    seg_len = float(seq // SEGMENTS)
    c_idx = lax.axis_index("c").astype(jnp.int32)
    chip_idx = (2 * lax.axis_index("x") + lax.axis_index("y")).astype(jnp.int32)
    c_arr = jnp.stack([c_idx])
    idx_arr = jnp.stack([c_idx, chip_idx])

    chip_arr = jnp.stack([chip_idx])
    placed = {}

    def cast(pieces, layer):
        for p in pieces:
            key = (_PIECES[p][1], layer)
            placed[key] = _cast_place(_PIECES[p], w[_PIECES[p][0]], layer, placed.get(key), chip_arr,
                                      "cast_%s_l%d" % (_PIECES[p][0], layer))

    for layer, pieces in _GATHER_GROUPS:
        cast(pieces, layer)
    flying, ssems, rsems, token = _gather_start(list(range(len(_GATHER_GROUPS))), placed)
    wf = {b[0]: [None] * DEPTH for b in _BUFFERS}

    handing = {}

    def arrive(g, after):
        ks = _group_keys(g)
        landed = _gather_wait(g, [flying[k] for k in ks], ssems[g], rsems[g], after)
        for k, a in zip(ks, _gather_forward(g, landed)):
            wf[k[0]][k[1]] = a

    def hand_over(g, after):
        ks = _group_keys(g)
        landed = _gather_wait(g, [flying[k] for k in ks], ssems[g], rsems[g], after)
        bufs, fs, fr, tok = _forward_start(g, landed)
        handing[g] = (bufs, fs, fr)
        return tok[0:1, 0:1]

    def complete(g, after):
        for k, a in zip(_group_keys(g), _forward_wait(g, *handing[g], after)):
            wf[k[0]][k[1]] = a

    cos, sin = _rope_tables(seq)

    started = token[0, 0]
    s5_ops, s5_vjps = [], []
    for i in range(DEPTH):
        s5_raw = (w["ssm_a_re"][i] + started, w["ssm_a_im"][i], w["ssm_log_dt"][i], w["ssm_b_re"][i], w["ssm_b_im"][i])
        disc, disc_vjp = jax.vjp(functools.partial(_s5_discretize, seg_len=seg_len), *s5_raw)
        bblk, cblk, lam = _s5_pack(*disc, w["ssm_c_re"][i] + started, w["ssm_c_im"][i] + started)
        s5_ops.append((bblk.astype(BF16), cblk.astype(BF16), lam))
        s5_vjps.append(disc_vjp)
    tiny_flat = [_flatten({**d, "ln_final_g": d["ln_final_g"] + started}, _TINY) for d in (w, m, v)]
    corner = lambda a: a[(0,) * (a.ndim - 2)][0:1, 0:1].astype(F32)
    prepared = sum(corner(a) for ops in s5_ops for a in ops) + sum(corner(a) for a in tiny_flat) + corner(cos) + corner(sin)

    saved = []
    xc = x + token[0, 0]
    for i in range(DEPTH):
        t = "_l%d" % i
        s = {"x_in": xc}
        if i == 0:
            s["h"] = _rms_fwd(xc, w["ln_mix_g"][i:i + 1], "rms_mix" + t)
            arrive(0, prepared + corner(s["h"]))
        else:
            s["h"] = _rms_fwd(xc, w["ln_mix_g"][i:i + 1] + next_in, "rms_mix" + t)
        s["proj"] = _matmul(s["h"], wf["in"][i], "nn", [F32], name="mm_in" + t)[0]
        s["qr"], s["kr"] = _rot_fwd(s["proj"], cos, sin, "rot" + t)
        s["lg"] = jnp.broadcast_to(w["ret_log_gamma"][i].T[:, :, None], (HEADS, 2, LANES))
        s["y"] = _ret_fwd(s["qr"], s["kr"], s["proj"], s["lg"], "ret" + t)
        s["s5"], s["disc_vjp"] = s5_ops[i], s5_vjps[i]
        s["s5y"] = _s5_fwd(s["proj"], *s["s5"], "s5" + t)
        first = GROUPS_PER_LAYER * i
        d_skip = w["ssm_d"][i:i + 1] + hand_over(first + 1, s["s5y"])
        s["ret"], s["ysg"], s["ysgb"] = _post1_fwd(s["y"], s["proj"], s["s5y"], d_skip, "post" + t)
        complete(first + 1, s["ysgb"])
        s["z"] = _matmul(s["ysgb"], wf["glu"][i], "nn", [F32], name="mm_glu" + t)[0]
        b_glu = w["b_glu"][i:i + 1] + hand_over(first + 2, s["z"])
        s["merged"] = _merge_fwd(s["z"], s["ysg"], s["proj"], s["ret"], b_glu, "merge" + t)
        s["x1"] = _matmul(s["merged"], wf["out"][i], "nn", [F32], add=xc, name="mm_out" + t)[0]
        s["h2"] = _rms_fwd(s["x1"], w["ln_ffn_g"][i:i + 1], "rms_ffn" + t)
        complete(first + 2, s["h2"])
        s["ab"] = _matmul(s["h2"], wf["gu"][i], "nn", [F32], name="mm_gu" + t)[0]
        if i + 1 < DEPTH:
            next_in = hand_over(first + GROUPS_PER_LAYER, s["ab"])
        s["f"] = _glu_fwd(s["ab"], "glu" + t)
        xc = _matmul(s["f"], wf["down"][i], "nn", [F32], add=s["x1"], name="mm_down" + t)[0]
        if i + 1 < DEPTH:
            complete(first + GROUPS_PER_LAYER, xc)
        saved.append(s)

    dx, dxb, loss_row, dg_final = _loss_stage(xc, target, w["ln_final_g"][None, :], "loss")
    loss = lax.psum(loss_row[0, 0], ("x", "y", "c"))

    g_small = {"ln_final_g": dg_final[0]}
    per_layer = {n: [None] * DEPTH for n in _SMALL if n != "ln_final_g"}
    dws, got, swaps, flights = {}, {}, {}, []

    def dw_mm(a, b, buf, i, name):
        dws[(buf, i)] = _matmul(a, b, "tn", [F32, BF16], name=name)

    def depart(g):
        keys = _reduce_keys(_REDUCE_GROUPS[g])
        own, land, ssem, rsem, tok = _swap_start(g, {k: dws[k][1] for k in keys})
        swaps[g] = (own, land, ssem, rsem)
        return tok[0:1, 0:1]

    def proceed(g, after):
        group = _REDUCE_GROUPS[g]
        got.update(_swap_wait(g, *swaps[g], after))
        partials = {}
        for p, layer in group:
            key = (_PIECES[p][1], layer)
            partials[key] = _chip_partial(_PIECES[p], dws[key][0], got[key], partials.get(key), c_arr,
                                          "chip_partial_%s_l%d" % (_PIECES[p][0], layer))
        pt, land, ssem, rsem, tok = _scatter_start(g, partials)
        flights.append((g, pt, land, ssem, rsem))
        return tok[0:1, 0:1]

    for i in reversed(range(DEPTH)):
        t = "_l%d" % i
        s = saved[i]
        g_ffn, g_mix, d_skip = w["ln_ffn_g"][i:i + 1], w["ln_mix_g"][i:i + 1], w["ssm_d"][i:i + 1]
        dw_mm(s["f"], dxb, "down", i, "dw_down" + t)
        df = _matmul(dxb, wf["down"][i], "nt", [F32], name="dx_down" + t)[0]
        if i == 0:
            g_ffn = g_ffn + proceed(0, df)
        dab = _glu_bwd(s["ab"], df, "glu_bwd" + t)
        dw_mm(s["h2"], dab, "gu", i, "dw_gu" + t)
        if i == 0:
            g_ffn = g_ffn + depart(1)
        dh2 = _matmul(dab, wf["gu"][i], "nt", [F32], name="dx_gu" + t)[0]
        if i == 0:
            g_ffn = g_ffn + proceed(1, dh2)
        dx1, dx1b, dg = _rms_bwd(s["x1"], dh2, dx, g_ffn, "rms_ffn_bwd" + t)
        per_layer["ln_ffn_g"][i] = dg[0]

        dw_mm(s["merged"], dx1b, "out", i, "dw_out" + t)
        dmerged = _matmul(dx1b, wf["out"][i], "nt", [F32], name="dx_out" + t)[0]
        dz, dys_part, dgs, db = _merge_bwd(s["z"], s["ysg"], s["proj"], s["ret"], dmerged, w["b_glu"][i:i + 1],
                                           "merge_bwd" + t)
        per_layer["b_glu"][i] = db[0]
        dw_mm(s["ysgb"], dz, "glu", i, "dw_glu" + t)
        if i == 0:
            d_skip = d_skip + depart(2)
        dys = _matmul(dz, wf["glu"][i], "nt", [F32], add=dys_part, name="dx_glu" + t)[0]
        if i == 0:
            d_skip = d_skip + proceed(2, dys)
        dy, dgg, dgr, ds5, du_part, dd = _post1_bwd(s["y"], s["proj"], s["s5y"], dmerged, dys,
                                                    d_skip, "post_bwd" + t)
        per_layer["ssm_d"][i] = dd[0]
        du, dbblk, dcblk, dlam = _s5_bwd(s["proj"], ds5, du_part, *s["s5"], "s5_bwd" + t)
        dlr, dli, dbr, dbi, dcr, dci = _s5_unpack(dbblk, dcblk, dlam)
        zeros = jnp.zeros_like(dlr)
        da_re, da_im, dlog_dt, db_re, db_im = s["disc_vjp"]((dlr, dli, zeros, zeros, dbr, dbi))
        for n, val in (("ssm_a_re", da_re), ("ssm_a_im", da_im), ("ssm_log_dt", dlog_dt), ("ssm_b_re", db_re),
                       ("ssm_b_im", db_im), ("ssm_c_re", dcr), ("ssm_c_im", dci)):
            per_layer[n][i] = val
        dqr, dkr, dv, dlg = _ret_bwd(s["qr"], s["kr"], s["proj"], dy, s["lg"], "ret_bwd" + t)
        per_layer["ret_log_gamma"][i] = dlg[:, :, 0].T
        dqkv = _rot_bwd(dqr, dkr, dv, cos, sin, "rot_bwd" + t)
        dproj = jnp.concatenate([dqkv, dgg, du, dgr, dgs], axis=1)
        dw_mm(s["h"], dproj, "in", i, "dw_in" + t)
        if i == 0:
            g_mix = g_mix + depart(3)
        dh = _matmul(dproj, wf["in"][i], "nt", [F32], name="dx_in" + t)[0]
        if i == 0:
            g_mix = g_mix + proceed(3, dh)
        dx, dxb, dg = _rms_bwd(s["x_in"], dh, dx1, g_mix, "rms_mix_bwd" + t)
        per_layer["ln_mix_g"][i] = dg[0]
        if i == DEPTH - 1:
            dxb = dxb + depart(0).astype(BF16)

    for n in per_layer:
        g_small[n] = jnp.stack(per_layer[n])
    me_arr = jnp.stack([2 * chip_idx + c_idx])
    g_mine = _flatten(g_small, _SMALL)
    rs_bufs, rs_ssem, rs_rsem, small_token = _exchange_start(_place_slot(g_mine, me_arr, True), g_mine,
                                                             "small_scatter_start")

    reduced = [None] * N_PIECES
    before = small_token
    for g, pt, land, ssem, rsem in flights:
        landed = _scatter_wait(g, pt, land, ssem, rsem, before)
        for (p, layer), buf in zip(_REDUCE_GROUPS[g], landed):
            key = (_PIECES[p][1], layer)
            reduced[p] = _reduce_half(_PIECES[p], layer, dws[key][0], got[key], buf, reduced[p], idx_arr,
                                      "reduce_%s_l%d" % (_PIECES[p][0], layer))
            before = reduced[p]
    g_big = dict(zip([p[0] for p in _PIECES], _share_halves(reduced)))

    landed = _exchange_wait(rs_bufs, rs_ssem, rs_rsem, g_big[_BIG[-1]], "small_scatter_wait")
    ag_bufs, ag_ssem, ag_rsem, ag_token = _exchange_start(
        _place_slot(_sum_slots(landed, "sum_small"), me_arr, False), None, "small_gather_start")

    grads, delta, new_m, new_v = {}, {}, {}, {}
    for n in _BIG:
        d, r, cc = w[n].shape
        two_d = lambda a: a.reshape(d * r, cc)
        dl, mn, vn = _adamw(two_d(w[n]), two_d(g_big[n]), two_d(m[n]), two_d(v[n]), "adamw_" + n, after=ag_token)
        grads[n], delta[n], new_m[n], new_v[n] = g_big[n], dl.reshape(d, r, cc), mn.reshape(d, r, cc), vn.reshape(d, r, cc)

    all_done = sum(corner(delta[n]) for n in _BIG)
    gathered = _exchange_wait(ag_bufs, ag_ssem, ag_rsem, all_done, "small_gather_wait")
    g_flat = gathered.reshape(-1, LANES)
    grads.update(_unflatten(g_flat, w, _SMALL))
    tiny_rows = _flat_rows(w, _TINY)
    dl, mn, vn = _adamw(tiny_flat[0], g_flat[:tiny_rows], tiny_flat[1], tiny_flat[2], "adamw_tiny")
    for dst, flat in ((delta, dl), (new_m, mn), (new_v, vn)):
        dst.update(_unflatten(flat, w, _TINY))
    for n in _MID:
        delta[n], new_m[n], new_v[n] = _adamw_nd(w[n], grads[n], m[n], v[n], "adamw_" + n)
    return loss, dx, grads, delta, new_m, new_v


def kernel(x, ln_mix_g, w_in, ret_log_gamma, ssm_a_re, ssm_a_im, ssm_log_dt, ssm_b_re, ssm_b_im, ssm_c_re, ssm_c_im, ssm_d, w_glu, b_glu, w_out, ln_ffn_g, w_ffn_gate, w_ffn_up, w_ffn_down, ln_final_g, loss_target, m_ln_mix_g, m_w_in, m_ret_log_gamma, m_ssm_a_re, m_ssm_a_im, m_ssm_log_dt, m_ssm_b_re, m_ssm_b_im, m_ssm_c_re, m_ssm_c_im, m_ssm_d, m_w_glu, m_b_glu, m_w_out, m_ln_ffn_g, m_w_ffn_gate, m_w_ffn_up, m_w_ffn_down, m_ln_final_g, v_ln_mix_g, v_w_in, v_ret_log_gamma, v_ssm_a_re, v_ssm_a_im, v_ssm_log_dt, v_ssm_b_re, v_ssm_b_im, v_ssm_c_re, v_ssm_c_im, v_ssm_d, v_w_glu, v_b_glu, v_w_out, v_ln_ffn_g, v_w_ffn_gate, v_w_ffn_up, v_w_ffn_down, v_ln_final_g):
    given = dict(locals())
    w = {n: given[n] for n in _WEIGHTS}
    m = {n: given["m_" + n] for n in _WEIGHTS}
    v = {n: given["v_" + n] for n in _WEIGHTS}
    loss, dx, grads, delta, new_m, new_v = _step(w, m, v, x[0], loss_target[0])
    return (loss, dx[None], *[grads[n] for n in _WEIGHTS], *[delta[n] for n in _WEIGHTS],
            *[new_m[n] for n in _WEIGHTS], *[new_v[n] for n in _WEIGHTS])
```

```python
import functools
import math

import jax
import jax.numpy as jnp
from jax import lax
from jax.experimental import pallas as pl
from jax.experimental.pallas import tpu as pltpu

F32 = jnp.float32
BF16 = jnp.bfloat16

D_MODEL = 2048
DEPTH = 2
HEADS = 4
QK_DIM = 256
V_DIM = 512
QK_WIDTH = HEADS * QK_DIM
ROPE_BASE = 10000.0
GROUP = 16
N_GROUPS = D_MODEL // GROUP
N_STATE = 64
D_FF = 5632
IN_WIDTH = 2 * QK_WIDTH + 5 * D_MODEL
EPS = 1e-6
N_CHIPS = 4

ADAM_LR = 0.001
ADAM_B1 = 0.9
ADAM_B2 = 0.999
ADAM_EPS = 1e-08
ADAM_WD = 0.01
ADAM_STEP = 10

LANES = 128
SUBLANES = 8
VMEM_LIMIT = 56 * 1024 * 1024
SEGMENTS = SUBLANES
GROUPS_PER_TILE = LANES // GROUP
STATE_COLS = GROUPS_PER_TILE * N_STATE
N_TILES = D_MODEL // LANES
SCAN_UNROLL = 8

MESH = pl.DeviceIdType.MESH
HBM_SPEC = pl.BlockSpec(memory_space=pltpu.HBM)


def _params(sem=None, **kw):
    return pltpu.CompilerParams(dimension_semantics=sem, vmem_limit_bytes=VMEM_LIMIT, **kw)


def _tile(n, cap=1024):
    for t in (2048, 1024, 512, 256, 128, 64):
        if t <= cap and n % t == 0:
            return t
    raise ValueError(n)


def _rows_call(fn, rows, pars, row_outs, par_outs, *, tm, name):
    m = rows[0][0].shape[0]
    nr, npar, nro, npo = len(rows), len(pars), len(row_outs), len(par_outs)

    def body(*refs):
        rin = refs[:nr]
        pin = refs[nr:nr + npar]
        rout = refs[nr + npar:nr + npar + nro]
        pout = refs[nr + npar + nro:]
        res = fn(*[r[...] for r in rin], *[p[...] for p in pin])
        if not isinstance(res, (tuple, list)):
            res = (res,)
        for r, v in zip(rout, res[:nro]):
            r[...] = v.astype(r.dtype)
        if npo:
            @pl.when(pl.program_id(0) == 0)
            def _():
                for p in pout:
                    p[...] = jnp.zeros(p.shape, p.dtype)
            for p, v in zip(pout, res[nro:]):
                p[...] += v

    in_specs = [pl.BlockSpec((tm, w), functools.partial(lambda cb, i: (i, cb), cb)) for (_, w, cb) in rows]
    in_specs += [pl.BlockSpec(p.shape, lambda i: (0, 0)) for p in pars]
    out_specs = [pl.BlockSpec((tm, w), lambda i: (i, 0)) for (w, _) in row_outs]
    out_specs += [pl.BlockSpec(s, lambda i: (0, 0)) for s in par_outs]
    out_shape = [jax.ShapeDtypeStruct((m, w), dt) for (w, dt) in row_outs]
    out_shape += [jax.ShapeDtypeStruct(s, F32) for s in par_outs]
    res = pl.pallas_call(
        body, name=name, grid=(m // tm,), in_specs=in_specs, out_specs=out_specs, out_shape=out_shape,
        compiler_params=_params(("arbitrary",) if npo else ("parallel",)),
    )(*[a for (a, _, _) in rows], *pars)
    return res


def _f32(*vals):
    return [v.astype(F32) for v in vals]


def _f_rms(x, g):
    r = lax.rsqrt(jnp.mean(x * x, axis=-1, keepdims=True) + EPS)
    return x * r * g


def _rms_fwd(x, g, name):
    return _rows_call(lambda xv, gv: _f_rms(xv, gv), [(x, D_MODEL, 0)], [g], [(D_MODEL, BF16)], [],
                      tm=256, name=name)[0]


def _rms_bwd(x, dh, dres, g, name):
    def fn(xv, dhv, drv, gv):
        _, vjp = jax.vjp(_f_rms, xv, gv)
        dx, dg = vjp(dhv)
        dx = dx + drv
        return dx, dx, dg
    return _rows_call(fn, [(x, D_MODEL, 0), (dh, D_MODEL, 0), (dres, D_MODEL, 0)], [g],
                      [(D_MODEL, F32), (D_MODEL, BF16)], [(1, D_MODEL)], tm=256, name=name)


def _rot_heads(xv, cos, sin, scale):
    half = QK_DIM // 2
    outs = []
    for h in range(HEADS):
        x1 = xv[:, h * QK_DIM:h * QK_DIM + half]
        x2 = xv[:, h * QK_DIM + half:(h + 1) * QK_DIM]
        outs += [(x1 * cos - x2 * sin) * scale, (x1 * sin + x2 * cos) * scale]
    return jnp.concatenate(outs, axis=1)


def _rot_fwd(proj, cos, sin, name):
    def fn(q, k, cv, sv):
        return _rot_heads(q, cv, sv, 1.0), _rot_heads(k, cv, sv, QK_DIM ** -0.5)
    return _rows_call(fn, [(proj, QK_WIDTH, 0), (proj, QK_WIDTH, 1), (cos, LANES, 0), (sin, LANES, 0)], [],
                      [(QK_WIDTH, BF16), (QK_WIDTH, BF16)], [], tm=256, name=name)


def _rot_bwd(dqr, dkr, dv, cos, sin, name):
    def fn(dq, dk, dvv, cv, sv):
        return jnp.concatenate([_rot_heads(dq, cv, -sv, 1.0), _rot_heads(dk, cv, -sv, QK_DIM ** -0.5), dvv], axis=1)
    return _rows_call(fn, [(dqr, QK_WIDTH, 0), (dkr, QK_WIDTH, 0), (dv, D_MODEL, 0), (cos, LANES, 0), (sin, LANES, 0)],
                      [], [(2 * QK_WIDTH + D_MODEL, BF16)], [], tm=256, name=name)[0]


def _f_post1(y0, y1, y2, y3, g, gr, s5, u, dsk):
    yn = [yh * lax.rsqrt(jnp.mean(yh * yh, axis=-1, keepdims=True) + EPS) for yh in (y0, y1, y2, y3)]
    ret = jax.nn.sigmoid(gr) * (jax.nn.silu(g) * jnp.concatenate(yn, axis=1))
    ysg = jax.nn.gelu(s5 + dsk * u)
    return ret, ysg


def _post1_rows(y, proj, s5y):
    rows = [(y, V_DIM, h) for h in range(HEADS)]
    rows += [(proj, D_MODEL, 2), (proj, D_MODEL, 4), (s5y, D_MODEL, 0), (proj, D_MODEL, 3)]
    return rows


def _post1_fwd(y, proj, s5y, dsk, name):
    def fn(*vals):
        ret, ysg = _f_post1(*vals)
        return ret, ysg, ysg
    return _rows_call(fn, _post1_rows(y, proj, s5y), [dsk],
                      [(D_MODEL, F32), (D_MODEL, F32), (D_MODEL, BF16)], [], tm=128, name=name)


def _post1_bwd(y, proj, s5y, dret, dys, dsk, name):
    def fn(*vals):
        prim = vals[:8] + (vals[10],)
        _, vjp = jax.vjp(_f_post1, *prim)
        gy0, gy1, gy2, gy3, gg, ggr, gs5, gu, gd = vjp((vals[8], vals[9]))
        return jnp.concatenate([gy0, gy1, gy2, gy3], axis=1), gg, ggr, gs5, gu, gd
    rows = _post1_rows(y, proj, s5y) + [(dret, D_MODEL, 0), (dys, D_MODEL, 0)]
    return _rows_call(fn, rows, [dsk],
                      [(D_MODEL, BF16), (D_MODEL, BF16), (D_MODEL, BF16), (D_MODEL, F32), (D_MODEL, F32)],
                      [(1, D_MODEL)], tm=128, name=name)


def _f_merge(z, ysg, gs, ret, b):
    return ret + jax.nn.sigmoid(gs) * (ysg * jax.nn.sigmoid(z + b))


def _merge_fwd(z, ysg, proj, ret, b, name):
    return _rows_call(_f_merge, [(z, D_MODEL, 0), (ysg, D_MODEL, 0), (proj, D_MODEL, 5), (ret, D_MODEL, 0)], [b],
                      [(D_MODEL, BF16)], [], tm=128, name=name)[0]


def _merge_bwd(z, ysg, proj, ret, dm, b, name):
    def fn(zv, yv, gv, rv, dmv, bv):
        _, vjp = jax.vjp(_f_merge, zv, yv, gv, rv, bv)
        gz, gy, gg, _, gb = vjp(dmv)
        return gz, gy, gg, gb
    rows = [(z, D_MODEL, 0), (ysg, D_MODEL, 0), (proj, D_MODEL, 5), (ret, D_MODEL, 0), (dm, D_MODEL, 0)]
    return _rows_call(fn, rows, [b], [(D_MODEL, BF16), (D_MODEL, F32), (D_MODEL, BF16)], [(1, D_MODEL)],
                      tm=128, name=name)


def _f_glu(a, b):
    return jax.nn.silu(a) * b


def _glu_fwd(ab, name):
    return _rows_call(_f_glu, [(ab, D_FF, 0), (ab, D_FF, 1)], [], [(D_FF, BF16)], [], tm=128, name=name)[0]


def _glu_bwd(ab, df, name):
    def fn(a, b, d):
        _, vjp = jax.vjp(_f_glu, a, b)
        ga, gb = vjp(d)
        return jnp.concatenate([ga, gb], axis=1)
    return _rows_call(fn, [(ab, D_FF, 0), (ab, D_FF, 1), (df, D_FF, 0)], [], [(2 * D_FF, BF16)], [],
                      tm=128, name=name)[0]


def _loss_stage(x, tgt, g, name):
    def fn(xv, tv, gv):
        def lf(xx, gg):
            err = _f_rms(xx, gg) - tv
            row = jnp.mean(err * err, axis=-1, keepdims=True)
            return 0.5 * jnp.sum(row, axis=0, keepdims=True)
        l, vjp = jax.vjp(lf, xv, gv)
        dx, dg = vjp(jnp.ones((1, 1), F32))
        return dx, dx, jnp.broadcast_to(l, (1, LANES)), dg
    return _rows_call(fn, [(x, D_MODEL, 0), (tgt, D_MODEL, 0)], [g], [(D_MODEL, F32), (D_MODEL, BF16)],
                      [(1, LANES), (1, D_MODEL)], tm=256, name=name)


def _adam_math(wv, gv, mv, vv):
    mn = ADAM_B1 * mv + (1.0 - ADAM_B1) * gv
    vn = ADAM_B2 * vv + (1.0 - ADAM_B2) * (gv * gv)
    m_hat = mn / (1.0 - ADAM_B1 ** ADAM_STEP)
    v_hat = vn / (1.0 - ADAM_B2 ** ADAM_STEP)
    delta = -ADAM_LR * (m_hat / (jnp.sqrt(v_hat) + ADAM_EPS) + ADAM_WD * wv)
    return delta, mn, vn


def _adamw(w, g, m, v, name, after=None):
    rows, cols = w.shape
    tm = _tile(rows, 128 if cols > D_FF // N_CHIPS else (256 if cols > LANES else 512))
    fn = _adam_math if after is None else (lambda wv, gv, mv, vv, _: _adam_math(wv, gv, mv, vv))
    return _rows_call(fn, [(w, cols, 0), (g, cols, 0), (m, cols, 0), (v, cols, 0)], [] if after is None else [after],
                      [(cols, F32)] * 3, [], tm=tm, name=name)


def _adamw_nd(w, g, m, v, name):
    shape = w.shape
    lead = math.prod(shape[:-2])
    blk = (lead // 8,) + shape[-2:]
    three_d = lambda a: a.reshape((lead,) + shape[-2:])

    def body(w_ref, g_ref, m_ref, v_ref, d_ref, mn_ref, vn_ref):
        d_ref[...], mn_ref[...], vn_ref[...] = _adam_math(w_ref[...], g_ref[...], m_ref[...], v_ref[...])

    spec = pl.BlockSpec(blk, lambda i: (i, 0, 0))
    res = pl.pallas_call(
        body, name=name, grid=(8,), in_specs=[spec] * 4, out_specs=[spec] * 3,
        out_shape=[jax.ShapeDtypeStruct((lead,) + shape[-2:], F32)] * 3,
        compiler_params=_params(("parallel",)),
    )(three_d(w), three_d(g), three_d(m), three_d(v))
    return [r.reshape(shape) for r in res]


MATMUL_VMEM_BUDGET = 44 * 1024 * 1024


def _matmul_tiles(m, n, k, out_bytes, has_add):
    if k > 2048:
        return _tile(m, 1024), _tile(n, 1024), _tile(k, 1024)
    tm, tn, tk = _tile(m, 2048), _tile(n, 1024), k

    def footprint():
        acc = 4 * tm * tn if k // tk > 1 else 0
        return 2 * 2 * (tm * tk + tk * tn) + 2 * (out_bytes + 4 * has_add) * tm * tn + acc

    while footprint() > MATMUL_VMEM_BUDGET:
        if tn > 512 and n % (tn // 2) == 0:
            tn //= 2
        elif tk > 512 and k % (tk // 2) == 0:
            tk //= 2
        else:
            tm //= 2
    return tm, tn, tk


def _matmul(a, b, mode, out_dtypes, *, name, add=None):
    if mode == "nn":
        (m, k), (_, n) = a.shape, b.shape
    elif mode == "nt":
        (m, k), (n, _) = a.shape, b.shape
    else:
        (k, m), (_, n) = a.shape, b.shape
    n_out = len(out_dtypes)
    has_add = add is not None
    tm, tn, tk = _matmul_tiles(m, n, k, sum(jnp.dtype(dt).itemsize for dt in out_dtypes), has_add)
    nk = k // tk
    if mode == "nn":
        a_spec = pl.BlockSpec((tm, tk), lambda i, j, kk: (i, kk))
        b_spec = pl.BlockSpec((tk, tn), lambda i, j, kk: (kk, j))
        dims = (((1,), (0,)), ((), ()))
    elif mode == "nt":
        a_spec = pl.BlockSpec((tm, tk), lambda i, j, kk: (i, kk))
        b_spec = pl.BlockSpec((tn, tk), lambda i, j, kk: (j, kk))
        dims = (((1,), (1,)), ((), ()))
    else:
        a_spec = pl.BlockSpec((tk, tm), lambda i, j, kk: (kk, i))
        b_spec = pl.BlockSpec((tk, tn), lambda i, j, kk: (kk, j))
        dims = (((0,), (0,)), ((), ()))

    def body(*refs):
        a_ref, b_ref = refs[0], refs[1]
        add_ref = refs[2] if has_add else None
        outs = refs[2 + has_add:2 + has_add + n_out]

        def finish(r):
            if has_add:
                r = r + add_ref[...]
            for o in outs:
                o[...] = r.astype(o.dtype)

        if nk == 1:
            finish(lax.dot_general(a_ref[...], b_ref[...], dims, preferred_element_type=F32))
            return
        acc = refs[-1]
        kk = pl.program_id(2)

        @pl.when(kk == 0)
        def _():
            acc[...] = jnp.zeros(acc.shape, F32)

        acc[...] += lax.dot_general(a_ref[...], b_ref[...], dims, preferred_element_type=F32)

        @pl.when(kk == nk - 1)
        def _():
            finish(acc[...])

    in_specs = [a_spec, b_spec]
    args = [a, b]
    if has_add:
        in_specs.append(pl.BlockSpec((tm, tn), lambda i, j, kk: (i, j)))
        args.append(add)
    return pl.pallas_call(
        body, name=name, grid=(m // tm, n // tn, nk), in_specs=in_specs,
        out_specs=[pl.BlockSpec((tm, tn), lambda i, j, kk: (i, j))] * n_out,
        out_shape=[jax.ShapeDtypeStruct((m, n), dt) for dt in out_dtypes],
        scratch_shapes=[pltpu.VMEM((tm, tn), F32)] if nk > 1 else [],
        compiler_params=_params(("parallel", "parallel", "arbitrary")),
    )(*args)


RET_TQ = 512


def _decay(lg_ref, i, tq, seq):
    n_idx = i * tq + lax.broadcasted_iota(jnp.int32, (tq, seq), 0)
    m_idx = lax.broadcasted_iota(jnp.int32, (tq, seq), 1)
    diff = (n_idx - m_idx).astype(F32)
    lgf = lg_ref[0, 0:1, 0:1]
    lgb = lg_ref[0, 1:2, 0:1]
    causal = diff >= 0
    return jnp.exp(jnp.where(causal, lgf * diff, -lgb * diff)), diff, causal


_NT = (((1,), (1,)), ((), ()))
_TN = (((0,), (0,)), ((), ()))


def _ret_fwd(qr, kr, proj, lg, name):
    seq = qr.shape[0]
    tq = RET_TQ
    v_blk0 = (2 * QK_WIDTH) // V_DIM

    def body(q_ref, k_ref, v_ref, lg_ref, y_ref):
        i = pl.program_id(1)
        s = lax.dot_general(q_ref[...], k_ref[...], _NT, preferred_element_type=F32)
        dm, _, _ = _decay(lg_ref, i, tq, seq)
        p = (s * dm).astype(BF16)
        y_ref[...] = jnp.dot(p, v_ref[...].astype(BF16), preferred_element_type=F32)

    return pl.pallas_call(
        body, name=name, grid=(HEADS, seq // tq),
        in_specs=[pl.BlockSpec((tq, QK_DIM), lambda h, i: (i, h)),
                  pl.BlockSpec((seq, QK_DIM), lambda h, i: (0, h)),
                  pl.BlockSpec((seq, V_DIM), lambda h, i: (0, v_blk0 + h)),
                  pl.BlockSpec((1, 2, LANES), lambda h, i: (h, 0, 0))],
        out_specs=pl.BlockSpec((tq, V_DIM), lambda h, i: (i, h)),
        out_shape=jax.ShapeDtypeStruct((seq, HEADS * V_DIM), F32),
        compiler_params=_params(("parallel", "parallel")),
    )(qr, kr, proj, lg)


def _ret_bwd(qr, kr, proj, dy, lg, name):
    seq = qr.shape[0]
    tq = RET_TQ
    v_blk0 = (2 * QK_WIDTH) // V_DIM

    def body(q_ref, k_ref, v_ref, dy_ref, lg_ref, dq_ref, dk_ref, dv_ref, dlg_ref):
        i = pl.program_id(1)

        @pl.when(i == 0)
        def _():
            dk_ref[...] = jnp.zeros(dk_ref.shape, F32)
            dv_ref[...] = jnp.zeros(dv_ref.shape, F32)
            dlg_ref[...] = jnp.zeros(dlg_ref.shape, F32)

        q = q_ref[...]
        k = k_ref[...]
        vb = v_ref[...].astype(BF16)
        dyb = dy_ref[...]
        s = lax.dot_general(q, k, _NT, preferred_element_type=F32)
        dm, diff, causal = _decay(lg_ref, i, tq, seq)
        p = s * dm
        dp = lax.dot_general(dyb, vb, _NT, preferred_element_type=F32)
        dv_ref[...] += lax.dot_general(p.astype(BF16), dyb, _TN, preferred_element_type=F32)
        ds = (dp * dm).astype(BF16)
        dq_ref[...] = jnp.dot(ds, k, preferred_element_type=F32)
        dk_ref[...] += lax.dot_general(ds, q, _TN, preferred_element_type=F32)
        gd = dp * p * diff
        dlf = jnp.sum(jnp.sum(jnp.where(causal, gd, 0.0), axis=1, keepdims=True), axis=0, keepdims=True)
        dlb = jnp.sum(jnp.sum(jnp.where(causal, 0.0, -gd), axis=1, keepdims=True), axis=0, keepdims=True)
        row = lax.broadcasted_iota(jnp.int32, (2, LANES), 0)
        dlg_ref[0] += jnp.where(row == 0, dlf, dlb)

    return pl.pallas_call(
        body, name=name, grid=(HEADS, seq // tq),
        in_specs=[pl.BlockSpec((tq, QK_DIM), lambda h, i: (i, h)),
                  pl.BlockSpec((seq, QK_DIM), lambda h, i: (0, h)),
                  pl.BlockSpec((seq, V_DIM), lambda h, i: (0, v_blk0 + h)),
                  pl.BlockSpec((tq, V_DIM), lambda h, i: (i, h)),
                  pl.BlockSpec((1, 2, LANES), lambda h, i: (h, 0, 0))],
        out_specs=[pl.BlockSpec((tq, QK_DIM), lambda h, i: (i, h)),
                   pl.BlockSpec((seq, QK_DIM), lambda h, i: (0, h)),
                   pl.BlockSpec((seq, V_DIM), lambda h, i: (0, h)),
                   pl.BlockSpec((1, 2, LANES), lambda h, i: (h, 0, 0))],
        out_shape=[jax.ShapeDtypeStruct((seq, QK_WIDTH), F32), jax.ShapeDtypeStruct((seq, QK_WIDTH), F32),
                   jax.ShapeDtypeStruct((seq, HEADS * V_DIM), F32), jax.ShapeDtypeStruct((HEADS, 2, LANES), F32)],
        compiler_params=_params(("parallel", "arbitrary")),
    )(qr, kr, proj, dy, lg)


def _shift_rows(v, reverse):
    row = lax.broadcasted_iota(jnp.int32, v.shape, 0)
    if reverse:
        return jnp.where(row == SEGMENTS - 1, 0.0, pltpu.roll(v, SEGMENTS - 1, 0))
    return jnp.where(row == 0, 0.0, pltpu.roll(v, 1, 0))


def _slab(t):
    if isinstance(t, int):
        return pl.ds(t * SEGMENTS, SEGMENTS)
    return pl.ds(pl.multiple_of(t * SEGMENTS, SEGMENTS), SEGMENTS)


def _unrolled_loop(body, lo, hi, init):
    main = (hi - lo) // SCAN_UNROLL

    def unrolled(g, carry):
        for k in range(SCAN_UNROLL):
            carry = body(lo + g * SCAN_UNROLL + k, carry)
        return carry

    carry = lax.fori_loop(0, main, unrolled, init)
    for t in range(lo + main * SCAN_UNROLL, hi):
        carry = body(t, carry)
    return carry


def _scan(xr_ref, xi_ref, lam, reverse, conj):
    steps = xr_ref.shape[0] // SEGMENTS
    cols = xr_ref.shape[1]
    lr = jnp.broadcast_to(lam[0], (SEGMENTS, cols))
    li = jnp.broadcast_to(lam[1], (SEGMENTS, cols))
    lrt = jnp.broadcast_to(lam[2], (SEGMENTS, cols))
    lit = jnp.broadcast_to(lam[3], (SEGMENTS, cols))
    if conj:
        li, lit = -li, -lit
    zero = jnp.zeros((SEGMENTS, cols), F32)

    def rows_of(t):
        return _slab(steps - 1 - t if reverse else t)

    def advance(t, carry):
        sr, si = carry
        rows = rows_of(t)
        return lr * sr - li * si + xr_ref[rows, :], lr * si + li * sr + xi_ref[rows, :]

    def step(t, carry):
        nr, ni = advance(t, carry)
        rows = rows_of(t)
        xr_ref[rows, :] = nr
        xi_ref[rows, :] = ni
        return nr, ni

    def run(body, init):
        return _unrolled_loop(body, 0, steps, init)

    er, ei = run(advance, (zero, zero))
    cr, ci = zero, zero
    for _ in range(SEGMENTS - 1):
        tr = er + lrt * cr - lit * ci
        ti = ei + lrt * ci + lit * cr
        cr, ci = _shift_rows(tr, reverse), _shift_rows(ti, reverse)
    run(step, (cr, ci))


def _permute_in(dst_ref, src_ref):
    steps = src_ref.shape[0] // SEGMENTS
    for s in range(SEGMENTS):
        dst_ref[pl.ds(s, steps, stride=SEGMENTS), :] = src_ref[s * steps:(s + 1) * steps, :].astype(dst_ref.dtype)


def _unpermute(src_ref, s):
    steps = src_ref.shape[0] // SEGMENTS
    return src_ref[pl.ds(s, steps, stride=SEGMENTS), :]


def _s5_fwd(proj, bblk, cblk, lam, name):
    seq = proj.shape[0]
    u_blk0 = (2 * QK_WIDTH + 2 * D_MODEL) // LANES
    sc = STATE_COLS

    def body(u_ref, b_ref, c_ref, lam_ref, y_ref, up_ref, yp_ref, xr_ref, xi_ref):
        _permute_in(up_ref, u_ref)
        ub = up_ref[...].astype(BF16)
        for d in range(2):
            xr_ref[...] = jnp.dot(ub, b_ref[d, :, 0:sc], preferred_element_type=F32)
            xi_ref[...] = jnp.dot(ub, b_ref[d, :, sc:2 * sc], preferred_element_type=F32)
            lm = [lam_ref[d, r:r + 1, :] for r in range(4)]
            _scan(xr_ref, xi_ref, lm, reverse=(d == 1), conj=False)
            yd = (jnp.dot(xr_ref[...].astype(BF16), c_ref[d, 0:sc, :], preferred_element_type=F32)
                  + jnp.dot(xi_ref[...].astype(BF16), c_ref[d, sc:2 * sc, :], preferred_element_type=F32))
            if d == 0:
                yp_ref[...] = yd
            else:
                yp_ref[...] += yd
        steps = seq // SEGMENTS
        for s in range(SEGMENTS):
            y_ref[s * steps:(s + 1) * steps, :] = _unpermute(yp_ref, s)

    return pl.pallas_call(
        body, name=name, grid=(N_TILES,),
        in_specs=[pl.BlockSpec((seq, LANES), lambda j: (0, u_blk0 + j)),
                  pl.BlockSpec((2, None, LANES, 2 * sc), lambda j: (0, j, 0, 0)),
                  pl.BlockSpec((2, None, 2 * sc, LANES), lambda j: (0, j, 0, 0)),
                  pl.BlockSpec((2, None, 4, sc), lambda j: (0, j, 0, 0))],
        out_specs=pl.BlockSpec((seq, LANES), lambda j: (0, j)),
        out_shape=jax.ShapeDtypeStruct((seq, D_MODEL), F32),
        scratch_shapes=[pltpu.VMEM((seq, LANES), F32), pltpu.VMEM((seq, LANES), F32),
                        pltpu.VMEM((seq, sc), F32), pltpu.VMEM((seq, sc), F32)],
        compiler_params=_params(("parallel",)),
    )(proj, bblk, cblk, lam)


def _s5_bwd(proj, dy, du_part, bblk, cblk, lam, name):
    seq = proj.shape[0]
    u_blk0 = (2 * QK_WIDTH + 2 * D_MODEL) // LANES
    sc = STATE_COLS
    steps = seq // SEGMENTS

    def body(u_ref, dy_ref, dup_ref, b_ref, c_ref, lam_ref, du_ref, db_ref, dc_ref, dlam_ref,
             up_ref, dyp_ref, dua_ref, xr_ref, xi_ref, gr_ref, gi_ref):
        _permute_in(up_ref, u_ref)
        _permute_in(dyp_ref, dy_ref)
        ub = up_ref[...].astype(BF16)
        dyb = dyp_ref[...].astype(BF16)
        ubt = up_ref[...].T.astype(BF16)
        dybt = dyp_ref[...].T.astype(BF16)
        for d in range(2):
            reverse = d == 1
            xr_ref[...] = jnp.dot(ub, b_ref[d, :, 0:sc], preferred_element_type=F32)
            xi_ref[...] = jnp.dot(ub, b_ref[d, :, sc:2 * sc], preferred_element_type=F32)
            lm = [lam_ref[d, r:r + 1, :] for r in range(4)]
            _scan(xr_ref, xi_ref, lm, reverse=reverse, conj=False)
            xrb = xr_ref[...].astype(BF16)
            xib = xi_ref[...].astype(BF16)
            dc_ref[d, :, 0:sc] = jnp.dot(dybt, xrb, preferred_element_type=F32)
            dc_ref[d, :, sc:2 * sc] = jnp.dot(dybt, xib, preferred_element_type=F32)
            gr_ref[...] = lax.dot_general(dyb, c_ref[d, 0:sc, :], _NT, preferred_element_type=F32)
            gi_ref[...] = lax.dot_general(dyb, c_ref[d, sc:2 * sc, :], _NT, preferred_element_type=F32)
            _scan(gr_ref, gi_ref, lm, reverse=not reverse, conj=True)

            def acc_step(t, carry):
                ar, ai = carry
                prev = _slab(t + 1 if reverse else t - 1)
                pr = xr_ref[prev, :]
                pi = xi_ref[prev, :]
                zr = gr_ref[_slab(t), :]
                zi = gi_ref[_slab(t), :]
                return ar + zr * pr + zi * pi, ai + zi * pr - zr * pi

            zero = jnp.zeros((SEGMENTS, sc), F32)
            if reverse:
                ar, ai = _unrolled_loop(acc_step, 0, steps - 1, (zero, zero))
                edge = _slab(steps - 1)
                pr = _shift_rows(xr_ref[_slab(0), :], True)
                pi = _shift_rows(xi_ref[_slab(0), :], True)
            else:
                ar, ai = _unrolled_loop(acc_step, 1, steps, (zero, zero))
                edge = _slab(0)
                pr = _shift_rows(xr_ref[_slab(steps - 1), :], False)
                pi = _shift_rows(xi_ref[_slab(steps - 1), :], False)
            zr = gr_ref[edge, :]
            zi = gi_ref[edge, :]
            ar = ar + zr * pr + zi * pi
            ai = ai + zi * pr - zr * pi
            dlam_ref[d, 0:1, :] = jnp.sum(ar, axis=0, keepdims=True)
            dlam_ref[d, 1:2, :] = jnp.sum(ai, axis=0, keepdims=True)

            grb = gr_ref[...].astype(BF16)
            gib = gi_ref[...].astype(BF16)
            db_ref[d, :, 0:sc] = jnp.dot(ubt, grb, preferred_element_type=F32)
            db_ref[d, :, sc:2 * sc] = jnp.dot(ubt, gib, preferred_element_type=F32)
            dud = (lax.dot_general(grb, b_ref[d, :, 0:sc], _NT, preferred_element_type=F32)
                   + lax.dot_general(gib, b_ref[d, :, sc:2 * sc], _NT, preferred_element_type=F32))
            if d == 0:
                dua_ref[...] = dud
            else:
                dua_ref[...] += dud
        for s in range(SEGMENTS):
            rows = slice(s * steps, (s + 1) * steps)
            du_ref[rows, :] = (_unpermute(dua_ref, s) + dup_ref[rows, :]).astype(du_ref.dtype)

    return pl.pallas_call(
        body, name=name, grid=(N_TILES,),
        in_specs=[pl.BlockSpec((seq, LANES), lambda j: (0, u_blk0 + j)),
                  pl.BlockSpec((seq, LANES), lambda j: (0, j)),
                  pl.BlockSpec((seq, LANES), lambda j: (0, j)),
                  pl.BlockSpec((2, None, LANES, 2 * sc), lambda j: (0, j, 0, 0)),
                  pl.BlockSpec((2, None, 2 * sc, LANES), lambda j: (0, j, 0, 0)),
                  pl.BlockSpec((2, None, 4, sc), lambda j: (0, j, 0, 0))],
        out_specs=[pl.BlockSpec((seq, LANES), lambda j: (0, j)),
                   pl.BlockSpec((2, None, LANES, 2 * sc), lambda j: (0, j, 0, 0)),
                   pl.BlockSpec((2, None, LANES, 2 * sc), lambda j: (0, j, 0, 0)),
                   pl.BlockSpec((2, None, 2, sc), lambda j: (0, j, 0, 0))],
        out_shape=[jax.ShapeDtypeStruct((seq, D_MODEL), BF16),
                   jax.ShapeDtypeStruct((2, N_TILES, LANES, 2 * sc), F32),
                   jax.ShapeDtypeStruct((2, N_TILES, LANES, 2 * sc), F32),
                   jax.ShapeDtypeStruct((2, N_TILES, 2, sc), F32)],
        scratch_shapes=[pltpu.VMEM((seq, LANES), F32), pltpu.VMEM((seq, LANES), F32), pltpu.VMEM((seq, LANES), F32),
                        pltpu.VMEM((seq, sc), F32), pltpu.VMEM((seq, sc), F32),
                        pltpu.VMEM((seq, sc), F32), pltpu.VMEM((seq, sc), F32)],
        compiler_params=_params(("parallel",)),
    )(proj, dy, du_part, bblk, cblk, lam)


def _s5_discretize(a_re, a_im, log_dt, b_re, b_im, seg_len):
    dt = jnp.exp(log_dt)[..., None]
    e = jnp.exp(a_re * dt)
    lr, li = e * jnp.cos(a_im * dt), e * jnp.sin(a_im * dt)
    et = jnp.exp(a_re * dt * seg_len)
    lrt, lit = et * jnp.cos(a_im * dt * seg_len), et * jnp.sin(a_im * dt * seg_len)
    den = a_re * a_re + a_im * a_im
    qr = ((lr - 1.0) * a_re + li * a_im) / den
    qi = (li * a_re - (lr - 1.0) * a_im) / den
    br = qr[..., None] * b_re - qi[..., None] * b_im
    bi = qr[..., None] * b_im + qi[..., None] * b_re
    return lr, li, lrt, lit, br, bi


def _s5_pack(lr, li, lrt, lit, br, bi, c_re, c_im):
    eye = jnp.eye(GROUPS_PER_TILE, dtype=F32)

    def bd_b(b):
        b5 = b.reshape(2, N_TILES, GROUPS_PER_TILE, N_STATE, GROUP)
        return jnp.einsum("dtgph,gk->dtghkp", b5, eye).reshape(2, N_TILES, LANES, STATE_COLS)

    def bd_c(c):
        c5 = c.reshape(2, N_TILES, GROUPS_PER_TILE, GROUP, N_STATE)
        return jnp.einsum("dtghp,gk->dtkpgh", c5, eye).reshape(2, N_TILES, STATE_COLS, LANES)

    bblk = jnp.concatenate([bd_b(br), bd_b(bi)], axis=3)
    cblk = jnp.concatenate([bd_c(c_re), -bd_c(c_im)], axis=2)
    lam = jnp.stack([v.reshape(2, N_TILES, STATE_COLS) for v in (lr, li, lrt, lit)], axis=2)
    return bblk, cblk, lam


def _s5_unpack(dbblk, dcblk, dlam):
    eye = jnp.eye(GROUPS_PER_TILE, dtype=F32)

    def diag_b(d):
        d6 = d.reshape(2, N_TILES, GROUPS_PER_TILE, GROUP, GROUPS_PER_TILE, N_STATE)
        return jnp.einsum("dtghkp,gk->dtgph", d6, eye).reshape(2, N_GROUPS, N_STATE, GROUP)

    def diag_c(d):
        d6 = d.reshape(2, N_TILES, GROUPS_PER_TILE, GROUP, GROUPS_PER_TILE, N_STATE)
        return jnp.einsum("dtghkp,gk->dtghp", d6, eye).reshape(2, N_GROUPS, GROUP, N_STATE)

    dbr, dbi = diag_b(dbblk[..., :STATE_COLS]), diag_b(dbblk[..., STATE_COLS:])
    dcr, dci = diag_c(dcblk[..., :STATE_COLS]), -diag_c(dcblk[..., STATE_COLS:])
    dlr = dlam[:, :, 0, :].reshape(2, N_GROUPS, N_STATE)
    dli = dlam[:, :, 1, :].reshape(2, N_GROUPS, N_STATE)
    return dlr, dli, dbr, dbi, dcr, dci


def _pos():
    return lax.axis_index("x"), lax.axis_index("y"), lax.axis_index("c")


def _remote(src, dst, ssem, rsem, dev):
    return pltpu.make_async_remote_copy(src_ref=src, dst_ref=dst, send_sem=ssem, recv_sem=rsem,
                                        device_id=dev, device_id_type=MESH)


_PIECES = (
    ("w_in", "in", D_MODEL, IN_WIDTH // N_CHIPS, 0, IN_WIDTH // N_CHIPS, 0),
    ("w_glu", "glu", D_MODEL // N_CHIPS, D_MODEL, D_MODEL // N_CHIPS, 0, 0),
    ("w_out", "out", D_MODEL // N_CHIPS, D_MODEL, D_MODEL // N_CHIPS, 0, 0),
    ("w_ffn_gate", "gu", D_MODEL, D_FF // N_CHIPS, 0, D_FF // N_CHIPS, 0),
    ("w_ffn_up", "gu", D_MODEL, D_FF // N_CHIPS, 0, D_FF // N_CHIPS, D_FF),
    ("w_ffn_down", "down", D_FF // N_CHIPS, D_MODEL, D_FF // N_CHIPS, 0, 0),
)
_BUFFERS = (("in", D_MODEL, IN_WIDTH), ("glu", D_MODEL, D_MODEL), ("out", D_MODEL, D_MODEL),
            ("gu", D_MODEL, 2 * D_FF), ("down", D_FF, D_MODEL))
_BUF_INDEX = {name: t for t, (name, _, _) in enumerate(_BUFFERS)}
N_PIECES = len(_PIECES)
N_BUFFERS = len(_BUFFERS)


def _own_block(piece, tm):
    _, _, _, cs, rstep, cstep, coff = piece
    return lambda i, chip: (i + chip * (rstep // tm), coff // cs + chip * (cstep // cs))


def _cast_place(piece, w3, layer, prev, chip_arr, name):
    _, r, cc = w3.shape
    _, rf, cf = _BUFFERS[_BUF_INDEX[piece[1]]]
    tm = _tile(r, 256)
    own = _own_block(piece, tm)

    def body(s_ref, w_ref, *rest):
        rest[-1][...] = w_ref[...].astype(BF16)

    in_specs = [pl.BlockSpec((None, tm, cc), lambda i, s: (layer, i, 0))]
    args = [w3]
    aliases = {}
    if prev is not None:
        in_specs.append(pl.BlockSpec(memory_space=pl.ANY))
        args.append(prev)
        aliases = {2: 0}
    return pl.pallas_call(
        body, name=name,
        grid_spec=pltpu.PrefetchScalarGridSpec(
            num_scalar_prefetch=1, grid=(r // tm,), in_specs=in_specs,
            out_specs=pl.BlockSpec((tm, cc), lambda i, s: own(i, s[0]))),
        out_shape=jax.ShapeDtypeStruct((rf, cf), BF16), input_output_aliases=aliases,
        compiler_params=_params(("parallel",)),
    )(chip_arr, *args)


_GATHER_GROUPS = ((0, (0,)), (0, (1, 2)), (0, (3, 4, 5)), (1, (0,)), (1, (1, 2)), (1, (3, 4, 5)))
GROUPS_PER_LAYER = len(_GATHER_GROUPS) // DEPTH
_SPLIT_EFFECT = pltpu.SideEffectType.DATAFLOW_SIDE_EFFECTING
SEM_SPEC = pl.BlockSpec(memory_space=pltpu.SEMAPHORE)
BF16_ROWS = 2 * SUBLANES


def _group_keys(g):
    layer, pieces = _GATHER_GROUPS[g]
    keys = []
    for p in pieces:
        if (_PIECES[p][1], layer) not in keys:
            keys.append((_PIECES[p][1], layer))
    return keys


def _half_view(ref, piece, j, c):
    _, _, rs, cs, rstep, cstep, coff = piece
    half = rs // 2
    return ref.at[pl.ds(pl.multiple_of(j * rstep + c * half, BF16_ROWS), half), pl.ds(coff + j * cstep, cs)]


def _for_my_chip(fn):
    x, y, _ = _pos()
    for mine in range(N_CHIPS):
        pl.when(2 * x + y == mine)(functools.partial(fn, mine, [j for j in range(N_CHIPS) if j != mine]))


def _gather_start(groups, placed):
    keys = [k for g in groups for k in _group_keys(g)]
    nb, ng = len(keys), len(groups)

    def body(*refs):
        bufs = dict(zip(keys, refs[nb:2 * nb]))
        ssems = refs[2 * nb:2 * nb + ng]
        rsems = refs[2 * nb + ng:2 * nb + 2 * ng]
        token = refs[2 * nb + 2 * ng]
        _, _, c = _pos()

        def send(mine, others):
            for t, g in enumerate(groups):
                layer, pieces = _GATHER_GROUPS[g]
                for k, p in enumerate(pieces):
                    view = _half_view(bufs[(_PIECES[p][1], layer)], _PIECES[p], mine, c)
                    for j in others:
                        _remote(view, view, ssems[t].at[k * N_CHIPS + j], rsems[t].at[k * N_CHIPS + mine],
                                (j // 2, j % 2, c)).start()

        _for_my_chip(send)
        token[...] = jnp.zeros(token.shape, token.dtype)

    sems = [pltpu.SemaphoreType.DMA((N_CHIPS * len(_GATHER_GROUPS[g][1]),)) for g in groups]
    shapes = [jax.ShapeDtypeStruct(placed[k].shape, placed[k].dtype) for k in keys]
    res = pl.pallas_call(
        body, name="gather_start_g%d" % groups[0],
        in_specs=[HBM_SPEC] * nb,
        out_specs=[HBM_SPEC] * nb + [SEM_SPEC] * (2 * ng) + [pl.BlockSpec(memory_space=pltpu.VMEM)],
        out_shape=shapes + sems + sems + [jax.ShapeDtypeStruct((SUBLANES, LANES), F32)],
        input_output_aliases={t: t for t in range(nb)},
        compiler_params=_params(has_side_effects=_SPLIT_EFFECT),
    )(*[pltpu.with_memory_space_constraint(placed[k], pltpu.HBM) for k in keys])
    return (dict(zip(keys, res[:nb])), dict(zip(groups, res[nb:nb + ng])),
            dict(zip(groups, res[nb + ng:nb + 2 * ng])), res[nb + 2 * ng])


def _gather_wait(g, bufs, ssem, rsem, after):
    layer, pieces = _GATHER_GROUPS[g]
    keys = _group_keys(g)
    nb = len(keys)

    def body(*refs):
        ssem_ref, rsem_ref = refs[nb], refs[nb + 1]
        land = dict(zip(keys, refs[nb + 3:]))
        _, _, c = _pos()

        def wait(mine, others):
            for k, p in enumerate(pieces):
                ref = land[(_PIECES[p][1], layer)]
                for j in others:
                    cp = _remote(_half_view(ref, _PIECES[p], mine, c), _half_view(ref, _PIECES[p], j, c),
                                 ssem_ref.at[k * N_CHIPS + j], rsem_ref.at[k * N_CHIPS + j], (j // 2, j % 2, c))
                    cp.wait_send()
                    cp.wait_recv()

        _for_my_chip(wait)

    return pl.pallas_call(
        body, name="gather_wait_g%d" % g,
        in_specs=[HBM_SPEC] * nb + [SEM_SPEC, SEM_SPEC, pl.BlockSpec(memory_space=pl.ANY)],
        out_specs=[HBM_SPEC] * nb,
        out_shape=[jax.ShapeDtypeStruct(a.shape, a.dtype) for a in bufs],
        input_output_aliases={t: t for t in range(nb)},
        compiler_params=_params(has_side_effects=_SPLIT_EFFECT),
    )(*bufs, ssem, rsem, after)


def _gather_forward(g, bufs):
    layer, pieces = _GATHER_GROUPS[g]
    keys = _group_keys(g)
    nb = len(keys)

    def body(*refs):
        land = dict(zip(keys, refs[nb:2 * nb]))
        ssem, rsem = refs[2 * nb:]
        x, y, c = _pos()

        def forward(mine, others):
            cps = []
            for k, p in enumerate(pieces):
                ref = land[(_PIECES[p][1], layer)]
                for j in others:
                    view = _half_view(ref, _PIECES[p], j, c)
                    cp = _remote(view, view, ssem.at[k * N_CHIPS + j], rsem.at[k * N_CHIPS + j], (x, y, 1 - c))
                    cp.start()
                    cps.append(cp)
            for k, p in enumerate(pieces):
                ref = land[(_PIECES[p][1], layer)]
                for j in others:
                    view = _half_view(ref, _PIECES[p], j, 1 - c)
                    _remote(view, view, ssem.at[k * N_CHIPS + j], rsem.at[k * N_CHIPS + j], (x, y, 1 - c)).wait_recv()
            for cp in cps:
                cp.wait_send()

        _for_my_chip(forward)

    nsem = N_CHIPS * len(pieces)
    return pl.pallas_call(
        body, name="gather_forward_g%d" % g,
        in_specs=[HBM_SPEC] * nb, out_specs=[HBM_SPEC] * nb,
        out_shape=[jax.ShapeDtypeStruct(a.shape, a.dtype) for a in bufs],
        input_output_aliases={t: t for t in range(nb)},
        scratch_shapes=[pltpu.SemaphoreType.DMA((nsem,)), pltpu.SemaphoreType.DMA((nsem,))],
        compiler_params=_params(has_side_effects=True),
    )(*bufs)


def _forward_start(g, bufs):
    layer, pieces = _GATHER_GROUPS[g]
    keys = _group_keys(g)
    nb = len(keys)

    def body(*refs):
        land = dict(zip(keys, refs[nb:2 * nb]))
        ssem, rsem, token = refs[2 * nb:]
        x, y, c = _pos()

        def forward(mine, others):
            for k, p in enumerate(pieces):
                for j in others:
                    view = _half_view(land[(_PIECES[p][1], layer)], _PIECES[p], j, c)
                    _remote(view, view, ssem.at[k * N_CHIPS + j], rsem.at[k * N_CHIPS + j], (x, y, 1 - c)).start()

        _for_my_chip(forward)
        token[...] = jnp.zeros(token.shape, token.dtype)

    sem = pltpu.SemaphoreType.DMA((N_CHIPS * len(pieces),))
    res = pl.pallas_call(
        body, name="forward_start_g%d" % g,
        in_specs=[HBM_SPEC] * nb,
        out_specs=[HBM_SPEC] * nb + [SEM_SPEC, SEM_SPEC, pl.BlockSpec(memory_space=pltpu.VMEM)],
        out_shape=[jax.ShapeDtypeStruct(a.shape, a.dtype) for a in bufs]
        + [sem, sem, jax.ShapeDtypeStruct((SUBLANES, LANES), F32)],
        input_output_aliases={t: t for t in range(nb)},
        compiler_params=_params(has_side_effects=_SPLIT_EFFECT),
    )(*bufs)
    return list(res[:nb]), res[nb], res[nb + 1], res[nb + 2]


def _forward_wait(g, bufs, ssem, rsem, after):
    layer, pieces = _GATHER_GROUPS[g]
    keys = _group_keys(g)
    nb = len(keys)

    def body(*refs):
        ssem_ref, rsem_ref = refs[nb], refs[nb + 1]
        land = dict(zip(keys, refs[nb + 3:]))
        x, y, c = _pos()

        def wait(mine, others):
            for k, p in enumerate(pieces):
                ref = land[(_PIECES[p][1], layer)]
                for j in others:
                    cp = _remote(_half_view(ref, _PIECES[p], j, c), _half_view(ref, _PIECES[p], j, 1 - c),
                                 ssem_ref.at[k * N_CHIPS + j], rsem_ref.at[k * N_CHIPS + j], (x, y, 1 - c))
                    cp.wait_send()
                    cp.wait_recv()

        _for_my_chip(wait)

    return pl.pallas_call(
        body, name="forward_wait_g%d" % g,
        in_specs=[HBM_SPEC] * nb + [SEM_SPEC, SEM_SPEC, pl.BlockSpec(memory_space=pl.ANY)],
        out_specs=[HBM_SPEC] * nb,
        out_shape=[jax.ShapeDtypeStruct(a.shape, a.dtype) for a in bufs],
        input_output_aliases={t: t for t in range(nb)},
        compiler_params=_params(has_side_effects=_SPLIT_EFFECT),
    )(*bufs, ssem, rsem, after)


_REDUCE_GROUPS = (
    ((5, 1), (3, 1), (4, 1), (2, 1), (1, 1), (0, 1)),
    ((5, 0), (3, 0), (4, 0)),
    ((2, 0), (1, 0)),
    ((0, 0),),
)


def _reduce_keys(group):
    keys = []
    for p, layer in group:
        if (_PIECES[p][1], layer) not in keys:
            keys.append((_PIECES[p][1], layer))
    return keys


def _half_block(piece, tm):
    _, _, rs, cs, rstep, cstep, coff = piece
    return lambda i, j, c: (j * (rstep // tm) + c * (rs // 2 // tm) + i, coff // cs + j * (cstep // cs))


def _swap_start(g, dwb):
    group = _REDUCE_GROUPS[g]
    keys = _reduce_keys(group)
    nk = len(keys)

    def body(*refs):
        src = dict(zip(keys, refs[nk:2 * nk]))
        dst = dict(zip(keys, refs[2 * nk:3 * nk]))
        ssem, rsem, token = refs[3 * nk:]
        x, y, c = _pos()
        for k, (p, layer) in enumerate(group):
            key = (_PIECES[p][1], layer)
            for j in range(N_CHIPS):
                _remote(_half_view(src[key], _PIECES[p], j, 1 - c), _half_view(dst[key], _PIECES[p], j, 1 - c),
                        ssem.at[k * N_CHIPS + j], rsem.at[k * N_CHIPS + j], (x, y, 1 - c)).start()
        token[...] = jnp.zeros(token.shape, token.dtype)

    sem = pltpu.SemaphoreType.DMA((N_CHIPS * len(group),))
    shapes = [jax.ShapeDtypeStruct(dwb[k].shape, BF16) for k in keys]
    res = pl.pallas_call(
        body, name="swap_start_g%d" % g,
        in_specs=[HBM_SPEC] * nk,
        out_specs=[HBM_SPEC] * (2 * nk) + [SEM_SPEC, SEM_SPEC, pl.BlockSpec(memory_space=pltpu.VMEM)],
        out_shape=shapes + shapes + [sem, sem, jax.ShapeDtypeStruct((SUBLANES, LANES), F32)],
        input_output_aliases={t: t for t in range(nk)},
        compiler_params=_params(has_side_effects=_SPLIT_EFFECT),
    )(*[pltpu.with_memory_space_constraint(dwb[k], pltpu.HBM) for k in keys])
    return list(res[:nk]), list(res[nk:2 * nk]), res[2 * nk], res[2 * nk + 1], res[2 * nk + 2]


def _swap_wait(g, own, land, ssem, rsem, after):
    group = _REDUCE_GROUPS[g]
    keys = _reduce_keys(group)
    nk = len(keys)

    def body(*refs):
        ssem_ref, rsem_ref = refs[2 * nk], refs[2 * nk + 1]
        src = dict(zip(keys, refs[2 * nk + 3:3 * nk + 3]))
        dst = dict(zip(keys, refs[3 * nk + 3:]))
        x, y, c = _pos()
        for k, (p, layer) in enumerate(group):
            key = (_PIECES[p][1], layer)
            for j in range(N_CHIPS):
                cp = _remote(_half_view(src[key], _PIECES[p], j, 1 - c), _half_view(dst[key], _PIECES[p], j, c),
                             ssem_ref.at[k * N_CHIPS + j], rsem_ref.at[k * N_CHIPS + j], (x, y, 1 - c))
                cp.wait_send()
                cp.wait_recv()

    res = pl.pallas_call(
        body, name="swap_wait_g%d" % g,
        in_specs=[HBM_SPEC] * (2 * nk) + [SEM_SPEC, SEM_SPEC, pl.BlockSpec(memory_space=pl.ANY)],
        out_specs=[HBM_SPEC] * (2 * nk),
        out_shape=[jax.ShapeDtypeStruct(a.shape, a.dtype) for a in list(own) + list(land)],
        input_output_aliases={t: t for t in range(2 * nk)},
        compiler_params=_params(has_side_effects=_SPLIT_EFFECT),
    )(*own, *land, ssem, rsem, after)
    return dict(zip(keys, res[nk:]))


def _chip_partial(piece, dw, got, prev, c_arr, name):
    _, _, rs, cs, _, _, _ = piece
    half = rs // 2
    tm = _tile(half, 256)
    blk = _half_block(piece, tm)

    def body(s_ref, dw_ref, got_ref, *rest):
        rest[-1][...] = (dw_ref[...] + got_ref[...].astype(F32)).astype(BF16)

    spec = pl.BlockSpec((tm, cs), lambda j, i, s: blk(i, j, s[0]))
    in_specs = [spec, spec]
    args = [dw, got]
    aliases = {}
    if prev is not None:
        in_specs.append(pl.BlockSpec(memory_space=pl.ANY))
        args.append(prev)
        aliases = {3: 0}
    return pl.pallas_call(
        body, name=name,
        grid_spec=pltpu.PrefetchScalarGridSpec(
            num_scalar_prefetch=1, grid=(N_CHIPS, half // tm), in_specs=in_specs, out_specs=spec),
        out_shape=jax.ShapeDtypeStruct(dw.shape, BF16), input_output_aliases=aliases,
        compiler_params=_params(("parallel", "parallel")),
    )(c_arr, *args)


def _scatter_start(g, partials):
    group = _REDUCE_GROUPS[g]
    keys = _reduce_keys(group)
    nk, n = len(keys), len(group)

    def body(*refs):
        pt = dict(zip(keys, refs[nk:2 * nk]))
        land = refs[2 * nk:2 * nk + n]
        ssem, rsem, token = refs[2 * nk + n:]
        _, _, c = _pos()

        def send(mine, others):
            for k, (p, layer) in enumerate(group):
                for j in others:
                    _remote(_half_view(pt[(_PIECES[p][1], layer)], _PIECES[p], j, c), land[k].at[mine],
                            ssem.at[k * N_CHIPS + j], rsem.at[k * N_CHIPS + mine], (j // 2, j % 2, c)).start()

        _for_my_chip(send)
        token[...] = jnp.zeros(token.shape, token.dtype)

    sem = pltpu.SemaphoreType.DMA((N_CHIPS * n,))
    res = pl.pallas_call(
        body, name="scatter_start_g%d" % g,
        in_specs=[HBM_SPEC] * nk,
        out_specs=[HBM_SPEC] * (nk + n) + [SEM_SPEC, SEM_SPEC, pl.BlockSpec(memory_space=pltpu.VMEM)],
        out_shape=([jax.ShapeDtypeStruct(partials[k].shape, BF16) for k in keys]
                   + [jax.ShapeDtypeStruct((N_CHIPS, _PIECES[p][2] // 2, _PIECES[p][3]), BF16) for p, _ in group]
                   + [sem, sem, jax.ShapeDtypeStruct((SUBLANES, LANES), F32)]),
        input_output_aliases={t: t for t in range(nk)},
        compiler_params=_params(has_side_effects=_SPLIT_EFFECT),
    )(*[pltpu.with_memory_space_constraint(partials[k], pltpu.HBM) for k in keys])
    return list(res[:nk]), list(res[nk:nk + n]), res[nk + n], res[nk + n + 1], res[nk + n + 2]


def _scatter_wait(g, partials, land, ssem, rsem, after):
    group = _REDUCE_GROUPS[g]
    keys = _reduce_keys(group)
    nk, n = len(keys), len(group)

    def body(*refs):
        ssem_ref, rsem_ref = refs[nk + n], refs[nk + n + 1]
        pt = dict(zip(keys, refs[nk + n + 3:2 * nk + n + 3]))
        land_ref = refs[2 * nk + n + 3:]
        _, _, c = _pos()

        def wait(mine, others):
            for k, (p, layer) in enumerate(group):
                for j in others:
                    cp = _remote(_half_view(pt[(_PIECES[p][1], layer)], _PIECES[p], j, c), land_ref[k].at[j],
                                 ssem_ref.at[k * N_CHIPS + j], rsem_ref.at[k * N_CHIPS + j], (j // 2, j % 2, c))
                    cp.wait_send()
                    cp.wait_recv()

        _for_my_chip(wait)

    res = pl.pallas_call(
        body, name="scatter_wait_g%d" % g,
        in_specs=[HBM_SPEC] * (nk + n) + [SEM_SPEC, SEM_SPEC, pl.BlockSpec(memory_space=pl.ANY)],
        out_specs=[HBM_SPEC] * (nk + n),
        out_shape=[jax.ShapeDtypeStruct(a.shape, a.dtype) for a in list(partials) + list(land)],
        input_output_aliases={t: t for t in range(nk + n)},
        compiler_params=_params(has_side_effects=_SPLIT_EFFECT),
    )(*partials, *land, ssem, rsem, after)
    return list(res[nk:])


def _reduce_half(piece, layer, dw, got, land, prev, idx, name):
    _, _, rs, cs, _, _, _ = piece
    half = rs // 2
    tm = _tile(half, 256)
    blk = _half_block(piece, tm)

    def body(s_ref, dw_ref, got_ref, r1, r2, r3, *rest):
        acc = dw_ref[...] + got_ref[...].astype(F32)
        for r in (r1, r2, r3):
            acc = acc + r[...].astype(F32)
        rest[-1][...] = acc

    def land_map(k):
        return lambda i, s: ((s[1] + k) % N_CHIPS, i, 0)

    own = pl.BlockSpec((tm, cs), lambda i, s: blk(i, s[1], s[0]))
    in_specs = [own, own] + [pl.BlockSpec((None, tm, cs), land_map(k)) for k in (1, 2, 3)]
    args = [dw, got, land, land, land]
    aliases = {}
    if prev is not None:
        in_specs.append(pl.BlockSpec(memory_space=pl.ANY))
        args.append(prev)
        aliases = {6: 0}
    return pl.pallas_call(
        body, name=name,
        grid_spec=pltpu.PrefetchScalarGridSpec(
            num_scalar_prefetch=1, grid=(half // tm,), in_specs=in_specs,
            out_specs=pl.BlockSpec((None, tm, cs), lambda i, s: (layer, s[0] * (half // tm) + i, 0))),
        out_shape=jax.ShapeDtypeStruct((DEPTH, rs, cs), F32), input_output_aliases=aliases,
        compiler_params=_params(("parallel",)),
    )(idx, *args)


def _share_halves(reduced):
    def body(*refs):
        buf = refs[N_PIECES:2 * N_PIECES]
        ssem, rsem = refs[2 * N_PIECES:]
        x, y, c = _pos()

        def half(p, layer, cc):
            rows = _PIECES[p][2] // 2
            return buf[p].at[layer, pl.ds(pl.multiple_of(cc * rows, SUBLANES), rows), :]

        pairs = [(p, layer) for p in range(N_PIECES) for layer in range(DEPTH)]
        rem = [_remote(half(p, layer, c), half(p, layer, c), ssem.at[k], rsem.at[k], (x, y, 1 - c))
               for k, (p, layer) in enumerate(pairs)]
        for cp in rem:
            cp.start()
        for k, (p, layer) in enumerate(pairs):
            rem[k].wait_send()
            _remote(half(p, layer, 1 - c), half(p, layer, 1 - c), ssem.at[k], rsem.at[k], (x, y, 1 - c)).wait_recv()

    nsem = N_PIECES * DEPTH
    return pl.pallas_call(
        body, name="share_halves",
        in_specs=[HBM_SPEC] * N_PIECES, out_specs=[HBM_SPEC] * N_PIECES,
        out_shape=[jax.ShapeDtypeStruct((DEPTH, p[2], p[3]), F32) for p in _PIECES],
        input_output_aliases={t: t for t in range(N_PIECES)},
        scratch_shapes=[pltpu.SemaphoreType.DMA((nsem,)), pltpu.SemaphoreType.DMA((nsem,))],
        compiler_params=_params(has_side_effects=True),
    )(*reduced)


N_DEV = 8


def _place_slot(v, me_arr, take_block):
    rows = v.shape[0] // N_DEV if take_block else v.shape[0]
    tm = _tile(rows, 512)
    steps = rows // tm

    def body(s_ref, v_ref, out_ref):
        out_ref[...] = v_ref[...]

    return pl.pallas_call(
        body, name="place_small_block" if take_block else "place_small_sum",
        grid_spec=pltpu.PrefetchScalarGridSpec(
            num_scalar_prefetch=1, grid=(steps,),
            in_specs=[pl.BlockSpec((tm, LANES), lambda i, s: (s[0] * steps * take_block + i, 0))],
            out_specs=pl.BlockSpec((None, tm, LANES), lambda i, s: (s[0], i, 0))),
        out_shape=jax.ShapeDtypeStruct((N_DEV, rows, LANES), F32),
        compiler_params=_params(("parallel",)),
    )(me_arr, v)


def _all_peers():
    x, y, c = _pos()
    flip = lambda v, f: 1 - v if f else v
    return (x, y, c), [(flip(x, a), flip(y, b), flip(c, d))
                       for a in (0, 1) for b in (0, 1) for d in (0, 1) if a or b or d]


def _slot_index(dev):
    return 4 * dev[0] + 2 * dev[1] + dev[2]


def _exchange_start(g, src, name):
    rows = g.shape[1]
    n_in = 1 if src is None else 2

    def body(*refs):
        g_ref = refs[n_in]
        src_ref = refs[n_in + 1] if src is not None else None
        ssem, rsem, token = refs[2 * n_in:]
        me, peers = _all_peers()
        for k, dev in enumerate(peers):
            if src is None:
                mine = g_ref.at[_slot_index(me)]
            else:
                mine = src_ref.at[pl.ds(pl.multiple_of(_slot_index(dev) * rows, SUBLANES), rows), :]
            _remote(mine, g_ref.at[_slot_index(me)], ssem.at[k], rsem.at[k], dev).start()
        token[...] = jnp.zeros(token.shape, token.dtype)

    sem = pltpu.SemaphoreType.DMA((N_DEV - 1,))
    args = [g] if src is None else [g, src]
    res = pl.pallas_call(
        body, name=name,
        in_specs=[HBM_SPEC] * n_in,
        out_specs=[HBM_SPEC] * n_in + [SEM_SPEC, SEM_SPEC, pl.BlockSpec(memory_space=pltpu.VMEM)],
        out_shape=[jax.ShapeDtypeStruct(a.shape, a.dtype) for a in args]
        + [sem, sem, jax.ShapeDtypeStruct((SUBLANES, LANES), F32)],
        input_output_aliases={t: t for t in range(n_in)},
        compiler_params=_params(has_side_effects=_SPLIT_EFFECT),
    )(*[pltpu.with_memory_space_constraint(a, pltpu.HBM) for a in args])
    return list(res[:n_in]), res[n_in], res[n_in + 1], res[n_in + 2]


def _exchange_wait(bufs, ssem, rsem, after, name):
    n_in = len(bufs)

    def body(*refs):
        ssem_ref, rsem_ref = refs[n_in], refs[n_in + 1]
        g_ref = refs[n_in + 3]
        me, peers = _all_peers()
        for k, dev in enumerate(peers):
            cp = _remote(g_ref.at[_slot_index(me)], g_ref.at[_slot_index(dev)], ssem_ref.at[k], rsem_ref.at[k], dev)
            cp.wait_send()
            cp.wait_recv()

    res = pl.pallas_call(
        body, name=name,
        in_specs=[HBM_SPEC] * n_in + [SEM_SPEC, SEM_SPEC, pl.BlockSpec(memory_space=pl.ANY)],
        out_specs=[HBM_SPEC] * n_in,
        out_shape=[jax.ShapeDtypeStruct(a.shape, a.dtype) for a in bufs],
        input_output_aliases={t: t for t in range(n_in)},
        compiler_params=_params(has_side_effects=_SPLIT_EFFECT),
    )(*bufs, ssem, rsem, after)
    return res[0]


def _sum_slots(g, name):
    n, rows, _ = g.shape
    tm = _tile(rows, 512)

    def body(g_ref, out_ref):
        acc = g_ref[0]
        for k in range(1, n):
            acc = acc + g_ref[k]
        out_ref[...] = acc

    return pl.pallas_call(
        body, name=name, grid=(rows // tm,),
        in_specs=[pl.BlockSpec((n, tm, LANES), lambda i: (0, i, 0))],
        out_specs=pl.BlockSpec((tm, LANES), lambda i: (i, 0)),
        out_shape=jax.ShapeDtypeStruct((rows, LANES), F32),
        compiler_params=_params(("parallel",)),
    )(g)


_TINY = ("ln_mix_g", "ret_log_gamma", "ssm_a_re", "ssm_a_im", "ssm_log_dt", "ssm_d", "b_glu", "ln_ffn_g", "ln_final_g")
_MID = ("ssm_b_re", "ssm_b_im", "ssm_c_re", "ssm_c_im")
_SMALL = _TINY + _MID
_FLAT_ALIGN = LANES * LANES
_FLAT_ROWS = 1024


def _flat_rows(like, names):
    rows = sum((math.prod(like[n].shape) + (-math.prod(like[n].shape)) % _FLAT_ALIGN) // LANES for n in names)
    return rows + (-rows) % _FLAT_ROWS


def _flatten(d, names):
    parts = []
    for n in names:
        f = d[n].reshape(-1)
        parts.append(jnp.pad(f, (0, (-f.shape[0]) % _FLAT_ALIGN)))
    total = sum(p.shape[0] for p in parts)
    parts.append(jnp.zeros(((-total) % (_FLAT_ROWS * LANES),), F32))
    return jnp.concatenate(parts).reshape(-1, LANES)


def _unflatten(flat, like, names):
    out, row = {}, 0
    for n in names:
        size = math.prod(like[n].shape)
        rows = (size + (-size) % _FLAT_ALIGN) // LANES
        part = lax.optimization_barrier(flat[row:row + rows])
        out[n] = part.reshape(-1)[:size].reshape(like[n].shape)
        row += rows
    return out


_BIG = ("w_in", "w_glu", "w_out", "w_ffn_gate", "w_ffn_up", "w_ffn_down")
_WEIGHTS = ("ln_mix_g", "w_in", "ret_log_gamma", "ssm_a_re", "ssm_a_im", "ssm_log_dt", "ssm_b_re", "ssm_b_im",
            "ssm_c_re", "ssm_c_im", "ssm_d", "w_glu", "b_glu", "w_out", "ln_ffn_g", "w_ffn_gate", "w_ffn_up",
            "w_ffn_down", "ln_final_g")


def _rope_tables(seq):
    half = QK_DIM // 2
    inv = 1.0 / (ROPE_BASE ** (jnp.arange(half, dtype=F32) / half))
    ang = jnp.arange(seq, dtype=F32)[:, None] * inv[None, :]
    return jnp.cos(ang), jnp.sin(ang)


def _step(w, m, v, x, target):
    seq = x.shape[0]
    seg_len = float(seq // SEGMENTS)
    c_idx = lax.axis_index("c").astype(jnp.int32)
    chip_idx = (2 * lax.axis_index("x") + lax.axis_index("y")).astype(jnp.int32)
    c_arr = jnp.stack([c_idx])
    idx_arr = jnp.stack([c_idx, chip_idx])

    chip_arr = jnp.stack([chip_idx])
    placed = {}

    def cast(pieces, layer):
        for p in pieces:
            key = (_PIECES[p][1], layer)
            placed[key] = _cast_place(_PIECES[p], w[_PIECES[p][0]], layer, placed.get(key), chip_arr,
                                      "cast_%s_l%d" % (_PIECES[p][0], layer))

    for layer, pieces in _GATHER_GROUPS:
        cast(pieces, layer)
    flying, ssems, rsems, token = _gather_start(list(range(len(_GATHER_GROUPS))), placed)
    wf = {b[0]: [None] * DEPTH for b in _BUFFERS}

    handing = {}

    def arrive(g, after):
        ks = _group_keys(g)
        landed = _gather_wait(g, [flying[k] for k in ks], ssems[g], rsems[g], after)
        for k, a in zip(ks, _gather_forward(g, landed)):
            wf[k[0]][k[1]] = a

    def hand_over(g, after):
        ks = _group_keys(g)
        landed = _gather_wait(g, [flying[k] for k in ks], ssems[g], rsems[g], after)
        bufs, fs, fr, tok = _forward_start(g, landed)
        handing[g] = (bufs, fs, fr)
        return tok[0:1, 0:1]

    def complete(g, after):
        for k, a in zip(_group_keys(g), _forward_wait(g, *handing[g], after)):
            wf[k[0]][k[1]] = a

    cos, sin = _rope_tables(seq)

    started = token[0, 0]
    s5_ops, s5_vjps = [], []
    for i in range(DEPTH):
        s5_raw = (w["ssm_a_re"][i] + started, w["ssm_a_im"][i], w["ssm_log_dt"][i], w["ssm_b_re"][i], w["ssm_b_im"][i])
        disc, disc_vjp = jax.vjp(functools.partial(_s5_discretize, seg_len=seg_len), *s5_raw)
        bblk, cblk, lam = _s5_pack(*disc, w["ssm_c_re"][i] + started, w["ssm_c_im"][i] + started)
        s5_ops.append((bblk.astype(BF16), cblk.astype(BF16), lam))
        s5_vjps.append(disc_vjp)
    tiny_flat = [_flatten({**d, "ln_final_g": d["ln_final_g"] + started}, _TINY) for d in (w, m, v)]
    corner = lambda a: a[(0,) * (a.ndim - 2)][0:1, 0:1].astype(F32)
    prepared = sum(corner(a) for ops in s5_ops for a in ops) + sum(corner(a) for a in tiny_flat) + corner(cos) + corner(sin)

    saved = []
    xc = x + token[0, 0]
    for i in range(DEPTH):
        t = "_l%d" % i
        s = {"x_in": xc}
        if i == 0:
            s["h"] = _rms_fwd(xc, w["ln_mix_g"][i:i + 1], "rms_mix" + t)
            arrive(0, prepared + corner(s["h"]))
        else:
            s["h"] = _rms_fwd(xc, w["ln_mix_g"][i:i + 1] + next_in, "rms_mix" + t)
        s["proj"] = _matmul(s["h"], wf["in"][i], "nn", [F32], name="mm_in" + t)[0]
        s["qr"], s["kr"] = _rot_fwd(s["proj"], cos, sin, "rot" + t)
        s["lg"] = jnp.broadcast_to(w["ret_log_gamma"][i].T[:, :, None], (HEADS, 2, LANES))
        s["y"] = _ret_fwd(s["qr"], s["kr"], s["proj"], s["lg"], "ret" + t)
        s["s5"], s["disc_vjp"] = s5_ops[i], s5_vjps[i]
        s["s5y"] = _s5_fwd(s["proj"], *s["s5"], "s5" + t)
        first = GROUPS_PER_LAYER * i
        d_skip = w["ssm_d"][i:i + 1] + hand_over(first + 1, s["s5y"])
        s["ret"], s["ysg"], s["ysgb"] = _post1_fwd(s["y"], s["proj"], s["s5y"], d_skip, "post" + t)
        complete(first + 1, s["ysgb"])
        s["z"] = _matmul(s["ysgb"], wf["glu"][i], "nn", [F32], name="mm_glu" + t)[0]
        b_glu = w["b_glu"][i:i + 1] + hand_over(first + 2, s["z"])
        s["merged"] = _merge_fwd(s["z"], s["ysg"], s["proj"], s["ret"], b_glu, "merge" + t)
        s["x1"] = _matmul(s["merged"], wf["out"][i], "nn", [F32], add=xc, name="mm_out" + t)[0]
        s["h2"] = _rms_fwd(s["x1"], w["ln_ffn_g"][i:i + 1], "rms_ffn" + t)
        complete(first + 2, s["h2"])
        s["ab"] = _matmul(s["h2"], wf["gu"][i], "nn", [F32], name="mm_gu" + t)[0]
        if i + 1 < DEPTH:
            next_in = hand_over(first + GROUPS_PER_LAYER, s["ab"])
        s["f"] = _glu_fwd(s["ab"], "glu" + t)
        xc = _matmul(s["f"], wf["down"][i], "nn", [F32], add=s["x1"], name="mm_down" + t)[0]
        if i + 1 < DEPTH:
            complete(first + GROUPS_PER_LAYER, xc)
        saved.append(s)

    dx, dxb, loss_row, dg_final = _loss_stage(xc, target, w["ln_final_g"][None, :], "loss")
    loss = lax.psum(loss_row[0, 0], ("x", "y", "c"))

    g_small = {"ln_final_g": dg_final[0]}
    per_layer = {n: [None] * DEPTH for n in _SMALL if n != "ln_final_g"}
    dws, got, swaps, flights = {}, {}, {}, []

    def dw_mm(a, b, buf, i, name):
        dws[(buf, i)] = _matmul(a, b, "tn", [F32, BF16], name=name)

    def depart(g):
        keys = _reduce_keys(_REDUCE_GROUPS[g])
        own, land, ssem, rsem, tok = _swap_start(g, {k: dws[k][1] for k in keys})
        swaps[g] = (own, land, ssem, rsem)
        return tok[0:1, 0:1]

    def proceed(g, after):
        group = _REDUCE_GROUPS[g]
        got.update(_swap_wait(g, *swaps[g], after))
        partials = {}
        for p, layer in group:
            key = (_PIECES[p][1], layer)
            partials[key] = _chip_partial(_PIECES[p], dws[key][0], got[key], partials.get(key), c_arr,
                                          "chip_partial_%s_l%d" % (_PIECES[p][0], layer))
        pt, land, ssem, rsem, tok = _scatter_start(g, partials)
        flights.append((g, pt, land, ssem, rsem))
        return tok[0:1, 0:1]

    for i in reversed(range(DEPTH)):
        t = "_l%d" % i
        s = saved[i]
        g_ffn, g_mix, d_skip = w["ln_ffn_g"][i:i + 1], w["ln_mix_g"][i:i + 1], w["ssm_d"][i:i + 1]
        dw_mm(s["f"], dxb, "down", i, "dw_down" + t)
        df = _matmul(dxb, wf["down"][i], "nt", [F32], name="dx_down" + t)[0]
        if i == 0:
            g_ffn = g_ffn + proceed(0, df)
        dab = _glu_bwd(s["ab"], df, "glu_bwd" + t)
        dw_mm(s["h2"], dab, "gu", i, "dw_gu" + t)
        if i == 0:
            g_ffn = g_ffn + depart(1)
        dh2 = _matmul(dab, wf["gu"][i], "nt", [F32], name="dx_gu" + t)[0]
        if i == 0:
            g_ffn = g_ffn + proceed(1, dh2)
        dx1, dx1b, dg = _rms_bwd(s["x1"], dh2, dx, g_ffn, "rms_ffn_bwd" + t)
        per_layer["ln_ffn_g"][i] = dg[0]

        dw_mm(s["merged"], dx1b, "out", i, "dw_out" + t)
        dmerged = _matmul(dx1b, wf["out"][i], "nt", [F32], name="dx_out" + t)[0]
        dz, dys_part, dgs, db = _merge_bwd(s["z"], s["ysg"], s["proj"], s["ret"], dmerged, w["b_glu"][i:i + 1],
                                           "merge_bwd" + t)
        per_layer["b_glu"][i] = db[0]
        dw_mm(s["ysgb"], dz, "glu", i, "dw_glu" + t)
        if i == 0:
            d_skip = d_skip + depart(2)
        dys = _matmul(dz, wf["glu"][i], "nt", [F32], add=dys_part, name="dx_glu" + t)[0]
        if i == 0:
            d_skip = d_skip + proceed(2, dys)
        dy, dgg, dgr, ds5, du_part, dd = _post1_bwd(s["y"], s["proj"], s["s5y"], dmerged, dys,
                                                    d_skip, "post_bwd" + t)
        per_layer["ssm_d"][i] = dd[0]
        du, dbblk, dcblk, dlam = _s5_bwd(s["proj"], ds5, du_part, *s["s5"], "s5_bwd" + t)
        dlr, dli, dbr, dbi, dcr, dci = _s5_unpack(dbblk, dcblk, dlam)
        zeros = jnp.zeros_like(dlr)
        da_re, da_im, dlog_dt, db_re, db_im = s["disc_vjp"]((dlr, dli, zeros, zeros, dbr, dbi))
        for n, val in (("ssm_a_re", da_re), ("ssm_a_im", da_im), ("ssm_log_dt", dlog_dt), ("ssm_b_re", db_re),
                       ("ssm_b_im", db_im), ("ssm_c_re", dcr), ("ssm_c_im", dci)):
            per_layer[n][i] = val
        dqr, dkr, dv, dlg = _ret_bwd(s["qr"], s["kr"], s["proj"], dy, s["lg"], "ret_bwd" + t)
        per_layer["ret_log_gamma"][i] = dlg[:, :, 0].T
        dqkv = _rot_bwd(dqr, dkr, dv, cos, sin, "rot_bwd" + t)
        dproj = jnp.concatenate([dqkv, dgg, du, dgr, dgs], axis=1)
        dw_mm(s["h"], dproj, "in", i, "dw_in" + t)
        if i == 0:
            g_mix = g_mix + depart(3)
        dh = _matmul(dproj, wf["in"][i], "nt", [F32], name="dx_in" + t)[0]
        if i == 0:
            g_mix = g_mix + proceed(3, dh)
        dx, dxb, dg = _rms_bwd(s["x_in"], dh, dx1, g_mix, "rms_mix_bwd" + t)
        per_layer["ln_mix_g"][i] = dg[0]
        if i == DEPTH - 1:
            dxb = dxb + depart(0).astype(BF16)

    for n in per_layer:
        g_small[n] = jnp.stack(per_layer[n])
    me_arr = jnp.stack([2 * chip_idx + c_idx])
    g_mine = _flatten(g_small, _SMALL)
    rs_bufs, rs_ssem, rs_rsem, small_token = _exchange_start(_place_slot(g_mine, me_arr, True), g_mine,
                                                             "small_scatter_start")

    reduced = [None] * N_PIECES
    before = small_token
    for g, pt, land, ssem, rsem in flights:
        landed = _scatter_wait(g, pt, land, ssem, rsem, before)
        for (p, layer), buf in zip(_REDUCE_GROUPS[g], landed):
            key = (_PIECES[p][1], layer)
            reduced[p] = _reduce_half(_PIECES[p], layer, dws[key][0], got[key], buf, reduced[p], idx_arr,
                                      "reduce_%s_l%d" % (_PIECES[p][0], layer))
            before = reduced[p]
    g_big = dict(zip([p[0] for p in _PIECES], _share_halves(reduced)))

    landed = _exchange_wait(rs_bufs, rs_ssem, rs_rsem, g_big[_BIG[-1]], "small_scatter_wait")
    ag_bufs, ag_ssem, ag_rsem, ag_token = _exchange_start(
        _place_slot(_sum_slots(landed, "sum_small"), me_arr, False), None, "small_gather_start")

    grads, delta, new_m, new_v = {}, {}, {}, {}
    for n in _BIG:
        d, r, cc = w[n].shape
        two_d = lambda a: a.reshape(d * r, cc)
        dl, mn, vn = _adamw(two_d(w[n]), two_d(g_big[n]), two_d(m[n]), two_d(v[n]), "adamw_" + n, after=ag_token)
        grads[n], delta[n], new_m[n], new_v[n] = g_big[n], dl.reshape(d, r, cc), mn.reshape(d, r, cc), vn.reshape(d, r, cc)

    all_done = sum(corner(delta[n]) for n in _BIG)
    gathered = _exchange_wait(ag_bufs, ag_ssem, ag_rsem, all_done, "small_gather_wait")
    g_flat = gathered.reshape(-1, LANES)
    grads.update(_unflatten(g_flat, w, _SMALL))
    tiny_rows = _flat_rows(w, _TINY)
    dl, mn, vn = _adamw(tiny_flat[0], g_flat[:tiny_rows], tiny_flat[1], tiny_flat[2], "adamw_tiny")
    for dst, flat in ((delta, dl), (new_m, mn), (new_v, vn)):
        dst.update(_unflatten(flat, w, _TINY))
    for n in _MID:
        delta[n], new_m[n], new_v[n] = _adamw_nd(w[n], grads[n], m[n], v[n], "adamw_" + n)
    return loss, dx, grads, delta, new_m, new_v


def kernel(x, ln_mix_g, w_in, ret_log_gamma, ssm_a_re, ssm_a_im, ssm_log_dt, ssm_b_re, ssm_b_im, ssm_c_re, ssm_c_im, ssm_d, w_glu, b_glu, w_out, ln_ffn_g, w_ffn_gate, w_ffn_up, w_ffn_down, ln_final_g, loss_target, m_ln_mix_g, m_w_in, m_ret_log_gamma, m_ssm_a_re, m_ssm_a_im, m_ssm_log_dt, m_ssm_b_re, m_ssm_b_im, m_ssm_c_re, m_ssm_c_im, m_ssm_d, m_w_glu, m_b_glu, m_w_out, m_ln_ffn_g, m_w_ffn_gate, m_w_ffn_up, m_w_ffn_down, m_ln_final_g, v_ln_mix_g, v_w_in, v_ret_log_gamma, v_ssm_a_re, v_ssm_a_im, v_ssm_log_dt, v_ssm_b_re, v_ssm_b_im, v_ssm_c_re, v_ssm_c_im, v_ssm_d, v_w_glu, v_b_glu, v_w_out, v_ln_ffn_g, v_w_ffn_gate, v_w_ffn_up, v_w_ffn_down, v_ln_final_g):
    given = dict(locals())
    w = {n: given[n] for n in _WEIGHTS}
    m = {n: given["m_" + n] for n in _WEIGHTS}
    v = {n: given["v_" + n] for n in _WEIGHTS}
    loss, dx, grads, delta, new_m, new_v = _step(w, m, v, x[0], loss_target[0])
    return (loss, dx[None], *[grads[n] for n in _WEIGHTS], *[delta[n] for n in _WEIGHTS],
            *[new_m[n] for n in _WEIGHTS], *[new_v[n] for n in _WEIGHTS])
```

```python
import functools
import math

import jax
import jax.numpy as jnp
from jax import lax
from jax.experimental import pallas as pl
from jax.experimental.pallas import tpu as pltpu

F32 = jnp.float32
BF16 = jnp.bfloat16

D_MODEL = 2048
DEPTH = 2
HEADS = 4
QK_DIM = 256
V_DIM = 512
QK_WIDTH = HEADS * QK_DIM
ROPE_BASE = 10000.0
GROUP = 16
N_GROUPS = D_MODEL // GROUP
N_STATE = 64
D_FF = 5632
IN_WIDTH = 2 * QK_WIDTH + 5 * D_MODEL
EPS = 1e-6
N_CHIPS = 4

ADAM_LR = 0.001
ADAM_B1 = 0.9
ADAM_B2 = 0.999
ADAM_EPS = 1e-08
ADAM_WD = 0.01
ADAM_STEP = 10

LANES = 128
SUBLANES = 8
VMEM_LIMIT = 56 * 1024 * 1024
SEGMENTS = SUBLANES
GROUPS_PER_TILE = LANES // GROUP
STATE_COLS = GROUPS_PER_TILE * N_STATE
N_TILES = D_MODEL // LANES
SCAN_UNROLL = 8

MESH = pl.DeviceIdType.MESH
HBM_SPEC = pl.BlockSpec(memory_space=pltpu.HBM)


def _params(sem=None, **kw):
    return pltpu.CompilerParams(dimension_semantics=sem, vmem_limit_bytes=VMEM_LIMIT, **kw)


def _tile(n, cap=1024):
    for t in (2048, 1024, 512, 256, 128, 64):
        if t <= cap and n % t == 0:
            return t
    raise ValueError(n)


def _rows_call(fn, rows, pars, row_outs, par_outs, *, tm, name):
    m = rows[0][0].shape[0]
    nr, npar, nro, npo = len(rows), len(pars), len(row_outs), len(par_outs)

    def body(*refs):
        rin = refs[:nr]
        pin = refs[nr:nr + npar]
        rout = refs[nr + npar:nr + npar + nro]
        pout = refs[nr + npar + nro:]
        res = fn(*[r[...] for r in rin], *[p[...] for p in pin])
        if not isinstance(res, (tuple, list)):
            res = (res,)
        for r, v in zip(rout, res[:nro]):
            r[...] = v.astype(r.dtype)
        if npo:
            @pl.when(pl.program_id(0) == 0)
            def _():
                for p in pout:
                    p[...] = jnp.zeros(p.shape, p.dtype)
            for p, v in zip(pout, res[nro:]):
                p[...] += v

    in_specs = [pl.BlockSpec((tm, w), functools.partial(lambda cb, i: (i, cb), cb)) for (_, w, cb) in rows]
    in_specs += [pl.BlockSpec(p.shape, lambda i: (0, 0)) for p in pars]
    out_specs = [pl.BlockSpec((tm, w), lambda i: (i, 0)) for (w, _) in row_outs]
    out_specs += [pl.BlockSpec(s, lambda i: (0, 0)) for s in par_outs]
    out_shape = [jax.ShapeDtypeStruct((m, w), dt) for (w, dt) in row_outs]
    out_shape += [jax.ShapeDtypeStruct(s, F32) for s in par_outs]
    res = pl.pallas_call(
        body, name=name, grid=(m // tm,), in_specs=in_specs, out_specs=out_specs, out_shape=out_shape,
        compiler_params=_params(("arbitrary",) if npo else ("parallel",)),
    )(*[a for (a, _, _) in rows], *pars)
    return res


def _f32(*vals):
    return [v.astype(F32) for v in vals]


def _f_rms(x, g):
    r = lax.rsqrt(jnp.mean(x * x, axis=-1, keepdims=True) + EPS)
    return x * r * g


def _rms_fwd(x, g, name):
    return _rows_call(lambda xv, gv: _f_rms(xv, gv), [(x, D_MODEL, 0)], [g], [(D_MODEL, BF16)], [],
                      tm=256, name=name)[0]


def _rms_bwd(x, dh, dres, g, name):
    def fn(xv, dhv, drv, gv):
        _, vjp = jax.vjp(_f_rms, xv, gv)
        dx, dg = vjp(dhv)
        dx = dx + drv
        return dx, dx, dg
    return _rows_call(fn, [(x, D_MODEL, 0), (dh, D_MODEL, 0), (dres, D_MODEL, 0)], [g],
                      [(D_MODEL, F32), (D_MODEL, BF16)], [(1, D_MODEL)], tm=256, name=name)


def _rot_heads(xv, cos, sin, scale):
    half = QK_DIM // 2
    outs = []
    for h in range(HEADS):
        x1 = xv[:, h * QK_DIM:h * QK_DIM + half]
        x2 = xv[:, h * QK_DIM + half:(h + 1) * QK_DIM]
        outs += [(x1 * cos - x2 * sin) * scale, (x1 * sin + x2 * cos) * scale]
    return jnp.concatenate(outs, axis=1)


def _rot_fwd(proj, cos, sin, name):
    def fn(q, k, cv, sv):
        return _rot_heads(q, cv, sv, 1.0), _rot_heads(k, cv, sv, QK_DIM ** -0.5)
    return _rows_call(fn, [(proj, QK_WIDTH, 0), (proj, QK_WIDTH, 1), (cos, LANES, 0), (sin, LANES, 0)], [],
                      [(QK_WIDTH, BF16), (QK_WIDTH, BF16)], [], tm=256, name=name)


def _rot_bwd(dqr, dkr, dv, cos, sin, name):
    def fn(dq, dk, dvv, cv, sv):
        return jnp.concatenate([_rot_heads(dq, cv, -sv, 1.0), _rot_heads(dk, cv, -sv, QK_DIM ** -0.5), dvv], axis=1)
    return _rows_call(fn, [(dqr, QK_WIDTH, 0), (dkr, QK_WIDTH, 0), (dv, D_MODEL, 0), (cos, LANES, 0), (sin, LANES, 0)],
                      [], [(2 * QK_WIDTH + D_MODEL, BF16)], [], tm=256, name=name)[0]


def _f_post1(y0, y1, y2, y3, g, gr, s5, u, dsk):
    yn = [yh * lax.rsqrt(jnp.mean(yh * yh, axis=-1, keepdims=True) + EPS) for yh in (y0, y1, y2, y3)]
    ret = jax.nn.sigmoid(gr) * (jax.nn.silu(g) * jnp.concatenate(yn, axis=1))
    ysg = jax.nn.gelu(s5 + dsk * u)
    return ret, ysg


def _post1_rows(y, proj, s5y):
    rows = [(y, V_DIM, h) for h in range(HEADS)]
    rows += [(proj, D_MODEL, 2), (proj, D_MODEL, 4), (s5y, D_MODEL, 0), (proj, D_MODEL, 3)]
    return rows


def _post1_fwd(y, proj, s5y, dsk, name):
    def fn(*vals):
        ret, ysg = _f_post1(*vals)
        return ret, ysg, ysg
    return _rows_call(fn, _post1_rows(y, proj, s5y), [dsk],
                      [(D_MODEL, F32), (D_MODEL, F32), (D_MODEL, BF16)], [], tm=128, name=name)


def _post1_bwd(y, proj, s5y, dret, dys, dsk, name):
    def fn(*vals):
        prim = vals[:8] + (vals[10],)
        _, vjp = jax.vjp(_f_post1, *prim)
        gy0, gy1, gy2, gy3, gg, ggr, gs5, gu, gd = vjp((vals[8], vals[9]))
        return jnp.concatenate([gy0, gy1, gy2, gy3], axis=1), gg, ggr, gs5, gu, gd
    rows = _post1_rows(y, proj, s5y) + [(dret, D_MODEL, 0), (dys, D_MODEL, 0)]
    return _rows_call(fn, rows, [dsk],
                      [(D_MODEL, BF16), (D_MODEL, BF16), (D_MODEL, BF16), (D_MODEL, F32), (D_MODEL, F32)],
                      [(1, D_MODEL)], tm=128, name=name)


def _f_merge(z, ysg, gs, ret, b):
    return ret + jax.nn.sigmoid(gs) * (ysg * jax.nn.sigmoid(z + b))


def _merge_fwd(z, ysg, proj, ret, b, name):
    return _rows_call(_f_merge, [(z, D_MODEL, 0), (ysg, D_MODEL, 0), (proj, D_MODEL, 5), (ret, D_MODEL, 0)], [b],
                      [(D_MODEL, BF16)], [], tm=128, name=name)[0]


def _merge_bwd(z, ysg, proj, ret, dm, b, name):
    def fn(zv, yv, gv, rv, dmv, bv):
        _, vjp = jax.vjp(_f_merge, zv, yv, gv, rv, bv)
        gz, gy, gg, _, gb = vjp(dmv)
        return gz, gy, gg, gb
    rows = [(z, D_MODEL, 0), (ysg, D_MODEL, 0), (proj, D_MODEL, 5), (ret, D_MODEL, 0), (dm, D_MODEL, 0)]
    return _rows_call(fn, rows, [b], [(D_MODEL, BF16), (D_MODEL, F32), (D_MODEL, BF16)], [(1, D_MODEL)],
                      tm=128, name=name)


def _f_glu(a, b):
    return jax.nn.silu(a) * b


def _glu_fwd(ab, name):
    return _rows_call(_f_glu, [(ab, D_FF, 0), (ab, D_FF, 1)], [], [(D_FF, BF16)], [], tm=128, name=name)[0]


def _glu_bwd(ab, df, name):
    def fn(a, b, d):
        _, vjp = jax.vjp(_f_glu, a, b)
        ga, gb = vjp(d)
        return jnp.concatenate([ga, gb], axis=1)
    return _rows_call(fn, [(ab, D_FF, 0), (ab, D_FF, 1), (df, D_FF, 0)], [], [(2 * D_FF, BF16)], [],
                      tm=128, name=name)[0]


def _loss_stage(x, tgt, g, name):
    def fn(xv, tv, gv):
        def lf(xx, gg):
            err = _f_rms(xx, gg) - tv
            row = jnp.mean(err * err, axis=-1, keepdims=True)
            return 0.5 * jnp.sum(row, axis=0, keepdims=True)
        l, vjp = jax.vjp(lf, xv, gv)
        dx, dg = vjp(jnp.ones((1, 1), F32))
        return dx, dx, jnp.broadcast_to(l, (1, LANES)), dg
    return _rows_call(fn, [(x, D_MODEL, 0), (tgt, D_MODEL, 0)], [g], [(D_MODEL, F32), (D_MODEL, BF16)],
                      [(1, LANES), (1, D_MODEL)], tm=256, name=name)


def _adam_math(wv, gv, mv, vv):
    mn = ADAM_B1 * mv + (1.0 - ADAM_B1) * gv
    vn = ADAM_B2 * vv + (1.0 - ADAM_B2) * (gv * gv)
    m_hat = mn / (1.0 - ADAM_B1 ** ADAM_STEP)
    v_hat = vn / (1.0 - ADAM_B2 ** ADAM_STEP)
    delta = -ADAM_LR * (m_hat / (jnp.sqrt(v_hat) + ADAM_EPS) + ADAM_WD * wv)
    return delta, mn, vn


def _adamw(w, g, m, v, name, after=None):
    rows, cols = w.shape
    tm = _tile(rows, 128 if cols > D_FF // N_CHIPS else (256 if cols > LANES else 512))
    fn = _adam_math if after is None else (lambda wv, gv, mv, vv, _: _adam_math(wv, gv, mv, vv))
    return _rows_call(fn, [(w, cols, 0), (g, cols, 0), (m, cols, 0), (v, cols, 0)], [] if after is None else [after],
                      [(cols, F32)] * 3, [], tm=tm, name=name)


def _adamw_nd(w, g, m, v, name):
    shape = w.shape
    lead = math.prod(shape[:-2])
    blk = (lead // 8,) + shape[-2:]
    three_d = lambda a: a.reshape((lead,) + shape[-2:])

    def body(w_ref, g_ref, m_ref, v_ref, d_ref, mn_ref, vn_ref):
        d_ref[...], mn_ref[...], vn_ref[...] = _adam_math(w_ref[...], g_ref[...], m_ref[...], v_ref[...])

    spec = pl.BlockSpec(blk, lambda i: (i, 0, 0))
    res = pl.pallas_call(
        body, name=name, grid=(8,), in_specs=[spec] * 4, out_specs=[spec] * 3,
        out_shape=[jax.ShapeDtypeStruct((lead,) + shape[-2:], F32)] * 3,
        compiler_params=_params(("parallel",)),
    )(three_d(w), three_d(g), three_d(m), three_d(v))
    return [r.reshape(shape) for r in res]


MATMUL_VMEM_BUDGET = 44 * 1024 * 1024


def _matmul_tiles(m, n, k, out_bytes, has_add):
    if k > 2048:
        return _tile(m, 1024), _tile(n, 1024), _tile(k, 1024)
    tm, tn, tk = _tile(m, 2048), _tile(n, 1024), k

    def footprint():
        acc = 4 * tm * tn if k // tk > 1 else 0
        return 2 * 2 * (tm * tk + tk * tn) + 2 * (out_bytes + 4 * has_add) * tm * tn + acc

    while footprint() > MATMUL_VMEM_BUDGET:
        if tn > 512 and n % (tn // 2) == 0:
            tn //= 2
        elif tk > 512 and k % (tk // 2) == 0:
            tk //= 2
        else:
            tm //= 2
    return tm, tn, tk


def _matmul(a, b, mode, out_dtypes, *, name, add=None):
    if mode == "nn":
        (m, k), (_, n) = a.shape, b.shape
    elif mode == "nt":
        (m, k), (n, _) = a.shape, b.shape
    else:
        (k, m), (_, n) = a.shape, b.shape
    n_out = len(out_dtypes)
    has_add = add is not None
    tm, tn, tk = _matmul_tiles(m, n, k, sum(jnp.dtype(dt).itemsize for dt in out_dtypes), has_add)
    nk = k // tk
    if mode == "nn":
        a_spec = pl.BlockSpec((tm, tk), lambda i, j, kk: (i, kk))
        b_spec = pl.BlockSpec((tk, tn), lambda i, j, kk: (kk, j))
        dims = (((1,), (0,)), ((), ()))
    elif mode == "nt":
        a_spec = pl.BlockSpec((tm, tk), lambda i, j, kk: (i, kk))
        b_spec = pl.BlockSpec((tn, tk), lambda i, j, kk: (j, kk))
        dims = (((1,), (1,)), ((), ()))
    else:
        a_spec = pl.BlockSpec((tk, tm), lambda i, j, kk: (kk, i))
        b_spec = pl.BlockSpec((tk, tn), lambda i, j, kk: (kk, j))
        dims = (((0,), (0,)), ((), ()))

    def body(*refs):
        a_ref, b_ref = refs[0], refs[1]
        add_ref = refs[2] if has_add else None
        outs = refs[2 + has_add:2 + has_add + n_out]

        def finish(r):
            if has_add:
                r = r + add_ref[...]
            for o in outs:
                o[...] = r.astype(o.dtype)

        if nk == 1:
            finish(lax.dot_general(a_ref[...], b_ref[...], dims, preferred_element_type=F32))
            return
        acc = refs[-1]
        kk = pl.program_id(2)

        @pl.when(kk == 0)
        def _():
            acc[...] = jnp.zeros(acc.shape, F32)

        acc[...] += lax.dot_general(a_ref[...], b_ref[...], dims, preferred_element_type=F32)

        @pl.when(kk == nk - 1)
        def _():
            finish(acc[...])

    in_specs = [a_spec, b_spec]
    args = [a, b]
    if has_add:
        in_specs.append(pl.BlockSpec((tm, tn), lambda i, j, kk: (i, j)))
        args.append(add)
    return pl.pallas_call(
        body, name=name, grid=(m // tm, n // tn, nk), in_specs=in_specs,
        out_specs=[pl.BlockSpec((tm, tn), lambda i, j, kk: (i, j))] * n_out,
        out_shape=[jax.ShapeDtypeStruct((m, n), dt) for dt in out_dtypes],
        scratch_shapes=[pltpu.VMEM((tm, tn), F32)] if nk > 1 else [],
        compiler_params=_params(("parallel", "parallel", "arbitrary")),
    )(*args)


RET_TQ = 512


def _decay(lg_ref, i, tq, seq):
    n_idx = i * tq + lax.broadcasted_iota(jnp.int32, (tq, seq), 0)
    m_idx = lax.broadcasted_iota(jnp.int32, (tq, seq), 1)
    diff = (n_idx - m_idx).astype(F32)
    lgf = lg_ref[0, 0:1, 0:1]
    lgb = lg_ref[0, 1:2, 0:1]
    causal = diff >= 0
    return jnp.exp(jnp.where(causal, lgf * diff, -lgb * diff)), diff, causal


_NT = (((1,), (1,)), ((), ()))
_TN = (((0,), (0,)), ((), ()))


def _ret_fwd(qr, kr, proj, lg, name):
    seq = qr.shape[0]
    tq = RET_TQ
    v_blk0 = (2 * QK_WIDTH) // V_DIM

    def body(q_ref, k_ref, v_ref, lg_ref, y_ref):
        i = pl.program_id(1)
        s = lax.dot_general(q_ref[...], k_ref[...], _NT, preferred_element_type=F32)
        dm, _, _ = _decay(lg_ref, i, tq, seq)
        p = (s * dm).astype(BF16)
        y_ref[...] = jnp.dot(p, v_ref[...].astype(BF16), preferred_element_type=F32)

    return pl.pallas_call(
        body, name=name, grid=(HEADS, seq // tq),
        in_specs=[pl.BlockSpec((tq, QK_DIM), lambda h, i: (i, h)),
                  pl.BlockSpec((seq, QK_DIM), lambda h, i: (0, h)),
                  pl.BlockSpec((seq, V_DIM), lambda h, i: (0, v_blk0 + h)),
                  pl.BlockSpec((1, 2, LANES), lambda h, i: (h, 0, 0))],
        out_specs=pl.BlockSpec((tq, V_DIM), lambda h, i: (i, h)),
        out_shape=jax.ShapeDtypeStruct((seq, HEADS * V_DIM), F32),
        compiler_params=_params(("parallel", "parallel")),
    )(qr, kr, proj, lg)


def _ret_bwd(qr, kr, proj, dy, lg, name):
    seq = qr.shape[0]
    tq = RET_TQ
    v_blk0 = (2 * QK_WIDTH) // V_DIM

    def body(q_ref, k_ref, v_ref, dy_ref, lg_ref, dq_ref, dk_ref, dv_ref, dlg_ref):
        i = pl.program_id(1)

        @pl.when(i == 0)
        def _():
            dk_ref[...] = jnp.zeros(dk_ref.shape, F32)
            dv_ref[...] = jnp.zeros(dv_ref.shape, F32)
            dlg_ref[...] = jnp.zeros(dlg_ref.shape, F32)

        q = q_ref[...]
        k = k_ref[...]
        vb = v_ref[...].astype(BF16)
        dyb = dy_ref[...]
        s = lax.dot_general(q, k, _NT, preferred_element_type=F32)
        dm, diff, causal = _decay(lg_ref, i, tq, seq)
        p = s * dm
        dp = lax.dot_general(dyb, vb, _NT, preferred_element_type=F32)
        dv_ref[...] += lax.dot_general(p.astype(BF16), dyb, _TN, preferred_element_type=F32)
        ds = (dp * dm).astype(BF16)
        dq_ref[...] = jnp.dot(ds, k, preferred_element_type=F32)
        dk_ref[...] += lax.dot_general(ds, q, _TN, preferred_element_type=F32)
        gd = dp * p * diff
        dlf = jnp.sum(jnp.sum(jnp.where(causal, gd, 0.0), axis=1, keepdims=True), axis=0, keepdims=True)
        dlb = jnp.sum(jnp.sum(jnp.where(causal, 0.0, -gd), axis=1, keepdims=True), axis=0, keepdims=True)
        row = lax.broadcasted_iota(jnp.int32, (2, LANES), 0)
        dlg_ref[0] += jnp.where(row == 0, dlf, dlb)

    return pl.pallas_call(
        body, name=name, grid=(HEADS, seq // tq),
        in_specs=[pl.BlockSpec((tq, QK_DIM), lambda h, i: (i, h)),
                  pl.BlockSpec((seq, QK_DIM), lambda h, i: (0, h)),
                  pl.BlockSpec((seq, V_DIM), lambda h, i: (0, v_blk0 + h)),
                  pl.BlockSpec((tq, V_DIM), lambda h, i: (i, h)),
                  pl.BlockSpec((1, 2, LANES), lambda h, i: (h, 0, 0))],
        out_specs=[pl.BlockSpec((tq, QK_DIM), lambda h, i: (i, h)),
                   pl.BlockSpec((seq, QK_DIM), lambda h, i: (0, h)),
                   pl.BlockSpec((seq, V_DIM), lambda h, i: (0, h)),
                   pl.BlockSpec((1, 2, LANES), lambda h, i: (h, 0, 0))],
        out_shape=[jax.ShapeDtypeStruct((seq, QK_WIDTH), F32), jax.ShapeDtypeStruct((seq, QK_WIDTH), F32),
                   jax.ShapeDtypeStruct((seq, HEADS * V_DIM), F32), jax.ShapeDtypeStruct((HEADS, 2, LANES), F32)],
        compiler_params=_params(("parallel", "arbitrary")),
    )(qr, kr, proj, dy, lg)


def _shift_rows(v, reverse):
    row = lax.broadcasted_iota(jnp.int32, v.shape, 0)
    if reverse:
        return jnp.where(row == SEGMENTS - 1, 0.0, pltpu.roll(v, SEGMENTS - 1, 0))
    return jnp.where(row == 0, 0.0, pltpu.roll(v, 1, 0))


def _slab(t):
    if isinstance(t, int):
        return pl.ds(t * SEGMENTS, SEGMENTS)
    return pl.ds(pl.multiple_of(t * SEGMENTS, SEGMENTS), SEGMENTS)


def _unrolled_loop(body, lo, hi, init):
    main = (hi - lo) // SCAN_UNROLL

    def unrolled(g, carry):
        for k in range(SCAN_UNROLL):
            carry = body(lo + g * SCAN_UNROLL + k, carry)
        return carry

    carry = lax.fori_loop(0, main, unrolled, init)
    for t in range(lo + main * SCAN_UNROLL, hi):
        carry = body(t, carry)
    return carry


def _scan(xr_ref, xi_ref, lam, reverse, conj):
    steps = xr_ref.shape[0] // SEGMENTS
    cols = xr_ref.shape[1]
    lr = jnp.broadcast_to(lam[0], (SEGMENTS, cols))
    li = jnp.broadcast_to(lam[1], (SEGMENTS, cols))
    lrt = jnp.broadcast_to(lam[2], (SEGMENTS, cols))
    lit = jnp.broadcast_to(lam[3], (SEGMENTS, cols))
    if conj:
        li, lit = -li, -lit
    zero = jnp.zeros((SEGMENTS, cols), F32)

    def rows_of(t):
        return _slab(steps - 1 - t if reverse else t)

    def advance(t, carry):
        sr, si = carry
        rows = rows_of(t)
        return lr * sr - li * si + xr_ref[rows, :], lr * si + li * sr + xi_ref[rows, :]

    def step(t, carry):
        nr, ni = advance(t, carry)
        rows = rows_of(t)
        xr_ref[rows, :] = nr
        xi_ref[rows, :] = ni
        return nr, ni

    def run(body, init):
        return _unrolled_loop(body, 0, steps, init)

    er, ei = run(advance, (zero, zero))
    cr, ci = zero, zero
    for _ in range(SEGMENTS - 1):
        tr = er + lrt * cr - lit * ci
        ti = ei + lrt * ci + lit * cr
        cr, ci = _shift_rows(tr, reverse), _shift_rows(ti, reverse)
    run(step, (cr, ci))


def _permute_in(dst_ref, src_ref):
    steps = src_ref.shape[0] // SEGMENTS
    for s in range(SEGMENTS):
        dst_ref[pl.ds(s, steps, stride=SEGMENTS), :] = src_ref[s * steps:(s + 1) * steps, :].astype(dst_ref.dtype)


def _unpermute(src_ref, s):
    steps = src_ref.shape[0] // SEGMENTS
    return src_ref[pl.ds(s, steps, stride=SEGMENTS), :]


def _s5_fwd(proj, bblk, cblk, lam, name):
    seq = proj.shape[0]
    u_blk0 = (2 * QK_WIDTH + 2 * D_MODEL) // LANES
    sc = STATE_COLS

    def body(u_ref, b_ref, c_ref, lam_ref, y_ref, up_ref, yp_ref, xr_ref, xi_ref):
        _permute_in(up_ref, u_ref)
        ub = up_ref[...].astype(BF16)
        for d in range(2):
            xr_ref[...] = jnp.dot(ub, b_ref[d, :, 0:sc], preferred_element_type=F32)
            xi_ref[...] = jnp.dot(ub, b_ref[d, :, sc:2 * sc], preferred_element_type=F32)
            lm = [lam_ref[d, r:r + 1, :] for r in range(4)]
            _scan(xr_ref, xi_ref, lm, reverse=(d == 1), conj=False)
            yd = (jnp.dot(xr_ref[...].astype(BF16), c_ref[d, 0:sc, :], preferred_element_type=F32)
                  + jnp.dot(xi_ref[...].astype(BF16), c_ref[d, sc:2 * sc, :], preferred_element_type=F32))
            if d == 0:
                yp_ref[...] = yd
            else:
                yp_ref[...] += yd
        steps = seq // SEGMENTS
        for s in range(SEGMENTS):
            y_ref[s * steps:(s + 1) * steps, :] = _unpermute(yp_ref, s)

    return pl.pallas_call(
        body, name=name, grid=(N_TILES,),
        in_specs=[pl.BlockSpec((seq, LANES), lambda j: (0, u_blk0 + j)),
                  pl.BlockSpec((2, None, LANES, 2 * sc), lambda j: (0, j, 0, 0)),
                  pl.BlockSpec((2, None, 2 * sc, LANES), lambda j: (0, j, 0, 0)),
                  pl.BlockSpec((2, None, 4, sc), lambda j: (0, j, 0, 0))],
        out_specs=pl.BlockSpec((seq, LANES), lambda j: (0, j)),
        out_shape=jax.ShapeDtypeStruct((seq, D_MODEL), F32),
        scratch_shapes=[pltpu.VMEM((seq, LANES), F32), pltpu.VMEM((seq, LANES), F32),
                        pltpu.VMEM((seq, sc), F32), pltpu.VMEM((seq, sc), F32)],
        compiler_params=_params(("parallel",)),
    )(proj, bblk, cblk, lam)


def _s5_bwd(proj, dy, du_part, bblk, cblk, lam, name):
    seq = proj.shape[0]
    u_blk0 = (2 * QK_WIDTH + 2 * D_MODEL) // LANES
    sc = STATE_COLS
    steps = seq // SEGMENTS

    def body(u_ref, dy_ref, dup_ref, b_ref, c_ref, lam_ref, du_ref, db_ref, dc_ref, dlam_ref,
             up_ref, dyp_ref, dua_ref, xr_ref, xi_ref, gr_ref, gi_ref):
        _permute_in(up_ref, u_ref)
        _permute_in(dyp_ref, dy_ref)
        ub = up_ref[...].astype(BF16)
        dyb = dyp_ref[...].astype(BF16)
        ubt = up_ref[...].T.astype(BF16)
        dybt = dyp_ref[...].T.astype(BF16)
        for d in range(2):
            reverse = d == 1
            xr_ref[...] = jnp.dot(ub, b_ref[d, :, 0:sc], preferred_element_type=F32)
            xi_ref[...] = jnp.dot(ub, b_ref[d, :, sc:2 * sc], preferred_element_type=F32)
            lm = [lam_ref[d, r:r + 1, :] for r in range(4)]
            _scan(xr_ref, xi_ref, lm, reverse=reverse, conj=False)
            xrb = xr_ref[...].astype(BF16)
            xib = xi_ref[...].astype(BF16)
            dc_ref[d, :, 0:sc] = jnp.dot(dybt, xrb, preferred_element_type=F32)
            dc_ref[d, :, sc:2 * sc] = jnp.dot(dybt, xib, preferred_element_type=F32)
            gr_ref[...] = lax.dot_general(dyb, c_ref[d, 0:sc, :], _NT, preferred_element_type=F32)
            gi_ref[...] = lax.dot_general(dyb, c_ref[d, sc:2 * sc, :], _NT, preferred_element_type=F32)
            _scan(gr_ref, gi_ref, lm, reverse=not reverse, conj=True)

            def acc_step(t, carry):
                ar, ai = carry
                prev = _slab(t + 1 if reverse else t - 1)
                pr = xr_ref[prev, :]
                pi = xi_ref[prev, :]
                zr = gr_ref[_slab(t), :]
                zi = gi_ref[_slab(t), :]
                return ar + zr * pr + zi * pi, ai + zi * pr - zr * pi

            zero = jnp.zeros((SEGMENTS, sc), F32)
            if reverse:
                ar, ai = _unrolled_loop(acc_step, 0, steps - 1, (zero, zero))
                edge = _slab(steps - 1)
                pr = _shift_rows(xr_ref[_slab(0), :], True)
                pi = _shift_rows(xi_ref[_slab(0), :], True)
            else:
                ar, ai = _unrolled_loop(acc_step, 1, steps, (zero, zero))
                edge = _slab(0)
                pr = _shift_rows(xr_ref[_slab(steps - 1), :], False)
                pi = _shift_rows(xi_ref[_slab(steps - 1), :], False)
            zr = gr_ref[edge, :]
            zi = gi_ref[edge, :]
            ar = ar + zr * pr + zi * pi
            ai = ai + zi * pr - zr * pi
            dlam_ref[d, 0:1, :] = jnp.sum(ar, axis=0, keepdims=True)
            dlam_ref[d, 1:2, :] = jnp.sum(ai, axis=0, keepdims=True)

            grb = gr_ref[...].astype(BF16)
            gib = gi_ref[...].astype(BF16)
            db_ref[d, :, 0:sc] = jnp.dot(ubt, grb, preferred_element_type=F32)
            db_ref[d, :, sc:2 * sc] = jnp.dot(ubt, gib, preferred_element_type=F32)
            dud = (lax.dot_general(grb, b_ref[d, :, 0:sc], _NT, preferred_element_type=F32)
                   + lax.dot_general(gib, b_ref[d, :, sc:2 * sc], _NT, preferred_element_type=F32))
            if d == 0:
                dua_ref[...] = dud
            else:
                dua_ref[...] += dud
        for s in range(SEGMENTS):
            rows = slice(s * steps, (s + 1) * steps)
            du_ref[rows, :] = (_unpermute(dua_ref, s) + dup_ref[rows, :]).astype(du_ref.dtype)

    return pl.pallas_call(
        body, name=name, grid=(N_TILES,),
        in_specs=[pl.BlockSpec((seq, LANES), lambda j: (0, u_blk0 + j)),
                  pl.BlockSpec((seq, LANES), lambda j: (0, j)),
                  pl.BlockSpec((seq, LANES), lambda j: (0, j)),
                  pl.BlockSpec((2, None, LANES, 2 * sc), lambda j: (0, j, 0, 0)),
                  pl.BlockSpec((2, None, 2 * sc, LANES), lambda j: (0, j, 0, 0)),
                  pl.BlockSpec((2, None, 4, sc), lambda j: (0, j, 0, 0))],
        out_specs=[pl.BlockSpec((seq, LANES), lambda j: (0, j)),
                   pl.BlockSpec((2, None, LANES, 2 * sc), lambda j: (0, j, 0, 0)),
                   pl.BlockSpec((2, None, LANES, 2 * sc), lambda j: (0, j, 0, 0)),
                   pl.BlockSpec((2, None, 2, sc), lambda j: (0, j, 0, 0))],
        out_shape=[jax.ShapeDtypeStruct((seq, D_MODEL), BF16),
                   jax.ShapeDtypeStruct((2, N_TILES, LANES, 2 * sc), F32),
                   jax.ShapeDtypeStruct((2, N_TILES, LANES, 2 * sc), F32),
                   jax.ShapeDtypeStruct((2, N_TILES, 2, sc), F32)],
        scratch_shapes=[pltpu.VMEM((seq, LANES), F32), pltpu.VMEM((seq, LANES), F32), pltpu.VMEM((seq, LANES), F32),
                        pltpu.VMEM((seq, sc), F32), pltpu.VMEM((seq, sc), F32),
                        pltpu.VMEM((seq, sc), F32), pltpu.VMEM((seq, sc), F32)],
        compiler_params=_params(("parallel",)),
    )(proj, dy, du_part, bblk, cblk, lam)


def _s5_discretize(a_re, a_im, log_dt, b_re, b_im, seg_len):
    dt = jnp.exp(log_dt)[..., None]
    e = jnp.exp(a_re * dt)
    lr, li = e * jnp.cos(a_im * dt), e * jnp.sin(a_im * dt)
    et = jnp.exp(a_re * dt * seg_len)
    lrt, lit = et * jnp.cos(a_im * dt * seg_len), et * jnp.sin(a_im * dt * seg_len)
    den = a_re * a_re + a_im * a_im
    qr = ((lr - 1.0) * a_re + li * a_im) / den
    qi = (li * a_re - (lr - 1.0) * a_im) / den
    br = qr[..., None] * b_re - qi[..., None] * b_im
    bi = qr[..., None] * b_im + qi[..., None] * b_re
    return lr, li, lrt, lit, br, bi


def _s5_pack(lr, li, lrt, lit, br, bi, c_re, c_im):
    eye = jnp.eye(GROUPS_PER_TILE, dtype=F32)

    def bd_b(b):
        b5 = b.reshape(2, N_TILES, GROUPS_PER_TILE, N_STATE, GROUP)
        return jnp.einsum("dtgph,gk->dtghkp", b5, eye).reshape(2, N_TILES, LANES, STATE_COLS)

    def bd_c(c):
        c5 = c.reshape(2, N_TILES, GROUPS_PER_TILE, GROUP, N_STATE)
        return jnp.einsum("dtghp,gk->dtkpgh", c5, eye).reshape(2, N_TILES, STATE_COLS, LANES)

    bblk = jnp.concatenate([bd_b(br), bd_b(bi)], axis=3)
    cblk = jnp.concatenate([bd_c(c_re), -bd_c(c_im)], axis=2)
    lam = jnp.stack([v.reshape(2, N_TILES, STATE_COLS) for v in (lr, li, lrt, lit)], axis=2)
    return bblk, cblk, lam


def _s5_unpack(dbblk, dcblk, dlam):
    eye = jnp.eye(GROUPS_PER_TILE, dtype=F32)

    def diag_b(d):
        d6 = d.reshape(2, N_TILES, GROUPS_PER_TILE, GROUP, GROUPS_PER_TILE, N_STATE)
        return jnp.einsum("dtghkp,gk->dtgph", d6, eye).reshape(2, N_GROUPS, N_STATE, GROUP)

    def diag_c(d):
        d6 = d.reshape(2, N_TILES, GROUPS_PER_TILE, GROUP, GROUPS_PER_TILE, N_STATE)
        return jnp.einsum("dtghkp,gk->dtghp", d6, eye).reshape(2, N_GROUPS, GROUP, N_STATE)

    dbr, dbi = diag_b(dbblk[..., :STATE_COLS]), diag_b(dbblk[..., STATE_COLS:])
    dcr, dci = diag_c(dcblk[..., :STATE_COLS]), -diag_c(dcblk[..., STATE_COLS:])
    dlr = dlam[:, :, 0, :].reshape(2, N_GROUPS, N_STATE)
    dli = dlam[:, :, 1, :].reshape(2, N_GROUPS, N_STATE)
    return dlr, dli, dbr, dbi, dcr, dci


def _pos():
    return lax.axis_index("x"), lax.axis_index("y"), lax.axis_index("c")


def _remote(src, dst, ssem, rsem, dev):
    return pltpu.make_async_remote_copy(src_ref=src, dst_ref=dst, send_sem=ssem, recv_sem=rsem,
                                        device_id=dev, device_id_type=MESH)


_PIECES = (
    ("w_in", "in", D_MODEL, IN_WIDTH // N_CHIPS, 0, IN_WIDTH // N_CHIPS, 0),
    ("w_glu", "glu", D_MODEL // N_CHIPS, D_MODEL, D_MODEL // N_CHIPS, 0, 0),
    ("w_out", "out", D_MODEL // N_CHIPS, D_MODEL, D_MODEL // N_CHIPS, 0, 0),
    ("w_ffn_gate", "gu", D_MODEL, D_FF // N_CHIPS, 0, D_FF // N_CHIPS, 0),
    ("w_ffn_up", "gu", D_MODEL, D_FF // N_CHIPS, 0, D_FF // N_CHIPS, D_FF),
    ("w_ffn_down", "down", D_FF // N_CHIPS, D_MODEL, D_FF // N_CHIPS, 0, 0),
)
_BUFFERS = (("in", D_MODEL, IN_WIDTH), ("glu", D_MODEL, D_MODEL), ("out", D_MODEL, D_MODEL),
            ("gu", D_MODEL, 2 * D_FF), ("down", D_FF, D_MODEL))
_BUF_INDEX = {name: t for t, (name, _, _) in enumerate(_BUFFERS)}
N_PIECES = len(_PIECES)
N_BUFFERS = len(_BUFFERS)


def _own_block(piece, tm):
    _, _, _, cs, rstep, cstep, coff = piece
    return lambda i, chip: (i + chip * (rstep // tm), coff // cs + chip * (cstep // cs))


def _cast_place(piece, w3, layer, prev, chip_arr, name):
    _, r, cc = w3.shape
    _, rf, cf = _BUFFERS[_BUF_INDEX[piece[1]]]
    tm = _tile(r, 256)
    own = _own_block(piece, tm)

    def body(s_ref, w_ref, *rest):
        rest[-1][...] = w_ref[...].astype(BF16)

    in_specs = [pl.BlockSpec((None, tm, cc), lambda i, s: (layer, i, 0))]
    args = [w3]
    aliases = {}
    if prev is not None:
        in_specs.append(pl.BlockSpec(memory_space=pl.ANY))
        args.append(prev)
        aliases = {2: 0}
    return pl.pallas_call(
        body, name=name,
        grid_spec=pltpu.PrefetchScalarGridSpec(
            num_scalar_prefetch=1, grid=(r // tm,), in_specs=in_specs,
            out_specs=pl.BlockSpec((tm, cc), lambda i, s: own(i, s[0]))),
        out_shape=jax.ShapeDtypeStruct((rf, cf), BF16), input_output_aliases=aliases,
        compiler_params=_params(("parallel",)),
    )(chip_arr, *args)


_GATHER_GROUPS = ((0, (0,)), (0, (1, 2)), (0, (3, 4)), (0, (5,)), (1, (0,)), (1, (1, 2)), (1, (3, 4)), (1, (5,)))
GROUPS_PER_LAYER = len(_GATHER_GROUPS) // DEPTH
_SPLIT_EFFECT = pltpu.SideEffectType.DATAFLOW_SIDE_EFFECTING
SEM_SPEC = pl.BlockSpec(memory_space=pltpu.SEMAPHORE)
BF16_ROWS = 2 * SUBLANES


def _group_keys(g):
    layer, pieces = _GATHER_GROUPS[g]
    keys = []
    for p in pieces:
        if (_PIECES[p][1], layer) not in keys:
            keys.append((_PIECES[p][1], layer))
    return keys


def _half_view(ref, piece, j, c):
    _, _, rs, cs, rstep, cstep, coff = piece
    half = rs // 2
    return ref.at[pl.ds(pl.multiple_of(j * rstep + c * half, BF16_ROWS), half), pl.ds(coff + j * cstep, cs)]


def _for_my_chip(fn):
    x, y, _ = _pos()
    for mine in range(N_CHIPS):
        pl.when(2 * x + y == mine)(functools.partial(fn, mine, [j for j in range(N_CHIPS) if j != mine]))


def _gather_start(groups, placed):
    keys = [k for g in groups for k in _group_keys(g)]
    nb, ng = len(keys), len(groups)

    def body(*refs):
        bufs = dict(zip(keys, refs[nb:2 * nb]))
        ssems = refs[2 * nb:2 * nb + ng]
        rsems = refs[2 * nb + ng:2 * nb + 2 * ng]
        token = refs[2 * nb + 2 * ng]
        _, _, c = _pos()

        def send(mine, others):
            for t, g in enumerate(groups):
                layer, pieces = _GATHER_GROUPS[g]
                for k, p in enumerate(pieces):
                    view = _half_view(bufs[(_PIECES[p][1], layer)], _PIECES[p], mine, c)
                    for j in others:
                        _remote(view, view, ssems[t].at[k * N_CHIPS + j], rsems[t].at[k * N_CHIPS + mine],
                                (j // 2, j % 2, c)).start()

        _for_my_chip(send)
        token[...] = jnp.zeros(token.shape, token.dtype)

    sems = [pltpu.SemaphoreType.DMA((N_CHIPS * len(_GATHER_GROUPS[g][1]),)) for g in groups]
    shapes = [jax.ShapeDtypeStruct(placed[k].shape, placed[k].dtype) for k in keys]
    res = pl.pallas_call(
        body, name="gather_start_g%d" % groups[0],
        in_specs=[HBM_SPEC] * nb,
        out_specs=[HBM_SPEC] * nb + [SEM_SPEC] * (2 * ng) + [pl.BlockSpec(memory_space=pltpu.VMEM)],
        out_shape=shapes + sems + sems + [jax.ShapeDtypeStruct((SUBLANES, LANES), F32)],
        input_output_aliases={t: t for t in range(nb)},
        compiler_params=_params(has_side_effects=_SPLIT_EFFECT),
    )(*[pltpu.with_memory_space_constraint(placed[k], pltpu.HBM) for k in keys])
    return (dict(zip(keys, res[:nb])), dict(zip(groups, res[nb:nb + ng])),
            dict(zip(groups, res[nb + ng:nb + 2 * ng])), res[nb + 2 * ng])


def _gather_wait(g, bufs, ssem, rsem, after):
    layer, pieces = _GATHER_GROUPS[g]
    keys = _group_keys(g)
    nb = len(keys)

    def body(*refs):
        ssem_ref, rsem_ref = refs[nb], refs[nb + 1]
        land = dict(zip(keys, refs[nb + 3:]))
        _, _, c = _pos()

        def wait(mine, others):
            for k, p in enumerate(pieces):
                ref = land[(_PIECES[p][1], layer)]
                for j in others:
                    cp = _remote(_half_view(ref, _PIECES[p], mine, c), _half_view(ref, _PIECES[p], j, c),
                                 ssem_ref.at[k * N_CHIPS + j], rsem_ref.at[k * N_CHIPS + j], (j // 2, j % 2, c))
                    cp.wait_send()
                    cp.wait_recv()

        _for_my_chip(wait)

    return pl.pallas_call(
        body, name="gather_wait_g%d" % g,
        in_specs=[HBM_SPEC] * nb + [SEM_SPEC, SEM_SPEC, pl.BlockSpec(memory_space=pl.ANY)],
        out_specs=[HBM_SPEC] * nb,
        out_shape=[jax.ShapeDtypeStruct(a.shape, a.dtype) for a in bufs],
        input_output_aliases={t: t for t in range(nb)},
        compiler_params=_params(has_side_effects=_SPLIT_EFFECT),
    )(*bufs, ssem, rsem, after)


def _gather_forward(g, bufs):
    layer, pieces = _GATHER_GROUPS[g]
    keys = _group_keys(g)
    nb = len(keys)

    def body(*refs):
        land = dict(zip(keys, refs[nb:2 * nb]))
        ssem, rsem = refs[2 * nb:]
        x, y, c = _pos()

        def forward(mine, others):
            cps = []
            for k, p in enumerate(pieces):
                ref = land[(_PIECES[p][1], layer)]
                for j in others:
                    view = _half_view(ref, _PIECES[p], j, c)
                    cp = _remote(view, view, ssem.at[k * N_CHIPS + j], rsem.at[k * N_CHIPS + j], (x, y, 1 - c))
                    cp.start()
                    cps.append(cp)
            for k, p in enumerate(pieces):
                ref = land[(_PIECES[p][1], layer)]
                for j in others:
                    view = _half_view(ref, _PIECES[p], j, 1 - c)
                    _remote(view, view, ssem.at[k * N_CHIPS + j], rsem.at[k * N_CHIPS + j], (x, y, 1 - c)).wait_recv()
            for cp in cps:
                cp.wait_send()

        _for_my_chip(forward)

    nsem = N_CHIPS * len(pieces)
    return pl.pallas_call(
        body, name="gather_forward_g%d" % g,
        in_specs=[HBM_SPEC] * nb, out_specs=[HBM_SPEC] * nb,
        out_shape=[jax.ShapeDtypeStruct(a.shape, a.dtype) for a in bufs],
        input_output_aliases={t: t for t in range(nb)},
        scratch_shapes=[pltpu.SemaphoreType.DMA((nsem,)), pltpu.SemaphoreType.DMA((nsem,))],
        compiler_params=_params(has_side_effects=True),
    )(*bufs)


def _forward_start(g, bufs):
    layer, pieces = _GATHER_GROUPS[g]
    keys = _group_keys(g)
    nb = len(keys)

    def body(*refs):
        land = dict(zip(keys, refs[nb:2 * nb]))
        ssem, rsem, token = refs[2 * nb:]
        x, y, c = _pos()

        def forward(mine, others):
            for k, p in enumerate(pieces):
                for j in others:
                    view = _half_view(land[(_PIECES[p][1], layer)], _PIECES[p], j, c)
                    _remote(view, view, ssem.at[k * N_CHIPS + j], rsem.at[k * N_CHIPS + j], (x, y, 1 - c)).start()

        _for_my_chip(forward)
        token[...] = jnp.zeros(token.shape, token.dtype)

    sem = pltpu.SemaphoreType.DMA((N_CHIPS * len(pieces),))
    res = pl.pallas_call(
        body, name="forward_start_g%d" % g,
        in_specs=[HBM_SPEC] * nb,
        out_specs=[HBM_SPEC] * nb + [SEM_SPEC, SEM_SPEC, pl.BlockSpec(memory_space=pltpu.VMEM)],
        out_shape=[jax.ShapeDtypeStruct(a.shape, a.dtype) for a in bufs]
        + [sem, sem, jax.ShapeDtypeStruct((SUBLANES, LANES), F32)],
        input_output_aliases={t: t for t in range(nb)},
        compiler_params=_params(has_side_effects=_SPLIT_EFFECT),
    )(*bufs)
    return list(res[:nb]), res[nb], res[nb + 1], res[nb + 2]


def _forward_wait(g, bufs, ssem, rsem, after):
    layer, pieces = _GATHER_GROUPS[g]
    keys = _group_keys(g)
    nb = len(keys)

    def body(*refs):
        ssem_ref, rsem_ref = refs[nb], refs[nb + 1]
        land = dict(zip(keys, refs[nb + 3:]))
        x, y, c = _pos()

        def wait(mine, others):
            for k, p in enumerate(pieces):
                ref = land[(_PIECES[p][1], layer)]
                for j in others:
                    cp = _remote(_half_view(ref, _PIECES[p], j, c), _half_view(ref, _PIECES[p], j, 1 - c),
                                 ssem_ref.at[k * N_CHIPS + j], rsem_ref.at[k * N_CHIPS + j], (x, y, 1 - c))
                    cp.wait_send()
                    cp.wait_recv()

        _for_my_chip(wait)

    return pl.pallas_call(
        body, name="forward_wait_g%d" % g,
        in_specs=[HBM_SPEC] * nb + [SEM_SPEC, SEM_SPEC, pl.BlockSpec(memory_space=pl.ANY)],
        out_specs=[HBM_SPEC] * nb,
        out_shape=[jax.ShapeDtypeStruct(a.shape, a.dtype) for a in bufs],
        input_output_aliases={t: t for t in range(nb)},
        compiler_params=_params(has_side_effects=_SPLIT_EFFECT),
    )(*bufs, ssem, rsem, after)


_REDUCE_GROUPS = (
    ((5, 1), (3, 1), (4, 1), (2, 1), (1, 1), (0, 1)),
    ((5, 0), (3, 0), (4, 0)),
    ((2, 0), (1, 0)),
    ((0, 0),),
)


def _reduce_keys(group):
    keys = []
    for p, layer in group:
        if (_PIECES[p][1], layer) not in keys:
            keys.append((_PIECES[p][1], layer))
    return keys


def _half_block(piece, tm):
    _, _, rs, cs, rstep, cstep, coff = piece
    return lambda i, j, c: (j * (rstep // tm) + c * (rs // 2 // tm) + i, coff // cs + j * (cstep // cs))


def _swap_start(g, dwb):
    group = _REDUCE_GROUPS[g]
    keys = _reduce_keys(group)
    nk = len(keys)

    def body(*refs):
        src = dict(zip(keys, refs[nk:2 * nk]))
        dst = dict(zip(keys, refs[2 * nk:3 * nk]))
        ssem, rsem, token = refs[3 * nk:]
        x, y, c = _pos()
        for k, (p, layer) in enumerate(group):
            key = (_PIECES[p][1], layer)
            for j in range(N_CHIPS):
                _remote(_half_view(src[key], _PIECES[p], j, 1 - c), _half_view(dst[key], _PIECES[p], j, 1 - c),
                        ssem.at[k * N_CHIPS + j], rsem.at[k * N_CHIPS + j], (x, y, 1 - c)).start()
        token[...] = jnp.zeros(token.shape, token.dtype)

    sem = pltpu.SemaphoreType.DMA((N_CHIPS * len(group),))
    shapes = [jax.ShapeDtypeStruct(dwb[k].shape, BF16) for k in keys]
    res = pl.pallas_call(
        body, name="swap_start_g%d" % g,
        in_specs=[HBM_SPEC] * nk,
        out_specs=[HBM_SPEC] * (2 * nk) + [SEM_SPEC, SEM_SPEC, pl.BlockSpec(memory_space=pltpu.VMEM)],
        out_shape=shapes + shapes + [sem, sem, jax.ShapeDtypeStruct((SUBLANES, LANES), F32)],
        input_output_aliases={t: t for t in range(nk)},
        compiler_params=_params(has_side_effects=_SPLIT_EFFECT),
    )(*[pltpu.with_memory_space_constraint(dwb[k], pltpu.HBM) for k in keys])
    return list(res[:nk]), list(res[nk:2 * nk]), res[2 * nk], res[2 * nk + 1], res[2 * nk + 2]


def _swap_wait(g, own, land, ssem, rsem, after):
    group = _REDUCE_GROUPS[g]
    keys = _reduce_keys(group)
    nk = len(keys)

    def body(*refs):
        ssem_ref, rsem_ref = refs[2 * nk], refs[2 * nk + 1]
        src = dict(zip(keys, refs[2 * nk + 3:3 * nk + 3]))
        dst = dict(zip(keys, refs[3 * nk + 3:]))
        x, y, c = _pos()
        for k, (p, layer) in enumerate(group):
            key = (_PIECES[p][1], layer)
            for j in range(N_CHIPS):
                cp = _remote(_half_view(src[key], _PIECES[p], j, 1 - c), _half_view(dst[key], _PIECES[p], j, c),
                             ssem_ref.at[k * N_CHIPS + j], rsem_ref.at[k * N_CHIPS + j], (x, y, 1 - c))
                cp.wait_send()
                cp.wait_recv()

    res = pl.pallas_call(
        body, name="swap_wait_g%d" % g,
        in_specs=[HBM_SPEC] * (2 * nk) + [SEM_SPEC, SEM_SPEC, pl.BlockSpec(memory_space=pl.ANY)],
        out_specs=[HBM_SPEC] * (2 * nk),
        out_shape=[jax.ShapeDtypeStruct(a.shape, a.dtype) for a in list(own) + list(land)],
        input_output_aliases={t: t for t in range(2 * nk)},
        compiler_params=_params(has_side_effects=_SPLIT_EFFECT),
    )(*own, *land, ssem, rsem, after)
    return dict(zip(keys, res[nk:]))


def _chip_partial(piece, dw, got, prev, c_arr, name):
    _, _, rs, cs, _, _, _ = piece
    half = rs // 2
    tm = _tile(half, 256)
    blk = _half_block(piece, tm)

    def body(s_ref, dw_ref, got_ref, *rest):
        rest[-1][...] = (dw_ref[...] + got_ref[...].astype(F32)).astype(BF16)

    spec = pl.BlockSpec((tm, cs), lambda j, i, s: blk(i, j, s[0]))
    in_specs = [spec, spec]
    args = [dw, got]
    aliases = {}
    if prev is not None:
        in_specs.append(pl.BlockSpec(memory_space=pl.ANY))
        args.append(prev)
        aliases = {3: 0}
    return pl.pallas_call(
        body, name=name,
        grid_spec=pltpu.PrefetchScalarGridSpec(
            num_scalar_prefetch=1, grid=(N_CHIPS, half // tm), in_specs=in_specs, out_specs=spec),
        out_shape=jax.ShapeDtypeStruct(dw.shape, BF16), input_output_aliases=aliases,
        compiler_params=_params(("parallel", "parallel")),
    )(c_arr, *args)


def _scatter_start(g, partials):
    group = _REDUCE_GROUPS[g]
    keys = _reduce_keys(group)
    nk, n = len(keys), len(group)

    def body(*refs):
        pt = dict(zip(keys, refs[nk:2 * nk]))
        land = refs[2 * nk:2 * nk + n]
        ssem, rsem, token = refs[2 * nk + n:]
        _, _, c = _pos()

        def send(mine, others):
            for k, (p, layer) in enumerate(group):
                for j in others:
                    _remote(_half_view(pt[(_PIECES[p][1], layer)], _PIECES[p], j, c), land[k].at[mine],
                            ssem.at[k * N_CHIPS + j], rsem.at[k * N_CHIPS + mine], (j // 2, j % 2, c)).start()

        _for_my_chip(send)
        token[...] = jnp.zeros(token.shape, token.dtype)

    sem = pltpu.SemaphoreType.DMA((N_CHIPS * n,))
    res = pl.pallas_call(
        body, name="scatter_start_g%d" % g,
        in_specs=[HBM_SPEC] * nk,
        out_specs=[HBM_SPEC] * (nk + n) + [SEM_SPEC, SEM_SPEC, pl.BlockSpec(memory_space=pltpu.VMEM)],
        out_shape=([jax.ShapeDtypeStruct(partials[k].shape, BF16) for k in keys]
                   + [jax.ShapeDtypeStruct((N_CHIPS, _PIECES[p][2] // 2, _PIECES[p][3]), BF16) for p, _ in group]
                   + [sem, sem, jax.ShapeDtypeStruct((SUBLANES, LANES), F32)]),
        input_output_aliases={t: t for t in range(nk)},
        compiler_params=_params(has_side_effects=_SPLIT_EFFECT),
    )(*[pltpu.with_memory_space_constraint(partials[k], pltpu.HBM) for k in keys])
    return list(res[:nk]), list(res[nk:nk + n]), res[nk + n], res[nk + n + 1], res[nk + n + 2]


def _scatter_wait(g, partials, land, ssem, rsem, after):
    group = _REDUCE_GROUPS[g]
    keys = _reduce_keys(group)
    nk, n = len(keys), len(group)

    def body(*refs):
        ssem_ref, rsem_ref = refs[nk + n], refs[nk + n + 1]
        pt = dict(zip(keys, refs[nk + n + 3:2 * nk + n + 3]))
        land_ref = refs[2 * nk + n + 3:]
        _, _, c = _pos()

        def wait(mine, others):
            for k, (p, layer) in enumerate(group):
                for j in others:
                    cp = _remote(_half_view(pt[(_PIECES[p][1], layer)], _PIECES[p], j, c), land_ref[k].at[j],
                                 ssem_ref.at[k * N_CHIPS + j], rsem_ref.at[k * N_CHIPS + j], (j // 2, j % 2, c))
                    cp.wait_send()
                    cp.wait_recv()

        _for_my_chip(wait)

    res = pl.pallas_call(
        body, name="scatter_wait_g%d" % g,
        in_specs=[HBM_SPEC] * (nk + n) + [SEM_SPEC, SEM_SPEC, pl.BlockSpec(memory_space=pl.ANY)],
        out_specs=[HBM_SPEC] * (nk + n),
        out_shape=[jax.ShapeDtypeStruct(a.shape, a.dtype) for a in list(partials) + list(land)],
        input_output_aliases={t: t for t in range(nk + n)},
        compiler_params=_params(has_side_effects=_SPLIT_EFFECT),
    )(*partials, *land, ssem, rsem, after)
    return list(res[nk:])


def _reduce_half(piece, layer, dw, got, land, prev, idx, name):
    _, _, rs, cs, _, _, _ = piece
    half = rs // 2
    tm = _tile(half, 256)
    blk = _half_block(piece, tm)

    def body(s_ref, dw_ref, got_ref, r1, r2, r3, *rest):
        acc = dw_ref[...] + got_ref[...].astype(F32)
        for r in (r1, r2, r3):
            acc = acc + r[...].astype(F32)
        rest[-1][...] = acc

    def land_map(k):
        return lambda i, s: ((s[1] + k) % N_CHIPS, i, 0)

    own = pl.BlockSpec((tm, cs), lambda i, s: blk(i, s[1], s[0]))
    in_specs = [own, own] + [pl.BlockSpec((None, tm, cs), land_map(k)) for k in (1, 2, 3)]
    args = [dw, got, land, land, land]
    aliases = {}
    if prev is not None:
        in_specs.append(pl.BlockSpec(memory_space=pl.ANY))
        args.append(prev)
        aliases = {6: 0}
    return pl.pallas_call(
        body, name=name,
        grid_spec=pltpu.PrefetchScalarGridSpec(
            num_scalar_prefetch=1, grid=(half // tm,), in_specs=in_specs,
            out_specs=pl.BlockSpec((None, tm, cs), lambda i, s: (layer, s[0] * (half // tm) + i, 0))),
        out_shape=jax.ShapeDtypeStruct((DEPTH, rs, cs), F32), input_output_aliases=aliases,
        compiler_params=_params(("parallel",)),
    )(idx, *args)


def _share_halves(reduced):
    def body(*refs):
        buf = refs[N_PIECES:2 * N_PIECES]
        ssem, rsem = refs[2 * N_PIECES:]
        x, y, c = _pos()

        def half(p, layer, cc):
            rows = _PIECES[p][2] // 2
            return buf[p].at[layer, pl.ds(pl.multiple_of(cc * rows, SUBLANES), rows), :]

        pairs = [(p, layer) for p in range(N_PIECES) for layer in range(DEPTH)]
        rem = [_remote(half(p, layer, c), half(p, layer, c), ssem.at[k], rsem.at[k], (x, y, 1 - c))
               for k, (p, layer) in enumerate(pairs)]
        for cp in rem:
            cp.start()
        for k, (p, layer) in enumerate(pairs):
            rem[k].wait_send()
            _remote(half(p, layer, 1 - c), half(p, layer, 1 - c), ssem.at[k], rsem.at[k], (x, y, 1 - c)).wait_recv()

    nsem = N_PIECES * DEPTH
    return pl.pallas_call(
        body, name="share_halves",
        in_specs=[HBM_SPEC] * N_PIECES, out_specs=[HBM_SPEC] * N_PIECES,
        out_shape=[jax.ShapeDtypeStruct((DEPTH, p[2], p[3]), F32) for p in _PIECES],
        input_output_aliases={t: t for t in range(N_PIECES)},
        scratch_shapes=[pltpu.SemaphoreType.DMA((nsem,)), pltpu.SemaphoreType.DMA((nsem,))],
        compiler_params=_params(has_side_effects=True),
    )(*reduced)


N_DEV = 8


def _place_slot(v, me_arr, take_block):
    rows = v.shape[0] // N_DEV if take_block else v.shape[0]
    tm = _tile(rows, 512)
    steps = rows // tm

    def body(s_ref, v_ref, out_ref):
        out_ref[...] = v_ref[...]

    return pl.pallas_call(
        body, name="place_small_block" if take_block else "place_small_sum",
        grid_spec=pltpu.PrefetchScalarGridSpec(
            num_scalar_prefetch=1, grid=(steps,),
            in_specs=[pl.BlockSpec((tm, LANES), lambda i, s: (s[0] * steps * take_block + i, 0))],
            out_specs=pl.BlockSpec((None, tm, LANES), lambda i, s: (s[0], i, 0))),
        out_shape=jax.ShapeDtypeStruct((N_DEV, rows, LANES), F32),
        compiler_params=_params(("parallel",)),
    )(me_arr, v)


def _all_peers():
    x, y, c = _pos()
    flip = lambda v, f: 1 - v if f else v
    return (x, y, c), [(flip(x, a), flip(y, b), flip(c, d))
                       for a in (0, 1) for b in (0, 1) for d in (0, 1) if a or b or d]


def _slot_index(dev):
    return 4 * dev[0] + 2 * dev[1] + dev[2]


def _exchange_start(g, src, name):
    rows = g.shape[1]
    n_in = 1 if src is None else 2

    def body(*refs):
        g_ref = refs[n_in]
        src_ref = refs[n_in + 1] if src is not None else None
        ssem, rsem, token = refs[2 * n_in:]
        me, peers = _all_peers()
        for k, dev in enumerate(peers):
            if src is None:
                mine = g_ref.at[_slot_index(me)]
            else:
                mine = src_ref.at[pl.ds(pl.multiple_of(_slot_index(dev) * rows, SUBLANES), rows), :]
            _remote(mine, g_ref.at[_slot_index(me)], ssem.at[k], rsem.at[k], dev).start()
        token[...] = jnp.zeros(token.shape, token.dtype)

    sem = pltpu.SemaphoreType.DMA((N_DEV - 1,))
    args = [g] if src is None else [g, src]
    res = pl.pallas_call(
        body, name=name,
        in_specs=[HBM_SPEC] * n_in,
        out_specs=[HBM_SPEC] * n_in + [SEM_SPEC, SEM_SPEC, pl.BlockSpec(memory_space=pltpu.VMEM)],
        out_shape=[jax.ShapeDtypeStruct(a.shape, a.dtype) for a in args]
        + [sem, sem, jax.ShapeDtypeStruct((SUBLANES, LANES), F32)],
        input_output_aliases={t: t for t in range(n_in)},
        compiler_params=_params(has_side_effects=_SPLIT_EFFECT),
    )(*[pltpu.with_memory_space_constraint(a, pltpu.HBM) for a in args])
    return list(res[:n_in]), res[n_in], res[n_in + 1], res[n_in + 2]


def _exchange_wait(bufs, ssem, rsem, after, name):
    n_in = len(bufs)

    def body(*refs):
        ssem_ref, rsem_ref = refs[n_in], refs[n_in + 1]
        g_ref = refs[n_in + 3]
        me, peers = _all_peers()
        for k, dev in enumerate(peers):
            cp = _remote(g_ref.at[_slot_index(me)], g_ref.at[_slot_index(dev)], ssem_ref.at[k], rsem_ref.at[k], dev)
            cp.wait_send()
            cp.wait_recv()

    res = pl.pallas_call(
        body, name=name,
        in_specs=[HBM_SPEC] * n_in + [SEM_SPEC, SEM_SPEC, pl.BlockSpec(memory_space=pl.ANY)],
        out_specs=[HBM_SPEC] * n_in,
        out_shape=[jax.ShapeDtypeStruct(a.shape, a.dtype) for a in bufs],
        input_output_aliases={t: t for t in range(n_in)},
        compiler_params=_params(has_side_effects=_SPLIT_EFFECT),
    )(*bufs, ssem, rsem, after)
    return res[0]


def _sum_slots(g, name):
    n, rows, _ = g.shape
    tm = _tile(rows, 512)

    def body(g_ref, out_ref):
        acc = g_ref[0]
        for k in range(1, n):
            acc = acc + g_ref[k]
        out_ref[...] = acc

    return pl.pallas_call(
        body, name=name, grid=(rows // tm,),
        in_specs=[pl.BlockSpec((n, tm, LANES), lambda i: (0, i, 0))],
        out_specs=pl.BlockSpec((tm, LANES), lambda i: (i, 0)),
        out_shape=jax.ShapeDtypeStruct((rows, LANES), F32),
        compiler_params=_params(("parallel",)),
    )(g)


_TINY = ("ln_mix_g", "ret_log_gamma", "ssm_a_re", "ssm_a_im", "ssm_log_dt", "ssm_d", "b_glu", "ln_ffn_g", "ln_final_g")
_MID = ("ssm_b_re", "ssm_b_im", "ssm_c_re", "ssm_c_im")
_SMALL = _TINY + _MID
_FLAT_ALIGN = LANES * LANES
_FLAT_ROWS = 1024


def _flat_rows(like, names):
    rows = sum((math.prod(like[n].shape) + (-math.prod(like[n].shape)) % _FLAT_ALIGN) // LANES for n in names)
    return rows + (-rows) % _FLAT_ROWS


def _flatten(d, names):
    parts = []
    for n in names:
        f = d[n].reshape(-1)
        parts.append(jnp.pad(f, (0, (-f.shape[0]) % _FLAT_ALIGN)))
    total = sum(p.shape[0] for p in parts)
    parts.append(jnp.zeros(((-total) % (_FLAT_ROWS * LANES),), F32))
    return jnp.concatenate(parts).reshape(-1, LANES)


def _unflatten(flat, like, names):
    out, row = {}, 0
    for n in names:
        size = math.prod(like[n].shape)
        rows = (size + (-size) % _FLAT_ALIGN) // LANES
        part = lax.optimization_barrier(flat[row:row + rows])
        out[n] = part.reshape(-1)[:size].reshape(like[n].shape)
        row += rows
    return out


_BIG = ("w_in", "w_glu", "w_out", "w_ffn_gate", "w_ffn_up", "w_ffn_down")
_WEIGHTS = ("ln_mix_g", "w_in", "ret_log_gamma", "ssm_a_re", "ssm_a_im", "ssm_log_dt", "ssm_b_re", "ssm_b_im",
            "ssm_c_re", "ssm_c_im", "ssm_d", "w_glu", "b_glu", "w_out", "ln_ffn_g", "w_ffn_gate", "w_ffn_up",
            "w_ffn_down", "ln_final_g")


def _rope_tables(seq):
    half = QK_DIM // 2
    inv = 1.0 / (ROPE_BASE ** (jnp.arange(half, dtype=F32) / half))
    ang = jnp.arange(seq, dtype=F32)[:, None] * inv[None, :]
    return jnp.cos(ang), jnp.sin(ang)


def _step(w, m, v, x, target):
    seq = x.shape[0]
    seg_len = float(seq // SEGMENTS)
    c_idx = lax.axis_index("c").astype(jnp.int32)
    chip_idx = (2 * lax.axis_index("x") + lax.axis_index("y")).astype(jnp.int32)
    c_arr = jnp.stack([c_idx])
    idx_arr = jnp.stack([c_idx, chip_idx])

    chip_arr = jnp.stack([chip_idx])
    placed = {}

    def cast(pieces, layer):
        for p in pieces:
            key = (_PIECES[p][1], layer)
            placed[key] = _cast_place(_PIECES[p], w[_PIECES[p][0]], layer, placed.get(key), chip_arr,
                                      "cast_%s_l%d" % (_PIECES[p][0], layer))

    for layer, pieces in _GATHER_GROUPS:
        cast(pieces, layer)
    flying, ssems, rsems, token = _gather_start(list(range(len(_GATHER_GROUPS))), placed)
    wf = {b[0]: [None] * DEPTH for b in _BUFFERS}

    handing = {}

    def arrive(g, after):
        ks = _group_keys(g)
        landed = _gather_wait(g, [flying[k] for k in ks], ssems[g], rsems[g], after)
        for k, a in zip(ks, _gather_forward(g, landed)):
            wf[k[0]][k[1]] = a

    def hand_over(g, after):
        ks = _group_keys(g)
        landed = _gather_wait(g, [flying[k] for k in ks], ssems[g], rsems[g], after)
        bufs, fs, fr, tok = _forward_start(g, landed)
        handing[g] = (bufs, fs, fr)
        return tok[0:1, 0:1]

    def complete(g, after):
        for k, a in zip(_group_keys(g), _forward_wait(g, *handing[g], after)):
            wf[k[0]][k[1]] = a

    cos, sin = _rope_tables(seq)

    started = token[0, 0]
    s5_ops, s5_vjps = [], []
    for i in range(DEPTH):
        s5_raw = (w["ssm_a_re"][i] + started, w["ssm_a_im"][i], w["ssm_log_dt"][i], w["ssm_b_re"][i], w["ssm_b_im"][i])
        disc, disc_vjp = jax.vjp(functools.partial(_s5_discretize, seg_len=seg_len), *s5_raw)
        bblk, cblk, lam = _s5_pack(*disc, w["ssm_c_re"][i] + started, w["ssm_c_im"][i] + started)
        s5_ops.append((bblk.astype(BF16), cblk.astype(BF16), lam))
        s5_vjps.append(disc_vjp)
    tiny_flat = [_flatten({**d, "ln_final_g": d["ln_final_g"] + started}, _TINY) for d in (w, m, v)]
    corner = lambda a: a[(0,) * (a.ndim - 2)][0:1, 0:1].astype(F32)
    prepared = sum(corner(a) for ops in s5_ops for a in ops) + sum(corner(a) for a in tiny_flat) + corner(cos) + corner(sin)

    saved = []
    xc = x + token[0, 0]
    for i in range(DEPTH):
        t = "_l%d" % i
        s = {"x_in": xc}
        if i == 0:
            s["h"] = _rms_fwd(xc, w["ln_mix_g"][i:i + 1], "rms_mix" + t)
            arrive(0, prepared + corner(s["h"]))
        else:
            s["h"] = _rms_fwd(xc, w["ln_mix_g"][i:i + 1] + next_in, "rms_mix" + t)
        s["proj"] = _matmul(s["h"], wf["in"][i], "nn", [F32], name="mm_in" + t)[0]
        s["qr"], s["kr"] = _rot_fwd(s["proj"], cos, sin, "rot" + t)
        s["lg"] = jnp.broadcast_to(w["ret_log_gamma"][i].T[:, :, None], (HEADS, 2, LANES))
        s["y"] = _ret_fwd(s["qr"], s["kr"], s["proj"], s["lg"], "ret" + t)
        s["s5"], s["disc_vjp"] = s5_ops[i], s5_vjps[i]
        s["s5y"] = _s5_fwd(s["proj"], *s["s5"], "s5" + t)
        first = GROUPS_PER_LAYER * i
        d_skip = w["ssm_d"][i:i + 1] + hand_over(first + 1, s["s5y"])
        s["ret"], s["ysg"], s["ysgb"] = _post1_fwd(s["y"], s["proj"], s["s5y"], d_skip, "post" + t)
        complete(first + 1, s["ysgb"])
        s["z"] = _matmul(s["ysgb"], wf["glu"][i], "nn", [F32], name="mm_glu" + t)[0]
        b_glu = w["b_glu"][i:i + 1] + hand_over(first + 2, s["z"])
        s["merged"] = _merge_fwd(s["z"], s["ysg"], s["proj"], s["ret"], b_glu, "merge" + t)
        s["x1"] = _matmul(s["merged"], wf["out"][i], "nn", [F32], add=xc, name="mm_out" + t)[0]
        s["h2"] = _rms_fwd(s["x1"], w["ln_ffn_g"][i:i + 1], "rms_ffn" + t)
        complete(first + 2, s["h2"])
        s["ab"] = _matmul(s["h2"], wf["gu"][i], "nn", [F32], name="mm_gu" + t)[0]
        hand_over(first + 3, s["ab"])
        if i + 1 < DEPTH:
            next_in = hand_over(first + GROUPS_PER_LAYER, s["ab"])
        s["f"] = _glu_fwd(s["ab"], "glu" + t)
        complete(first + 3, s["f"])
        xc = _matmul(s["f"], wf["down"][i], "nn", [F32], add=s["x1"], name="mm_down" + t)[0]
        if i + 1 < DEPTH:
            complete(first + GROUPS_PER_LAYER, xc)
        saved.append(s)

    dx, dxb, loss_row, dg_final = _loss_stage(xc, target, w["ln_final_g"][None, :], "loss")
    loss = lax.psum(loss_row[0, 0], ("x", "y", "c"))

    g_small = {"ln_final_g": dg_final[0]}
    per_layer = {n: [None] * DEPTH for n in _SMALL if n != "ln_final_g"}
    dws, got, swaps, flights = {}, {}, {}, []

    def dw_mm(a, b, buf, i, name):
        dws[(buf, i)] = _matmul(a, b, "tn", [F32, BF16], name=name)

    def depart(g):
        keys = _reduce_keys(_REDUCE_GROUPS[g])
        own, land, ssem, rsem, tok = _swap_start(g, {k: dws[k][1] for k in keys})
        swaps[g] = (own, land, ssem, rsem)
        return tok[0:1, 0:1]

    def proceed(g, after):
        group = _REDUCE_GROUPS[g]
        got.update(_swap_wait(g, *swaps[g], after))
        partials = {}
        for p, layer in group:
            key = (_PIECES[p][1], layer)
            partials[key] = _chip_partial(_PIECES[p], dws[key][0], got[key], partials.get(key), c_arr,
                                          "chip_partial_%s_l%d" % (_PIECES[p][0], layer))
        pt, land, ssem, rsem, tok = _scatter_start(g, partials)
        flights.append((g, pt, land, ssem, rsem))
        return tok[0:1, 0:1]

    for i in reversed(range(DEPTH)):
        t = "_l%d" % i
        s = saved[i]
        g_ffn, g_mix, d_skip = w["ln_ffn_g"][i:i + 1], w["ln_mix_g"][i:i + 1], w["ssm_d"][i:i + 1]
        dw_mm(s["f"], dxb, "down", i, "dw_down" + t)
        df = _matmul(dxb, wf["down"][i], "nt", [F32], name="dx_down" + t)[0]
        if i == 0:
            g_ffn = g_ffn + proceed(0, df)
        dab = _glu_bwd(s["ab"], df, "glu_bwd" + t)
        dw_mm(s["h2"], dab, "gu", i, "dw_gu" + t)
        if i == 0:
            g_ffn = g_ffn + depart(1)
        dh2 = _matmul(dab, wf["gu"][i], "nt", [F32], name="dx_gu" + t)[0]
        if i == 0:
            g_ffn = g_ffn + proceed(1, dh2)
        dx1, dx1b, dg = _rms_bwd(s["x1"], dh2, dx, g_ffn, "rms_ffn_bwd" + t)
        per_layer["ln_ffn_g"][i] = dg[0]

        dw_mm(s["merged"], dx1b, "out", i, "dw_out" + t)
        dmerged = _matmul(dx1b, wf["out"][i], "nt", [F32], name="dx_out" + t)[0]
        dz, dys_part, dgs, db = _merge_bwd(s["z"], s["ysg"], s["proj"], s["ret"], dmerged, w["b_glu"][i:i + 1],
                                           "merge_bwd" + t)
        per_layer["b_glu"][i] = db[0]
        dw_mm(s["ysgb"], dz, "glu", i, "dw_glu" + t)
        if i == 0:
            d_skip = d_skip + depart(2)
        dys = _matmul(dz, wf["glu"][i], "nt", [F32], add=dys_part, name="dx_glu" + t)[0]
        if i == 0:
            d_skip = d_skip + proceed(2, dys)
        dy, dgg, dgr, ds5, du_part, dd = _post1_bwd(s["y"], s["proj"], s["s5y"], dmerged, dys,
                                                    d_skip, "post_bwd" + t)
        per_layer["ssm_d"][i] = dd[0]
        du, dbblk, dcblk, dlam = _s5_bwd(s["proj"], ds5, du_part, *s["s5"], "s5_bwd" + t)
        dlr, dli, dbr, dbi, dcr, dci = _s5_unpack(dbblk, dcblk, dlam)
        zeros = jnp.zeros_like(dlr)
        da_re, da_im, dlog_dt, db_re, db_im = s["disc_vjp"]((dlr, dli, zeros, zeros, dbr, dbi))
        for n, val in (("ssm_a_re", da_re), ("ssm_a_im", da_im), ("ssm_log_dt", dlog_dt), ("ssm_b_re", db_re),
                       ("ssm_b_im", db_im), ("ssm_c_re", dcr), ("ssm_c_im", dci)):
            per_layer[n][i] = val
        dqr, dkr, dv, dlg = _ret_bwd(s["qr"], s["kr"], s["proj"], dy, s["lg"], "ret_bwd" + t)
        per_layer["ret_log_gamma"][i] = dlg[:, :, 0].T
        dqkv = _rot_bwd(dqr, dkr, dv, cos, sin, "rot_bwd" + t)
        dproj = jnp.concatenate([dqkv, dgg, du, dgr, dgs], axis=1)
        dw_mm(s["h"], dproj, "in", i, "dw_in" + t)
        if i == 0:
            g_mix = g_mix + depart(3)
        dh = _matmul(dproj, wf["in"][i], "nt", [F32], name="dx_in" + t)[0]
        if i == 0:
            g_mix = g_mix + proceed(3, dh)
        dx, dxb, dg = _rms_bwd(s["x_in"], dh, dx1, g_mix, "rms_mix_bwd" + t)
        per_layer["ln_mix_g"][i] = dg[0]
        if i == DEPTH - 1:
            dxb = dxb + depart(0).astype(BF16)

    for n in per_layer:
        g_small[n] = jnp.stack(per_layer[n])
    me_arr = jnp.stack([2 * chip_idx + c_idx])
    g_mine = _flatten(g_small, _SMALL)
    rs_bufs, rs_ssem, rs_rsem, small_token = _exchange_start(_place_slot(g_mine, me_arr, True), g_mine,
                                                             "small_scatter_start")

    reduced = [None] * N_PIECES
    before = small_token
    for g, pt, land, ssem, rsem in flights:
        landed = _scatter_wait(g, pt, land, ssem, rsem, before)
        for (p, layer), buf in zip(_REDUCE_GROUPS[g], landed):
            key = (_PIECES[p][1], layer)
            reduced[p] = _reduce_half(_PIECES[p], layer, dws[key][0], got[key], buf, reduced[p], idx_arr,
                                      "reduce_%s_l%d" % (_PIECES[p][0], layer))
            before = reduced[p]
    g_big = dict(zip([p[0] for p in _PIECES], _share_halves(reduced)))

    landed = _exchange_wait(rs_bufs, rs_ssem, rs_rsem, g_big[_BIG[-1]], "small_scatter_wait")
    ag_bufs, ag_ssem, ag_rsem, ag_token = _exchange_start(
        _place_slot(_sum_slots(landed, "sum_small"), me_arr, False), None, "small_gather_start")

    grads, delta, new_m, new_v = {}, {}, {}, {}
    for n in _BIG:
        d, r, cc = w[n].shape
        two_d = lambda a: a.reshape(d * r, cc)
        dl, mn, vn = _adamw(two_d(w[n]), two_d(g_big[n]), two_d(m[n]), two_d(v[n]), "adamw_" + n, after=ag_token)
        grads[n], delta[n], new_m[n], new_v[n] = g_big[n], dl.reshape(d, r, cc), mn.reshape(d, r, cc), vn.reshape(d, r, cc)

    all_done = sum(corner(delta[n]) for n in _BIG)
    gathered = _exchange_wait(ag_bufs, ag_ssem, ag_rsem, all_done, "small_gather_wait")
    g_flat = gathered.reshape(-1, LANES)
    grads.update(_unflatten(g_flat, w, _SMALL))
    tiny_rows = _flat_rows(w, _TINY)
    dl, mn, vn = _adamw(tiny_flat[0], g_flat[:tiny_rows], tiny_flat[1], tiny_flat[2], "adamw_tiny")
    for dst, flat in ((delta, dl), (new_m, mn), (new_v, vn)):
        dst.update(_unflatten(flat, w, _TINY))
    for n in _MID:
        delta[n], new_m[n], new_v[n] = _adamw_nd(w[n], grads[n], m[n], v[n], "adamw_" + n)
    return loss, dx, grads, delta, new_m, new_v


def kernel(x, ln_mix_g, w_in, ret_log_gamma, ssm_a_re, ssm_a_im, ssm_log_dt, ssm_b_re, ssm_b_im, ssm_c_re, ssm_c_im, ssm_d, w_glu, b_glu, w_out, ln_ffn_g, w_ffn_gate, w_ffn_up, w_ffn_down, ln_final_g, loss_target, m_ln_mix_g, m_w_in, m_ret_log_gamma, m_ssm_a_re, m_ssm_a_im, m_ssm_log_dt, m_ssm_b_re, m_ssm_b_im, m_ssm_c_re, m_ssm_c_im, m_ssm_d, m_w_glu, m_b_glu, m_w_out, m_ln_ffn_g, m_w_ffn_gate, m_w_ffn_up, m_w_ffn_down, m_ln_final_g, v_ln_mix_g, v_w_in, v_ret_log_gamma, v_ssm_a_re, v_ssm_a_im, v_ssm_log_dt, v_ssm_b_re, v_ssm_b_im, v_ssm_c_re, v_ssm_c_im, v_ssm_d, v_w_glu, v_b_glu, v_w_out, v_ln_ffn_g, v_w_ffn_gate, v_w_ffn_up, v_w_ffn_down, v_ln_final_g):
    given = dict(locals())
    w = {n: given[n] for n in _WEIGHTS}
    m = {n: given["m_" + n] for n in _WEIGHTS}
    v = {n: given["v_" + n] for n in _WEIGHTS}
    loss, dx, grads, delta, new_m, new_v = _step(w, m, v, x[0], loss_target[0])
    return (loss, dx[None], *[grads[n] for n in _WEIGHTS], *[delta[n] for n in _WEIGHTS],
            *[new_m[n] for n in _WEIGHTS], *[new_v[n] for n in _WEIGHTS])
```

```python
import functools
import math

import jax
import jax.numpy as jnp
from jax import lax
from jax.experimental import pallas as pl
from jax.experimental.pallas import tpu as pltpu

F32 = jnp.float32
BF16 = jnp.bfloat16

D_MODEL = 2048
DEPTH = 2
HEADS = 4
QK_DIM = 256
V_DIM = 512
QK_WIDTH = HEADS * QK_DIM
ROPE_BASE = 10000.0
GROUP = 16
N_GROUPS = D_MODEL // GROUP
N_STATE = 64
D_FF = 5632
IN_WIDTH = 2 * QK_WIDTH + 5 * D_MODEL
EPS = 1e-6
N_CHIPS = 4

ADAM_LR = 0.001
ADAM_B1 = 0.9
ADAM_B2 = 0.999
ADAM_EPS = 1e-08
ADAM_WD = 0.01
ADAM_STEP = 10

LANES = 128
SUBLANES = 8
VMEM_LIMIT = 56 * 1024 * 1024
SEGMENTS = SUBLANES
GROUPS_PER_TILE = LANES // GROUP
STATE_COLS = GROUPS_PER_TILE * N_STATE
N_TILES = D_MODEL // LANES
SCAN_UNROLL = 8

MESH = pl.DeviceIdType.MESH
HBM_SPEC = pl.BlockSpec(memory_space=pltpu.HBM)


def _params(sem=None, **kw):
    return pltpu.CompilerParams(dimension_semantics=sem, vmem_limit_bytes=VMEM_LIMIT, **kw)


def _tile(n, cap=1024):
    for t in (2048, 1024, 512, 256, 128, 64):
        if t <= cap and n % t == 0:
            return t
    raise ValueError(n)


def _rows_call(fn, rows, pars, row_outs, par_outs, *, tm, name):
    m = rows[0][0].shape[0]
    nr, npar, nro, npo = len(rows), len(pars), len(row_outs), len(par_outs)

    def body(*refs):
        rin = refs[:nr]
        pin = refs[nr:nr + npar]
        rout = refs[nr + npar:nr + npar + nro]
        pout = refs[nr + npar + nro:]
        res = fn(*[r[...] for r in rin], *[p[...] for p in pin])
        if not isinstance(res, (tuple, list)):
            res = (res,)
        for r, v in zip(rout, res[:nro]):
            r[...] = v.astype(r.dtype)
        if npo:
            @pl.when(pl.program_id(0) == 0)
            def _():
                for p in pout:
                    p[...] = jnp.zeros(p.shape, p.dtype)
            for p, v in zip(pout, res[nro:]):
                p[...] += v

    in_specs = [pl.BlockSpec((tm, w), functools.partial(lambda cb, i: (i, cb), cb)) for (_, w, cb) in rows]
    in_specs += [pl.BlockSpec(p.shape, lambda i: (0, 0)) for p in pars]
    out_specs = [pl.BlockSpec((tm, w), lambda i: (i, 0)) for (w, _) in row_outs]
    out_specs += [pl.BlockSpec(s, lambda i: (0, 0)) for s in par_outs]
    out_shape = [jax.ShapeDtypeStruct((m, w), dt) for (w, dt) in row_outs]
    out_shape += [jax.ShapeDtypeStruct(s, F32) for s in par_outs]
    res = pl.pallas_call(
        body, name=name, grid=(m // tm,), in_specs=in_specs, out_specs=out_specs, out_shape=out_shape,
        compiler_params=_params(("arbitrary",) if npo else ("parallel",)),
    )(*[a for (a, _, _) in rows], *pars)
    return res


def _f32(*vals):
    return [v.astype(F32) for v in vals]


def _f_rms(x, g):
    r = lax.rsqrt(jnp.mean(x * x, axis=-1, keepdims=True) + EPS)
    return x * r * g


def _rms_fwd(x, g, name):
    return _rows_call(lambda xv, gv: _f_rms(xv, gv), [(x, D_MODEL, 0)], [g], [(D_MODEL, BF16)], [],
                      tm=256, name=name)[0]


def _rms_bwd(x, dh, dres, g, name):
    def fn(xv, dhv, drv, gv):
        _, vjp = jax.vjp(_f_rms, xv, gv)
        dx, dg = vjp(dhv)
        dx = dx + drv
        return dx, dx, dg
    return _rows_call(fn, [(x, D_MODEL, 0), (dh, D_MODEL, 0), (dres, D_MODEL, 0)], [g],
                      [(D_MODEL, F32), (D_MODEL, BF16)], [(1, D_MODEL)], tm=256, name=name)


def _rot_heads(xv, cos, sin, scale):
    half = QK_DIM // 2
    outs = []
    for h in range(HEADS):
        x1 = xv[:, h * QK_DIM:h * QK_DIM + half]
        x2 = xv[:, h * QK_DIM + half:(h + 1) * QK_DIM]
        outs += [(x1 * cos - x2 * sin) * scale, (x1 * sin + x2 * cos) * scale]
    return jnp.concatenate(outs, axis=1)


def _rot_fwd(proj, cos, sin, name):
    def fn(q, k, cv, sv):
        return _rot_heads(q, cv, sv, 1.0), _rot_heads(k, cv, sv, QK_DIM ** -0.5)
    return _rows_call(fn, [(proj, QK_WIDTH, 0), (proj, QK_WIDTH, 1), (cos, LANES, 0), (sin, LANES, 0)], [],
                      [(QK_WIDTH, BF16), (QK_WIDTH, BF16)], [], tm=256, name=name)


def _rot_bwd(dqr, dkr, dv, cos, sin, name):
    def fn(dq, dk, dvv, cv, sv):
        return jnp.concatenate([_rot_heads(dq, cv, -sv, 1.0), _rot_heads(dk, cv, -sv, QK_DIM ** -0.5), dvv], axis=1)
    return _rows_call(fn, [(dqr, QK_WIDTH, 0), (dkr, QK_WIDTH, 0), (dv, D_MODEL, 0), (cos, LANES, 0), (sin, LANES, 0)],
                      [], [(2 * QK_WIDTH + D_MODEL, BF16)], [], tm=256, name=name)[0]


def _f_post1(y0, y1, y2, y3, g, gr, s5, u, dsk):
    yn = [yh * lax.rsqrt(jnp.mean(yh * yh, axis=-1, keepdims=True) + EPS) for yh in (y0, y1, y2, y3)]
    ret = jax.nn.sigmoid(gr) * (jax.nn.silu(g) * jnp.concatenate(yn, axis=1))
    ysg = jax.nn.gelu(s5 + dsk * u)
    return ret, ysg


def _post1_rows(y, proj, s5y):
    rows = [(y, V_DIM, h) for h in range(HEADS)]
    rows += [(proj, D_MODEL, 2), (proj, D_MODEL, 4), (s5y, D_MODEL, 0), (proj, D_MODEL, 3)]
    return rows


def _post1_fwd(y, proj, s5y, dsk, name):
    def fn(*vals):
        ret, ysg = _f_post1(*vals)
        return ret, ysg, ysg
    return _rows_call(fn, _post1_rows(y, proj, s5y), [dsk],
                      [(D_MODEL, F32), (D_MODEL, F32), (D_MODEL, BF16)], [], tm=128, name=name)


def _post1_bwd(y, proj, s5y, dret, dys, dsk, name):
    def fn(*vals):
        prim = vals[:8] + (vals[10],)
        _, vjp = jax.vjp(_f_post1, *prim)
        gy0, gy1, gy2, gy3, gg, ggr, gs5, gu, gd = vjp((vals[8], vals[9]))
        return jnp.concatenate([gy0, gy1, gy2, gy3], axis=1), gg, ggr, gs5, gu, gd
    rows = _post1_rows(y, proj, s5y) + [(dret, D_MODEL, 0), (dys, D_MODEL, 0)]
    return _rows_call(fn, rows, [dsk],
                      [(D_MODEL, BF16), (D_MODEL, BF16), (D_MODEL, BF16), (D_MODEL, F32), (D_MODEL, F32)],
                      [(1, D_MODEL)], tm=128, name=name)


def _f_merge(z, ysg, gs, ret, b):
    return ret + jax.nn.sigmoid(gs) * (ysg * jax.nn.sigmoid(z + b))


def _merge_fwd(z, ysg, proj, ret, b, name):
    return _rows_call(_f_merge, [(z, D_MODEL, 0), (ysg, D_MODEL, 0), (proj, D_MODEL, 5), (ret, D_MODEL, 0)], [b],
                      [(D_MODEL, BF16)], [], tm=128, name=name)[0]


def _merge_bwd(z, ysg, proj, ret, dm, b, name):
    def fn(zv, yv, gv, rv, dmv, bv):
        _, vjp = jax.vjp(_f_merge, zv, yv, gv, rv, bv)
        gz, gy, gg, _, gb = vjp(dmv)
        return gz, gy, gg, gb
    rows = [(z, D_MODEL, 0), (ysg, D_MODEL, 0), (proj, D_MODEL, 5), (ret, D_MODEL, 0), (dm, D_MODEL, 0)]
    return _rows_call(fn, rows, [b], [(D_MODEL, BF16), (D_MODEL, F32), (D_MODEL, BF16)], [(1, D_MODEL)],
                      tm=128, name=name)


def _f_glu(a, b):
    return jax.nn.silu(a) * b


def _glu_fwd(ab, name):
    return _rows_call(_f_glu, [(ab, D_FF, 0), (ab, D_FF, 1)], [], [(D_FF, BF16)], [], tm=128, name=name)[0]


def _glu_bwd(ab, df, name):
    def fn(a, b, d):
        _, vjp = jax.vjp(_f_glu, a, b)
        ga, gb = vjp(d)
        return jnp.concatenate([ga, gb], axis=1)
    return _rows_call(fn, [(ab, D_FF, 0), (ab, D_FF, 1), (df, D_FF, 0)], [], [(2 * D_FF, BF16)], [],
                      tm=128, name=name)[0]


def _loss_stage(x, tgt, g, name):
    def fn(xv, tv, gv):
        def lf(xx, gg):
            err = _f_rms(xx, gg) - tv
            row = jnp.mean(err * err, axis=-1, keepdims=True)
            return 0.5 * jnp.sum(row, axis=0, keepdims=True)
        l, vjp = jax.vjp(lf, xv, gv)
        dx, dg = vjp(jnp.ones((1, 1), F32))
        return dx, dx, jnp.broadcast_to(l, (1, LANES)), dg
    return _rows_call(fn, [(x, D_MODEL, 0), (tgt, D_MODEL, 0)], [g], [(D_MODEL, F32), (D_MODEL, BF16)],
                      [(1, LANES), (1, D_MODEL)], tm=256, name=name)


def _adam_math(wv, gv, mv, vv):
    mn = ADAM_B1 * mv + (1.0 - ADAM_B1) * gv
    vn = ADAM_B2 * vv + (1.0 - ADAM_B2) * (gv * gv)
    m_hat = mn / (1.0 - ADAM_B1 ** ADAM_STEP)
    v_hat = vn / (1.0 - ADAM_B2 ** ADAM_STEP)
    delta = -ADAM_LR * (m_hat / (jnp.sqrt(v_hat) + ADAM_EPS) + ADAM_WD * wv)
    return delta, mn, vn


def _adamw(w, g, m, v, name, after=None):
    rows, cols = w.shape
    tm = _tile(rows, 128 if cols > D_FF // N_CHIPS else (256 if cols > LANES else 512))
    fn = _adam_math if after is None else (lambda wv, gv, mv, vv, _: _adam_math(wv, gv, mv, vv))
    return _rows_call(fn, [(w, cols, 0), (g, cols, 0), (m, cols, 0), (v, cols, 0)], [] if after is None else [after],
                      [(cols, F32)] * 3, [], tm=tm, name=name)


def _adamw_nd(w, g, m, v, name):
    shape = w.shape
    lead = math.prod(shape[:-2])
    blk = (lead // 8,) + shape[-2:]
    three_d = lambda a: a.reshape((lead,) + shape[-2:])

    def body(w_ref, g_ref, m_ref, v_ref, d_ref, mn_ref, vn_ref):
        d_ref[...], mn_ref[...], vn_ref[...] = _adam_math(w_ref[...], g_ref[...], m_ref[...], v_ref[...])

    spec = pl.BlockSpec(blk, lambda i: (i, 0, 0))
    res = pl.pallas_call(
        body, name=name, grid=(8,), in_specs=[spec] * 4, out_specs=[spec] * 3,
        out_shape=[jax.ShapeDtypeStruct((lead,) + shape[-2:], F32)] * 3,
        compiler_params=_params(("parallel",)),
    )(three_d(w), three_d(g), three_d(m), three_d(v))
    return [r.reshape(shape) for r in res]


MATMUL_VMEM_BUDGET = 44 * 1024 * 1024


def _matmul_tiles(m, n, k, out_bytes, has_add):
    if k > 2048:
        return _tile(m, 1024), _tile(n, 1024), _tile(k, 1024)
    tm, tn, tk = _tile(m, 2048), _tile(n, 1024), k

    def footprint():
        acc = 4 * tm * tn if k // tk > 1 else 0
        return 2 * 2 * (tm * tk + tk * tn) + 2 * (out_bytes + 4 * has_add) * tm * tn + acc

    while footprint() > MATMUL_VMEM_BUDGET:
        if tn > 512 and n % (tn // 2) == 0:
            tn //= 2
        elif tk > 512 and k % (tk // 2) == 0:
            tk //= 2
        else:
            tm //= 2
    return tm, tn, tk


def _matmul(a, b, mode, out_dtypes, *, name, add=None):
    if mode == "nn":
        (m, k), (_, n) = a.shape, b.shape
    elif mode == "nt":
        (m, k), (n, _) = a.shape, b.shape
    else:
        (k, m), (_, n) = a.shape, b.shape
    n_out = len(out_dtypes)
    has_add = add is not None
    tm, tn, tk = _matmul_tiles(m, n, k, sum(jnp.dtype(dt).itemsize for dt in out_dtypes), has_add)
    nk = k // tk
    if mode == "nn":
        a_spec = pl.BlockSpec((tm, tk), lambda i, j, kk: (i, kk))
        b_spec = pl.BlockSpec((tk, tn), lambda i, j, kk: (kk, j))
        dims = (((1,), (0,)), ((), ()))
    elif mode == "nt":
        a_spec = pl.BlockSpec((tm, tk), lambda i, j, kk: (i, kk))
        b_spec = pl.BlockSpec((tn, tk), lambda i, j, kk: (j, kk))
        dims = (((1,), (1,)), ((), ()))
    else:
        a_spec = pl.BlockSpec((tk, tm), lambda i, j, kk: (kk, i))
        b_spec = pl.BlockSpec((tk, tn), lambda i, j, kk: (kk, j))
        dims = (((0,), (0,)), ((), ()))

    def body(*refs):
        a_ref, b_ref = refs[0], refs[1]
        add_ref = refs[2] if has_add else None
        outs = refs[2 + has_add:2 + has_add + n_out]

        def finish(r):
            if has_add:
                r = r + add_ref[...]
            for o in outs:
                o[...] = r.astype(o.dtype)

        if nk == 1:
            finish(lax.dot_general(a_ref[...], b_ref[...], dims, preferred_element_type=F32))
            return
        acc = refs[-1]
        kk = pl.program_id(2)

        @pl.when(kk == 0)
        def _():
            acc[...] = jnp.zeros(acc.shape, F32)

        acc[...] += lax.dot_general(a_ref[...], b_ref[...], dims, preferred_element_type=F32)

        @pl.when(kk == nk - 1)
        def _():
            finish(acc[...])

    in_specs = [a_spec, b_spec]
    args = [a, b]
    if has_add:
        in_specs.append(pl.BlockSpec((tm, tn), lambda i, j, kk: (i, j)))
        args.append(add)
    return pl.pallas_call(
        body, name=name, grid=(m // tm, n // tn, nk), in_specs=in_specs,
        out_specs=[pl.BlockSpec((tm, tn), lambda i, j, kk: (i, j))] * n_out,
        out_shape=[jax.ShapeDtypeStruct((m, n), dt) for dt in out_dtypes],
        scratch_shapes=[pltpu.VMEM((tm, tn), F32)] if nk > 1 else [],
        compiler_params=_params(("parallel", "parallel", "arbitrary")),
    )(*args)


RET_TQ = 512


def _decay(lg_ref, i, tq, seq):
    n_idx = i * tq + lax.broadcasted_iota(jnp.int32, (tq, seq), 0)
    m_idx = lax.broadcasted_iota(jnp.int32, (tq, seq), 1)
    diff = (n_idx - m_idx).astype(F32)
    lgf = lg_ref[0, 0:1, 0:1]
    lgb = lg_ref[0, 1:2, 0:1]
    causal = diff >= 0
    return jnp.exp(jnp.where(causal, lgf * diff, -lgb * diff)), diff, causal


_NT = (((1,), (1,)), ((), ()))
_TN = (((0,), (0,)), ((), ()))


def _ret_fwd(qr, kr, proj, lg, name):
    seq = qr.shape[0]
    tq = RET_TQ
    v_blk0 = (2 * QK_WIDTH) // V_DIM

    def body(q_ref, k_ref, v_ref, lg_ref, y_ref):
        i = pl.program_id(1)
        s = lax.dot_general(q_ref[...], k_ref[...], _NT, preferred_element_type=F32)
        dm, _, _ = _decay(lg_ref, i, tq, seq)
        p = (s * dm).astype(BF16)
        y_ref[...] = jnp.dot(p, v_ref[...].astype(BF16), preferred_element_type=F32)

    return pl.pallas_call(
        body, name=name, grid=(HEADS, seq // tq),
        in_specs=[pl.BlockSpec((tq, QK_DIM), lambda h, i: (i, h)),
                  pl.BlockSpec((seq, QK_DIM), lambda h, i: (0, h)),
                  pl.BlockSpec((seq, V_DIM), lambda h, i: (0, v_blk0 + h)),
                  pl.BlockSpec((1, 2, LANES), lambda h, i: (h, 0, 0))],
        out_specs=pl.BlockSpec((tq, V_DIM), lambda h, i: (i, h)),
        out_shape=jax.ShapeDtypeStruct((seq, HEADS * V_DIM), F32),
        compiler_params=_params(("parallel", "parallel")),
    )(qr, kr, proj, lg)


def _ret_bwd(qr, kr, proj, dy, lg, name):
    seq = qr.shape[0]
    tq = RET_TQ
    v_blk0 = (2 * QK_WIDTH) // V_DIM

    def body(q_ref, k_ref, v_ref, dy_ref, lg_ref, dq_ref, dk_ref, dv_ref, dlg_ref):
        i = pl.program_id(1)

        @pl.when(i == 0)
        def _():
            dk_ref[...] = jnp.zeros(dk_ref.shape, F32)
            dv_ref[...] = jnp.zeros(dv_ref.shape, F32)
            dlg_ref[...] = jnp.zeros(dlg_ref.shape, F32)

        q = q_ref[...]
        k = k_ref[...]
        vb = v_ref[...].astype(BF16)
        dyb = dy_ref[...]
        s = lax.dot_general(q, k, _NT, preferred_element_type=F32)
        dm, diff, causal = _decay(lg_ref, i, tq, seq)
        p = s * dm
        dp = lax.dot_general(dyb, vb, _NT, preferred_element_type=F32)
        dv_ref[...] += lax.dot_general(p.astype(BF16), dyb, _TN, preferred_element_type=F32)
        ds = (dp * dm).astype(BF16)
        dq_ref[...] = jnp.dot(ds, k, preferred_element_type=F32)
        dk_ref[...] += lax.dot_general(ds, q, _TN, preferred_element_type=F32)
        gd = dp * p * diff
        dlf = jnp.sum(jnp.sum(jnp.where(causal, gd, 0.0), axis=1, keepdims=True), axis=0, keepdims=True)
        dlb = jnp.sum(jnp.sum(jnp.where(causal, 0.0, -gd), axis=1, keepdims=True), axis=0, keepdims=True)
        row = lax.broadcasted_iota(jnp.int32, (2, LANES), 0)
        dlg_ref[0] += jnp.where(row == 0, dlf, dlb)

    return pl.pallas_call(
        body, name=name, grid=(HEADS, seq // tq),
        in_specs=[pl.BlockSpec((tq, QK_DIM), lambda h, i: (i, h)),
                  pl.BlockSpec((seq, QK_DIM), lambda h, i: (0, h)),
                  pl.BlockSpec((seq, V_DIM), lambda h, i: (0, v_blk0 + h)),
                  pl.BlockSpec((tq, V_DIM), lambda h, i: (i, h)),
                  pl.BlockSpec((1, 2, LANES), lambda h, i: (h, 0, 0))],
        out_specs=[pl.BlockSpec((tq, QK_DIM), lambda h, i: (i, h)),
                   pl.BlockSpec((seq, QK_DIM), lambda h, i: (0, h)),
                   pl.BlockSpec((seq, V_DIM), lambda h, i: (0, h)),
                   pl.BlockSpec((1, 2, LANES), lambda h, i: (h, 0, 0))],
        out_shape=[jax.ShapeDtypeStruct((seq, QK_WIDTH), F32), jax.ShapeDtypeStruct((seq, QK_WIDTH), F32),
                   jax.ShapeDtypeStruct((seq, HEADS * V_DIM), F32), jax.ShapeDtypeStruct((HEADS, 2, LANES), F32)],
        compiler_params=_params(("parallel", "arbitrary")),
    )(qr, kr, proj, dy, lg)


def _shift_rows(v, reverse):
    row = lax.broadcasted_iota(jnp.int32, v.shape, 0)
    if reverse:
        return jnp.where(row == SEGMENTS - 1, 0.0, pltpu.roll(v, SEGMENTS - 1, 0))
    return jnp.where(row == 0, 0.0, pltpu.roll(v, 1, 0))


def _slab(t):
    if isinstance(t, int):
        return pl.ds(t * SEGMENTS, SEGMENTS)
    return pl.ds(pl.multiple_of(t * SEGMENTS, SEGMENTS), SEGMENTS)


def _unrolled_loop(body, lo, hi, init):
    main = (hi - lo) // SCAN_UNROLL

    def unrolled(g, carry):
        for k in range(SCAN_UNROLL):
            carry = body(lo + g * SCAN_UNROLL + k, carry)
        return carry

    carry = lax.fori_loop(0, main, unrolled, init)
    for t in range(lo + main * SCAN_UNROLL, hi):
        carry = body(t, carry)
    return carry


def _scan(xr_ref, xi_ref, lam, reverse, conj):
    steps = xr_ref.shape[0] // SEGMENTS
    cols = xr_ref.shape[1]
    lr = jnp.broadcast_to(lam[0], (SEGMENTS, cols))
    li = jnp.broadcast_to(lam[1], (SEGMENTS, cols))
    lrt = jnp.broadcast_to(lam[2], (SEGMENTS, cols))
    lit = jnp.broadcast_to(lam[3], (SEGMENTS, cols))
    if conj:
        li, lit = -li, -lit
    zero = jnp.zeros((SEGMENTS, cols), F32)

    def rows_of(t):
        return _slab(steps - 1 - t if reverse else t)

    def advance(t, carry):
        sr, si = carry
        rows = rows_of(t)
        return lr * sr - li * si + xr_ref[rows, :], lr * si + li * sr + xi_ref[rows, :]

    def step(t, carry):
        nr, ni = advance(t, carry)
        rows = rows_of(t)
        xr_ref[rows, :] = nr
        xi_ref[rows, :] = ni
        return nr, ni

    def run(body, init):
        return _unrolled_loop(body, 0, steps, init)

    er, ei = run(advance, (zero, zero))
    cr, ci = zero, zero
    for _ in range(SEGMENTS - 1):
        tr = er + lrt * cr - lit * ci
        ti = ei + lrt * ci + lit * cr
        cr, ci = _shift_rows(tr, reverse), _shift_rows(ti, reverse)
    run(step, (cr, ci))


def _permute_in(dst_ref, src_ref):
    steps = src_ref.shape[0] // SEGMENTS
    for s in range(SEGMENTS):
        dst_ref[pl.ds(s, steps, stride=SEGMENTS), :] = src_ref[s * steps:(s + 1) * steps, :].astype(dst_ref.dtype)


def _unpermute(src_ref, s):
    steps = src_ref.shape[0] // SEGMENTS
    return src_ref[pl.ds(s, steps, stride=SEGMENTS), :]


def _s5_fwd(proj, bblk, cblk, lam, name):
    seq = proj.shape[0]
    u_blk0 = (2 * QK_WIDTH + 2 * D_MODEL) // LANES
    sc = STATE_COLS

    def body(u_ref, b_ref, c_ref, lam_ref, y_ref, up_ref, yp_ref, xr_ref, xi_ref):
        _permute_in(up_ref, u_ref)
        ub = up_ref[...].astype(BF16)
        for d in range(2):
            xr_ref[...] = jnp.dot(ub, b_ref[d, :, 0:sc], preferred_element_type=F32)
            xi_ref[...] = jnp.dot(ub, b_ref[d, :, sc:2 * sc], preferred_element_type=F32)
            lm = [lam_ref[d, r:r + 1, :] for r in range(4)]
            _scan(xr_ref, xi_ref, lm, reverse=(d == 1), conj=False)
            yd = (jnp.dot(xr_ref[...].astype(BF16), c_ref[d, 0:sc, :], preferred_element_type=F32)
                  + jnp.dot(xi_ref[...].astype(BF16), c_ref[d, sc:2 * sc, :], preferred_element_type=F32))
            if d == 0:
                yp_ref[...] = yd
            else:
                yp_ref[...] += yd
        steps = seq // SEGMENTS
        for s in range(SEGMENTS):
            y_ref[s * steps:(s + 1) * steps, :] = _unpermute(yp_ref, s)

    return pl.pallas_call(
        body, name=name, grid=(N_TILES,),
        in_specs=[pl.BlockSpec((seq, LANES), lambda j: (0, u_blk0 + j)),
                  pl.BlockSpec((2, None, LANES, 2 * sc), lambda j: (0, j, 0, 0)),
                  pl.BlockSpec((2, None, 2 * sc, LANES), lambda j: (0, j, 0, 0)),
                  pl.BlockSpec((2, None, 4, sc), lambda j: (0, j, 0, 0))],
        out_specs=pl.BlockSpec((seq, LANES), lambda j: (0, j)),
        out_shape=jax.ShapeDtypeStruct((seq, D_MODEL), F32),
        scratch_shapes=[pltpu.VMEM((seq, LANES), F32), pltpu.VMEM((seq, LANES), F32),
                        pltpu.VMEM((seq, sc), F32), pltpu.VMEM((seq, sc), F32)],
        compiler_params=_params(("parallel",)),
    )(proj, bblk, cblk, lam)


def _s5_bwd(proj, dy, du_part, bblk, cblk, lam, name):
    seq = proj.shape[0]
    u_blk0 = (2 * QK_WIDTH + 2 * D_MODEL) // LANES
    sc = STATE_COLS
    steps = seq // SEGMENTS

    def body(u_ref, dy_ref, dup_ref, b_ref, c_ref, lam_ref, du_ref, db_ref, dc_ref, dlam_ref,
             up_ref, dyp_ref, dua_ref, xr_ref, xi_ref, gr_ref, gi_ref):
        _permute_in(up_ref, u_ref)
        _permute_in(dyp_ref, dy_ref)
        ub = up_ref[...].astype(BF16)
        dyb = dyp_ref[...].astype(BF16)
        ubt = up_ref[...].T.astype(BF16)
        dybt = dyp_ref[...].T.astype(BF16)
        for d in range(2):
            reverse = d == 1
            xr_ref[...] = jnp.dot(ub, b_ref[d, :, 0:sc], preferred_element_type=F32)
            xi_ref[...] = jnp.dot(ub, b_ref[d, :, sc:2 * sc], preferred_element_type=F32)
            lm = [lam_ref[d, r:r + 1, :] for r in range(4)]
            _scan(xr_ref, xi_ref, lm, reverse=reverse, conj=False)
            xrb = xr_ref[...].astype(BF16)
            xib = xi_ref[...].astype(BF16)
            dc_ref[d, :, 0:sc] = jnp.dot(dybt, xrb, preferred_element_type=F32)
            dc_ref[d, :, sc:2 * sc] = jnp.dot(dybt, xib, preferred_element_type=F32)
            gr_ref[...] = lax.dot_general(dyb, c_ref[d, 0:sc, :], _NT, preferred_element_type=F32)
            gi_ref[...] = lax.dot_general(dyb, c_ref[d, sc:2 * sc, :], _NT, preferred_element_type=F32)
            _scan(gr_ref, gi_ref, lm, reverse=not reverse, conj=True)

            def acc_step(t, carry):
                ar, ai = carry
                prev = _slab(t + 1 if reverse else t - 1)
                pr = xr_ref[prev, :]
                pi = xi_ref[prev, :]
                zr = gr_ref[_slab(t), :]
                zi = gi_ref[_slab(t), :]
                return ar + zr * pr + zi * pi, ai + zi * pr - zr * pi

            zero = jnp.zeros((SEGMENTS, sc), F32)
            if reverse:
                ar, ai = _unrolled_loop(acc_step, 0, steps - 1, (zero, zero))
                edge = _slab(steps - 1)
                pr = _shift_rows(xr_ref[_slab(0), :], True)
                pi = _shift_rows(xi_ref[_slab(0), :], True)
            else:
                ar, ai = _unrolled_loop(acc_step, 1, steps, (zero, zero))
                edge = _slab(0)
                pr = _shift_rows(xr_ref[_slab(steps - 1), :], False)
                pi = _shift_rows(xi_ref[_slab(steps - 1), :], False)
            zr = gr_ref[edge, :]
            zi = gi_ref[edge, :]
            ar = ar + zr * pr + zi * pi
            ai = ai + zi * pr - zr * pi
            dlam_ref[d, 0:1, :] = jnp.sum(ar, axis=0, keepdims=True)
            dlam_ref[d, 1:2, :] = jnp.sum(ai, axis=0, keepdims=True)

            grb = gr_ref[...].astype(BF16)
            gib = gi_ref[...].astype(BF16)
            db_ref[d, :, 0:sc] = jnp.dot(ubt, grb, preferred_element_type=F32)
            db_ref[d, :, sc:2 * sc] = jnp.dot(ubt, gib, preferred_element_type=F32)
            dud = (lax.dot_general(grb, b_ref[d, :, 0:sc], _NT, preferred_element_type=F32)
                   + lax.dot_general(gib, b_ref[d, :, sc:2 * sc], _NT, preferred_element_type=F32))
            if d == 0:
                dua_ref[...] = dud
            else:
                dua_ref[...] += dud
        for s in range(SEGMENTS):
            rows = slice(s * steps, (s + 1) * steps)
            du_ref[rows, :] = (_unpermute(dua_ref, s) + dup_ref[rows, :]).astype(du_ref.dtype)

    return pl.pallas_call(
        body, name=name, grid=(N_TILES,),
        in_specs=[pl.BlockSpec((seq, LANES), lambda j: (0, u_blk0 + j)),
                  pl.BlockSpec((seq, LANES), lambda j: (0, j)),
                  pl.BlockSpec((seq, LANES), lambda j: (0, j)),
                  pl.BlockSpec((2, None, LANES, 2 * sc), lambda j: (0, j, 0, 0)),
                  pl.BlockSpec((2, None, 2 * sc, LANES), lambda j: (0, j, 0, 0)),
                  pl.BlockSpec((2, None, 4, sc), lambda j: (0, j, 0, 0))],
        out_specs=[pl.BlockSpec((seq, LANES), lambda j: (0, j)),
                   pl.BlockSpec((2, None, LANES, 2 * sc), lambda j: (0, j, 0, 0)),
                   pl.BlockSpec((2, None, LANES, 2 * sc), lambda j: (0, j, 0, 0)),
                   pl.BlockSpec((2, None, 2, sc), lambda j: (0, j, 0, 0))],
        out_shape=[jax.ShapeDtypeStruct((seq, D_MODEL), BF16),
                   jax.ShapeDtypeStruct((2, N_TILES, LANES, 2 * sc), F32),
                   jax.ShapeDtypeStruct((2, N_TILES, LANES, 2 * sc), F32),
                   jax.ShapeDtypeStruct((2, N_TILES, 2, sc), F32)],
        scratch_shapes=[pltpu.VMEM((seq, LANES), F32), pltpu.VMEM((seq, LANES), F32), pltpu.VMEM((seq, LANES), F32),
                        pltpu.VMEM((seq, sc), F32), pltpu.VMEM((seq, sc), F32),
                        pltpu.VMEM((seq, sc), F32), pltpu.VMEM((seq, sc), F32)],
        compiler_params=_params(("parallel",)),
    )(proj, dy, du_part, bblk, cblk, lam)


def _s5_discretize(a_re, a_im, log_dt, b_re, b_im, seg_len):
    dt = jnp.exp(log_dt)[..., None]
    e = jnp.exp(a_re * dt)
    lr, li = e * jnp.cos(a_im * dt), e * jnp.sin(a_im * dt)
    et = jnp.exp(a_re * dt * seg_len)
    lrt, lit = et * jnp.cos(a_im * dt * seg_len), et * jnp.sin(a_im * dt * seg_len)
    den = a_re * a_re + a_im * a_im
    qr = ((lr - 1.0) * a_re + li * a_im) / den
    qi = (li * a_re - (lr - 1.0) * a_im) / den
    br = qr[..., None] * b_re - qi[..., None] * b_im
    bi = qr[..., None] * b_im + qi[..., None] * b_re
    return lr, li, lrt, lit, br, bi


def _s5_pack(lr, li, lrt, lit, br, bi, c_re, c_im):
    eye = jnp.eye(GROUPS_PER_TILE, dtype=F32)

    def bd_b(b):
        b5 = b.reshape(2, N_TILES, GROUPS_PER_TILE, N_STATE, GROUP)
        return jnp.einsum("dtgph,gk->dtghkp", b5, eye).reshape(2, N_TILES, LANES, STATE_COLS)

    def bd_c(c):
        c5 = c.reshape(2, N_TILES, GROUPS_PER_TILE, GROUP, N_STATE)
        return jnp.einsum("dtghp,gk->dtkpgh", c5, eye).reshape(2, N_TILES, STATE_COLS, LANES)

    bblk = jnp.concatenate([bd_b(br), bd_b(bi)], axis=3)
    cblk = jnp.concatenate([bd_c(c_re), -bd_c(c_im)], axis=2)
    lam = jnp.stack([v.reshape(2, N_TILES, STATE_COLS) for v in (lr, li, lrt, lit)], axis=2)
    return bblk, cblk, lam


def _s5_unpack(dbblk, dcblk, dlam):
    eye = jnp.eye(GROUPS_PER_TILE, dtype=F32)

    def diag_b(d):
        d6 = d.reshape(2, N_TILES, GROUPS_PER_TILE, GROUP, GROUPS_PER_TILE, N_STATE)
        return jnp.einsum("dtghkp,gk->dtgph", d6, eye).reshape(2, N_GROUPS, N_STATE, GROUP)

    def diag_c(d):
        d6 = d.reshape(2, N_TILES, GROUPS_PER_TILE, GROUP, GROUPS_PER_TILE, N_STATE)
        return jnp.einsum("dtghkp,gk->dtghp", d6, eye).reshape(2, N_GROUPS, GROUP, N_STATE)

    dbr, dbi = diag_b(dbblk[..., :STATE_COLS]), diag_b(dbblk[..., STATE_COLS:])
    dcr, dci = diag_c(dcblk[..., :STATE_COLS]), -diag_c(dcblk[..., STATE_COLS:])
    dlr = dlam[:, :, 0, :].reshape(2, N_GROUPS, N_STATE)
    dli = dlam[:, :, 1, :].reshape(2, N_GROUPS, N_STATE)
    return dlr, dli, dbr, dbi, dcr, dci


def _pos():
    return lax.axis_index("x"), lax.axis_index("y"), lax.axis_index("c")


def _remote(src, dst, ssem, rsem, dev):
    return pltpu.make_async_remote_copy(src_ref=src, dst_ref=dst, send_sem=ssem, recv_sem=rsem,
                                        device_id=dev, device_id_type=MESH)


_PIECES = (
    ("w_in", "in", D_MODEL, IN_WIDTH // N_CHIPS, 0, IN_WIDTH // N_CHIPS, 0),
    ("w_glu", "glu", D_MODEL // N_CHIPS, D_MODEL, D_MODEL // N_CHIPS, 0, 0),
    ("w_out", "out", D_MODEL // N_CHIPS, D_MODEL, D_MODEL // N_CHIPS, 0, 0),
    ("w_ffn_gate", "gu", D_MODEL, D_FF // N_CHIPS, 0, D_FF // N_CHIPS, 0),
    ("w_ffn_up", "gu", D_MODEL, D_FF // N_CHIPS, 0, D_FF // N_CHIPS, D_FF),
    ("w_ffn_down", "down", D_FF // N_CHIPS, D_MODEL, D_FF // N_CHIPS, 0, 0),
)
_BUFFERS = (("in", D_MODEL, IN_WIDTH), ("glu", D_MODEL, D_MODEL), ("out", D_MODEL, D_MODEL),
            ("gu", D_MODEL, 2 * D_FF), ("down", D_FF, D_MODEL))
_BUF_INDEX = {name: t for t, (name, _, _) in enumerate(_BUFFERS)}
N_PIECES = len(_PIECES)
N_BUFFERS = len(_BUFFERS)


def _own_block(piece, tm):
    _, _, _, cs, rstep, cstep, coff = piece
    return lambda i, chip: (i + chip * (rstep // tm), coff // cs + chip * (cstep // cs))


def _cast_place(piece, w3, layer, prev, chip_arr, name):
    _, r, cc = w3.shape
    _, rf, cf = _BUFFERS[_BUF_INDEX[piece[1]]]
    tm = _tile(r, 256)
    own = _own_block(piece, tm)

    def body(s_ref, w_ref, *rest):
        rest[-1][...] = w_ref[...].astype(BF16)

    in_specs = [pl.BlockSpec((None, tm, cc), lambda i, s: (layer, i, 0))]
    args = [w3]
    aliases = {}
    if prev is not None:
        in_specs.append(pl.BlockSpec(memory_space=pl.ANY))
        args.append(prev)
        aliases = {2: 0}
    return pl.pallas_call(
        body, name=name,
        grid_spec=pltpu.PrefetchScalarGridSpec(
            num_scalar_prefetch=1, grid=(r // tm,), in_specs=in_specs,
            out_specs=pl.BlockSpec((tm, cc), lambda i, s: own(i, s[0]))),
        out_shape=jax.ShapeDtypeStruct((rf, cf), BF16), input_output_aliases=aliases,
        compiler_params=_params(("parallel",)),
    )(chip_arr, *args)


_GATHER_GROUPS = ((0, (0,)), (0, (1, 2)), (0, (3, 4)), (0, (5,)), (1, (0,)), (1, (1, 2)), (1, (3, 4)), (1, (5,)))
GROUPS_PER_LAYER = len(_GATHER_GROUPS) // DEPTH
_SPLIT_EFFECT = pltpu.SideEffectType.DATAFLOW_SIDE_EFFECTING
SEM_SPEC = pl.BlockSpec(memory_space=pltpu.SEMAPHORE)
BF16_ROWS = 2 * SUBLANES


def _group_keys(g):
    layer, pieces = _GATHER_GROUPS[g]
    keys = []
    for p in pieces:
        if (_PIECES[p][1], layer) not in keys:
            keys.append((_PIECES[p][1], layer))
    return keys


def _half_view(ref, piece, j, c):
    _, _, rs, cs, rstep, cstep, coff = piece
    half = rs // 2
    return ref.at[pl.ds(pl.multiple_of(j * rstep + c * half, BF16_ROWS), half), pl.ds(coff + j * cstep, cs)]


def _for_my_chip(fn):
    x, y, _ = _pos()
    for mine in range(N_CHIPS):
        pl.when(2 * x + y == mine)(functools.partial(fn, mine, [j for j in range(N_CHIPS) if j != mine]))


def _gather_start(groups, placed):
    keys = [k for g in groups for k in _group_keys(g)]
    nb, ng = len(keys), len(groups)

    def body(*refs):
        bufs = dict(zip(keys, refs[nb:2 * nb]))
        ssems = refs[2 * nb:2 * nb + ng]
        rsems = refs[2 * nb + ng:2 * nb + 2 * ng]
        token = refs[2 * nb + 2 * ng]
        _, _, c = _pos()

        def send(mine, others):
            for t, g in enumerate(groups):
                layer, pieces = _GATHER_GROUPS[g]
                for k, p in enumerate(pieces):
                    view = _half_view(bufs[(_PIECES[p][1], layer)], _PIECES[p], mine, c)
                    for j in others:
                        _remote(view, view, ssems[t].at[k * N_CHIPS + j], rsems[t].at[k * N_CHIPS + mine],
                                (j // 2, j % 2, c)).start()

        _for_my_chip(send)
        token[...] = jnp.zeros(token.shape, token.dtype)

    sems = [pltpu.SemaphoreType.DMA((N_CHIPS * len(_GATHER_GROUPS[g][1]),)) for g in groups]
    shapes = [jax.ShapeDtypeStruct(placed[k].shape, placed[k].dtype) for k in keys]
    res = pl.pallas_call(
        body, name="gather_start_g%d" % groups[0],
        in_specs=[HBM_SPEC] * nb,
        out_specs=[HBM_SPEC] * nb + [SEM_SPEC] * (2 * ng) + [pl.BlockSpec(memory_space=pltpu.VMEM)],
        out_shape=shapes + sems + sems + [jax.ShapeDtypeStruct((SUBLANES, LANES), F32)],
        input_output_aliases={t: t for t in range(nb)},
        compiler_params=_params(has_side_effects=_SPLIT_EFFECT),
    )(*[pltpu.with_memory_space_constraint(placed[k], pltpu.HBM) for k in keys])
    return (dict(zip(keys, res[:nb])), dict(zip(groups, res[nb:nb + ng])),
            dict(zip(groups, res[nb + ng:nb + 2 * ng])), res[nb + 2 * ng])


def _gather_wait(g, bufs, ssem, rsem, after):
    layer, pieces = _GATHER_GROUPS[g]
    keys = _group_keys(g)
    nb = len(keys)

    def body(*refs):
        ssem_ref, rsem_ref = refs[nb], refs[nb + 1]
        land = dict(zip(keys, refs[nb + 3:]))
        _, _, c = _pos()

        def wait(mine, others):
            for k, p in enumerate(pieces):
                ref = land[(_PIECES[p][1], layer)]
                for j in others:
                    cp = _remote(_half_view(ref, _PIECES[p], mine, c), _half_view(ref, _PIECES[p], j, c),
                                 ssem_ref.at[k * N_CHIPS + j], rsem_ref.at[k * N_CHIPS + j], (j // 2, j % 2, c))
                    cp.wait_send()
                    cp.wait_recv()

        _for_my_chip(wait)

    return pl.pallas_call(
        body, name="gather_wait_g%d" % g,
        in_specs=[HBM_SPEC] * nb + [SEM_SPEC, SEM_SPEC, pl.BlockSpec(memory_space=pl.ANY)],
        out_specs=[HBM_SPEC] * nb,
        out_shape=[jax.ShapeDtypeStruct(a.shape, a.dtype) for a in bufs],
        input_output_aliases={t: t for t in range(nb)},
        compiler_params=_params(has_side_effects=_SPLIT_EFFECT),
    )(*bufs, ssem, rsem, after)


def _gather_forward(g, bufs):
    layer, pieces = _GATHER_GROUPS[g]
    keys = _group_keys(g)
    nb = len(keys)

    def body(*refs):
        land = dict(zip(keys, refs[nb:2 * nb]))
        ssem, rsem = refs[2 * nb:]
        x, y, c = _pos()

        def forward(mine, others):
            cps = []
            for k, p in enumerate(pieces):
                ref = land[(_PIECES[p][1], layer)]
                for j in others:
                    view = _half_view(ref, _PIECES[p], j, c)
                    cp = _remote(view, view, ssem.at[k * N_CHIPS + j], rsem.at[k * N_CHIPS + j], (x, y, 1 - c))
                    cp.start()
                    cps.append(cp)
            for k, p in enumerate(pieces):
                ref = land[(_PIECES[p][1], layer)]
                for j in others:
                    view = _half_view(ref, _PIECES[p], j, 1 - c)
                    _remote(view, view, ssem.at[k * N_CHIPS + j], rsem.at[k * N_CHIPS + j], (x, y, 1 - c)).wait_recv()
            for cp in cps:
                cp.wait_send()

        _for_my_chip(forward)

    nsem = N_CHIPS * len(pieces)
    return pl.pallas_call(
        body, name="gather_forward_g%d" % g,
        in_specs=[HBM_SPEC] * nb, out_specs=[HBM_SPEC] * nb,
        out_shape=[jax.ShapeDtypeStruct(a.shape, a.dtype) for a in bufs],
        input_output_aliases={t: t for t in range(nb)},
        scratch_shapes=[pltpu.SemaphoreType.DMA((nsem,)), pltpu.SemaphoreType.DMA((nsem,))],
        compiler_params=_params(has_side_effects=True),
    )(*bufs)


def _forward_start(g, bufs):
    layer, pieces = _GATHER_GROUPS[g]
    keys = _group_keys(g)
    nb = len(keys)

    def body(*refs):
        land = dict(zip(keys, refs[nb:2 * nb]))
        ssem, rsem, token = refs[2 * nb:]
        x, y, c = _pos()

        def forward(mine, others):
            for k, p in enumerate(pieces):
                for j in others:
                    view = _half_view(land[(_PIECES[p][1], layer)], _PIECES[p], j, c)
                    _remote(view, view, ssem.at[k * N_CHIPS + j], rsem.at[k * N_CHIPS + j], (x, y, 1 - c)).start()

        _for_my_chip(forward)
        token[...] = jnp.zeros(token.shape, token.dtype)

    sem = pltpu.SemaphoreType.DMA((N_CHIPS * len(pieces),))
    res = pl.pallas_call(
        body, name="forward_start_g%d" % g,
        in_specs=[HBM_SPEC] * nb,
        out_specs=[HBM_SPEC] * nb + [SEM_SPEC, SEM_SPEC, pl.BlockSpec(memory_space=pltpu.VMEM)],
        out_shape=[jax.ShapeDtypeStruct(a.shape, a.dtype) for a in bufs]
        + [sem, sem, jax.ShapeDtypeStruct((SUBLANES, LANES), F32)],
        input_output_aliases={t: t for t in range(nb)},
        compiler_params=_params(has_side_effects=_SPLIT_EFFECT),
    )(*bufs)
    return list(res[:nb]), res[nb], res[nb + 1], res[nb + 2]


def _forward_wait(g, bufs, ssem, rsem, after):
    layer, pieces = _GATHER_GROUPS[g]
    keys = _group_keys(g)
    nb = len(keys)

    def body(*refs):
        ssem_ref, rsem_ref = refs[nb], refs[nb + 1]
        land = dict(zip(keys, refs[nb + 3:]))
        x, y, c = _pos()

        def wait(mine, others):
            for k, p in enumerate(pieces):
                ref = land[(_PIECES[p][1], layer)]
                for j in others:
                    cp = _remote(_half_view(ref, _PIECES[p], j, c), _half_view(ref, _PIECES[p], j, 1 - c),
                                 ssem_ref.at[k * N_CHIPS + j], rsem_ref.at[k * N_CHIPS + j], (x, y, 1 - c))
                    cp.wait_send()
                    cp.wait_recv()

        _for_my_chip(wait)

    return pl.pallas_call(
        body, name="forward_wait_g%d" % g,
        in_specs=[HBM_SPEC] * nb + [SEM_SPEC, SEM_SPEC, pl.BlockSpec(memory_space=pl.ANY)],
        out_specs=[HBM_SPEC] * nb,
        out_shape=[jax.ShapeDtypeStruct(a.shape, a.dtype) for a in bufs],
        input_output_aliases={t: t for t in range(nb)},
        compiler_params=_params(has_side_effects=_SPLIT_EFFECT),
    )(*bufs, ssem, rsem, after)


_REDUCE_GROUPS = (
    ((5, 1), (3, 1), (4, 1), (2, 1), (1, 1), (0, 1)),
    ((5, 0), (3, 0), (4, 0)),
    ((2, 0), (1, 0)),
    ((0, 0),),
)


def _reduce_keys(group):
    keys = []
    for p, layer in group:
        if (_PIECES[p][1], layer) not in keys:
            keys.append((_PIECES[p][1], layer))
    return keys


def _half_block(piece, tm):
    _, _, rs, cs, rstep, cstep, coff = piece
    return lambda i, j, c: (j * (rstep // tm) + c * (rs // 2 // tm) + i, coff // cs + j * (cstep // cs))


def _swap_start(g, dwb):
    group = _REDUCE_GROUPS[g]
    keys = _reduce_keys(group)
    nk = len(keys)

    def body(*refs):
        src = dict(zip(keys, refs[nk:2 * nk]))
        dst = dict(zip(keys, refs[2 * nk:3 * nk]))
        ssem, rsem, token = refs[3 * nk:]
        x, y, c = _pos()
        for k, (p, layer) in enumerate(group):
            key = (_PIECES[p][1], layer)
            for j in range(N_CHIPS):
                _remote(_half_view(src[key], _PIECES[p], j, 1 - c), _half_view(dst[key], _PIECES[p], j, 1 - c),
                        ssem.at[k * N_CHIPS + j], rsem.at[k * N_CHIPS + j], (x, y, 1 - c)).start()
        token[...] = jnp.zeros(token.shape, token.dtype)

    sem = pltpu.SemaphoreType.DMA((N_CHIPS * len(group),))
    shapes = [jax.ShapeDtypeStruct(dwb[k].shape, BF16) for k in keys]
    res = pl.pallas_call(
        body, name="swap_start_g%d" % g,
        in_specs=[HBM_SPEC] * nk,
        out_specs=[HBM_SPEC] * (2 * nk) + [SEM_SPEC, SEM_SPEC, pl.BlockSpec(memory_space=pltpu.VMEM)],
        out_shape=shapes + shapes + [sem, sem, jax.ShapeDtypeStruct((SUBLANES, LANES), F32)],
        input_output_aliases={t: t for t in range(nk)},
        compiler_params=_params(has_side_effects=_SPLIT_EFFECT),
    )(*[pltpu.with_memory_space_constraint(dwb[k], pltpu.HBM) for k in keys])
    return list(res[:nk]), list(res[nk:2 * nk]), res[2 * nk], res[2 * nk + 1], res[2 * nk + 2]


def _swap_wait(g, own, land, ssem, rsem, after):
    group = _REDUCE_GROUPS[g]
    keys = _reduce_keys(group)
    nk = len(keys)

    def body(*refs):
        ssem_ref, rsem_ref = refs[2 * nk], refs[2 * nk + 1]
        src = dict(zip(keys, refs[2 * nk + 3:3 * nk + 3]))
        dst = dict(zip(keys, refs[3 * nk + 3:]))
        x, y, c = _pos()
        for k, (p, layer) in enumerate(group):
            key = (_PIECES[p][1], layer)
            for j in range(N_CHIPS):
                cp = _remote(_half_view(src[key], _PIECES[p], j, 1 - c), _half_view(dst[key], _PIECES[p], j, c),
                             ssem_ref.at[k * N_CHIPS + j], rsem_ref.at[k * N_CHIPS + j], (x, y, 1 - c))
                cp.wait_send()
                cp.wait_recv()

    res = pl.pallas_call(
        body, name="swap_wait_g%d" % g,
        in_specs=[HBM_SPEC] * (2 * nk) + [SEM_SPEC, SEM_SPEC, pl.BlockSpec(memory_space=pl.ANY)],
        out_specs=[HBM_SPEC] * (2 * nk),
        out_shape=[jax.ShapeDtypeStruct(a.shape, a.dtype) for a in list(own) + list(land)],
        input_output_aliases={t: t for t in range(2 * nk)},
        compiler_params=_params(has_side_effects=_SPLIT_EFFECT),
    )(*own, *land, ssem, rsem, after)
    return dict(zip(keys, res[nk:]))


def _chip_partial(piece, dw, got, prev, c_arr, name):
    _, _, rs, cs, _, _, _ = piece
    half = rs // 2
    tm = _tile(half, 256)
    blk = _half_block(piece, tm)

    def body(s_ref, dw_ref, got_ref, *rest):
        rest[-1][...] = (dw_ref[...] + got_ref[...].astype(F32)).astype(BF16)

    spec = pl.BlockSpec((tm, cs), lambda j, i, s: blk(i, j, s[0]))
    in_specs = [spec, spec]
    args = [dw, got]
    aliases = {}
    if prev is not None:
        in_specs.append(pl.BlockSpec(memory_space=pl.ANY))
        args.append(prev)
        aliases = {3: 0}
    return pl.pallas_call(
        body, name=name,
        grid_spec=pltpu.PrefetchScalarGridSpec(
            num_scalar_prefetch=1, grid=(N_CHIPS, half // tm), in_specs=in_specs, out_specs=spec),
        out_shape=jax.ShapeDtypeStruct(dw.shape, BF16), input_output_aliases=aliases,
        compiler_params=_params(("parallel", "parallel")),
    )(c_arr, *args)


def _scatter_start(g, partials):
    group = _REDUCE_GROUPS[g]
    keys = _reduce_keys(group)
    nk, n = len(keys), len(group)

    def body(*refs):
        pt = dict(zip(keys, refs[nk:2 * nk]))
        land = refs[2 * nk:2 * nk + n]
        ssem, rsem, token = refs[2 * nk + n:]
        _, _, c = _pos()

        def send(mine, others):
            for k, (p, layer) in enumerate(group):
                for j in others:
                    _remote(_half_view(pt[(_PIECES[p][1], layer)], _PIECES[p], j, c), land[k].at[mine],
                            ssem.at[k * N_CHIPS + j], rsem.at[k * N_CHIPS + mine], (j // 2, j % 2, c)).start()

        _for_my_chip(send)
        token[...] = jnp.zeros(token.shape, token.dtype)

    sem = pltpu.SemaphoreType.DMA((N_CHIPS * n,))
    res = pl.pallas_call(
        body, name="scatter_start_g%d" % g,
        in_specs=[HBM_SPEC] * nk,
        out_specs=[HBM_SPEC] * (nk + n) + [SEM_SPEC, SEM_SPEC, pl.BlockSpec(memory_space=pltpu.VMEM)],
        out_shape=([jax.ShapeDtypeStruct(partials[k].shape, BF16) for k in keys]
                   + [jax.ShapeDtypeStruct((N_CHIPS, _PIECES[p][2] // 2, _PIECES[p][3]), BF16) for p, _ in group]
                   + [sem, sem, jax.ShapeDtypeStruct((SUBLANES, LANES), F32)]),
        input_output_aliases={t: t for t in range(nk)},
        compiler_params=_params(has_side_effects=_SPLIT_EFFECT),
    )(*[pltpu.with_memory_space_constraint(partials[k], pltpu.HBM) for k in keys])
    return list(res[:nk]), list(res[nk:nk + n]), res[nk + n], res[nk + n + 1], res[nk + n + 2]


def _scatter_wait(g, partials, land, ssem, rsem, after):
    group = _REDUCE_GROUPS[g]
    keys = _reduce_keys(group)
    nk, n = len(keys), len(group)

    def body(*refs):
        ssem_ref, rsem_ref = refs[nk + n], refs[nk + n + 1]
        pt = dict(zip(keys, refs[nk + n + 3:2 * nk + n + 3]))
        land_ref = refs[2 * nk + n + 3:]
        _, _, c = _pos()

        def wait(mine, others):
            for k, (p, layer) in enumerate(group):
                for j in others:
                    cp = _remote(_half_view(pt[(_PIECES[p][1], layer)], _PIECES[p], j, c), land_ref[k].at[j],
                                 ssem_ref.at[k * N_CHIPS + j], rsem_ref.at[k * N_CHIPS + j], (j // 2, j % 2, c))
                    cp.wait_send()
                    cp.wait_recv()

        _for_my_chip(wait)

    res = pl.pallas_call(
        body, name="scatter_wait_g%d" % g,
        in_specs=[HBM_SPEC] * (nk + n) + [SEM_SPEC, SEM_SPEC, pl.BlockSpec(memory_space=pl.ANY)],
        out_specs=[HBM_SPEC] * (nk + n),
        out_shape=[jax.ShapeDtypeStruct(a.shape, a.dtype) for a in list(partials) + list(land)],
        input_output_aliases={t: t for t in range(nk + n)},
        compiler_params=_params(has_side_effects=_SPLIT_EFFECT),
    )(*partials, *land, ssem, rsem, after)
    return list(res[nk:])


def _reduce_half(piece, layer, dw, got, land, prev, idx, name):
    _, _, rs, cs, _, _, _ = piece
    half = rs // 2
    tm = _tile(half, 256)
    blk = _half_block(piece, tm)

    def body(s_ref, dw_ref, got_ref, r1, r2, r3, *rest):
        acc = dw_ref[...] + got_ref[...].astype(F32)
        for r in (r1, r2, r3):
            acc = acc + r[...].astype(F32)
        rest[-1][...] = acc

    def land_map(k):
        return lambda i, s: ((s[1] + k) % N_CHIPS, i, 0)

    own = pl.BlockSpec((tm, cs), lambda i, s: blk(i, s[1], s[0]))
    in_specs = [own, own] + [pl.BlockSpec((None, tm, cs), land_map(k)) for k in (1, 2, 3)]
    args = [dw, got, land, land, land]
    aliases = {}
    if prev is not None:
        in_specs.append(pl.BlockSpec(memory_space=pl.ANY))
        args.append(prev)
        aliases = {6: 0}
    return pl.pallas_call(
        body, name=name,
        grid_spec=pltpu.PrefetchScalarGridSpec(
            num_scalar_prefetch=1, grid=(half // tm,), in_specs=in_specs,
            out_specs=pl.BlockSpec((None, tm, cs), lambda i, s: (layer, s[0] * (half // tm) + i, 0))),
        out_shape=jax.ShapeDtypeStruct((DEPTH, rs, cs), F32), input_output_aliases=aliases,
        compiler_params=_params(("parallel",)),
    )(idx, *args)


_SHARE_ORDER = (2, 1, 5, 3, 4, 0)


def _grad_half(ref, p, layer, cc):
    rows = _PIECES[p][2] // 2
    return ref.at[layer, pl.ds(pl.multiple_of(cc * rows, SUBLANES), rows), :]


def _share_start(reduced):
    def body(*refs):
        buf = refs[N_PIECES:2 * N_PIECES]
        ssems = refs[2 * N_PIECES:3 * N_PIECES]
        rsems = refs[3 * N_PIECES:4 * N_PIECES]
        token = refs[4 * N_PIECES]
        x, y, c = _pos()
        for p in _SHARE_ORDER:
            for layer in range(DEPTH):
                view = _grad_half(buf[p], p, layer, c)
                _remote(view, view, ssems[p].at[layer], rsems[p].at[layer], (x, y, 1 - c)).start()
        token[...] = jnp.zeros(token.shape, token.dtype)

    sems = [pltpu.SemaphoreType.DMA((DEPTH,))] * N_PIECES
    res = pl.pallas_call(
        body, name="share_start",
        in_specs=[HBM_SPEC] * N_PIECES,
        out_specs=[HBM_SPEC] * N_PIECES + [SEM_SPEC] * (2 * N_PIECES) + [pl.BlockSpec(memory_space=pltpu.VMEM)],
        out_shape=[jax.ShapeDtypeStruct((DEPTH, p[2], p[3]), F32) for p in _PIECES] + sems + sems
        + [jax.ShapeDtypeStruct((SUBLANES, LANES), F32)],
        input_output_aliases={t: t for t in range(N_PIECES)},
        compiler_params=_params(has_side_effects=_SPLIT_EFFECT),
    )(*[pltpu.with_memory_space_constraint(a, pltpu.HBM) for a in reduced])
    return (list(res[:N_PIECES]), list(res[N_PIECES:2 * N_PIECES]), list(res[2 * N_PIECES:3 * N_PIECES]),
            res[3 * N_PIECES])


def _share_wait(p, buf, ssem, rsem, after):
    def body(buf_in, ssem_ref, rsem_ref, after_ref, buf_ref):
        x, y, c = _pos()
        for layer in range(DEPTH):
            cp = _remote(_grad_half(buf_ref, p, layer, c), _grad_half(buf_ref, p, layer, 1 - c),
                         ssem_ref.at[layer], rsem_ref.at[layer], (x, y, 1 - c))
            cp.wait_send()
            cp.wait_recv()

    return pl.pallas_call(
        body, name="share_wait_" + _PIECES[p][0],
        in_specs=[HBM_SPEC, SEM_SPEC, SEM_SPEC, pl.BlockSpec(memory_space=pl.ANY)], out_specs=HBM_SPEC,
        out_shape=jax.ShapeDtypeStruct(buf.shape, buf.dtype), input_output_aliases={0: 0},
        compiler_params=_params(has_side_effects=_SPLIT_EFFECT),
    )(buf, ssem, rsem, after)


N_DEV = 8


def _place_slot(v, me_arr, take_block):
    rows = v.shape[0] // N_DEV if take_block else v.shape[0]
    tm = _tile(rows, 512)
    steps = rows // tm

    def body(s_ref, v_ref, out_ref):
        out_ref[...] = v_ref[...]

    return pl.pallas_call(
        body, name="place_small_block" if take_block else "place_small_sum",
        grid_spec=pltpu.PrefetchScalarGridSpec(
            num_scalar_prefetch=1, grid=(steps,),
            in_specs=[pl.BlockSpec((tm, LANES), lambda i, s: (s[0] * steps * take_block + i, 0))],
            out_specs=pl.BlockSpec((None, tm, LANES), lambda i, s: (s[0], i, 0))),
        out_shape=jax.ShapeDtypeStruct((N_DEV, rows, LANES), F32),
        compiler_params=_params(("parallel",)),
    )(me_arr, v)


def _all_peers():
    x, y, c = _pos()
    flip = lambda v, f: 1 - v if f else v
    return (x, y, c), [(flip(x, a), flip(y, b), flip(c, d))
                       for a in (0, 1) for b in (0, 1) for d in (0, 1) if a or b or d]


def _slot_index(dev):
    return 4 * dev[0] + 2 * dev[1] + dev[2]


def _exchange_start(g, src, name):
    rows = g.shape[1]
    n_in = 1 if src is None else 2

    def body(*refs):
        g_ref = refs[n_in]
        src_ref = refs[n_in + 1] if src is not None else None
        ssem, rsem, token = refs[2 * n_in:]
        me, peers = _all_peers()
        for k, dev in enumerate(peers):
            if src is None:
                mine = g_ref.at[_slot_index(me)]
            else:
                mine = src_ref.at[pl.ds(pl.multiple_of(_slot_index(dev) * rows, SUBLANES), rows), :]
            _remote(mine, g_ref.at[_slot_index(me)], ssem.at[k], rsem.at[k], dev).start()
        token[...] = jnp.zeros(token.shape, token.dtype)

    sem = pltpu.SemaphoreType.DMA((N_DEV - 1,))
    args = [g] if src is None else [g, src]
    res = pl.pallas_call(
        body, name=name,
        in_specs=[HBM_SPEC] * n_in,
        out_specs=[HBM_SPEC] * n_in + [SEM_SPEC, SEM_SPEC, pl.BlockSpec(memory_space=pltpu.VMEM)],
        out_shape=[jax.ShapeDtypeStruct(a.shape, a.dtype) for a in args]
        + [sem, sem, jax.ShapeDtypeStruct((SUBLANES, LANES), F32)],
        input_output_aliases={t: t for t in range(n_in)},
        compiler_params=_params(has_side_effects=_SPLIT_EFFECT),
    )(*[pltpu.with_memory_space_constraint(a, pltpu.HBM) for a in args])
    return list(res[:n_in]), res[n_in], res[n_in + 1], res[n_in + 2]


def _exchange_wait(bufs, ssem, rsem, after, name):
    n_in = len(bufs)

    def body(*refs):
        ssem_ref, rsem_ref = refs[n_in], refs[n_in + 1]
        g_ref = refs[n_in + 3]
        me, peers = _all_peers()
        for k, dev in enumerate(peers):
            cp = _remote(g_ref.at[_slot_index(me)], g_ref.at[_slot_index(dev)], ssem_ref.at[k], rsem_ref.at[k], dev)
            cp.wait_send()
            cp.wait_recv()

    res = pl.pallas_call(
        body, name=name,
        in_specs=[HBM_SPEC] * n_in + [SEM_SPEC, SEM_SPEC, pl.BlockSpec(memory_space=pl.ANY)],
        out_specs=[HBM_SPEC] * n_in,
        out_shape=[jax.ShapeDtypeStruct(a.shape, a.dtype) for a in bufs],
        input_output_aliases={t: t for t in range(n_in)},
        compiler_params=_params(has_side_effects=_SPLIT_EFFECT),
    )(*bufs, ssem, rsem, after)
    return res[0]


def _sum_slots(g, name):
    n, rows, _ = g.shape
    tm = _tile(rows, 512)

    def body(g_ref, out_ref):
        acc = g_ref[0]
        for k in range(1, n):
            acc = acc + g_ref[k]
        out_ref[...] = acc

    return pl.pallas_call(
        body, name=name, grid=(rows // tm,),
        in_specs=[pl.BlockSpec((n, tm, LANES), lambda i: (0, i, 0))],
        out_specs=pl.BlockSpec((tm, LANES), lambda i: (i, 0)),
        out_shape=jax.ShapeDtypeStruct((rows, LANES), F32),
        compiler_params=_params(("parallel",)),
    )(g)


_TINY = ("ln_mix_g", "ret_log_gamma", "ssm_a_re", "ssm_a_im", "ssm_log_dt", "ssm_d", "b_glu", "ln_ffn_g", "ln_final_g")
_MID = ("ssm_b_re", "ssm_b_im", "ssm_c_re", "ssm_c_im")
_SMALL = _TINY + _MID
_FLAT_ALIGN = LANES * LANES
_FLAT_ROWS = 1024


def _flat_rows(like, names):
    rows = sum((math.prod(like[n].shape) + (-math.prod(like[n].shape)) % _FLAT_ALIGN) // LANES for n in names)
    return rows + (-rows) % _FLAT_ROWS


def _flatten(d, names):
    parts = []
    for n in names:
        f = d[n].reshape(-1)
        parts.append(jnp.pad(f, (0, (-f.shape[0]) % _FLAT_ALIGN)))
    total = sum(p.shape[0] for p in parts)
    parts.append(jnp.zeros(((-total) % (_FLAT_ROWS * LANES),), F32))
    return jnp.concatenate(parts).reshape(-1, LANES)


def _unflatten(flat, like, names):
    out, row = {}, 0
    for n in names:
        size = math.prod(like[n].shape)
        rows = (size + (-size) % _FLAT_ALIGN) // LANES
        part = lax.optimization_barrier(flat[row:row + rows])
        out[n] = part.reshape(-1)[:size].reshape(like[n].shape)
        row += rows
    return out


_BIG = ("w_in", "w_glu", "w_out", "w_ffn_gate", "w_ffn_up", "w_ffn_down")
_WEIGHTS = ("ln_mix_g", "w_in", "ret_log_gamma", "ssm_a_re", "ssm_a_im", "ssm_log_dt", "ssm_b_re", "ssm_b_im",
            "ssm_c_re", "ssm_c_im", "ssm_d", "w_glu", "b_glu", "w_out", "ln_ffn_g", "w_ffn_gate", "w_ffn_up",
            "w_ffn_down", "ln_final_g")


def _rope_tables(seq):
    half = QK_DIM // 2
    inv = 1.0 / (ROPE_BASE ** (jnp.arange(half, dtype=F32) / half))
    ang = jnp.arange(seq, dtype=F32)[:, None] * inv[None, :]
    return jnp.cos(ang), jnp.sin(ang)


def _step(w, m, v, x, target):
    seq = x.shape[0]
    seg_len = float(seq // SEGMENTS)
    c_idx = lax.axis_index("c").astype(jnp.int32)
    chip_idx = (2 * lax.axis_index("x") + lax.axis_index("y")).astype(jnp.int32)
    c_arr = jnp.stack([c_idx])
    idx_arr = jnp.stack([c_idx, chip_idx])

    chip_arr = jnp.stack([chip_idx])
    placed = {}

    def cast(pieces, layer):
        for p in pieces:
            key = (_PIECES[p][1], layer)
            placed[key] = _cast_place(_PIECES[p], w[_PIECES[p][0]], layer, placed.get(key), chip_arr,
                                      "cast_%s_l%d" % (_PIECES[p][0], layer))

    for layer, pieces in _GATHER_GROUPS:
        cast(pieces, layer)
    flying, ssems, rsems, token = _gather_start(list(range(len(_GATHER_GROUPS))), placed)
    wf = {b[0]: [None] * DEPTH for b in _BUFFERS}

    handing = {}

    def arrive(g, after):
        ks = _group_keys(g)
        landed = _gather_wait(g, [flying[k] for k in ks], ssems[g], rsems[g], after)
        for k, a in zip(ks, _gather_forward(g, landed)):
            wf[k[0]][k[1]] = a

    def hand_over(g, after):
        ks = _group_keys(g)
        landed = _gather_wait(g, [flying[k] for k in ks], ssems[g], rsems[g], after)
        bufs, fs, fr, tok = _forward_start(g, landed)
        handing[g] = (bufs, fs, fr)
        return tok[0:1, 0:1]

    def complete(g, after):
        for k, a in zip(_group_keys(g), _forward_wait(g, *handing[g], after)):
            wf[k[0]][k[1]] = a

    cos, sin = _rope_tables(seq)

    started = token[0, 0]
    s5_ops, s5_vjps = [], []
    for i in range(DEPTH):
        s5_raw = (w["ssm_a_re"][i] + started, w["ssm_a_im"][i], w["ssm_log_dt"][i], w["ssm_b_re"][i], w["ssm_b_im"][i])
        disc, disc_vjp = jax.vjp(functools.partial(_s5_discretize, seg_len=seg_len), *s5_raw)
        bblk, cblk, lam = _s5_pack(*disc, w["ssm_c_re"][i] + started, w["ssm_c_im"][i] + started)
        s5_ops.append((bblk.astype(BF16), cblk.astype(BF16), lam))
        s5_vjps.append(disc_vjp)
    tiny_flat = [_flatten({**d, "ln_final_g": d["ln_final_g"] + started}, _TINY) for d in (w, m, v)]
    corner = lambda a: a[(0,) * (a.ndim - 2)][0:1, 0:1].astype(F32)
    prepared = sum(corner(a) for ops in s5_ops for a in ops) + sum(corner(a) for a in tiny_flat) + corner(cos) + corner(sin)

    saved = []
    xc = x + token[0, 0]
    for i in range(DEPTH):
        t = "_l%d" % i
        s = {"x_in": xc}
        if i == 0:
            s["h"] = _rms_fwd(xc, w["ln_mix_g"][i:i + 1], "rms_mix" + t)
            arrive(0, prepared + corner(s["h"]))
        else:
            s["h"] = _rms_fwd(xc, w["ln_mix_g"][i:i + 1] + next_in, "rms_mix" + t)
        s["proj"] = _matmul(s["h"], wf["in"][i], "nn", [F32], name="mm_in" + t)[0]
        s["qr"], s["kr"] = _rot_fwd(s["proj"], cos, sin, "rot" + t)
        s["lg"] = jnp.broadcast_to(w["ret_log_gamma"][i].T[:, :, None], (HEADS, 2, LANES))
        s["y"] = _ret_fwd(s["qr"], s["kr"], s["proj"], s["lg"], "ret" + t)
        s["s5"], s["disc_vjp"] = s5_ops[i], s5_vjps[i]
        s["s5y"] = _s5_fwd(s["proj"], *s["s5"], "s5" + t)
        first = GROUPS_PER_LAYER * i
        d_skip = w["ssm_d"][i:i + 1] + hand_over(first + 1, s["s5y"])
        s["ret"], s["ysg"], s["ysgb"] = _post1_fwd(s["y"], s["proj"], s["s5y"], d_skip, "post" + t)
        complete(first + 1, s["ysgb"])
        s["z"] = _matmul(s["ysgb"], wf["glu"][i], "nn", [F32], name="mm_glu" + t)[0]
        b_glu = w["b_glu"][i:i + 1] + hand_over(first + 2, s["z"])
        s["merged"] = _merge_fwd(s["z"], s["ysg"], s["proj"], s["ret"], b_glu, "merge" + t)
        s["x1"] = _matmul(s["merged"], wf["out"][i], "nn", [F32], add=xc, name="mm_out" + t)[0]
        s["h2"] = _rms_fwd(s["x1"], w["ln_ffn_g"][i:i + 1], "rms_ffn" + t)
        complete(first + 2, s["h2"])
        s["ab"] = _matmul(s["h2"], wf["gu"][i], "nn", [F32], name="mm_gu" + t)[0]
        hand_over(first + 3, s["ab"])
        if i + 1 < DEPTH:
            next_in = hand_over(first + GROUPS_PER_LAYER, s["ab"])
        s["f"] = _glu_fwd(s["ab"], "glu" + t)
        complete(first + 3, s["f"])
        xc = _matmul(s["f"], wf["down"][i], "nn", [F32], add=s["x1"], name="mm_down" + t)[0]
        if i + 1 < DEPTH:
            complete(first + GROUPS_PER_LAYER, xc)
        saved.append(s)

    dx, dxb, loss_row, dg_final = _loss_stage(xc, target, w["ln_final_g"][None, :], "loss")
    loss = lax.psum(loss_row[0, 0], ("x", "y", "c"))

    g_small = {"ln_final_g": dg_final[0]}
    per_layer = {n: [None] * DEPTH for n in _SMALL if n != "ln_final_g"}
    dws, got, swaps, flights = {}, {}, {}, []

    def dw_mm(a, b, buf, i, name):
        dws[(buf, i)] = _matmul(a, b, "tn", [F32, BF16], name=name)

    def depart(g):
        keys = _reduce_keys(_REDUCE_GROUPS[g])
        own, land, ssem, rsem, tok = _swap_start(g, {k: dws[k][1] for k in keys})
        swaps[g] = (own, land, ssem, rsem)
        return tok[0:1, 0:1]

    def proceed(g, after):
        group = _REDUCE_GROUPS[g]
        got.update(_swap_wait(g, *swaps[g], after))
        partials = {}
        for p, layer in group:
            key = (_PIECES[p][1], layer)
            partials[key] = _chip_partial(_PIECES[p], dws[key][0], got[key], partials.get(key), c_arr,
                                          "chip_partial_%s_l%d" % (_PIECES[p][0], layer))
        pt, land, ssem, rsem, tok = _scatter_start(g, partials)
        flights.append((g, pt, land, ssem, rsem))
        return tok[0:1, 0:1]

    for i in reversed(range(DEPTH)):
        t = "_l%d" % i
        s = saved[i]
        g_ffn, g_mix, d_skip = w["ln_ffn_g"][i:i + 1], w["ln_mix_g"][i:i + 1], w["ssm_d"][i:i + 1]
        dw_mm(s["f"], dxb, "down", i, "dw_down" + t)
        df = _matmul(dxb, wf["down"][i], "nt", [F32], name="dx_down" + t)[0]
        if i == 0:
            g_ffn = g_ffn + proceed(0, df)
        dab = _glu_bwd(s["ab"], df, "glu_bwd" + t)
        dw_mm(s["h2"], dab, "gu", i, "dw_gu" + t)
        if i == 0:
            g_ffn = g_ffn + depart(1)
        dh2 = _matmul(dab, wf["gu"][i], "nt", [F32], name="dx_gu" + t)[0]
        if i == 0:
            g_ffn = g_ffn + proceed(1, dh2)
        dx1, dx1b, dg = _rms_bwd(s["x1"], dh2, dx, g_ffn, "rms_ffn_bwd" + t)
        per_layer["ln_ffn_g"][i] = dg[0]

        dw_mm(s["merged"], dx1b, "out", i, "dw_out" + t)
        dmerged = _matmul(dx1b, wf["out"][i], "nt", [F32], name="dx_out" + t)[0]
        dz, dys_part, dgs, db = _merge_bwd(s["z"], s["ysg"], s["proj"], s["ret"], dmerged, w["b_glu"][i:i + 1],
                                           "merge_bwd" + t)
        per_layer["b_glu"][i] = db[0]
        dw_mm(s["ysgb"], dz, "glu", i, "dw_glu" + t)
        if i == 0:
            d_skip = d_skip + depart(2)
        dys = _matmul(dz, wf["glu"][i], "nt", [F32], add=dys_part, name="dx_glu" + t)[0]
        if i == 0:
            d_skip = d_skip + proceed(2, dys)
        dy, dgg, dgr, ds5, du_part, dd = _post1_bwd(s["y"], s["proj"], s["s5y"], dmerged, dys,
                                                    d_skip, "post_bwd" + t)
        per_layer["ssm_d"][i] = dd[0]
        du, dbblk, dcblk, dlam = _s5_bwd(s["proj"], ds5, du_part, *s["s5"], "s5_bwd" + t)
        dlr, dli, dbr, dbi, dcr, dci = _s5_unpack(dbblk, dcblk, dlam)
        zeros = jnp.zeros_like(dlr)
        da_re, da_im, dlog_dt, db_re, db_im = s["disc_vjp"]((dlr, dli, zeros, zeros, dbr, dbi))
        for n, val in (("ssm_a_re", da_re), ("ssm_a_im", da_im), ("ssm_log_dt", dlog_dt), ("ssm_b_re", db_re),
                       ("ssm_b_im", db_im), ("ssm_c_re", dcr), ("ssm_c_im", dci)):
            per_layer[n][i] = val
        dqr, dkr, dv, dlg = _ret_bwd(s["qr"], s["kr"], s["proj"], dy, s["lg"], "ret_bwd" + t)
        per_layer["ret_log_gamma"][i] = dlg[:, :, 0].T
        dqkv = _rot_bwd(dqr, dkr, dv, cos, sin, "rot_bwd" + t)
        dproj = jnp.concatenate([dqkv, dgg, du, dgr, dgs], axis=1)
        dw_mm(s["h"], dproj, "in", i, "dw_in" + t)
        if i == 0:
            g_mix = g_mix + depart(3)
        dh = _matmul(dproj, wf["in"][i], "nt", [F32], name="dx_in" + t)[0]
        if i == 0:
            g_mix = g_mix + proceed(3, dh)
        dx, dxb, dg = _rms_bwd(s["x_in"], dh, dx1, g_mix, "rms_mix_bwd" + t)
        per_layer["ln_mix_g"][i] = dg[0]
        if i == DEPTH - 1:
            dxb = dxb + depart(0).astype(BF16)

    for n in per_layer:
        g_small[n] = jnp.stack(per_layer[n])
    me_arr = jnp.stack([2 * chip_idx + c_idx])
    g_mine = _flatten(g_small, _SMALL)
    rs_bufs, rs_ssem, rs_rsem, small_token = _exchange_start(_place_slot(g_mine, me_arr, True), g_mine,
                                                             "small_scatter_start")

    reduced = [None] * N_PIECES
    before = small_token
    for g, pt, land, ssem, rsem in flights:
        landed = _scatter_wait(g, pt, land, ssem, rsem, before)
        for (p, layer), buf in zip(_REDUCE_GROUPS[g], landed):
            key = (_PIECES[p][1], layer)
            reduced[p] = _reduce_half(_PIECES[p], layer, dws[key][0], got[key], buf, reduced[p], idx_arr,
                                      "reduce_%s_l%d" % (_PIECES[p][0], layer))
            before = reduced[p]
    shared, sh_ssem, sh_rsem, sh_token = _share_start(reduced)

    landed = _exchange_wait(rs_bufs, rs_ssem, rs_rsem, sh_token, "small_scatter_wait")
    ag_bufs, ag_ssem, ag_rsem, ag_token = _exchange_start(
        _place_slot(_sum_slots(landed, "sum_small"), me_arr, False), None, "small_gather_start")

    grads, delta, new_m, new_v = {}, {}, {}, {}
    previous = ag_token
    for p in _SHARE_ORDER:
        n = _PIECES[p][0]
        d, r, cc = w[n].shape
        two_d = lambda a: a.reshape(d * r, cc)
        g = _share_wait(p, shared[p], sh_ssem[p], sh_rsem[p], previous)
        dl, mn, vn = _adamw(two_d(w[n]), two_d(g), two_d(m[n]), two_d(v[n]), "adamw_" + n, after=ag_token)
        grads[n], delta[n], new_m[n], new_v[n] = g, dl.reshape(d, r, cc), mn.reshape(d, r, cc), vn.reshape(d, r, cc)
        previous = dl

    all_done = sum(corner(delta[n]) for n in _BIG)
    gathered = _exchange_wait(ag_bufs, ag_ssem, ag_rsem, all_done, "small_gather_wait")
    g_flat = gathered.reshape(-1, LANES)
    grads.update(_unflatten(g_flat, w, _SMALL))
    tiny_rows = _flat_rows(w, _TINY)
    dl, mn, vn = _adamw(tiny_flat[0], g_flat[:tiny_rows], tiny_flat[1], tiny_flat[2], "adamw_tiny")
    for dst, flat in ((delta, dl), (new_m, mn), (new_v, vn)):
        dst.update(_unflatten(flat, w, _TINY))
    for n in _MID:
        delta[n], new_m[n], new_v[n] = _adamw_nd(w[n], grads[n], m[n], v[n], "adamw_" + n)
    return loss, dx, grads, delta, new_m, new_v


def kernel(x, ln_mix_g, w_in, ret_log_gamma, ssm_a_re, ssm_a_im, ssm_log_dt, ssm_b_re, ssm_b_im, ssm_c_re, ssm_c_im, ssm_d, w_glu, b_glu, w_out, ln_ffn_g, w_ffn_gate, w_ffn_up, w_ffn_down, ln_final_g, loss_target, m_ln_mix_g, m_w_in, m_ret_log_gamma, m_ssm_a_re, m_ssm_a_im, m_ssm_log_dt, m_ssm_b_re, m_ssm_b_im, m_ssm_c_re, m_ssm_c_im, m_ssm_d, m_w_glu, m_b_glu, m_w_out, m_ln_ffn_g, m_w_ffn_gate, m_w_ffn_up, m_w_ffn_down, m_ln_final_g, v_ln_mix_g, v_w_in, v_ret_log_gamma, v_ssm_a_re, v_ssm_a_im, v_ssm_log_dt, v_ssm_b_re, v_ssm_b_im, v_ssm_c_re, v_ssm_c_im, v_ssm_d, v_w_glu, v_b_glu, v_w_out, v_ln_ffn_g, v_w_ffn_gate, v_w_ffn_up, v_w_ffn_down, v_ln_final_g):
    given = dict(locals())
    w = {n: given[n] for n in _WEIGHTS}
    m = {n: given["m_" + n] for n in _WEIGHTS}
    v = {n: given["v_" + n] for n in _WEIGHTS}
    loss, dx, grads, delta, new_m, new_v = _step(w, m, v, x[0], loss_target[0])
    return (loss, dx[None], *[grads[n] for n in _WEIGHTS], *[delta[n] for n in _WEIGHTS],
            *[new_m[n] for n in _WEIGHTS], *[new_v[n] for n in _WEIGHTS])
```

```python
import functools
import math

import jax
import jax.numpy as jnp
from jax import lax
from jax.experimental import pallas as pl
from jax.experimental.pallas import tpu as pltpu

F32 = jnp.float32
BF16 = jnp.bfloat16

D_MODEL = 2048
DEPTH = 2
HEADS = 4
QK_DIM = 256
V_DIM = 512
QK_WIDTH = HEADS * QK_DIM
ROPE_BASE = 10000.0
GROUP = 16
N_GROUPS = D_MODEL // GROUP
N_STATE = 64
D_FF = 5632
IN_WIDTH = 2 * QK_WIDTH + 5 * D_MODEL
EPS = 1e-6
N_CHIPS = 4

ADAM_LR = 0.001
ADAM_B1 = 0.9
ADAM_B2 = 0.999
ADAM_EPS = 1e-08
ADAM_WD = 0.01
ADAM_STEP = 10

LANES = 128
SUBLANES = 8
VMEM_LIMIT = 56 * 1024 * 1024
SEGMENTS = SUBLANES
GROUPS_PER_TILE = LANES // GROUP
STATE_COLS = GROUPS_PER_TILE * N_STATE
N_TILES = D_MODEL // LANES
SCAN_UNROLL = 8

MESH = pl.DeviceIdType.MESH
HBM_SPEC = pl.BlockSpec(memory_space=pltpu.HBM)


def _params(sem=None, **kw):
    return pltpu.CompilerParams(dimension_semantics=sem, vmem_limit_bytes=VMEM_LIMIT, **kw)


def _tile(n, cap=1024):
    for t in (2048, 1024, 512, 256, 128, 64):
        if t <= cap and n % t == 0:
            return t
    raise ValueError(n)


def _rows_call(fn, rows, pars, row_outs, par_outs, *, tm, name):
    m = rows[0][0].shape[0]
    nr, npar, nro, npo = len(rows), len(pars), len(row_outs), len(par_outs)

    def body(*refs):
        rin = refs[:nr]
        pin = refs[nr:nr + npar]
        rout = refs[nr + npar:nr + npar + nro]
        pout = refs[nr + npar + nro:]
        res = fn(*[r[...] for r in rin], *[p[...] for p in pin])
        if not isinstance(res, (tuple, list)):
            res = (res,)
        for r, v in zip(rout, res[:nro]):
            r[...] = v.astype(r.dtype)
        if npo:
            @pl.when(pl.program_id(0) == 0)
            def _():
                for p in pout:
                    p[...] = jnp.zeros(p.shape, p.dtype)
            for p, v in zip(pout, res[nro:]):
                p[...] += v

    in_specs = [pl.BlockSpec((tm, w), functools.partial(lambda cb, i: (i, cb), cb)) for (_, w, cb) in rows]
    in_specs += [pl.BlockSpec(p.shape, lambda i: (0, 0)) for p in pars]
    out_specs = [pl.BlockSpec((tm, w), lambda i: (i, 0)) for (w, _) in row_outs]
    out_specs += [pl.BlockSpec(s, lambda i: (0, 0)) for s in par_outs]
    out_shape = [jax.ShapeDtypeStruct((m, w), dt) for (w, dt) in row_outs]
    out_shape += [jax.ShapeDtypeStruct(s, F32) for s in par_outs]
    res = pl.pallas_call(
        body, name=name, grid=(m // tm,), in_specs=in_specs, out_specs=out_specs, out_shape=out_shape,
        compiler_params=_params(("arbitrary",) if npo else ("parallel",)),
    )(*[a for (a, _, _) in rows], *pars)
    return res


def _f32(*vals):
    return [v.astype(F32) for v in vals]


def _f_rms(x, g):
    r = lax.rsqrt(jnp.mean(x * x, axis=-1, keepdims=True) + EPS)
    return x * r * g


def _rms_fwd(x, g, name):
    return _rows_call(lambda xv, gv: _f_rms(xv, gv), [(x, D_MODEL, 0)], [g], [(D_MODEL, BF16)], [],
                      tm=256, name=name)[0]


def _rms_bwd(x, dh, dres, g, name):
    def fn(xv, dhv, drv, gv):
        _, vjp = jax.vjp(_f_rms, xv, gv)
        dx, dg = vjp(dhv)
        dx = dx + drv
        return dx, dx, dg
    return _rows_call(fn, [(x, D_MODEL, 0), (dh, D_MODEL, 0), (dres, D_MODEL, 0)], [g],
                      [(D_MODEL, F32), (D_MODEL, BF16)], [(1, D_MODEL)], tm=256, name=name)


def _rot_heads(xv, cos, sin, scale):
    half = QK_DIM // 2
    outs = []
    for h in range(HEADS):
        x1 = xv[:, h * QK_DIM:h * QK_DIM + half]
        x2 = xv[:, h * QK_DIM + half:(h + 1) * QK_DIM]
        outs += [(x1 * cos - x2 * sin) * scale, (x1 * sin + x2 * cos) * scale]
    return jnp.concatenate(outs, axis=1)


def _rot_fwd(proj, cos, sin, name):
    def fn(q, k, cv, sv):
        return _rot_heads(q, cv, sv, 1.0), _rot_heads(k, cv, sv, QK_DIM ** -0.5)
    return _rows_call(fn, [(proj, QK_WIDTH, 0), (proj, QK_WIDTH, 1), (cos, LANES, 0), (sin, LANES, 0)], [],
                      [(QK_WIDTH, BF16), (QK_WIDTH, BF16)], [], tm=256, name=name)


def _rot_bwd(dqr, dkr, dv, cos, sin, name):
    def fn(dq, dk, dvv, cv, sv):
        return jnp.concatenate([_rot_heads(dq, cv, -sv, 1.0), _rot_heads(dk, cv, -sv, QK_DIM ** -0.5), dvv], axis=1)
    return _rows_call(fn, [(dqr, QK_WIDTH, 0), (dkr, QK_WIDTH, 0), (dv, D_MODEL, 0), (cos, LANES, 0), (sin, LANES, 0)],
                      [], [(2 * QK_WIDTH + D_MODEL, BF16)], [], tm=256, name=name)[0]


def _f_post1(y0, y1, y2, y3, g, gr, s5, u, dsk):
    yn = [yh * lax.rsqrt(jnp.mean(yh * yh, axis=-1, keepdims=True) + EPS) for yh in (y0, y1, y2, y3)]
    ret = jax.nn.sigmoid(gr) * (jax.nn.silu(g) * jnp.concatenate(yn, axis=1))
    ysg = jax.nn.gelu(s5 + dsk * u)
    return ret, ysg


def _post1_rows(y, proj, s5y):
    rows = [(y, V_DIM, h) for h in range(HEADS)]
    rows += [(proj, D_MODEL, 2), (proj, D_MODEL, 4), (s5y, D_MODEL, 0), (proj, D_MODEL, 3)]
    return rows


def _post1_fwd(y, proj, s5y, dsk, name):
    def fn(*vals):
        ret, ysg = _f_post1(*vals)
        return ret, ysg, ysg
    return _rows_call(fn, _post1_rows(y, proj, s5y), [dsk],
                      [(D_MODEL, F32), (D_MODEL, F32), (D_MODEL, BF16)], [], tm=128, name=name)


def _post1_bwd(y, proj, s5y, dret, dys, dsk, name):
    def fn(*vals):
        prim = vals[:8] + (vals[10],)
        _, vjp = jax.vjp(_f_post1, *prim)
        gy0, gy1, gy2, gy3, gg, ggr, gs5, gu, gd = vjp((vals[8], vals[9]))
        return jnp.concatenate([gy0, gy1, gy2, gy3], axis=1), gg, ggr, gs5, gu, gd
    rows = _post1_rows(y, proj, s5y) + [(dret, D_MODEL, 0), (dys, D_MODEL, 0)]
    return _rows_call(fn, rows, [dsk],
                      [(D_MODEL, BF16), (D_MODEL, BF16), (D_MODEL, BF16), (D_MODEL, F32), (D_MODEL, F32)],
                      [(1, D_MODEL)], tm=128, name=name)


def _f_merge(z, ysg, gs, ret, b):
    return ret + jax.nn.sigmoid(gs) * (ysg * jax.nn.sigmoid(z + b))


def _merge_fwd(z, ysg, proj, ret, b, name):
    return _rows_call(_f_merge, [(z, D_MODEL, 0), (ysg, D_MODEL, 0), (proj, D_MODEL, 5), (ret, D_MODEL, 0)], [b],
                      [(D_MODEL, BF16)], [], tm=128, name=name)[0]


def _merge_bwd(z, ysg, proj, ret, dm, b, name):
    def fn(zv, yv, gv, rv, dmv, bv):
        _, vjp = jax.vjp(_f_merge, zv, yv, gv, rv, bv)
        gz, gy, gg, _, gb = vjp(dmv)
        return gz, gy, gg, gb
    rows = [(z, D_MODEL, 0), (ysg, D_MODEL, 0), (proj, D_MODEL, 5), (ret, D_MODEL, 0), (dm, D_MODEL, 0)]
    return _rows_call(fn, rows, [b], [(D_MODEL, BF16), (D_MODEL, F32), (D_MODEL, BF16)], [(1, D_MODEL)],
                      tm=128, name=name)


def _f_glu(a, b):
    return jax.nn.silu(a) * b


def _glu_fwd(ab, name):
    return _rows_call(_f_glu, [(ab, D_FF, 0), (ab, D_FF, 1)], [], [(D_FF, BF16)], [], tm=128, name=name)[0]


def _glu_bwd(ab, df, name):
    def fn(a, b, d):
        _, vjp = jax.vjp(_f_glu, a, b)
        ga, gb = vjp(d)
        return jnp.concatenate([ga, gb], axis=1)
    return _rows_call(fn, [(ab, D_FF, 0), (ab, D_FF, 1), (df, D_FF, 0)], [], [(2 * D_FF, BF16)], [],
                      tm=128, name=name)[0]


def _loss_stage(x, tgt, g, name):
    def fn(xv, tv, gv):
        def lf(xx, gg):
            err = _f_rms(xx, gg) - tv
            row = jnp.mean(err * err, axis=-1, keepdims=True)
            return 0.5 * jnp.sum(row, axis=0, keepdims=True)
        l, vjp = jax.vjp(lf, xv, gv)
        dx, dg = vjp(jnp.ones((1, 1), F32))
        return dx, dx, jnp.broadcast_to(l, (1, LANES)), dg
    return _rows_call(fn, [(x, D_MODEL, 0), (tgt, D_MODEL, 0)], [g], [(D_MODEL, F32), (D_MODEL, BF16)],
                      [(1, LANES), (1, D_MODEL)], tm=256, name=name)


def _adam_math(wv, gv, mv, vv):
    mn = ADAM_B1 * mv + (1.0 - ADAM_B1) * gv
    vn = ADAM_B2 * vv + (1.0 - ADAM_B2) * (gv * gv)
    m_hat = mn / (1.0 - ADAM_B1 ** ADAM_STEP)
    v_hat = vn / (1.0 - ADAM_B2 ** ADAM_STEP)
    delta = -ADAM_LR * (m_hat / (jnp.sqrt(v_hat) + ADAM_EPS) + ADAM_WD * wv)
    return delta, mn, vn


def _adamw(w, g, m, v, name, after=None):
    rows, cols = w.shape
    tm = _tile(rows, 128 if cols > D_FF // N_CHIPS else (256 if cols > LANES else 512))
    fn = _adam_math if after is None else (lambda wv, gv, mv, vv, _: _adam_math(wv, gv, mv, vv))
    return _rows_call(fn, [(w, cols, 0), (g, cols, 0), (m, cols, 0), (v, cols, 0)], [] if after is None else [after],
                      [(cols, F32)] * 3, [], tm=tm, name=name)


def _adamw_nd(w, g, m, v, name):
    shape = w.shape
    lead = math.prod(shape[:-2])
    blk = (lead // 8,) + shape[-2:]
    three_d = lambda a: a.reshape((lead,) + shape[-2:])

    def body(w_ref, g_ref, m_ref, v_ref, d_ref, mn_ref, vn_ref):
        d_ref[...], mn_ref[...], vn_ref[...] = _adam_math(w_ref[...], g_ref[...], m_ref[...], v_ref[...])

    spec = pl.BlockSpec(blk, lambda i: (i, 0, 0))
    res = pl.pallas_call(
        body, name=name, grid=(8,), in_specs=[spec] * 4, out_specs=[spec] * 3,
        out_shape=[jax.ShapeDtypeStruct((lead,) + shape[-2:], F32)] * 3,
        compiler_params=_params(("parallel",)),
    )(three_d(w), three_d(g), three_d(m), three_d(v))
    return [r.reshape(shape) for r in res]


MATMUL_VMEM_BUDGET = 44 * 1024 * 1024


def _matmul_tiles(m, n, k, out_bytes, has_add):
    if k > 2048:
        return _tile(m, 1024), _tile(n, 1024), _tile(k, 1024)
    tm, tn, tk = _tile(m, 2048), _tile(n, 1024), k

    def footprint():
        acc = 4 * tm * tn if k // tk > 1 else 0
        return 2 * 2 * (tm * tk + tk * tn) + 2 * (out_bytes + 4 * has_add) * tm * tn + acc

    while footprint() > MATMUL_VMEM_BUDGET:
        if tn > 512 and n % (tn // 2) == 0:
            tn //= 2
        elif tk > 512 and k % (tk // 2) == 0:
            tk //= 2
        else:
            tm //= 2
    return tm, tn, tk


def _matmul(a, b, mode, out_dtypes, *, name, add=None):
    if mode == "nn":
        (m, k), (_, n) = a.shape, b.shape
    elif mode == "nt":
        (m, k), (n, _) = a.shape, b.shape
    else:
        (k, m), (_, n) = a.shape, b.shape
    n_out = len(out_dtypes)
    has_add = add is not None
    tm, tn, tk = _matmul_tiles(m, n, k, sum(jnp.dtype(dt).itemsize for dt in out_dtypes), has_add)
    nk = k // tk
    if mode == "nn":
        a_spec = pl.BlockSpec((tm, tk), lambda i, j, kk: (i, kk))
        b_spec = pl.BlockSpec((tk, tn), lambda i, j, kk: (kk, j))
        dims = (((1,), (0,)), ((), ()))
    elif mode == "nt":
        a_spec = pl.BlockSpec((tm, tk), lambda i, j, kk: (i, kk))
        b_spec = pl.BlockSpec((tn, tk), lambda i, j, kk: (j, kk))
        dims = (((1,), (1,)), ((), ()))
    else:
        a_spec = pl.BlockSpec((tk, tm), lambda i, j, kk: (kk, i))
        b_spec = pl.BlockSpec((tk, tn), lambda i, j, kk: (kk, j))
        dims = (((0,), (0,)), ((), ()))

    def body(*refs):
        a_ref, b_ref = refs[0], refs[1]
        add_ref = refs[2] if has_add else None
        outs = refs[2 + has_add:2 + has_add + n_out]

        def finish(r):
            if has_add:
                r = r + add_ref[...]
            for o in outs:
                o[...] = r.astype(o.dtype)

        if nk == 1:
            finish(lax.dot_general(a_ref[...], b_ref[...], dims, preferred_element_type=F32))
            return
        acc = refs[-1]
        kk = pl.program_id(2)

        @pl.when(kk == 0)
        def _():
            acc[...] = jnp.zeros(acc.shape, F32)

        acc[...] += lax.dot_general(a_ref[...], b_ref[...], dims, preferred_element_type=F32)

        @pl.when(kk == nk - 1)
        def _():
            finish(acc[...])

    in_specs = [a_spec, b_spec]
    args = [a, b]
    if has_add:
        in_specs.append(pl.BlockSpec((tm, tn), lambda i, j, kk: (i, j)))
        args.append(add)
    return pl.pallas_call(
        body, name=name, grid=(m // tm, n // tn, nk), in_specs=in_specs,
        out_specs=[pl.BlockSpec((tm, tn), lambda i, j, kk: (i, j))] * n_out,
        out_shape=[jax.ShapeDtypeStruct((m, n), dt) for dt in out_dtypes],
        scratch_shapes=[pltpu.VMEM((tm, tn), F32)] if nk > 1 else [],
        compiler_params=_params(("parallel", "parallel", "arbitrary")),
    )(*args)


RET_TQ = 512


def _decay(lg_ref, i, tq, seq):
    n_idx = i * tq + lax.broadcasted_iota(jnp.int32, (tq, seq), 0)
    m_idx = lax.broadcasted_iota(jnp.int32, (tq, seq), 1)
    diff = (n_idx - m_idx).astype(F32)
    lgf = lg_ref[0, 0:1, 0:1]
    lgb = lg_ref[0, 1:2, 0:1]
    causal = diff >= 0
    return jnp.exp(jnp.where(causal, lgf * diff, -lgb * diff)), diff, causal


_NT = (((1,), (1,)), ((), ()))
_TN = (((0,), (0,)), ((), ()))


def _ret_fwd(qr, kr, proj, lg, name):
    seq = qr.shape[0]
    tq = RET_TQ
    v_blk0 = (2 * QK_WIDTH) // V_DIM

    def body(q_ref, k_ref, v_ref, lg_ref, y_ref):
        i = pl.program_id(1)
        s = lax.dot_general(q_ref[...], k_ref[...], _NT, preferred_element_type=F32)
        dm, _, _ = _decay(lg_ref, i, tq, seq)
        p = (s * dm).astype(BF16)
        y_ref[...] = jnp.dot(p, v_ref[...].astype(BF16), preferred_element_type=F32)

    return pl.pallas_call(
        body, name=name, grid=(HEADS, seq // tq),
        in_specs=[pl.BlockSpec((tq, QK_DIM), lambda h, i: (i, h)),
                  pl.BlockSpec((seq, QK_DIM), lambda h, i: (0, h)),
                  pl.BlockSpec((seq, V_DIM), lambda h, i: (0, v_blk0 + h)),
                  pl.BlockSpec((1, 2, LANES), lambda h, i: (h, 0, 0))],
        out_specs=pl.BlockSpec((tq, V_DIM), lambda h, i: (i, h)),
        out_shape=jax.ShapeDtypeStruct((seq, HEADS * V_DIM), F32),
        compiler_params=_params(("parallel", "parallel")),
    )(qr, kr, proj, lg)


def _ret_bwd(qr, kr, proj, dy, lg, name):
    seq = qr.shape[0]
    tq = RET_TQ
    v_blk0 = (2 * QK_WIDTH) // V_DIM

    def body(q_ref, k_ref, v_ref, dy_ref, lg_ref, dq_ref, dk_ref, dv_ref, dlg_ref):
        i = pl.program_id(1)

        @pl.when(i == 0)
        def _():
            dk_ref[...] = jnp.zeros(dk_ref.shape, F32)
            dv_ref[...] = jnp.zeros(dv_ref.shape, F32)
            dlg_ref[...] = jnp.zeros(dlg_ref.shape, F32)

        q = q_ref[...]
        k = k_ref[...]
        vb = v_ref[...].astype(BF16)
        dyb = dy_ref[...]
        s = lax.dot_general(q, k, _NT, preferred_element_type=F32)
        dm, diff, causal = _decay(lg_ref, i, tq, seq)
        p = s * dm
        dp = lax.dot_general(dyb, vb, _NT, preferred_element_type=F32)
        dv_ref[...] += lax.dot_general(p.astype(BF16), dyb, _TN, preferred_element_type=F32)
        ds = (dp * dm).astype(BF16)
        dq_ref[...] = jnp.dot(ds, k, preferred_element_type=F32)
        dk_ref[...] += lax.dot_general(ds, q, _TN, preferred_element_type=F32)
        gd = dp * p * diff
        dlf = jnp.sum(jnp.sum(jnp.where(causal, gd, 0.0), axis=1, keepdims=True), axis=0, keepdims=True)
        dlb = jnp.sum(jnp.sum(jnp.where(causal, 0.0, -gd), axis=1, keepdims=True), axis=0, keepdims=True)
        row = lax.broadcasted_iota(jnp.int32, (2, LANES), 0)
        dlg_ref[0] += jnp.where(row == 0, dlf, dlb)

    return pl.pallas_call(
        body, name=name, grid=(HEADS, seq // tq),
        in_specs=[pl.BlockSpec((tq, QK_DIM), lambda h, i: (i, h)),
                  pl.BlockSpec((seq, QK_DIM), lambda h, i: (0, h)),
                  pl.BlockSpec((seq, V_DIM), lambda h, i: (0, v_blk0 + h)),
                  pl.BlockSpec((tq, V_DIM), lambda h, i: (i, h)),
                  pl.BlockSpec((1, 2, LANES), lambda h, i: (h, 0, 0))],
        out_specs=[pl.BlockSpec((tq, QK_DIM), lambda h, i: (i, h)),
                   pl.BlockSpec((seq, QK_DIM), lambda h, i: (0, h)),
                   pl.BlockSpec((seq, V_DIM), lambda h, i: (0, h)),
                   pl.BlockSpec((1, 2, LANES), lambda h, i: (h, 0, 0))],
        out_shape=[jax.ShapeDtypeStruct((seq, QK_WIDTH), F32), jax.ShapeDtypeStruct((seq, QK_WIDTH), F32),
                   jax.ShapeDtypeStruct((seq, HEADS * V_DIM), F32), jax.ShapeDtypeStruct((HEADS, 2, LANES), F32)],
        compiler_params=_params(("parallel", "arbitrary")),
    )(qr, kr, proj, dy, lg)


def _shift_rows(v, reverse):
    row = lax.broadcasted_iota(jnp.int32, v.shape, 0)
    if reverse:
        return jnp.where(row == SEGMENTS - 1, 0.0, pltpu.roll(v, SEGMENTS - 1, 0))
    return jnp.where(row == 0, 0.0, pltpu.roll(v, 1, 0))


def _slab(t):
    if isinstance(t, int):
        return pl.ds(t * SEGMENTS, SEGMENTS)
    return pl.ds(pl.multiple_of(t * SEGMENTS, SEGMENTS), SEGMENTS)


def _unrolled_loop(body, lo, hi, init):
    main = (hi - lo) // SCAN_UNROLL

    def unrolled(g, carry):
        for k in range(SCAN_UNROLL):
            carry = body(lo + g * SCAN_UNROLL + k, carry)
        return carry

    carry = lax.fori_loop(0, main, unrolled, init)
    for t in range(lo + main * SCAN_UNROLL, hi):
        carry = body(t, carry)
    return carry


def _scan(xr_ref, xi_ref, lam, reverse, conj):
    steps = xr_ref.shape[0] // SEGMENTS
    cols = xr_ref.shape[1]
    lr = jnp.broadcast_to(lam[0], (SEGMENTS, cols))
    li = jnp.broadcast_to(lam[1], (SEGMENTS, cols))
    lrt = jnp.broadcast_to(lam[2], (SEGMENTS, cols))
    lit = jnp.broadcast_to(lam[3], (SEGMENTS, cols))
    if conj:
        li, lit = -li, -lit
    zero = jnp.zeros((SEGMENTS, cols), F32)

    def rows_of(t):
        return _slab(steps - 1 - t if reverse else t)

    def advance(t, carry):
        sr, si = carry
        rows = rows_of(t)
        return lr * sr - li * si + xr_ref[rows, :], lr * si + li * sr + xi_ref[rows, :]

    def step(t, carry):
        nr, ni = advance(t, carry)
        rows = rows_of(t)
        xr_ref[rows, :] = nr
        xi_ref[rows, :] = ni
        return nr, ni

    def run(body, init):
        return _unrolled_loop(body, 0, steps, init)

    er, ei = run(advance, (zero, zero))
    cr, ci = zero, zero
    for _ in range(SEGMENTS - 1):
        tr = er + lrt * cr - lit * ci
        ti = ei + lrt * ci + lit * cr
        cr, ci = _shift_rows(tr, reverse), _shift_rows(ti, reverse)
    run(step, (cr, ci))


def _permute_in(dst_ref, src_ref):
    steps = src_ref.shape[0] // SEGMENTS
    for s in range(SEGMENTS):
        dst_ref[pl.ds(s, steps, stride=SEGMENTS), :] = src_ref[s * steps:(s + 1) * steps, :].astype(dst_ref.dtype)


def _unpermute(src_ref, s):
    steps = src_ref.shape[0] // SEGMENTS
    return src_ref[pl.ds(s, steps, stride=SEGMENTS), :]


def _s5_fwd(proj, bblk, cblk, lam, name):
    seq = proj.shape[0]
    u_blk0 = (2 * QK_WIDTH + 2 * D_MODEL) // LANES
    sc = STATE_COLS

    def body(u_ref, b_ref, c_ref, lam_ref, y_ref, up_ref, yp_ref, xr_ref, xi_ref):
        _permute_in(up_ref, u_ref)
        ub = up_ref[...].astype(BF16)
        for d in range(2):
            xr_ref[...] = jnp.dot(ub, b_ref[d, :, 0:sc], preferred_element_type=F32)
            xi_ref[...] = jnp.dot(ub, b_ref[d, :, sc:2 * sc], preferred_element_type=F32)
            lm = [lam_ref[d, r:r + 1, :] for r in range(4)]
            _scan(xr_ref, xi_ref, lm, reverse=(d == 1), conj=False)
            yd = (jnp.dot(xr_ref[...].astype(BF16), c_ref[d, 0:sc, :], preferred_element_type=F32)
                  + jnp.dot(xi_ref[...].astype(BF16), c_ref[d, sc:2 * sc, :], preferred_element_type=F32))
            if d == 0:
                yp_ref[...] = yd
            else:
                yp_ref[...] += yd
        steps = seq // SEGMENTS
        for s in range(SEGMENTS):
            y_ref[s * steps:(s + 1) * steps, :] = _unpermute(yp_ref, s)

    return pl.pallas_call(
        body, name=name, grid=(N_TILES,),
        in_specs=[pl.BlockSpec((seq, LANES), lambda j: (0, u_blk0 + j)),
                  pl.BlockSpec((2, None, LANES, 2 * sc), lambda j: (0, j, 0, 0)),
                  pl.BlockSpec((2, None, 2 * sc, LANES), lambda j: (0, j, 0, 0)),
                  pl.BlockSpec((2, None, 4, sc), lambda j: (0, j, 0, 0))],
        out_specs=pl.BlockSpec((seq, LANES), lambda j: (0, j)),
        out_shape=jax.ShapeDtypeStruct((seq, D_MODEL), F32),
        scratch_shapes=[pltpu.VMEM((seq, LANES), F32), pltpu.VMEM((seq, LANES), F32),
                        pltpu.VMEM((seq, sc), F32), pltpu.VMEM((seq, sc), F32)],
        compiler_params=_params(("parallel",)),
    )(proj, bblk, cblk, lam)


def _s5_bwd(proj, dy, du_part, bblk, cblk, lam, name):
    seq = proj.shape[0]
    u_blk0 = (2 * QK_WIDTH + 2 * D_MODEL) // LANES
    sc = STATE_COLS
    steps = seq // SEGMENTS

    def body(u_ref, dy_ref, dup_ref, b_ref, c_ref, lam_ref, du_ref, db_ref, dc_ref, dlam_ref,
             up_ref, dyp_ref, dua_ref, xr_ref, xi_ref, gr_ref, gi_ref):
        _permute_in(up_ref, u_ref)
        _permute_in(dyp_ref, dy_ref)
        ub = up_ref[...].astype(BF16)
        dyb = dyp_ref[...].astype(BF16)
        ubt = up_ref[...].T.astype(BF16)
        dybt = dyp_ref[...].T.astype(BF16)
        for d in range(2):
            reverse = d == 1
            xr_ref[...] = jnp.dot(ub, b_ref[d, :, 0:sc], preferred_element_type=F32)
            xi_ref[...] = jnp.dot(ub, b_ref[d, :, sc:2 * sc], preferred_element_type=F32)
            lm = [lam_ref[d, r:r + 1, :] for r in range(4)]
            _scan(xr_ref, xi_ref, lm, reverse=reverse, conj=False)
            xrb = xr_ref[...].astype(BF16)
            xib = xi_ref[...].astype(BF16)
            dc_ref[d, :, 0:sc] = jnp.dot(dybt, xrb, preferred_element_type=F32)
            dc_ref[d, :, sc:2 * sc] = jnp.dot(dybt, xib, preferred_element_type=F32)
            gr_ref[...] = lax.dot_general(dyb, c_ref[d, 0:sc, :], _NT, preferred_element_type=F32)
            gi_ref[...] = lax.dot_general(dyb, c_ref[d, sc:2 * sc, :], _NT, preferred_element_type=F32)
            _scan(gr_ref, gi_ref, lm, reverse=not reverse, conj=True)

            def acc_step(t, carry):
                ar, ai = carry
                prev = _slab(t + 1 if reverse else t - 1)
                pr = xr_ref[prev, :]
                pi = xi_ref[prev, :]
                zr = gr_ref[_slab(t), :]
                zi = gi_ref[_slab(t), :]
                return ar + zr * pr + zi * pi, ai + zi * pr - zr * pi

            zero = jnp.zeros((SEGMENTS, sc), F32)
            if reverse:
                ar, ai = _unrolled_loop(acc_step, 0, steps - 1, (zero, zero))
                edge = _slab(steps - 1)
                pr = _shift_rows(xr_ref[_slab(0), :], True)
                pi = _shift_rows(xi_ref[_slab(0), :], True)
            else:
                ar, ai = _unrolled_loop(acc_step, 1, steps, (zero, zero))
                edge = _slab(0)
                pr = _shift_rows(xr_ref[_slab(steps - 1), :], False)
                pi = _shift_rows(xi_ref[_slab(steps - 1), :], False)
            zr = gr_ref[edge, :]
            zi = gi_ref[edge, :]
            ar = ar + zr * pr + zi * pi
            ai = ai + zi * pr - zr * pi
            dlam_ref[d, 0:1, :] = jnp.sum(ar, axis=0, keepdims=True)
            dlam_ref[d, 1:2, :] = jnp.sum(ai, axis=0, keepdims=True)

            grb = gr_ref[...].astype(BF16)
            gib = gi_ref[...].astype(BF16)
            db_ref[d, :, 0:sc] = jnp.dot(ubt, grb, preferred_element_type=F32)
            db_ref[d, :, sc:2 * sc] = jnp.dot(ubt, gib, preferred_element_type=F32)
            dud = (lax.dot_general(grb, b_ref[d, :, 0:sc], _NT, preferred_element_type=F32)
                   + lax.dot_general(gib, b_ref[d, :, sc:2 * sc], _NT, preferred_element_type=F32))
            if d == 0:
                dua_ref[...] = dud
            else:
                dua_ref[...] += dud
        for s in range(SEGMENTS):
            rows = slice(s * steps, (s + 1) * steps)
            du_ref[rows, :] = (_unpermute(dua_ref, s) + dup_ref[rows, :]).astype(du_ref.dtype)

    return pl.pallas_call(
        body, name=name, grid=(N_TILES,),
        in_specs=[pl.BlockSpec((seq, LANES), lambda j: (0, u_blk0 + j)),
                  pl.BlockSpec((seq, LANES), lambda j: (0, j)),
                  pl.BlockSpec((seq, LANES), lambda j: (0, j)),
                  pl.BlockSpec((2, None, LANES, 2 * sc), lambda j: (0, j, 0, 0)),
                  pl.BlockSpec((2, None, 2 * sc, LANES), lambda j: (0, j, 0, 0)),
                  pl.BlockSpec((2, None, 4, sc), lambda j: (0, j, 0, 0))],
        out_specs=[pl.BlockSpec((seq, LANES), lambda j: (0, j)),
                   pl.BlockSpec((2, None, LANES, 2 * sc), lambda j: (0, j, 0, 0)),
                   pl.BlockSpec((2, None, LANES, 2 * sc), lambda j: (0, j, 0, 0)),
                   pl.BlockSpec((2, None, 2, sc), lambda j: (0, j, 0, 0))],
        out_shape=[jax.ShapeDtypeStruct((seq, D_MODEL), BF16),
                   jax.ShapeDtypeStruct((2, N_TILES, LANES, 2 * sc), F32),
                   jax.ShapeDtypeStruct((2, N_TILES, LANES, 2 * sc), F32),
                   jax.ShapeDtypeStruct((2, N_TILES, 2, sc), F32)],
        scratch_shapes=[pltpu.VMEM((seq, LANES), F32), pltpu.VMEM((seq, LANES), F32), pltpu.VMEM((seq, LANES), F32),
                        pltpu.VMEM((seq, sc), F32), pltpu.VMEM((seq, sc), F32),
                        pltpu.VMEM((seq, sc), F32), pltpu.VMEM((seq, sc), F32)],
        compiler_params=_params(("parallel",)),
    )(proj, dy, du_part, bblk, cblk, lam)


def _s5_discretize(a_re, a_im, log_dt, b_re, b_im, seg_len):
    dt = jnp.exp(log_dt)[..., None]
    e = jnp.exp(a_re * dt)
    lr, li = e * jnp.cos(a_im * dt), e * jnp.sin(a_im * dt)
    et = jnp.exp(a_re * dt * seg_len)
    lrt, lit = et * jnp.cos(a_im * dt * seg_len), et * jnp.sin(a_im * dt * seg_len)
    den = a_re * a_re + a_im * a_im
    qr = ((lr - 1.0) * a_re + li * a_im) / den
    qi = (li * a_re - (lr - 1.0) * a_im) / den
    br = qr[..., None] * b_re - qi[..., None] * b_im
    bi = qr[..., None] * b_im + qi[..., None] * b_re
    return lr, li, lrt, lit, br, bi


def _s5_pack(lr, li, lrt, lit, br, bi, c_re, c_im):
    eye = jnp.eye(GROUPS_PER_TILE, dtype=F32)

    def bd_b(b):
        b5 = b.reshape(2, N_TILES, GROUPS_PER_TILE, N_STATE, GROUP)
        return jnp.einsum("dtgph,gk->dtghkp", b5, eye).reshape(2, N_TILES, LANES, STATE_COLS)

    def bd_c(c):
        c5 = c.reshape(2, N_TILES, GROUPS_PER_TILE, GROUP, N_STATE)
        return jnp.einsum("dtghp,gk->dtkpgh", c5, eye).reshape(2, N_TILES, STATE_COLS, LANES)

    bblk = jnp.concatenate([bd_b(br), bd_b(bi)], axis=3)
    cblk = jnp.concatenate([bd_c(c_re), -bd_c(c_im)], axis=2)
    lam = jnp.stack([v.reshape(2, N_TILES, STATE_COLS) for v in (lr, li, lrt, lit)], axis=2)
    return bblk, cblk, lam


def _s5_unpack(dbblk, dcblk, dlam):
    eye = jnp.eye(GROUPS_PER_TILE, dtype=F32)

    def diag_b(d):
        d6 = d.reshape(2, N_TILES, GROUPS_PER_TILE, GROUP, GROUPS_PER_TILE, N_STATE)
        return jnp.einsum("dtghkp,gk->dtgph", d6, eye).reshape(2, N_GROUPS, N_STATE, GROUP)

    def diag_c(d):
        d6 = d.reshape(2, N_TILES, GROUPS_PER_TILE, GROUP, GROUPS_PER_TILE, N_STATE)
        return jnp.einsum("dtghkp,gk->dtghp", d6, eye).reshape(2, N_GROUPS, GROUP, N_STATE)

    dbr, dbi = diag_b(dbblk[..., :STATE_COLS]), diag_b(dbblk[..., STATE_COLS:])
    dcr, dci = diag_c(dcblk[..., :STATE_COLS]), -diag_c(dcblk[..., STATE_COLS:])
    dlr = dlam[:, :, 0, :].reshape(2, N_GROUPS, N_STATE)
    dli = dlam[:, :, 1, :].reshape(2, N_GROUPS, N_STATE)
    return dlr, dli, dbr, dbi, dcr, dci


def _pos():
    return lax.axis_index("x"), lax.axis_index("y"), lax.axis_index("c")


def _remote(src, dst, ssem, rsem, dev):
    return pltpu.make_async_remote_copy(src_ref=src, dst_ref=dst, send_sem=ssem, recv_sem=rsem,
                                        device_id=dev, device_id_type=MESH)


_PIECES = (
    ("w_in", "in", D_MODEL, IN_WIDTH // N_CHIPS, 0, IN_WIDTH // N_CHIPS, 0),
    ("w_glu", "glu", D_MODEL // N_CHIPS, D_MODEL, D_MODEL // N_CHIPS, 0, 0),
    ("w_out", "out", D_MODEL // N_CHIPS, D_MODEL, D_MODEL // N_CHIPS, 0, 0),
    ("w_ffn_gate", "gu", D_MODEL, D_FF // N_CHIPS, 0, D_FF // N_CHIPS, 0),
    ("w_ffn_up", "gu", D_MODEL, D_FF // N_CHIPS, 0, D_FF // N_CHIPS, D_FF),
    ("w_ffn_down", "down", D_FF // N_CHIPS, D_MODEL, D_FF // N_CHIPS, 0, 0),
)
_BUFFERS = (("in", D_MODEL, IN_WIDTH), ("glu", D_MODEL, D_MODEL), ("out", D_MODEL, D_MODEL),
            ("gu", D_MODEL, 2 * D_FF), ("down", D_FF, D_MODEL))
_BUF_INDEX = {name: t for t, (name, _, _) in enumerate(_BUFFERS)}
N_PIECES = len(_PIECES)
N_BUFFERS = len(_BUFFERS)


def _own_block(piece, tm):
    _, _, _, cs, rstep, cstep, coff = piece
    return lambda i, chip: (i + chip * (rstep // tm), coff // cs + chip * (cstep // cs))


def _cast_place(piece, w3, layer, prev, chip_arr, name):
    _, r, cc = w3.shape
    _, rf, cf = _BUFFERS[_BUF_INDEX[piece[1]]]
    tm = _tile(r, 256)
    own = _own_block(piece, tm)

    def body(s_ref, w_ref, *rest):
        rest[-1][...] = w_ref[...].astype(BF16)

    in_specs = [pl.BlockSpec((None, tm, cc), lambda i, s: (layer, i, 0))]
    args = [w3]
    aliases = {}
    if prev is not None:
        in_specs.append(pl.BlockSpec(memory_space=pl.ANY))
        args.append(prev)
        aliases = {2: 0}
    return pl.pallas_call(
        body, name=name,
        grid_spec=pltpu.PrefetchScalarGridSpec(
            num_scalar_prefetch=1, grid=(r // tm,), in_specs=in_specs,
            out_specs=pl.BlockSpec((tm, cc), lambda i, s: own(i, s[0]))),
        out_shape=jax.ShapeDtypeStruct((rf, cf), BF16), input_output_aliases=aliases,
        compiler_params=_params(("parallel",)),
    )(chip_arr, *args)


_GATHER_GROUPS = ((0, (0,)), (0, (1, 2)), (0, (3, 4)), (0, (5,)), (1, (0,)), (1, (1, 2)), (1, (3, 4)), (1, (5,)))
GROUPS_PER_LAYER = len(_GATHER_GROUPS) // DEPTH
_SPLIT_EFFECT = pltpu.SideEffectType.DATAFLOW_SIDE_EFFECTING
SEM_SPEC = pl.BlockSpec(memory_space=pltpu.SEMAPHORE)
BF16_ROWS = 2 * SUBLANES


def _group_keys(g):
    layer, pieces = _GATHER_GROUPS[g]
    keys = []
    for p in pieces:
        if (_PIECES[p][1], layer) not in keys:
            keys.append((_PIECES[p][1], layer))
    return keys


def _half_view(ref, piece, j, c):
    _, _, rs, cs, rstep, cstep, coff = piece
    half = rs // 2
    return ref.at[pl.ds(pl.multiple_of(j * rstep + c * half, BF16_ROWS), half), pl.ds(coff + j * cstep, cs)]


def _for_my_chip(fn):
    x, y, _ = _pos()
    for mine in range(N_CHIPS):
        pl.when(2 * x + y == mine)(functools.partial(fn, mine, [j for j in range(N_CHIPS) if j != mine]))


def _gather_start(groups, placed):
    keys = [k for g in groups for k in _group_keys(g)]
    nb, ng = len(keys), len(groups)

    def body(*refs):
        bufs = dict(zip(keys, refs[nb:2 * nb]))
        ssems = refs[2 * nb:2 * nb + ng]
        rsems = refs[2 * nb + ng:2 * nb + 2 * ng]
        token = refs[2 * nb + 2 * ng]
        _, _, c = _pos()

        def send(mine, others):
            for t, g in enumerate(groups):
                layer, pieces = _GATHER_GROUPS[g]
                for k, p in enumerate(pieces):
                    view = _half_view(bufs[(_PIECES[p][1], layer)], _PIECES[p], mine, c)
                    for j in others:
                        _remote(view, view, ssems[t].at[k * N_CHIPS + j], rsems[t].at[k * N_CHIPS + mine],
                                (j // 2, j % 2, c)).start()

        _for_my_chip(send)
        token[...] = jnp.zeros(token.shape, token.dtype)

    sems = [pltpu.SemaphoreType.DMA((N_CHIPS * len(_GATHER_GROUPS[g][1]),)) for g in groups]
    shapes = [jax.ShapeDtypeStruct(placed[k].shape, placed[k].dtype) for k in keys]
    res = pl.pallas_call(
        body, name="gather_start_g%d" % groups[0],
        in_specs=[HBM_SPEC] * nb,
        out_specs=[HBM_SPEC] * nb + [SEM_SPEC] * (2 * ng) + [pl.BlockSpec(memory_space=pltpu.VMEM)],
        out_shape=shapes + sems + sems + [jax.ShapeDtypeStruct((SUBLANES, LANES), F32)],
        input_output_aliases={t: t for t in range(nb)},
        compiler_params=_params(has_side_effects=_SPLIT_EFFECT),
    )(*[pltpu.with_memory_space_constraint(placed[k], pltpu.HBM) for k in keys])
    return (dict(zip(keys, res[:nb])), dict(zip(groups, res[nb:nb + ng])),
            dict(zip(groups, res[nb + ng:nb + 2 * ng])), res[nb + 2 * ng])


def _gather_wait(g, bufs, ssem, rsem, after):
    layer, pieces = _GATHER_GROUPS[g]
    keys = _group_keys(g)
    nb = len(keys)

    def body(*refs):
        ssem_ref, rsem_ref = refs[nb], refs[nb + 1]
        land = dict(zip(keys, refs[nb + 3:]))
        _, _, c = _pos()

        def wait(mine, others):
            for k, p in enumerate(pieces):
                ref = land[(_PIECES[p][1], layer)]
                for j in others:
                    cp = _remote(_half_view(ref, _PIECES[p], mine, c), _half_view(ref, _PIECES[p], j, c),
                                 ssem_ref.at[k * N_CHIPS + j], rsem_ref.at[k * N_CHIPS + j], (j // 2, j % 2, c))
                    cp.wait_send()
                    cp.wait_recv()

        _for_my_chip(wait)

    return pl.pallas_call(
        body, name="gather_wait_g%d" % g,
        in_specs=[HBM_SPEC] * nb + [SEM_SPEC, SEM_SPEC, pl.BlockSpec(memory_space=pl.ANY)],
        out_specs=[HBM_SPEC] * nb,
        out_shape=[jax.ShapeDtypeStruct(a.shape, a.dtype) for a in bufs],
        input_output_aliases={t: t for t in range(nb)},
        compiler_params=_params(has_side_effects=_SPLIT_EFFECT),
    )(*bufs, ssem, rsem, after)


def _gather_forward(g, bufs):
    layer, pieces = _GATHER_GROUPS[g]
    keys = _group_keys(g)
    nb = len(keys)

    def body(*refs):
        land = dict(zip(keys, refs[nb:2 * nb]))
        ssem, rsem = refs[2 * nb:]
        x, y, c = _pos()

        def forward(mine, others):
            cps = []
            for k, p in enumerate(pieces):
                ref = land[(_PIECES[p][1], layer)]
                for j in others:
                    view = _half_view(ref, _PIECES[p], j, c)
                    cp = _remote(view, view, ssem.at[k * N_CHIPS + j], rsem.at[k * N_CHIPS + j], (x, y, 1 - c))
                    cp.start()
                    cps.append(cp)
            for k, p in enumerate(pieces):
                ref = land[(_PIECES[p][1], layer)]
                for j in others:
                    view = _half_view(ref, _PIECES[p], j, 1 - c)
                    _remote(view, view, ssem.at[k * N_CHIPS + j], rsem.at[k * N_CHIPS + j], (x, y, 1 - c)).wait_recv()
            for cp in cps:
                cp.wait_send()

        _for_my_chip(forward)

    nsem = N_CHIPS * len(pieces)
    return pl.pallas_call(
        body, name="gather_forward_g%d" % g,
        in_specs=[HBM_SPEC] * nb, out_specs=[HBM_SPEC] * nb,
        out_shape=[jax.ShapeDtypeStruct(a.shape, a.dtype) for a in bufs],
        input_output_aliases={t: t for t in range(nb)},
        scratch_shapes=[pltpu.SemaphoreType.DMA((nsem,)), pltpu.SemaphoreType.DMA((nsem,))],
        compiler_params=_params(has_side_effects=True),
    )(*bufs)


def _forward_start(g, bufs):
    layer, pieces = _GATHER_GROUPS[g]
    keys = _group_keys(g)
    nb = len(keys)

    def body(*refs):
        land = dict(zip(keys, refs[nb:2 * nb]))
        ssem, rsem, token = refs[2 * nb:]
        x, y, c = _pos()

        def forward(mine, others):
            for k, p in enumerate(pieces):
                for j in others:
                    view = _half_view(land[(_PIECES[p][1], layer)], _PIECES[p], j, c)
                    _remote(view, view, ssem.at[k * N_CHIPS + j], rsem.at[k * N_CHIPS + j], (x, y, 1 - c)).start()

        _for_my_chip(forward)
        token[...] = jnp.zeros(token.shape, token.dtype)

    sem = pltpu.SemaphoreType.DMA((N_CHIPS * len(pieces),))
    res = pl.pallas_call(
        body, name="forward_start_g%d" % g,
        in_specs=[HBM_SPEC] * nb,
        out_specs=[HBM_SPEC] * nb + [SEM_SPEC, SEM_SPEC, pl.BlockSpec(memory_space=pltpu.VMEM)],
        out_shape=[jax.ShapeDtypeStruct(a.shape, a.dtype) for a in bufs]
        + [sem, sem, jax.ShapeDtypeStruct((SUBLANES, LANES), F32)],
        input_output_aliases={t: t for t in range(nb)},
        compiler_params=_params(has_side_effects=_SPLIT_EFFECT),
    )(*bufs)
    return list(res[:nb]), res[nb], res[nb + 1], res[nb + 2]


def _forward_wait(g, bufs, ssem, rsem, after):
    layer, pieces = _GATHER_GROUPS[g]
    keys = _group_keys(g)
    nb = len(keys)

    def body(*refs):
        ssem_ref, rsem_ref = refs[nb], refs[nb + 1]
        land = dict(zip(keys, refs[nb + 3:]))
        x, y, c = _pos()

        def wait(mine, others):
            for k, p in enumerate(pieces):
                ref = land[(_PIECES[p][1], layer)]
                for j in others:
                    cp = _remote(_half_view(ref, _PIECES[p], j, c), _half_view(ref, _PIECES[p], j, 1 - c),
                                 ssem_ref.at[k * N_CHIPS + j], rsem_ref.at[k * N_CHIPS + j], (x, y, 1 - c))
                    cp.wait_send()
                    cp.wait_recv()

        _for_my_chip(wait)

    return pl.pallas_call(
        body, name="forward_wait_g%d" % g,
        in_specs=[HBM_SPEC] * nb + [SEM_SPEC, SEM_SPEC, pl.BlockSpec(memory_space=pl.ANY)],
        out_specs=[HBM_SPEC] * nb,
        out_shape=[jax.ShapeDtypeStruct(a.shape, a.dtype) for a in bufs],
        input_output_aliases={t: t for t in range(nb)},
        compiler_params=_params(has_side_effects=_SPLIT_EFFECT),
    )(*bufs, ssem, rsem, after)


_REDUCE_GROUPS = (
    ((5, 1), (3, 1), (4, 1), (2, 1), (1, 1), (0, 1)),
    ((5, 0), (3, 0), (4, 0)),
    ((2, 0), (1, 0)),
    ((0, 0),),
)


def _reduce_keys(group):
    keys = []
    for p, layer in group:
        if (_PIECES[p][1], layer) not in keys:
            keys.append((_PIECES[p][1], layer))
    return keys


def _half_block(piece, tm):
    _, _, rs, cs, rstep, cstep, coff = piece
    return lambda i, j, c: (j * (rstep // tm) + c * (rs // 2 // tm) + i, coff // cs + j * (cstep // cs))


def _swap_start(g, dwb):
    group = _REDUCE_GROUPS[g]
    keys = _reduce_keys(group)
    nk = len(keys)

    def body(*refs):
        src = dict(zip(keys, refs[nk:2 * nk]))
        dst = dict(zip(keys, refs[2 * nk:3 * nk]))
        ssem, rsem, token = refs[3 * nk:]
        x, y, c = _pos()
        for k, (p, layer) in enumerate(group):
            key = (_PIECES[p][1], layer)
            for j in range(N_CHIPS):
                _remote(_half_view(src[key], _PIECES[p], j, 1 - c), _half_view(dst[key], _PIECES[p], j, 1 - c),
                        ssem.at[k * N_CHIPS + j], rsem.at[k * N_CHIPS + j], (x, y, 1 - c)).start()
        token[...] = jnp.zeros(token.shape, token.dtype)

    sem = pltpu.SemaphoreType.DMA((N_CHIPS * len(group),))
    shapes = [jax.ShapeDtypeStruct(dwb[k].shape, BF16) for k in keys]
    res = pl.pallas_call(
        body, name="swap_start_g%d" % g,
        in_specs=[HBM_SPEC] * nk,
        out_specs=[HBM_SPEC] * (2 * nk) + [SEM_SPEC, SEM_SPEC, pl.BlockSpec(memory_space=pltpu.VMEM)],
        out_shape=shapes + shapes + [sem, sem, jax.ShapeDtypeStruct((SUBLANES, LANES), F32)],
        input_output_aliases={t: t for t in range(nk)},
        compiler_params=_params(has_side_effects=_SPLIT_EFFECT),
    )(*[pltpu.with_memory_space_constraint(dwb[k], pltpu.HBM) for k in keys])
    return list(res[:nk]), list(res[nk:2 * nk]), res[2 * nk], res[2 * nk + 1], res[2 * nk + 2]


def _swap_wait(g, own, land, ssem, rsem, after):
    group = _REDUCE_GROUPS[g]
    keys = _reduce_keys(group)
    nk = len(keys)

    def body(*refs):
        ssem_ref, rsem_ref = refs[2 * nk], refs[2 * nk + 1]
        src = dict(zip(keys, refs[2 * nk + 3:3 * nk + 3]))
        dst = dict(zip(keys, refs[3 * nk + 3:]))
        x, y, c = _pos()
        for k, (p, layer) in enumerate(group):
            key = (_PIECES[p][1], layer)
            for j in range(N_CHIPS):
                cp = _remote(_half_view(src[key], _PIECES[p], j, 1 - c), _half_view(dst[key], _PIECES[p], j, c),
                             ssem_ref.at[k * N_CHIPS + j], rsem_ref.at[k * N_CHIPS + j], (x, y, 1 - c))
                cp.wait_send()
                cp.wait_recv()

    res = pl.pallas_call(
        body, name="swap_wait_g%d" % g,
        in_specs=[HBM_SPEC] * (2 * nk) + [SEM_SPEC, SEM_SPEC, pl.BlockSpec(memory_space=pl.ANY)],
        out_specs=[HBM_SPEC] * (2 * nk),
        out_shape=[jax.ShapeDtypeStruct(a.shape, a.dtype) for a in list(own) + list(land)],
        input_output_aliases={t: t for t in range(2 * nk)},
        compiler_params=_params(has_side_effects=_SPLIT_EFFECT),
    )(*own, *land, ssem, rsem, after)
    return dict(zip(keys, res[nk:]))


def _chip_partial(piece, dw, got, prev, c_arr, name):
    _, _, rs, cs, _, _, _ = piece
    half = rs // 2
    tm = _tile(half, 256)
    blk = _half_block(piece, tm)

    def body(s_ref, dw_ref, got_ref, *rest):
        rest[-1][...] = (dw_ref[...] + got_ref[...].astype(F32)).astype(BF16)

    spec = pl.BlockSpec((tm, cs), lambda j, i, s: blk(i, j, s[0]))
    in_specs = [spec, spec]
    args = [dw, got]
    aliases = {}
    if prev is not None:
        in_specs.append(pl.BlockSpec(memory_space=pl.ANY))
        args.append(prev)
        aliases = {3: 0}
    return pl.pallas_call(
        body, name=name,
        grid_spec=pltpu.PrefetchScalarGridSpec(
            num_scalar_prefetch=1, grid=(N_CHIPS, half // tm), in_specs=in_specs, out_specs=spec),
        out_shape=jax.ShapeDtypeStruct(dw.shape, BF16), input_output_aliases=aliases,
        compiler_params=_params(("parallel", "parallel")),
    )(c_arr, *args)


def _scatter_start(g, partials):
    group = _REDUCE_GROUPS[g]
    keys = _reduce_keys(group)
    nk, n = len(keys), len(group)

    def body(*refs):
        pt = dict(zip(keys, refs[nk:2 * nk]))
        land = refs[2 * nk:2 * nk + n]
        ssem, rsem, token = refs[2 * nk + n:]
        _, _, c = _pos()

        def send(mine, others):
            for k, (p, layer) in enumerate(group):
                for j in others:
                    _remote(_half_view(pt[(_PIECES[p][1], layer)], _PIECES[p], j, c), land[k].at[mine],
                            ssem.at[k * N_CHIPS + j], rsem.at[k * N_CHIPS + mine], (j // 2, j % 2, c)).start()

        _for_my_chip(send)
        token[...] = jnp.zeros(token.shape, token.dtype)

    sem = pltpu.SemaphoreType.DMA((N_CHIPS * n,))
    res = pl.pallas_call(
        body, name="scatter_start_g%d" % g,
        in_specs=[HBM_SPEC] * nk,
        out_specs=[HBM_SPEC] * (nk + n) + [SEM_SPEC, SEM_SPEC, pl.BlockSpec(memory_space=pltpu.VMEM)],
        out_shape=([jax.ShapeDtypeStruct(partials[k].shape, BF16) for k in keys]
                   + [jax.ShapeDtypeStruct((N_CHIPS, _PIECES[p][2] // 2, _PIECES[p][3]), BF16) for p, _ in group]
                   + [sem, sem, jax.ShapeDtypeStruct((SUBLANES, LANES), F32)]),
        input_output_aliases={t: t for t in range(nk)},
        compiler_params=_params(has_side_effects=_SPLIT_EFFECT),
    )(*[pltpu.with_memory_space_constraint(partials[k], pltpu.HBM) for k in keys])
    return list(res[:nk]), list(res[nk:nk + n]), res[nk + n], res[nk + n + 1], res[nk + n + 2]


def _scatter_wait(g, partials, land, ssem, rsem, after):
    group = _REDUCE_GROUPS[g]
    keys = _reduce_keys(group)
    nk, n = len(keys), len(group)

    def body(*refs):
        ssem_ref, rsem_ref = refs[nk + n], refs[nk + n + 1]
        pt = dict(zip(keys, refs[nk + n + 3:2 * nk + n + 3]))
        land_ref = refs[2 * nk + n + 3:]
        _, _, c = _pos()

        def wait(mine, others):
            for k, (p, layer) in enumerate(group):
                for j in others:
                    cp = _remote(_half_view(pt[(_PIECES[p][1], layer)], _PIECES[p], j, c), land_ref[k].at[j],
                                 ssem_ref.at[k * N_CHIPS + j], rsem_ref.at[k * N_CHIPS + j], (j // 2, j % 2, c))
                    cp.wait_send()
                    cp.wait_recv()

        _for_my_chip(wait)

    res = pl.pallas_call(
        body, name="scatter_wait_g%d" % g,
        in_specs=[HBM_SPEC] * (nk + n) + [SEM_SPEC, SEM_SPEC, pl.BlockSpec(memory_space=pl.ANY)],
        out_specs=[HBM_SPEC] * (nk + n),
        out_shape=[jax.ShapeDtypeStruct(a.shape, a.dtype) for a in list(partials) + list(land)],
        input_output_aliases={t: t for t in range(nk + n)},
        compiler_params=_params(has_side_effects=_SPLIT_EFFECT),
    )(*partials, *land, ssem, rsem, after)
    return list(res[nk:])


def _reduce_half(piece, layer, dw, got, land, prev, idx, name):
    _, _, rs, cs, _, _, _ = piece
    half = rs // 2
    tm = _tile(half, 256)
    blk = _half_block(piece, tm)

    def body(s_ref, dw_ref, got_ref, r1, r2, r3, *rest):
        acc = dw_ref[...] + got_ref[...].astype(F32)
        for r in (r1, r2, r3):
            acc = acc + r[...].astype(F32)
        rest[-1][...] = acc

    def land_map(k):
        return lambda i, s: ((s[1] + k) % N_CHIPS, i, 0)

    own = pl.BlockSpec((tm, cs), lambda i, s: blk(i, s[1], s[0]))
    in_specs = [own, own] + [pl.BlockSpec((None, tm, cs), land_map(k)) for k in (1, 2, 3)]
    args = [dw, got, land, land, land]
    aliases = {}
    if prev is not None:
        in_specs.append(pl.BlockSpec(memory_space=pl.ANY))
        args.append(prev)
        aliases = {6: 0}
    return pl.pallas_call(
        body, name=name,
        grid_spec=pltpu.PrefetchScalarGridSpec(
            num_scalar_prefetch=1, grid=(half // tm,), in_specs=in_specs,
            out_specs=pl.BlockSpec((None, tm, cs), lambda i, s: (layer, s[0] * (half // tm) + i, 0))),
        out_shape=jax.ShapeDtypeStruct((DEPTH, rs, cs), F32), input_output_aliases=aliases,
        compiler_params=_params(("parallel",)),
    )(idx, *args)


_SHARE_ORDER = (2, 1, 5, 3, 4, 0)


def _grad_half(ref, p, layer, cc):
    rows = _PIECES[p][2] // 2
    return ref.at[layer, pl.ds(pl.multiple_of(cc * rows, SUBLANES), rows), :]


def _share_start(reduced):
    def body(*refs):
        buf = refs[N_PIECES:2 * N_PIECES]
        ssems = refs[2 * N_PIECES:3 * N_PIECES]
        rsems = refs[3 * N_PIECES:4 * N_PIECES]
        token = refs[4 * N_PIECES]
        x, y, c = _pos()
        for p in _SHARE_ORDER:
            for layer in range(DEPTH):
                view = _grad_half(buf[p], p, layer, c)
                _remote(view, view, ssems[p].at[layer], rsems[p].at[layer], (x, y, 1 - c)).start()
        token[...] = jnp.zeros(token.shape, token.dtype)

    sems = [pltpu.SemaphoreType.DMA((DEPTH,))] * N_PIECES
    res = pl.pallas_call(
        body, name="share_start",
        in_specs=[HBM_SPEC] * N_PIECES,
        out_specs=[HBM_SPEC] * N_PIECES + [SEM_SPEC] * (2 * N_PIECES) + [pl.BlockSpec(memory_space=pltpu.VMEM)],
        out_shape=[jax.ShapeDtypeStruct((DEPTH, p[2], p[3]), F32) for p in _PIECES] + sems + sems
        + [jax.ShapeDtypeStruct((SUBLANES, LANES), F32)],
        input_output_aliases={t: t for t in range(N_PIECES)},
        compiler_params=_params(has_side_effects=_SPLIT_EFFECT),
    )(*[pltpu.with_memory_space_constraint(a, pltpu.HBM) for a in reduced])
    return (list(res[:N_PIECES]), list(res[N_PIECES:2 * N_PIECES]), list(res[2 * N_PIECES:3 * N_PIECES]),
            res[3 * N_PIECES])


def _share_wait(p, buf, ssem, rsem, after):
    def body(buf_in, ssem_ref, rsem_ref, after_ref, buf_ref):
        x, y, c = _pos()
        for layer in range(DEPTH):
            cp = _remote(_grad_half(buf_ref, p, layer, c), _grad_half(buf_ref, p, layer, 1 - c),
                         ssem_ref.at[layer], rsem_ref.at[layer], (x, y, 1 - c))
            cp.wait_send()
            cp.wait_recv()

    return pl.pallas_call(
        body, name="share_wait_" + _PIECES[p][0],
        in_specs=[HBM_SPEC, SEM_SPEC, SEM_SPEC, pl.BlockSpec(memory_space=pl.ANY)], out_specs=HBM_SPEC,
        out_shape=jax.ShapeDtypeStruct(buf.shape, buf.dtype), input_output_aliases={0: 0},
        compiler_params=_params(has_side_effects=_SPLIT_EFFECT),
    )(buf, ssem, rsem, after)


N_DEV = 8


def _place_slot(v, me_arr, take_block):
    rows = v.shape[0] // N_DEV if take_block else v.shape[0]
    tm = _tile(rows, 512)
    steps = rows // tm

    def body(s_ref, v_ref, out_ref):
        out_ref[...] = v_ref[...]

    return pl.pallas_call(
        body, name="place_small_block" if take_block else "place_small_sum",
        grid_spec=pltpu.PrefetchScalarGridSpec(
            num_scalar_prefetch=1, grid=(steps,),
            in_specs=[pl.BlockSpec((tm, LANES), lambda i, s: (s[0] * steps * take_block + i, 0))],
            out_specs=pl.BlockSpec((None, tm, LANES), lambda i, s: (s[0], i, 0))),
        out_shape=jax.ShapeDtypeStruct((N_DEV, rows, LANES), F32),
        compiler_params=_params(("parallel",)),
    )(me_arr, v)


def _all_peers():
    x, y, c = _pos()
    flip = lambda v, f: 1 - v if f else v
    return (x, y, c), [(flip(x, a), flip(y, b), flip(c, d))
                       for a in (0, 1) for b in (0, 1) for d in (0, 1) if a or b or d]


def _slot_index(dev):
    return 4 * dev[0] + 2 * dev[1] + dev[2]


def _exchange_start(g, src, name):
    rows = g.shape[1]
    n_in = 1 if src is None else 2

    def body(*refs):
        g_ref = refs[n_in]
        src_ref = refs[n_in + 1] if src is not None else None
        ssem, rsem, token = refs[2 * n_in:]
        me, peers = _all_peers()
        for k, dev in enumerate(peers):
            if src is None:
                mine = g_ref.at[_slot_index(me)]
            else:
                mine = src_ref.at[pl.ds(pl.multiple_of(_slot_index(dev) * rows, SUBLANES), rows), :]
            _remote(mine, g_ref.at[_slot_index(me)], ssem.at[k], rsem.at[k], dev).start()
        token[...] = jnp.zeros(token.shape, token.dtype)

    sem = pltpu.SemaphoreType.DMA((N_DEV - 1,))
    args = [g] if src is None else [g, src]
    res = pl.pallas_call(
        body, name=name,
        in_specs=[HBM_SPEC] * n_in,
        out_specs=[HBM_SPEC] * n_in + [SEM_SPEC, SEM_SPEC, pl.BlockSpec(memory_space=pltpu.VMEM)],
        out_shape=[jax.ShapeDtypeStruct(a.shape, a.dtype) for a in args]
        + [sem, sem, jax.ShapeDtypeStruct((SUBLANES, LANES), F32)],
        input_output_aliases={t: t for t in range(n_in)},
        compiler_params=_params(has_side_effects=_SPLIT_EFFECT),
    )(*[pltpu.with_memory_space_constraint(a, pltpu.HBM) for a in args])
    return list(res[:n_in]), res[n_in], res[n_in + 1], res[n_in + 2]


def _exchange_wait(bufs, ssem, rsem, after, name):
    n_in = len(bufs)

    def body(*refs):
        ssem_ref, rsem_ref = refs[n_in], refs[n_in + 1]
        g_ref = refs[n_in + 3]
        me, peers = _all_peers()
        for k, dev in enumerate(peers):
            cp = _remote(g_ref.at[_slot_index(me)], g_ref.at[_slot_index(dev)], ssem_ref.at[k], rsem_ref.at[k], dev)
            cp.wait_send()
            cp.wait_recv()

    res = pl.pallas_call(
        body, name=name,
        in_specs=[HBM_SPEC] * n_in + [SEM_SPEC, SEM_SPEC, pl.BlockSpec(memory_space=pl.ANY)],
        out_specs=[HBM_SPEC] * n_in,
        out_shape=[jax.ShapeDtypeStruct(a.shape, a.dtype) for a in bufs],
        input_output_aliases={t: t for t in range(n_in)},
        compiler_params=_params(has_side_effects=_SPLIT_EFFECT),
    )(*bufs, ssem, rsem, after)
    return res[0]


def _sum_slots(g, name):
    n, rows, _ = g.shape
    tm = _tile(rows, 512)

    def body(g_ref, out_ref):
        acc = g_ref[0]
        for k in range(1, n):
            acc = acc + g_ref[k]
        out_ref[...] = acc

    return pl.pallas_call(
        body, name=name, grid=(rows // tm,),
        in_specs=[pl.BlockSpec((n, tm, LANES), lambda i: (0, i, 0))],
        out_specs=pl.BlockSpec((tm, LANES), lambda i: (i, 0)),
        out_shape=jax.ShapeDtypeStruct((rows, LANES), F32),
        compiler_params=_params(("parallel",)),
    )(g)


_TINY = ("ln_mix_g", "ret_log_gamma", "ssm_a_re", "ssm_a_im", "ssm_log_dt", "ssm_d", "b_glu", "ln_ffn_g", "ln_final_g")
_MID = ("ssm_b_re", "ssm_b_im", "ssm_c_re", "ssm_c_im")
_SMALL = _TINY + _MID
_FLAT_ALIGN = LANES * LANES
_FLAT_ROWS = 1024


def _flat_rows(like, names):
    rows = sum((math.prod(like[n].shape) + (-math.prod(like[n].shape)) % _FLAT_ALIGN) // LANES for n in names)
    return rows + (-rows) % _FLAT_ROWS


def _flatten(d, names):
    parts = []
    for n in names:
        f = d[n].reshape(-1)
        parts.append(jnp.pad(f, (0, (-f.shape[0]) % _FLAT_ALIGN)))
    total = sum(p.shape[0] for p in parts)
    parts.append(jnp.zeros(((-total) % (_FLAT_ROWS * LANES),), F32))
    return jnp.concatenate(parts).reshape(-1, LANES)


def _unflatten(flat, like, names):
    out, row = {}, 0
    for n in names:
        size = math.prod(like[n].shape)
        rows = (size + (-size) % _FLAT_ALIGN) // LANES
        part = lax.optimization_barrier(flat[row:row + rows])
        out[n] = part.reshape(-1)[:size].reshape(like[n].shape)
        row += rows
    return out


_BIG = ("w_in", "w_glu", "w_out", "w_ffn_gate", "w_ffn_up", "w_ffn_down")
_WEIGHTS = ("ln_mix_g", "w_in", "ret_log_gamma", "ssm_a_re", "ssm_a_im", "ssm_log_dt", "ssm_b_re", "ssm_b_im",
            "ssm_c_re", "ssm_c_im", "ssm_d", "w_glu", "b_glu", "w_out", "ln_ffn_g", "w_ffn_gate", "w_ffn_up",
            "w_ffn_down", "ln_final_g")


def _rope_tables(seq):
    half = QK_DIM // 2
    inv = 1.0 / (ROPE_BASE ** (jnp.arange(half, dtype=F32) / half))
    ang = jnp.arange(seq, dtype=F32)[:, None] * inv[None, :]
    return jnp.cos(ang), jnp.sin(ang)


def _step(w, m, v, x, target):
    seq = x.shape[0]
    seg_len = float(seq // SEGMENTS)
    c_idx = lax.axis_index("c").astype(jnp.int32)
    chip_idx = (2 * lax.axis_index("x") + lax.axis_index("y")).astype(jnp.int32)
    c_arr = jnp.stack([c_idx])
    idx_arr = jnp.stack([c_idx, chip_idx])

    chip_arr = jnp.stack([chip_idx])
    placed = {}

    def cast(pieces, layer):
        for p in pieces:
            key = (_PIECES[p][1], layer)
            placed[key] = _cast_place(_PIECES[p], w[_PIECES[p][0]], layer, placed.get(key), chip_arr,
                                      "cast_%s_l%d" % (_PIECES[p][0], layer))

    for layer, pieces in _GATHER_GROUPS:
        cast(pieces, layer)
    flying, ssems, rsems, token = _gather_start(list(range(len(_GATHER_GROUPS))), placed)
    wf = {b[0]: [None] * DEPTH for b in _BUFFERS}

    handing = {}

    def arrive(g, after):
        ks = _group_keys(g)
        landed = _gather_wait(g, [flying[k] for k in ks], ssems[g], rsems[g], after)
        for k, a in zip(ks, _gather_forward(g, landed)):
            wf[k[0]][k[1]] = a

    def hand_over(g, after):
        ks = _group_keys(g)
        landed = _gather_wait(g, [flying[k] for k in ks], ssems[g], rsems[g], after)
        bufs, fs, fr, tok = _forward_start(g, landed)
        handing[g] = (bufs, fs, fr)
        return tok[0:1, 0:1]

    def complete(g, after):
        for k, a in zip(_group_keys(g), _forward_wait(g, *handing[g], after)):
            wf[k[0]][k[1]] = a

    cos, sin = _rope_tables(seq)

    started = token[0, 0]
    s5_ops, s5_vjps = [], []
    for i in range(DEPTH):
        s5_raw = (w["ssm_a_re"][i] + started, w["ssm_a_im"][i], w["ssm_log_dt"][i], w["ssm_b_re"][i], w["ssm_b_im"][i])
        disc, disc_vjp = jax.vjp(functools.partial(_s5_discretize, seg_len=seg_len), *s5_raw)
        bblk, cblk, lam = _s5_pack(*disc, w["ssm_c_re"][i] + started, w["ssm_c_im"][i] + started)
        s5_ops.append((bblk.astype(BF16), cblk.astype(BF16), lam))
        s5_vjps.append(disc_vjp)
    tiny_flat = [_flatten({**d, "ln_final_g": d["ln_final_g"] + started}, _TINY) for d in (w, m, v)]
    corner = lambda a: a[(0,) * (a.ndim - 2)][0:1, 0:1].astype(F32)
    prepared = sum(corner(a) for ops in s5_ops for a in ops) + sum(corner(a) for a in tiny_flat) + corner(cos) + corner(sin)

    saved = []
    xc = x + token[0, 0]
    for i in range(DEPTH):
        t = "_l%d" % i
        s = {"x_in": xc}
        if i == 0:
            s["h"] = _rms_fwd(xc, w["ln_mix_g"][i:i + 1], "rms_mix" + t)
            arrive(0, prepared + corner(s["h"]))
        else:
            s["h"] = _rms_fwd(xc, w["ln_mix_g"][i:i + 1] + next_in, "rms_mix" + t)
        s["proj"] = _matmul(s["h"], wf["in"][i], "nn", [F32], name="mm_in" + t)[0]
        s["qr"], s["kr"] = _rot_fwd(s["proj"], cos, sin, "rot" + t)
        s["lg"] = jnp.broadcast_to(w["ret_log_gamma"][i].T[:, :, None], (HEADS, 2, LANES))
        s["y"] = _ret_fwd(s["qr"], s["kr"], s["proj"], s["lg"], "ret" + t)
        s["s5"], s["disc_vjp"] = s5_ops[i], s5_vjps[i]
        s["s5y"] = _s5_fwd(s["proj"], *s["s5"], "s5" + t)
        first = GROUPS_PER_LAYER * i
        d_skip = w["ssm_d"][i:i + 1] + hand_over(first + 1, s["s5y"])
        s["ret"], s["ysg"], s["ysgb"] = _post1_fwd(s["y"], s["proj"], s["s5y"], d_skip, "post" + t)
        complete(first + 1, s["ysgb"])
        s["z"] = _matmul(s["ysgb"], wf["glu"][i], "nn", [F32], name="mm_glu" + t)[0]
        b_glu = w["b_glu"][i:i + 1] + hand_over(first + 2, s["z"])
        s["merged"] = _merge_fwd(s["z"], s["ysg"], s["proj"], s["ret"], b_glu, "merge" + t)
        s["x1"] = _matmul(s["merged"], wf["out"][i], "nn", [F32], add=xc, name="mm_out" + t)[0]
        s["h2"] = _rms_fwd(s["x1"], w["ln_ffn_g"][i:i + 1], "rms_ffn" + t)
        complete(first + 2, s["h2"])
        s["ab"] = _matmul(s["h2"], wf["gu"][i], "nn", [F32], name="mm_gu" + t)[0]
        hand_over(first + 3, s["ab"])
        if i + 1 < DEPTH:
            next_in = hand_over(first + GROUPS_PER_LAYER, s["ab"])
        s["f"] = _glu_fwd(s["ab"], "glu" + t)
        complete(first + 3, s["f"])
        xc = _matmul(s["f"], wf["down"][i], "nn", [F32], add=s["x1"], name="mm_down" + t)[0]
        if i + 1 < DEPTH:
            complete(first + GROUPS_PER_LAYER, xc)
        saved.append(s)

    dx, dxb, loss_row, dg_final = _loss_stage(xc, target, w["ln_final_g"][None, :], "loss")
    loss = lax.psum(loss_row[0, 0], ("x", "y", "c"))

    g_small = {"ln_final_g": dg_final[0]}
    per_layer = {n: [None] * DEPTH for n in _SMALL if n != "ln_final_g"}
    dws, got, swaps, flights = {}, {}, {}, []

    def dw_mm(a, b, buf, i, name):
        dws[(buf, i)] = _matmul(a, b, "tn", [F32, BF16], name=name)

    def depart(g):
        keys = _reduce_keys(_REDUCE_GROUPS[g])
        own, land, ssem, rsem, tok = _swap_start(g, {k: dws[k][1] for k in keys})
        swaps[g] = (own, land, ssem, rsem)
        return tok[0:1, 0:1]

    def proceed(g, after):
        group = _REDUCE_GROUPS[g]
        got.update(_swap_wait(g, *swaps[g], after))
        partials = {}
        for p, layer in group:
            key = (_PIECES[p][1], layer)
            partials[key] = _chip_partial(_PIECES[p], dws[key][0], got[key], partials.get(key), c_arr,
                                          "chip_partial_%s_l%d" % (_PIECES[p][0], layer))
        pt, land, ssem, rsem, tok = _scatter_start(g, partials)
        flights.append((g, pt, land, ssem, rsem))
        return tok[0:1, 0:1]

    for i in reversed(range(DEPTH)):
        t = "_l%d" % i
        s = saved[i]
        g_ffn, g_mix, d_skip = w["ln_ffn_g"][i:i + 1], w["ln_mix_g"][i:i + 1], w["ssm_d"][i:i + 1]
        dw_mm(s["f"], dxb, "down", i, "dw_down" + t)
        df = _matmul(dxb, wf["down"][i], "nt", [F32], name="dx_down" + t)[0]
        if i == 0:
            g_ffn = g_ffn + proceed(0, df)
        dab = _glu_bwd(s["ab"], df, "glu_bwd" + t)
        dw_mm(s["h2"], dab, "gu", i, "dw_gu" + t)
        if i == 0:
            g_ffn = g_ffn + depart(1)
        dh2 = _matmul(dab, wf["gu"][i], "nt", [F32], name="dx_gu" + t)[0]
        if i == 0:
            g_ffn = g_ffn + proceed(1, dh2)
        dx1, dx1b, dg = _rms_bwd(s["x1"], dh2, dx, g_ffn, "rms_ffn_bwd" + t)
        per_layer["ln_ffn_g"][i] = dg[0]

        dw_mm(s["merged"], dx1b, "out", i, "dw_out" + t)
        dmerged = _matmul(dx1b, wf["out"][i], "nt", [F32], name="dx_out" + t)[0]
        dz, dys_part, dgs, db = _merge_bwd(s["z"], s["ysg"], s["proj"], s["ret"], dmerged, w["b_glu"][i:i + 1],
                                           "merge_bwd" + t)
        per_layer["b_glu"][i] = db[0]
        dw_mm(s["ysgb"], dz, "glu", i, "dw_glu" + t)
        if i == 0:
            d_skip = d_skip + depart(2)
        dys = _matmul(dz, wf["glu"][i], "nt", [F32], add=dys_part, name="dx_glu" + t)[0]
        if i == 0:
            d_skip = d_skip + proceed(2, dys)
        dy, dgg, dgr, ds5, du_part, dd = _post1_bwd(s["y"], s["proj"], s["s5y"], dmerged, dys,
                                                    d_skip, "post_bwd" + t)
        per_layer["ssm_d"][i] = dd[0]
        du, dbblk, dcblk, dlam = _s5_bwd(s["proj"], ds5, du_part, *s["s5"], "s5_bwd" + t)
        dlr, dli, dbr, dbi, dcr, dci = _s5_unpack(dbblk, dcblk, dlam)
        zeros = jnp.zeros_like(dlr)
        da_re, da_im, dlog_dt, db_re, db_im = s["disc_vjp"]((dlr, dli, zeros, zeros, dbr, dbi))
        for n, val in (("ssm_a_re", da_re), ("ssm_a_im", da_im), ("ssm_log_dt", dlog_dt), ("ssm_b_re", db_re),
                       ("ssm_b_im", db_im), ("ssm_c_re", dcr), ("ssm_c_im", dci)):
            per_layer[n][i] = val
        dqr, dkr, dv, dlg = _ret_bwd(s["qr"], s["kr"], s["proj"], dy, s["lg"], "ret_bwd" + t)
        per_layer["ret_log_gamma"][i] = dlg[:, :, 0].T
        dqkv = _rot_bwd(dqr, dkr, dv, cos, sin, "rot_bwd" + t)
        dproj = jnp.concatenate([dqkv, dgg, du, dgr, dgs], axis=1)
        dw_mm(s["h"], dproj, "in", i, "dw_in" + t)
        if i == 0:
            g_mix = g_mix + depart(3)
        dh = _matmul(dproj, wf["in"][i], "nt", [F32], name="dx_in" + t)[0]
        if i == 0:
            g_mix = g_mix + proceed(3, dh)
        dx, dxb, dg = _rms_bwd(s["x_in"], dh, dx1, g_mix, "rms_mix_bwd" + t)
        per_layer["ln_mix_g"][i] = dg[0]
        if i == DEPTH - 1:
            dxb = dxb + depart(0).astype(BF16)

    for n in per_layer:
        g_small[n] = jnp.stack(per_layer[n])
    me_arr = jnp.stack([2 * chip_idx + c_idx])
    g_mine = _flatten(g_small, _SMALL)
    rs_bufs, rs_ssem, rs_rsem, small_token = _exchange_start(_place_slot(g_mine, me_arr, True), g_mine,
                                                             "small_scatter_start")

    reduced = [None] * N_PIECES
    before = small_token
    for g, pt, land, ssem, rsem in flights:
        landed = _scatter_wait(g, pt, land, ssem, rsem, before)
        for (p, layer), buf in zip(_REDUCE_GROUPS[g], landed):
            key = (_PIECES[p][1], layer)
            reduced[p] = _reduce_half(_PIECES[p], layer, dws[key][0], got[key], buf, reduced[p], idx_arr,
                                      "reduce_%s_l%d" % (_PIECES[p][0], layer))
            before = reduced[p]
    shared, sh_ssem, sh_rsem, sh_token = _share_start(reduced)

    grads, delta, new_m, new_v = {}, {}, {}, {}
    previous = order = sh_token
    for k, p in enumerate(_SHARE_ORDER):
        if k == len(_SHARE_ORDER) // 2:
            landed = _exchange_wait(rs_bufs, rs_ssem, rs_rsem, previous, "small_scatter_wait")
            ag_bufs, ag_ssem, ag_rsem, ag_token = _exchange_start(
                _place_slot(_sum_slots(landed, "sum_small"), me_arr, False), None, "small_gather_start")
            previous = order = ag_token
        n = _PIECES[p][0]
        d, r, cc = w[n].shape
        two_d = lambda a: a.reshape(d * r, cc)
        g = _share_wait(p, shared[p], sh_ssem[p], sh_rsem[p], previous)
        dl, mn, vn = _adamw(two_d(w[n]), two_d(g), two_d(m[n]), two_d(v[n]), "adamw_" + n, after=order)
        grads[n], delta[n], new_m[n], new_v[n] = g, dl.reshape(d, r, cc), mn.reshape(d, r, cc), vn.reshape(d, r, cc)
        previous = dl

    all_done = sum(corner(delta[n]) for n in _BIG)
    gathered = _exchange_wait(ag_bufs, ag_ssem, ag_rsem, all_done, "small_gather_wait")
    g_flat = gathered.reshape(-1, LANES)
    grads.update(_unflatten(g_flat, w, _SMALL))
    tiny_rows = _flat_rows(w, _TINY)
    dl, mn, vn = _adamw(tiny_flat[0], g_flat[:tiny_rows], tiny_flat[1], tiny_flat[2], "adamw_tiny")
    for dst, flat in ((delta, dl), (new_m, mn), (new_v, vn)):
        dst.update(_unflatten(flat, w, _TINY))
    for n in _MID:
        delta[n], new_m[n], new_v[n] = _adamw_nd(w[n], grads[n], m[n], v[n], "adamw_" + n)
    return loss, dx, grads, delta, new_m, new_v


def kernel(x, ln_mix_g, w_in, ret_log_gamma, ssm_a_re, ssm_a_im, ssm_log_dt, ssm_b_re, ssm_b_im, ssm_c_re, ssm_c_im, ssm_d, w_glu, b_glu, w_out, ln_ffn_g, w_ffn_gate, w_ffn_up, w_ffn_down, ln_final_g, loss_target, m_ln_mix_g, m_w_in, m_ret_log_gamma, m_ssm_a_re, m_ssm_a_im, m_ssm_log_dt, m_ssm_b_re, m_ssm_b_im, m_ssm_c_re, m_ssm_c_im, m_ssm_d, m_w_glu, m_b_glu, m_w_out, m_ln_ffn_g, m_w_ffn_gate, m_w_ffn_up, m_w_ffn_down, m_ln_final_g, v_ln_mix_g, v_w_in, v_ret_log_gamma, v_ssm_a_re, v_ssm_a_im, v_ssm_log_dt, v_ssm_b_re, v_ssm_b_im, v_ssm_c_re, v_ssm_c_im, v_ssm_d, v_w_glu, v_b_glu, v_w_out, v_ln_ffn_g, v_w_ffn_gate, v_w_ffn_up, v_w_ffn_down, v_ln_final_g):
    given = dict(locals())
    w = {n: given[n] for n in _WEIGHTS}
    m = {n: given["m_" + n] for n in _WEIGHTS}
    v = {n: given["v_" + n] for n in _WEIGHTS}
    loss, dx, grads, delta, new_m, new_v = _step(w, m, v, x[0], loss_target[0])
    return (loss, dx[None], *[grads[n] for n in _WEIGHTS], *[delta[n] for n in _WEIGHTS],
            *[new_m[n] for n in _WEIGHTS], *[new_v[n] for n in _WEIGHTS])
```
